```python
import math
import jax, jax.numpy as jnp
from jax import lax
import numpy as np

D_MODEL = 1024
BATCH = 8
SEQ = 2048
DEPTH = 4

D_HEAD = 64
A_WIDTH = D_MODEL // 4
B_HEADS = (D_MODEL // 2) // D_HEAD
B_WIDTH = B_HEADS * D_HEAD
C_WIDTH = D_MODEL // 4
D_MIX = A_WIDTH + B_WIDTH + C_WIDTH
SPLIT_SIZES = (A_WIDTH, A_WIDTH, A_WIDTH,
               B_WIDTH, B_WIDTH, B_WIDTH,
               C_WIDTH, C_WIDTH)
IN_COLS = sum(SPLIT_SIZES)
DILATED_BRANCHES = ((128, 1), (512, 4), (2048, 16))
BLK = 128
NUM_BUCKETS = 32
MAX_DISTANCE = 2048
SHORT_CONV = 3
CONFORMER_CONV = 31
FFN_CONV = 3
D_FF = ((8 * D_MODEL // 3 + 127) // 128) * 128
EPS = 1e-6
NEG = -1e30

kernel_name = 'hybrid_shortconv_dilatedattn_conformer_trunk'


def rmsnorm(x, g):
    xf = x.astype(jnp.float32)
    y = xf * lax.rsqrt(jnp.mean(xf * xf, axis=-1, keepdims=True) + EPS)
    return (y * g.astype(jnp.float32)).astype(x.dtype)


def layernorm(x, g, b):
    xf = x.astype(jnp.float32)
    mu = jnp.mean(xf, axis=-1, keepdims=True)
    var = jnp.mean(jnp.square(xf - mu), axis=-1, keepdims=True)
    y = (xf - mu) * lax.rsqrt(var + EPS)
    return (y * g.astype(jnp.float32) + b.astype(jnp.float32)).astype(x.dtype)


def causal_dwconv(x, w):
    k_width, ch = w.shape
    return lax.conv_general_dilated(
        x, w[:, None, :].astype(x.dtype), window_strides=(1,),
        padding=[(k_width - 1, 0)], dimension_numbers=('NWC', 'WIO', 'NWC'),
        feature_group_count=ch)


def t5_bucket(dist):
    max_exact = NUM_BUCKETS // 2
    d_f = jnp.maximum(dist, 1).astype(jnp.float32)
    large = max_exact + (jnp.log(d_f / max_exact) / math.log(MAX_DISTANCE / max_exact)
                         * (NUM_BUCKETS - max_exact)).astype(jnp.int32)
    large = jnp.minimum(large, NUM_BUCKETS - 1)
    return jnp.where(dist < max_exact, dist, large)


def dilated_branch(q, k, v, rel_bias, window, dilation):
    bsz, seq, heads, hd = q.shape
    n_keys = window // dilation
    sub_len = seq // dilation
    n_blk = -(-sub_len // BLK)
    pad = n_blk * BLK - sub_len

    def to_blocks(t):
        t = t.reshape(bsz, sub_len, dilation, heads, hd).transpose(0, 2, 3, 1, 4)
        t = jnp.pad(t, ((0, 0), (0, 0), (0, 0), (0, pad), (0, 0)))
        return t.reshape(bsz, dilation, heads, n_blk, BLK, hd)

    qb, kb, vb = to_blocks(q), to_blocks(k), to_blocks(v)

    def with_prev(t):
        prev = jnp.concatenate([jnp.zeros_like(t[:, :, :, :1]), t[:, :, :, :-1]], axis=3)
        return jnp.concatenate([prev, t], axis=4)

    kk, vv = with_prev(kb), with_prev(vb)
    s = jnp.einsum('bdhnqc,bdhnkc->bdhnqk', qb, kk).astype(jnp.float32) * (hd ** -0.5)
    rel = jnp.arange(BLK)[:, None] - jnp.arange(2 * BLK)[None, :] + BLK
    k_idx = jnp.arange(n_blk)[:, None] * BLK + jnp.arange(2 * BLK)[None, :] - BLK
    valid = ((rel >= 0) & (rel <= n_keys))[None] & (k_idx >= 0)[:, None, :]
    bias = rel_bias[t5_bucket(jnp.maximum(rel, 0) * dilation)]
    bias = bias.transpose(2, 0, 1).astype(jnp.float32)[:, None]
    s = jnp.where(valid, s + bias, NEG)
    m = jnp.max(s, axis=-1, keepdims=True)
    p = jnp.exp(s - m)
    den = jnp.sum(p, axis=-1, keepdims=True)
    o = jnp.einsum('bdhnqk,bdhnkc->bdhnqc', p.astype(v.dtype), vv).astype(jnp.float32) / den
    lse = (m + jnp.log(den))[..., 0]
    o = o.reshape(bsz, dilation, heads, n_blk * BLK, hd)[:, :, :, :sub_len]
    o = o.transpose(0, 3, 1, 2, 4).reshape(bsz, seq, heads, hd)
    lse = lse.reshape(bsz, dilation, heads, n_blk * BLK)[..., :sub_len]
    lse = lse.transpose(0, 3, 1, 2).reshape(bsz, seq, heads)
    return o, lse


def dilated_mixture(q, k, v, rel_bias):
    outs, lses = [], []
    for window, dilation in DILATED_BRANCHES:
        o, l = dilated_branch(q, k, v, rel_bias, window, dilation)
        outs.append(o)
        lses.append(l)
    wts = jax.nn.softmax(jnp.stack(lses, axis=0), axis=0)
    return jnp.sum(wts[..., None] * jnp.stack(outs, axis=0), axis=0)


def _fwd_setup_inputs(seed: int = 0) -> dict:
    key = jax.random.key(seed)
    ks = jax.random.split(key, 20)
    f32 = jnp.float32

    def nrm(k, shape, scale):
        return jax.random.normal(k, shape, f32) * scale

    return {
        'x': nrm(ks[0], (BATCH, SEQ, D_MODEL), 1.0),
        'norm_mix_g': 1.0 + nrm(ks[1], (DEPTH, D_MODEL), 0.02),
        'w_in': nrm(ks[2], (DEPTH, D_MODEL, IN_COLS), D_MODEL ** -0.5),
        'conv_a_w': nrm(ks[3], (DEPTH, SHORT_CONV, A_WIDTH), SHORT_CONV ** -0.5),
        'conv_c_w': nrm(ks[4], (DEPTH, CONFORMER_CONV, C_WIDTH), CONFORMER_CONV ** -0.5),
        'conv_c_b': nrm(ks[5], (DEPTH, C_WIDTH), 0.02),
        'ln_c_g': 1.0 + nrm(ks[6], (DEPTH, C_WIDTH), 0.02),
        'ln_c_b': nrm(ks[7], (DEPTH, C_WIDTH), 0.02),
        'out_norm_g': 1.0 + nrm(ks[8], (DEPTH, D_MIX), 0.02),
        'w_out': nrm(ks[9], (DEPTH, D_MIX, D_MODEL), D_MIX ** -0.5),
        'norm_ffn_g': 1.0 + nrm(ks[10], (DEPTH, D_MODEL), 0.02),
        'w_up': nrm(ks[11], (DEPTH, D_MODEL, 2 * D_FF), D_MODEL ** -0.5),
        'conv_f_w': nrm(ks[12], (DEPTH, FFN_CONV, 2 * D_FF), FFN_CONV ** -0.5),
        'w_down': nrm(ks[13], (DEPTH, D_FF, D_MODEL), D_FF ** -0.5),
        'rel_bias': nrm(ks[14], (NUM_BUCKETS, B_HEADS), 0.5),
        'final_g': 1.0 + nrm(ks[15], (D_MODEL,), 0.02),
    }


def _fwd_reference(x, norm_mix_g, w_in, conv_a_w, conv_c_w, conv_c_b, ln_c_g, ln_c_b,
              out_norm_g, w_out, norm_ffn_g, w_up, conv_f_w, w_down, rel_bias, final_g):
    bsz, seq, _ = x.shape
    split_idx = list(np.cumsum(SPLIT_SIZES)[:-1])
    g_idx = [A_WIDTH, A_WIDTH + B_WIDTH]
    for l in range(DEPTH):
        h = rmsnorm(x, norm_mix_g[l])
        z = h @ w_in[l]
        a_h, a_b, a_c, q, k, v, c_val, c_gate = jnp.split(z, split_idx, axis=-1)
        y_a = a_b * causal_dwconv(a_c * a_h, conv_a_w[l])
        hs = (bsz, seq, B_HEADS, D_HEAD)
        y_b = dilated_mixture(q.reshape(hs), k.reshape(hs), v.reshape(hs), rel_bias)
        y_b = y_b.reshape(bsz, seq, B_WIDTH).astype(x.dtype)
        u = c_val * jax.nn.sigmoid(c_gate)
        u = causal_dwconv(u, conv_c_w[l]) + conv_c_b[l].astype(u.dtype)
        y_c = jax.nn.silu(layernorm(u, ln_c_g[l], ln_c_b[l]))
        g_a, g_b, g_c = jnp.split(out_norm_g[l], g_idx)
        y = jnp.concatenate([rmsnorm(y_a, g_a), rmsnorm(y_b, g_b), rmsnorm(y_c, g_c)], axis=-1)
        x = x + y @ w_out[l]
        h = rmsnorm(x, norm_ffn_g[l])
        up = causal_dwconv(h @ w_up[l], conv_f_w[l])
        gate, val = jnp.split(up, 2, axis=-1)
        x = x + (jax.nn.silu(gate) * val) @ w_down[l]
    return rmsnorm(x, final_g)


import jax as _jax
import jax.numpy as _jnp

TWIN_FORMAT = 'train_step'
FWD_PARAMS = ['x', 'norm_mix_g', 'w_in', 'conv_a_w', 'conv_c_w', 'conv_c_b', 'ln_c_g', 'ln_c_b', 'out_norm_g', 'w_out', 'norm_ffn_g', 'w_up', 'conv_f_w', 'w_down', 'rel_bias', 'final_g']
TWIN_WEIGHTS = ['norm_mix_g', 'w_in', 'conv_a_w', 'conv_c_w', 'conv_c_b', 'ln_c_g', 'ln_c_b', 'out_norm_g', 'w_out', 'norm_ffn_g', 'w_up', 'conv_f_w', 'w_down', 'rel_bias', 'final_g']
TWIN_DIFF_INPUT = 'x'
TWIN_INPUTS = ['x', 'norm_mix_g', 'w_in', 'conv_a_w', 'conv_c_w', 'conv_c_b', 'ln_c_g', 'ln_c_b', 'out_norm_g', 'w_out', 'norm_ffn_g', 'w_up', 'conv_f_w', 'w_down', 'rel_bias', 'final_g', 'loss_target', 'm_norm_mix_g', 'm_w_in', 'm_conv_a_w', 'm_conv_c_w', 'm_conv_c_b', 'm_ln_c_g', 'm_ln_c_b', 'm_out_norm_g', 'm_w_out', 'm_norm_ffn_g', 'm_w_up', 'm_conv_f_w', 'm_w_down', 'm_rel_bias', 'm_final_g', 'v_norm_mix_g', 'v_w_in', 'v_conv_a_w', 'v_conv_c_w', 'v_conv_c_b', 'v_ln_c_g', 'v_ln_c_b', 'v_out_norm_g', 'v_w_out', 'v_norm_ffn_g', 'v_w_up', 'v_conv_f_w', 'v_w_down', 'v_rel_bias', 'v_final_g']
TWIN_OUTPUTS = ['loss', 'grad_x', 'grad_norm_mix_g', 'grad_w_in', 'grad_conv_a_w', 'grad_conv_c_w', 'grad_conv_c_b', 'grad_ln_c_g', 'grad_ln_c_b', 'grad_out_norm_g', 'grad_w_out', 'grad_norm_ffn_g', 'grad_w_up', 'grad_conv_f_w', 'grad_w_down', 'grad_rel_bias', 'grad_final_g', 'delta_norm_mix_g', 'delta_w_in', 'delta_conv_a_w', 'delta_conv_c_w', 'delta_conv_c_b', 'delta_ln_c_g', 'delta_ln_c_b', 'delta_out_norm_g', 'delta_w_out', 'delta_norm_ffn_g', 'delta_w_up', 'delta_conv_f_w', 'delta_w_down', 'delta_rel_bias', 'delta_final_g', 'new_m_norm_mix_g', 'new_m_w_in', 'new_m_conv_a_w', 'new_m_conv_c_w', 'new_m_conv_c_b', 'new_m_ln_c_g', 'new_m_ln_c_b', 'new_m_out_norm_g', 'new_m_w_out', 'new_m_norm_ffn_g', 'new_m_w_up', 'new_m_conv_f_w', 'new_m_w_down', 'new_m_rel_bias', 'new_m_final_g', 'new_v_norm_mix_g', 'new_v_w_in', 'new_v_conv_a_w', 'new_v_conv_c_w', 'new_v_conv_c_b', 'new_v_ln_c_g', 'new_v_ln_c_b', 'new_v_out_norm_g', 'new_v_w_out', 'new_v_norm_ffn_g', 'new_v_w_up', 'new_v_conv_f_w', 'new_v_w_down', 'new_v_rel_bias', 'new_v_final_g']
TWIN_LEAF_KINDS = {'loss': 'loss', 'grad_x': 'grad_x', 'grad_norm_mix_g': 'grad_w', 'grad_w_in': 'grad_w', 'grad_conv_a_w': 'grad_w', 'grad_conv_c_w': 'grad_w', 'grad_conv_c_b': 'grad_w', 'grad_ln_c_g': 'grad_w', 'grad_ln_c_b': 'grad_w', 'grad_out_norm_g': 'grad_w', 'grad_w_out': 'grad_w', 'grad_norm_ffn_g': 'grad_w', 'grad_w_up': 'grad_w', 'grad_conv_f_w': 'grad_w', 'grad_w_down': 'grad_w', 'grad_rel_bias': 'grad_w', 'grad_final_g': 'grad_w', 'delta_norm_mix_g': 'delta_w', 'delta_w_in': 'delta_w', 'delta_conv_a_w': 'delta_w', 'delta_conv_c_w': 'delta_w', 'delta_conv_c_b': 'delta_w', 'delta_ln_c_g': 'delta_w', 'delta_ln_c_b': 'delta_w', 'delta_out_norm_g': 'delta_w', 'delta_w_out': 'delta_w', 'delta_norm_ffn_g': 'delta_w', 'delta_w_up': 'delta_w', 'delta_conv_f_w': 'delta_w', 'delta_w_down': 'delta_w', 'delta_rel_bias': 'delta_w', 'delta_final_g': 'delta_w', 'new_m_norm_mix_g': 'new_m', 'new_m_w_in': 'new_m', 'new_m_conv_a_w': 'new_m', 'new_m_conv_c_w': 'new_m', 'new_m_conv_c_b': 'new_m', 'new_m_ln_c_g': 'new_m', 'new_m_ln_c_b': 'new_m', 'new_m_out_norm_g': 'new_m', 'new_m_w_out': 'new_m', 'new_m_norm_ffn_g': 'new_m', 'new_m_w_up': 'new_m', 'new_m_conv_f_w': 'new_m', 'new_m_w_down': 'new_m', 'new_m_rel_bias': 'new_m', 'new_m_final_g': 'new_m', 'new_v_norm_mix_g': 'new_v', 'new_v_w_in': 'new_v', 'new_v_conv_a_w': 'new_v', 'new_v_conv_c_w': 'new_v', 'new_v_conv_c_b': 'new_v', 'new_v_ln_c_g': 'new_v', 'new_v_ln_c_b': 'new_v', 'new_v_out_norm_g': 'new_v', 'new_v_w_out': 'new_v', 'new_v_norm_ffn_g': 'new_v', 'new_v_w_up': 'new_v', 'new_v_conv_f_w': 'new_v', 'new_v_w_down': 'new_v', 'new_v_rel_bias': 'new_v', 'new_v_final_g': 'new_v'}


def _forward(args):
    return _fwd_reference(*[args[k] for k in FWD_PARAMS])


def _output_shape():
    out = _jax.eval_shape(lambda: _forward(_fwd_setup_inputs(0)))
    return out.shape, out.dtype

N_MICROBATCH = 1
ADAM_LR = 0.001
ADAM_B1 = 0.9
ADAM_B2 = 0.999
ADAM_EPS = 1e-08
ADAM_WD = 0.01
ADAM_STEP = 10
PER_EXAMPLE_BATCH_AXIS = {'x': 0, 'loss_target': 0}
SHARED_INPUTS = []
_WEIGHT_DTYPES = {'norm_mix_g': _jnp.float32, 'w_in': _jnp.float32, 'conv_a_w': _jnp.float32, 'conv_c_w': _jnp.float32, 'conv_c_b': _jnp.float32, 'ln_c_g': _jnp.float32, 'ln_c_b': _jnp.float32, 'out_norm_g': _jnp.float32, 'w_out': _jnp.float32, 'norm_ffn_g': _jnp.float32, 'w_up': _jnp.float32, 'conv_f_w': _jnp.float32, 'w_down': _jnp.float32, 'rel_bias': _jnp.float32, 'final_g': _jnp.float32}
MOMENT_SCALE = {'norm_mix_g': 1.320087e-01, 'w_in': 7.886722e-02, 'conv_a_w': 8.920346e-02, 'conv_c_w': 8.874795e-02, 'conv_c_b': 3.349745e-01, 'ln_c_g': 1.493934e-01, 'ln_c_b': 2.089961e-01, 'out_norm_g': 1.008618e-01, 'w_out': 1.021528e-01, 'norm_ffn_g': 6.825770e-02, 'w_up': 2.840437e-02, 'conv_f_w': 2.827600e-02, 'w_down': 4.666450e-02, 'rel_bias': 1.666527e-01, 'final_g': 1.624111e+01}


def _to_microbatches(a, axis):
    t = _jnp.moveaxis(a, axis, 0)
    t = t.reshape((N_MICROBATCH, t.shape[0] // N_MICROBATCH) + t.shape[1:])
    return _jnp.moveaxis(t, 1, axis + 1)


def setup_inputs(seed: int = 0) -> dict:
    inp = _fwd_setup_inputs(seed)
    key = _jax.random.fold_in(_jax.random.key(seed), 7919)
    shape, _ = _output_shape()
    out = dict(inp)
    out["loss_target"] = _jax.random.normal(_jax.random.fold_in(key, 0), shape, _jnp.float32)
    for i, name in enumerate(TWIN_WEIGHTS):
        w = inp[name].astype(_jnp.float32)
        if MOMENT_SCALE is None:
            s = _jnp.sqrt(_jnp.mean(_jnp.square(w)) + 1e-30)
        else:
            s = MOMENT_SCALE[name]
        km, kv = _jax.random.split(_jax.random.fold_in(key, i + 1))
        out[name] = w
        out["m_" + name] = s * _jax.random.normal(km, w.shape, _jnp.float32)
        out["v_" + name] = (s * s) * _jax.random.uniform(kv, w.shape, _jnp.float32, 0.5, 1.5)
    if N_MICROBATCH > 1:
        for name, axis in PER_EXAMPLE_BATCH_AXIS.items():
            out[name] = _to_microbatches(out[name], axis)
    return {'x': out['x'], 'norm_mix_g': out['norm_mix_g'], 'w_in': out['w_in'], 'conv_a_w': out['conv_a_w'], 'conv_c_w': out['conv_c_w'], 'conv_c_b': out['conv_c_b'], 'ln_c_g': out['ln_c_g'], 'ln_c_b': out['ln_c_b'], 'out_norm_g': out['out_norm_g'], 'w_out': out['w_out'], 'norm_ffn_g': out['norm_ffn_g'], 'w_up': out['w_up'], 'conv_f_w': out['conv_f_w'], 'w_down': out['w_down'], 'rel_bias': out['rel_bias'], 'final_g': out['final_g'], 'loss_target': out['loss_target'], 'm_norm_mix_g': out['m_norm_mix_g'], 'm_w_in': out['m_w_in'], 'm_conv_a_w': out['m_conv_a_w'], 'm_conv_c_w': out['m_conv_c_w'], 'm_conv_c_b': out['m_conv_c_b'], 'm_ln_c_g': out['m_ln_c_g'], 'm_ln_c_b': out['m_ln_c_b'], 'm_out_norm_g': out['m_out_norm_g'], 'm_w_out': out['m_w_out'], 'm_norm_ffn_g': out['m_norm_ffn_g'], 'm_w_up': out['m_w_up'], 'm_conv_f_w': out['m_conv_f_w'], 'm_w_down': out['m_w_down'], 'm_rel_bias': out['m_rel_bias'], 'm_final_g': out['m_final_g'], 'v_norm_mix_g': out['v_norm_mix_g'], 'v_w_in': out['v_w_in'], 'v_conv_a_w': out['v_conv_a_w'], 'v_conv_c_w': out['v_conv_c_w'], 'v_conv_c_b': out['v_conv_c_b'], 'v_ln_c_g': out['v_ln_c_g'], 'v_ln_c_b': out['v_ln_c_b'], 'v_out_norm_g': out['v_out_norm_g'], 'v_w_out': out['v_w_out'], 'v_norm_ffn_g': out['v_norm_ffn_g'], 'v_w_up': out['v_w_up'], 'v_conv_f_w': out['v_conv_f_w'], 'v_w_down': out['v_w_down'], 'v_rel_bias': out['v_rel_bias'], 'v_final_g': out['v_final_g']}


def _loss(weights, diff, rest, loss_target):
    with _jax.named_scope("forward"):
        args = {**rest, TWIN_DIFF_INPUT: diff, **{k: w.astype(_WEIGHT_DTYPES[k]) for k, w in weights.items()}}
        y = _forward(args)
    with _jax.named_scope("loss_head"):
        err = _jnp.square(y.astype(_jnp.float32) - loss_target)
        return 0.5 * _jnp.sum(_jnp.mean(err, axis=-1)) if err.ndim else 0.5 * err


def _adamw(w, g, m, v):
    m = ADAM_B1 * m + (1.0 - ADAM_B1) * g
    v = ADAM_B2 * v + (1.0 - ADAM_B2) * _jnp.square(g)
    m_hat = m / (1.0 - ADAM_B1 ** ADAM_STEP)
    v_hat = v / (1.0 - ADAM_B2 ** ADAM_STEP)
    delta = -ADAM_LR * (m_hat / (_jnp.sqrt(v_hat) + ADAM_EPS) + ADAM_WD * w)
    return delta, m, v


def reference(x, norm_mix_g, w_in, conv_a_w, conv_c_w, conv_c_b, ln_c_g, ln_c_b, out_norm_g, w_out, norm_ffn_g, w_up, conv_f_w, w_down, rel_bias, final_g, loss_target, m_norm_mix_g, m_w_in, m_conv_a_w, m_conv_c_w, m_conv_c_b, m_ln_c_g, m_ln_c_b, m_out_norm_g, m_w_out, m_norm_ffn_g, m_w_up, m_conv_f_w, m_w_down, m_rel_bias, m_final_g, v_norm_mix_g, v_w_in, v_conv_a_w, v_conv_c_w, v_conv_c_b, v_ln_c_g, v_ln_c_b, v_out_norm_g, v_w_out, v_norm_ffn_g, v_w_up, v_conv_f_w, v_w_down, v_rel_bias, v_final_g):
    given = dict(x=x, norm_mix_g=norm_mix_g, w_in=w_in, conv_a_w=conv_a_w, conv_c_w=conv_c_w, conv_c_b=conv_c_b, ln_c_g=ln_c_g, ln_c_b=ln_c_b, out_norm_g=out_norm_g, w_out=w_out, norm_ffn_g=norm_ffn_g, w_up=w_up, conv_f_w=conv_f_w, w_down=w_down, rel_bias=rel_bias, final_g=final_g, loss_target=loss_target, m_norm_mix_g=m_norm_mix_g, m_w_in=m_w_in, m_conv_a_w=m_conv_a_w, m_conv_c_w=m_conv_c_w, m_conv_c_b=m_conv_c_b, m_ln_c_g=m_ln_c_g, m_ln_c_b=m_ln_c_b, m_out_norm_g=m_out_norm_g, m_w_out=m_w_out, m_norm_ffn_g=m_norm_ffn_g, m_w_up=m_w_up, m_conv_f_w=m_conv_f_w, m_w_down=m_w_down, m_rel_bias=m_rel_bias, m_final_g=m_final_g, v_norm_mix_g=v_norm_mix_g, v_w_in=v_w_in, v_conv_a_w=v_conv_a_w, v_conv_c_w=v_conv_c_w, v_conv_c_b=v_conv_c_b, v_ln_c_g=v_ln_c_g, v_ln_c_b=v_ln_c_b, v_out_norm_g=v_out_norm_g, v_w_out=v_w_out, v_norm_ffn_g=v_norm_ffn_g, v_w_up=v_w_up, v_conv_f_w=v_conv_f_w, v_w_down=v_w_down, v_rel_bias=v_rel_bias, v_final_g=v_final_g)
    weights = {n: given[n] for n in TWIN_WEIGHTS}
    shared = {n: given[n] for n in SHARED_INPUTS}
    per_example = {n: given[n] for n in ['x']}
    grad_fn = _jax.value_and_grad(_loss, argnums=(0, 1))

    def one_microbatch(ex, loss_target):
        ex = dict(ex)
        diff = ex.pop(TWIN_DIFF_INPUT)
        return grad_fn(weights, diff, {**shared, **ex}, loss_target)

    if N_MICROBATCH == 1:
        loss, (grad_w, grad_x) = one_microbatch(per_example, given["loss_target"])
    else:
        def body(carry, xs):
            loss_sum, grad_sum = carry
            l_k, (gw_k, gx_k) = one_microbatch(xs[0], xs[1])
            with _jax.named_scope("update"):
                return (loss_sum + l_k, _jax.tree.map(_jnp.add, grad_sum, gw_k)), gx_k

        init = (_jnp.zeros((), _jnp.float32), _jax.tree.map(_jnp.zeros_like, weights))
        (loss, grad_w), grad_x = _jax.lax.scan(body, init, (per_example, given["loss_target"]))
    with _jax.named_scope("update"):
        delta_w, new_m, new_v = {}, {}, {}
        for n in TWIN_WEIGHTS:
            delta_w[n], new_m[n], new_v[n] = _adamw(weights[n], grad_w[n], given["m_" + n], given["v_" + n])
    return (loss, grad_x, *[grad_w[n] for n in TWIN_WEIGHTS], *[delta_w[n] for n in TWIN_WEIGHTS],
            *[new_m[n] for n in TWIN_WEIGHTS], *[new_v[n] for n in TWIN_WEIGHTS])
```

```python
import math

import numpy as np
import jax
import jax.numpy as jnp
from jax import lax
from jax.experimental import pallas as pl
from jax.experimental.pallas import tpu as pltpu

f32, bf16 = jnp.float32, jnp.bfloat16
SDS = jax.ShapeDtypeStruct
MESH = pl.DeviceIdType.MESH

N_DEV = 8
EPS = 1e-6
NEG = -1e30
D_HEAD = 64
BLK = 128
N_BUCKETS = 32
DILATIONS = (1, 4, 16)
N_KEYS = 128
K_SHORT, K_CONF, K_FFN = 3, 31, 3
ROW_TILE = 256
HALO = 32
HALO_FFN = 8
VMEM_LIMIT = 56 << 20
ADAM_LR, ADAM_B1, ADAM_B2, ADAM_EPS, ADAM_WD, ADAM_STEP = 0.001, 0.9, 0.999, 1e-08, 0.01, 10


def _pcall(body, **kw):
    return pl.pallas_call(body, **kw)


def _cp(sem=None):
    kw = dict(vmem_limit_bytes=VMEM_LIMIT)
    if sem is not None:
        kw["dimension_semantics"] = sem
    return pltpu.CompilerParams(**kw)


def _t5_buckets():
    rel = np.arange(BLK)[:, None] - np.arange(2 * BLK)[None, :] + BLK
    out = []
    for d in DILATIONS:
        dist = np.maximum(rel, 0) * d
        max_exact = N_BUCKETS // 2
        d_f = np.maximum(dist, 1).astype(np.float32)
        large = max_exact + (np.log(d_f / np.float32(max_exact)) / np.float32(math.log(2048 / max_exact))
                             * np.float32(N_BUCKETS - max_exact)).astype(np.int32)
        large = np.minimum(large, N_BUCKETS - 1)
        out.append(np.where(dist < max_exact, dist, large).astype(np.int32))
    return np.stack(out)


_BUCKETS = _t5_buckets()


def _mm(a, b, mode, tr, tc, name, res=None, out_dtype=f32):
    if mode == "nn":
        (m, k), n = a.shape, b.shape[1]
        a_spec = pl.BlockSpec((tr, k), lambda i, j: (i, 0))
        b_spec = pl.BlockSpec((k, tc), lambda i, j: (0, j))
        dims = (((1,), (0,)), ((), ()))
    elif mode == "nt":
        (m, k), n = a.shape, b.shape[0]
        a_spec = pl.BlockSpec((tr, k), lambda i, j: (i, 0))
        b_spec = pl.BlockSpec((tc, k), lambda i, j: (j, 0))
        dims = (((1,), (1,)), ((), ()))
    else:
        (k, m), n = a.shape, b.shape[1]
        a_spec = pl.BlockSpec((k, tr), lambda i, j: (0, i))
        b_spec = pl.BlockSpec((k, tc), lambda i, j: (0, j))
        dims = (((0,), (0,)), ((), ()))
    assert m % tr == 0 and n % tc == 0, (name, m, n, tr, tc)
    o_spec = pl.BlockSpec((tr, tc), lambda i, j: (i, j))

    def body(a_ref, b_ref, *rest):
        o_ref = rest[-1]
        acc = lax.dot_general(a_ref[...].astype(bf16), b_ref[...].astype(bf16), dims, preferred_element_type=f32)
        if res is not None:
            acc = acc + rest[0][...]
        o_ref[...] = acc.astype(out_dtype)

    ins = [a, b] + ([res] if res is not None else [])
    specs = [a_spec, b_spec] + ([o_spec] if res is not None else [])
    return _pcall(body, name=name, out_shape=SDS((m, n), out_dtype), grid=(m // tr, n // tc), in_specs=specs,
                  out_specs=o_spec, compiler_params=_cp(("parallel", "parallel")))(*ins)


def _rstd(x):
    return lax.rsqrt(jnp.mean(x * x, axis=-1, keepdims=True) + EPS)


def _rms_dx(x, r, g, dy):
    dxh = dy * g
    return r * (dxh - x * (r * r) * jnp.mean(dxh * x, axis=-1, keepdims=True))


def _sigmoid(x):
    return 1.0 / (1.0 + jnp.exp(-x))


def _colsum(x):
    return jnp.sum(x, axis=0, keepdims=True)


def _norm_cast(x, g, name):
    s, d = x.shape

    def body(x_ref, g_ref, o_ref):
        xv = x_ref[...]
        o_ref[...] = (xv * _rstd(xv) * g_ref[...]).astype(bf16)

    return _pcall(body, name=name, out_shape=SDS((s, d), bf16), grid=(s // ROW_TILE,),
                  in_specs=[pl.BlockSpec((ROW_TILE, d), lambda i: (i, 0)), pl.BlockSpec((1, d), lambda i: (0, 0))],
                  out_specs=pl.BlockSpec((ROW_TILE, d), lambda i: (i, 0)), compiler_params=_cp(("parallel",)))(x, g)


def _rms_bwd(x, g, dh, dres, name):
    s, d = x.shape

    def body(x_ref, g_ref, dh_ref, dres_ref, dx_ref, dg_ref):
        xv, dhv = x_ref[...], dh_ref[...]
        r = _rstd(xv)
        dx_ref[...] = dres_ref[...] + _rms_dx(xv, r, g_ref[...], dhv)

        @pl.when(pl.program_id(0) == 0)
        def _():
            dg_ref[...] = jnp.zeros_like(dg_ref)

        dg_ref[...] += _colsum(dhv * xv * r)

    row = pl.BlockSpec((ROW_TILE, d), lambda i: (i, 0))
    vec = pl.BlockSpec((1, d), lambda i: (0, 0))
    return _pcall(body, name=name, out_shape=(SDS((s, d), f32), SDS((1, d), f32)), grid=(s // ROW_TILE,),
                  in_specs=[row, vec, row, row], out_specs=(row, vec), compiler_params=_cp(("arbitrary",)))(x, g, dh, dres)


def _final_loss(x, g, target, name):
    s, d = x.shape

    def body(x_ref, g_ref, t_ref, loss_ref, dx_ref, dg_ref):
        xv, gv = x_ref[...], g_ref[...]
        r = _rstd(xv)
        diff = xv * r * gv - t_ref[...]
        dy = diff * (1.0 / d)

        @pl.when(pl.program_id(0) == 0)
        def _():
            dg_ref[...] = jnp.zeros_like(dg_ref)
            loss_ref[...] = jnp.zeros_like(loss_ref)

        part = jnp.sum(jnp.sum(diff * diff, axis=1, keepdims=True), axis=0, keepdims=True) * (0.5 / d)
        loss_ref[...] += part
        dx_ref[...] = _rms_dx(xv, r, gv, dy)
        dg_ref[...] += _colsum(dy * xv * r)

    row = pl.BlockSpec((ROW_TILE, d), lambda i: (i, 0))
    vec = pl.BlockSpec((1, d), lambda i: (0, 0))
    one = pl.BlockSpec((8, 128), lambda i: (0, 0))
    return _pcall(body, name=name, out_shape=(SDS((8, 128), f32), SDS((s, d), f32), SDS((1, d), f32)),
                  grid=(s // ROW_TILE,), in_specs=[row, vec, row], out_specs=(one, row, vec),
                  compiler_params=_cp(("arbitrary",)))(x, g, target)


Z_AH, Z_AB, Z_AC, Z_CV, Z_CG = 0, 1, 2, 9, 10
Z_Q128, Z_K128, Z_V128 = 6, 10, 14


def _cur(t, w, col):
    return pl.BlockSpec((t, w), lambda i: (i, col))


def _prev(hb, t, w, col):
    return pl.BlockSpec((hb, w), lambda i: (jnp.maximum(i * (t // hb) - 1, 0), col))


def _next(hb, t, w, col, rows):
    return pl.BlockSpec((hb, w), lambda i: (jnp.minimum((i + 1) * (t // hb), rows // hb - 1), col))


def _layernorm_parts(u, g, b):
    mu = jnp.mean(u, axis=-1, keepdims=True)
    var = jnp.mean(jnp.square(u - mu), axis=-1, keepdims=True)
    rs = lax.rsqrt(var + EPS)
    xhat = (u - mu) * rs
    return xhat, rs, xhat * g + b


def _mixer_fwd(z, yb, conv_a, conv_c, cb, lng, lnb, og, name):
    s = z.shape[0]
    t, hb, w = ROW_TILE, HALO, 256

    def body(ah, ab, ac, cv, cg, ah_p, ac_p, cv_p, cg_p, yb_ref, wa, wc, cb_ref, lng_ref, lnb_ref, og_ref, y_ref, p_sc, u_sc):
        first = pl.program_id(0) == 0

        def prev(ref):
            return jnp.where(first, 0.0, ref[...])

        p_sc[pl.ds(0, hb), :] = prev(ah_p) * prev(ac_p)
        p_sc[pl.ds(hb, t), :] = ah[...] * ac[...]
        conv = jnp.zeros((t, w), f32)
        for k in range(K_SHORT):
            conv = conv + wa[pl.ds(k, 1), :] * p_sc[pl.ds(hb - (K_SHORT - 1) + k, t), :]
        y_a = ab[...] * conv
        u_sc[pl.ds(0, hb), :] = prev(cv_p) * _sigmoid(prev(cg_p))
        u_sc[pl.ds(hb, t), :] = cv[...] * _sigmoid(cg[...])
        u = jnp.zeros((t, w), f32) + cb_ref[...]
        for k in range(K_CONF):
            u = u + wc[pl.ds(k, 1), :] * u_sc[pl.ds(hb - (K_CONF - 1) + k, t), :]
        _, _, lnv = _layernorm_parts(u, lng_ref[...], lnb_ref[...])
        y_c = lnv * _sigmoid(lnv)
        ybv = yb_ref[...]
        y_ref[:, 0:256] = (y_a * _rstd(y_a) * og_ref[:, 0:256]).astype(bf16)
        y_ref[:, 256:768] = (ybv * _rstd(ybv) * og_ref[:, 256:768]).astype(bf16)
        y_ref[:, 768:1024] = (y_c * _rstd(y_c) * og_ref[:, 768:1024]).astype(bf16)

    full = lambda shape: pl.BlockSpec(shape, lambda i: tuple(0 for _ in shape))
    in_specs = [_cur(t, w, c) for c in (Z_AH, Z_AB, Z_AC, Z_CV, Z_CG)] + [_prev(hb, t, w, c) for c in (Z_AH, Z_AC, Z_CV, Z_CG)]
    in_specs += [_cur(t, 512, 0), full((K_SHORT, w)), full((K_CONF, w)), full((1, w)), full((1, w)), full((1, w)), full((1, 1024))]
    return _pcall(body, name=name, out_shape=SDS((s, 1024), bf16), grid=(s // t,), in_specs=in_specs,
                  out_specs=_cur(t, 1024, 0), scratch_shapes=[pltpu.VMEM((hb + t, w), f32), pltpu.VMEM((hb + t, w), f32)],
                  compiler_params=_cp(("parallel",)))(z, z, z, z, z, z, z, z, z, yb, conv_a, conv_c, cb, lng, lnb, og)


PG_CONV_A, PG_CONV_C, PG_CB, PG_LNG, PG_LNB, PG_OG_A, PG_OG_C, PG_ROWS = 0, 8, 40, 41, 42, 43, 44, 48


def _mixer_bwd(z, yb, dy, conv_a, conv_c, cb, lng, lnb, og, name):
    s = z.shape[0]
    t, hb, w = ROW_TILE, HALO, 256
    e = t + hb

    def body(ah, ab, ac, cv, cg, ah_p, ac_p, cv_p, cg_p, ah_n, ab_n, ac_n, cv_n, cg_n,
             dya, dyb1, dyb2, dyc, dya_n, dyc_n, yb_ref, wa, wc, cb_ref, lng_ref, lnb_ref, og_ref,
             dza_ref, dzc_ref, dyb_ref, pg_ref, dgb_ref, p_sc, u_sc, dc_sc, du_sc):
        i = pl.program_id(0)
        first, last = i == 0, i == pl.num_programs(0) - 1

        def prev(ref):
            return jnp.where(first, 0.0, ref[...])

        def nxt(ref):
            return jnp.where(last, 0.0, ref[...])

        def ext(cur_ref, next_ref):
            return jnp.concatenate([cur_ref[...], nxt(next_ref)], axis=0)

        @pl.when(first)
        def _():
            pg_ref[...] = jnp.zeros_like(pg_ref)
            dgb_ref[...] = jnp.zeros_like(dgb_ref)

        g_a = og_ref[:, 0:256]
        p_sc[pl.ds(0, hb), :] = prev(ah_p) * prev(ac_p)
        p_sc[pl.ds(hb, t), :] = ah[...] * ac[...]
        p_sc[pl.ds(hb + t, hb), :] = nxt(ah_n) * nxt(ac_n)
        conv = jnp.zeros((e, w), f32)
        for k in range(K_SHORT):
            conv = conv + wa[pl.ds(k, 1), :] * p_sc[pl.ds(hb - (K_SHORT - 1) + k, e), :]
        ab_e, dya_e = ext(ab, ab_n), ext(dya, dya_n)
        y_a = ab_e * conv
        r_a = _rstd(y_a)
        dy_a = _rms_dx(y_a, r_a, g_a, dya_e)
        dc_sc[...] = dy_a * ab_e
        dza_ref[:, 256:512] = (dy_a[:t] * conv[:t]).astype(bf16)
        dp = jnp.zeros((t, w), f32)
        for k in range(K_SHORT):
            dp = dp + wa[pl.ds(k, 1), :] * dc_sc[pl.ds(K_SHORT - 1 - k, t), :]
            pg_ref[pl.ds(PG_CONV_A + k, 1), :] += _colsum(dc_sc[pl.ds(0, t), :] * p_sc[pl.ds(hb - (K_SHORT - 1) + k, t), :])
        dza_ref[:, 0:256] = (dp * ac[...]).astype(bf16)
        dza_ref[:, 512:768] = (dp * ah[...]).astype(bf16)
        pg_ref[pl.ds(PG_OG_A, 1), :] += _colsum(dya[...] * y_a[:t] * r_a[:t])

        g_c = og_ref[:, 768:1024]
        u_sc[pl.ds(0, hb), :] = prev(cv_p) * _sigmoid(prev(cg_p))
        u_sc[pl.ds(hb, t), :] = cv[...] * _sigmoid(cg[...])
        u_sc[pl.ds(hb + t, hb), :] = nxt(cv_n) * _sigmoid(nxt(cg_n))
        u = jnp.zeros((e, w), f32) + cb_ref[...]
        for k in range(K_CONF):
            u = u + wc[pl.ds(k, 1), :] * u_sc[pl.ds(hb - (K_CONF - 1) + k, e), :]
        xhat, rs, lnv = _layernorm_parts(u, lng_ref[...], lnb_ref[...])
        sg = _sigmoid(lnv)
        y_c = lnv * sg
        r_c = _rstd(y_c)
        dyc_e = ext(dyc, dyc_n)
        dlnv = _rms_dx(y_c, r_c, g_c, dyc_e) * (sg * (1.0 + lnv * (1.0 - sg)))
        dxh = dlnv * lng_ref[...]
        du = rs * (dxh - jnp.mean(dxh, axis=-1, keepdims=True) - xhat * jnp.mean(dxh * xhat, axis=-1, keepdims=True))
        du_sc[...] = du
        duin = jnp.zeros((t, w), f32)
        for k in range(K_CONF):
            duin = duin + wc[pl.ds(k, 1), :] * du_sc[pl.ds(K_CONF - 1 - k, t), :]
            pg_ref[pl.ds(PG_CONV_C + k, 1), :] += _colsum(du_sc[pl.ds(0, t), :] * u_sc[pl.ds(hb - (K_CONF - 1) + k, t), :])
        sgg = _sigmoid(cg[...])
        dzc_ref[:, 0:256] = (duin * sgg).astype(bf16)
        dzc_ref[:, 256:512] = (duin * cv[...] * sgg * (1.0 - sgg)).astype(bf16)
        pg_ref[pl.ds(PG_CB, 1), :] += _colsum(du[:t])
        pg_ref[pl.ds(PG_LNG, 1), :] += _colsum(dlnv[:t] * xhat[:t])
        pg_ref[pl.ds(PG_LNB, 1), :] += _colsum(dlnv[:t])
        pg_ref[pl.ds(PG_OG_C, 1), :] += _colsum(dyc[...] * y_c[:t] * r_c[:t])

        g_b = og_ref[:, 256:768]
        ybv = yb_ref[...]
        r_b = _rstd(ybv)
        dyb = jnp.concatenate([dyb1[...], dyb2[...]], axis=1)
        dyb_ref[...] = _rms_dx(ybv, r_b, g_b, dyb)
        dgb_ref[...] += _colsum(dyb * ybv * r_b)

    full = lambda shape: pl.BlockSpec(shape, lambda i: tuple(0 for _ in shape))
    zc = (Z_AH, Z_AB, Z_AC, Z_CV, Z_CG)
    in_specs = [_cur(t, w, c) for c in zc] + [_prev(hb, t, w, c) for c in (Z_AH, Z_AC, Z_CV, Z_CG)]
    in_specs += [_next(hb, t, w, c, s) for c in zc]
    in_specs += [_cur(t, w, c) for c in (0, 1, 2, 3)] + [_next(hb, t, w, 0, s), _next(hb, t, w, 3, s)]
    in_specs += [_cur(t, 512, 0), full((K_SHORT, w)), full((K_CONF, w)), full((1, w)), full((1, w)), full((1, w)), full((1, 1024))]
    out_shape = (SDS((s, 768), bf16), SDS((s, 512), bf16), SDS((s, 512), f32), SDS((PG_ROWS, w), f32), SDS((1, 512), f32))
    out_specs = (_cur(t, 768, 0), _cur(t, 512, 0), _cur(t, 512, 0), full((PG_ROWS, w)), full((1, 512)))
    scratch = [pltpu.VMEM((hb + t + hb, w), f32), pltpu.VMEM((hb + t + hb, w), f32), pltpu.VMEM((e, w), f32), pltpu.VMEM((e, w), f32)]
    args = [z] * 14 + [dy] * 6 + [yb, conv_a, conv_c, cb, lng, lnb, og]
    return _pcall(body, name=name, out_shape=out_shape, grid=(s // t,), in_specs=in_specs, out_specs=out_specs,
                  scratch_shapes=scratch, compiler_params=_cp(("arbitrary",)))(*args)


def _ffn_act_fwd(upre, conv_f, name):
    s, f2 = upre.shape
    t, hb, w = ROW_TILE, HALO_FFN, 256
    nj = f2 // 2 // w

    def body(g_c, v_c, g_p, v_p, wg, wv, o_ref, g_sc, v_sc):
        first = pl.program_id(1) == 0
        g_sc[pl.ds(0, hb), :] = jnp.where(first, 0.0, g_p[...])
        v_sc[pl.ds(0, hb), :] = jnp.where(first, 0.0, v_p[...])
        g_sc[pl.ds(hb, t), :] = g_c[...]
        v_sc[pl.ds(hb, t), :] = v_c[...]
        ug = jnp.zeros((t, w), f32)
        uv = jnp.zeros((t, w), f32)
        for k in range(K_FFN):
            ug = ug + wg[pl.ds(k, 1), :] * g_sc[pl.ds(hb - (K_FFN - 1) + k, t), :]
            uv = uv + wv[pl.ds(k, 1), :] * v_sc[pl.ds(hb - (K_FFN - 1) + k, t), :]
        o_ref[...] = (ug * _sigmoid(ug) * uv).astype(bf16)

    cur = lambda off: pl.BlockSpec((t, w), lambda j, i: (i, off + j))
    prv = lambda off: pl.BlockSpec((hb, w), lambda j, i: (jnp.maximum(i * (t // hb) - 1, 0), off + j))
    wsp = lambda off: pl.BlockSpec((K_FFN, w), lambda j, i: (0, off + j))
    return _pcall(body, name=name, out_shape=SDS((s, f2 // 2), bf16), grid=(nj, s // t),
                  in_specs=[cur(0), cur(nj), prv(0), prv(nj), wsp(0), wsp(nj)],
                  out_specs=pl.BlockSpec((t, w), lambda j, i: (i, j)),
                  scratch_shapes=[pltpu.VMEM((hb + t, w), f32), pltpu.VMEM((hb + t, w), f32)],
                  compiler_params=_cp(("parallel", "parallel")))(upre, upre, upre, upre, conv_f, conv_f)


def _ffn_act_bwd(upre, dact, conv_f, name):
    s, f2 = upre.shape
    t, hb, w = ROW_TILE, HALO_FFN, 256
    nj = f2 // 2 // w
    e = t + hb

    def body(g_c, v_c, g_p, v_p, g_n, v_n, da_c, da_n, wg, wv, dg_ref, dv_ref, dwg_ref, dwv_ref, g_sc, v_sc, dg_sc, dv_sc):
        i = pl.program_id(1)
        first, last = i == 0, i == pl.num_programs(1) - 1

        @pl.when(first)
        def _():
            dwg_ref[...] = jnp.zeros_like(dwg_ref)
            dwv_ref[...] = jnp.zeros_like(dwv_ref)

        for sc, p, c, n in ((g_sc, g_p, g_c, g_n), (v_sc, v_p, v_c, v_n)):
            sc[pl.ds(0, hb), :] = jnp.where(first, 0.0, p[...])
            sc[pl.ds(hb, t), :] = c[...]
            sc[pl.ds(hb + t, hb), :] = jnp.where(last, 0.0, n[...])
        ug = jnp.zeros((e, w), f32)
        uv = jnp.zeros((e, w), f32)
        for k in range(K_FFN):
            ug = ug + wg[pl.ds(k, 1), :] * g_sc[pl.ds(hb - (K_FFN - 1) + k, e), :]
            uv = uv + wv[pl.ds(k, 1), :] * v_sc[pl.ds(hb - (K_FFN - 1) + k, e), :]
        da = jnp.concatenate([da_c[...], jnp.where(last, 0.0, da_n[...])], axis=0)
        sg = _sigmoid(ug)
        dv_sc[...] = da * (ug * sg)
        dg_sc[...] = da * uv * (sg * (1.0 + ug * (1.0 - sg)))
        for sc, dsc, wr, dref, dwref in ((g_sc, dg_sc, wg, dg_ref, dwg_ref), (v_sc, dv_sc, wv, dv_ref, dwv_ref)):
            acc = jnp.zeros((t, w), f32)
            for k in range(K_FFN):
                acc = acc + wr[pl.ds(k, 1), :] * dsc[pl.ds(K_FFN - 1 - k, t), :]
                dwref[pl.ds(k, 1), :] += _colsum(dsc[pl.ds(0, t), :] * sc[pl.ds(hb - (K_FFN - 1) + k, t), :])
            dref[...] = acc.astype(bf16)

    cur = lambda off: pl.BlockSpec((t, w), lambda j, i: (i, off + j))
    prv = lambda off: pl.BlockSpec((hb, w), lambda j, i: (jnp.maximum(i * (t // hb) - 1, 0), off + j))
    nxt = lambda off: pl.BlockSpec((hb, w), lambda j, i: (jnp.minimum((i + 1) * (t // hb), s // hb - 1), off + j))
    wsp = lambda off: pl.BlockSpec((K_FFN, w), lambda j, i: (0, off + j))
    half = SDS((s, f2 // 2), bf16)
    dws = SDS((8, f2 // 2), f32)
    o_cur = pl.BlockSpec((t, w), lambda j, i: (i, j))
    o_dw = pl.BlockSpec((8, w), lambda j, i: (0, j))
    return _pcall(body, name=name, out_shape=(half, half, dws, dws), grid=(nj, s // t),
                  in_specs=[cur(0), cur(nj), prv(0), prv(nj), nxt(0), nxt(nj), cur(0), nxt(0), wsp(0), wsp(nj)],
                  out_specs=(o_cur, o_cur, o_dw, o_dw),
                  scratch_shapes=[pltpu.VMEM((hb + t + hb, w), f32), pltpu.VMEM((hb + t + hb, w), f32),
                                  pltpu.VMEM((e, w), f32), pltpu.VMEM((e, w), f32)],
                  compiler_params=_cp(("parallel", "arbitrary")))(upre, upre, upre, upre, upre, upre, dact, dact, conv_f, conv_f)


def _bias_tables(rel_bias, name):
    nb = len(DILATIONS)

    def body(rb_ref, idx_ref, o_ref):
        h = pl.program_id(1)
        iv = idx_ref[0]
        acc = jnp.zeros((BLK, 2 * BLK), f32)
        for b in range(N_BUCKETS):
            acc = jnp.where(iv == b, rb_ref[b, h], acc)
        o_ref[0, 0] = acc

    return _pcall(body, name=name, out_shape=SDS((nb, 8, BLK, 2 * BLK), f32), grid=(nb, 8),
                  in_specs=[pl.BlockSpec(memory_space=pltpu.SMEM), pl.BlockSpec((1, BLK, 2 * BLK), lambda br, h: (br, 0, 0))],
                  out_specs=pl.BlockSpec((1, 1, BLK, 2 * BLK), lambda br, h: (br, h, 0, 0)),
                  compiler_params=_cp(("parallel", "parallel")))(rel_bias, jnp.asarray(_BUCKETS))


def _bias_grad(ds_tabs, name):
    nl, nb = ds_tabs.shape[0], ds_tabs.shape[1]

    def body(t_ref, idx_ref, o_ref):
        br, h = pl.program_id(0), pl.program_id(1)

        @pl.when(jnp.logical_and(br == 0, h == 0))
        def _():
            o_ref[...] = jnp.zeros_like(o_ref)

        tab = t_ref[0, 0, 0]
        for l in range(1, nl):
            tab = tab + t_ref[l, 0, 0]
        iv = idx_ref[0]
        row = lax.broadcasted_iota(jnp.int32, (N_BUCKETS, 128), 0)
        col = lax.broadcasted_iota(jnp.int32, (N_BUCKETS, 128), 1)
        acc = jnp.zeros((N_BUCKETS, 128), f32)
        for b in range(N_BUCKETS):
            sel = jnp.where(iv == b, tab, 0.0)
            tot = jnp.sum(jnp.sum(sel, axis=0, keepdims=True), axis=1, keepdims=True)
            acc = acc + jnp.where(jnp.logical_and(row == b, col == h), tot, 0.0)
        o_ref[...] += acc

    return _pcall(body, name=name, out_shape=SDS((N_BUCKETS, 128), f32), grid=(nb, 8),
                  in_specs=[pl.BlockSpec((nl, 1, 1, BLK, 2 * BLK), lambda br, h: (0, br, h, 0, 0)),
                            pl.BlockSpec((1, BLK, 2 * BLK), lambda br, h: (br, 0, 0))],
                  out_specs=pl.BlockSpec((N_BUCKETS, 128), lambda br, h: (0, 0)),
                  compiler_params=_cp(("arbitrary", "arbitrary")))(ds_tabs, jnp.asarray(_BUCKETS))


def _permute(src, col0, ncol, name):
    s = src.shape[0]

    def body(x_ref, o4_ref, o16_ref):
        for d, o_ref in ((4, o4_ref), (16, o16_ref)):
            sub = s // d
            for r in range(d):
                o_ref[pl.ds(r * sub, sub), :] = x_ref[pl.ds(r, sub, stride=d), :]

    out = SDS((s, ncol * BLK), src.dtype)
    blk = pl.BlockSpec((s, BLK), lambda j: (0, j))
    return _pcall(body, name=name, out_shape=(out, out), grid=(ncol,),
                  in_specs=[pl.BlockSpec((s, BLK), lambda j: (0, col0 + j))], out_specs=(blk, blk),
                  compiler_params=_cp(("parallel",)))(src)


def _band_masks():
    qi = lax.broadcasted_iota(jnp.int32, (BLK, 2 * BLK), 0)
    kj = lax.broadcasted_iota(jnp.int32, (BLK, 2 * BLK), 1)
    band = jnp.logical_and(kj >= qi, kj <= qi + N_KEYS)
    return band, kj


def _attn_fwd(src, offs, bias, branch, name):
    s = src.shape[0]
    nblk = s // BLK
    per_res = nblk // DILATIONS[branch]
    qo, ko, vo = offs

    def body(q_ref, k_ref, v_ref, b_ref, o_ref, l_ref):
        band, kj = _band_masks()
        lane = lax.broadcasted_iota(jnp.int32, (BLK, BLK), 1)
        head0 = lane < D_HEAD

        def step(blk, carry):
            r0 = pl.multiple_of(blk * BLK, BLK)
            p0 = pl.multiple_of(jnp.maximum(blk - 1, 0) * BLK, BLK)
            first_key = jnp.where(blk % per_res == 0, BLK, 0)
            valid = jnp.logical_and(band, kj >= first_key)
            q = (q_ref[pl.ds(r0, BLK), :] * (D_HEAD ** -0.5)).astype(bf16)
            kk = jnp.concatenate([k_ref[pl.ds(p0, BLK), :], k_ref[pl.ds(r0, BLK), :]], axis=0).astype(bf16)
            vv = jnp.concatenate([v_ref[pl.ds(p0, BLK), :], v_ref[pl.ds(r0, BLK), :]], axis=0).astype(bf16)
            outs, lses = [], []
            for hh in range(2):
                mine = head0 if hh == 0 else jnp.logical_not(head0)
                qm = jnp.where(mine, q, jnp.zeros_like(q))
                sc = lax.dot_general(qm, kk, (((1,), (1,)), ((), ())), preferred_element_type=f32) + b_ref[0, hh]
                sc = jnp.where(valid, sc, NEG)
                m = jnp.max(sc, axis=-1, keepdims=True)
                p = jnp.exp(sc - m)
                den = jnp.sum(p, axis=-1, keepdims=True)
                outs.append(jnp.dot(p.astype(bf16), vv, preferred_element_type=f32) / den)
                lses.append(m + jnp.log(den))
            o_ref[pl.ds(r0, BLK), :] = jnp.where(head0, outs[0], outs[1])
            l_ref[pl.ds(r0, BLK), :] = jnp.where(head0, lses[0], lses[1])
            return carry

        lax.fori_loop(0, nblk, step, 0)

    col = lambda off: pl.BlockSpec((s, BLK), lambda hp: (0, off + hp))
    out = SDS((s, 4 * BLK), f32)
    return _pcall(body, name=name, out_shape=(out, out), grid=(4,),
                  in_specs=[col(qo), col(ko), col(vo), pl.BlockSpec((1, 2, BLK, 2 * BLK), lambda hp: (branch, hp, 0, 0))],
                  out_specs=(col(0), col(0)), compiler_params=_cp(("parallel",)))(src, src, src, bias)


def _attn_mix(o1, l1, o2, l2, o3, l3, name):
    s = o1.shape[0]
    chunk = 256

    def body(o1_ref, l1_ref, o2_ref, l2_ref, o3_ref, l3_ref, y_ref, lt_ref, o2n, l2n, o3n, l3n):
        for d, pairs in ((4, ((o2_ref, o2n), (l2_ref, l2n))), (16, ((o3_ref, o3n), (l3_ref, l3n)))):
            sub = s // d
            for src, dst in pairs:
                for r in range(d):
                    dst[pl.ds(r, sub, stride=d), :] = src[pl.ds(r * sub, sub), :]

        def step(c, carry):
            rows = pl.ds(pl.multiple_of(c * chunk, chunk), chunk)
            la, lb, lc = l1_ref[rows, :], l2n[rows, :], l3n[rows, :]
            m = jnp.maximum(jnp.maximum(la, lb), lc)
            ea, eb, ec = jnp.exp(la - m), jnp.exp(lb - m), jnp.exp(lc - m)
            den = ea + eb + ec
            y_ref[rows, :] = (ea * o1_ref[rows, :] + eb * o2n[rows, :] + ec * o3n[rows, :]) / den
            lt_ref[rows, :] = m + jnp.log(den)
            return carry

        lax.fori_loop(0, s // chunk, step, 0)

    col = pl.BlockSpec((s, BLK), lambda hp: (0, hp))
    out = SDS((s, 4 * BLK), f32)
    return _pcall(body, name=name, out_shape=(out, out), grid=(4,), in_specs=[col] * 6, out_specs=(col, col),
                  scratch_shapes=[pltpu.VMEM((s, BLK), f32)] * 4, compiler_params=_cp(("parallel",)))(o1, l1, o2, l2, o3, l3)


def _attn_bwd(src, offs, cat, bias, branch, name):
    s = src.shape[0]
    nblk = s // BLK
    per_res = nblk // DILATIONS[branch]
    qo, ko, vo = offs
    scale = D_HEAD ** -0.5

    def body(q_ref, k_ref, v_ref, dy_ref, y_ref, lt_ref, b_ref, dqkv_ref, ds_ref):
        band, kj = _band_masks()
        lane = lax.broadcasted_iota(jnp.int32, (BLK, BLK), 1)
        head0 = lane < D_HEAD
        dqkv_ref[1] = jnp.zeros((s, BLK), f32)
        dqkv_ref[2] = jnp.zeros((s, BLK), f32)
        ds_ref[...] = jnp.zeros_like(ds_ref)

        def step(blk, carry):
            r0 = pl.multiple_of(blk * BLK, BLK)
            p0 = pl.multiple_of(jnp.maximum(blk - 1, 0) * BLK, BLK)
            first_key = jnp.where(blk % per_res == 0, BLK, 0)
            valid = jnp.logical_and(band, kj >= first_key)
            q = (q_ref[pl.ds(r0, BLK), :] * scale).astype(bf16)
            kk = jnp.concatenate([k_ref[pl.ds(p0, BLK), :], k_ref[pl.ds(r0, BLK), :]], axis=0).astype(bf16)
            vv = jnp.concatenate([v_ref[pl.ds(p0, BLK), :], v_ref[pl.ds(r0, BLK), :]], axis=0).astype(bf16)
            dy = dy_ref[pl.ds(r0, BLK), :]
            dyy = dy * y_ref[pl.ds(r0, BLK), :]
            lt = lt_ref[pl.ds(r0, BLK), :]
            dyb = dy.astype(bf16)
            dqs, dkk, dvv = [], 0.0, 0.0
            for hh in range(2):
                mine = head0 if hh == 0 else jnp.logical_not(head0)
                qm = jnp.where(mine, q, jnp.zeros_like(q))
                dym = jnp.where(mine, dyb, jnp.zeros_like(dyb))
                sc = lax.dot_general(qm, kk, (((1,), (1,)), ((), ())), preferred_element_type=f32) + b_ref[0, hh]
                lt_h = lt[:, hh * D_HEAD:hh * D_HEAD + 1]
                p = jnp.where(valid, jnp.exp(sc - lt_h), 0.0)
                dp = lax.dot_general(dym, vv, (((1,), (1,)), ((), ())), preferred_element_type=f32)
                dd = jnp.sum(jnp.where(mine, dyy, 0.0), axis=-1, keepdims=True)
                ds = p * (dp - dd)
                ds_ref[hh] += ds
                dsb = ds.astype(bf16)
                dqs.append(jnp.dot(dsb, kk, preferred_element_type=f32) * scale)
                dkk = dkk + lax.dot_general(dsb, qm, (((0,), (0,)), ((), ())), preferred_element_type=f32)
                dvv = dvv + lax.dot_general(p.astype(bf16), dym, (((0,), (0,)), ((), ())), preferred_element_type=f32)
            dqkv_ref[0, pl.ds(r0, BLK), :] = jnp.where(head0, dqs[0], dqs[1])
            dqkv_ref[1, pl.ds(p0, BLK), :] += dkk[:BLK]
            dqkv_ref[1, pl.ds(r0, BLK), :] += dkk[BLK:]
            dqkv_ref[2, pl.ds(p0, BLK), :] += dvv[:BLK]
            dqkv_ref[2, pl.ds(r0, BLK), :] += dvv[BLK:]
            return carry

        lax.fori_loop(0, nblk, step, 0)

    col = lambda off: pl.BlockSpec((s, BLK), lambda hp: (0, off + hp))
    tab = lambda br: pl.BlockSpec((1, 2, BLK, 2 * BLK), lambda hp: (br, hp, 0, 0))
    return _pcall(body, name=name, out_shape=(SDS((3, s, 4 * BLK), f32), SDS((8, BLK, 2 * BLK), f32)), grid=(4,),
                  in_specs=[col(qo), col(ko), col(vo), col(0), col(4), col(8), tab(branch)],
                  out_specs=(pl.BlockSpec((3, s, BLK), lambda hp: (0, 0, hp)), pl.BlockSpec((2, BLK, 2 * BLK), lambda hp: (hp, 0, 0))),
                  compiler_params=_cp(("parallel",)))(src, src, src, cat, cat, cat, bias)


def _attn_bwd_sum(d1, d2, d3, name):
    s = d1.shape[1]
    chunk = 256

    def body(d1_ref, d2_ref, d3_ref, o_ref, acc):
        acc[...] = d1_ref[0]
        for d, src in ((4, d2_ref), (16, d3_ref)):
            sub = s // d
            for r in range(d):
                acc[pl.ds(r, sub, stride=d), :] += src[0, pl.ds(r * sub, sub), :]

        def step(c, carry):
            rows = pl.ds(pl.multiple_of(c * chunk, chunk), chunk)
            o_ref[rows, :] = acc[rows, :].astype(bf16)
            return carry

        lax.fori_loop(0, s // chunk, step, 0)

    blk = pl.BlockSpec((1, s, BLK), lambda t, hp: (t, 0, hp))
    return _pcall(body, name=name, out_shape=SDS((s, 12 * BLK), bf16), grid=(3, 4), in_specs=[blk] * 3,
                  out_specs=pl.BlockSpec((s, BLK), lambda t, hp: (0, 4 * t + hp)),
                  scratch_shapes=[pltpu.VMEM((s, BLK), f32)], compiler_params=_cp(("parallel", "parallel")))(d1, d2, d3)


def _adamw_math(w, g, m, v):
    m = ADAM_B1 * m + (1.0 - ADAM_B1) * g
    v = ADAM_B2 * v + (1.0 - ADAM_B2) * (g * g)
    m_hat = m / (1.0 - ADAM_B1 ** ADAM_STEP)
    v_hat = v / (1.0 - ADAM_B2 ** ADAM_STEP)
    delta = -ADAM_LR * (m_hat / (jnp.sqrt(v_hat) + ADAM_EPS) + ADAM_WD * w)
    return delta, m, v


def _adamw(parts, w, m, v, tr, name):
    r, c = w.shape
    assert r % tr == 0

    def body(p_ref, w_ref, m_ref, v_ref, g_ref, d_ref, nm_ref, nv_ref):
        g = p_ref[0].astype(f32)
        for j in range(1, N_DEV):
            g = g + p_ref[j].astype(f32)
        delta, nm, nv = _adamw_math(w_ref[...], g, m_ref[...], v_ref[...])
        g_ref[...] = g
        d_ref[...] = delta
        nm_ref[...] = nm
        nv_ref[...] = nv

    row = pl.BlockSpec((tr, c), lambda i: (i, 0))
    out = SDS((r, c), f32)
    return _pcall(body, name=name, out_shape=(out, out, out, out), grid=(r // tr,),
                  in_specs=[pl.BlockSpec((N_DEV, tr, c), lambda i: (0, i, 0)), row, row, row], out_specs=(row, row, row, row),
                  compiler_params=_cp(("parallel",)))(parts, w, m, v)


def _sum_parts(parts, name):
    _, r, c = parts.shape

    def body(p_ref, o_ref):
        g = p_ref[0]
        for j in range(1, N_DEV):
            g = g + p_ref[j]
        o_ref[...] = g

    return _pcall(body, name=name, out_shape=SDS((r, c), f32), compiler_params=_cp())(parts)


def _adamw_small(g, w, m, v, name):
    def body(g_ref, w_ref, m_ref, v_ref, d_ref, nm_ref, nv_ref):
        delta, nm, nv = _adamw_math(w_ref[...], g_ref[...], m_ref[...], v_ref[...])
        d_ref[...] = delta
        nm_ref[...] = nm
        nv_ref[...] = nv

    out = SDS(w.shape, f32)
    return _pcall(body, name=name, out_shape=(out, out, out), compiler_params=_cp())(g, w, m, v)


def _my_place():
    return lax.axis_index("x"), lax.axis_index("y"), lax.axis_index("c")


def _all_gather(x, name):
    r, c = x.shape

    def body(x_ref, out_ref, send_sems, recv_sems, local_sem):
        mx, my, mc = _my_place()
        me, sibling = (mx, my, mc), (mx, my, 1 - mc)
        chips = [(1 - mx, my), (mx, 1 - my), (1 - mx, 1 - my)]

        def rows(px, py, pc):
            return out_ref.at[4 * px + 2 * py + pc]

        def copy(k, block, to, src=None):
            return pltpu.make_async_remote_copy(src_ref=rows(*block) if src is None else src, dst_ref=rows(*block),
                                                send_sem=send_sems.at[k], recv_sem=recv_sems.at[k], device_id=to, device_id_type=MESH)

        mine = pltpu.make_async_copy(x_ref, rows(*me), local_sem)
        mine.start()
        first = [copy(0, me, sibling, src=x_ref)] + [copy(1 + j, me, (*chip, mc), src=x_ref) for j, chip in enumerate(chips)]
        for cp in first:
            cp.start()
        passed = [copy(4 + j, (*chip, mc), sibling) for j, chip in enumerate(chips)]
        for j, chip in enumerate(chips):
            copy(1 + j, (*chip, mc), me).wait_recv()
            passed[j].start()
        copy(0, sibling, me).wait_recv()
        for j, chip in enumerate(chips):
            copy(4 + j, (*chip, 1 - mc), me).wait_recv()
        for cp in first + passed:
            cp.wait_send()
        mine.wait()

    return _pcall(body, name=name, out_shape=SDS((N_DEV, r, c), x.dtype),
                  in_specs=[pl.BlockSpec(memory_space=pl.ANY)], out_specs=pl.BlockSpec(memory_space=pl.ANY),
                  scratch_shapes=[pltpu.SemaphoreType.DMA((7,)), pltpu.SemaphoreType.DMA((7,)), pltpu.SemaphoreType.DMA(())])(x)


def _all_to_all(x, name):
    def body(x_ref, out_ref, send_sems, recv_sems, local_sem):
        mx, my, mc = _my_place()
        me = 4 * mx + 2 * my + mc
        mine = pltpu.make_async_copy(x_ref.at[me], out_ref.at[me], local_sem)
        mine.start()
        copies = []
        for k in range(1, N_DEV):
            tx, ty, tc = mx ^ ((k >> 2) & 1), my ^ ((k >> 1) & 1), mc ^ (k & 1)
            peer = 4 * tx + 2 * ty + tc
            cp = pltpu.make_async_remote_copy(src_ref=x_ref.at[peer], dst_ref=out_ref.at[me], send_sem=send_sems.at[k - 1],
                                              recv_sem=recv_sems.at[k - 1], device_id=(tx, ty, tc), device_id_type=MESH)
            cp.start()
            copies.append((cp, peer, k))
        for cp, peer, k in copies:
            pltpu.make_async_remote_copy(src_ref=x_ref.at[peer], dst_ref=out_ref.at[peer], send_sem=send_sems.at[k - 1],
                                         recv_sem=recv_sems.at[k - 1], device_id=(mx, my, mc), device_id_type=MESH).wait_recv()
        for cp, _, _ in copies:
            cp.wait_send()
        mine.wait()

    return _pcall(body, name=name, out_shape=SDS(x.shape, x.dtype),
                  in_specs=[pl.BlockSpec(memory_space=pl.ANY)], out_specs=pl.BlockSpec(memory_space=pl.ANY),
                  scratch_shapes=[pltpu.SemaphoreType.DMA((7,)), pltpu.SemaphoreType.DMA((7,)), pltpu.SemaphoreType.DMA(())])(x)


def _layer_fwd(x, p, bias, l):
    tag = f"l{l}"
    h = _norm_cast(x, p["norm_mix_g"], f"{tag}_norm_mix")
    z = _mm(h, p["w_in"], "nn", 512, 1408, f"{tag}_mm_in")
    qkv4, qkv16 = _permute(z, Z_Q128, 12, f"{tag}_perm_qkv")
    o1, l1 = _attn_fwd(z, (Z_Q128, Z_K128, Z_V128), bias, 0, f"{tag}_attn_fwd0")
    o2, l2 = _attn_fwd(qkv4, (0, 4, 8), bias, 1, f"{tag}_attn_fwd1")
    o3, l3 = _attn_fwd(qkv16, (0, 4, 8), bias, 2, f"{tag}_attn_fwd2")
    yb, lt = _attn_mix(o1, l1, o2, l2, o3, l3, f"{tag}_attn_mix")
    y = _mixer_fwd(z, yb, p["conv_a_w"], p["conv_c_w"], p["conv_c_b"], p["ln_c_g"], p["ln_c_b"], p["out_norm_g"], f"{tag}_mixer_fwd")
    x1 = _mm(y, p["w_out"], "nn", 512, 1024, f"{tag}_mm_out", res=x)
    h2 = _norm_cast(x1, p["norm_ffn_g"], f"{tag}_norm_ffn")
    upre = _mm(h2, p["w_up"], "nn", 512, 1408, f"{tag}_mm_up")
    act = _ffn_act_fwd(upre, p["conv_f_w"], f"{tag}_ffn_act")
    x2 = _mm(act, p["w_down"], "nn", 512, 1024, f"{tag}_mm_down", res=x1)
    saved = dict(x=x, h=h, z=z, qkv4=qkv4, qkv16=qkv16, yb=yb, lt=lt, y=y, x1=x1, h2=h2, upre=upre, act=act)
    return x2, saved


def _layer_bwd(dx2, p, sv, bias, l):
    tag = f"l{l}"
    g = {}
    dact = _mm(dx2, p["w_down"], "nt", 512, 1408, f"{tag}_mm_dact")
    g["w_down"] = _mm(sv["act"], dx2, "tn", 1408, 512, f"{tag}_mm_dwdown")
    dug, duv, dwg, dwv = _ffn_act_bwd(sv["upre"], dact, p["conv_f_w"], f"{tag}_ffn_act_bwd")
    g["conv_f_w"] = jnp.concatenate([dwg[:K_FFN], dwv[:K_FFN]], axis=1)
    dupre = jnp.concatenate([dug, duv], axis=1)
    dh2 = _mm(dupre, p["w_up"], "nt", 512, 512, f"{tag}_mm_dh2")
    g["w_up"] = _mm(sv["h2"], dupre, "tn", 512, 1408, f"{tag}_mm_dwup")
    dx1, g["norm_ffn_g"] = _rms_bwd(sv["x1"], p["norm_ffn_g"], dh2, dx2, f"{tag}_norm_ffn_bwd")
    dy = _mm(dx1, p["w_out"], "nt", 512, 1024, f"{tag}_mm_dy")
    g["w_out"] = _mm(sv["y"], dx1, "tn", 512, 1024, f"{tag}_mm_dwout")
    dza, dzc, dyb, pg, dgb = _mixer_bwd(sv["z"], sv["yb"], dy, p["conv_a_w"], p["conv_c_w"], p["conv_c_b"], p["ln_c_g"],
                                        p["ln_c_b"], p["out_norm_g"], f"{tag}_mixer_bwd")
    g["conv_a_w"] = pg[PG_CONV_A:PG_CONV_A + K_SHORT]
    g["conv_c_w"] = pg[PG_CONV_C:PG_CONV_C + K_CONF]
    g["conv_c_b"], g["ln_c_g"], g["ln_c_b"] = pg[PG_CB:PG_CB + 1], pg[PG_LNG:PG_LNG + 1], pg[PG_LNB:PG_LNB + 1]
    g["out_norm_g"] = jnp.concatenate([pg[PG_OG_A:PG_OG_A + 1], dgb, pg[PG_OG_C:PG_OG_C + 1]], axis=1)
    cat = jnp.concatenate([dyb, sv["yb"], sv["lt"]], axis=1)
    cat4, cat16 = _permute(cat, 0, 12, f"{tag}_perm_cat")
    d1, t1 = _attn_bwd(sv["z"], (Z_Q128, Z_K128, Z_V128), cat, bias, 0, f"{tag}_attn_bwd0")
    d2, t2 = _attn_bwd(sv["qkv4"], (0, 4, 8), cat4, bias, 1, f"{tag}_attn_bwd1")
    d3, t3 = _attn_bwd(sv["qkv16"], (0, 4, 8), cat16, bias, 2, f"{tag}_attn_bwd2")
    dqkv = _attn_bwd_sum(d1, d2, d3, f"{tag}_attn_bwd_sum")
    dz = jnp.concatenate([dza, dqkv, dzc], axis=1)
    dh = _mm(dz, p["w_in"], "nt", 512, 1024, f"{tag}_mm_dh")
    g["w_in"] = _mm(sv["h"], dz, "tn", 512, 1408, f"{tag}_mm_dwin")
    dx, g["norm_mix_g"] = _rms_bwd(sv["x"], p["norm_mix_g"], dh, dx1, f"{tag}_norm_mix_bwd")
    return dx, g, jnp.stack([t1, t2, t3])


LAYER_PARAMS = ("norm_mix_g", "w_in", "conv_a_w", "conv_c_w", "conv_c_b", "ln_c_g", "ln_c_b", "out_norm_g", "w_out",
                "norm_ffn_g", "w_up", "conv_f_w", "w_down")


def _local_step(x, target, params):
    depth = params["w_in"].shape[0]
    bias = _bias_tables(params["rel_bias"], "bias_tables")
    saved = []
    for l in range(depth):
        x, sv = _layer_fwd(x, {k: params[k][l] for k in LAYER_PARAMS}, bias, l)
        saved.append(sv)
    loss, dx, d_final = _final_loss(x, params["final_g"], target, "final_loss")
    grads, tabs = [None] * depth, [None] * depth
    for l in reversed(range(depth)):
        dx, grads[l], tabs[l] = _layer_bwd(dx, {k: params[k][l] for k in LAYER_PARAMS}, saved[l], bias, l)
    out = {k: jnp.stack([grads[l][k] for l in range(depth)]) for k in LAYER_PARAMS}
    out["rel_bias"] = _bias_grad(jnp.stack(tabs), "bias_grad")[:, :8]
    out["final_g"] = d_final
    return loss[0, 0], dx, out


BIG = ("w_in", "w_out", "w_up", "w_down")
COL_SHARDED = {"w_in": True, "w_out": False, "w_up": True, "w_down": False}
ADAM_ROWS = {"w_in": 512, "w_out": 512, "w_up": 512, "w_down": 352}
SMALL_SHARDED = ("conv_a_w", "conv_c_w", "conv_f_w")
SMALL_REPLICATED = ("norm_mix_g", "conv_c_b", "ln_c_g", "ln_c_b", "out_norm_g", "norm_ffn_g", "rel_bias", "final_g")
WEIGHTS = ("norm_mix_g", "w_in", "conv_a_w", "conv_c_w", "conv_c_b", "ln_c_g", "ln_c_b", "out_norm_g", "w_out",
           "norm_ffn_g", "w_up", "conv_f_w", "w_down", "rel_bias", "final_g")


def _pack(arrays):
    flat = jnp.concatenate([a.reshape(-1) for a in arrays])
    rows = -(-flat.shape[0] // 1024) * 8
    return jnp.pad(flat, (0, rows * 128 - flat.shape[0])).reshape(rows, 128)


def _unpack(packed, shapes):
    flat, out, at = packed.reshape(-1), [], 0
    for shp in shapes:
        n = int(np.prod(shp))
        out.append(flat[at:at + n].reshape(shp))
        at += n
    return out


def _gather_shards(g, local_shape):
    nd = len(local_shape)
    return jnp.moveaxis(g, 0, nd - 1).reshape(local_shape[:-1] + (N_DEV * local_shape[-1],))


def kernel(x, norm_mix_g, w_in, conv_a_w, conv_c_w, conv_c_b, ln_c_g, ln_c_b, out_norm_g, w_out, norm_ffn_g, w_up, conv_f_w, w_down, rel_bias, final_g, loss_target, m_norm_mix_g, m_w_in, m_conv_a_w, m_conv_c_w, m_conv_c_b, m_ln_c_g, m_ln_c_b, m_out_norm_g, m_w_out, m_norm_ffn_g, m_w_up, m_conv_f_w, m_w_down, m_rel_bias, m_final_g, v_norm_mix_g, v_w_in, v_conv_a_w, v_conv_c_w, v_conv_c_b, v_ln_c_g, v_ln_c_b, v_out_norm_g, v_w_out, v_norm_ffn_g, v_w_up, v_conv_f_w, v_w_down, v_rel_bias, v_final_g):
    w = dict(norm_mix_g=norm_mix_g, w_in=w_in, conv_a_w=conv_a_w, conv_c_w=conv_c_w, conv_c_b=conv_c_b, ln_c_g=ln_c_g,
             ln_c_b=ln_c_b, out_norm_g=out_norm_g, w_out=w_out, norm_ffn_g=norm_ffn_g, w_up=w_up, conv_f_w=conv_f_w,
             w_down=w_down, rel_bias=rel_bias, final_g=final_g)
    mom = dict(norm_mix_g=m_norm_mix_g, w_in=m_w_in, conv_a_w=m_conv_a_w, conv_c_w=m_conv_c_w, conv_c_b=m_conv_c_b,
               ln_c_g=m_ln_c_g, ln_c_b=m_ln_c_b, out_norm_g=m_out_norm_g, w_out=m_w_out, norm_ffn_g=m_norm_ffn_g,
               w_up=m_w_up, conv_f_w=m_conv_f_w, w_down=m_w_down, rel_bias=m_rel_bias, final_g=m_final_g)
    var = dict(norm_mix_g=v_norm_mix_g, w_in=v_w_in, conv_a_w=v_conv_a_w, conv_c_w=v_conv_c_w, conv_c_b=v_conv_c_b,
               ln_c_g=v_ln_c_g, ln_c_b=v_ln_c_b, out_norm_g=v_out_norm_g, w_out=v_w_out, norm_ffn_g=v_norm_ffn_g,
               w_up=v_w_up, conv_f_w=v_conv_f_w, w_down=v_w_down, rel_bias=v_rel_bias, final_g=v_final_g)
    depth = w_in.shape[0]
    me = 4 * lax.axis_index("x") + 2 * lax.axis_index("y") + lax.axis_index("c")

    params = {}
    for k in BIG:
        shard = w[k]
        _, r, c = shard.shape
        g = _all_gather(shard.astype(bf16).reshape(depth * r, c), f"gather_{k}").reshape(N_DEV, depth, r, c)
        if COL_SHARDED[k]:
            params[k] = jnp.transpose(g, (1, 2, 0, 3)).reshape(depth, r, N_DEV * c)
        else:
            params[k] = jnp.transpose(g, (1, 0, 2, 3)).reshape(depth, N_DEV * r, c)
    small_shapes = [w[k].shape for k in SMALL_SHARDED]
    gs = _all_gather(_pack([w[k] for k in SMALL_SHARDED]), "gather_small")
    parts = [_unpack(gs[j], small_shapes) for j in range(N_DEV)]
    for n, k in enumerate(SMALL_SHARDED):
        params[k] = _gather_shards(jnp.stack([parts[j][n] for j in range(N_DEV)]), w[k].shape)
    for k in ("norm_mix_g", "conv_c_b", "ln_c_g", "ln_c_b", "out_norm_g", "norm_ffn_g"):
        params[k] = w[k][:, None, :]
    params["rel_bias"] = rel_bias
    params["final_g"] = final_g[None, :]

    loss, dx, grads = _local_step(x[0], loss_target[0], params)
    loss = lax.psum(loss, ("x", "y", "c"))

    out_g, out_d, out_m, out_v = {}, {}, {}, {}
    for k in BIG:
        _, r, c = w[k].shape
        gk = grads[k]
        if COL_SHARDED[k]:
            blocks = jnp.transpose(gk.reshape(depth, r, N_DEV, c), (2, 0, 1, 3))
        else:
            blocks = jnp.transpose(gk.reshape(depth, N_DEV, r, c), (1, 0, 2, 3))
        got = _all_to_all(blocks.astype(bf16).reshape(N_DEV, depth * r, c), f"exchange_{k}")
        res = _adamw(got, w[k].reshape(depth * r, c), mom[k].reshape(depth * r, c), var[k].reshape(depth * r, c),
                     ADAM_ROWS[k], f"adamw_{k}")
        out_g[k], out_d[k], out_m[k], out_v[k] = [a.reshape(depth, r, c) for a in res]

    small = SMALL_SHARDED + SMALL_REPLICATED
    full_shapes = [params[k].shape if k in SMALL_SHARDED else w[k].shape for k in small]
    gsum = _sum_parts(_all_gather(_pack([grads[k].reshape(shp) for k, shp in zip(small, full_shapes)]), "gather_small_grads"),
                      "sum_small_grads")
    gfull = dict(zip(small, _unpack(gsum, full_shapes)))
    for k in SMALL_SHARDED:
        shp = w[k].shape
        gfull[k] = lax.dynamic_index_in_dim(gfull[k].reshape(shp[:-1] + (N_DEV, shp[-1])), me, axis=len(shp) - 1, keepdims=False)
    local_shapes = [w[k].shape for k in small]
    res = _adamw_small(_pack([gfull[k] for k in small]), _pack([w[k] for k in small]), _pack([mom[k] for k in small]),
                       _pack([var[k] for k in small]), "adamw_small")
    for name_map, packed in zip((out_d, out_m, out_v), res):
        name_map.update(zip(small, _unpack(packed, local_shapes)))
    out_g.update(gfull)

    return (loss, dx[None], *[out_g[k] for k in WEIGHTS], *[out_d[k] for k in WEIGHTS], *[out_m[k] for k in WEIGHTS],
            *[out_v[k] for k in WEIGHTS])
```

```python
import math

import numpy as np
import jax
import jax.numpy as jnp
from jax import lax
from jax.experimental import pallas as pl
from jax.experimental.pallas import tpu as pltpu

f32, bf16 = jnp.float32, jnp.bfloat16
SDS = jax.ShapeDtypeStruct
MESH = pl.DeviceIdType.MESH

N_DEV = 8
EPS = 1e-6
NEG = -1e30
D_HEAD = 64
BLK = 128
N_BUCKETS = 32
DILATIONS = (1, 4, 16)
N_KEYS = 128
K_SHORT, K_CONF, K_FFN = 3, 31, 3
ROW_TILE = 256
HALO = 32
HALO_FFN = 8
VMEM_LIMIT = 56 << 20
ADAM_LR, ADAM_B1, ADAM_B2, ADAM_EPS, ADAM_WD, ADAM_STEP = 0.001, 0.9, 0.999, 1e-08, 0.01, 10


def _pcall(body, **kw):
    return pl.pallas_call(body, **kw)


def _cp(sem=None):
    kw = dict(vmem_limit_bytes=VMEM_LIMIT)
    if sem is not None:
        kw["dimension_semantics"] = sem
    return pltpu.CompilerParams(**kw)


def _t5_buckets():
    rel = np.arange(BLK)[:, None] - np.arange(2 * BLK)[None, :] + BLK
    out = []
    for d in DILATIONS:
        dist = np.maximum(rel, 0) * d
        max_exact = N_BUCKETS // 2
        d_f = np.maximum(dist, 1).astype(np.float32)
        large = max_exact + (np.log(d_f / np.float32(max_exact)) / np.float32(math.log(2048 / max_exact))
                             * np.float32(N_BUCKETS - max_exact)).astype(np.int32)
        large = np.minimum(large, N_BUCKETS - 1)
        out.append(np.where(dist < max_exact, dist, large).astype(np.int32))
    return np.stack(out)


_BUCKETS = _t5_buckets()


def _mm(a, b, mode, tr, tc, name, res=None, out_dtype=f32, after=None):
    if mode == "nn":
        (m, k), n = a.shape, b.shape[1]
        a_spec = pl.BlockSpec((tr, k), lambda i, j: (i, 0))
        b_spec = pl.BlockSpec((k, tc), lambda i, j: (0, j))
        dims = (((1,), (0,)), ((), ()))
    elif mode == "nt":
        (m, k), n = a.shape, b.shape[0]
        a_spec = pl.BlockSpec((tr, k), lambda i, j: (i, 0))
        b_spec = pl.BlockSpec((tc, k), lambda i, j: (j, 0))
        dims = (((1,), (1,)), ((), ()))
    else:
        (k, m), n = a.shape, b.shape[1]
        a_spec = pl.BlockSpec((k, tr), lambda i, j: (0, i))
        b_spec = pl.BlockSpec((k, tc), lambda i, j: (0, j))
        dims = (((0,), (0,)), ((), ()))
    assert m % tr == 0 and n % tc == 0, (name, m, n, tr, tc)
    o_spec = pl.BlockSpec((tr, tc), lambda i, j: (i, j))

    def body(a_ref, b_ref, *rest):
        o_ref = rest[-1]
        acc = lax.dot_general(a_ref[...].astype(bf16), b_ref[...].astype(bf16), dims, preferred_element_type=f32)
        if res is not None:
            acc = acc + rest[0][...]
        o_ref[...] = acc.astype(out_dtype)

    ins = [a, b] + ([res] if res is not None else []) + ([after] if after is not None else [])
    specs = [a_spec, b_spec] + ([o_spec] if res is not None else []) + ([pl.BlockSpec(memory_space=pl.ANY)] if after is not None else [])
    return _pcall(body, name=name, out_shape=SDS((m, n), out_dtype), grid=(m // tr, n // tc), in_specs=specs,
                  out_specs=o_spec, compiler_params=_cp(("parallel", "parallel")))(*ins)


def _mm_tn_blocks(a, b, c, tr, tc, name):
    (k, m), n = a.shape, b.shape[1]
    assert n == N_DEV * c and tc % c == 0 and n % tc == 0 and m % tr == 0
    nb = tc // c

    def body(a_ref, b_ref, o_ref):
        acc = lax.dot_general(a_ref[...].astype(bf16), b_ref[...].astype(bf16), (((0,), (0,)), ((), ())), preferred_element_type=f32)
        for q in range(nb):
            o_ref[q] = acc[:, c * q:c * (q + 1)].astype(bf16)

    return _pcall(body, name=name, out_shape=SDS((N_DEV, m, c), bf16), grid=(m // tr, n // tc),
                  in_specs=[pl.BlockSpec((k, tr), lambda i, j: (0, i)), pl.BlockSpec((k, tc), lambda i, j: (0, j))],
                  out_specs=pl.BlockSpec((nb, tr, c), lambda i, j: (j, i, 0)), compiler_params=_cp(("parallel", "parallel")))(a, b)


def _rstd(x):
    return lax.rsqrt(jnp.mean(x * x, axis=-1, keepdims=True) + EPS)


def _rms_dx(x, r, g, dy):
    dxh = dy * g
    return r * (dxh - x * (r * r) * jnp.mean(dxh * x, axis=-1, keepdims=True))


def _sigmoid(x):
    return 1.0 / (1.0 + jnp.exp(-x))


def _colsum(x):
    return jnp.sum(x, axis=0, keepdims=True)


def _norm_cast(x, g, name):
    s, d = x.shape

    def body(x_ref, g_ref, o_ref):
        xv = x_ref[...]
        o_ref[...] = (xv * _rstd(xv) * g_ref[...]).astype(bf16)

    return _pcall(body, name=name, out_shape=SDS((s, d), bf16), grid=(s // ROW_TILE,),
                  in_specs=[pl.BlockSpec((ROW_TILE, d), lambda i: (i, 0)), pl.BlockSpec((1, d), lambda i: (0, 0))],
                  out_specs=pl.BlockSpec((ROW_TILE, d), lambda i: (i, 0)), compiler_params=_cp(("parallel",)))(x, g)


def _rms_bwd(x, g, dh, dres, name):
    s, d = x.shape

    def body(x_ref, g_ref, dh_ref, dres_ref, dx_ref, dg_ref):
        xv, dhv = x_ref[...], dh_ref[...]
        r = _rstd(xv)
        dx_ref[...] = dres_ref[...] + _rms_dx(xv, r, g_ref[...], dhv)

        @pl.when(pl.program_id(0) == 0)
        def _():
            dg_ref[...] = jnp.zeros_like(dg_ref)

        dg_ref[...] += _colsum(dhv * xv * r)

    row = pl.BlockSpec((ROW_TILE, d), lambda i: (i, 0))
    vec = pl.BlockSpec((1, d), lambda i: (0, 0))
    return _pcall(body, name=name, out_shape=(SDS((s, d), f32), SDS((1, d), f32)), grid=(s // ROW_TILE,),
                  in_specs=[row, vec, row, row], out_specs=(row, vec), compiler_params=_cp(("arbitrary",)))(x, g, dh, dres)


def _final_loss(x, g, target, name):
    s, d = x.shape

    def body(x_ref, g_ref, t_ref, loss_ref, dx_ref, dg_ref):
        xv, gv = x_ref[...], g_ref[...]
        r = _rstd(xv)
        diff = xv * r * gv - t_ref[...]
        dy = diff * (1.0 / d)

        @pl.when(pl.program_id(0) == 0)
        def _():
            dg_ref[...] = jnp.zeros_like(dg_ref)
            loss_ref[...] = jnp.zeros_like(loss_ref)

        part = jnp.sum(jnp.sum(diff * diff, axis=1, keepdims=True), axis=0, keepdims=True) * (0.5 / d)
        loss_ref[...] += part
        dx_ref[...] = _rms_dx(xv, r, gv, dy)
        dg_ref[...] += _colsum(dy * xv * r)

    row = pl.BlockSpec((ROW_TILE, d), lambda i: (i, 0))
    vec = pl.BlockSpec((1, d), lambda i: (0, 0))
    one = pl.BlockSpec((8, 128), lambda i: (0, 0))
    return _pcall(body, name=name, out_shape=(SDS((8, 128), f32), SDS((s, d), f32), SDS((1, d), f32)),
                  grid=(s // ROW_TILE,), in_specs=[row, vec, row], out_specs=(one, row, vec),
                  compiler_params=_cp(("arbitrary",)))(x, g, target)


Z_AH, Z_AB, Z_AC, Z_CV, Z_CG = 0, 1, 2, 9, 10
Z_Q128, Z_K128, Z_V128 = 6, 10, 14


def _cur(t, w, col):
    return pl.BlockSpec((t, w), lambda i: (i, col))


def _prev(hb, t, w, col):
    return pl.BlockSpec((hb, w), lambda i: (jnp.maximum(i * (t // hb) - 1, 0), col))


def _next(hb, t, w, col, rows):
    return pl.BlockSpec((hb, w), lambda i: (jnp.minimum((i + 1) * (t // hb), rows // hb - 1), col))


def _layernorm_parts(u, g, b):
    mu = jnp.mean(u, axis=-1, keepdims=True)
    var = jnp.mean(jnp.square(u - mu), axis=-1, keepdims=True)
    rs = lax.rsqrt(var + EPS)
    xhat = (u - mu) * rs
    return xhat, rs, xhat * g + b


def _mixer_fwd(z, yb, conv_a, conv_c, cb, lng, lnb, og, name):
    s = z.shape[0]
    t, hb, w = ROW_TILE, HALO, 256

    def body(ah, ab, ac, cv, cg, ah_p, ac_p, cv_p, cg_p, yb_ref, wa, wc, cb_ref, lng_ref, lnb_ref, og_ref, y_ref, p_sc, u_sc):
        first = pl.program_id(0) == 0

        def prev(ref):
            return jnp.where(first, 0.0, ref[...])

        p_sc[pl.ds(0, hb), :] = prev(ah_p) * prev(ac_p)
        p_sc[pl.ds(hb, t), :] = ah[...] * ac[...]
        conv = jnp.zeros((t, w), f32)
        for k in range(K_SHORT):
            conv = conv + wa[pl.ds(k, 1), :] * p_sc[pl.ds(hb - (K_SHORT - 1) + k, t), :]
        y_a = ab[...] * conv
        u_sc[pl.ds(0, hb), :] = prev(cv_p) * _sigmoid(prev(cg_p))
        u_sc[pl.ds(hb, t), :] = cv[...] * _sigmoid(cg[...])
        u = jnp.zeros((t, w), f32) + cb_ref[...]
        for k in range(K_CONF):
            u = u + wc[pl.ds(k, 1), :] * u_sc[pl.ds(hb - (K_CONF - 1) + k, t), :]
        _, _, lnv = _layernorm_parts(u, lng_ref[...], lnb_ref[...])
        y_c = lnv * _sigmoid(lnv)
        ybv = yb_ref[...]
        y_ref[:, 0:256] = (y_a * _rstd(y_a) * og_ref[:, 0:256]).astype(bf16)
        y_ref[:, 256:768] = (ybv * _rstd(ybv) * og_ref[:, 256:768]).astype(bf16)
        y_ref[:, 768:1024] = (y_c * _rstd(y_c) * og_ref[:, 768:1024]).astype(bf16)

    full = lambda shape: pl.BlockSpec(shape, lambda i: tuple(0 for _ in shape))
    in_specs = [_cur(t, w, c) for c in (Z_AH, Z_AB, Z_AC, Z_CV, Z_CG)] + [_prev(hb, t, w, c) for c in (Z_AH, Z_AC, Z_CV, Z_CG)]
    in_specs += [_cur(t, 512, 0), full((K_SHORT, w)), full((K_CONF, w)), full((1, w)), full((1, w)), full((1, w)), full((1, 1024))]
    return _pcall(body, name=name, out_shape=SDS((s, 1024), bf16), grid=(s // t,), in_specs=in_specs,
                  out_specs=_cur(t, 1024, 0), scratch_shapes=[pltpu.VMEM((hb + t, w), f32), pltpu.VMEM((hb + t, w), f32)],
                  compiler_params=_cp(("parallel",)))(z, z, z, z, z, z, z, z, z, yb, conv_a, conv_c, cb, lng, lnb, og)


PG_CONV_A, PG_CONV_C, PG_CB, PG_LNG, PG_LNB, PG_OG_A, PG_OG_C, PG_ROWS = 0, 8, 40, 41, 42, 43, 44, 48


def _mixer_bwd(z, yb, dy, conv_a, conv_c, cb, lng, lnb, og, name):
    s = z.shape[0]
    t, hb, w = ROW_TILE, HALO, 256
    e = t + hb

    def body(ah, ab, ac, cv, cg, ah_p, ac_p, cv_p, cg_p, ah_n, ab_n, ac_n, cv_n, cg_n,
             dya, dyb1, dyb2, dyc, dya_n, dyc_n, yb_ref, wa, wc, cb_ref, lng_ref, lnb_ref, og_ref,
             dza_ref, dzc_ref, dyb_ref, pg_ref, dgb_ref, p_sc, u_sc, dc_sc, du_sc):
        i = pl.program_id(0)
        first, last = i == 0, i == pl.num_programs(0) - 1

        def prev(ref):
            return jnp.where(first, 0.0, ref[...])

        def nxt(ref):
            return jnp.where(last, 0.0, ref[...])

        def ext(cur_ref, next_ref):
            return jnp.concatenate([cur_ref[...], nxt(next_ref)], axis=0)

        @pl.when(first)
        def _():
            pg_ref[...] = jnp.zeros_like(pg_ref)
            dgb_ref[...] = jnp.zeros_like(dgb_ref)

        g_a = og_ref[:, 0:256]
        p_sc[pl.ds(0, hb), :] = prev(ah_p) * prev(ac_p)
        p_sc[pl.ds(hb, t), :] = ah[...] * ac[...]
        p_sc[pl.ds(hb + t, hb), :] = nxt(ah_n) * nxt(ac_n)
        conv = jnp.zeros((e, w), f32)
        for k in range(K_SHORT):
            conv = conv + wa[pl.ds(k, 1), :] * p_sc[pl.ds(hb - (K_SHORT - 1) + k, e), :]
        ab_e, dya_e = ext(ab, ab_n), ext(dya, dya_n)
        y_a = ab_e * conv
        r_a = _rstd(y_a)
        dy_a = _rms_dx(y_a, r_a, g_a, dya_e)
        dc_sc[...] = dy_a * ab_e
        dza_ref[:, 256:512] = (dy_a[:t] * conv[:t]).astype(bf16)
        dp = jnp.zeros((t, w), f32)
        for k in range(K_SHORT):
            dp = dp + wa[pl.ds(k, 1), :] * dc_sc[pl.ds(K_SHORT - 1 - k, t), :]
            pg_ref[pl.ds(PG_CONV_A + k, 1), :] += _colsum(dc_sc[pl.ds(0, t), :] * p_sc[pl.ds(hb - (K_SHORT - 1) + k, t), :])
        dza_ref[:, 0:256] = (dp * ac[...]).astype(bf16)
        dza_ref[:, 512:768] = (dp * ah[...]).astype(bf16)
        pg_ref[pl.ds(PG_OG_A, 1), :] += _colsum(dya[...] * y_a[:t] * r_a[:t])

        g_c = og_ref[:, 768:1024]
        u_sc[pl.ds(0, hb), :] = prev(cv_p) * _sigmoid(prev(cg_p))
        u_sc[pl.ds(hb, t), :] = cv[...] * _sigmoid(cg[...])
        u_sc[pl.ds(hb + t, hb), :] = nxt(cv_n) * _sigmoid(nxt(cg_n))
        u = jnp.zeros((e, w), f32) + cb_ref[...]
        for k in range(K_CONF):
            u = u + wc[pl.ds(k, 1), :] * u_sc[pl.ds(hb - (K_CONF - 1) + k, e), :]
        xhat, rs, lnv = _layernorm_parts(u, lng_ref[...], lnb_ref[...])
        sg = _sigmoid(lnv)
        y_c = lnv * sg
        r_c = _rstd(y_c)
        dyc_e = ext(dyc, dyc_n)
        dlnv = _rms_dx(y_c, r_c, g_c, dyc_e) * (sg * (1.0 + lnv * (1.0 - sg)))
        dxh = dlnv * lng_ref[...]
        du = rs * (dxh - jnp.mean(dxh, axis=-1, keepdims=True) - xhat * jnp.mean(dxh * xhat, axis=-1, keepdims=True))
        du_sc[...] = du
        duin = jnp.zeros((t, w), f32)
        for k in range(K_CONF):
            duin = duin + wc[pl.ds(k, 1), :] * du_sc[pl.ds(K_CONF - 1 - k, t), :]
            pg_ref[pl.ds(PG_CONV_C + k, 1), :] += _colsum(du_sc[pl.ds(0, t), :] * u_sc[pl.ds(hb - (K_CONF - 1) + k, t), :])
        sgg = _sigmoid(cg[...])
        dzc_ref[:, 0:256] = (duin * sgg).astype(bf16)
        dzc_ref[:, 256:512] = (duin * cv[...] * sgg * (1.0 - sgg)).astype(bf16)
        pg_ref[pl.ds(PG_CB, 1), :] += _colsum(du[:t])
        pg_ref[pl.ds(PG_LNG, 1), :] += _colsum(dlnv[:t] * xhat[:t])
        pg_ref[pl.ds(PG_LNB, 1), :] += _colsum(dlnv[:t])
        pg_ref[pl.ds(PG_OG_C, 1), :] += _colsum(dyc[...] * y_c[:t] * r_c[:t])

        g_b = og_ref[:, 256:768]
        ybv = yb_ref[...]
        r_b = _rstd(ybv)
        dyb = jnp.concatenate([dyb1[...], dyb2[...]], axis=1)
        dyb_ref[...] = _rms_dx(ybv, r_b, g_b, dyb)
        dgb_ref[...] += _colsum(dyb * ybv * r_b)

    full = lambda shape: pl.BlockSpec(shape, lambda i: tuple(0 for _ in shape))
    zc = (Z_AH, Z_AB, Z_AC, Z_CV, Z_CG)
    in_specs = [_cur(t, w, c) for c in zc] + [_prev(hb, t, w, c) for c in (Z_AH, Z_AC, Z_CV, Z_CG)]
    in_specs += [_next(hb, t, w, c, s) for c in zc]
    in_specs += [_cur(t, w, c) for c in (0, 1, 2, 3)] + [_next(hb, t, w, 0, s), _next(hb, t, w, 3, s)]
    in_specs += [_cur(t, 512, 0), full((K_SHORT, w)), full((K_CONF, w)), full((1, w)), full((1, w)), full((1, w)), full((1, 1024))]
    out_shape = (SDS((s, 768), bf16), SDS((s, 512), bf16), SDS((s, 512), f32), SDS((PG_ROWS, w), f32), SDS((1, 512), f32))
    out_specs = (_cur(t, 768, 0), _cur(t, 512, 0), _cur(t, 512, 0), full((PG_ROWS, w)), full((1, 512)))
    scratch = [pltpu.VMEM((hb + t + hb, w), f32), pltpu.VMEM((hb + t + hb, w), f32), pltpu.VMEM((e, w), f32), pltpu.VMEM((e, w), f32)]
    args = [z] * 14 + [dy] * 6 + [yb, conv_a, conv_c, cb, lng, lnb, og]
    return _pcall(body, name=name, out_shape=out_shape, grid=(s // t,), in_specs=in_specs, out_specs=out_specs,
                  scratch_shapes=scratch, compiler_params=_cp(("arbitrary",)))(*args)


def _ffn_act_fwd(upre, conv_f, name):
    s, f2 = upre.shape
    t, hb, w = ROW_TILE, HALO_FFN, 256
    nj = f2 // 2 // w

    def body(g_c, v_c, g_p, v_p, wg, wv, o_ref, g_sc, v_sc):
        first = pl.program_id(1) == 0
        g_sc[pl.ds(0, hb), :] = jnp.where(first, 0.0, g_p[...])
        v_sc[pl.ds(0, hb), :] = jnp.where(first, 0.0, v_p[...])
        g_sc[pl.ds(hb, t), :] = g_c[...]
        v_sc[pl.ds(hb, t), :] = v_c[...]
        ug = jnp.zeros((t, w), f32)
        uv = jnp.zeros((t, w), f32)
        for k in range(K_FFN):
            ug = ug + wg[pl.ds(k, 1), :] * g_sc[pl.ds(hb - (K_FFN - 1) + k, t), :]
            uv = uv + wv[pl.ds(k, 1), :] * v_sc[pl.ds(hb - (K_FFN - 1) + k, t), :]
        o_ref[...] = (ug * _sigmoid(ug) * uv).astype(bf16)

    cur = lambda off: pl.BlockSpec((t, w), lambda j, i: (i, off + j))
    prv = lambda off: pl.BlockSpec((hb, w), lambda j, i: (jnp.maximum(i * (t // hb) - 1, 0), off + j))
    wsp = lambda off: pl.BlockSpec((K_FFN, w), lambda j, i: (0, off + j))
    return _pcall(body, name=name, out_shape=SDS((s, f2 // 2), bf16), grid=(nj, s // t),
                  in_specs=[cur(0), cur(nj), prv(0), prv(nj), wsp(0), wsp(nj)],
                  out_specs=pl.BlockSpec((t, w), lambda j, i: (i, j)),
                  scratch_shapes=[pltpu.VMEM((hb + t, w), f32), pltpu.VMEM((hb + t, w), f32)],
                  compiler_params=_cp(("parallel", "parallel")))(upre, upre, upre, upre, conv_f, conv_f)


def _ffn_act_bwd(upre, dact, conv_f, name):
    s, f2 = upre.shape
    t, hb, w = ROW_TILE, HALO_FFN, 256
    nj = f2 // 2 // w
    e = t + hb

    def body(g_c, v_c, g_p, v_p, g_n, v_n, da_c, da_n, wg, wv, dg_ref, dv_ref, dwg_ref, dwv_ref, g_sc, v_sc, dg_sc, dv_sc):
        i = pl.program_id(1)
        first, last = i == 0, i == pl.num_programs(1) - 1

        @pl.when(first)
        def _():
            dwg_ref[...] = jnp.zeros_like(dwg_ref)
            dwv_ref[...] = jnp.zeros_like(dwv_ref)

        for sc, p, c, n in ((g_sc, g_p, g_c, g_n), (v_sc, v_p, v_c, v_n)):
            sc[pl.ds(0, hb), :] = jnp.where(first, 0.0, p[...])
            sc[pl.ds(hb, t), :] = c[...]
            sc[pl.ds(hb + t, hb), :] = jnp.where(last, 0.0, n[...])
        ug = jnp.zeros((e, w), f32)
        uv = jnp.zeros((e, w), f32)
        for k in range(K_FFN):
            ug = ug + wg[pl.ds(k, 1), :] * g_sc[pl.ds(hb - (K_FFN - 1) + k, e), :]
            uv = uv + wv[pl.ds(k, 1), :] * v_sc[pl.ds(hb - (K_FFN - 1) + k, e), :]
        da = jnp.concatenate([da_c[...], jnp.where(last, 0.0, da_n[...])], axis=0)
        sg = _sigmoid(ug)
        dv_sc[...] = da * (ug * sg)
        dg_sc[...] = da * uv * (sg * (1.0 + ug * (1.0 - sg)))
        for sc, dsc, wr, dref, dwref in ((g_sc, dg_sc, wg, dg_ref, dwg_ref), (v_sc, dv_sc, wv, dv_ref, dwv_ref)):
            acc = jnp.zeros((t, w), f32)
            for k in range(K_FFN):
                acc = acc + wr[pl.ds(k, 1), :] * dsc[pl.ds(K_FFN - 1 - k, t), :]
                dwref[pl.ds(k, 1), :] += _colsum(dsc[pl.ds(0, t), :] * sc[pl.ds(hb - (K_FFN - 1) + k, t), :])
            dref[...] = acc.astype(bf16)

    cur = lambda off: pl.BlockSpec((t, w), lambda j, i: (i, off + j))
    prv = lambda off: pl.BlockSpec((hb, w), lambda j, i: (jnp.maximum(i * (t // hb) - 1, 0), off + j))
    nxt = lambda off: pl.BlockSpec((hb, w), lambda j, i: (jnp.minimum((i + 1) * (t // hb), s // hb - 1), off + j))
    wsp = lambda off: pl.BlockSpec((K_FFN, w), lambda j, i: (0, off + j))
    half = SDS((s, f2 // 2), bf16)
    dws = SDS((8, f2 // 2), f32)
    o_cur = pl.BlockSpec((t, w), lambda j, i: (i, j))
    o_dw = pl.BlockSpec((8, w), lambda j, i: (0, j))
    return _pcall(body, name=name, out_shape=(half, half, dws, dws), grid=(nj, s // t),
                  in_specs=[cur(0), cur(nj), prv(0), prv(nj), nxt(0), nxt(nj), cur(0), nxt(0), wsp(0), wsp(nj)],
                  out_specs=(o_cur, o_cur, o_dw, o_dw),
                  scratch_shapes=[pltpu.VMEM((hb + t + hb, w), f32), pltpu.VMEM((hb + t + hb, w), f32),
                                  pltpu.VMEM((e, w), f32), pltpu.VMEM((e, w), f32)],
                  compiler_params=_cp(("parallel", "arbitrary")))(upre, upre, upre, upre, upre, upre, dact, dact, conv_f, conv_f)


def _bias_tables(rel_bias, name):
    nb = len(DILATIONS)

    def body(rb_ref, idx_ref, o_ref):
        h = pl.program_id(1)
        iv = idx_ref[0]
        acc = jnp.zeros((BLK, 2 * BLK), f32)
        for b in range(N_BUCKETS):
            acc = jnp.where(iv == b, rb_ref[b, h], acc)
        o_ref[0, 0] = acc

    return _pcall(body, name=name, out_shape=SDS((nb, 8, BLK, 2 * BLK), f32), grid=(nb, 8),
                  in_specs=[pl.BlockSpec(memory_space=pltpu.SMEM), pl.BlockSpec((1, BLK, 2 * BLK), lambda br, h: (br, 0, 0))],
                  out_specs=pl.BlockSpec((1, 1, BLK, 2 * BLK), lambda br, h: (br, h, 0, 0)),
                  compiler_params=_cp(("parallel", "parallel")))(rel_bias, jnp.asarray(_BUCKETS))


def _bias_grad(ds_tabs, name):
    nl, nb = ds_tabs.shape[0], ds_tabs.shape[1]

    def body(t_ref, idx_ref, o_ref):
        br, h = pl.program_id(0), pl.program_id(1)

        @pl.when(jnp.logical_and(br == 0, h == 0))
        def _():
            o_ref[...] = jnp.zeros_like(o_ref)

        tab = t_ref[0, 0, 0]
        for l in range(1, nl):
            tab = tab + t_ref[l, 0, 0]
        iv = idx_ref[0]
        row = lax.broadcasted_iota(jnp.int32, (N_BUCKETS, 128), 0)
        col = lax.broadcasted_iota(jnp.int32, (N_BUCKETS, 128), 1)
        acc = jnp.zeros((N_BUCKETS, 128), f32)
        for b in range(N_BUCKETS):
            sel = jnp.where(iv == b, tab, 0.0)
            tot = jnp.sum(jnp.sum(sel, axis=0, keepdims=True), axis=1, keepdims=True)
            acc = acc + jnp.where(jnp.logical_and(row == b, col == h), tot, 0.0)
        o_ref[...] += acc

    return _pcall(body, name=name, out_shape=SDS((N_BUCKETS, 128), f32), grid=(nb, 8),
                  in_specs=[pl.BlockSpec((nl, 1, 1, BLK, 2 * BLK), lambda br, h: (0, br, h, 0, 0)),
                            pl.BlockSpec((1, BLK, 2 * BLK), lambda br, h: (br, 0, 0))],
                  out_specs=pl.BlockSpec((N_BUCKETS, 128), lambda br, h: (0, 0)),
                  compiler_params=_cp(("arbitrary", "arbitrary")))(ds_tabs, jnp.asarray(_BUCKETS))


def _permute(src, col0, ncol, name):
    s = src.shape[0]

    def body(x_ref, o4_ref, o16_ref):
        for d, o_ref in ((4, o4_ref), (16, o16_ref)):
            sub = s // d
            for r in range(d):
                o_ref[pl.ds(r * sub, sub), :] = x_ref[pl.ds(r, sub, stride=d), :]

    out = SDS((s, ncol * BLK), src.dtype)
    blk = pl.BlockSpec((s, BLK), lambda j: (0, j))
    return _pcall(body, name=name, out_shape=(out, out), grid=(ncol,),
                  in_specs=[pl.BlockSpec((s, BLK), lambda j: (0, col0 + j))], out_specs=(blk, blk),
                  compiler_params=_cp(("parallel",)))(src)


def _band_masks():
    qi = lax.broadcasted_iota(jnp.int32, (BLK, 2 * BLK), 0)
    kj = lax.broadcasted_iota(jnp.int32, (BLK, 2 * BLK), 1)
    band = jnp.logical_and(kj >= qi, kj <= qi + N_KEYS)
    return band, kj


def _attn_fwd(src, offs, bias, branch, name):
    s = src.shape[0]
    nblk = s // BLK
    per_res = nblk // DILATIONS[branch]
    qo, ko, vo = offs

    def body(q_ref, k_ref, v_ref, b_ref, o_ref, l_ref):
        band, kj = _band_masks()
        lane = lax.broadcasted_iota(jnp.int32, (BLK, BLK), 1)
        head0 = lane < D_HEAD

        def step(blk, carry):
            r0 = pl.multiple_of(blk * BLK, BLK)
            p0 = pl.multiple_of(jnp.maximum(blk - 1, 0) * BLK, BLK)
            first_key = jnp.where(blk % per_res == 0, BLK, 0)
            valid = jnp.logical_and(band, kj >= first_key)
            q = (q_ref[pl.ds(r0, BLK), :] * (D_HEAD ** -0.5)).astype(bf16)
            kk = jnp.concatenate([k_ref[pl.ds(p0, BLK), :], k_ref[pl.ds(r0, BLK), :]], axis=0).astype(bf16)
            vv = jnp.concatenate([v_ref[pl.ds(p0, BLK), :], v_ref[pl.ds(r0, BLK), :]], axis=0).astype(bf16)
            outs, lses = [], []
            for hh in range(2):
                mine = head0 if hh == 0 else jnp.logical_not(head0)
                qm = jnp.where(mine, q, jnp.zeros_like(q))
                sc = lax.dot_general(qm, kk, (((1,), (1,)), ((), ())), preferred_element_type=f32) + b_ref[0, hh]
                sc = jnp.where(valid, sc, NEG)
                m = jnp.max(sc, axis=-1, keepdims=True)
                p = jnp.exp(sc - m)
                den = jnp.sum(p, axis=-1, keepdims=True)
                outs.append(jnp.dot(p.astype(bf16), vv, preferred_element_type=f32) / den)
                lses.append(m + jnp.log(den))
            o_ref[pl.ds(r0, BLK), :] = jnp.where(head0, outs[0], outs[1])
            l_ref[pl.ds(r0, BLK), :] = jnp.where(head0, lses[0], lses[1])
            return carry

        lax.fori_loop(0, nblk, step, 0)

    col = lambda off: pl.BlockSpec((s, BLK), lambda hp: (0, off + hp))
    out = SDS((s, 4 * BLK), f32)
    return _pcall(body, name=name, out_shape=(out, out), grid=(4,),
                  in_specs=[col(qo), col(ko), col(vo), pl.BlockSpec((1, 2, BLK, 2 * BLK), lambda hp: (branch, hp, 0, 0))],
                  out_specs=(col(0), col(0)), compiler_params=_cp(("parallel",)))(src, src, src, bias)


def _attn_mix(o1, l1, o2, l2, o3, l3, name):
    s = o1.shape[0]
    chunk = 256

    def body(o1_ref, l1_ref, o2_ref, l2_ref, o3_ref, l3_ref, y_ref, lt_ref, o2n, l2n, o3n, l3n):
        for d, pairs in ((4, ((o2_ref, o2n), (l2_ref, l2n))), (16, ((o3_ref, o3n), (l3_ref, l3n)))):
            sub = s // d
            for src, dst in pairs:
                for r in range(d):
                    dst[pl.ds(r, sub, stride=d), :] = src[pl.ds(r * sub, sub), :]

        def step(c, carry):
            rows = pl.ds(pl.multiple_of(c * chunk, chunk), chunk)
            la, lb, lc = l1_ref[rows, :], l2n[rows, :], l3n[rows, :]
            m = jnp.maximum(jnp.maximum(la, lb), lc)
            ea, eb, ec = jnp.exp(la - m), jnp.exp(lb - m), jnp.exp(lc - m)
            den = ea + eb + ec
            y_ref[rows, :] = (ea * o1_ref[rows, :] + eb * o2n[rows, :] + ec * o3n[rows, :]) / den
            lt_ref[rows, :] = m + jnp.log(den)
            return carry

        lax.fori_loop(0, s // chunk, step, 0)

    col = pl.BlockSpec((s, BLK), lambda hp: (0, hp))
    out = SDS((s, 4 * BLK), f32)
    return _pcall(body, name=name, out_shape=(out, out), grid=(4,), in_specs=[col] * 6, out_specs=(col, col),
                  scratch_shapes=[pltpu.VMEM((s, BLK), f32)] * 4, compiler_params=_cp(("parallel",)))(o1, l1, o2, l2, o3, l3)


def _attn_bwd(src, offs, cat, bias, branch, name):
    s = src.shape[0]
    nblk = s // BLK
    per_res = nblk // DILATIONS[branch]
    qo, ko, vo = offs
    scale = D_HEAD ** -0.5

    def body(q_ref, k_ref, v_ref, dy_ref, y_ref, lt_ref, b_ref, dqkv_ref, ds_ref):
        band, kj = _band_masks()
        lane = lax.broadcasted_iota(jnp.int32, (BLK, BLK), 1)
        head0 = lane < D_HEAD
        dqkv_ref[1] = jnp.zeros((s, BLK), f32)
        dqkv_ref[2] = jnp.zeros((s, BLK), f32)
        ds_ref[...] = jnp.zeros_like(ds_ref)

        def step(blk, carry):
            r0 = pl.multiple_of(blk * BLK, BLK)
            p0 = pl.multiple_of(jnp.maximum(blk - 1, 0) * BLK, BLK)
            first_key = jnp.where(blk % per_res == 0, BLK, 0)
            valid = jnp.logical_and(band, kj >= first_key)
            q = (q_ref[pl.ds(r0, BLK), :] * scale).astype(bf16)
            kk = jnp.concatenate([k_ref[pl.ds(p0, BLK), :], k_ref[pl.ds(r0, BLK), :]], axis=0).astype(bf16)
            vv = jnp.concatenate([v_ref[pl.ds(p0, BLK), :], v_ref[pl.ds(r0, BLK), :]], axis=0).astype(bf16)
            dy = dy_ref[pl.ds(r0, BLK), :]
            dyy = dy * y_ref[pl.ds(r0, BLK), :]
            lt = lt_ref[pl.ds(r0, BLK), :]
            dyb = dy.astype(bf16)
            dqs, dkk, dvv = [], 0.0, 0.0
            for hh in range(2):
                mine = head0 if hh == 0 else jnp.logical_not(head0)
                qm = jnp.where(mine, q, jnp.zeros_like(q))
                dym = jnp.where(mine, dyb, jnp.zeros_like(dyb))
                sc = lax.dot_general(qm, kk, (((1,), (1,)), ((), ())), preferred_element_type=f32) + b_ref[0, hh]
                lt_h = lt[:, hh * D_HEAD:hh * D_HEAD + 1]
                p = jnp.where(valid, jnp.exp(sc - lt_h), 0.0)
                dp = lax.dot_general(dym, vv, (((1,), (1,)), ((), ())), preferred_element_type=f32)
                dd = jnp.sum(jnp.where(mine, dyy, 0.0), axis=-1, keepdims=True)
                ds = p * (dp - dd)
                ds_ref[hh] += ds
                dsb = ds.astype(bf16)
                dqs.append(jnp.dot(dsb, kk, preferred_element_type=f32) * scale)
                dkk = dkk + lax.dot_general(dsb, qm, (((0,), (0,)), ((), ())), preferred_element_type=f32)
                dvv = dvv + lax.dot_general(p.astype(bf16), dym, (((0,), (0,)), ((), ())), preferred_element_type=f32)
            dqkv_ref[0, pl.ds(r0, BLK), :] = jnp.where(head0, dqs[0], dqs[1])
            dqkv_ref[1, pl.ds(p0, BLK), :] += dkk[:BLK]
            dqkv_ref[1, pl.ds(r0, BLK), :] += dkk[BLK:]
            dqkv_ref[2, pl.ds(p0, BLK), :] += dvv[:BLK]
            dqkv_ref[2, pl.ds(r0, BLK), :] += dvv[BLK:]
            return carry

        lax.fori_loop(0, nblk, step, 0)

    col = lambda off: pl.BlockSpec((s, BLK), lambda hp: (0, off + hp))
    tab = lambda br: pl.BlockSpec((1, 2, BLK, 2 * BLK), lambda hp: (br, hp, 0, 0))
    return _pcall(body, name=name, out_shape=(SDS((3, s, 4 * BLK), f32), SDS((8, BLK, 2 * BLK), f32)), grid=(4,),
                  in_specs=[col(qo), col(ko), col(vo), col(0), col(4), col(8), tab(branch)],
                  out_specs=(pl.BlockSpec((3, s, BLK), lambda hp: (0, 0, hp)), pl.BlockSpec((2, BLK, 2 * BLK), lambda hp: (hp, 0, 0))),
                  compiler_params=_cp(("parallel",)))(src, src, src, cat, cat, cat, bias)


def _attn_bwd_sum(d1, d2, d3, name):
    s = d1.shape[1]
    chunk = 256

    def body(d1_ref, d2_ref, d3_ref, o_ref, acc):
        acc[...] = d1_ref[0]
        for d, src in ((4, d2_ref), (16, d3_ref)):
            sub = s // d
            for r in range(d):
                acc[pl.ds(r, sub, stride=d), :] += src[0, pl.ds(r * sub, sub), :]

        def step(c, carry):
            rows = pl.ds(pl.multiple_of(c * chunk, chunk), chunk)
            o_ref[rows, :] = acc[rows, :].astype(bf16)
            return carry

        lax.fori_loop(0, s // chunk, step, 0)

    blk = pl.BlockSpec((1, s, BLK), lambda t, hp: (t, 0, hp))
    return _pcall(body, name=name, out_shape=SDS((s, 12 * BLK), bf16), grid=(3, 4), in_specs=[blk] * 3,
                  out_specs=pl.BlockSpec((s, BLK), lambda t, hp: (0, 4 * t + hp)),
                  scratch_shapes=[pltpu.VMEM((s, BLK), f32)], compiler_params=_cp(("parallel", "parallel")))(d1, d2, d3)


def _adamw_math(w, g, m, v):
    m = ADAM_B1 * m + (1.0 - ADAM_B1) * g
    v = ADAM_B2 * v + (1.0 - ADAM_B2) * (g * g)
    m_hat = m / (1.0 - ADAM_B1 ** ADAM_STEP)
    v_hat = v / (1.0 - ADAM_B2 ** ADAM_STEP)
    delta = -ADAM_LR * (m_hat / (jnp.sqrt(v_hat) + ADAM_EPS) + ADAM_WD * w)
    return delta, m, v


def _adamw_layer(land, send, w, m, v, prev, l, tr, name):
    _, r, c = w.shape
    assert r % tr == 0

    def body(land_ref, send_ref, w_ref, m_ref, v_ref, p0, p1, p2, p3, g_ref, d_ref, nm_ref, nv_ref):
        mx, my, mc = _my_place()
        me = jnp.zeros((tr, c), jnp.int32) + (4 * mx + 2 * my + mc)
        g = jnp.zeros((tr, c), f32)
        for j in range(N_DEV):
            g = g + jnp.where(me == j, send_ref[j], land_ref[j]).astype(f32)
        delta, nm, nv = _adamw_math(w_ref[...], g, m_ref[...], v_ref[...])
        g_ref[...] = g
        d_ref[...] = delta
        nm_ref[...] = nm
        nv_ref[...] = nv

    parts = pl.BlockSpec((N_DEV, tr, c), lambda i: (0, i, 0))
    row = pl.BlockSpec((None, tr, c), lambda i: (l, i, 0))
    anywhere = pl.BlockSpec(memory_space=pl.ANY)
    out = SDS(w.shape, f32)
    return _pcall(body, name=name, out_shape=(out, out, out, out), grid=(r // tr,),
                  in_specs=[parts, parts, row, row, row] + [anywhere] * 4, out_specs=(row, row, row, row),
                  input_output_aliases={5: 0, 6: 1, 7: 2, 8: 3}, compiler_params=_cp(("parallel",)))(land, send, w, m, v, *prev)


def _sum_parts(parts, name):
    _, r, c = parts.shape

    def body(p_ref, o_ref):
        g = p_ref[0]
        for j in range(1, N_DEV):
            g = g + p_ref[j]
        o_ref[...] = g

    return _pcall(body, name=name, out_shape=SDS((r, c), f32), compiler_params=_cp())(parts)


def _adamw_small(g, w, m, v, name):
    def body(g_ref, w_ref, m_ref, v_ref, d_ref, nm_ref, nv_ref):
        delta, nm, nv = _adamw_math(w_ref[...], g_ref[...], m_ref[...], v_ref[...])
        d_ref[...] = delta
        nm_ref[...] = nm
        nv_ref[...] = nv

    out = SDS(w.shape, f32)
    return _pcall(body, name=name, out_shape=(out, out, out), compiler_params=_cp())(g, w, m, v)


def _my_place():
    return lax.axis_index("x"), lax.axis_index("y"), lax.axis_index("c")


def _all_gather(x, name):
    r, c = x.shape

    def body(x_ref, out_ref, send_sems, recv_sems, local_sem):
        mx, my, mc = _my_place()
        me, sibling = (mx, my, mc), (mx, my, 1 - mc)
        chips = [(1 - mx, my), (mx, 1 - my), (1 - mx, 1 - my)]

        def rows(px, py, pc):
            return out_ref.at[4 * px + 2 * py + pc]

        def copy(k, block, to, src=None):
            return pltpu.make_async_remote_copy(src_ref=rows(*block) if src is None else src, dst_ref=rows(*block),
                                                send_sem=send_sems.at[k], recv_sem=recv_sems.at[k], device_id=to, device_id_type=MESH)

        mine = pltpu.make_async_copy(x_ref, rows(*me), local_sem)
        mine.start()
        first = [copy(0, me, sibling, src=x_ref)] + [copy(1 + j, me, (*chip, mc), src=x_ref) for j, chip in enumerate(chips)]
        for cp in first:
            cp.start()
        passed = [copy(4 + j, (*chip, mc), sibling) for j, chip in enumerate(chips)]
        for j, chip in enumerate(chips):
            copy(1 + j, (*chip, mc), me).wait_recv()
            passed[j].start()
        copy(0, sibling, me).wait_recv()
        for j, chip in enumerate(chips):
            copy(4 + j, (*chip, 1 - mc), me).wait_recv()
        for cp in first + passed:
            cp.wait_send()
        mine.wait()

    return _pcall(body, name=name, out_shape=SDS((N_DEV, r, c), x.dtype),
                  in_specs=[pl.BlockSpec(memory_space=pl.ANY)], out_specs=pl.BlockSpec(memory_space=pl.ANY),
                  scratch_shapes=[pltpu.SemaphoreType.DMA((7,)), pltpu.SemaphoreType.DMA((7,)), pltpu.SemaphoreType.DMA(())])(x)


HBM_SPEC = pl.BlockSpec(memory_space=pltpu.HBM)
SEM_SPEC = pl.BlockSpec(memory_space=pltpu.SEMAPHORE)
ANY_SPEC = pl.BlockSpec(memory_space=pl.ANY)
DATAFLOW = pltpu.SideEffectType.DATAFLOW_SIDE_EFFECTING


def _peers():
    mx, my, mc = _my_place()
    out = []
    for k in range(1, N_DEV):
        tx, ty, tc = mx ^ ((k >> 2) & 1), my ^ ((k >> 1) & 1), mc ^ (k & 1)
        out.append(((tx, ty, tc), 4 * tx + 2 * ty + tc))
    return out


def _in_hbm(arrays):
    return [pltpu.with_memory_space_constraint(a, pltpu.HBM) for a in arrays]


def _gather_start(bufs, name):
    nl, nt = len(bufs), len(bufs[0])
    flat = [b for layer in bufs for b in layer]
    n = len(flat)

    def body(*refs):
        ins, sems, token = refs[:n], refs[n:n + 2 * nl], refs[-1]
        mx, my, mc = _my_place()
        me = 4 * mx + 2 * my + mc
        peers = _peers()
        for l in range(nl):
            for t in range(nt):
                buf = ins[l * nt + t]
                for dev, _ in peers:
                    pltpu.make_async_remote_copy(src_ref=buf.at[me], dst_ref=buf.at[me], send_sem=sems[2 * l].at[t],
                                                 recv_sem=sems[2 * l + 1].at[t], device_id=dev, device_id_type=MESH).start()
        token[...] = jnp.zeros_like(token)

    out_shape = tuple(pltpu.SemaphoreType.DMA((nt,)) for _ in range(2 * nl)) + tuple(pltpu.HBM(b.shape, b.dtype) for b in flat)
    out = _pcall(body, name=name, out_shape=out_shape + (SDS((8, 128), f32),), in_specs=(HBM_SPEC,) * n,
                 out_specs=(SEM_SPEC,) * (2 * nl) + (HBM_SPEC,) * n + (pl.BlockSpec(memory_space=pltpu.VMEM),),
                 input_output_aliases={i: 2 * nl + i for i in range(n)},
                 compiler_params=pltpu.CompilerParams(has_side_effects=DATAFLOW))(*_in_hbm(flat))
    sems = [(out[2 * l], out[2 * l + 1]) for l in range(nl)]
    thru = [list(out[2 * nl + l * nt:2 * nl + (l + 1) * nt]) for l in range(nl)]
    return sems, thru, out[-1]


def _exchange_start(sends, lands, name):
    nt = len(sends)

    def body(*refs):
        srcs, dsts, s_sem, r_sem, token = refs[:nt], refs[nt:2 * nt], refs[2 * nt], refs[2 * nt + 1], refs[-1]
        mx, my, mc = _my_place()
        me = 4 * mx + 2 * my + mc
        for t in range(nt):
            for dev, idx in _peers():
                pltpu.make_async_remote_copy(src_ref=srcs[t].at[idx], dst_ref=dsts[t].at[me], send_sem=s_sem.at[t],
                                             recv_sem=r_sem.at[t], device_id=dev, device_id_type=MESH).start()
        token[...] = jnp.zeros_like(token)

    both = list(sends) + list(lands)
    out_shape = (pltpu.SemaphoreType.DMA((nt,)), pltpu.SemaphoreType.DMA((nt,))) + tuple(pltpu.HBM(b.shape, b.dtype) for b in both)
    out = _pcall(body, name=name, out_shape=out_shape + (SDS((8, 128), f32),), in_specs=(HBM_SPEC,) * (2 * nt),
                 out_specs=(SEM_SPEC, SEM_SPEC) + (HBM_SPEC,) * (2 * nt) + (pl.BlockSpec(memory_space=pltpu.VMEM),),
                 input_output_aliases={i: 2 + i for i in range(2 * nt)},
                 compiler_params=pltpu.CompilerParams(has_side_effects=DATAFLOW))(*_in_hbm(both))
    return (out[0], out[1]), list(out[2:2 + nt]), list(out[2 + nt:2 + 2 * nt]), out[-1]


def _copies_wait(srcs, dsts, sems, after, name):
    nt = len(srcs)
    same = srcs is dsts
    bufs = list(srcs) if same else list(srcs) + list(dsts)
    nb = len(bufs)

    def body(*refs):
        s_sem, r_sem = refs[nb], refs[nb + 1]
        mx, my, mc = _my_place()
        for t in range(nt):
            src = refs[t].at[pl.ds(0, N_DEV - 1)]
            dst = src if same else refs[nt + t].at[pl.ds(0, N_DEV - 1)]
            cp = pltpu.make_async_remote_copy(src_ref=src, dst_ref=dst, send_sem=s_sem.at[t], recv_sem=r_sem.at[t],
                                              device_id=(mx, my, 1 - mc), device_id_type=MESH)
            cp.wait_send()
            cp.wait_recv()

    out = _pcall(body, name=name, out_shape=tuple(pltpu.HBM(b.shape, b.dtype) for b in bufs),
                 in_specs=(HBM_SPEC,) * nb + (SEM_SPEC, SEM_SPEC, ANY_SPEC), out_specs=(HBM_SPEC,) * nb,
                 input_output_aliases={i: i for i in range(nb)},
                 compiler_params=pltpu.CompilerParams(has_side_effects=DATAFLOW))(*bufs, sems[0], sems[1], after)
    out = list(out)
    return (out, out) if same else (out[:nt], out[nt:])


def _cast_blocks(weights, l, me, name):
    nt = len(weights)
    halves = 2

    def body(me_ref, *refs):
        for t in range(nt):
            refs[nt + t][...] = refs[t][...].astype(bf16)

    in_specs, out_specs, out_shape = [], [], []
    for wt in weights:
        _, r, c = wt.shape
        tr = r // halves
        in_specs.append(pl.BlockSpec((None, tr, c), lambda i, me_ref: (l, i, 0)))
        out_specs.append(pl.BlockSpec((None, tr, c), lambda i, me_ref: (me_ref[0], i, 0)))
        out_shape.append(SDS((N_DEV, r, c), bf16))
    grid_spec = pltpu.PrefetchScalarGridSpec(num_scalar_prefetch=1, grid=(halves,), in_specs=in_specs, out_specs=out_specs)
    return list(_pcall(body, name=name, out_shape=out_shape, grid_spec=grid_spec, compiler_params=_cp(("parallel",)))(me, *weights))


def _assemble(buf, name):
    _, r, c = buf.shape
    tr = 256

    def body(b_ref, o_ref):
        for j in range(N_DEV):
            o_ref[:, pl.ds(c * j, c)] = b_ref[j]

    return _pcall(body, name=name, out_shape=SDS((r, N_DEV * c), buf.dtype), grid=(r // tr,),
                  in_specs=[pl.BlockSpec((N_DEV, tr, c), lambda i: (0, i, 0))], out_specs=pl.BlockSpec((tr, N_DEV * c), lambda i: (i, 0)),
                  compiler_params=_cp(("parallel",)))(buf)


def _layer_fwd(x, p, bias, l):
    tag = f"l{l}"
    h = _norm_cast(x, p["norm_mix_g"], f"{tag}_norm_mix")
    z = _mm(h, p["w_in"], "nn", 512, 1408, f"{tag}_mm_in")
    qkv4, qkv16 = _permute(z, Z_Q128, 12, f"{tag}_perm_qkv")
    o1, l1 = _attn_fwd(z, (Z_Q128, Z_K128, Z_V128), bias, 0, f"{tag}_attn_fwd0")
    o2, l2 = _attn_fwd(qkv4, (0, 4, 8), bias, 1, f"{tag}_attn_fwd1")
    o3, l3 = _attn_fwd(qkv16, (0, 4, 8), bias, 2, f"{tag}_attn_fwd2")
    yb, lt = _attn_mix(o1, l1, o2, l2, o3, l3, f"{tag}_attn_mix")
    y = _mixer_fwd(z, yb, p["conv_a_w"], p["conv_c_w"], p["conv_c_b"], p["ln_c_g"], p["ln_c_b"], p["out_norm_g"], f"{tag}_mixer_fwd")
    x1 = _mm(y, p["w_out"], "nn", 512, 1024, f"{tag}_mm_out", res=x)
    h2 = _norm_cast(x1, p["norm_ffn_g"], f"{tag}_norm_ffn")
    upre = _mm(h2, p["w_up"], "nn", 512, 1408, f"{tag}_mm_up")
    act = _ffn_act_fwd(upre, p["conv_f_w"], f"{tag}_ffn_act")
    x2 = _mm(act, p["w_down"], "nn", 512, 1024, f"{tag}_mm_down", res=x1)
    saved = dict(x=x, h=h, z=z, qkv4=qkv4, qkv16=qkv16, yb=yb, lt=lt, y=y, x1=x1, h2=h2, upre=upre, act=act)
    return x2, saved


def _layer_bwd(dx2, p, sv, bias, l, after=None):
    tag = f"l{l}"
    g = {}
    dact = _mm(dx2, p["w_down"], "nt", 512, 1408, f"{tag}_mm_dact", after=after)
    dwd = _mm(sv["act"], dx2, "tn", 1408, 512, f"{tag}_mm_dwdown", out_dtype=bf16)
    g["w_down"] = dwd.reshape(N_DEV, dwd.shape[0] // N_DEV, dwd.shape[1])
    dug, duv, dwg, dwv = _ffn_act_bwd(sv["upre"], dact, p["conv_f_w"], f"{tag}_ffn_act_bwd")
    g["conv_f_w"] = jnp.concatenate([dwg[:K_FFN], dwv[:K_FFN]], axis=1)
    dupre = jnp.concatenate([dug, duv], axis=1)
    dh2 = _mm(dupre, p["w_up"], "nt", 512, 512, f"{tag}_mm_dh2")
    g["w_up"] = _mm_tn_blocks(sv["h2"], dupre, dupre.shape[1] // N_DEV, 512, 1408, f"{tag}_mm_dwup")
    dx1, g["norm_ffn_g"] = _rms_bwd(sv["x1"], p["norm_ffn_g"], dh2, dx2, f"{tag}_norm_ffn_bwd")
    dy = _mm(dx1, p["w_out"], "nt", 512, 1024, f"{tag}_mm_dy")
    dwo = _mm(sv["y"], dx1, "tn", 512, 1024, f"{tag}_mm_dwout", out_dtype=bf16)
    g["w_out"] = dwo.reshape(N_DEV, dwo.shape[0] // N_DEV, dwo.shape[1])
    dza, dzc, dyb, pg, dgb = _mixer_bwd(sv["z"], sv["yb"], dy, p["conv_a_w"], p["conv_c_w"], p["conv_c_b"], p["ln_c_g"],
                                        p["ln_c_b"], p["out_norm_g"], f"{tag}_mixer_bwd")
    g["conv_a_w"] = pg[PG_CONV_A:PG_CONV_A + K_SHORT]
    g["conv_c_w"] = pg[PG_CONV_C:PG_CONV_C + K_CONF]
    g["conv_c_b"], g["ln_c_g"], g["ln_c_b"] = pg[PG_CB:PG_CB + 1], pg[PG_LNG:PG_LNG + 1], pg[PG_LNB:PG_LNB + 1]
    g["out_norm_g"] = jnp.concatenate([pg[PG_OG_A:PG_OG_A + 1], dgb, pg[PG_OG_C:PG_OG_C + 1]], axis=1)
    cat = jnp.concatenate([dyb, sv["yb"], sv["lt"]], axis=1)
    cat4, cat16 = _permute(cat, 0, 12, f"{tag}_perm_cat")
    d1, t1 = _attn_bwd(sv["z"], (Z_Q128, Z_K128, Z_V128), cat, bias, 0, f"{tag}_attn_bwd0")
    d2, t2 = _attn_bwd(sv["qkv4"], (0, 4, 8), cat4, bias, 1, f"{tag}_attn_bwd1")
    d3, t3 = _attn_bwd(sv["qkv16"], (0, 4, 8), cat16, bias, 2, f"{tag}_attn_bwd2")
    dqkv = _attn_bwd_sum(d1, d2, d3, f"{tag}_attn_bwd_sum")
    dz = jnp.concatenate([dza, dqkv, dzc], axis=1)
    dh = _mm(dz, p["w_in"], "nt", 512, 1024, f"{tag}_mm_dh")
    g["w_in"] = _mm_tn_blocks(sv["h"], dz, dz.shape[1] // N_DEV, 512, 1408, f"{tag}_mm_dwin")
    dx, g["norm_mix_g"] = _rms_bwd(sv["x"], p["norm_mix_g"], dh, dx1, f"{tag}_norm_mix_bwd")
    return dx, g, jnp.stack([t1, t2, t3])


SMALL_LAYER_PARAMS = ("norm_mix_g", "conv_a_w", "conv_c_w", "conv_c_b", "ln_c_g", "ln_c_b", "out_norm_g", "norm_ffn_g", "conv_f_w")


BIG = ("w_in", "w_out", "w_up", "w_down")
COL_SHARDED = {"w_in": True, "w_out": False, "w_up": True, "w_down": False}
ADAM_ROWS = {"w_in": 512, "w_out": 128, "w_up": 256, "w_down": 176}
SMALL_SHARDED = ("conv_a_w", "conv_c_w", "conv_f_w")
SMALL_REPLICATED = ("norm_mix_g", "conv_c_b", "ln_c_g", "ln_c_b", "out_norm_g", "norm_ffn_g", "rel_bias", "final_g")
WEIGHTS = ("norm_mix_g", "w_in", "conv_a_w", "conv_c_w", "conv_c_b", "ln_c_g", "ln_c_b", "out_norm_g", "w_out",
           "norm_ffn_g", "w_up", "conv_f_w", "w_down", "rel_bias", "final_g")


def _pack(arrays):
    flat = jnp.concatenate([a.reshape(-1) for a in arrays])
    rows = -(-flat.shape[0] // 1024) * 8
    return jnp.pad(flat, (0, rows * 128 - flat.shape[0])).reshape(rows, 128)


def _unpack(packed, shapes):
    flat, out, at = packed.reshape(-1), [], 0
    for shp in shapes:
        n = int(np.prod(shp))
        out.append(flat[at:at + n].reshape(shp))
        at += n
    return out


def _gather_shards(g, local_shape):
    nd = len(local_shape)
    return jnp.moveaxis(g, 0, nd - 1).reshape(local_shape[:-1] + (N_DEV * local_shape[-1],))


def kernel(x, norm_mix_g, w_in, conv_a_w, conv_c_w, conv_c_b, ln_c_g, ln_c_b, out_norm_g, w_out, norm_ffn_g, w_up, conv_f_w, w_down, rel_bias, final_g, loss_target, m_norm_mix_g, m_w_in, m_conv_a_w, m_conv_c_w, m_conv_c_b, m_ln_c_g, m_ln_c_b, m_out_norm_g, m_w_out, m_norm_ffn_g, m_w_up, m_conv_f_w, m_w_down, m_rel_bias, m_final_g, v_norm_mix_g, v_w_in, v_conv_a_w, v_conv_c_w, v_conv_c_b, v_ln_c_g, v_ln_c_b, v_out_norm_g, v_w_out, v_norm_ffn_g, v_w_up, v_conv_f_w, v_w_down, v_rel_bias, v_final_g):
    w = dict(norm_mix_g=norm_mix_g, w_in=w_in, conv_a_w=conv_a_w, conv_c_w=conv_c_w, conv_c_b=conv_c_b, ln_c_g=ln_c_g,
             ln_c_b=ln_c_b, out_norm_g=out_norm_g, w_out=w_out, norm_ffn_g=norm_ffn_g, w_up=w_up, conv_f_w=conv_f_w,
             w_down=w_down, rel_bias=rel_bias, final_g=final_g)
    mom = dict(norm_mix_g=m_norm_mix_g, w_in=m_w_in, conv_a_w=m_conv_a_w, conv_c_w=m_conv_c_w, conv_c_b=m_conv_c_b,
               ln_c_g=m_ln_c_g, ln_c_b=m_ln_c_b, out_norm_g=m_out_norm_g, w_out=m_w_out, norm_ffn_g=m_norm_ffn_g,
               w_up=m_w_up, conv_f_w=m_conv_f_w, w_down=m_w_down, rel_bias=m_rel_bias, final_g=m_final_g)
    var = dict(norm_mix_g=v_norm_mix_g, w_in=v_w_in, conv_a_w=v_conv_a_w, conv_c_w=v_conv_c_w, conv_c_b=v_conv_c_b,
               ln_c_g=v_ln_c_g, ln_c_b=v_ln_c_b, out_norm_g=v_out_norm_g, w_out=v_w_out, norm_ffn_g=v_norm_ffn_g,
               w_up=v_w_up, conv_f_w=v_conv_f_w, w_down=v_w_down, rel_bias=v_rel_bias, final_g=v_final_g)
    depth = w_in.shape[0]
    me = 4 * lax.axis_index("x") + 2 * lax.axis_index("y") + lax.axis_index("c")

    me_arr = me.astype(jnp.int32).reshape(1)
    bufs = [_cast_blocks([w[k] for k in BIG], l, me_arr, f"l{l}_cast_blocks") for l in range(depth)]
    gather_sems, bufs, token = _gather_start(bufs, "gather_start")
    params = {}
    small_shapes = [w[k].shape for k in SMALL_SHARDED]
    gs = _all_gather(_pack([w[k] for k in SMALL_SHARDED]), "gather_small")
    parts = [_unpack(gs[j], small_shapes) for j in range(N_DEV)]
    for n, k in enumerate(SMALL_SHARDED):
        params[k] = _gather_shards(jnp.stack([parts[j][n] for j in range(N_DEV)]), w[k].shape)
    for k in ("norm_mix_g", "conv_c_b", "ln_c_g", "ln_c_b", "out_norm_g", "norm_ffn_g"):
        params[k] = w[k][:, None, :]
    params["rel_bias"] = rel_bias
    params["final_g"] = final_g[None, :]

    bias = _bias_tables(rel_bias, "bias_tables")
    xl, after = x[0], token
    layer_params, saved = [], []
    for l in range(depth):
        _, got = _copies_wait(bufs[l], bufs[l], gather_sems[l], after, f"l{l}_gather_wait")
        p = {k: params[k][l] for k in SMALL_LAYER_PARAMS}
        for k, buf in zip(BIG, got):
            p[k] = _assemble(buf, f"l{l}_assemble_{k}") if COL_SHARDED[k] else buf.reshape(N_DEV * buf.shape[1], buf.shape[2])
        xl, sv = _layer_fwd(xl, p, bias, l)
        layer_params.append(p)
        saved.append(sv)
        after = xl
    loss, dx, d_final = _final_loss(xl, params["final_g"], loss_target[0], "final_loss")
    loss = lax.psum(loss[0, 0], ("x", "y", "c"))

    grads, tabs, pending = [None] * depth, [None] * depth, [None] * depth
    after = None
    for l in reversed(range(depth)):
        dx, grads[l], tabs[l] = _layer_bwd(dx, layer_params[l], saved[l], bias, l, after=after)
        sends = [grads[l][k] for k in BIG]
        sems, sends, lands, after = _exchange_start(sends, [lax.empty(s.shape, s.dtype) for s in sends], f"l{l}_exchange_start")
        pending[l] = (sems, sends, lands)
    grads_small = {k: jnp.stack([grads[l][k] for l in range(depth)]) for k in SMALL_LAYER_PARAMS}
    grads_small["rel_bias"] = _bias_grad(jnp.stack(tabs), "bias_grad")[:, :8]
    grads_small["final_g"] = d_final

    out_g, out_d, out_m, out_v = {}, {}, {}, {}
    results = {k: [lax.empty(w[k].shape, f32) for _ in range(4)] for k in BIG}
    for l in reversed(range(depth)):
        sems, sends, lands = pending[l]
        sends, lands = _copies_wait(sends, lands, sems, dx, f"l{l}_exchange_wait")
        for k, send, land in zip(BIG, sends, lands):
            results[k] = _adamw_layer(land, send, w[k], mom[k], var[k], results[k], l, ADAM_ROWS[k], f"l{l}_adamw_{k}")
    for k in BIG:
        out_g[k], out_d[k], out_m[k], out_v[k] = results[k]
    grads = grads_small

    small = SMALL_SHARDED + SMALL_REPLICATED
    full_shapes = [params[k].shape if k in SMALL_SHARDED else w[k].shape for k in small]
    gsum = _sum_parts(_all_gather(_pack([grads[k].reshape(shp) for k, shp in zip(small, full_shapes)]), "gather_small_grads"),
                      "sum_small_grads")
    gfull = dict(zip(small, _unpack(gsum, full_shapes)))
    for k in SMALL_SHARDED:
        shp = w[k].shape
        gfull[k] = lax.dynamic_index_in_dim(gfull[k].reshape(shp[:-1] + (N_DEV, shp[-1])), me, axis=len(shp) - 1, keepdims=False)
    local_shapes = [w[k].shape for k in small]
    res = _adamw_small(_pack([gfull[k] for k in small]), _pack([w[k] for k in small]), _pack([mom[k] for k in small]),
                       _pack([var[k] for k in small]), "adamw_small")
    for name_map, packed in zip((out_d, out_m, out_v), res):
        name_map.update(zip(small, _unpack(packed, local_shapes)))
    out_g.update(gfull)

    return (loss, dx[None], *[out_g[k] for k in WEIGHTS], *[out_d[k] for k in WEIGHTS], *[out_m[k] for k in WEIGHTS],
            *[out_v[k] for k in WEIGHTS])
```

```python
import math

import numpy as np
import jax
import jax.numpy as jnp
from jax import lax
from jax.experimental import pallas as pl
from jax.experimental.pallas import tpu as pltpu

f32, bf16 = jnp.float32, jnp.bfloat16
SDS = jax.ShapeDtypeStruct
MESH = pl.DeviceIdType.MESH

N_DEV = 8
EPS = 1e-6
NEG = -1e30
D_HEAD = 64
BLK = 128
N_BUCKETS = 32
DILATIONS = (1, 4, 16)
N_KEYS = 128
K_SHORT, K_CONF, K_FFN = 3, 31, 3
ROW_TILE = 256
HALO = 32
HALO_FFN = 8
VMEM_LIMIT = 56 << 20
ADAM_LR, ADAM_B1, ADAM_B2, ADAM_EPS, ADAM_WD, ADAM_STEP = 0.001, 0.9, 0.999, 1e-08, 0.01, 10


def _pcall(body, **kw):
    return pl.pallas_call(body, **kw)


def _cp(sem=None):
    kw = dict(vmem_limit_bytes=VMEM_LIMIT)
    if sem is not None:
        kw["dimension_semantics"] = sem
    return pltpu.CompilerParams(**kw)


def _t5_buckets():
    rel = np.arange(BLK)[:, None] - np.arange(2 * BLK)[None, :] + BLK
    out = []
    for d in DILATIONS:
        dist = np.maximum(rel, 0) * d
        max_exact = N_BUCKETS // 2
        d_f = np.maximum(dist, 1).astype(np.float32)
        large = max_exact + (np.log(d_f / np.float32(max_exact)) / np.float32(math.log(2048 / max_exact))
                             * np.float32(N_BUCKETS - max_exact)).astype(np.int32)
        large = np.minimum(large, N_BUCKETS - 1)
        out.append(np.where(dist < max_exact, dist, large).astype(np.int32))
    return np.stack(out)


_BUCKETS = _t5_buckets()


def _mm(a, b, mode, tr, tc, name, res=None, out_dtype=f32, after=None, b_outer=False):
    ij = (lambda f: (lambda j, i: f(i, j))) if b_outer else (lambda f: f)
    if mode == "nn":
        (m, k), n = a.shape, b.shape[1]
        a_spec = pl.BlockSpec((tr, k), ij(lambda i, j: (i, 0)))
        b_spec = pl.BlockSpec((k, tc), ij(lambda i, j: (0, j)))
        dims = (((1,), (0,)), ((), ()))
    elif mode == "nt":
        (m, k), n = a.shape, b.shape[0]
        a_spec = pl.BlockSpec((tr, k), ij(lambda i, j: (i, 0)))
        b_spec = pl.BlockSpec((tc, k), ij(lambda i, j: (j, 0)))
        dims = (((1,), (1,)), ((), ()))
    else:
        (k, m), n = a.shape, b.shape[1]
        a_spec = pl.BlockSpec((k, tr), ij(lambda i, j: (0, i)))
        b_spec = pl.BlockSpec((k, tc), ij(lambda i, j: (0, j)))
        dims = (((0,), (0,)), ((), ()))
    assert m % tr == 0 and n % tc == 0, (name, m, n, tr, tc)
    o_spec = pl.BlockSpec((tr, tc), ij(lambda i, j: (i, j)))
    grid = (n // tc, m // tr) if b_outer else (m // tr, n // tc)

    def body(a_ref, b_ref, *rest):
        o_ref = rest[-1]
        acc = lax.dot_general(a_ref[...].astype(bf16), b_ref[...].astype(bf16), dims, preferred_element_type=f32)
        if res is not None:
            acc = acc + rest[0][...]
        o_ref[...] = acc.astype(out_dtype)

    ins = [a, b] + ([res] if res is not None else []) + ([after] if after is not None else [])
    specs = [a_spec, b_spec] + ([o_spec] if res is not None else []) + ([pl.BlockSpec(memory_space=pl.ANY)] if after is not None else [])
    return _pcall(body, name=name, out_shape=SDS((m, n), out_dtype), grid=grid, in_specs=specs,
                  out_specs=o_spec, compiler_params=_cp(("parallel", "parallel")))(*ins)


def _rstd(x):
    return lax.rsqrt(jnp.mean(x * x, axis=-1, keepdims=True) + EPS)


def _rms_dx(x, r, g, dy):
    dxh = dy * g
    return r * (dxh - x * (r * r) * jnp.mean(dxh * x, axis=-1, keepdims=True))


def _sigmoid(x):
    return 1.0 / (1.0 + jnp.exp(-x))


def _colsum(x):
    return jnp.sum(x, axis=0, keepdims=True)


def _norm_cast(x, g, name):
    s, d = x.shape

    def body(x_ref, g_ref, o_ref):
        xv = x_ref[...]
        o_ref[...] = (xv * _rstd(xv) * g_ref[...]).astype(bf16)

    return _pcall(body, name=name, out_shape=SDS((s, d), bf16), grid=(s // ROW_TILE,),
                  in_specs=[pl.BlockSpec((ROW_TILE, d), lambda i: (i, 0)), pl.BlockSpec((1, d), lambda i: (0, 0))],
                  out_specs=pl.BlockSpec((ROW_TILE, d), lambda i: (i, 0)), compiler_params=_cp(("parallel",)))(x, g)


def _rms_bwd(x, g, dh, dres, name):
    s, d = x.shape

    def body(x_ref, g_ref, dh_ref, dres_ref, dx_ref, dg_ref):
        xv, dhv = x_ref[...], dh_ref[...]
        r = _rstd(xv)
        dx_ref[...] = dres_ref[...] + _rms_dx(xv, r, g_ref[...], dhv)

        @pl.when(pl.program_id(0) == 0)
        def _():
            dg_ref[...] = jnp.zeros_like(dg_ref)

        dg_ref[...] += _colsum(dhv * xv * r)

    row = pl.BlockSpec((ROW_TILE, d), lambda i: (i, 0))
    vec = pl.BlockSpec((1, d), lambda i: (0, 0))
    return _pcall(body, name=name, out_shape=(SDS((s, d), f32), SDS((1, d), f32)), grid=(s // ROW_TILE,),
                  in_specs=[row, vec, row, row], out_specs=(row, vec), compiler_params=_cp(("arbitrary",)))(x, g, dh, dres)


def _final_loss(x, g, target, name):
    s, d = x.shape

    def body(x_ref, g_ref, t_ref, loss_ref, dx_ref, dg_ref):
        xv, gv = x_ref[...], g_ref[...]
        r = _rstd(xv)
        diff = xv * r * gv - t_ref[...]
        dy = diff * (1.0 / d)

        @pl.when(pl.program_id(0) == 0)
        def _():
            dg_ref[...] = jnp.zeros_like(dg_ref)
            loss_ref[...] = jnp.zeros_like(loss_ref)

        part = jnp.sum(jnp.sum(diff * diff, axis=1, keepdims=True), axis=0, keepdims=True) * (0.5 / d)
        loss_ref[...] += part
        dx_ref[...] = _rms_dx(xv, r, gv, dy)
        dg_ref[...] += _colsum(dy * xv * r)

    row = pl.BlockSpec((ROW_TILE, d), lambda i: (i, 0))
    vec = pl.BlockSpec((1, d), lambda i: (0, 0))
    one = pl.BlockSpec((8, 128), lambda i: (0, 0))
    return _pcall(body, name=name, out_shape=(SDS((8, 128), f32), SDS((s, d), f32), SDS((1, d), f32)),
                  grid=(s // ROW_TILE,), in_specs=[row, vec, row], out_specs=(one, row, vec),
                  compiler_params=_cp(("arbitrary",)))(x, g, target)


Z_AH, Z_AB, Z_AC, Z_CV, Z_CG = 0, 1, 2, 9, 10
Z_Q128, Z_K128, Z_V128 = 6, 10, 14


def _cur(t, w, col):
    return pl.BlockSpec((t, w), lambda i: (i, col))


def _prev(hb, t, w, col):
    return pl.BlockSpec((hb, w), lambda i: (jnp.maximum(i * (t // hb) - 1, 0), col))


def _next(hb, t, w, col, rows):
    return pl.BlockSpec((hb, w), lambda i: (jnp.minimum((i + 1) * (t // hb), rows // hb - 1), col))


def _layernorm_parts(u, g, b):
    mu = jnp.mean(u, axis=-1, keepdims=True)
    var = jnp.mean(jnp.square(u - mu), axis=-1, keepdims=True)
    rs = lax.rsqrt(var + EPS)
    xhat = (u - mu) * rs
    return xhat, rs, xhat * g + b


def _mixer_fwd(z, yb, conv_a, conv_c, cb, lng, lnb, og, name):
    s = z.shape[0]
    t, hb, w = ROW_TILE, HALO, 256

    def body(ah, ab, ac, cv, cg, ah_p, ac_p, cv_p, cg_p, yb_ref, wa, wc, cb_ref, lng_ref, lnb_ref, og_ref, y_ref, p_sc, u_sc):
        first = pl.program_id(0) == 0

        def prev(ref):
            return jnp.where(first, 0.0, ref[...])

        p_sc[pl.ds(0, hb), :] = prev(ah_p) * prev(ac_p)
        p_sc[pl.ds(hb, t), :] = ah[...] * ac[...]
        conv = jnp.zeros((t, w), f32)
        for k in range(K_SHORT):
            conv = conv + wa[pl.ds(k, 1), :] * p_sc[pl.ds(hb - (K_SHORT - 1) + k, t), :]
        y_a = ab[...] * conv
        u_sc[pl.ds(0, hb), :] = prev(cv_p) * _sigmoid(prev(cg_p))
        u_sc[pl.ds(hb, t), :] = cv[...] * _sigmoid(cg[...])
        u = jnp.zeros((t, w), f32) + cb_ref[...]
        for k in range(K_CONF):
            u = u + wc[pl.ds(k, 1), :] * u_sc[pl.ds(hb - (K_CONF - 1) + k, t), :]
        _, _, lnv = _layernorm_parts(u, lng_ref[...], lnb_ref[...])
        y_c = lnv * _sigmoid(lnv)
        ybv = yb_ref[...]
        y_ref[:, 0:256] = (y_a * _rstd(y_a) * og_ref[:, 0:256]).astype(bf16)
        y_ref[:, 256:768] = (ybv * _rstd(ybv) * og_ref[:, 256:768]).astype(bf16)
        y_ref[:, 768:1024] = (y_c * _rstd(y_c) * og_ref[:, 768:1024]).astype(bf16)

    full = lambda shape: pl.BlockSpec(shape, lambda i: tuple(0 for _ in shape))
    in_specs = [_cur(t, w, c) for c in (Z_AH, Z_AB, Z_AC, Z_CV, Z_CG)] + [_prev(hb, t, w, c) for c in (Z_AH, Z_AC, Z_CV, Z_CG)]
    in_specs += [_cur(t, 512, 0), full((K_SHORT, w)), full((K_CONF, w)), full((1, w)), full((1, w)), full((1, w)), full((1, 1024))]
    return _pcall(body, name=name, out_shape=SDS((s, 1024), bf16), grid=(s // t,), in_specs=in_specs,
                  out_specs=_cur(t, 1024, 0), scratch_shapes=[pltpu.VMEM((hb + t, w), f32), pltpu.VMEM((hb + t, w), f32)],
                  compiler_params=_cp(("parallel",)))(z, z, z, z, z, z, z, z, z, yb, conv_a, conv_c, cb, lng, lnb, og)


PG_CONV_A, PG_CONV_C, PG_CB, PG_LNG, PG_LNB, PG_OG_A, PG_OG_C, PG_ROWS = 0, 8, 40, 41, 42, 43, 44, 48


def _mixer_bwd(z, yb, dy, conv_a, conv_c, cb, lng, lnb, og, name):
    s = z.shape[0]
    t, hb, w = ROW_TILE, HALO, 256
    e = t + hb

    def body(ah, ab, ac, cv, cg, ah_p, ac_p, cv_p, cg_p, ah_n, ab_n, ac_n, cv_n, cg_n,
             dya, dyb1, dyb2, dyc, dya_n, dyc_n, yb_ref, wa, wc, cb_ref, lng_ref, lnb_ref, og_ref,
             dza_ref, dzc_ref, dyb_ref, pg_ref, dgb_ref, p_sc, u_sc, dc_sc, du_sc):
        i = pl.program_id(0)
        first, last = i == 0, i == pl.num_programs(0) - 1

        def prev(ref):
            return jnp.where(first, 0.0, ref[...])

        def nxt(ref):
            return jnp.where(last, 0.0, ref[...])

        def ext(cur_ref, next_ref):
            return jnp.concatenate([cur_ref[...], nxt(next_ref)], axis=0)

        @pl.when(first)
        def _():
            pg_ref[...] = jnp.zeros_like(pg_ref)
            dgb_ref[...] = jnp.zeros_like(dgb_ref)

        g_a = og_ref[:, 0:256]
        p_sc[pl.ds(0, hb), :] = prev(ah_p) * prev(ac_p)
        p_sc[pl.ds(hb, t), :] = ah[...] * ac[...]
        p_sc[pl.ds(hb + t, hb), :] = nxt(ah_n) * nxt(ac_n)
        conv = jnp.zeros((e, w), f32)
        for k in range(K_SHORT):
            conv = conv + wa[pl.ds(k, 1), :] * p_sc[pl.ds(hb - (K_SHORT - 1) + k, e), :]
        ab_e, dya_e = ext(ab, ab_n), ext(dya, dya_n)
        y_a = ab_e * conv
        r_a = _rstd(y_a)
        dy_a = _rms_dx(y_a, r_a, g_a, dya_e)
        dc_sc[...] = dy_a * ab_e
        dza_ref[:, 256:512] = (dy_a[:t] * conv[:t]).astype(bf16)
        dp = jnp.zeros((t, w), f32)
        for k in range(K_SHORT):
            dp = dp + wa[pl.ds(k, 1), :] * dc_sc[pl.ds(K_SHORT - 1 - k, t), :]
            pg_ref[pl.ds(PG_CONV_A + k, 1), :] += _colsum(dc_sc[pl.ds(0, t), :] * p_sc[pl.ds(hb - (K_SHORT - 1) + k, t), :])
        dza_ref[:, 0:256] = (dp * ac[...]).astype(bf16)
        dza_ref[:, 512:768] = (dp * ah[...]).astype(bf16)
        pg_ref[pl.ds(PG_OG_A, 1), :] += _colsum(dya[...] * y_a[:t] * r_a[:t])

        g_c = og_ref[:, 768:1024]
        u_sc[pl.ds(0, hb), :] = prev(cv_p) * _sigmoid(prev(cg_p))
        u_sc[pl.ds(hb, t), :] = cv[...] * _sigmoid(cg[...])
        u_sc[pl.ds(hb + t, hb), :] = nxt(cv_n) * _sigmoid(nxt(cg_n))
        u = jnp.zeros((e, w), f32) + cb_ref[...]
        for k in range(K_CONF):
            u = u + wc[pl.ds(k, 1), :] * u_sc[pl.ds(hb - (K_CONF - 1) + k, e), :]
        xhat, rs, lnv = _layernorm_parts(u, lng_ref[...], lnb_ref[...])
        sg = _sigmoid(lnv)
        y_c = lnv * sg
        r_c = _rstd(y_c)
        dyc_e = ext(dyc, dyc_n)
        dlnv = _rms_dx(y_c, r_c, g_c, dyc_e) * (sg * (1.0 + lnv * (1.0 - sg)))
        dxh = dlnv * lng_ref[...]
        du = rs * (dxh - jnp.mean(dxh, axis=-1, keepdims=True) - xhat * jnp.mean(dxh * xhat, axis=-1, keepdims=True))
        du_sc[...] = du
        duin = jnp.zeros((t, w), f32)
        for k in range(K_CONF):
            duin = duin + wc[pl.ds(k, 1), :] * du_sc[pl.ds(K_CONF - 1 - k, t), :]
            pg_ref[pl.ds(PG_CONV_C + k, 1), :] += _colsum(du_sc[pl.ds(0, t), :] * u_sc[pl.ds(hb - (K_CONF - 1) + k, t), :])
        sgg = _sigmoid(cg[...])
        dzc_ref[:, 0:256] = (duin * sgg).astype(bf16)
        dzc_ref[:, 256:512] = (duin * cv[...] * sgg * (1.0 - sgg)).astype(bf16)
        pg_ref[pl.ds(PG_CB, 1), :] += _colsum(du[:t])
        pg_ref[pl.ds(PG_LNG, 1), :] += _colsum(dlnv[:t] * xhat[:t])
        pg_ref[pl.ds(PG_LNB, 1), :] += _colsum(dlnv[:t])
        pg_ref[pl.ds(PG_OG_C, 1), :] += _colsum(dyc[...] * y_c[:t] * r_c[:t])

        g_b = og_ref[:, 256:768]
        ybv = yb_ref[...]
        r_b = _rstd(ybv)
        dyb = jnp.concatenate([dyb1[...], dyb2[...]], axis=1)
        dyb_ref[...] = _rms_dx(ybv, r_b, g_b, dyb)
        dgb_ref[...] += _colsum(dyb * ybv * r_b)

    full = lambda shape: pl.BlockSpec(shape, lambda i: tuple(0 for _ in shape))
    zc = (Z_AH, Z_AB, Z_AC, Z_CV, Z_CG)
    in_specs = [_cur(t, w, c) for c in zc] + [_prev(hb, t, w, c) for c in (Z_AH, Z_AC, Z_CV, Z_CG)]
    in_specs += [_next(hb, t, w, c, s) for c in zc]
    in_specs += [_cur(t, w, c) for c in (0, 1, 2, 3)] + [_next(hb, t, w, 0, s), _next(hb, t, w, 3, s)]
    in_specs += [_cur(t, 512, 0), full((K_SHORT, w)), full((K_CONF, w)), full((1, w)), full((1, w)), full((1, w)), full((1, 1024))]
    out_shape = (SDS((s, 768), bf16), SDS((s, 512), bf16), SDS((s, 512), f32), SDS((PG_ROWS, w), f32), SDS((1, 512), f32))
    out_specs = (_cur(t, 768, 0), _cur(t, 512, 0), _cur(t, 512, 0), full((PG_ROWS, w)), full((1, 512)))
    scratch = [pltpu.VMEM((hb + t + hb, w), f32), pltpu.VMEM((hb + t + hb, w), f32), pltpu.VMEM((e, w), f32), pltpu.VMEM((e, w), f32)]
    args = [z] * 14 + [dy] * 6 + [yb, conv_a, conv_c, cb, lng, lnb, og]
    return _pcall(body, name=name, out_shape=out_shape, grid=(s // t,), in_specs=in_specs, out_specs=out_specs,
                  scratch_shapes=scratch, compiler_params=_cp(("arbitrary",)))(*args)


def _ffn_act_fwd(upre, conv_f, name):
    s, f2 = upre.shape
    t, hb, w = ROW_TILE, HALO_FFN, 256
    nj = f2 // 2 // w

    def body(g_c, v_c, g_p, v_p, wg, wv, o_ref, g_sc, v_sc):
        first = pl.program_id(1) == 0
        g_sc[pl.ds(0, hb), :] = jnp.where(first, 0.0, g_p[...])
        v_sc[pl.ds(0, hb), :] = jnp.where(first, 0.0, v_p[...])
        g_sc[pl.ds(hb, t), :] = g_c[...]
        v_sc[pl.ds(hb, t), :] = v_c[...]
        ug = jnp.zeros((t, w), f32)
        uv = jnp.zeros((t, w), f32)
        for k in range(K_FFN):
            ug = ug + wg[pl.ds(k, 1), :] * g_sc[pl.ds(hb - (K_FFN - 1) + k, t), :]
            uv = uv + wv[pl.ds(k, 1), :] * v_sc[pl.ds(hb - (K_FFN - 1) + k, t), :]
        o_ref[...] = (ug * _sigmoid(ug) * uv).astype(bf16)

    cur = lambda off: pl.BlockSpec((t, w), lambda j, i: (i, off + j))
    prv = lambda off: pl.BlockSpec((hb, w), lambda j, i: (jnp.maximum(i * (t // hb) - 1, 0), off + j))
    wsp = lambda off: pl.BlockSpec((K_FFN, w), lambda j, i: (0, off + j))
    return _pcall(body, name=name, out_shape=SDS((s, f2 // 2), bf16), grid=(nj, s // t),
                  in_specs=[cur(0), cur(nj), prv(0), prv(nj), wsp(0), wsp(nj)],
                  out_specs=pl.BlockSpec((t, w), lambda j, i: (i, j)),
                  scratch_shapes=[pltpu.VMEM((hb + t, w), f32), pltpu.VMEM((hb + t, w), f32)],
                  compiler_params=_cp(("parallel", "parallel")))(upre, upre, upre, upre, conv_f, conv_f)


def _ffn_act_bwd(upre, dact, conv_f, name):
    s, f2 = upre.shape
    t, hb, w = ROW_TILE, HALO_FFN, 256
    nj = f2 // 2 // w
    e = t + hb

    def body(g_c, v_c, g_p, v_p, g_n, v_n, da_c, da_n, wg, wv, dg_ref, dv_ref, dwg_ref, dwv_ref, g_sc, v_sc, dg_sc, dv_sc):
        i = pl.program_id(1)
        first, last = i == 0, i == pl.num_programs(1) - 1

        @pl.when(first)
        def _():
            dwg_ref[...] = jnp.zeros_like(dwg_ref)
            dwv_ref[...] = jnp.zeros_like(dwv_ref)

        for sc, p, c, n in ((g_sc, g_p, g_c, g_n), (v_sc, v_p, v_c, v_n)):
            sc[pl.ds(0, hb), :] = jnp.where(first, 0.0, p[...])
            sc[pl.ds(hb, t), :] = c[...]
            sc[pl.ds(hb + t, hb), :] = jnp.where(last, 0.0, n[...])
        ug = jnp.zeros((e, w), f32)
        uv = jnp.zeros((e, w), f32)
        for k in range(K_FFN):
            ug = ug + wg[pl.ds(k, 1), :] * g_sc[pl.ds(hb - (K_FFN - 1) + k, e), :]
            uv = uv + wv[pl.ds(k, 1), :] * v_sc[pl.ds(hb - (K_FFN - 1) + k, e), :]
        da = jnp.concatenate([da_c[...], jnp.where(last, 0.0, da_n[...])], axis=0)
        sg = _sigmoid(ug)
        dv_sc[...] = da * (ug * sg)
        dg_sc[...] = da * uv * (sg * (1.0 + ug * (1.0 - sg)))
        for sc, dsc, wr, dref, dwref in ((g_sc, dg_sc, wg, dg_ref, dwg_ref), (v_sc, dv_sc, wv, dv_ref, dwv_ref)):
            acc = jnp.zeros((t, w), f32)
            for k in range(K_FFN):
                acc = acc + wr[pl.ds(k, 1), :] * dsc[pl.ds(K_FFN - 1 - k, t), :]
                dwref[pl.ds(k, 1), :] += _colsum(dsc[pl.ds(0, t), :] * sc[pl.ds(hb - (K_FFN - 1) + k, t), :])
            dref[...] = acc.astype(bf16)

    cur = lambda off: pl.BlockSpec((t, w), lambda j, i: (i, off + j))
    prv = lambda off: pl.BlockSpec((hb, w), lambda j, i: (jnp.maximum(i * (t // hb) - 1, 0), off + j))
    nxt = lambda off: pl.BlockSpec((hb, w), lambda j, i: (jnp.minimum((i + 1) * (t // hb), s // hb - 1), off + j))
    wsp = lambda off: pl.BlockSpec((K_FFN, w), lambda j, i: (0, off + j))
    half = SDS((s, f2 // 2), bf16)
    dws = SDS((8, f2 // 2), f32)
    o_cur = pl.BlockSpec((t, w), lambda j, i: (i, j))
    o_dw = pl.BlockSpec((8, w), lambda j, i: (0, j))
    return _pcall(body, name=name, out_shape=(half, half, dws, dws), grid=(nj, s // t),
                  in_specs=[cur(0), cur(nj), prv(0), prv(nj), nxt(0), nxt(nj), cur(0), nxt(0), wsp(0), wsp(nj)],
                  out_specs=(o_cur, o_cur, o_dw, o_dw),
                  scratch_shapes=[pltpu.VMEM((hb + t + hb, w), f32), pltpu.VMEM((hb + t + hb, w), f32),
                                  pltpu.VMEM((e, w), f32), pltpu.VMEM((e, w), f32)],
                  compiler_params=_cp(("parallel", "arbitrary")))(upre, upre, upre, upre, upre, upre, dact, dact, conv_f, conv_f)


def _bias_tables(rel_bias, name):
    nb = len(DILATIONS)

    def body(rb_ref, idx_ref, o_ref):
        h = pl.program_id(1)
        iv = idx_ref[0]
        acc = jnp.zeros((BLK, 2 * BLK), f32)
        for b in range(N_BUCKETS):
            acc = jnp.where(iv == b, rb_ref[b, h], acc)
        o_ref[0, 0] = acc

    return _pcall(body, name=name, out_shape=SDS((nb, 8, BLK, 2 * BLK), f32), grid=(nb, 8),
                  in_specs=[pl.BlockSpec(memory_space=pltpu.SMEM), pl.BlockSpec((1, BLK, 2 * BLK), lambda br, h: (br, 0, 0))],
                  out_specs=pl.BlockSpec((1, 1, BLK, 2 * BLK), lambda br, h: (br, h, 0, 0)),
                  compiler_params=_cp(("parallel", "parallel")))(rel_bias, jnp.asarray(_BUCKETS))


def _bias_grad(ds_tabs, name):
    nl, nb = ds_tabs.shape[0], ds_tabs.shape[1]

    def body(t_ref, idx_ref, o_ref):
        br, h = pl.program_id(0), pl.program_id(1)

        @pl.when(jnp.logical_and(br == 0, h == 0))
        def _():
            o_ref[...] = jnp.zeros_like(o_ref)

        tab = t_ref[0, 0, 0]
        for l in range(1, nl):
            tab = tab + t_ref[l, 0, 0]
        iv = idx_ref[0]
        row = lax.broadcasted_iota(jnp.int32, (N_BUCKETS, 128), 0)
        col = lax.broadcasted_iota(jnp.int32, (N_BUCKETS, 128), 1)
        acc = jnp.zeros((N_BUCKETS, 128), f32)
        for b in range(N_BUCKETS):
            sel = jnp.where(iv == b, tab, 0.0)
            tot = jnp.sum(jnp.sum(sel, axis=0, keepdims=True), axis=1, keepdims=True)
            acc = acc + jnp.where(jnp.logical_and(row == b, col == h), tot, 0.0)
        o_ref[...] += acc

    return _pcall(body, name=name, out_shape=SDS((N_BUCKETS, 128), f32), grid=(nb, 8),
                  in_specs=[pl.BlockSpec((nl, 1, 1, BLK, 2 * BLK), lambda br, h: (0, br, h, 0, 0)),
                            pl.BlockSpec((1, BLK, 2 * BLK), lambda br, h: (br, 0, 0))],
                  out_specs=pl.BlockSpec((N_BUCKETS, 128), lambda br, h: (0, 0)),
                  compiler_params=_cp(("arbitrary", "arbitrary")))(ds_tabs, jnp.asarray(_BUCKETS))


def _permute(src, col0, ncol, name):
    s = src.shape[0]

    def body(x_ref, o4_ref, o16_ref):
        for d, o_ref in ((4, o4_ref), (16, o16_ref)):
            sub = s // d
            for r in range(d):
                o_ref[pl.ds(r * sub, sub), :] = x_ref[pl.ds(r, sub, stride=d), :]

    out = SDS((s, ncol * BLK), src.dtype)
    blk = pl.BlockSpec((s, BLK), lambda j: (0, j))
    return _pcall(body, name=name, out_shape=(out, out), grid=(ncol,),
                  in_specs=[pl.BlockSpec((s, BLK), lambda j: (0, col0 + j))], out_specs=(blk, blk),
                  compiler_params=_cp(("parallel",)))(src)


def _band_masks():
    qi = lax.broadcasted_iota(jnp.int32, (2 * BLK, 2 * BLK), 0) % BLK
    kj = lax.broadcasted_iota(jnp.int32, (2 * BLK, 2 * BLK), 1)
    band = jnp.logical_and(kj >= qi, kj <= qi + N_KEYS)
    return band, kj


def _stack_heads(x, head0):
    zero = jnp.zeros_like(x)
    return jnp.concatenate([jnp.where(head0, x, zero), jnp.where(head0, zero, x)], axis=0)


def _attn_fwd(src, offs, bias, branch, name):
    s = src.shape[0]
    nblk = s // BLK
    per_res = nblk // DILATIONS[branch]
    qo, ko, vo = offs

    def body(q_ref, k_ref, v_ref, b_ref, o_ref, l_ref):
        band, kj = _band_masks()
        lane = lax.broadcasted_iota(jnp.int32, (BLK, BLK), 1)
        head0 = lane < D_HEAD

        def step(blk, carry):
            r0 = pl.multiple_of(blk * BLK, BLK)
            p0 = pl.multiple_of(jnp.maximum(blk - 1, 0) * BLK, BLK)
            first_key = jnp.where(blk % per_res == 0, BLK, 0)
            valid = jnp.logical_and(band, kj >= first_key)
            q = (q_ref[pl.ds(r0, BLK), :] * (D_HEAD ** -0.5)).astype(bf16)
            kk = jnp.concatenate([k_ref[pl.ds(p0, BLK), :], k_ref[pl.ds(r0, BLK), :]], axis=0).astype(bf16)
            vv = jnp.concatenate([v_ref[pl.ds(p0, BLK), :], v_ref[pl.ds(r0, BLK), :]], axis=0).astype(bf16)
            sc = lax.dot_general(_stack_heads(q, head0), kk, (((1,), (1,)), ((), ())), preferred_element_type=f32) + b_ref[0, 0]
            sc = jnp.where(valid, sc, NEG)
            m = jnp.max(sc, axis=-1, keepdims=True)
            p = jnp.exp(sc - m)
            den = jnp.sum(p, axis=-1, keepdims=True)
            out = jnp.dot(p.astype(bf16), vv, preferred_element_type=f32) / den
            lse = m + jnp.log(den)
            o_ref[pl.ds(r0, BLK), :] = jnp.where(head0, out[:BLK], out[BLK:])
            l_ref[pl.ds(r0, BLK), :] = jnp.where(head0, lse[:BLK], lse[BLK:])
            return carry

        lax.fori_loop(0, nblk, step, 0, unroll=4)

    col = lambda off: pl.BlockSpec((s, BLK), lambda hp: (0, off + hp))
    out = SDS((s, 4 * BLK), f32)
    return _pcall(body, name=name, out_shape=(out, out), grid=(4,),
                  in_specs=[col(qo), col(ko), col(vo), pl.BlockSpec((1, 1, 2 * BLK, 2 * BLK), lambda hp: (branch, hp, 0, 0))],
                  out_specs=(col(0), col(0)), compiler_params=_cp(("parallel",)))(src, src, src, bias)


def _attn_mix(o1, l1, o2, l2, o3, l3, name):
    s = o1.shape[0]
    chunk = 256

    def body(o1_ref, l1_ref, o2_ref, l2_ref, o3_ref, l3_ref, y_ref, lt_ref, o2n, l2n, o3n, l3n):
        for d, pairs in ((4, ((o2_ref, o2n), (l2_ref, l2n))), (16, ((o3_ref, o3n), (l3_ref, l3n)))):
            sub = s // d
            for src, dst in pairs:
                for r in range(d):
                    dst[pl.ds(r, sub, stride=d), :] = src[pl.ds(r * sub, sub), :]

        def step(c, carry):
            rows = pl.ds(pl.multiple_of(c * chunk, chunk), chunk)
            la, lb, lc = l1_ref[rows, :], l2n[rows, :], l3n[rows, :]
            m = jnp.maximum(jnp.maximum(la, lb), lc)
            ea, eb, ec = jnp.exp(la - m), jnp.exp(lb - m), jnp.exp(lc - m)
            den = ea + eb + ec
            y_ref[rows, :] = (ea * o1_ref[rows, :] + eb * o2n[rows, :] + ec * o3n[rows, :]) / den
            lt_ref[rows, :] = m + jnp.log(den)
            return carry

        lax.fori_loop(0, s // chunk, step, 0)

    col = pl.BlockSpec((s, BLK), lambda hp: (0, hp))
    out = SDS((s, 4 * BLK), f32)
    return _pcall(body, name=name, out_shape=(out, out), grid=(4,), in_specs=[col] * 6, out_specs=(col, col),
                  scratch_shapes=[pltpu.VMEM((s, BLK), f32)] * 4, compiler_params=_cp(("parallel",)))(o1, l1, o2, l2, o3, l3)


def _attn_bwd(src, offs, cat, bias, branch, name):
    s = src.shape[0]
    nblk = s // BLK
    per_res = nblk // DILATIONS[branch]
    qo, ko, vo = offs
    scale = D_HEAD ** -0.5

    def body(q_ref, k_ref, v_ref, dy_ref, y_ref, lt_ref, b_ref, dqkv_ref, ds_ref):
        band, kj = _band_masks()
        lane = lax.broadcasted_iota(jnp.int32, (BLK, BLK), 1)
        head0 = lane < D_HEAD
        dqkv_ref[1] = jnp.zeros((s, BLK), f32)
        dqkv_ref[2] = jnp.zeros((s, BLK), f32)
        ds_ref[...] = jnp.zeros_like(ds_ref)

        def step(blk, carry):
            r0 = pl.multiple_of(blk * BLK, BLK)
            p0 = pl.multiple_of(jnp.maximum(blk - 1, 0) * BLK, BLK)
            first_key = jnp.where(blk % per_res == 0, BLK, 0)
            valid = jnp.logical_and(band, kj >= first_key)
            q = (q_ref[pl.ds(r0, BLK), :] * scale).astype(bf16)
            kk = jnp.concatenate([k_ref[pl.ds(p0, BLK), :], k_ref[pl.ds(r0, BLK), :]], axis=0).astype(bf16)
            vv = jnp.concatenate([v_ref[pl.ds(p0, BLK), :], v_ref[pl.ds(r0, BLK), :]], axis=0).astype(bf16)
            dy = dy_ref[pl.ds(r0, BLK), :]
            lt = lt_ref[pl.ds(r0, BLK), :]
            qs = _stack_heads(q, head0)
            dys = _stack_heads(dy.astype(bf16), head0)
            dd = jnp.sum(_stack_heads(dy * y_ref[pl.ds(r0, BLK), :], head0), axis=-1, keepdims=True)
            lts = jnp.concatenate([lt[:, 0:1], lt[:, D_HEAD:D_HEAD + 1]], axis=0)
            sc = lax.dot_general(qs, kk, (((1,), (1,)), ((), ())), preferred_element_type=f32) + b_ref[0, 0]
            p = jnp.where(valid, jnp.exp(sc - lts), 0.0)
            dp = lax.dot_general(dys, vv, (((1,), (1,)), ((), ())), preferred_element_type=f32)
            ds = p * (dp - dd)
            ds_ref[0] += ds
            dsb = ds.astype(bf16)
            dq = jnp.dot(dsb, kk, preferred_element_type=f32) * scale
            dkk = lax.dot_general(dsb, qs, (((0,), (0,)), ((), ())), preferred_element_type=f32)
            dvv = lax.dot_general(p.astype(bf16), dys, (((0,), (0,)), ((), ())), preferred_element_type=f32)
            dqkv_ref[0, pl.ds(r0, BLK), :] = jnp.where(head0, dq[:BLK], dq[BLK:])
            dqkv_ref[1, pl.ds(p0, BLK), :] += dkk[:BLK]
            dqkv_ref[1, pl.ds(r0, BLK), :] += dkk[BLK:]
            dqkv_ref[2, pl.ds(p0, BLK), :] += dvv[:BLK]
            dqkv_ref[2, pl.ds(r0, BLK), :] += dvv[BLK:]
            return carry

        lax.fori_loop(0, nblk, step, 0, unroll=4)

    col = lambda off: pl.BlockSpec((s, BLK), lambda hp: (0, off + hp))
    tab = lambda br: pl.BlockSpec((1, 1, 2 * BLK, 2 * BLK), lambda hp: (br, hp, 0, 0))
    return _pcall(body, name=name, out_shape=(SDS((3, s, 4 * BLK), f32), SDS((4, 2 * BLK, 2 * BLK), f32)), grid=(4,),
                  in_specs=[col(qo), col(ko), col(vo), col(0), col(4), col(8), tab(branch)],
                  out_specs=(pl.BlockSpec((3, s, BLK), lambda hp: (0, 0, hp)), pl.BlockSpec((1, 2 * BLK, 2 * BLK), lambda hp: (hp, 0, 0))),
                  compiler_params=_cp(("parallel",)))(src, src, src, cat, cat, cat, bias)


def _attn_bwd_sum(d1, d2, d3, name):
    s = d1.shape[1]
    chunk = 256

    def body(d1_ref, d2_ref, d3_ref, o_ref, acc):
        acc[...] = d1_ref[0]
        for d, src in ((4, d2_ref), (16, d3_ref)):
            sub = s // d
            for r in range(d):
                acc[pl.ds(r, sub, stride=d), :] += src[0, pl.ds(r * sub, sub), :]

        def step(c, carry):
            rows = pl.ds(pl.multiple_of(c * chunk, chunk), chunk)
            o_ref[rows, :] = acc[rows, :].astype(bf16)
            return carry

        lax.fori_loop(0, s // chunk, step, 0)

    blk = pl.BlockSpec((1, s, BLK), lambda t, hp: (t, 0, hp))
    return _pcall(body, name=name, out_shape=SDS((s, 12 * BLK), bf16), grid=(3, 4), in_specs=[blk] * 3,
                  out_specs=pl.BlockSpec((s, BLK), lambda t, hp: (0, 4 * t + hp)),
                  scratch_shapes=[pltpu.VMEM((s, BLK), f32)], compiler_params=_cp(("parallel", "parallel")))(d1, d2, d3)


def _adamw_math(w, g, m, v):
    m = ADAM_B1 * m + (1.0 - ADAM_B1) * g
    v = ADAM_B2 * v + (1.0 - ADAM_B2) * (g * g)
    m_hat = m / (1.0 - ADAM_B1 ** ADAM_STEP)
    v_hat = v / (1.0 - ADAM_B2 ** ADAM_STEP)
    delta = -ADAM_LR * (m_hat / (jnp.sqrt(v_hat) + ADAM_EPS) + ADAM_WD * w)
    return delta, m, v


def _adamw_layer(land, send, w, m, v, prev, l, tr, name):
    _, r, c = w.shape
    assert r % tr == 0

    def body(land_ref, send_ref, w_ref, m_ref, v_ref, p0, p1, p2, p3, g_ref, d_ref, nm_ref, nv_ref):
        mx, my, mc = _my_place()
        me = jnp.zeros((tr, c), jnp.int32) + (4 * mx + 2 * my + mc)
        g = jnp.zeros((tr, c), f32)
        for j in range(N_DEV):
            g = g + jnp.where(me == j, send_ref[j], land_ref[j]).astype(f32)
        delta, nm, nv = _adamw_math(w_ref[...], g, m_ref[...], v_ref[...])
        g_ref[...] = g
        d_ref[...] = delta
        nm_ref[...] = nm
        nv_ref[...] = nv

    parts = pl.BlockSpec((N_DEV, tr, c), lambda i: (0, i, 0))
    row = pl.BlockSpec((None, tr, c), lambda i: (l, i, 0))
    anywhere = pl.BlockSpec(memory_space=pl.ANY)
    out = SDS(w.shape, f32)
    return _pcall(body, name=name, out_shape=(out, out, out, out), grid=(r // tr,),
                  in_specs=[parts, parts, row, row, row] + [anywhere] * 4, out_specs=(row, row, row, row),
                  input_output_aliases={5: 0, 6: 1, 7: 2, 8: 3}, compiler_params=_cp(("parallel",)))(land, send, w, m, v, *prev)


def _sum_parts(parts, name):
    _, r, c = parts.shape

    def body(p_ref, o_ref):
        g = p_ref[0]
        for j in range(1, N_DEV):
            g = g + p_ref[j]
        o_ref[...] = g

    return _pcall(body, name=name, out_shape=SDS((r, c), f32), compiler_params=_cp())(parts)


def _adamw_small(g, w, m, v, name):
    def body(g_ref, w_ref, m_ref, v_ref, d_ref, nm_ref, nv_ref):
        delta, nm, nv = _adamw_math(w_ref[...], g_ref[...], m_ref[...], v_ref[...])
        d_ref[...] = delta
        nm_ref[...] = nm
        nv_ref[...] = nv

    out = SDS(w.shape, f32)
    return _pcall(body, name=name, out_shape=(out, out, out), compiler_params=_cp())(g, w, m, v)


def _my_place():
    return lax.axis_index("x"), lax.axis_index("y"), lax.axis_index("c")


def _all_gather(x, name):
    r, c = x.shape

    def body(x_ref, out_ref, send_sems, recv_sems, local_sem):
        mx, my, mc = _my_place()
        me, sibling = (mx, my, mc), (mx, my, 1 - mc)
        chips = [(1 - mx, my), (mx, 1 - my), (1 - mx, 1 - my)]

        def rows(px, py, pc):
            return out_ref.at[4 * px + 2 * py + pc]

        def copy(k, block, to, src=None):
            return pltpu.make_async_remote_copy(src_ref=rows(*block) if src is None else src, dst_ref=rows(*block),
                                                send_sem=send_sems.at[k], recv_sem=recv_sems.at[k], device_id=to, device_id_type=MESH)

        mine = pltpu.make_async_copy(x_ref, rows(*me), local_sem)
        mine.start()
        first = [copy(0, me, sibling, src=x_ref)] + [copy(1 + j, me, (*chip, mc), src=x_ref) for j, chip in enumerate(chips)]
        for cp in first:
            cp.start()
        passed = [copy(4 + j, (*chip, mc), sibling) for j, chip in enumerate(chips)]
        for j, chip in enumerate(chips):
            copy(1 + j, (*chip, mc), me).wait_recv()
            passed[j].start()
        copy(0, sibling, me).wait_recv()
        for j, chip in enumerate(chips):
            copy(4 + j, (*chip, 1 - mc), me).wait_recv()
        for cp in first + passed:
            cp.wait_send()
        mine.wait()

    return _pcall(body, name=name, out_shape=SDS((N_DEV, r, c), x.dtype),
                  in_specs=[pl.BlockSpec(memory_space=pl.ANY)], out_specs=pl.BlockSpec(memory_space=pl.ANY),
                  scratch_shapes=[pltpu.SemaphoreType.DMA((7,)), pltpu.SemaphoreType.DMA((7,)), pltpu.SemaphoreType.DMA(())])(x)


HBM_SPEC = pl.BlockSpec(memory_space=pltpu.HBM)
SEM_SPEC = pl.BlockSpec(memory_space=pltpu.SEMAPHORE)
ANY_SPEC = pl.BlockSpec(memory_space=pl.ANY)
DATAFLOW = pltpu.SideEffectType.DATAFLOW_SIDE_EFFECTING


def _peers():
    mx, my, mc = _my_place()
    out = []
    for k in range(1, N_DEV):
        tx, ty, tc = mx ^ ((k >> 2) & 1), my ^ ((k >> 1) & 1), mc ^ (k & 1)
        out.append(((tx, ty, tc), 4 * tx + 2 * ty + tc))
    return out


def _in_hbm(arrays):
    return [pltpu.with_memory_space_constraint(a, pltpu.HBM) for a in arrays]


def _gather_start(groups, after, name):
    ng = len(groups)
    flat = [b for grp in groups for b in grp]
    n = len(flat)

    def body(*refs):
        ins, sems, token = refs[:n], refs[n + 1:n + 1 + 2 * ng], refs[-1]
        mx, my, mc = _my_place()
        me = 4 * mx + 2 * my + mc
        peers = _peers()
        at = 0
        for g, grp in enumerate(groups):
            for t in range(len(grp)):
                buf = ins[at]
                at += 1
                for dev, _ in peers:
                    pltpu.make_async_remote_copy(src_ref=buf.at[me], dst_ref=buf.at[me], send_sem=sems[2 * g].at[t],
                                                 recv_sem=sems[2 * g + 1].at[t], device_id=dev, device_id_type=MESH).start()
        token[...] = jnp.zeros_like(token)

    sem_shapes = tuple(pltpu.SemaphoreType.DMA((len(grp),)) for grp in groups for _ in range(2))
    out_shape = sem_shapes + tuple(pltpu.HBM(b.shape, b.dtype) for b in flat) + (SDS((8, 128), f32),)
    out = _pcall(body, name=name, out_shape=out_shape, in_specs=(HBM_SPEC,) * n + (ANY_SPEC,),
                 out_specs=(SEM_SPEC,) * (2 * ng) + (HBM_SPEC,) * n + (pl.BlockSpec(memory_space=pltpu.VMEM),),
                 input_output_aliases={i: 2 * ng + i for i in range(n)},
                 compiler_params=pltpu.CompilerParams(has_side_effects=DATAFLOW))(*_in_hbm(flat), after)
    sems = [(out[2 * g], out[2 * g + 1]) for g in range(ng)]
    thru, at = [], 2 * ng
    for grp in groups:
        thru.append(list(out[at:at + len(grp)]))
        at += len(grp)
    return sems, thru, out[-1]


def _exchange_start(sends, lands, name):
    nt = len(sends)

    def body(*refs):
        srcs, dsts, s_sem, r_sem, token = refs[:nt], refs[nt:2 * nt], refs[2 * nt], refs[2 * nt + 1], refs[-1]
        mx, my, mc = _my_place()
        me = 4 * mx + 2 * my + mc
        for t in range(nt):
            for dev, idx in _peers():
                pltpu.make_async_remote_copy(src_ref=srcs[t].at[idx], dst_ref=dsts[t].at[me], send_sem=s_sem.at[t],
                                             recv_sem=r_sem.at[t], device_id=dev, device_id_type=MESH).start()
        token[...] = jnp.zeros_like(token)

    both = list(sends) + list(lands)
    out_shape = (pltpu.SemaphoreType.DMA((nt,)), pltpu.SemaphoreType.DMA((nt,))) + tuple(pltpu.HBM(b.shape, b.dtype) for b in both)
    out = _pcall(body, name=name, out_shape=out_shape + (SDS((8, 128), f32),), in_specs=(HBM_SPEC,) * (2 * nt),
                 out_specs=(SEM_SPEC, SEM_SPEC) + (HBM_SPEC,) * (2 * nt) + (pl.BlockSpec(memory_space=pltpu.VMEM),),
                 input_output_aliases={i: 2 + i for i in range(2 * nt)},
                 compiler_params=pltpu.CompilerParams(has_side_effects=DATAFLOW))(*_in_hbm(both))
    return (out[0], out[1]), list(out[2:2 + nt]), list(out[2 + nt:2 + 2 * nt]), out[-1]


def _copies_wait(srcs, dsts, sems, after, name):
    nt = len(srcs)
    same = srcs is dsts
    bufs = list(srcs) if same else list(srcs) + list(dsts)
    nb = len(bufs)

    def body(*refs):
        s_sem, r_sem = refs[nb], refs[nb + 1]
        mx, my, mc = _my_place()
        for t in range(nt):
            src = refs[t].at[pl.ds(0, N_DEV - 1)]
            dst = src if same else refs[nt + t].at[pl.ds(0, N_DEV - 1)]
            cp = pltpu.make_async_remote_copy(src_ref=src, dst_ref=dst, send_sem=s_sem.at[t], recv_sem=r_sem.at[t],
                                              device_id=(mx, my, 1 - mc), device_id_type=MESH)
            cp.wait_send()
            cp.wait_recv()

    out = _pcall(body, name=name, out_shape=tuple(pltpu.HBM(b.shape, b.dtype) for b in bufs),
                 in_specs=(HBM_SPEC,) * nb + (SEM_SPEC, SEM_SPEC, ANY_SPEC), out_specs=(HBM_SPEC,) * nb,
                 input_output_aliases={i: i for i in range(nb)},
                 compiler_params=pltpu.CompilerParams(has_side_effects=DATAFLOW))(*bufs, sems[0], sems[1], after)
    out = list(out)
    return (out, out) if same else (out[:nt], out[nt:])


def _cast_blocks(weights, l, me, name):
    nt = len(weights)
    halves = 2

    def body(me_ref, *refs):
        for t in range(nt):
            refs[nt + t][...] = refs[t][...].astype(bf16)

    in_specs, out_specs, out_shape = [], [], []
    for wt in weights:
        _, r, c = wt.shape
        tr = r // halves
        in_specs.append(pl.BlockSpec((None, tr, c), lambda i, me_ref: (l, i, 0)))
        out_specs.append(pl.BlockSpec((None, tr, c), lambda i, me_ref: (me_ref[0], i, 0)))
        out_shape.append(SDS((N_DEV, r, c), bf16))
    grid_spec = pltpu.PrefetchScalarGridSpec(num_scalar_prefetch=1, grid=(halves,), in_specs=in_specs, out_specs=out_specs)
    return list(_pcall(body, name=name, out_shape=out_shape, grid_spec=grid_spec, compiler_params=_cp(("parallel",)))(me, *weights))

def _layer_fwd_mix(x, p, bias, l, after=None):
    tag = f"l{l}"
    h = _norm_cast(x, p["norm_mix_g"], f"{tag}_norm_mix")
    z = _mm(h, p["w_in"], "nt", 512, 1408, f"{tag}_mm_in", after=after, b_outer=True)
    qkv4, qkv16 = _permute(z, Z_Q128, 12, f"{tag}_perm_qkv")
    o1, l1 = _attn_fwd(z, (Z_Q128, Z_K128, Z_V128), bias, 0, f"{tag}_attn_fwd0")
    o2, l2 = _attn_fwd(qkv4, (0, 4, 8), bias, 1, f"{tag}_attn_fwd1")
    o3, l3 = _attn_fwd(qkv16, (0, 4, 8), bias, 2, f"{tag}_attn_fwd2")
    yb, lt = _attn_mix(o1, l1, o2, l2, o3, l3, f"{tag}_attn_mix")
    y = _mixer_fwd(z, yb, p["conv_a_w"], p["conv_c_w"], p["conv_c_b"], p["ln_c_g"], p["ln_c_b"], p["out_norm_g"], f"{tag}_mixer_fwd")
    return y, dict(x=x, h=h, z=z, qkv4=qkv4, qkv16=qkv16, yb=yb, lt=lt, y=y)


def _layer_fwd_ffn(x, y, p, l, after=None):
    tag = f"l{l}"
    x1 = _mm(y, p["w_out"], "nn", 512, 1024, f"{tag}_mm_out", res=x, after=after)
    h2 = _norm_cast(x1, p["norm_ffn_g"], f"{tag}_norm_ffn")
    upre = _mm(h2, p["w_up"], "nt", 512, 1408, f"{tag}_mm_up", b_outer=True)
    act = _ffn_act_fwd(upre, p["conv_f_w"], f"{tag}_ffn_act")
    x2 = _mm(act, p["w_down"], "nn", 512, 1024, f"{tag}_mm_down", res=x1)
    return x2, dict(x1=x1, h2=h2, upre=upre, act=act)


def _layer_bwd_ffn(dx2, p, sv, l, after=None):
    tag = f"l{l}"
    g = {}
    dact = _mm(dx2, p["w_down"], "nt", 512, 1408, f"{tag}_mm_dact", after=after, b_outer=True)
    dwd = _mm(sv["act"], dx2, "tn", 1408, 1024, f"{tag}_mm_dwdown", out_dtype=bf16)
    g["w_down"] = dwd.reshape(N_DEV, dwd.shape[0] // N_DEV, dwd.shape[1])
    dug, duv, dwg, dwv = _ffn_act_bwd(sv["upre"], dact, p["conv_f_w"], f"{tag}_ffn_act_bwd")
    g["conv_f_w"] = jnp.concatenate([dwg[:K_FFN], dwv[:K_FFN]], axis=1)
    dupre = jnp.concatenate([dug, duv], axis=1)
    dh2 = _mm(dupre, p["w_up"], "nn", 512, 1024, f"{tag}_mm_dh2")
    dwu = _mm(dupre, sv["h2"], "tn", 1408, 1024, f"{tag}_mm_dwup", out_dtype=bf16)
    g["w_up"] = dwu.reshape(N_DEV, dwu.shape[0] // N_DEV, dwu.shape[1])
    dx1, g["norm_ffn_g"] = _rms_bwd(sv["x1"], p["norm_ffn_g"], dh2, dx2, f"{tag}_norm_ffn_bwd")
    return dx1, g


def _layer_bwd_mix(dx1, p, sv, bias, l, after=None):
    tag = f"l{l}"
    g = {}
    dy = _mm(dx1, p["w_out"], "nt", 512, 1024, f"{tag}_mm_dy", after=after)
    dwo = _mm(sv["y"], dx1, "tn", 512, 1024, f"{tag}_mm_dwout", out_dtype=bf16)
    g["w_out"] = dwo.reshape(N_DEV, dwo.shape[0] // N_DEV, dwo.shape[1])
    dza, dzc, dyb, pg, dgb = _mixer_bwd(sv["z"], sv["yb"], dy, p["conv_a_w"], p["conv_c_w"], p["conv_c_b"], p["ln_c_g"],
                                        p["ln_c_b"], p["out_norm_g"], f"{tag}_mixer_bwd")
    g["conv_a_w"] = pg[PG_CONV_A:PG_CONV_A + K_SHORT]
    g["conv_c_w"] = pg[PG_CONV_C:PG_CONV_C + K_CONF]
    g["conv_c_b"], g["ln_c_g"], g["ln_c_b"] = pg[PG_CB:PG_CB + 1], pg[PG_LNG:PG_LNG + 1], pg[PG_LNB:PG_LNB + 1]
    g["out_norm_g"] = jnp.concatenate([pg[PG_OG_A:PG_OG_A + 1], dgb, pg[PG_OG_C:PG_OG_C + 1]], axis=1)
    cat = jnp.concatenate([dyb, sv["yb"], sv["lt"]], axis=1)
    cat4, cat16 = _permute(cat, 0, 12, f"{tag}_perm_cat")
    d1, t1 = _attn_bwd(sv["z"], (Z_Q128, Z_K128, Z_V128), cat, bias, 0, f"{tag}_attn_bwd0")
    d2, t2 = _attn_bwd(sv["qkv4"], (0, 4, 8), cat4, bias, 1, f"{tag}_attn_bwd1")
    d3, t3 = _attn_bwd(sv["qkv16"], (0, 4, 8), cat16, bias, 2, f"{tag}_attn_bwd2")
    dqkv = _attn_bwd_sum(d1, d2, d3, f"{tag}_attn_bwd_sum")
    dz = jnp.concatenate([dza, dqkv, dzc], axis=1)
    dh = _mm(dz, p["w_in"], "nn", 512, 1024, f"{tag}_mm_dh")
    dwi = _mm(dz, sv["h"], "tn", 1408, 1024, f"{tag}_mm_dwin", out_dtype=bf16)
    g["w_in"] = dwi.reshape(N_DEV, dwi.shape[0] // N_DEV, dwi.shape[1])
    dx, g["norm_mix_g"] = _rms_bwd(sv["x"], p["norm_mix_g"], dh, dx1, f"{tag}_norm_mix_bwd")
    return dx, g, jnp.stack([t1, t2, t3])


SMALL_LAYER_PARAMS = ("norm_mix_g", "conv_a_w", "conv_c_w", "conv_c_b", "ln_c_g", "ln_c_b", "out_norm_g", "norm_ffn_g", "conv_f_w")


BIG = ("w_in", "w_out", "w_up", "w_down")
TRANSPOSED = ("w_in", "w_up")
ADAM_ROWS = {"w_in": 176, "w_out": 128, "w_up": 176, "w_down": 176}
SMALL_SHARDED = ("conv_a_w", "conv_c_w", "conv_f_w")
SMALL_REPLICATED = ("norm_mix_g", "conv_c_b", "ln_c_g", "ln_c_b", "out_norm_g", "norm_ffn_g", "rel_bias", "final_g")
WEIGHTS = ("norm_mix_g", "w_in", "conv_a_w", "conv_c_w", "conv_c_b", "ln_c_g", "ln_c_b", "out_norm_g", "w_out",
           "norm_ffn_g", "w_up", "conv_f_w", "w_down", "rel_bias", "final_g")


def _pack(arrays):
    flat = jnp.concatenate([a.reshape(-1) for a in arrays])
    rows = -(-flat.shape[0] // 1024) * 8
    return jnp.pad(flat, (0, rows * 128 - flat.shape[0])).reshape(rows, 128)


def _unpack(packed, shapes):
    flat, out, at = packed.reshape(-1), [], 0
    for shp in shapes:
        n = int(np.prod(shp))
        out.append(flat[at:at + n].reshape(shp))
        at += n
    return out


def _gather_shards(g, local_shape):
    nd = len(local_shape)
    return jnp.moveaxis(g, 0, nd - 1).reshape(local_shape[:-1] + (N_DEV * local_shape[-1],))


def kernel(x, norm_mix_g, w_in, conv_a_w, conv_c_w, conv_c_b, ln_c_g, ln_c_b, out_norm_g, w_out, norm_ffn_g, w_up, conv_f_w, w_down, rel_bias, final_g, loss_target, m_norm_mix_g, m_w_in, m_conv_a_w, m_conv_c_w, m_conv_c_b, m_ln_c_g, m_ln_c_b, m_out_norm_g, m_w_out, m_norm_ffn_g, m_w_up, m_conv_f_w, m_w_down, m_rel_bias, m_final_g, v_norm_mix_g, v_w_in, v_conv_a_w, v_conv_c_w, v_conv_c_b, v_ln_c_g, v_ln_c_b, v_out_norm_g, v_w_out, v_norm_ffn_g, v_w_up, v_conv_f_w, v_w_down, v_rel_bias, v_final_g):
    w = dict(norm_mix_g=norm_mix_g, w_in=w_in, conv_a_w=conv_a_w, conv_c_w=conv_c_w, conv_c_b=conv_c_b, ln_c_g=ln_c_g,
             ln_c_b=ln_c_b, out_norm_g=out_norm_g, w_out=w_out, norm_ffn_g=norm_ffn_g, w_up=w_up, conv_f_w=conv_f_w,
             w_down=w_down, rel_bias=rel_bias, final_g=final_g)
    mom = dict(norm_mix_g=m_norm_mix_g, w_in=m_w_in, conv_a_w=m_conv_a_w, conv_c_w=m_conv_c_w, conv_c_b=m_conv_c_b,
               ln_c_g=m_ln_c_g, ln_c_b=m_ln_c_b, out_norm_g=m_out_norm_g, w_out=m_w_out, norm_ffn_g=m_norm_ffn_g,
               w_up=m_w_up, conv_f_w=m_conv_f_w, w_down=m_w_down, rel_bias=m_rel_bias, final_g=m_final_g)
    var = dict(norm_mix_g=v_norm_mix_g, w_in=v_w_in, conv_a_w=v_conv_a_w, conv_c_w=v_conv_c_w, conv_c_b=v_conv_c_b,
               ln_c_g=v_ln_c_g, ln_c_b=v_ln_c_b, out_norm_g=v_out_norm_g, w_out=v_w_out, norm_ffn_g=v_norm_ffn_g,
               w_up=v_w_up, conv_f_w=v_conv_f_w, w_down=v_w_down, rel_bias=v_rel_bias, final_g=v_final_g)
    depth = w_in.shape[0]
    me = 4 * lax.axis_index("x") + 2 * lax.axis_index("y") + lax.axis_index("c")
    for k in TRANSPOSED:
        w[k], mom[k], var[k] = (jnp.transpose(a, (0, 2, 1)) for a in (w[k], mom[k], var[k]))

    params = {}
    small_shapes = [w[k].shape for k in SMALL_SHARDED]
    gs = _all_gather(_pack([w[k] for k in SMALL_SHARDED]), "gather_small")
    parts = [_unpack(gs[j], small_shapes) for j in range(N_DEV)]
    for n, k in enumerate(SMALL_SHARDED):
        params[k] = _gather_shards(jnp.stack([parts[j][n] for j in range(N_DEV)]), w[k].shape)
    for k in ("norm_mix_g", "conv_c_b", "ln_c_g", "ln_c_b", "out_norm_g", "norm_ffn_g"):
        params[k] = w[k][:, None, :]
    params["rel_bias"] = rel_bias
    params["final_g"] = final_g[None, :]

    me_arr = me.astype(jnp.int32).reshape(1)
    bufs = [_cast_blocks([w[k] for k in BIG], l, me_arr, f"l{l}_cast_blocks") for l in range(depth)]

    def full(k, buf, l):
        return buf.reshape(N_DEV * buf.shape[1], buf.shape[2])

    bias = _bias_tables(rel_bias, "bias_tables").reshape(len(DILATIONS), 4, 2 * BLK, 2 * BLK)
    (in_sems,), (in_bufs,), token = _gather_start([bufs[0][:1]], gs, "l0_gather_start_in")
    xl, after = x[0], token
    layer_params, saved = [], []
    for l in range(depth):
        _, in_bufs = _copies_wait(in_bufs, in_bufs, in_sems, after, f"l{l}_gather_wait_in")
        token = None
        if l == 0:
            (rest_sems,), (rest_bufs,), token = _gather_start([bufs[0][1:]], in_bufs[0], "l0_gather_start_rest")
        p = {k: params[k][l] for k in SMALL_LAYER_PARAMS}
        p["w_in"] = full("w_in", in_bufs[0], l)
        y, sv = _layer_fwd_mix(xl, p, bias, l, after=token)
        _, rest_bufs = _copies_wait(rest_bufs, rest_bufs, rest_sems, y, f"l{l}_gather_wait_rest")
        for k, buf in zip(BIG[1:], rest_bufs):
            p[k] = full(k, buf, l)
        token = None
        if l + 1 < depth:
            (in_sems, rest_sems), (in_bufs, rest_bufs), token = _gather_start([bufs[l + 1][:1], bufs[l + 1][1:]], rest_bufs[0],
                                                                               f"l{l + 1}_gather_start")
        xl, sv2 = _layer_fwd_ffn(xl, y, p, l, after=token)
        layer_params.append(p)
        saved.append({**sv, **sv2})
        after = xl
    loss, dx, d_final = _final_loss(xl, params["final_g"], loss_target[0], "final_loss")
    loss = lax.psum(loss[0, 0], ("x", "y", "c"))

    grads, tabs, pending = [None] * depth, [None] * depth, []
    after = None
    for l in reversed(range(depth)):
        dx1, g_ffn = _layer_bwd_ffn(dx, layer_params[l], saved[l], l, after=after)
        names = ("w_up", "w_down")
        sends = [g_ffn[k] for k in names]
        sems, sends, lands, after = _exchange_start(sends, [lax.empty(s.shape, s.dtype) for s in sends], f"l{l}_exchange_start_ffn")
        pending.append((l, "ffn", names, sems, sends, lands))
        dx, g_mix, tabs[l] = _layer_bwd_mix(dx1, layer_params[l], saved[l], bias, l, after=after)
        names = ("w_out", "w_in")
        sends = [g_mix[k] for k in names]
        sems, sends, lands, after = _exchange_start(sends, [lax.empty(s.shape, s.dtype) for s in sends], f"l{l}_exchange_start_mix")
        pending.append((l, "mix", names, sems, sends, lands))
        grads[l] = {**g_ffn, **g_mix}
    grads_small = {k: jnp.stack([grads[l][k] for l in range(depth)]) for k in SMALL_LAYER_PARAMS}
    grads_small["rel_bias"] = _bias_grad(jnp.stack(tabs).reshape(depth, len(DILATIONS), 8, BLK, 2 * BLK), "bias_grad")[:, :8]
    grads_small["final_g"] = d_final

    out_g, out_d, out_m, out_v = {}, {}, {}, {}
    results = {k: [lax.empty(w[k].shape, f32) for _ in range(4)] for k in BIG}
    after = dx
    for l, part, names, sems, sends, lands in pending:
        sends, lands = _copies_wait(sends, lands, sems, after, f"l{l}_exchange_wait_{part}")
        for k, send, land in zip(names, sends, lands):
            results[k] = _adamw_layer(land, send, w[k], mom[k], var[k], results[k], l, ADAM_ROWS[k], f"l{l}_adamw_{k}")
            after = results[k][0]
    for k in BIG:
        res = [jnp.transpose(a, (0, 2, 1)) for a in results[k]] if k in TRANSPOSED else results[k]
        out_g[k], out_d[k], out_m[k], out_v[k] = res
    grads = grads_small

    small = SMALL_SHARDED + SMALL_REPLICATED
    full_shapes = [params[k].shape if k in SMALL_SHARDED else w[k].shape for k in small]
    gsum = _sum_parts(_all_gather(_pack([grads[k].reshape(shp) for k, shp in zip(small, full_shapes)]), "gather_small_grads"),
                      "sum_small_grads")
    gfull = dict(zip(small, _unpack(gsum, full_shapes)))
    for k in SMALL_SHARDED:
        shp = w[k].shape
        gfull[k] = lax.dynamic_index_in_dim(gfull[k].reshape(shp[:-1] + (N_DEV, shp[-1])), me, axis=len(shp) - 1, keepdims=False)
    local_shapes = [w[k].shape for k in small]
    res = _adamw_small(_pack([gfull[k] for k in small]), _pack([w[k] for k in small]), _pack([mom[k] for k in small]),
                       _pack([var[k] for k in small]), "adamw_small")
    for name_map, packed in zip((out_d, out_m, out_v), res):
        name_map.update(zip(small, _unpack(packed, local_shapes)))
    out_g.update(gfull)

    return (loss, dx[None], *[out_g[k] for k in WEIGHTS], *[out_d[k] for k in WEIGHTS], *[out_m[k] for k in WEIGHTS],
            *[out_v[k] for k in WEIGHTS])
```

```python
import math

import numpy as np
import jax
import jax.numpy as jnp
from jax import lax
from jax.experimental import pallas as pl
from jax.experimental.pallas import tpu as pltpu

f32, bf16 = jnp.float32, jnp.bfloat16
SDS = jax.ShapeDtypeStruct
MESH = pl.DeviceIdType.MESH

N_DEV = 8
EPS = 1e-6
NEG = -1e30
D_HEAD = 64
BLK = 128
N_BUCKETS = 32
DILATIONS = (1, 4, 16)
N_KEYS = 128
K_SHORT, K_CONF, K_FFN = 3, 31, 3
ROW_TILE = 256
HALO = 32
HALO_FFN = 8
VMEM_LIMIT = 56 << 20
ADAM_LR, ADAM_B1, ADAM_B2, ADAM_EPS, ADAM_WD, ADAM_STEP = 0.001, 0.9, 0.999, 1e-08, 0.01, 10


def _pcall(body, **kw):
    return pl.pallas_call(body, **kw)


def _cp(sem=None):
    kw = dict(vmem_limit_bytes=VMEM_LIMIT)
    if sem is not None:
        kw["dimension_semantics"] = sem
    return pltpu.CompilerParams(**kw)


def _t5_buckets():
    rel = np.arange(BLK)[:, None] - np.arange(2 * BLK)[None, :] + BLK
    out = []
    for d in DILATIONS:
        dist = np.maximum(rel, 0) * d
        max_exact = N_BUCKETS // 2
        d_f = np.maximum(dist, 1).astype(np.float32)
        large = max_exact + (np.log(d_f / np.float32(max_exact)) / np.float32(math.log(2048 / max_exact))
                             * np.float32(N_BUCKETS - max_exact)).astype(np.int32)
        large = np.minimum(large, N_BUCKETS - 1)
        out.append(np.where(dist < max_exact, dist, large).astype(np.int32))
    return np.stack(out)


_BUCKETS = _t5_buckets()


def _mm(a, b, mode, tr, tc, name, res=None, out_dtype=f32, after=None, b_outer=False):
    ij = (lambda f: (lambda j, i: f(i, j))) if b_outer else (lambda f: f)
    if mode == "nn":
        (m, k), n = a.shape, b.shape[1]
        a_spec = pl.BlockSpec((tr, k), ij(lambda i, j: (i, 0)))
        b_spec = pl.BlockSpec((k, tc), ij(lambda i, j: (0, j)))
        dims = (((1,), (0,)), ((), ()))
    elif mode == "nt":
        (m, k), n = a.shape, b.shape[0]
        a_spec = pl.BlockSpec((tr, k), ij(lambda i, j: (i, 0)))
        b_spec = pl.BlockSpec((tc, k), ij(lambda i, j: (j, 0)))
        dims = (((1,), (1,)), ((), ()))
    else:
        (k, m), n = a.shape, b.shape[1]
        a_spec = pl.BlockSpec((k, tr), ij(lambda i, j: (0, i)))
        b_spec = pl.BlockSpec((k, tc), ij(lambda i, j: (0, j)))
        dims = (((0,), (0,)), ((), ()))
    assert m % tr == 0 and n % tc == 0, (name, m, n, tr, tc)
    o_spec = pl.BlockSpec((tr, tc), ij(lambda i, j: (i, j)))
    grid = (n // tc, m // tr) if b_outer else (m // tr, n // tc)

    def body(a_ref, b_ref, *rest):
        o_ref = rest[-1]
        acc = lax.dot_general(a_ref[...].astype(bf16), b_ref[...].astype(bf16), dims, preferred_element_type=f32)
        if res is not None:
            acc = acc + rest[0][...]
        o_ref[...] = acc.astype(out_dtype)

    ins = [a, b] + ([res] if res is not None else []) + ([after] if after is not None else [])
    specs = [a_spec, b_spec] + ([o_spec] if res is not None else []) + ([pl.BlockSpec(memory_space=pl.ANY)] if after is not None else [])
    return _pcall(body, name=name, out_shape=SDS((m, n), out_dtype), grid=grid, in_specs=specs,
                  out_specs=o_spec, compiler_params=_cp(("parallel", "parallel")))(*ins)


def _rstd(x):
    return lax.rsqrt(jnp.mean(x * x, axis=-1, keepdims=True) + EPS)


def _rms_dx(x, r, g, dy):
    dxh = dy * g
    return r * (dxh - x * (r * r) * jnp.mean(dxh * x, axis=-1, keepdims=True))


def _sigmoid(x):
    return 1.0 / (1.0 + jnp.exp(-x))


def _colsum(x):
    return jnp.sum(x, axis=0, keepdims=True)


def _norm_cast(x, g, name):
    s, d = x.shape

    def body(x_ref, g_ref, o_ref):
        xv = x_ref[...]
        o_ref[...] = (xv * _rstd(xv) * g_ref[...]).astype(bf16)

    return _pcall(body, name=name, out_shape=SDS((s, d), bf16), grid=(s // ROW_TILE,),
                  in_specs=[pl.BlockSpec((ROW_TILE, d), lambda i: (i, 0)), pl.BlockSpec((1, d), lambda i: (0, 0))],
                  out_specs=pl.BlockSpec((ROW_TILE, d), lambda i: (i, 0)), compiler_params=_cp(("parallel",)))(x, g)


def _rms_bwd(x, g, dh, dres, name):
    s, d = x.shape

    def body(x_ref, g_ref, dh_ref, dres_ref, dx_ref, dg_ref):
        xv, dhv = x_ref[...], dh_ref[...]
        r = _rstd(xv)
        dx_ref[...] = dres_ref[...] + _rms_dx(xv, r, g_ref[...], dhv)

        @pl.when(pl.program_id(0) == 0)
        def _():
            dg_ref[...] = jnp.zeros_like(dg_ref)

        dg_ref[...] += _colsum(dhv * xv * r)

    row = pl.BlockSpec((ROW_TILE, d), lambda i: (i, 0))
    vec = pl.BlockSpec((1, d), lambda i: (0, 0))
    return _pcall(body, name=name, out_shape=(SDS((s, d), f32), SDS((1, d), f32)), grid=(s // ROW_TILE,),
                  in_specs=[row, vec, row, row], out_specs=(row, vec), compiler_params=_cp(("arbitrary",)))(x, g, dh, dres)


def _final_loss(x, g, target, name):
    s, d = x.shape

    def body(x_ref, g_ref, t_ref, loss_ref, dx_ref, dg_ref):
        xv, gv = x_ref[...], g_ref[...]
        r = _rstd(xv)
        diff = xv * r * gv - t_ref[...]
        dy = diff * (1.0 / d)

        @pl.when(pl.program_id(0) == 0)
        def _():
            dg_ref[...] = jnp.zeros_like(dg_ref)
            loss_ref[...] = jnp.zeros_like(loss_ref)

        part = jnp.sum(jnp.sum(diff * diff, axis=1, keepdims=True), axis=0, keepdims=True) * (0.5 / d)
        loss_ref[...] += part
        dx_ref[...] = _rms_dx(xv, r, gv, dy)
        dg_ref[...] += _colsum(dy * xv * r)

    row = pl.BlockSpec((ROW_TILE, d), lambda i: (i, 0))
    vec = pl.BlockSpec((1, d), lambda i: (0, 0))
    one = pl.BlockSpec((8, 128), lambda i: (0, 0))
    return _pcall(body, name=name, out_shape=(SDS((8, 128), f32), SDS((s, d), f32), SDS((1, d), f32)),
                  grid=(s // ROW_TILE,), in_specs=[row, vec, row], out_specs=(one, row, vec),
                  compiler_params=_cp(("arbitrary",)))(x, g, target)


Z_AH, Z_AB, Z_AC, Z_CV, Z_CG = 0, 1, 2, 9, 10
Z_Q128, Z_K128, Z_V128 = 6, 10, 14


def _cur(t, w, col):
    return pl.BlockSpec((t, w), lambda i: (i, col))


def _prev(hb, t, w, col):
    return pl.BlockSpec((hb, w), lambda i: (jnp.maximum(i * (t // hb) - 1, 0), col))


def _next(hb, t, w, col, rows):
    return pl.BlockSpec((hb, w), lambda i: (jnp.minimum((i + 1) * (t // hb), rows // hb - 1), col))


def _layernorm_parts(u, g, b):
    mu = jnp.mean(u, axis=-1, keepdims=True)
    var = jnp.mean(jnp.square(u - mu), axis=-1, keepdims=True)
    rs = lax.rsqrt(var + EPS)
    xhat = (u - mu) * rs
    return xhat, rs, xhat * g + b


def _mixer_fwd(z, yb, conv_a, conv_c, cb, lng, lnb, og, name):
    s = z.shape[0]
    t, hb, w = ROW_TILE, HALO, 256

    def body(ah, ab, ac, cv, cg, ah_p, ac_p, cv_p, cg_p, yb_ref, wa, wc, cb_ref, lng_ref, lnb_ref, og_ref, y_ref, p_sc, u_sc):
        first = pl.program_id(0) == 0

        def prev(ref):
            return jnp.where(first, 0.0, ref[...])

        p_sc[pl.ds(0, hb), :] = prev(ah_p) * prev(ac_p)
        p_sc[pl.ds(hb, t), :] = ah[...] * ac[...]
        conv = jnp.zeros((t, w), f32)
        for k in range(K_SHORT):
            conv = conv + wa[pl.ds(k, 1), :] * p_sc[pl.ds(hb - (K_SHORT - 1) + k, t), :]
        y_a = ab[...] * conv
        u_sc[pl.ds(0, hb), :] = prev(cv_p) * _sigmoid(prev(cg_p))
        u_sc[pl.ds(hb, t), :] = cv[...] * _sigmoid(cg[...])
        u = jnp.zeros((t, w), f32) + cb_ref[...]
        for k in range(K_CONF):
            u = u + wc[pl.ds(k, 1), :] * u_sc[pl.ds(hb - (K_CONF - 1) + k, t), :]
        _, _, lnv = _layernorm_parts(u, lng_ref[...], lnb_ref[...])
        y_c = lnv * _sigmoid(lnv)
        ybv = yb_ref[...]
        y_ref[:, 0:256] = (y_a * _rstd(y_a) * og_ref[:, 0:256]).astype(bf16)
        y_ref[:, 256:768] = (ybv * _rstd(ybv) * og_ref[:, 256:768]).astype(bf16)
        y_ref[:, 768:1024] = (y_c * _rstd(y_c) * og_ref[:, 768:1024]).astype(bf16)

    full = lambda shape: pl.BlockSpec(shape, lambda i: tuple(0 for _ in shape))
    in_specs = [_cur(t, w, c) for c in (Z_AH, Z_AB, Z_AC, Z_CV, Z_CG)] + [_prev(hb, t, w, c) for c in (Z_AH, Z_AC, Z_CV, Z_CG)]
    in_specs += [_cur(t, 512, 0), full((K_SHORT, w)), full((K_CONF, w)), full((1, w)), full((1, w)), full((1, w)), full((1, 1024))]
    return _pcall(body, name=name, out_shape=SDS((s, 1024), bf16), grid=(s // t,), in_specs=in_specs,
                  out_specs=_cur(t, 1024, 0), scratch_shapes=[pltpu.VMEM((hb + t, w), f32), pltpu.VMEM((hb + t, w), f32)],
                  compiler_params=_cp(("parallel",)))(z, z, z, z, z, z, z, z, z, yb, conv_a, conv_c, cb, lng, lnb, og)


PG_CONV_A, PG_CONV_C, PG_CB, PG_LNG, PG_LNB, PG_OG_A, PG_OG_C, PG_ROWS = 0, 8, 40, 41, 42, 43, 44, 48


def _mixer_bwd(z, yb, dy, conv_a, conv_c, cb, lng, lnb, og, name):
    s = z.shape[0]
    t, hb, w = ROW_TILE, HALO, 256
    e = t + hb

    def body(ah, ab, ac, cv, cg, ah_p, ac_p, cv_p, cg_p, ah_n, ab_n, ac_n, cv_n, cg_n,
             dya, dyb1, dyb2, dyc, dya_n, dyc_n, yb_ref, wa, wc, cb_ref, lng_ref, lnb_ref, og_ref,
             dza_ref, dzc_ref, dyb_ref, pg_ref, dgb_ref, p_sc, u_sc, dc_sc, du_sc):
        i = pl.program_id(0)
        first, last = i == 0, i == pl.num_programs(0) - 1

        def prev(ref):
            return jnp.where(first, 0.0, ref[...])

        def nxt(ref):
            return jnp.where(last, 0.0, ref[...])

        def ext(cur_ref, next_ref):
            return jnp.concatenate([cur_ref[...], nxt(next_ref)], axis=0)

        @pl.when(first)
        def _():
            pg_ref[...] = jnp.zeros_like(pg_ref)
            dgb_ref[...] = jnp.zeros_like(dgb_ref)

        g_a = og_ref[:, 0:256]
        p_sc[pl.ds(0, hb), :] = prev(ah_p) * prev(ac_p)
        p_sc[pl.ds(hb, t), :] = ah[...] * ac[...]
        p_sc[pl.ds(hb + t, hb), :] = nxt(ah_n) * nxt(ac_n)
        conv = jnp.zeros((e, w), f32)
        for k in range(K_SHORT):
            conv = conv + wa[pl.ds(k, 1), :] * p_sc[pl.ds(hb - (K_SHORT - 1) + k, e), :]
        ab_e, dya_e = ext(ab, ab_n), ext(dya, dya_n)
        y_a = ab_e * conv
        r_a = _rstd(y_a)
        dy_a = _rms_dx(y_a, r_a, g_a, dya_e)
        dc_sc[...] = dy_a * ab_e
        dza_ref[:, 256:512] = (dy_a[:t] * conv[:t]).astype(bf16)
        dp = jnp.zeros((t, w), f32)
        for k in range(K_SHORT):
            dp = dp + wa[pl.ds(k, 1), :] * dc_sc[pl.ds(K_SHORT - 1 - k, t), :]
            pg_ref[pl.ds(PG_CONV_A + k, 1), :] += _colsum(dc_sc[pl.ds(0, t), :] * p_sc[pl.ds(hb - (K_SHORT - 1) + k, t), :])
        dza_ref[:, 0:256] = (dp * ac[...]).astype(bf16)
        dza_ref[:, 512:768] = (dp * ah[...]).astype(bf16)
        pg_ref[pl.ds(PG_OG_A, 1), :] += _colsum(dya[...] * y_a[:t] * r_a[:t])

        g_c = og_ref[:, 768:1024]
        u_sc[pl.ds(0, hb), :] = prev(cv_p) * _sigmoid(prev(cg_p))
        u_sc[pl.ds(hb, t), :] = cv[...] * _sigmoid(cg[...])
        u_sc[pl.ds(hb + t, hb), :] = nxt(cv_n) * _sigmoid(nxt(cg_n))
        u = jnp.zeros((e, w), f32) + cb_ref[...]
        for k in range(K_CONF):
            u = u + wc[pl.ds(k, 1), :] * u_sc[pl.ds(hb - (K_CONF - 1) + k, e), :]
        xhat, rs, lnv = _layernorm_parts(u, lng_ref[...], lnb_ref[...])
        sg = _sigmoid(lnv)
        y_c = lnv * sg
        r_c = _rstd(y_c)
        dyc_e = ext(dyc, dyc_n)
        dlnv = _rms_dx(y_c, r_c, g_c, dyc_e) * (sg * (1.0 + lnv * (1.0 - sg)))
        dxh = dlnv * lng_ref[...]
        du = rs * (dxh - jnp.mean(dxh, axis=-1, keepdims=True) - xhat * jnp.mean(dxh * xhat, axis=-1, keepdims=True))
        du_sc[...] = du
        duin = jnp.zeros((t, w), f32)
        for k in range(K_CONF):
            duin = duin + wc[pl.ds(k, 1), :] * du_sc[pl.ds(K_CONF - 1 - k, t), :]
            pg_ref[pl.ds(PG_CONV_C + k, 1), :] += _colsum(du_sc[pl.ds(0, t), :] * u_sc[pl.ds(hb - (K_CONF - 1) + k, t), :])
        sgg = _sigmoid(cg[...])
        dzc_ref[:, 0:256] = (duin * sgg).astype(bf16)
        dzc_ref[:, 256:512] = (duin * cv[...] * sgg * (1.0 - sgg)).astype(bf16)
        pg_ref[pl.ds(PG_CB, 1), :] += _colsum(du[:t])
        pg_ref[pl.ds(PG_LNG, 1), :] += _colsum(dlnv[:t] * xhat[:t])
        pg_ref[pl.ds(PG_LNB, 1), :] += _colsum(dlnv[:t])
        pg_ref[pl.ds(PG_OG_C, 1), :] += _colsum(dyc[...] * y_c[:t] * r_c[:t])

        g_b = og_ref[:, 256:768]
        ybv = yb_ref[...]
        r_b = _rstd(ybv)
        dyb = jnp.concatenate([dyb1[...], dyb2[...]], axis=1)
        dyb_ref[...] = _rms_dx(ybv, r_b, g_b, dyb)
        dgb_ref[...] += _colsum(dyb * ybv * r_b)

    full = lambda shape: pl.BlockSpec(shape, lambda i: tuple(0 for _ in shape))
    zc = (Z_AH, Z_AB, Z_AC, Z_CV, Z_CG)
    in_specs = [_cur(t, w, c) for c in zc] + [_prev(hb, t, w, c) for c in (Z_AH, Z_AC, Z_CV, Z_CG)]
    in_specs += [_next(hb, t, w, c, s) for c in zc]
    in_specs += [_cur(t, w, c) for c in (0, 1, 2, 3)] + [_next(hb, t, w, 0, s), _next(hb, t, w, 3, s)]
    in_specs += [_cur(t, 512, 0), full((K_SHORT, w)), full((K_CONF, w)), full((1, w)), full((1, w)), full((1, w)), full((1, 1024))]
    out_shape = (SDS((s, 768), bf16), SDS((s, 512), bf16), SDS((s, 512), f32), SDS((PG_ROWS, w), f32), SDS((1, 512), f32))
    out_specs = (_cur(t, 768, 0), _cur(t, 512, 0), _cur(t, 512, 0), full((PG_ROWS, w)), full((1, 512)))
    scratch = [pltpu.VMEM((hb + t + hb, w), f32), pltpu.VMEM((hb + t + hb, w), f32), pltpu.VMEM((e, w), f32), pltpu.VMEM((e, w), f32)]
    args = [z] * 14 + [dy] * 6 + [yb, conv_a, conv_c, cb, lng, lnb, og]
    return _pcall(body, name=name, out_shape=out_shape, grid=(s // t,), in_specs=in_specs, out_specs=out_specs,
                  scratch_shapes=scratch, compiler_params=_cp(("arbitrary",)))(*args)


def _ffn_act_fwd(upre, conv_f, name):
    s, f2 = upre.shape
    t, hb, w = ROW_TILE, HALO_FFN, 256
    nj = f2 // 2 // w

    def body(g_c, v_c, g_p, v_p, wg, wv, o_ref, g_sc, v_sc):
        first = pl.program_id(1) == 0
        g_sc[pl.ds(0, hb), :] = jnp.where(first, 0.0, g_p[...])
        v_sc[pl.ds(0, hb), :] = jnp.where(first, 0.0, v_p[...])
        g_sc[pl.ds(hb, t), :] = g_c[...]
        v_sc[pl.ds(hb, t), :] = v_c[...]
        ug = jnp.zeros((t, w), f32)
        uv = jnp.zeros((t, w), f32)
        for k in range(K_FFN):
            ug = ug + wg[pl.ds(k, 1), :] * g_sc[pl.ds(hb - (K_FFN - 1) + k, t), :]
            uv = uv + wv[pl.ds(k, 1), :] * v_sc[pl.ds(hb - (K_FFN - 1) + k, t), :]
        o_ref[...] = (ug * _sigmoid(ug) * uv).astype(bf16)

    cur = lambda off: pl.BlockSpec((t, w), lambda j, i: (i, off + j))
    prv = lambda off: pl.BlockSpec((hb, w), lambda j, i: (jnp.maximum(i * (t // hb) - 1, 0), off + j))
    wsp = lambda off: pl.BlockSpec((K_FFN, w), lambda j, i: (0, off + j))
    return _pcall(body, name=name, out_shape=SDS((s, f2 // 2), bf16), grid=(nj, s // t),
                  in_specs=[cur(0), cur(nj), prv(0), prv(nj), wsp(0), wsp(nj)],
                  out_specs=pl.BlockSpec((t, w), lambda j, i: (i, j)),
                  scratch_shapes=[pltpu.VMEM((hb + t, w), f32), pltpu.VMEM((hb + t, w), f32)],
                  compiler_params=_cp(("parallel", "parallel")))(upre, upre, upre, upre, conv_f, conv_f)


def _ffn_act_bwd(upre, dact, conv_f, name):
    s, f2 = upre.shape
    t, hb, w = ROW_TILE, HALO_FFN, 256
    nj = f2 // 2 // w
    e = t + hb

    def body(g_c, v_c, g_p, v_p, g_n, v_n, da_c, da_n, wg, wv, dg_ref, dv_ref, dwg_ref, dwv_ref, g_sc, v_sc, dg_sc, dv_sc):
        i = pl.program_id(1)
        first, last = i == 0, i == pl.num_programs(1) - 1

        @pl.when(first)
        def _():
            dwg_ref[...] = jnp.zeros_like(dwg_ref)
            dwv_ref[...] = jnp.zeros_like(dwv_ref)

        for sc, p, c, n in ((g_sc, g_p, g_c, g_n), (v_sc, v_p, v_c, v_n)):
            sc[pl.ds(0, hb), :] = jnp.where(first, 0.0, p[...])
            sc[pl.ds(hb, t), :] = c[...]
            sc[pl.ds(hb + t, hb), :] = jnp.where(last, 0.0, n[...])
        g_taps = [g_sc[pl.ds(hb - (K_FFN - 1) + k, e), :] for k in range(K_FFN)]
        v_taps = [v_sc[pl.ds(hb - (K_FFN - 1) + k, e), :] for k in range(K_FFN)]
        ug = jnp.zeros((e, w), f32)
        uv = jnp.zeros((e, w), f32)
        for k in range(K_FFN):
            ug = ug + wg[pl.ds(k, 1), :] * g_taps[k]
            uv = uv + wv[pl.ds(k, 1), :] * v_taps[k]
        da = jnp.concatenate([da_c[...], jnp.where(last, 0.0, da_n[...])], axis=0)
        sg = _sigmoid(ug)
        dv = da * (ug * sg)
        dg = da * uv * (sg * (1.0 + ug * (1.0 - sg)))
        dv_sc[...] = dv
        dg_sc[...] = dg
        for taps, dsc, dval, wr, dref, dwref in ((g_taps, dg_sc, dg, wg, dg_ref, dwg_ref), (v_taps, dv_sc, dv, wv, dv_ref, dwv_ref)):
            acc = wr[pl.ds(K_FFN - 1, 1), :] * dval[:t]
            for k in range(K_FFN):
                if k < K_FFN - 1:
                    acc = acc + wr[pl.ds(k, 1), :] * dsc[pl.ds(K_FFN - 1 - k, t), :]
                dwref[pl.ds(k, 1), :] += _colsum(dval[:t] * taps[k][:t])
            dref[...] = acc.astype(bf16)

    cur = lambda off: pl.BlockSpec((t, w), lambda j, i: (i, off + j))
    prv = lambda off: pl.BlockSpec((hb, w), lambda j, i: (jnp.maximum(i * (t // hb) - 1, 0), off + j))
    nxt = lambda off: pl.BlockSpec((hb, w), lambda j, i: (jnp.minimum((i + 1) * (t // hb), s // hb - 1), off + j))
    wsp = lambda off: pl.BlockSpec((K_FFN, w), lambda j, i: (0, off + j))
    half = SDS((s, f2 // 2), bf16)
    dws = SDS((8, f2 // 2), f32)
    o_cur = pl.BlockSpec((t, w), lambda j, i: (i, j))
    o_dw = pl.BlockSpec((8, w), lambda j, i: (0, j))
    return _pcall(body, name=name, out_shape=(half, half, dws, dws), grid=(nj, s // t),
                  in_specs=[cur(0), cur(nj), prv(0), prv(nj), nxt(0), nxt(nj), cur(0), nxt(0), wsp(0), wsp(nj)],
                  out_specs=(o_cur, o_cur, o_dw, o_dw),
                  scratch_shapes=[pltpu.VMEM((hb + t + hb, w), f32), pltpu.VMEM((hb + t + hb, w), f32),
                                  pltpu.VMEM((e, w), f32), pltpu.VMEM((e, w), f32)],
                  compiler_params=_cp(("parallel", "arbitrary")))(upre, upre, upre, upre, upre, upre, dact, dact, conv_f, conv_f)


def _bias_tables(rel_bias, name):
    nb = len(DILATIONS)

    def body(rb_ref, idx_ref, o_ref):
        h = pl.program_id(1)
        iv = idx_ref[0]
        acc = jnp.zeros((BLK, 2 * BLK), f32)
        for b in range(N_BUCKETS):
            acc = jnp.where(iv == b, rb_ref[b, h], acc)
        o_ref[0, 0] = acc

    return _pcall(body, name=name, out_shape=SDS((nb, 8, BLK, 2 * BLK), f32), grid=(nb, 8),
                  in_specs=[pl.BlockSpec(memory_space=pltpu.SMEM), pl.BlockSpec((1, BLK, 2 * BLK), lambda br, h: (br, 0, 0))],
                  out_specs=pl.BlockSpec((1, 1, BLK, 2 * BLK), lambda br, h: (br, h, 0, 0)),
                  compiler_params=_cp(("parallel", "parallel")))(rel_bias, jnp.asarray(_BUCKETS))


def _bias_grad(ds_tabs, name):
    nl, nb = ds_tabs.shape[0], ds_tabs.shape[1]

    def body(t_ref, idx_ref, o_ref):
        br, h = pl.program_id(0), pl.program_id(1)

        @pl.when(jnp.logical_and(br == 0, h == 0))
        def _():
            o_ref[...] = jnp.zeros_like(o_ref)

        tab = t_ref[0, 0, 0]
        for l in range(1, nl):
            tab = tab + t_ref[l, 0, 0]
        iv = idx_ref[0]
        row = lax.broadcasted_iota(jnp.int32, (N_BUCKETS, 128), 0)
        col = lax.broadcasted_iota(jnp.int32, (N_BUCKETS, 128), 1)
        acc = jnp.zeros((N_BUCKETS, 128), f32)
        for b in range(N_BUCKETS):
            sel = jnp.where(iv == b, tab, 0.0)
            tot = jnp.sum(jnp.sum(sel, axis=0, keepdims=True), axis=1, keepdims=True)
            acc = acc + jnp.where(jnp.logical_and(row == b, col == h), tot, 0.0)
        o_ref[...] += acc

    return _pcall(body, name=name, out_shape=SDS((N_BUCKETS, 128), f32), grid=(nb, 8),
                  in_specs=[pl.BlockSpec((nl, 1, 1, BLK, 2 * BLK), lambda br, h: (0, br, h, 0, 0)),
                            pl.BlockSpec((1, BLK, 2 * BLK), lambda br, h: (br, 0, 0))],
                  out_specs=pl.BlockSpec((N_BUCKETS, 128), lambda br, h: (0, 0)),
                  compiler_params=_cp(("arbitrary", "arbitrary")))(ds_tabs, jnp.asarray(_BUCKETS))


def _band_masks():
    qi = lax.broadcasted_iota(jnp.int32, (2 * BLK, 2 * BLK), 0) % BLK
    kj = lax.broadcasted_iota(jnp.int32, (2 * BLK, 2 * BLK), 1)
    band = jnp.logical_and(kj >= qi, kj <= qi + N_KEYS)
    return band, kj


def _stack_heads(x, head0):
    zero = jnp.zeros_like(x)
    return jnp.concatenate([jnp.where(head0, x, zero), jnp.where(head0, zero, x)], axis=0)


def _block_rows(blk, d, nblk):
    per_res = nblk // d
    res, n = blk // per_res, blk % per_res
    cur = pl.ds(res + d * BLK * n, BLK, stride=d) if d > 1 else pl.ds(pl.multiple_of(BLK * n, BLK), BLK)
    before = jnp.maximum(n - 1, 0)
    prev = pl.ds(res + d * BLK * before, BLK, stride=d) if d > 1 else pl.ds(pl.multiple_of(BLK * before, BLK), BLK)
    return cur, prev, jnp.where(n == 0, BLK, 0)


def _attn_fwd(z, bias, name):
    s = z.shape[0]
    nblk = s // BLK

    def body(q_ref, k_ref, v_ref, b_ref, y_ref, l_ref):
        band, kj = _band_masks()
        lane = lax.broadcasted_iota(jnp.int32, (BLK, BLK), 1)
        head0 = lane < D_HEAD

        for branch, d in enumerate(DILATIONS):
            def step(blk, carry, branch=branch, d=d):
                cur, prev, first_key = _block_rows(blk, d, nblk)
                valid = jnp.logical_and(band, kj >= first_key)
                q = (q_ref[cur, :] * (D_HEAD ** -0.5)).astype(bf16)
                kk = jnp.concatenate([k_ref[prev, :], k_ref[cur, :]], axis=0).astype(bf16)
                vv = jnp.concatenate([v_ref[prev, :], v_ref[cur, :]], axis=0).astype(bf16)
                sc = lax.dot_general(_stack_heads(q, head0), kk, (((1,), (1,)), ((), ())), preferred_element_type=f32) + b_ref[branch, 0]
                sc = jnp.where(valid, sc, NEG)
                m = jnp.max(sc, axis=-1, keepdims=True)
                p = jnp.exp(sc - m)
                den = jnp.sum(p, axis=-1, keepdims=True)
                out = jnp.dot(p.astype(bf16), vv, preferred_element_type=f32) / den
                lse = m + jnp.log(den)
                out = jnp.where(head0, out[:BLK], out[BLK:])
                lse = jnp.where(head0, lse[:BLK], lse[BLK:]) + jnp.zeros((BLK, BLK), f32)
                if branch == 0:
                    y_ref[cur, :] = out
                    l_ref[cur, :] = lse
                else:
                    y_old, l_old = y_ref[cur, :], l_ref[cur, :]
                    top = jnp.maximum(l_old, lse)
                    e_old, e_new = jnp.exp(l_old - top), jnp.exp(lse - top)
                    tot = e_old + e_new
                    y_ref[cur, :] = (e_old * y_old + e_new * out) / tot
                    l_ref[cur, :] = top + jnp.log(tot)
                return carry

            lax.fori_loop(0, nblk, step, 0, unroll=4)

    col = lambda off: pl.BlockSpec((s, BLK), lambda hp: (0, off + hp))
    out = SDS((s, 4 * BLK), f32)
    nbr = len(DILATIONS)
    return _pcall(body, name=name, out_shape=(out, out), grid=(4,),
                  in_specs=[col(Z_Q128), col(Z_K128), col(Z_V128), pl.BlockSpec((nbr, 1, 2 * BLK, 2 * BLK), lambda hp: (0, hp, 0, 0))],
                  out_specs=(col(0), col(0)), compiler_params=_cp(("parallel",)))(z, z, z, bias)


def _attn_bwd(z, dyb, yb, lt, bias, name):
    s = z.shape[0]
    nblk = s // BLK
    scale = D_HEAD ** -0.5
    chunk = 256

    def body(q_ref, k_ref, v_ref, dy_ref, y_ref, lt_ref, b_ref, dq_ref, dk_ref, dv_ref, ds_ref, aq, ak, av):
        band, kj = _band_masks()
        lane = lax.broadcasted_iota(jnp.int32, (BLK, BLK), 1)
        head0 = lane < D_HEAD
        ak[...] = jnp.zeros_like(ak)
        av[...] = jnp.zeros_like(av)
        ds_ref[...] = jnp.zeros_like(ds_ref)

        for branch, d in enumerate(DILATIONS):
            def step(blk, carry, branch=branch, d=d):
                cur, prev, first_key = _block_rows(blk, d, nblk)
                valid = jnp.logical_and(band, kj >= first_key)
                q = (q_ref[cur, :] * scale).astype(bf16)
                kk = jnp.concatenate([k_ref[prev, :], k_ref[cur, :]], axis=0).astype(bf16)
                vv = jnp.concatenate([v_ref[prev, :], v_ref[cur, :]], axis=0).astype(bf16)
                dy = dy_ref[cur, :]
                lt_b = lt_ref[cur, :]
                qs = _stack_heads(q, head0)
                dys = _stack_heads(dy.astype(bf16), head0)
                dd = jnp.sum(_stack_heads(dy * y_ref[cur, :], head0), axis=-1, keepdims=True)
                lts = jnp.concatenate([lt_b[:, 0:1], lt_b[:, D_HEAD:D_HEAD + 1]], axis=0)
                sc = lax.dot_general(qs, kk, (((1,), (1,)), ((), ())), preferred_element_type=f32) + b_ref[branch, 0]
                p = jnp.where(valid, jnp.exp(sc - lts), 0.0)
                dp = lax.dot_general(dys, vv, (((1,), (1,)), ((), ())), preferred_element_type=f32)
                ds = p * (dp - dd)
                ds_ref[branch, 0] += ds
                dsb = ds.astype(bf16)
                dq = jnp.dot(dsb, kk, preferred_element_type=f32) * scale
                dkk = lax.dot_general(dsb, qs, (((0,), (0,)), ((), ())), preferred_element_type=f32)
                dvv = lax.dot_general(p.astype(bf16), dys, (((0,), (0,)), ((), ())), preferred_element_type=f32)
                dq = jnp.where(head0, dq[:BLK], dq[BLK:])
                if branch == 0:
                    aq[cur, :] = dq
                else:
                    aq[cur, :] += dq
                ak[prev, :] += dkk[:BLK]
                ak[cur, :] += dkk[BLK:]
                av[prev, :] += dvv[:BLK]
                av[cur, :] += dvv[BLK:]
                return carry

            lax.fori_loop(0, nblk, step, 0, unroll=4)

        def store(c, carry):
            rows = pl.ds(pl.multiple_of(c * chunk, chunk), chunk)
            dq_ref[rows, :] = aq[rows, :].astype(bf16)
            dk_ref[rows, :] = ak[rows, :].astype(bf16)
            dv_ref[rows, :] = av[rows, :].astype(bf16)
            return carry

        lax.fori_loop(0, s // chunk, store, 0)

    col = lambda off: pl.BlockSpec((s, BLK), lambda hp: (0, off + hp))
    nbr = len(DILATIONS)
    tabs = pl.BlockSpec((nbr, 1, 2 * BLK, 2 * BLK), lambda hp: (0, hp, 0, 0))
    half = SDS((s, 4 * BLK), bf16)
    return _pcall(body, name=name, out_shape=(half, half, half, SDS((nbr, 4, 2 * BLK, 2 * BLK), f32)), grid=(4,),
                  in_specs=[col(Z_Q128), col(Z_K128), col(Z_V128), col(0), col(0), col(0), tabs],
                  out_specs=(col(0), col(0), col(0), tabs), scratch_shapes=[pltpu.VMEM((s, BLK), f32)] * 3,
                  compiler_params=_cp(("parallel",)))(z, z, z, dyb, yb, lt, bias)


def _adamw_math(w, g, m, v):
    m = ADAM_B1 * m + (1.0 - ADAM_B1) * g
    v = ADAM_B2 * v + (1.0 - ADAM_B2) * (g * g)
    m_hat = m / (1.0 - ADAM_B1 ** ADAM_STEP)
    v_hat = v / (1.0 - ADAM_B2 ** ADAM_STEP)
    delta = -ADAM_LR * (m_hat / (jnp.sqrt(v_hat) + ADAM_EPS) + ADAM_WD * w)
    return delta, m, v


def _adamw_layer(land, send, w, m, v, prev, l, tr, name):
    _, r, c = w.shape
    assert r % tr == 0

    def body(land_ref, send_ref, w_ref, m_ref, v_ref, p0, p1, p2, p3, g_ref, d_ref, nm_ref, nv_ref):
        mx, my, mc = _my_place()
        me = jnp.zeros((tr, c), jnp.int32) + (4 * mx + 2 * my + mc)
        g = jnp.zeros((tr, c), f32)
        for j in range(N_DEV):
            g = g + jnp.where(me == j, send_ref[j], land_ref[j]).astype(f32)
        delta, nm, nv = _adamw_math(w_ref[...], g, m_ref[...], v_ref[...])
        g_ref[...] = g
        d_ref[...] = delta
        nm_ref[...] = nm
        nv_ref[...] = nv

    parts = pl.BlockSpec((N_DEV, tr, c), lambda i: (0, i, 0))
    row = pl.BlockSpec((None, tr, c), lambda i: (l, i, 0))
    anywhere = pl.BlockSpec(memory_space=pl.ANY)
    out = SDS(w.shape, f32)
    return _pcall(body, name=name, out_shape=(out, out, out, out), grid=(r // tr,),
                  in_specs=[parts, parts, row, row, row] + [anywhere] * 4, out_specs=(row, row, row, row),
                  input_output_aliases={5: 0, 6: 1, 7: 2, 8: 3}, compiler_params=_cp(("parallel",)))(land, send, w, m, v, *prev)


def _sum_parts(parts, name):
    _, r, c = parts.shape

    def body(p_ref, o_ref):
        g = p_ref[0]
        for j in range(1, N_DEV):
            g = g + p_ref[j]
        o_ref[...] = g

    return _pcall(body, name=name, out_shape=SDS((r, c), f32), compiler_params=_cp())(parts)


def _adamw_small(g, w, m, v, name):
    def body(g_ref, w_ref, m_ref, v_ref, d_ref, nm_ref, nv_ref):
        delta, nm, nv = _adamw_math(w_ref[...], g_ref[...], m_ref[...], v_ref[...])
        d_ref[...] = delta
        nm_ref[...] = nm
        nv_ref[...] = nv

    out = SDS(w.shape, f32)
    return _pcall(body, name=name, out_shape=(out, out, out), compiler_params=_cp())(g, w, m, v)


def _my_place():
    return lax.axis_index("x"), lax.axis_index("y"), lax.axis_index("c")


def _all_gather(x, name, after=None):
    r, c = x.shape

    def body(x_ref, *rest):
        out_ref, send_sems, recv_sems, local_sem = rest[-4:]
        mx, my, mc = _my_place()
        me, sibling = (mx, my, mc), (mx, my, 1 - mc)
        chips = [(1 - mx, my), (mx, 1 - my), (1 - mx, 1 - my)]

        def rows(px, py, pc):
            return out_ref.at[4 * px + 2 * py + pc]

        def copy(k, block, to, src=None):
            return pltpu.make_async_remote_copy(src_ref=rows(*block) if src is None else src, dst_ref=rows(*block),
                                                send_sem=send_sems.at[k], recv_sem=recv_sems.at[k], device_id=to, device_id_type=MESH)

        mine = pltpu.make_async_copy(x_ref, rows(*me), local_sem)
        mine.start()
        first = [copy(0, me, sibling, src=x_ref)] + [copy(1 + j, me, (*chip, mc), src=x_ref) for j, chip in enumerate(chips)]
        for cp in first:
            cp.start()
        passed = [copy(4 + j, (*chip, mc), sibling) for j, chip in enumerate(chips)]
        for j, chip in enumerate(chips):
            copy(1 + j, (*chip, mc), me).wait_recv()
            passed[j].start()
        copy(0, sibling, me).wait_recv()
        for j, chip in enumerate(chips):
            copy(4 + j, (*chip, 1 - mc), me).wait_recv()
        for cp in first + passed:
            cp.wait_send()
        mine.wait()

    ins = [x] + ([after] if after is not None else [])
    return _pcall(body, name=name, out_shape=SDS((N_DEV, r, c), x.dtype),
                  in_specs=[pl.BlockSpec(memory_space=pl.ANY)] * len(ins), out_specs=pl.BlockSpec(memory_space=pl.ANY),
                  scratch_shapes=[pltpu.SemaphoreType.DMA((7,)), pltpu.SemaphoreType.DMA((7,)), pltpu.SemaphoreType.DMA(())])(*ins)


HBM_SPEC = pl.BlockSpec(memory_space=pltpu.HBM)
SEM_SPEC = pl.BlockSpec(memory_space=pltpu.SEMAPHORE)
ANY_SPEC = pl.BlockSpec(memory_space=pl.ANY)
DATAFLOW = pltpu.SideEffectType.DATAFLOW_SIDE_EFFECTING


def _peers():
    mx, my, mc = _my_place()
    out = []
    for k in range(1, N_DEV):
        tx, ty, tc = mx ^ ((k >> 2) & 1), my ^ ((k >> 1) & 1), mc ^ (k & 1)
        out.append(((tx, ty, tc), 4 * tx + 2 * ty + tc))
    return out


def _in_hbm(arrays):
    return [pltpu.with_memory_space_constraint(a, pltpu.HBM) for a in arrays]


def _gather_start(groups, after, name):
    ng = len(groups)
    flat = [b for grp in groups for b in grp]
    n = len(flat)

    def body(*refs):
        ins, sems, token = refs[:n], refs[n + 1:n + 1 + 2 * ng], refs[-1]
        mx, my, mc = _my_place()
        me = 4 * mx + 2 * my + mc
        peers = _peers()
        at = 0
        for g, grp in enumerate(groups):
            for t in range(len(grp)):
                buf = ins[at]
                at += 1
                for dev, _ in peers:
                    pltpu.make_async_remote_copy(src_ref=buf.at[me], dst_ref=buf.at[me], send_sem=sems[2 * g].at[t],
                                                 recv_sem=sems[2 * g + 1].at[t], device_id=dev, device_id_type=MESH).start()
        token[...] = jnp.zeros_like(token)

    sem_shapes = tuple(pltpu.SemaphoreType.DMA((len(grp),)) for grp in groups for _ in range(2))
    out_shape = sem_shapes + tuple(pltpu.HBM(b.shape, b.dtype) for b in flat) + (SDS((8, 128), f32),)
    out = _pcall(body, name=name, out_shape=out_shape, in_specs=(HBM_SPEC,) * n + (ANY_SPEC,),
                 out_specs=(SEM_SPEC,) * (2 * ng) + (HBM_SPEC,) * n + (pl.BlockSpec(memory_space=pltpu.VMEM),),
                 input_output_aliases={i: 2 * ng + i for i in range(n)},
                 compiler_params=pltpu.CompilerParams(has_side_effects=DATAFLOW))(*_in_hbm(flat), after)
    sems = [(out[2 * g], out[2 * g + 1]) for g in range(ng)]
    thru, at = [], 2 * ng
    for grp in groups:
        thru.append(list(out[at:at + len(grp)]))
        at += len(grp)
    return sems, thru, out[-1]


def _exchange_start(sends, lands, name):
    nt = len(sends)

    def body(*refs):
        srcs, dsts, s_sem, r_sem, token = refs[:nt], refs[nt:2 * nt], refs[2 * nt], refs[2 * nt + 1], refs[-1]
        mx, my, mc = _my_place()
        me = 4 * mx + 2 * my + mc
        for t in range(nt):
            for dev, idx in _peers():
                pltpu.make_async_remote_copy(src_ref=srcs[t].at[idx], dst_ref=dsts[t].at[me], send_sem=s_sem.at[t],
                                             recv_sem=r_sem.at[t], device_id=dev, device_id_type=MESH).start()
        token[...] = jnp.zeros_like(token)

    both = list(sends) + list(lands)
    out_shape = (pltpu.SemaphoreType.DMA((nt,)), pltpu.SemaphoreType.DMA((nt,))) + tuple(pltpu.HBM(b.shape, b.dtype) for b in both)
    out = _pcall(body, name=name, out_shape=out_shape + (SDS((8, 128), f32),), in_specs=(HBM_SPEC,) * (2 * nt),
                 out_specs=(SEM_SPEC, SEM_SPEC) + (HBM_SPEC,) * (2 * nt) + (pl.BlockSpec(memory_space=pltpu.VMEM),),
                 input_output_aliases={i: 2 + i for i in range(2 * nt)},
                 compiler_params=pltpu.CompilerParams(has_side_effects=DATAFLOW))(*_in_hbm(both))
    return (out[0], out[1]), list(out[2:2 + nt]), list(out[2 + nt:2 + 2 * nt]), out[-1]


def _copies_wait(srcs, dsts, sems, after, name):
    nt = len(srcs)
    same = srcs is dsts
    bufs = list(srcs) if same else list(srcs) + list(dsts)
    nb = len(bufs)

    def body(*refs):
        s_sem, r_sem = refs[nb], refs[nb + 1]
        mx, my, mc = _my_place()
        for t in range(nt):
            src = refs[t].at[pl.ds(0, N_DEV - 1)]
            dst = src if same else refs[nt + t].at[pl.ds(0, N_DEV - 1)]
            cp = pltpu.make_async_remote_copy(src_ref=src, dst_ref=dst, send_sem=s_sem.at[t], recv_sem=r_sem.at[t],
                                              device_id=(mx, my, 1 - mc), device_id_type=MESH)
            cp.wait_send()
            cp.wait_recv()

    out = _pcall(body, name=name, out_shape=tuple(pltpu.HBM(b.shape, b.dtype) for b in bufs),
                 in_specs=(HBM_SPEC,) * nb + (SEM_SPEC, SEM_SPEC, ANY_SPEC), out_specs=(HBM_SPEC,) * nb,
                 input_output_aliases={i: i for i in range(nb)},
                 compiler_params=pltpu.CompilerParams(has_side_effects=DATAFLOW))(*bufs, sems[0], sems[1], after)
    out = list(out)
    return (out, out) if same else (out[:nt], out[nt:])


def _cast_blocks(weights, l, me, after, name):
    nt = len(weights)
    halves = 2
    extra = [] if after is None else [after]

    def body(me_ref, *refs):
        outs = refs[nt + len(extra):]
        for t in range(nt):
            outs[t][...] = refs[t][...].astype(bf16)

    in_specs, out_specs, out_shape = [], [], []
    for wt in weights:
        _, r, c = wt.shape
        tr = r // halves
        in_specs.append(pl.BlockSpec((None, tr, c), lambda i, me_ref: (l, i, 0)))
        out_specs.append(pl.BlockSpec((None, tr, c), lambda i, me_ref: (me_ref[0], i, 0)))
        out_shape.append(SDS((N_DEV, r, c), bf16))
    in_specs += [pl.BlockSpec(memory_space=pl.ANY)] * len(extra)
    grid_spec = pltpu.PrefetchScalarGridSpec(num_scalar_prefetch=1, grid=(halves,), in_specs=in_specs, out_specs=out_specs)
    return list(_pcall(body, name=name, out_shape=out_shape, grid_spec=grid_spec, compiler_params=_cp(("parallel",)))(me, *weights, *extra))

def _layer_fwd_mix(x, p, bias, l, after=None):
    tag = f"l{l}"
    h = _norm_cast(x, p["norm_mix_g"], f"{tag}_norm_mix")
    z = _mm(h, p["w_in"], "nt", 512, 1408, f"{tag}_mm_in", after=after, b_outer=True)
    yb, lt = _attn_fwd(z, bias, f"{tag}_attn_fwd")
    y = _mixer_fwd(z, yb, p["conv_a_w"], p["conv_c_w"], p["conv_c_b"], p["ln_c_g"], p["ln_c_b"], p["out_norm_g"], f"{tag}_mixer_fwd")
    return y, dict(x=x, h=h, z=z, yb=yb, lt=lt, y=y)


def _layer_fwd_ffn(x, y, p, l, after=None):
    tag = f"l{l}"
    x1 = _mm(y, p["w_out"], "nn", 512, 1024, f"{tag}_mm_out", res=x, after=after)
    h2 = _norm_cast(x1, p["norm_ffn_g"], f"{tag}_norm_ffn")
    upre = _mm(h2, p["w_up"], "nt", 512, 1408, f"{tag}_mm_up", b_outer=True)
    act = _ffn_act_fwd(upre, p["conv_f_w"], f"{tag}_ffn_act")
    x2 = _mm(act, p["w_down"], "nn", 512, 1024, f"{tag}_mm_down", res=x1)
    return x2, dict(x1=x1, h2=h2, upre=upre, act=act)


def _layer_bwd_ffn(dx2, p, sv, l, after=None):
    tag = f"l{l}"
    g = {}
    dact = _mm(dx2, p["w_down"], "nt", 512, 1408, f"{tag}_mm_dact", after=after, b_outer=True)
    dwd = _mm(sv["act"], dx2, "tn", 1408, 1024, f"{tag}_mm_dwdown", out_dtype=bf16)
    g["w_down"] = dwd.reshape(N_DEV, dwd.shape[0] // N_DEV, dwd.shape[1])
    dug, duv, dwg, dwv = _ffn_act_bwd(sv["upre"], dact, p["conv_f_w"], f"{tag}_ffn_act_bwd")
    g["conv_f_w"] = jnp.concatenate([dwg[:K_FFN], dwv[:K_FFN]], axis=1)
    dupre = jnp.concatenate([dug, duv], axis=1)
    dh2 = _mm(dupre, p["w_up"], "nn", 512, 1024, f"{tag}_mm_dh2")
    dwu = _mm(dupre, sv["h2"], "tn", 1408, 1024, f"{tag}_mm_dwup", out_dtype=bf16)
    g["w_up"] = dwu.reshape(N_DEV, dwu.shape[0] // N_DEV, dwu.shape[1])
    dx1, g["norm_ffn_g"] = _rms_bwd(sv["x1"], p["norm_ffn_g"], dh2, dx2, f"{tag}_norm_ffn_bwd")
    return dx1, g


def _layer_bwd_mix(dx1, p, sv, bias, l, after=None):
    tag = f"l{l}"
    g = {}
    dy = _mm(dx1, p["w_out"], "nt", 512, 1024, f"{tag}_mm_dy", after=after)
    dwo = _mm(sv["y"], dx1, "tn", 512, 1024, f"{tag}_mm_dwout", out_dtype=bf16)
    g["w_out"] = dwo.reshape(N_DEV, dwo.shape[0] // N_DEV, dwo.shape[1])
    dza, dzc, dyb, pg, dgb = _mixer_bwd(sv["z"], sv["yb"], dy, p["conv_a_w"], p["conv_c_w"], p["conv_c_b"], p["ln_c_g"],
                                        p["ln_c_b"], p["out_norm_g"], f"{tag}_mixer_bwd")
    g["conv_a_w"] = pg[PG_CONV_A:PG_CONV_A + K_SHORT]
    g["conv_c_w"] = pg[PG_CONV_C:PG_CONV_C + K_CONF]
    g["conv_c_b"], g["ln_c_g"], g["ln_c_b"] = pg[PG_CB:PG_CB + 1], pg[PG_LNG:PG_LNG + 1], pg[PG_LNB:PG_LNB + 1]
    g["out_norm_g"] = jnp.concatenate([pg[PG_OG_A:PG_OG_A + 1], dgb, pg[PG_OG_C:PG_OG_C + 1]], axis=1)
    dq, dk, dv, tabs = _attn_bwd(sv["z"], dyb, sv["yb"], sv["lt"], bias, f"{tag}_attn_bwd")
    dz = jnp.concatenate([dza, dq, dk, dv, dzc], axis=1)
    dh = _mm(dz, p["w_in"], "nn", 512, 1024, f"{tag}_mm_dh")
    dwi = _mm(dz, sv["h"], "tn", 1408, 1024, f"{tag}_mm_dwin", out_dtype=bf16)
    g["w_in"] = dwi.reshape(N_DEV, dwi.shape[0] // N_DEV, dwi.shape[1])
    dx, g["norm_mix_g"] = _rms_bwd(sv["x"], p["norm_mix_g"], dh, dx1, f"{tag}_norm_mix_bwd")
    return dx, g, tabs


SMALL_LAYER_PARAMS = ("norm_mix_g", "conv_a_w", "conv_c_w", "conv_c_b", "ln_c_g", "ln_c_b", "out_norm_g", "norm_ffn_g", "conv_f_w")


BIG = ("w_in", "w_out", "w_up", "w_down")
TRANSPOSED = ("w_in", "w_up")
ADAM_ROWS = {"w_in": 176, "w_out": 128, "w_up": 176, "w_down": 176}
SMALL_SHARDED = ("conv_a_w", "conv_c_w", "conv_f_w")
SMALL_REPLICATED = ("norm_mix_g", "conv_c_b", "ln_c_g", "ln_c_b", "out_norm_g", "norm_ffn_g", "rel_bias", "final_g")
WEIGHTS = ("norm_mix_g", "w_in", "conv_a_w", "conv_c_w", "conv_c_b", "ln_c_g", "ln_c_b", "out_norm_g", "w_out",
           "norm_ffn_g", "w_up", "conv_f_w", "w_down", "rel_bias", "final_g")


def _pack(arrays):
    flat = jnp.concatenate([a.reshape(-1) for a in arrays])
    rows = -(-flat.shape[0] // 1024) * 8
    return jnp.pad(flat, (0, rows * 128 - flat.shape[0])).reshape(rows, 128)


def _unpack(packed, shapes):
    flat, out, at = packed.reshape(-1), [], 0
    for shp in shapes:
        n = int(np.prod(shp))
        out.append(flat[at:at + n].reshape(shp))
        at += n
    return out


def _gather_shards(g, local_shape):
    nd = len(local_shape)
    return jnp.moveaxis(g, 0, nd - 1).reshape(local_shape[:-1] + (N_DEV * local_shape[-1],))


def kernel(x, norm_mix_g, w_in, conv_a_w, conv_c_w, conv_c_b, ln_c_g, ln_c_b, out_norm_g, w_out, norm_ffn_g, w_up, conv_f_w, w_down, rel_bias, final_g, loss_target, m_norm_mix_g, m_w_in, m_conv_a_w, m_conv_c_w, m_conv_c_b, m_ln_c_g, m_ln_c_b, m_out_norm_g, m_w_out, m_norm_ffn_g, m_w_up, m_conv_f_w, m_w_down, m_rel_bias, m_final_g, v_norm_mix_g, v_w_in, v_conv_a_w, v_conv_c_w, v_conv_c_b, v_ln_c_g, v_ln_c_b, v_out_norm_g, v_w_out, v_norm_ffn_g, v_w_up, v_conv_f_w, v_w_down, v_rel_bias, v_final_g):
    w = dict(norm_mix_g=norm_mix_g, w_in=w_in, conv_a_w=conv_a_w, conv_c_w=conv_c_w, conv_c_b=conv_c_b, ln_c_g=ln_c_g,
             ln_c_b=ln_c_b, out_norm_g=out_norm_g, w_out=w_out, norm_ffn_g=norm_ffn_g, w_up=w_up, conv_f_w=conv_f_w,
             w_down=w_down, rel_bias=rel_bias, final_g=final_g)
    mom = dict(norm_mix_g=m_norm_mix_g, w_in=m_w_in, conv_a_w=m_conv_a_w, conv_c_w=m_conv_c_w, conv_c_b=m_conv_c_b,
               ln_c_g=m_ln_c_g, ln_c_b=m_ln_c_b, out_norm_g=m_out_norm_g, w_out=m_w_out, norm_ffn_g=m_norm_ffn_g,
               w_up=m_w_up, conv_f_w=m_conv_f_w, w_down=m_w_down, rel_bias=m_rel_bias, final_g=m_final_g)
    var = dict(norm_mix_g=v_norm_mix_g, w_in=v_w_in, conv_a_w=v_conv_a_w, conv_c_w=v_conv_c_w, conv_c_b=v_conv_c_b,
               ln_c_g=v_ln_c_g, ln_c_b=v_ln_c_b, out_norm_g=v_out_norm_g, w_out=v_w_out, norm_ffn_g=v_norm_ffn_g,
               w_up=v_w_up, conv_f_w=v_conv_f_w, w_down=v_w_down, rel_bias=v_rel_bias, final_g=v_final_g)
    depth = w_in.shape[0]
    me = 4 * lax.axis_index("x") + 2 * lax.axis_index("y") + lax.axis_index("c")
    for k in TRANSPOSED:
        w[k], mom[k], var[k] = (jnp.transpose(a, (0, 2, 1)) for a in (w[k], mom[k], var[k]))

    me_arr = me.astype(jnp.int32).reshape(1)
    bufs = [_cast_blocks([w[k] for k in BIG], 0, me_arr, None, "l0_cast_blocks")]
    (in_sems,), (in_bufs,), token = _gather_start([bufs[0][:1]], me_arr, "l0_gather_start_in")
    bufs += [_cast_blocks([w[k] for k in BIG], l, me_arr, token, f"l{l}_cast_blocks") for l in range(1, depth)]
    params = {}
    small_shapes = [w[k].shape for k in SMALL_SHARDED]
    gs = _all_gather(_pack([w[k] for k in SMALL_SHARDED]), "gather_small", after=token)
    parts = [_unpack(gs[j], small_shapes) for j in range(N_DEV)]
    for n, k in enumerate(SMALL_SHARDED):
        params[k] = _gather_shards(jnp.stack([parts[j][n] for j in range(N_DEV)]), w[k].shape)
    for k in ("norm_mix_g", "conv_c_b", "ln_c_g", "ln_c_b", "out_norm_g", "norm_ffn_g"):
        params[k] = w[k][:, None, :]
    params["rel_bias"] = rel_bias
    params["final_g"] = final_g[None, :]

    def full(k, buf, l):
        return buf.reshape(N_DEV * buf.shape[1], buf.shape[2])

    bias = _bias_tables(rel_bias, "bias_tables").reshape(len(DILATIONS), 4, 2 * BLK, 2 * BLK)
    xl, after = x[0], gs
    layer_params, saved = [], []
    for l in range(depth):
        _, in_bufs = _copies_wait(in_bufs, in_bufs, in_sems, after, f"l{l}_gather_wait_in")
        token = None
        if l == 0:
            (rest_sems,), (rest_bufs,), token = _gather_start([bufs[0][1:]], in_bufs[0], "l0_gather_start_rest")
        p = {k: params[k][l] for k in SMALL_LAYER_PARAMS}
        p["w_in"] = full("w_in", in_bufs[0], l)
        y, sv = _layer_fwd_mix(xl, p, bias, l, after=token)
        _, rest_bufs = _copies_wait(rest_bufs, rest_bufs, rest_sems, y, f"l{l}_gather_wait_rest")
        for k, buf in zip(BIG[1:], rest_bufs):
            p[k] = full(k, buf, l)
        token = None
        if l + 1 < depth:
            (in_sems, rest_sems), (in_bufs, rest_bufs), token = _gather_start([bufs[l + 1][:1], bufs[l + 1][1:]], rest_bufs[0],
                                                                               f"l{l + 1}_gather_start")
        xl, sv2 = _layer_fwd_ffn(xl, y, p, l, after=token)
        layer_params.append(p)
        saved.append({**sv, **sv2})
        after = xl
    loss, dx, d_final = _final_loss(xl, params["final_g"], loss_target[0], "final_loss")
    loss = lax.psum(loss[0, 0], ("x", "y", "c"))

    grads, tabs, pending = [None] * depth, [None] * depth, []
    after = None
    for l in reversed(range(depth)):
        dx1, g_ffn = _layer_bwd_ffn(dx, layer_params[l], saved[l], l, after=after)
        names = ("w_up", "w_down")
        sends = [g_ffn[k] for k in names]
        sems, sends, lands, after = _exchange_start(sends, [lax.empty(s.shape, s.dtype) for s in sends], f"l{l}_exchange_start_ffn")
        pending.append((l, "ffn", names, sems, sends, lands))
        dx, g_mix, tabs[l] = _layer_bwd_mix(dx1, layer_params[l], saved[l], bias, l, after=after)
        names = ("w_out", "w_in")
        sends = [g_mix[k] for k in names]
        sems, sends, lands, after = _exchange_start(sends, [lax.empty(s.shape, s.dtype) for s in sends], f"l{l}_exchange_start_mix")
        pending.append((l, "mix", names, sems, sends, lands))
        grads[l] = {**g_ffn, **g_mix}
    grads_small = {k: jnp.stack([grads[l][k] for l in range(depth)]) for k in SMALL_LAYER_PARAMS}
    grads_small["rel_bias"] = _bias_grad(jnp.stack(tabs).reshape(depth, len(DILATIONS), 8, BLK, 2 * BLK), "bias_grad")[:, :8]
    grads_small["final_g"] = d_final

    out_g, out_d, out_m, out_v = {}, {}, {}, {}
    results = {k: [lax.empty(w[k].shape, f32) for _ in range(4)] for k in BIG}
    after = dx
    for l, part, names, sems, sends, lands in pending:
        sends, lands = _copies_wait(sends, lands, sems, after, f"l{l}_exchange_wait_{part}")
        for k, send, land in zip(names, sends, lands):
            results[k] = _adamw_layer(land, send, w[k], mom[k], var[k], results[k], l, ADAM_ROWS[k], f"l{l}_adamw_{k}")
            after = results[k][0]
    for k in BIG:
        res = [jnp.transpose(a, (0, 2, 1)) for a in results[k]] if k in TRANSPOSED else results[k]
        out_g[k], out_d[k], out_m[k], out_v[k] = res
    grads = grads_small

    small = SMALL_SHARDED + SMALL_REPLICATED
    full_shapes = [params[k].shape if k in SMALL_SHARDED else w[k].shape for k in small]
    gsum = _sum_parts(_all_gather(_pack([grads[k].reshape(shp) for k, shp in zip(small, full_shapes)]), "gather_small_grads"),
                      "sum_small_grads")
    gfull = dict(zip(small, _unpack(gsum, full_shapes)))
    for k in SMALL_SHARDED:
        shp = w[k].shape
        gfull[k] = lax.dynamic_index_in_dim(gfull[k].reshape(shp[:-1] + (N_DEV, shp[-1])), me, axis=len(shp) - 1, keepdims=False)
    local_shapes = [w[k].shape for k in small]
    res = _adamw_small(_pack([gfull[k] for k in small]), _pack([w[k] for k in small]), _pack([mom[k] for k in small]),
                       _pack([var[k] for k in small]), "adamw_small")
    for name_map, packed in zip((out_d, out_m, out_v), res):
        name_map.update(zip(small, _unpack(packed, local_shapes)))
    out_g.update(gfull)

    return (loss, dx[None], *[out_g[k] for k in WEIGHTS], *[out_d[k] for k in WEIGHTS], *[out_m[k] for k in WEIGHTS],
            *[out_v[k] for k in WEIGHTS])
```

```python
import math

import numpy as np
import jax
import jax.numpy as jnp
from jax import lax
from jax.experimental import pallas as pl
from jax.experimental.pallas import tpu as pltpu

f32, bf16 = jnp.float32, jnp.bfloat16
SDS = jax.ShapeDtypeStruct
MESH = pl.DeviceIdType.MESH

N_DEV = 8
EPS = 1e-6
NEG = -1e30
D_HEAD = 64
BLK = 128
N_BUCKETS = 32
DILATIONS = (1, 4, 16)
N_KEYS = 128
K_SHORT, K_CONF, K_FFN = 3, 31, 3
ROW_TILE = 256
HALO = 32
HALO_FFN = 8
FFN_FWD_ROWS, FFN_BWD_ROWS = 1024, 512
VMEM_LIMIT = 56 << 20
ADAM_LR, ADAM_B1, ADAM_B2, ADAM_EPS, ADAM_WD, ADAM_STEP = 0.001, 0.9, 0.999, 1e-08, 0.01, 10


def _pcall(body, **kw):
    return pl.pallas_call(body, **kw)


def _cp(sem=None):
    kw = dict(vmem_limit_bytes=VMEM_LIMIT)
    if sem is not None:
        kw["dimension_semantics"] = sem
    return pltpu.CompilerParams(**kw)


def _t5_buckets():
    rel = np.arange(BLK)[:, None] - np.arange(2 * BLK)[None, :] + BLK
    out = []
    for d in DILATIONS:
        dist = np.maximum(rel, 0) * d
        max_exact = N_BUCKETS // 2
        d_f = np.maximum(dist, 1).astype(np.float32)
        large = max_exact + (np.log(d_f / np.float32(max_exact)) / np.float32(math.log(2048 / max_exact))
                             * np.float32(N_BUCKETS - max_exact)).astype(np.int32)
        large = np.minimum(large, N_BUCKETS - 1)
        out.append(np.where(dist < max_exact, dist, large).astype(np.int32))
    return np.stack(out)


_BUCKETS = _t5_buckets()


def _mm(a, b, mode, tr, tc, name, res=None, out_dtype=f32, after=None, b_outer=False, norm_a=None, norm_b=None, rms_bwd=None):
    if rms_bwd is not None:
        return _mm_rms_bwd(a, b, tr, name, *rms_bwd)
    ij = (lambda f: (lambda j, i: f(i, j))) if b_outer else (lambda f: f)
    if mode == "nn":
        (m, k), n = a.shape, b.shape[1]
        a_spec = pl.BlockSpec((tr, k), ij(lambda i, j: (i, 0)))
        b_spec = pl.BlockSpec((k, tc), ij(lambda i, j: (0, j)))
        dims = (((1,), (0,)), ((), ()))
    elif mode == "nt":
        (m, k), n = a.shape, b.shape[0]
        a_spec = pl.BlockSpec((tr, k), ij(lambda i, j: (i, 0)))
        b_spec = pl.BlockSpec((tc, k), ij(lambda i, j: (j, 0)))
        dims = (((1,), (1,)), ((), ()))
    else:
        (k, m), n = a.shape, b.shape[1]
        a_spec = pl.BlockSpec((k, tr), ij(lambda i, j: (0, i)))
        b_spec = pl.BlockSpec((k, tc), ij(lambda i, j: (0, j)))
        dims = (((0,), (0,)), ((), ()))
    assert m % tr == 0 and n % tc == 0, (name, m, n, tr, tc)
    o_spec = pl.BlockSpec((tr, tc), ij(lambda i, j: (i, j)))
    grid = (n // tc, m // tr) if b_outer else (m // tr, n // tc)

    gain = norm_a if norm_a is not None else norm_b
    assert norm_b is None or (mode == "tn" and tc == n and not b_outer)

    def body(a_ref, b_ref, *rest):
        o_ref = rest[-2] if norm_b is not None else rest[-1]
        av = a_ref[...]
        if norm_a is not None:
            av = av * _rstd(av) * rest[0][...]
        if norm_b is not None:
            hb_sc = rest[-1]

            @pl.when(pl.program_id(0) == 0)
            def _():
                bv = b_ref[...]
                hb_sc[...] = (bv * _rstd(bv) * rest[0][...]).astype(bf16)

            bv = hb_sc[...]
        else:
            bv = b_ref[...].astype(bf16)
        acc = lax.dot_general(av.astype(bf16), bv, dims, preferred_element_type=f32)
        if res is not None:
            acc = acc + rest[1 if gain is not None else 0][...]
        o_ref[...] = acc.astype(out_dtype)

    vec = pl.BlockSpec((1, k if norm_a is not None else n), lambda i, j: (0, 0))
    ins = [a, b] + ([gain] if gain is not None else []) + ([res] if res is not None else []) + ([after] if after is not None else [])
    specs = ([a_spec, b_spec] + ([vec] if gain is not None else []) + ([o_spec] if res is not None else [])
             + ([pl.BlockSpec(memory_space=pl.ANY)] if after is not None else []))
    scratch = [pltpu.VMEM((k, n), bf16)] if norm_b is not None else []
    sem = ("arbitrary", "arbitrary") if norm_b is not None else ("parallel", "parallel")
    return _pcall(body, name=name, out_shape=SDS((m, n), out_dtype), grid=grid, in_specs=specs, out_specs=o_spec,
                  scratch_shapes=scratch, compiler_params=_cp(sem))(*ins)


def _mm_rms_bwd(a, b, tr, name, x, g, dres):
    (m, k), n = a.shape, b.shape[1]

    def body(a_ref, b_ref, x_ref, g_ref, dres_ref, dx_ref, dg_ref):
        dh = jnp.dot(a_ref[...].astype(bf16), b_ref[...].astype(bf16), preferred_element_type=f32)
        xv = x_ref[...]
        r = _rstd(xv)
        dx_ref[...] = dres_ref[...] + _rms_dx(xv, r, g_ref[...], dh)

        @pl.when(pl.program_id(0) == 0)
        def _():
            dg_ref[...] = jnp.zeros_like(dg_ref)

        dg_ref[...] += _colsum(dh * xv * r)

    row = pl.BlockSpec((tr, n), lambda i: (i, 0))
    vec = pl.BlockSpec((1, n), lambda i: (0, 0))
    return _pcall(body, name=name, out_shape=(SDS((m, n), f32), SDS((1, n), f32)), grid=(m // tr,),
                  in_specs=[pl.BlockSpec((tr, k), lambda i: (i, 0)), pl.BlockSpec((k, n), lambda i: (0, 0)), row, vec, row],
                  out_specs=(row, vec), compiler_params=_cp(("arbitrary",)))(a, b, x, g, dres)


def _rstd(x):
    return lax.rsqrt(jnp.mean(x * x, axis=-1, keepdims=True) + EPS)


def _rms_dx(x, r, g, dy):
    dxh = dy * g
    return r * (dxh - x * (r * r) * jnp.mean(dxh * x, axis=-1, keepdims=True))


def _sigmoid(x):
    return 1.0 / (1.0 + jnp.exp(-x))


def _colsum(x):
    return jnp.sum(x, axis=0, keepdims=True)


def _final_loss(x, g, target, name):
    s, d = x.shape

    def body(x_ref, g_ref, t_ref, loss_ref, dx_ref, dg_ref):
        xv, gv = x_ref[...], g_ref[...]
        r = _rstd(xv)
        diff = xv * r * gv - t_ref[...]
        dy = diff * (1.0 / d)

        @pl.when(pl.program_id(0) == 0)
        def _():
            dg_ref[...] = jnp.zeros_like(dg_ref)
            loss_ref[...] = jnp.zeros_like(loss_ref)

        part = jnp.sum(jnp.sum(diff * diff, axis=1, keepdims=True), axis=0, keepdims=True) * (0.5 / d)
        loss_ref[...] += part
        dx_ref[...] = _rms_dx(xv, r, gv, dy)
        dg_ref[...] += _colsum(dy * xv * r)

    row = pl.BlockSpec((ROW_TILE, d), lambda i: (i, 0))
    vec = pl.BlockSpec((1, d), lambda i: (0, 0))
    one = pl.BlockSpec((8, 128), lambda i: (0, 0))
    return _pcall(body, name=name, out_shape=(SDS((8, 128), f32), SDS((s, d), f32), SDS((1, d), f32)),
                  grid=(s // ROW_TILE,), in_specs=[row, vec, row], out_specs=(one, row, vec),
                  compiler_params=_cp(("arbitrary",)))(x, g, target)


Z_AH, Z_AB, Z_AC, Z_CV, Z_CG = 0, 1, 2, 9, 10
Z_Q128, Z_K128, Z_V128 = 6, 10, 14


def _cur(t, w, col):
    return pl.BlockSpec((t, w), lambda i: (i, col))


def _prev(hb, t, w, col):
    return pl.BlockSpec((hb, w), lambda i: (jnp.maximum(i * (t // hb) - 1, 0), col))


def _next(hb, t, w, col, rows):
    return pl.BlockSpec((hb, w), lambda i: (jnp.minimum((i + 1) * (t // hb), rows // hb - 1), col))


def _layernorm_parts(u, g, b):
    mu = jnp.mean(u, axis=-1, keepdims=True)
    var = jnp.mean(jnp.square(u - mu), axis=-1, keepdims=True)
    rs = lax.rsqrt(var + EPS)
    xhat = (u - mu) * rs
    return xhat, rs, xhat * g + b


def _fill_shifted(rot):
    n = rot.shape[1]
    for r in range(1, 8):
        rot[r, pl.ds(0, n - 8), :] = rot[0, pl.ds(r, n - 8), :]


def _window(rot, offset, rows):
    return rot[offset % 8, pl.ds(offset - offset % 8, rows), :]


def _mixer_fwd(z, yb, conv_a, conv_c, cb, lng, lnb, og, name):
    s = z.shape[0]
    t, hb, w = ROW_TILE, HALO, 256

    def body(ah, ab, ac, cv, cg, ah_p, ac_p, cv_p, cg_p, yb_ref, wa, wc, cb_ref, lng_ref, lnb_ref, og_ref, y_ref, p_sc, u_sc):
        first = pl.program_id(0) == 0

        def prev(ref):
            return jnp.where(first, 0.0, ref[...])

        p_sc[pl.ds(0, hb), :] = prev(ah_p) * prev(ac_p)
        p_sc[pl.ds(hb, t), :] = ah[...] * ac[...]
        conv = jnp.zeros((t, w), f32)
        for k in range(K_SHORT):
            conv = conv + wa[pl.ds(k, 1), :] * p_sc[pl.ds(hb - (K_SHORT - 1) + k, t), :]
        y_a = ab[...] * conv
        u_sc[0, pl.ds(0, hb), :] = prev(cv_p) * _sigmoid(prev(cg_p))
        u_sc[0, pl.ds(hb, t), :] = cv[...] * _sigmoid(cg[...])
        _fill_shifted(u_sc)
        u = jnp.zeros((t, w), f32) + cb_ref[...]
        for k in range(K_CONF):
            u = u + wc[pl.ds(k, 1), :] * _window(u_sc, hb - (K_CONF - 1) + k, t)
        _, _, lnv = _layernorm_parts(u, lng_ref[...], lnb_ref[...])
        y_c = lnv * _sigmoid(lnv)
        ybv = yb_ref[...]
        y_ref[:, 0:256] = (y_a * _rstd(y_a) * og_ref[:, 0:256]).astype(bf16)
        y_ref[:, 256:768] = (ybv * _rstd(ybv) * og_ref[:, 256:768]).astype(bf16)
        y_ref[:, 768:1024] = (y_c * _rstd(y_c) * og_ref[:, 768:1024]).astype(bf16)

    full = lambda shape: pl.BlockSpec(shape, lambda i: tuple(0 for _ in shape))
    in_specs = [_cur(t, w, c) for c in (Z_AH, Z_AB, Z_AC, Z_CV, Z_CG)] + [_prev(hb, t, w, c) for c in (Z_AH, Z_AC, Z_CV, Z_CG)]
    in_specs += [_cur(t, 512, 0), full((K_SHORT, w)), full((K_CONF, w)), full((1, w)), full((1, w)), full((1, w)), full((1, 1024))]
    return _pcall(body, name=name, out_shape=SDS((s, 1024), bf16), grid=(s // t,), in_specs=in_specs,
                  out_specs=_cur(t, 1024, 0), scratch_shapes=[pltpu.VMEM((hb + t, w), f32), pltpu.VMEM((8, hb + t, w), f32)],
                  compiler_params=_cp(("parallel",)))(z, z, z, z, z, z, z, z, z, yb, conv_a, conv_c, cb, lng, lnb, og)


PG_CONV_A, PG_CONV_C, PG_CB, PG_LNG, PG_LNB, PG_OG_A, PG_OG_C, PG_ROWS = 0, 8, 40, 41, 42, 43, 44, 48


def _mixer_bwd(z, yb, dy, conv_a, conv_c, cb, lng, lnb, og, name):
    s = z.shape[0]
    t, hb, w = ROW_TILE, HALO, 256
    e = t + hb

    def body(ah, ab, ac, cv, cg, ah_p, ac_p, cv_p, cg_p, ah_n, ab_n, ac_n, cv_n, cg_n,
             dya, dyb1, dyb2, dyc, dya_n, dyc_n, yb_ref, wa, wc, cb_ref, lng_ref, lnb_ref, og_ref,
             dza_ref, dzc_ref, dyb_ref, pg_ref, dgb_ref, p_sc, u_sc, dc_sc, du_sc):
        i = pl.program_id(0)
        first, last = i == 0, i == pl.num_programs(0) - 1

        def prev(ref):
            return jnp.where(first, 0.0, ref[...])

        def nxt(ref):
            return jnp.where(last, 0.0, ref[...])

        def ext(cur_ref, next_ref):
            return jnp.concatenate([cur_ref[...], nxt(next_ref)], axis=0)

        @pl.when(first)
        def _():
            pg_ref[...] = jnp.zeros_like(pg_ref)
            dgb_ref[...] = jnp.zeros_like(dgb_ref)

        g_a = og_ref[:, 0:256]
        p_sc[pl.ds(0, hb), :] = prev(ah_p) * prev(ac_p)
        p_sc[pl.ds(hb, t), :] = ah[...] * ac[...]
        p_sc[pl.ds(hb + t, hb), :] = nxt(ah_n) * nxt(ac_n)
        conv = jnp.zeros((e, w), f32)
        for k in range(K_SHORT):
            conv = conv + wa[pl.ds(k, 1), :] * p_sc[pl.ds(hb - (K_SHORT - 1) + k, e), :]
        ab_e, dya_e = ext(ab, ab_n), ext(dya, dya_n)
        y_a = ab_e * conv
        r_a = _rstd(y_a)
        dy_a = _rms_dx(y_a, r_a, g_a, dya_e)
        dc_sc[...] = dy_a * ab_e
        dza_ref[:, 256:512] = (dy_a[:t] * conv[:t]).astype(bf16)
        dp = jnp.zeros((t, w), f32)
        for k in range(K_SHORT):
            dp = dp + wa[pl.ds(k, 1), :] * dc_sc[pl.ds(K_SHORT - 1 - k, t), :]
            pg_ref[pl.ds(PG_CONV_A + k, 1), :] += _colsum(dc_sc[pl.ds(0, t), :] * p_sc[pl.ds(hb - (K_SHORT - 1) + k, t), :])
        dza_ref[:, 0:256] = (dp * ac[...]).astype(bf16)
        dza_ref[:, 512:768] = (dp * ah[...]).astype(bf16)
        pg_ref[pl.ds(PG_OG_A, 1), :] += _colsum(dya[...] * y_a[:t] * r_a[:t])

        g_c = og_ref[:, 768:1024]
        u_sc[0, pl.ds(0, hb), :] = prev(cv_p) * _sigmoid(prev(cg_p))
        u_sc[0, pl.ds(hb, t), :] = cv[...] * _sigmoid(cg[...])
        u_sc[0, pl.ds(hb + t, hb), :] = nxt(cv_n) * _sigmoid(nxt(cg_n))
        _fill_shifted(u_sc)
        u = jnp.zeros((e, w), f32) + cb_ref[...]
        for k in range(K_CONF):
            u = u + wc[pl.ds(k, 1), :] * _window(u_sc, hb - (K_CONF - 1) + k, e)
        xhat, rs, lnv = _layernorm_parts(u, lng_ref[...], lnb_ref[...])
        sg = _sigmoid(lnv)
        y_c = lnv * sg
        r_c = _rstd(y_c)
        dyc_e = ext(dyc, dyc_n)
        dlnv = _rms_dx(y_c, r_c, g_c, dyc_e) * (sg * (1.0 + lnv * (1.0 - sg)))
        dxh = dlnv * lng_ref[...]
        du = rs * (dxh - jnp.mean(dxh, axis=-1, keepdims=True) - xhat * jnp.mean(dxh * xhat, axis=-1, keepdims=True))
        du_sc[0] = du
        _fill_shifted(du_sc)
        duin = jnp.zeros((t, w), f32)
        for k in range(K_CONF):
            duin = duin + wc[pl.ds(k, 1), :] * _window(du_sc, K_CONF - 1 - k, t)
            pg_ref[pl.ds(PG_CONV_C + k, 1), :] += _colsum(du[:t] * _window(u_sc, hb - (K_CONF - 1) + k, t))
        sgg = _sigmoid(cg[...])
        dzc_ref[:, 0:256] = (duin * sgg).astype(bf16)
        dzc_ref[:, 256:512] = (duin * cv[...] * sgg * (1.0 - sgg)).astype(bf16)
        pg_ref[pl.ds(PG_CB, 1), :] += _colsum(du[:t])
        pg_ref[pl.ds(PG_LNG, 1), :] += _colsum(dlnv[:t] * xhat[:t])
        pg_ref[pl.ds(PG_LNB, 1), :] += _colsum(dlnv[:t])
        pg_ref[pl.ds(PG_OG_C, 1), :] += _colsum(dyc[...] * y_c[:t] * r_c[:t])

        g_b = og_ref[:, 256:768]
        ybv = yb_ref[...]
        r_b = _rstd(ybv)
        dyb = jnp.concatenate([dyb1[...], dyb2[...]], axis=1)
        dyb_ref[...] = _rms_dx(ybv, r_b, g_b, dyb)
        dgb_ref[...] += _colsum(dyb * ybv * r_b)

    full = lambda shape: pl.BlockSpec(shape, lambda i: tuple(0 for _ in shape))
    zc = (Z_AH, Z_AB, Z_AC, Z_CV, Z_CG)
    in_specs = [_cur(t, w, c) for c in zc] + [_prev(hb, t, w, c) for c in (Z_AH, Z_AC, Z_CV, Z_CG)]
    in_specs += [_next(hb, t, w, c, s) for c in zc]
    in_specs += [_cur(t, w, c) for c in (0, 1, 2, 3)] + [_next(hb, t, w, 0, s), _next(hb, t, w, 3, s)]
    in_specs += [_cur(t, 512, 0), full((K_SHORT, w)), full((K_CONF, w)), full((1, w)), full((1, w)), full((1, w)), full((1, 1024))]
    out_shape = (SDS((s, 768), bf16), SDS((s, 512), bf16), SDS((s, 512), f32), SDS((PG_ROWS, w), f32), SDS((1, 512), f32))
    out_specs = (_cur(t, 768, 0), _cur(t, 512, 0), _cur(t, 512, 0), full((PG_ROWS, w)), full((1, 512)))
    scratch = [pltpu.VMEM((hb + t + hb, w), f32), pltpu.VMEM((8, hb + t + hb, w), f32), pltpu.VMEM((e, w), f32), pltpu.VMEM((8, e, w), f32)]
    args = [z] * 14 + [dy] * 6 + [yb, conv_a, conv_c, cb, lng, lnb, og]
    return _pcall(body, name=name, out_shape=out_shape, grid=(s // t,), in_specs=in_specs, out_specs=out_specs,
                  scratch_shapes=scratch, compiler_params=_cp(("arbitrary",)))(*args)


def _ffn_act_fwd(upre, conv_f, name):
    s, f2 = upre.shape
    t, hb, w = FFN_FWD_ROWS, HALO_FFN, 256
    nj = f2 // 2 // w

    def body(g_c, v_c, g_p, v_p, wg, wv, o_ref, g_sc, v_sc):
        first = pl.program_id(1) == 0
        g_sc[pl.ds(0, hb), :] = jnp.where(first, 0.0, g_p[...])
        v_sc[pl.ds(0, hb), :] = jnp.where(first, 0.0, v_p[...])
        g_sc[pl.ds(hb, t), :] = g_c[...]
        v_sc[pl.ds(hb, t), :] = v_c[...]
        ug = jnp.zeros((t, w), f32)
        uv = jnp.zeros((t, w), f32)
        for k in range(K_FFN):
            ug = ug + wg[pl.ds(k, 1), :] * g_sc[pl.ds(hb - (K_FFN - 1) + k, t), :]
            uv = uv + wv[pl.ds(k, 1), :] * v_sc[pl.ds(hb - (K_FFN - 1) + k, t), :]
        o_ref[...] = (ug * _sigmoid(ug) * uv).astype(bf16)

    cur = lambda off: pl.BlockSpec((t, w), lambda j, i: (i, off + j))
    prv = lambda off: pl.BlockSpec((hb, w), lambda j, i: (jnp.maximum(i * (t // hb) - 1, 0), off + j))
    wsp = lambda off: pl.BlockSpec((K_FFN, w), lambda j, i: (0, off + j))
    return _pcall(body, name=name, out_shape=SDS((s, f2 // 2), bf16), grid=(nj, s // t),
                  in_specs=[cur(0), cur(nj), prv(0), prv(nj), wsp(0), wsp(nj)],
                  out_specs=pl.BlockSpec((t, w), lambda j, i: (i, j)),
                  scratch_shapes=[pltpu.VMEM((hb + t, w), f32), pltpu.VMEM((hb + t, w), f32)],
                  compiler_params=_cp(("parallel", "parallel")))(upre, upre, upre, upre, conv_f, conv_f)


def _ffn_act_bwd(upre, dact, conv_f, name):
    s, f2 = upre.shape
    t, hb, w = FFN_BWD_ROWS, HALO_FFN, 256
    nj = f2 // 2 // w
    e = t + hb

    def body(g_c, v_c, g_p, v_p, g_n, v_n, da_c, da_n, wg, wv, dg_ref, dv_ref, dwg_ref, dwv_ref, g_sc, v_sc, dg_sc, dv_sc):
        i = pl.program_id(1)
        first, last = i == 0, i == pl.num_programs(1) - 1

        @pl.when(first)
        def _():
            dwg_ref[...] = jnp.zeros_like(dwg_ref)
            dwv_ref[...] = jnp.zeros_like(dwv_ref)

        for sc, p, c, n in ((g_sc, g_p, g_c, g_n), (v_sc, v_p, v_c, v_n)):
            sc[pl.ds(0, hb), :] = jnp.where(first, 0.0, p[...])
            sc[pl.ds(hb, t), :] = c[...]
            sc[pl.ds(hb + t, hb), :] = jnp.where(last, 0.0, n[...])
        g_taps = [g_sc[pl.ds(hb - (K_FFN - 1) + k, e), :] for k in range(K_FFN)]
        v_taps = [v_sc[pl.ds(hb - (K_FFN - 1) + k, e), :] for k in range(K_FFN)]
        ug = jnp.zeros((e, w), f32)
        uv = jnp.zeros((e, w), f32)
        for k in range(K_FFN):
            ug = ug + wg[pl.ds(k, 1), :] * g_taps[k]
            uv = uv + wv[pl.ds(k, 1), :] * v_taps[k]
        da = jnp.concatenate([da_c[...], jnp.where(last, 0.0, da_n[...])], axis=0)
        sg = _sigmoid(ug)
        dv = da * (ug * sg)
        dg = da * uv * (sg * (1.0 + ug * (1.0 - sg)))
        dv_sc[...] = dv
        dg_sc[...] = dg
        for taps, dsc, dval, wr, dref, dwref in ((g_taps, dg_sc, dg, wg, dg_ref, dwg_ref), (v_taps, dv_sc, dv, wv, dv_ref, dwv_ref)):
            acc = wr[pl.ds(K_FFN - 1, 1), :] * dval[:t]
            for k in range(K_FFN):
                if k < K_FFN - 1:
                    acc = acc + wr[pl.ds(k, 1), :] * dsc[pl.ds(K_FFN - 1 - k, t), :]
                dwref[pl.ds(k, 1), :] += _colsum(dval[:t] * taps[k][:t])
            dref[...] = acc.astype(bf16)

    cur = lambda off: pl.BlockSpec((t, w), lambda j, i: (i, off + j))
    prv = lambda off: pl.BlockSpec((hb, w), lambda j, i: (jnp.maximum(i * (t // hb) - 1, 0), off + j))
    nxt = lambda off: pl.BlockSpec((hb, w), lambda j, i: (jnp.minimum((i + 1) * (t // hb), s // hb - 1), off + j))
    wsp = lambda off: pl.BlockSpec((K_FFN, w), lambda j, i: (0, off + j))
    half = SDS((s, f2 // 2), bf16)
    dws = SDS((8, f2 // 2), f32)
    o_cur = pl.BlockSpec((t, w), lambda j, i: (i, j))
    o_dw = pl.BlockSpec((8, w), lambda j, i: (0, j))
    return _pcall(body, name=name, out_shape=(half, half, dws, dws), grid=(nj, s // t),
                  in_specs=[cur(0), cur(nj), prv(0), prv(nj), nxt(0), nxt(nj), cur(0), nxt(0), wsp(0), wsp(nj)],
                  out_specs=(o_cur, o_cur, o_dw, o_dw),
                  scratch_shapes=[pltpu.VMEM((hb + t + hb, w), f32), pltpu.VMEM((hb + t + hb, w), f32),
                                  pltpu.VMEM((e, w), f32), pltpu.VMEM((e, w), f32)],
                  compiler_params=_cp(("parallel", "arbitrary")))(upre, upre, upre, upre, upre, upre, dact, dact, conv_f, conv_f)


def _bias_tables(rel_bias, name):
    nb = len(DILATIONS)

    def body(rb_ref, idx_ref, o_ref):
        h = pl.program_id(1)
        iv = idx_ref[0]
        acc = jnp.zeros((BLK, 2 * BLK), f32)
        for b in range(N_BUCKETS):
            acc = jnp.where(iv == b, rb_ref[b, h], acc)
        o_ref[0, 0] = acc

    return _pcall(body, name=name, out_shape=SDS((nb, 8, BLK, 2 * BLK), f32), grid=(nb, 8),
                  in_specs=[pl.BlockSpec(memory_space=pltpu.SMEM), pl.BlockSpec((1, BLK, 2 * BLK), lambda br, h: (br, 0, 0))],
                  out_specs=pl.BlockSpec((1, 1, BLK, 2 * BLK), lambda br, h: (br, h, 0, 0)),
                  compiler_params=_cp(("parallel", "parallel")))(rel_bias, jnp.asarray(_BUCKETS))


def _bias_grad(ds_tabs, name):
    nl, nb = ds_tabs.shape[0], ds_tabs.shape[1]

    def body(t_ref, idx_ref, o_ref):
        br, h = pl.program_id(0), pl.program_id(1)

        @pl.when(jnp.logical_and(br == 0, h == 0))
        def _():
            o_ref[...] = jnp.zeros_like(o_ref)

        tab = t_ref[0, 0, 0]
        for l in range(1, nl):
            tab = tab + t_ref[l, 0, 0]
        iv = idx_ref[0]
        row = lax.broadcasted_iota(jnp.int32, (N_BUCKETS, 128), 0)
        col = lax.broadcasted_iota(jnp.int32, (N_BUCKETS, 128), 1)
        acc = jnp.zeros((N_BUCKETS, 128), f32)
        for b in range(N_BUCKETS):
            sel = jnp.where(iv == b, tab, 0.0)
            tot = jnp.sum(jnp.sum(sel, axis=0, keepdims=True), axis=1, keepdims=True)
            acc = acc + jnp.where(jnp.logical_and(row == b, col == h), tot, 0.0)
        o_ref[...] += acc

    return _pcall(body, name=name, out_shape=SDS((N_BUCKETS, 128), f32), grid=(nb, 8),
                  in_specs=[pl.BlockSpec((nl, 1, 1, BLK, 2 * BLK), lambda br, h: (0, br, h, 0, 0)),
                            pl.BlockSpec((1, BLK, 2 * BLK), lambda br, h: (br, 0, 0))],
                  out_specs=pl.BlockSpec((N_BUCKETS, 128), lambda br, h: (0, 0)),
                  compiler_params=_cp(("arbitrary", "arbitrary")))(ds_tabs, jnp.asarray(_BUCKETS))


def _band_masks():
    qi = lax.broadcasted_iota(jnp.int32, (2 * BLK, 2 * BLK), 0) % BLK
    kj = lax.broadcasted_iota(jnp.int32, (2 * BLK, 2 * BLK), 1)
    band = jnp.logical_and(kj >= qi, kj <= qi + N_KEYS)
    return band, kj


def _stack_heads(x, head0):
    zero = jnp.zeros_like(x)
    return jnp.concatenate([jnp.where(head0, x, zero), jnp.where(head0, zero, x)], axis=0)


def _block_rows(blk, d, nblk):
    per_res = nblk // d
    res, n = blk // per_res, blk % per_res
    cur = pl.ds(res + d * BLK * n, BLK, stride=d) if d > 1 else pl.ds(pl.multiple_of(BLK * n, BLK), BLK)
    before = jnp.maximum(n - 1, 0)
    prev = pl.ds(res + d * BLK * before, BLK, stride=d) if d > 1 else pl.ds(pl.multiple_of(BLK * before, BLK), BLK)
    return cur, prev, jnp.where(n == 0, BLK, 0)


def _attn_fwd(z, bias, name):
    s = z.shape[0]
    nblk = s // BLK

    def body(q_ref, k_ref, v_ref, b_ref, y_ref, l_ref):
        band, kj = _band_masks()
        lane = lax.broadcasted_iota(jnp.int32, (BLK, BLK), 1)
        head0 = lane < D_HEAD

        for branch, d in enumerate(DILATIONS):
            def step(blk, carry, branch=branch, d=d):
                cur, prev, first_key = _block_rows(blk, d, nblk)
                valid = jnp.logical_and(band, kj >= first_key)
                q = (q_ref[cur, :] * (D_HEAD ** -0.5)).astype(bf16)
                kk = jnp.concatenate([k_ref[prev, :], k_ref[cur, :]], axis=0).astype(bf16)
                vv = jnp.concatenate([v_ref[prev, :], v_ref[cur, :]], axis=0).astype(bf16)
                sc = lax.dot_general(_stack_heads(q, head0), kk, (((1,), (1,)), ((), ())), preferred_element_type=f32) + b_ref[branch, 0]
                sc = jnp.where(valid, sc, NEG)
                m = jnp.max(sc, axis=-1, keepdims=True)
                p = jnp.exp(sc - m)
                den = jnp.sum(p, axis=-1, keepdims=True)
                out = jnp.dot(p.astype(bf16), vv, preferred_element_type=f32) / den
                lse = m + jnp.log(den)
                out = jnp.where(head0, out[:BLK], out[BLK:])
                lse = jnp.where(head0, lse[:BLK], lse[BLK:]) + jnp.zeros((BLK, BLK), f32)
                if branch == 0:
                    y_ref[cur, :] = out
                    l_ref[cur, :] = lse
                else:
                    y_old, l_old = y_ref[cur, :], l_ref[cur, :]
                    top = jnp.maximum(l_old, lse)
                    e_old, e_new = jnp.exp(l_old - top), jnp.exp(lse - top)
                    tot = e_old + e_new
                    y_ref[cur, :] = (e_old * y_old + e_new * out) / tot
                    l_ref[cur, :] = top + jnp.log(tot)
                return carry

            lax.fori_loop(0, nblk, step, 0, unroll=4)

    col = lambda off: pl.BlockSpec((s, BLK), lambda hp: (0, off + hp))
    out = SDS((s, 4 * BLK), f32)
    nbr = len(DILATIONS)
    return _pcall(body, name=name, out_shape=(out, out), grid=(4,),
                  in_specs=[col(Z_Q128), col(Z_K128), col(Z_V128), pl.BlockSpec((nbr, 1, 2 * BLK, 2 * BLK), lambda hp: (0, hp, 0, 0))],
                  out_specs=(col(0), col(0)), compiler_params=_cp(("parallel",)))(z, z, z, bias)


def _attn_bwd(z, dyb, yb, lt, bias, name):
    s = z.shape[0]
    nblk = s // BLK
    scale = D_HEAD ** -0.5
    chunk = 256

    def body(q_ref, k_ref, v_ref, dy_ref, y_ref, lt_ref, b_ref, dq_ref, dk_ref, dv_ref, ds_ref, aq, ak, av):
        band, kj = _band_masks()
        lane = lax.broadcasted_iota(jnp.int32, (BLK, BLK), 1)
        head0 = lane < D_HEAD
        ak[...] = jnp.zeros_like(ak)
        av[...] = jnp.zeros_like(av)
        ds_ref[...] = jnp.zeros_like(ds_ref)

        for branch, d in enumerate(DILATIONS):
            def step(blk, carry, branch=branch, d=d):
                cur, prev, first_key = _block_rows(blk, d, nblk)
                valid = jnp.logical_and(band, kj >= first_key)
                q = (q_ref[cur, :] * scale).astype(bf16)
                kk = jnp.concatenate([k_ref[prev, :], k_ref[cur, :]], axis=0).astype(bf16)
                vv = jnp.concatenate([v_ref[prev, :], v_ref[cur, :]], axis=0).astype(bf16)
                dy = dy_ref[cur, :]
                lt_b = lt_ref[cur, :]
                qs = _stack_heads(q, head0)
                dys = _stack_heads(dy.astype(bf16), head0)
                dd = jnp.sum(_stack_heads(dy * y_ref[cur, :], head0), axis=-1, keepdims=True)
                lts = jnp.concatenate([lt_b[:, 0:1], lt_b[:, D_HEAD:D_HEAD + 1]], axis=0)
                sc = lax.dot_general(qs, kk, (((1,), (1,)), ((), ())), preferred_element_type=f32) + b_ref[branch, 0]
                p = jnp.where(valid, jnp.exp(sc - lts), 0.0)
                dp = lax.dot_general(dys, vv, (((1,), (1,)), ((), ())), preferred_element_type=f32)
                ds = p * (dp - dd)
                ds_ref[branch, 0] += ds
                dsb = ds.astype(bf16)
                dq = jnp.dot(dsb, kk, preferred_element_type=f32) * scale
                dkk = lax.dot_general(dsb, qs, (((0,), (0,)), ((), ())), preferred_element_type=f32)
                dvv = lax.dot_general(p.astype(bf16), dys, (((0,), (0,)), ((), ())), preferred_element_type=f32)
                dq = jnp.where(head0, dq[:BLK], dq[BLK:])
                if branch == 0:
                    aq[cur, :] = dq
                else:
                    aq[cur, :] += dq
                ak[prev, :] += dkk[:BLK]
                ak[cur, :] += dkk[BLK:]
                av[prev, :] += dvv[:BLK]
                av[cur, :] += dvv[BLK:]
                return carry

            lax.fori_loop(0, nblk, step, 0, unroll=4)

        def store(c, carry):
            rows = pl.ds(pl.multiple_of(c * chunk, chunk), chunk)
            dq_ref[rows, :] = aq[rows, :].astype(bf16)
            dk_ref[rows, :] = ak[rows, :].astype(bf16)
            dv_ref[rows, :] = av[rows, :].astype(bf16)
            return carry

        lax.fori_loop(0, s // chunk, store, 0)

    col = lambda off: pl.BlockSpec((s, BLK), lambda hp: (0, off + hp))
    nbr = len(DILATIONS)
    tabs = pl.BlockSpec((nbr, 1, 2 * BLK, 2 * BLK), lambda hp: (0, hp, 0, 0))
    half = SDS((s, 4 * BLK), bf16)
    return _pcall(body, name=name, out_shape=(half, half, half, SDS((nbr, 4, 2 * BLK, 2 * BLK), f32)), grid=(4,),
                  in_specs=[col(Z_Q128), col(Z_K128), col(Z_V128), col(0), col(0), col(0), tabs],
                  out_specs=(col(0), col(0), col(0), tabs), scratch_shapes=[pltpu.VMEM((s, BLK), f32)] * 3,
                  compiler_params=_cp(("parallel",)))(z, z, z, dyb, yb, lt, bias)


def _adamw_math(w, g, m, v):
    m = ADAM_B1 * m + (1.0 - ADAM_B1) * g
    v = ADAM_B2 * v + (1.0 - ADAM_B2) * (g * g)
    m_hat = m / (1.0 - ADAM_B1 ** ADAM_STEP)
    v_hat = v / (1.0 - ADAM_B2 ** ADAM_STEP)
    delta = -ADAM_LR * (m_hat / (jnp.sqrt(v_hat) + ADAM_EPS) + ADAM_WD * w)
    return delta, m, v


def _adamw_layer(land, send, w, m, v, prev, l, tr, name):
    _, r, c = w.shape
    assert r % tr == 0

    def body(land_ref, send_ref, w_ref, m_ref, v_ref, p0, p1, p2, p3, g_ref, d_ref, nm_ref, nv_ref):
        mx, my, mc = _my_place()
        me = jnp.zeros((tr, c), jnp.int32) + (4 * mx + 2 * my + mc)
        g = jnp.zeros((tr, c), f32)
        for j in range(N_DEV):
            g = g + jnp.where(me == j, send_ref[j], land_ref[j]).astype(f32)
        delta, nm, nv = _adamw_math(w_ref[...], g, m_ref[...], v_ref[...])
        g_ref[...] = g
        d_ref[...] = delta
        nm_ref[...] = nm
        nv_ref[...] = nv

    parts = pl.BlockSpec((N_DEV, tr, c), lambda i: (0, i, 0))
    row = pl.BlockSpec((None, tr, c), lambda i: (l, i, 0))
    anywhere = pl.BlockSpec(memory_space=pl.ANY)
    out = SDS(w.shape, f32)
    return _pcall(body, name=name, out_shape=(out, out, out, out), grid=(r // tr,),
                  in_specs=[parts, parts, row, row, row] + [anywhere] * 4, out_specs=(row, row, row, row),
                  input_output_aliases={5: 0, 6: 1, 7: 2, 8: 3}, compiler_params=_cp(("parallel",)))(land, send, w, m, v, *prev)


def _sum_parts(parts, name):
    _, r, c = parts.shape

    def body(p_ref, o_ref):
        g = p_ref[0]
        for j in range(1, N_DEV):
            g = g + p_ref[j]
        o_ref[...] = g

    return _pcall(body, name=name, out_shape=SDS((r, c), f32), compiler_params=_cp())(parts)


def _adamw_small(g, w, m, v, name):
    def body(g_ref, w_ref, m_ref, v_ref, d_ref, nm_ref, nv_ref):
        delta, nm, nv = _adamw_math(w_ref[...], g_ref[...], m_ref[...], v_ref[...])
        d_ref[...] = delta
        nm_ref[...] = nm
        nv_ref[...] = nv

    out = SDS(w.shape, f32)
    return _pcall(body, name=name, out_shape=(out, out, out), compiler_params=_cp())(g, w, m, v)


def _my_place():
    return lax.axis_index("x"), lax.axis_index("y"), lax.axis_index("c")


def _all_gather(x, name, after=None):
    r, c = x.shape

    def body(x_ref, *rest):
        out_ref, send_sems, recv_sems, local_sem = rest[-4:]
        mx, my, mc = _my_place()
        me, sibling = (mx, my, mc), (mx, my, 1 - mc)
        chips = [(1 - mx, my), (mx, 1 - my), (1 - mx, 1 - my)]

        def rows(px, py, pc):
            return out_ref.at[4 * px + 2 * py + pc]

        def copy(k, block, to, src=None):
            return pltpu.make_async_remote_copy(src_ref=rows(*block) if src is None else src, dst_ref=rows(*block),
                                                send_sem=send_sems.at[k], recv_sem=recv_sems.at[k], device_id=to, device_id_type=MESH)

        mine = pltpu.make_async_copy(x_ref, rows(*me), local_sem)
        mine.start()
        first = [copy(0, me, sibling, src=x_ref)] + [copy(1 + j, me, (*chip, mc), src=x_ref) for j, chip in enumerate(chips)]
        for cp in first:
            cp.start()
        passed = [copy(4 + j, (*chip, mc), sibling) for j, chip in enumerate(chips)]
        for j, chip in enumerate(chips):
            copy(1 + j, (*chip, mc), me).wait_recv()
            passed[j].start()
        copy(0, sibling, me).wait_recv()
        for j, chip in enumerate(chips):
            copy(4 + j, (*chip, 1 - mc), me).wait_recv()
        for cp in first + passed:
            cp.wait_send()
        mine.wait()

    ins = [x] + ([after] if after is not None else [])
    return _pcall(body, name=name, out_shape=SDS((N_DEV, r, c), x.dtype),
                  in_specs=[pl.BlockSpec(memory_space=pl.ANY)] * len(ins), out_specs=pl.BlockSpec(memory_space=pl.ANY),
                  scratch_shapes=[pltpu.SemaphoreType.DMA((7,)), pltpu.SemaphoreType.DMA((7,)), pltpu.SemaphoreType.DMA(())])(*ins)


HBM_SPEC = pl.BlockSpec(memory_space=pltpu.HBM)
SEM_SPEC = pl.BlockSpec(memory_space=pltpu.SEMAPHORE)
ANY_SPEC = pl.BlockSpec(memory_space=pl.ANY)
DATAFLOW = pltpu.SideEffectType.DATAFLOW_SIDE_EFFECTING


def _peers():
    mx, my, mc = _my_place()
    out = []
    for k in range(1, N_DEV):
        tx, ty, tc = mx ^ ((k >> 2) & 1), my ^ ((k >> 1) & 1), mc ^ (k & 1)
        out.append(((tx, ty, tc), 4 * tx + 2 * ty + tc))
    return out


def _in_hbm(arrays):
    return [pltpu.with_memory_space_constraint(a, pltpu.HBM) for a in arrays]


N_FIRST_LEG, N_SECOND_LEG = 4, 3


def _chip_peers():
    mx, my, mc = _my_place()
    return [((tx, ty, mc), 4 * tx + 2 * ty + mc) for tx, ty in ((1 - mx, my), (mx, 1 - my), (1 - mx, 1 - my))]


def _gather_start(bufs, after, name):
    n = len(bufs)

    def body(*refs):
        ins, s_sem, r_sem, token = refs[:n], refs[n + 1], refs[n + 2], refs[-1]
        mx, my, mc = _my_place()
        me = 4 * mx + 2 * my + mc
        for t in range(n):
            for dev in [(mx, my, 1 - mc)] + [d for d, _ in _chip_peers()]:
                pltpu.make_async_remote_copy(src_ref=ins[t].at[me], dst_ref=ins[t].at[me], send_sem=s_sem.at[t],
                                             recv_sem=r_sem.at[t], device_id=dev, device_id_type=MESH).start()
        token[...] = jnp.zeros_like(token)

    out_shape = ((pltpu.SemaphoreType.DMA((n,)), pltpu.SemaphoreType.DMA((n,))) + tuple(pltpu.HBM(b.shape, b.dtype) for b in bufs)
                 + (SDS((8, 128), f32),))
    out = _pcall(body, name=name, out_shape=out_shape, in_specs=(HBM_SPEC,) * n + (ANY_SPEC,),
                 out_specs=(SEM_SPEC, SEM_SPEC) + (HBM_SPEC,) * n + (pl.BlockSpec(memory_space=pltpu.VMEM),),
                 input_output_aliases={i: 2 + i for i in range(n)},
                 compiler_params=pltpu.CompilerParams(has_side_effects=DATAFLOW))(*_in_hbm(bufs), after)
    return (out[0], out[1]), list(out[2:2 + n]), out[-1]


def _gather_forward(bufs, sems, after, name):
    n = len(bufs)

    def body(*refs):
        ins, s1, r1 = refs[:n], refs[n], refs[n + 1]
        s2, r2, token = refs[n + 3], refs[n + 4], refs[-1]
        mx, my, mc = _my_place()
        sibling = (mx, my, 1 - mc)
        for t in range(n):
            legs = ins[t].at[pl.ds(0, N_FIRST_LEG)]
            first = pltpu.make_async_remote_copy(src_ref=legs, dst_ref=legs, send_sem=s1.at[t], recv_sem=r1.at[t],
                                                 device_id=sibling, device_id_type=MESH)
            first.wait_send()
            first.wait_recv()
        for t in range(n):
            for _, idx in _chip_peers():
                pltpu.make_async_remote_copy(src_ref=ins[t].at[idx], dst_ref=ins[t].at[idx], send_sem=s2.at[t],
                                             recv_sem=r2.at[t], device_id=sibling, device_id_type=MESH).start()
        token[...] = jnp.zeros_like(token)

    out_shape = ((pltpu.SemaphoreType.DMA((n,)), pltpu.SemaphoreType.DMA((n,))) + tuple(pltpu.HBM(b.shape, b.dtype) for b in bufs)
                 + (SDS((8, 128), f32),))
    out = _pcall(body, name=name, out_shape=out_shape, in_specs=(HBM_SPEC,) * n + (SEM_SPEC, SEM_SPEC, ANY_SPEC),
                 out_specs=(SEM_SPEC, SEM_SPEC) + (HBM_SPEC,) * n + (pl.BlockSpec(memory_space=pltpu.VMEM),),
                 input_output_aliases={i: 2 + i for i in range(n)},
                 compiler_params=pltpu.CompilerParams(has_side_effects=DATAFLOW))(*bufs, sems[0], sems[1], after)
    return (out[0], out[1]), list(out[2:2 + n]), out[-1]


def _exchange_start(sends, lands, name):
    nt = len(sends)

    def body(*refs):
        srcs, dsts, s_sem, r_sem, token = refs[:nt], refs[nt:2 * nt], refs[2 * nt], refs[2 * nt + 1], refs[-1]
        mx, my, mc = _my_place()
        me = 4 * mx + 2 * my + mc
        for t in range(nt):
            for dev, idx in _peers():
                pltpu.make_async_remote_copy(src_ref=srcs[t].at[idx], dst_ref=dsts[t].at[me], send_sem=s_sem.at[t],
                                             recv_sem=r_sem.at[t], device_id=dev, device_id_type=MESH).start()
        token[...] = jnp.zeros_like(token)

    both = list(sends) + list(lands)
    out_shape = (pltpu.SemaphoreType.DMA((nt,)), pltpu.SemaphoreType.DMA((nt,))) + tuple(pltpu.HBM(b.shape, b.dtype) for b in both)
    out = _pcall(body, name=name, out_shape=out_shape + (SDS((8, 128), f32),), in_specs=(HBM_SPEC,) * (2 * nt),
                 out_specs=(SEM_SPEC, SEM_SPEC) + (HBM_SPEC,) * (2 * nt) + (pl.BlockSpec(memory_space=pltpu.VMEM),),
                 input_output_aliases={i: 2 + i for i in range(2 * nt)},
                 compiler_params=pltpu.CompilerParams(has_side_effects=DATAFLOW))(*_in_hbm(both))
    return (out[0], out[1]), list(out[2:2 + nt]), list(out[2 + nt:2 + 2 * nt]), out[-1]


def _copies_wait(srcs, dsts, sems, after, name, ncopies=N_DEV - 1):
    nt = len(srcs)
    same = srcs is dsts
    bufs = list(srcs) if same else list(srcs) + list(dsts)
    nb = len(bufs)

    def body(*refs):
        s_sem, r_sem = refs[nb], refs[nb + 1]
        mx, my, mc = _my_place()
        for t in range(nt):
            src = refs[t].at[pl.ds(0, ncopies)]
            dst = src if same else refs[nt + t].at[pl.ds(0, ncopies)]
            cp = pltpu.make_async_remote_copy(src_ref=src, dst_ref=dst, send_sem=s_sem.at[t], recv_sem=r_sem.at[t],
                                              device_id=(mx, my, 1 - mc), device_id_type=MESH)
            cp.wait_send()
            cp.wait_recv()

    out = _pcall(body, name=name, out_shape=tuple(pltpu.HBM(b.shape, b.dtype) for b in bufs),
                 in_specs=(HBM_SPEC,) * nb + (SEM_SPEC, SEM_SPEC, ANY_SPEC), out_specs=(HBM_SPEC,) * nb,
                 input_output_aliases={i: i for i in range(nb)},
                 compiler_params=pltpu.CompilerParams(has_side_effects=DATAFLOW))(*bufs, sems[0], sems[1], after)
    out = list(out)
    return (out, out) if same else (out[:nt], out[nt:])


def _cast_blocks(weights, l, me, after, name):
    nt = len(weights)
    halves = 2
    extra = [] if after is None else [after]

    def body(me_ref, *refs):
        outs = refs[nt + len(extra):]
        for t in range(nt):
            outs[t][...] = refs[t][...].astype(bf16)

    in_specs, out_specs, out_shape = [], [], []
    for wt in weights:
        _, r, c = wt.shape
        tr = r // halves
        in_specs.append(pl.BlockSpec((None, tr, c), lambda i, me_ref: (l, i, 0)))
        out_specs.append(pl.BlockSpec((None, tr, c), lambda i, me_ref: (me_ref[0], i, 0)))
        out_shape.append(SDS((N_DEV, r, c), bf16))
    in_specs += [pl.BlockSpec(memory_space=pl.ANY)] * len(extra)
    grid_spec = pltpu.PrefetchScalarGridSpec(num_scalar_prefetch=1, grid=(halves,), in_specs=in_specs, out_specs=out_specs)
    return list(_pcall(body, name=name, out_shape=out_shape, grid_spec=grid_spec, compiler_params=_cp(("parallel",)))(me, *weights, *extra))

def _layer_fwd_mix(x, p, bias, l, after=None):
    tag = f"l{l}"
    z = _mm(x, p["w_in"], "nt", 512, 1408, f"{tag}_mm_in", after=after, b_outer=True, norm_a=p["norm_mix_g"])
    yb, lt = _attn_fwd(z, bias, f"{tag}_attn_fwd")
    y = _mixer_fwd(z, yb, p["conv_a_w"], p["conv_c_w"], p["conv_c_b"], p["ln_c_g"], p["ln_c_b"], p["out_norm_g"], f"{tag}_mixer_fwd")
    return y, dict(x=x, z=z, yb=yb, lt=lt, y=y)


def _layer_fwd_ffn(x, y, p, l, after=None):
    tag = f"l{l}"
    x1 = _mm(y, p["w_out"], "nn", 512, 1024, f"{tag}_mm_out", res=x, after=after)
    upre = _mm(x1, p["w_up"], "nt", 512, 1408, f"{tag}_mm_up", b_outer=True, norm_a=p["norm_ffn_g"])
    act = _ffn_act_fwd(upre, p["conv_f_w"], f"{tag}_ffn_act")
    x2 = _mm(act, p["w_down"], "nn", 512, 1024, f"{tag}_mm_down", res=x1)
    return x2, dict(x1=x1, upre=upre, act=act)


def _layer_bwd_ffn(dx2, p, sv, l, after=None):
    tag = f"l{l}"
    g = {}
    dact = _mm(dx2, p["w_down"], "nt", 512, 1408, f"{tag}_mm_dact", after=after, b_outer=True)
    dwd = _mm(sv["act"], dx2, "tn", 1408, 1024, f"{tag}_mm_dwdown", out_dtype=bf16)
    g["w_down"] = dwd.reshape(N_DEV, dwd.shape[0] // N_DEV, dwd.shape[1])
    dug, duv, dwg, dwv = _ffn_act_bwd(sv["upre"], dact, p["conv_f_w"], f"{tag}_ffn_act_bwd")
    g["conv_f_w"] = jnp.concatenate([dwg[:K_FFN], dwv[:K_FFN]], axis=1)
    dupre = jnp.concatenate([dug, duv], axis=1)
    dwu = _mm(dupre, sv["x1"], "tn", 1408, 1024, f"{tag}_mm_dwup", out_dtype=bf16, norm_b=p["norm_ffn_g"])
    g["w_up"] = dwu.reshape(N_DEV, dwu.shape[0] // N_DEV, dwu.shape[1])
    dx1, g["norm_ffn_g"] = _mm(dupre, p["w_up"], "nn", 256, 1024, f"{tag}_mm_dh2", rms_bwd=(sv["x1"], p["norm_ffn_g"], dx2))
    return dx1, g


def _layer_bwd_mix(dx1, p, sv, bias, l, after=None):
    tag = f"l{l}"
    g = {}
    dy = _mm(dx1, p["w_out"], "nt", 512, 1024, f"{tag}_mm_dy", after=after)
    dwo = _mm(sv["y"], dx1, "tn", 512, 1024, f"{tag}_mm_dwout", out_dtype=bf16)
    g["w_out"] = dwo.reshape(N_DEV, dwo.shape[0] // N_DEV, dwo.shape[1])
    dza, dzc, dyb, pg, dgb = _mixer_bwd(sv["z"], sv["yb"], dy, p["conv_a_w"], p["conv_c_w"], p["conv_c_b"], p["ln_c_g"],
                                        p["ln_c_b"], p["out_norm_g"], f"{tag}_mixer_bwd")
    g["conv_a_w"] = pg[PG_CONV_A:PG_CONV_A + K_SHORT]
    g["conv_c_w"] = pg[PG_CONV_C:PG_CONV_C + K_CONF]
    g["conv_c_b"], g["ln_c_g"], g["ln_c_b"] = pg[PG_CB:PG_CB + 1], pg[PG_LNG:PG_LNG + 1], pg[PG_LNB:PG_LNB + 1]
    g["out_norm_g"] = jnp.concatenate([pg[PG_OG_A:PG_OG_A + 1], dgb, pg[PG_OG_C:PG_OG_C + 1]], axis=1)
    dq, dk, dv, tabs = _attn_bwd(sv["z"], dyb, sv["yb"], sv["lt"], bias, f"{tag}_attn_bwd")
    dz = jnp.concatenate([dza, dq, dk, dv, dzc], axis=1)
    dwi = _mm(dz, sv["x"], "tn", 1408, 1024, f"{tag}_mm_dwin", out_dtype=bf16, norm_b=p["norm_mix_g"])
    g["w_in"] = dwi.reshape(N_DEV, dwi.shape[0] // N_DEV, dwi.shape[1])
    dx, g["norm_mix_g"] = _mm(dz, p["w_in"], "nn", 512, 1024, f"{tag}_mm_dh", rms_bwd=(sv["x"], p["norm_mix_g"], dx1))
    return dx, g, tabs


SMALL_LAYER_PARAMS = ("norm_mix_g", "conv_a_w", "conv_c_w", "conv_c_b", "ln_c_g", "ln_c_b", "out_norm_g", "norm_ffn_g", "conv_f_w")


BIG = ("w_in", "w_out", "w_up", "w_down")
TRANSPOSED = ("w_in", "w_up")
ADAM_ROWS = {"w_in": 176, "w_out": 128, "w_up": 176, "w_down": 176}
SMALL_SHARDED = ("conv_a_w", "conv_c_w", "conv_f_w")
SMALL_REPLICATED = ("norm_mix_g", "conv_c_b", "ln_c_g", "ln_c_b", "out_norm_g", "norm_ffn_g", "rel_bias", "final_g")
WEIGHTS = ("norm_mix_g", "w_in", "conv_a_w", "conv_c_w", "conv_c_b", "ln_c_g", "ln_c_b", "out_norm_g", "w_out",
           "norm_ffn_g", "w_up", "conv_f_w", "w_down", "rel_bias", "final_g")


def _pack(arrays):
    flat = jnp.concatenate([a.reshape(-1) for a in arrays])
    rows = -(-flat.shape[0] // 1024) * 8
    return jnp.pad(flat, (0, rows * 128 - flat.shape[0])).reshape(rows, 128)


def _unpack(packed, shapes):
    flat, out, at = packed.reshape(-1), [], 0
    for shp in shapes:
        n = int(np.prod(shp))
        out.append(flat[at:at + n].reshape(shp))
        at += n
    return out


def _gather_shards(g, local_shape):
    nd = len(local_shape)
    return jnp.moveaxis(g, 0, nd - 1).reshape(local_shape[:-1] + (N_DEV * local_shape[-1],))


def kernel(x, norm_mix_g, w_in, conv_a_w, conv_c_w, conv_c_b, ln_c_g, ln_c_b, out_norm_g, w_out, norm_ffn_g, w_up, conv_f_w, w_down, rel_bias, final_g, loss_target, m_norm_mix_g, m_w_in, m_conv_a_w, m_conv_c_w, m_conv_c_b, m_ln_c_g, m_ln_c_b, m_out_norm_g, m_w_out, m_norm_ffn_g, m_w_up, m_conv_f_w, m_w_down, m_rel_bias, m_final_g, v_norm_mix_g, v_w_in, v_conv_a_w, v_conv_c_w, v_conv_c_b, v_ln_c_g, v_ln_c_b, v_out_norm_g, v_w_out, v_norm_ffn_g, v_w_up, v_conv_f_w, v_w_down, v_rel_bias, v_final_g):
    w = dict(norm_mix_g=norm_mix_g, w_in=w_in, conv_a_w=conv_a_w, conv_c_w=conv_c_w, conv_c_b=conv_c_b, ln_c_g=ln_c_g,
             ln_c_b=ln_c_b, out_norm_g=out_norm_g, w_out=w_out, norm_ffn_g=norm_ffn_g, w_up=w_up, conv_f_w=conv_f_w,
             w_down=w_down, rel_bias=rel_bias, final_g=final_g)
    mom = dict(norm_mix_g=m_norm_mix_g, w_in=m_w_in, conv_a_w=m_conv_a_w, conv_c_w=m_conv_c_w, conv_c_b=m_conv_c_b,
               ln_c_g=m_ln_c_g, ln_c_b=m_ln_c_b, out_norm_g=m_out_norm_g, w_out=m_w_out, norm_ffn_g=m_norm_ffn_g,
               w_up=m_w_up, conv_f_w=m_conv_f_w, w_down=m_w_down, rel_bias=m_rel_bias, final_g=m_final_g)
    var = dict(norm_mix_g=v_norm_mix_g, w_in=v_w_in, conv_a_w=v_conv_a_w, conv_c_w=v_conv_c_w, conv_c_b=v_conv_c_b,
               ln_c_g=v_ln_c_g, ln_c_b=v_ln_c_b, out_norm_g=v_out_norm_g, w_out=v_w_out, norm_ffn_g=v_norm_ffn_g,
               w_up=v_w_up, conv_f_w=v_conv_f_w, w_down=v_w_down, rel_bias=v_rel_bias, final_g=v_final_g)
    depth = w_in.shape[0]
    me = 4 * lax.axis_index("x") + 2 * lax.axis_index("y") + lax.axis_index("c")
    for k in TRANSPOSED:
        w[k], mom[k], var[k] = (jnp.transpose(a, (0, 2, 1)) for a in (w[k], mom[k], var[k]))

    me_arr = me.astype(jnp.int32).reshape(1)
    bufs = [_cast_blocks([w[k] for k in BIG], 0, me_arr, None, "l0_cast_blocks")]
    in_sems, in_bufs, token = _gather_start(bufs[0][:1], me_arr, "l0_gather_start_in")
    bufs += [_cast_blocks([w[k] for k in BIG], l, me_arr, token, f"l{l}_cast_blocks") for l in range(1, depth)]
    params = {}
    small_shapes = [w[k].shape for k in SMALL_SHARDED]
    gs = _all_gather(_pack([w[k] for k in SMALL_SHARDED]), "gather_small", after=token)
    parts = [_unpack(gs[j], small_shapes) for j in range(N_DEV)]
    for n, k in enumerate(SMALL_SHARDED):
        params[k] = _gather_shards(jnp.stack([parts[j][n] for j in range(N_DEV)]), w[k].shape)
    for k in ("norm_mix_g", "conv_c_b", "ln_c_g", "ln_c_b", "out_norm_g", "norm_ffn_g"):
        params[k] = w[k][:, None, :]
    params["rel_bias"] = rel_bias
    params["final_g"] = final_g[None, :]

    def full(k, buf, l):
        return buf.reshape(N_DEV * buf.shape[1], buf.shape[2])

    bias = _bias_tables(rel_bias, "bias_tables").reshape(len(DILATIONS), 4, 2 * BLK, 2 * BLK)
    in_sems, in_bufs, token = _gather_forward(in_bufs, in_sems, gs, "l0_gather_forward_in")
    rest_sems, rest_bufs, token = _gather_start(bufs[0][1:], token, "l0_gather_start_rest")
    xl, after = x[0], token
    layer_params, saved = [], []
    for l in range(depth):
        _, in_bufs = _copies_wait(in_bufs, in_bufs, in_sems, after, f"l{l}_gather_wait_in", N_SECOND_LEG)
        p = {k: params[k][l] for k in SMALL_LAYER_PARAMS}
        p["w_in"] = full("w_in", in_bufs[0], l)
        token = in_bufs[0]
        if l > 0:
            rest_sems, rest_bufs, token = _gather_forward(rest_bufs, rest_sems, token, f"l{l}_gather_forward_rest")
        if l + 1 < depth:
            in_sems, in_bufs, token = _gather_start(bufs[l + 1][:1], token, f"l{l + 1}_gather_start_in")
        y, sv = _layer_fwd_mix(xl, p, bias, l, after=token)
        if l == 0:
            rest_sems, rest_bufs, _ = _gather_forward(rest_bufs, rest_sems, y, "l0_gather_forward_rest")
        _, rest_bufs = _copies_wait(rest_bufs, rest_bufs, rest_sems, y, f"l{l}_gather_wait_rest", N_SECOND_LEG)
        for k, buf in zip(BIG[1:], rest_bufs):
            p[k] = full(k, buf, l)
        token = rest_bufs[0]
        if l + 1 < depth:
            in_sems, in_bufs, token = _gather_forward(in_bufs, in_sems, token, f"l{l + 1}_gather_forward_in")
            rest_sems, rest_bufs, token = _gather_start(bufs[l + 1][1:], token, f"l{l + 1}_gather_start_rest")
        xl, sv2 = _layer_fwd_ffn(xl, y, p, l, after=token)
        layer_params.append(p)
        saved.append({**sv, **sv2})
        after = xl
    loss, dx, d_final = _final_loss(xl, params["final_g"], loss_target[0], "final_loss")
    loss = lax.psum(loss[0, 0], ("x", "y", "c"))

    grads, tabs, pending = [None] * depth, [None] * depth, []
    after = None
    for l in reversed(range(depth)):
        dx1, g_ffn = _layer_bwd_ffn(dx, layer_params[l], saved[l], l, after=after)
        names = ("w_up", "w_down")
        sends = [g_ffn[k] for k in names]
        sems, sends, lands, after = _exchange_start(sends, [lax.empty(s.shape, s.dtype) for s in sends], f"l{l}_exchange_start_ffn")
        pending.append((l, "ffn", names, sems, sends, lands))
        dx, g_mix, tabs[l] = _layer_bwd_mix(dx1, layer_params[l], saved[l], bias, l, after=after)
        names = ("w_out", "w_in")
        sends = [g_mix[k] for k in names]
        sems, sends, lands, after = _exchange_start(sends, [lax.empty(s.shape, s.dtype) for s in sends], f"l{l}_exchange_start_mix")
        pending.append((l, "mix", names, sems, sends, lands))
        grads[l] = {**g_ffn, **g_mix}
    grads_small = {k: jnp.stack([grads[l][k] for l in range(depth)]) for k in SMALL_LAYER_PARAMS}
    grads_small["rel_bias"] = _bias_grad(jnp.stack(tabs).reshape(depth, len(DILATIONS), 8, BLK, 2 * BLK), "bias_grad")[:, :8]
    grads_small["final_g"] = d_final

    out_g, out_d, out_m, out_v = {}, {}, {}, {}
    results = {k: [lax.empty(w[k].shape, f32) for _ in range(4)] for k in BIG}
    after = dx
    for l, part, names, sems, sends, lands in pending:
        sends, lands = _copies_wait(sends, lands, sems, after, f"l{l}_exchange_wait_{part}")
        for k, send, land in zip(names, sends, lands):
            results[k] = _adamw_layer(land, send, w[k], mom[k], var[k], results[k], l, ADAM_ROWS[k], f"l{l}_adamw_{k}")
            after = results[k][0]
    for k in BIG:
        res = [jnp.transpose(a, (0, 2, 1)) for a in results[k]] if k in TRANSPOSED else results[k]
        out_g[k], out_d[k], out_m[k], out_v[k] = res
    grads = grads_small

    small = SMALL_SHARDED + SMALL_REPLICATED
    full_shapes = [params[k].shape if k in SMALL_SHARDED else w[k].shape for k in small]
    gsum = _sum_parts(_all_gather(_pack([grads[k].reshape(shp) for k, shp in zip(small, full_shapes)]), "gather_small_grads"),
                      "sum_small_grads")
    gfull = dict(zip(small, _unpack(gsum, full_shapes)))
    for k in SMALL_SHARDED:
        shp = w[k].shape
        gfull[k] = lax.dynamic_index_in_dim(gfull[k].reshape(shp[:-1] + (N_DEV, shp[-1])), me, axis=len(shp) - 1, keepdims=False)
    local_shapes = [w[k].shape for k in small]
    res = _adamw_small(_pack([gfull[k] for k in small]), _pack([w[k] for k in small]), _pack([mom[k] for k in small]),
                       _pack([var[k] for k in small]), "adamw_small")
    for name_map, packed in zip((out_d, out_m, out_v), res):
        name_map.update(zip(small, _unpack(packed, local_shapes)))
    out_g.update(gfull)

    return (loss, dx[None], *[out_g[k] for k in WEIGHTS], *[out_d[k] for k in WEIGHTS], *[out_m[k] for k in WEIGHTS],
            *[out_v[k] for k in WEIGHTS])
```

```python
import math

import numpy as np
import jax
import jax.numpy as jnp
from jax import lax
from jax.experimental import pallas as pl
from jax.experimental.pallas import tpu as pltpu

f32, bf16 = jnp.float32, jnp.bfloat16
SDS = jax.ShapeDtypeStruct
MESH = pl.DeviceIdType.MESH

N_DEV = 8
EPS = 1e-6
NEG = -1e30
D_HEAD = 64
BLK = 128
N_BUCKETS = 32
DILATIONS = (1, 4, 16)
N_KEYS = 128
K_SHORT, K_CONF, K_FFN = 3, 31, 3
ROW_TILE = 256
HALO = 32
HALO_FFN = 8
FFN_FWD_ROWS, FFN_BWD_ROWS = 1024, 512
VMEM_LIMIT = 56 << 20
ADAM_LR, ADAM_B1, ADAM_B2, ADAM_EPS, ADAM_WD, ADAM_STEP = 0.001, 0.9, 0.999, 1e-08, 0.01, 10


def _pcall(body, **kw):
    return pl.pallas_call(body, **kw)


def _cp(sem=None):
    kw = dict(vmem_limit_bytes=VMEM_LIMIT)
    if sem is not None:
        kw["dimension_semantics"] = sem
    return pltpu.CompilerParams(**kw)


def _t5_buckets():
    rel = np.arange(BLK)[:, None] - np.arange(2 * BLK)[None, :] + BLK
    out = []
    for d in DILATIONS:
        dist = np.maximum(rel, 0) * d
        max_exact = N_BUCKETS // 2
        d_f = np.maximum(dist, 1).astype(np.float32)
        large = max_exact + (np.log(d_f / np.float32(max_exact)) / np.float32(math.log(2048 / max_exact))
                             * np.float32(N_BUCKETS - max_exact)).astype(np.int32)
        large = np.minimum(large, N_BUCKETS - 1)
        out.append(np.where(dist < max_exact, dist, large).astype(np.int32))
    return np.stack(out)


_BUCKETS = _t5_buckets()


def _mm(a, b, mode, tr, tc, name, res=None, out_dtype=f32, after=None, b_outer=False, norm_a=None, norm_b=None, rms_bwd=None):
    if rms_bwd is not None:
        return _mm_rms_bwd(a, b, tr, name, *rms_bwd)
    ij = (lambda f: (lambda j, i: f(i, j))) if b_outer else (lambda f: f)
    if mode == "nn":
        (m, k), n = a.shape, b.shape[1]
        a_spec = pl.BlockSpec((tr, k), ij(lambda i, j: (i, 0)))
        b_spec = pl.BlockSpec((k, tc), ij(lambda i, j: (0, j)))
        dims = (((1,), (0,)), ((), ()))
    elif mode == "nt":
        (m, k), n = a.shape, b.shape[0]
        a_spec = pl.BlockSpec((tr, k), ij(lambda i, j: (i, 0)))
        b_spec = pl.BlockSpec((tc, k), ij(lambda i, j: (j, 0)))
        dims = (((1,), (1,)), ((), ()))
    else:
        (k, m), n = a.shape, b.shape[1]
        a_spec = pl.BlockSpec((k, tr), ij(lambda i, j: (0, i)))
        b_spec = pl.BlockSpec((k, tc), ij(lambda i, j: (0, j)))
        dims = (((0,), (0,)), ((), ()))
    assert m % tr == 0 and n % tc == 0, (name, m, n, tr, tc)
    o_spec = pl.BlockSpec((tr, tc), ij(lambda i, j: (i, j)))
    grid = (n // tc, m // tr) if b_outer else (m // tr, n // tc)

    gain = norm_a if norm_a is not None else norm_b
    assert norm_b is None or (mode == "tn" and tc == n and not b_outer)
    assert norm_a is None or (mode != "tn" and not b_outer)

    def body(a_ref, b_ref, *rest):
        o_ref = rest[-2] if gain is not None else rest[-1]
        if norm_a is not None:
            ha_sc = rest[-1]

            @pl.when(pl.program_id(1) == 0)
            def _():
                av = a_ref[...]
                ha_sc[...] = (av * _rstd(av) * rest[0][...]).astype(bf16)

            av = ha_sc[...]
        else:
            av = a_ref[...].astype(bf16)
        if norm_b is not None:
            hb_sc = rest[-1]

            @pl.when(pl.program_id(0) == 0)
            def _():
                bv = b_ref[...]
                hb_sc[...] = (bv * _rstd(bv) * rest[0][...]).astype(bf16)

            bv = hb_sc[...]
        else:
            bv = b_ref[...].astype(bf16)
        acc = lax.dot_general(av, bv, dims, preferred_element_type=f32)
        if res is not None:
            acc = acc + rest[1 if gain is not None else 0][...]
        o_ref[...] = acc.astype(out_dtype)

    vec = pl.BlockSpec((1, k if norm_a is not None else n), lambda i, j: (0, 0))
    ins = [a, b] + ([gain] if gain is not None else []) + ([res] if res is not None else []) + ([after] if after is not None else [])
    specs = ([a_spec, b_spec] + ([vec] if gain is not None else []) + ([o_spec] if res is not None else [])
             + ([pl.BlockSpec(memory_space=pl.ANY)] if after is not None else []))
    scratch = [pltpu.VMEM((k, n), bf16)] if norm_b is not None else [pltpu.VMEM((tr, k), bf16)] if norm_a is not None else []
    sem = ("arbitrary", "arbitrary") if gain is not None else ("parallel", "parallel")
    return _pcall(body, name=name, out_shape=SDS((m, n), out_dtype), grid=grid, in_specs=specs, out_specs=o_spec,
                  scratch_shapes=scratch, compiler_params=_cp(sem))(*ins)


def _mm_rms_bwd(a, b, tr, name, x, g, dres):
    (m, k), n = a.shape, b.shape[1]

    def body(a_ref, b_ref, x_ref, g_ref, dres_ref, dx_ref, dg_ref):
        dh = jnp.dot(a_ref[...].astype(bf16), b_ref[...].astype(bf16), preferred_element_type=f32)
        xv = x_ref[...]
        r = _rstd(xv)
        dx_ref[...] = dres_ref[...] + _rms_dx(xv, r, g_ref[...], dh)

        @pl.when(pl.program_id(0) == 0)
        def _():
            dg_ref[...] = jnp.zeros_like(dg_ref)

        dg_ref[...] += _colsum(dh * xv * r)

    row = pl.BlockSpec((tr, n), lambda i: (i, 0))
    vec = pl.BlockSpec((1, n), lambda i: (0, 0))
    return _pcall(body, name=name, out_shape=(SDS((m, n), f32), SDS((1, n), f32)), grid=(m // tr,),
                  in_specs=[pl.BlockSpec((tr, k), lambda i: (i, 0)), pl.BlockSpec((k, n), lambda i: (0, 0), pipeline_mode=pl.Buffered(1)),
                            row, vec, row],
                  out_specs=(row, vec), compiler_params=_cp(("arbitrary",)))(a, b, x, g, dres)


def _rstd(x):
    return lax.rsqrt(jnp.mean(x * x, axis=-1, keepdims=True) + EPS)


def _rms_dx(x, r, g, dy):
    dxh = dy * g
    return r * (dxh - x * (r * r) * jnp.mean(dxh * x, axis=-1, keepdims=True))


def _sigmoid(x):
    return 1.0 / (1.0 + jnp.exp(-x))


def _colsum(x):
    return jnp.sum(x, axis=0, keepdims=True)


def _final_loss(x, g, target, name):
    s, d = x.shape

    def body(x_ref, g_ref, t_ref, loss_ref, dx_ref, dg_ref):
        xv, gv = x_ref[...], g_ref[...]
        r = _rstd(xv)
        diff = xv * r * gv - t_ref[...]
        dy = diff * (1.0 / d)

        @pl.when(pl.program_id(0) == 0)
        def _():
            dg_ref[...] = jnp.zeros_like(dg_ref)
            loss_ref[...] = jnp.zeros_like(loss_ref)

        part = jnp.sum(jnp.sum(diff * diff, axis=1, keepdims=True), axis=0, keepdims=True) * (0.5 / d)
        loss_ref[...] += part
        dx_ref[...] = _rms_dx(xv, r, gv, dy)
        dg_ref[...] += _colsum(dy * xv * r)

    row = pl.BlockSpec((ROW_TILE, d), lambda i: (i, 0))
    vec = pl.BlockSpec((1, d), lambda i: (0, 0))
    one = pl.BlockSpec((8, 128), lambda i: (0, 0))
    return _pcall(body, name=name, out_shape=(SDS((8, 128), f32), SDS((s, d), f32), SDS((1, d), f32)),
                  grid=(s // ROW_TILE,), in_specs=[row, vec, row], out_specs=(one, row, vec),
                  compiler_params=_cp(("arbitrary",)))(x, g, target)


Z_AH, Z_AB, Z_AC, Z_CV, Z_CG = 0, 1, 2, 9, 10
Z_Q128, Z_K128, Z_V128 = 6, 10, 14


def _cur(t, w, col):
    return pl.BlockSpec((t, w), lambda i: (i, col))


def _prev(hb, t, w, col):
    return pl.BlockSpec((hb, w), lambda i: (jnp.maximum(i * (t // hb) - 1, 0), col))


def _next(hb, t, w, col, rows):
    return pl.BlockSpec((hb, w), lambda i: (jnp.minimum((i + 1) * (t // hb), rows // hb - 1), col))


def _layernorm_parts(u, g, b):
    mu = jnp.mean(u, axis=-1, keepdims=True)
    var = jnp.mean(jnp.square(u - mu), axis=-1, keepdims=True)
    rs = lax.rsqrt(var + EPS)
    xhat = (u - mu) * rs
    return xhat, rs, xhat * g + b


def _fill_shifted(rot):
    n = rot.shape[1]
    for r in range(1, 8):
        rot[r, pl.ds(0, n - 8), :] = rot[0, pl.ds(r, n - 8), :]


def _window(rot, offset, rows):
    return rot[offset % 8, pl.ds(offset - offset % 8, rows), :]


def _mixer_fwd(z, yb, conv_a, conv_c, cb, lng, lnb, og, name):
    s = z.shape[0]
    t, hb, w = ROW_TILE, HALO, 256

    def body(ah, ab, ac, cv, cg, ah_p, ac_p, cv_p, cg_p, yb_ref, wa, wc, cb_ref, lng_ref, lnb_ref, og_ref, y_ref, p_sc, u_sc):
        first = pl.program_id(0) == 0

        def prev(ref):
            return jnp.where(first, 0.0, ref[...])

        p_sc[pl.ds(0, hb), :] = prev(ah_p) * prev(ac_p)
        p_sc[pl.ds(hb, t), :] = ah[...] * ac[...]
        conv = jnp.zeros((t, w), f32)
        for k in range(K_SHORT):
            conv = conv + wa[pl.ds(k, 1), :] * p_sc[pl.ds(hb - (K_SHORT - 1) + k, t), :]
        y_a = ab[...] * conv
        u_sc[0, pl.ds(0, hb), :] = prev(cv_p) * _sigmoid(prev(cg_p))
        u_sc[0, pl.ds(hb, t), :] = cv[...] * _sigmoid(cg[...])
        _fill_shifted(u_sc)
        u = jnp.zeros((t, w), f32) + cb_ref[...]
        for k in range(K_CONF):
            u = u + wc[pl.ds(k, 1), :] * _window(u_sc, hb - (K_CONF - 1) + k, t)
        _, _, lnv = _layernorm_parts(u, lng_ref[...], lnb_ref[...])
        y_c = lnv * _sigmoid(lnv)
        ybv = yb_ref[...]
        y_ref[:, 0:256] = (y_a * _rstd(y_a) * og_ref[:, 0:256]).astype(bf16)
        y_ref[:, 256:768] = (ybv * _rstd(ybv) * og_ref[:, 256:768]).astype(bf16)
        y_ref[:, 768:1024] = (y_c * _rstd(y_c) * og_ref[:, 768:1024]).astype(bf16)

    full = lambda shape: pl.BlockSpec(shape, lambda i: tuple(0 for _ in shape))
    in_specs = [_cur(t, w, c) for c in (Z_AH, Z_AB, Z_AC, Z_CV, Z_CG)] + [_prev(hb, t, w, c) for c in (Z_AH, Z_AC, Z_CV, Z_CG)]
    in_specs += [_cur(t, 512, 0), full((K_SHORT, w)), full((K_CONF, w)), full((1, w)), full((1, w)), full((1, w)), full((1, 1024))]
    return _pcall(body, name=name, out_shape=SDS((s, 1024), bf16), grid=(s // t,), in_specs=in_specs,
                  out_specs=_cur(t, 1024, 0), scratch_shapes=[pltpu.VMEM((hb + t, w), f32), pltpu.VMEM((8, hb + t, w), f32)],
                  compiler_params=_cp(("parallel",)))(z, z, z, z, z, z, z, z, z, yb, conv_a, conv_c, cb, lng, lnb, og)


PG_CONV_A, PG_CONV_C, PG_CB, PG_LNG, PG_LNB, PG_OG_A, PG_OG_C, PG_ROWS = 0, 8, 40, 41, 42, 43, 44, 48


def _mixer_bwd(z, yb, dy, conv_a, conv_c, cb, lng, lnb, og, name):
    s = z.shape[0]
    t, hb, w = ROW_TILE, HALO, 256
    e = t + hb

    def body(ah, ab, ac, cv, cg, ah_p, ac_p, cv_p, cg_p, ah_n, ab_n, ac_n, cv_n, cg_n,
             dya, dyb1, dyb2, dyc, dya_n, dyc_n, yb_ref, wa, wc, cb_ref, lng_ref, lnb_ref, og_ref,
             dza_ref, dzc_ref, dyb_ref, pg_ref, dgb_ref, p_sc, u_sc, dc_sc, du_sc):
        i = pl.program_id(0)
        first, last = i == 0, i == pl.num_programs(0) - 1

        def prev(ref):
            return jnp.where(first, 0.0, ref[...])

        def nxt(ref):
            return jnp.where(last, 0.0, ref[...])

        def ext(cur_ref, next_ref):
            return jnp.concatenate([cur_ref[...], nxt(next_ref)], axis=0)

        @pl.when(first)
        def _():
            pg_ref[...] = jnp.zeros_like(pg_ref)
            dgb_ref[...] = jnp.zeros_like(dgb_ref)

        g_a = og_ref[:, 0:256]
        p_sc[pl.ds(0, hb), :] = prev(ah_p) * prev(ac_p)
        p_sc[pl.ds(hb, t), :] = ah[...] * ac[...]
        p_sc[pl.ds(hb + t, hb), :] = nxt(ah_n) * nxt(ac_n)
        conv = jnp.zeros((e, w), f32)
        for k in range(K_SHORT):
            conv = conv + wa[pl.ds(k, 1), :] * p_sc[pl.ds(hb - (K_SHORT - 1) + k, e), :]
        ab_e, dya_e = ext(ab, ab_n), ext(dya, dya_n)
        y_a = ab_e * conv
        r_a = _rstd(y_a)
        dy_a = _rms_dx(y_a, r_a, g_a, dya_e)
        dc_sc[...] = dy_a * ab_e
        dza_ref[:, 256:512] = (dy_a[:t] * conv[:t]).astype(bf16)
        dp = jnp.zeros((t, w), f32)
        for k in range(K_SHORT):
            dp = dp + wa[pl.ds(k, 1), :] * dc_sc[pl.ds(K_SHORT - 1 - k, t), :]
            pg_ref[pl.ds(PG_CONV_A + k, 1), :] += _colsum(dc_sc[pl.ds(0, t), :] * p_sc[pl.ds(hb - (K_SHORT - 1) + k, t), :])
        dza_ref[:, 0:256] = (dp * ac[...]).astype(bf16)
        dza_ref[:, 512:768] = (dp * ah[...]).astype(bf16)
        pg_ref[pl.ds(PG_OG_A, 1), :] += _colsum(dya[...] * y_a[:t] * r_a[:t])

        g_c = og_ref[:, 768:1024]
        u_sc[0, pl.ds(0, hb), :] = prev(cv_p) * _sigmoid(prev(cg_p))
        u_sc[0, pl.ds(hb, t), :] = cv[...] * _sigmoid(cg[...])
        u_sc[0, pl.ds(hb + t, hb), :] = nxt(cv_n) * _sigmoid(nxt(cg_n))
        _fill_shifted(u_sc)
        u = jnp.zeros((e, w), f32) + cb_ref[...]
        for k in range(K_CONF):
            u = u + wc[pl.ds(k, 1), :] * _window(u_sc, hb - (K_CONF - 1) + k, e)
        xhat, rs, lnv = _layernorm_parts(u, lng_ref[...], lnb_ref[...])
        sg = _sigmoid(lnv)
        y_c = lnv * sg
        r_c = _rstd(y_c)
        dyc_e = ext(dyc, dyc_n)
        dlnv = _rms_dx(y_c, r_c, g_c, dyc_e) * (sg * (1.0 + lnv * (1.0 - sg)))
        dxh = dlnv * lng_ref[...]
        du = rs * (dxh - jnp.mean(dxh, axis=-1, keepdims=True) - xhat * jnp.mean(dxh * xhat, axis=-1, keepdims=True))
        du_sc[0] = du
        _fill_shifted(du_sc)
        duin = jnp.zeros((t, w), f32)
        for k in range(K_CONF):
            duin = duin + wc[pl.ds(k, 1), :] * _window(du_sc, K_CONF - 1 - k, t)
            pg_ref[pl.ds(PG_CONV_C + k, 1), :] += _colsum(du[:t] * _window(u_sc, hb - (K_CONF - 1) + k, t))
        sgg = _sigmoid(cg[...])
        dzc_ref[:, 0:256] = (duin * sgg).astype(bf16)
        dzc_ref[:, 256:512] = (duin * cv[...] * sgg * (1.0 - sgg)).astype(bf16)
        pg_ref[pl.ds(PG_CB, 1), :] += _colsum(du[:t])
        pg_ref[pl.ds(PG_LNG, 1), :] += _colsum(dlnv[:t] * xhat[:t])
        pg_ref[pl.ds(PG_LNB, 1), :] += _colsum(dlnv[:t])
        pg_ref[pl.ds(PG_OG_C, 1), :] += _colsum(dyc[...] * y_c[:t] * r_c[:t])

        g_b = og_ref[:, 256:768]
        ybv = yb_ref[...]
        r_b = _rstd(ybv)
        dyb = jnp.concatenate([dyb1[...], dyb2[...]], axis=1)
        dyb_ref[...] = _rms_dx(ybv, r_b, g_b, dyb)
        dgb_ref[...] += _colsum(dyb * ybv * r_b)

    full = lambda shape: pl.BlockSpec(shape, lambda i: tuple(0 for _ in shape))
    zc = (Z_AH, Z_AB, Z_AC, Z_CV, Z_CG)
    in_specs = [_cur(t, w, c) for c in zc] + [_prev(hb, t, w, c) for c in (Z_AH, Z_AC, Z_CV, Z_CG)]
    in_specs += [_next(hb, t, w, c, s) for c in zc]
    in_specs += [_cur(t, w, c) for c in (0, 1, 2, 3)] + [_next(hb, t, w, 0, s), _next(hb, t, w, 3, s)]
    in_specs += [_cur(t, 512, 0), full((K_SHORT, w)), full((K_CONF, w)), full((1, w)), full((1, w)), full((1, w)), full((1, 1024))]
    out_shape = (SDS((s, 768), bf16), SDS((s, 512), bf16), SDS((s, 512), f32), SDS((PG_ROWS, w), f32), SDS((1, 512), f32))
    out_specs = (_cur(t, 768, 0), _cur(t, 512, 0), _cur(t, 512, 0), full((PG_ROWS, w)), full((1, 512)))
    scratch = [pltpu.VMEM((hb + t + hb, w), f32), pltpu.VMEM((8, hb + t + hb, w), f32), pltpu.VMEM((e, w), f32), pltpu.VMEM((8, e, w), f32)]
    args = [z] * 14 + [dy] * 6 + [yb, conv_a, conv_c, cb, lng, lnb, og]
    return _pcall(body, name=name, out_shape=out_shape, grid=(s // t,), in_specs=in_specs, out_specs=out_specs,
                  scratch_shapes=scratch, compiler_params=_cp(("arbitrary",)))(*args)


def _ffn_act_fwd(upre, conv_f, name):
    s, f2 = upre.shape
    t, hb, w = FFN_FWD_ROWS, HALO_FFN, 256
    nj = f2 // 2 // w

    def body(g_c, v_c, g_p, v_p, wg, wv, o_ref, g_sc, v_sc):
        first = pl.program_id(1) == 0
        g_sc[pl.ds(0, hb), :] = jnp.where(first, 0.0, g_p[...])
        v_sc[pl.ds(0, hb), :] = jnp.where(first, 0.0, v_p[...])
        g_sc[pl.ds(hb, t), :] = g_c[...]
        v_sc[pl.ds(hb, t), :] = v_c[...]
        ug = jnp.zeros((t, w), f32)
        uv = jnp.zeros((t, w), f32)
        for k in range(K_FFN):
            ug = ug + wg[pl.ds(k, 1), :] * g_sc[pl.ds(hb - (K_FFN - 1) + k, t), :]
            uv = uv + wv[pl.ds(k, 1), :] * v_sc[pl.ds(hb - (K_FFN - 1) + k, t), :]
        o_ref[...] = (ug * _sigmoid(ug) * uv).astype(bf16)

    cur = lambda off: pl.BlockSpec((t, w), lambda j, i: (i, off + j))
    prv = lambda off: pl.BlockSpec((hb, w), lambda j, i: (jnp.maximum(i * (t // hb) - 1, 0), off + j))
    wsp = lambda off: pl.BlockSpec((K_FFN, w), lambda j, i: (0, off + j))
    return _pcall(body, name=name, out_shape=SDS((s, f2 // 2), bf16), grid=(nj, s // t),
                  in_specs=[cur(0), cur(nj), prv(0), prv(nj), wsp(0), wsp(nj)],
                  out_specs=pl.BlockSpec((t, w), lambda j, i: (i, j)),
                  scratch_shapes=[pltpu.VMEM((hb + t, w), f32), pltpu.VMEM((hb + t, w), f32)],
                  compiler_params=_cp(("parallel", "parallel")))(upre, upre, upre, upre, conv_f, conv_f)


def _ffn_act_bwd(upre, dact, conv_f, name):
    s, f2 = upre.shape
    t, hb, w = FFN_BWD_ROWS, HALO_FFN, 256
    nj = f2 // 2 // w
    e = t + hb

    def body(g_c, v_c, g_p, v_p, g_n, v_n, da_c, da_n, wg, wv, dg_ref, dv_ref, dwg_ref, dwv_ref, g_sc, v_sc, dg_sc, dv_sc):
        i = pl.program_id(1)
        first, last = i == 0, i == pl.num_programs(1) - 1

        @pl.when(first)
        def _():
            dwg_ref[...] = jnp.zeros_like(dwg_ref)
            dwv_ref[...] = jnp.zeros_like(dwv_ref)

        for sc, p, c, n in ((g_sc, g_p, g_c, g_n), (v_sc, v_p, v_c, v_n)):
            sc[pl.ds(0, hb), :] = jnp.where(first, 0.0, p[...])
            sc[pl.ds(hb, t), :] = c[...]
            sc[pl.ds(hb + t, hb), :] = jnp.where(last, 0.0, n[...])
        g_taps = [g_sc[pl.ds(hb - (K_FFN - 1) + k, e), :] for k in range(K_FFN)]
        v_taps = [v_sc[pl.ds(hb - (K_FFN - 1) + k, e), :] for k in range(K_FFN)]
        ug = jnp.zeros((e, w), f32)
        uv = jnp.zeros((e, w), f32)
        for k in range(K_FFN):
            ug = ug + wg[pl.ds(k, 1), :] * g_taps[k]
            uv = uv + wv[pl.ds(k, 1), :] * v_taps[k]
        da = jnp.concatenate([da_c[...], jnp.where(last, 0.0, da_n[...])], axis=0)
        sg = _sigmoid(ug)
        dv = da * (ug * sg)
        dg = da * uv * (sg * (1.0 + ug * (1.0 - sg)))
        dv_sc[...] = dv
        dg_sc[...] = dg
        for taps, dsc, dval, wr, dref, dwref in ((g_taps, dg_sc, dg, wg, dg_ref, dwg_ref), (v_taps, dv_sc, dv, wv, dv_ref, dwv_ref)):
            acc = wr[pl.ds(K_FFN - 1, 1), :] * dval[:t]
            for k in range(K_FFN):
                if k < K_FFN - 1:
                    acc = acc + wr[pl.ds(k, 1), :] * dsc[pl.ds(K_FFN - 1 - k, t), :]
                dwref[pl.ds(k, 1), :] += _colsum(dval[:t] * taps[k][:t])
            dref[...] = acc.astype(bf16)

    cur = lambda off: pl.BlockSpec((t, w), lambda j, i: (i, off + j))
    prv = lambda off: pl.BlockSpec((hb, w), lambda j, i: (jnp.maximum(i * (t // hb) - 1, 0), off + j))
    nxt = lambda off: pl.BlockSpec((hb, w), lambda j, i: (jnp.minimum((i + 1) * (t // hb), s // hb - 1), off + j))
    wsp = lambda off: pl.BlockSpec((K_FFN, w), lambda j, i: (0, off + j))
    half = SDS((s, f2 // 2), bf16)
    dws = SDS((8, f2 // 2), f32)
    o_cur = pl.BlockSpec((t, w), lambda j, i: (i, j))
    o_dw = pl.BlockSpec((8, w), lambda j, i: (0, j))
    return _pcall(body, name=name, out_shape=(half, half, dws, dws), grid=(nj, s // t),
                  in_specs=[cur(0), cur(nj), prv(0), prv(nj), nxt(0), nxt(nj), cur(0), nxt(0), wsp(0), wsp(nj)],
                  out_specs=(o_cur, o_cur, o_dw, o_dw),
                  scratch_shapes=[pltpu.VMEM((hb + t + hb, w), f32), pltpu.VMEM((hb + t + hb, w), f32),
                                  pltpu.VMEM((e, w), f32), pltpu.VMEM((e, w), f32)],
                  compiler_params=_cp(("parallel", "arbitrary")))(upre, upre, upre, upre, upre, upre, dact, dact, conv_f, conv_f)


def _bias_tables(rel_bias, name):
    nb = len(DILATIONS)

    def body(rb_ref, idx_ref, o_ref):
        variant, h = pl.program_id(1), pl.program_id(2)
        iv = idx_ref[0]
        qi = lax.broadcasted_iota(jnp.int32, (BLK, 2 * BLK), 0)
        kj = lax.broadcasted_iota(jnp.int32, (BLK, 2 * BLK), 1)
        valid = jnp.logical_and(jnp.logical_and(kj >= qi, kj <= qi + N_KEYS), kj >= variant * BLK)
        acc = jnp.zeros((BLK, 2 * BLK), f32)
        for b in range(N_BUCKETS):
            acc = jnp.where(iv == b, rb_ref[b, h], acc)
        o_ref[0, 0, 0] = jnp.where(valid, acc, NEG)

    return _pcall(body, name=name, out_shape=SDS((nb, 2, 8, BLK, 2 * BLK), f32), grid=(nb, 2, 8),
                  in_specs=[pl.BlockSpec(memory_space=pltpu.SMEM), pl.BlockSpec((1, BLK, 2 * BLK), lambda br, vr, h: (br, 0, 0))],
                  out_specs=pl.BlockSpec((1, 1, 1, BLK, 2 * BLK), lambda br, vr, h: (br, vr, h, 0, 0)),
                  compiler_params=_cp(("parallel", "parallel", "parallel")))(rel_bias, jnp.asarray(_BUCKETS))


def _bias_grad(ds_tabs, name):
    nl, nb = ds_tabs.shape[0], ds_tabs.shape[1]

    def body(t_ref, idx_ref, o_ref):
        br, h = pl.program_id(0), pl.program_id(1)

        @pl.when(jnp.logical_and(br == 0, h == 0))
        def _():
            o_ref[...] = jnp.zeros_like(o_ref)

        tab = t_ref[0, 0, 0]
        for l in range(1, nl):
            tab = tab + t_ref[l, 0, 0]
        iv = idx_ref[0]
        row = lax.broadcasted_iota(jnp.int32, (N_BUCKETS, 128), 0)
        col = lax.broadcasted_iota(jnp.int32, (N_BUCKETS, 128), 1)
        acc = jnp.zeros((N_BUCKETS, 128), f32)
        for b in range(N_BUCKETS):
            sel = jnp.where(iv == b, tab, 0.0)
            tot = jnp.sum(jnp.sum(sel, axis=0, keepdims=True), axis=1, keepdims=True)
            acc = acc + jnp.where(jnp.logical_and(row == b, col == h), tot, 0.0)
        o_ref[...] += acc

    return _pcall(body, name=name, out_shape=SDS((N_BUCKETS, 128), f32), grid=(nb, 8),
                  in_specs=[pl.BlockSpec((nl, 1, 1, BLK, 2 * BLK), lambda br, h: (0, br, h, 0, 0)),
                            pl.BlockSpec((1, BLK, 2 * BLK), lambda br, h: (br, 0, 0))],
                  out_specs=pl.BlockSpec((N_BUCKETS, 128), lambda br, h: (0, 0)),
                  compiler_params=_cp(("arbitrary", "arbitrary")))(ds_tabs, jnp.asarray(_BUCKETS))


def _stack_heads(x, head0):
    zero = jnp.zeros_like(x)
    return jnp.concatenate([jnp.where(head0, x, zero), jnp.where(head0, zero, x)], axis=0)


def _block_rows(blk, d, nblk):
    per_res = nblk // d
    res, n = blk // per_res, blk % per_res
    cur = pl.ds(res + d * BLK * n, BLK, stride=d) if d > 1 else pl.ds(pl.multiple_of(BLK * n, BLK), BLK)
    before = jnp.maximum(n - 1, 0)
    prev = pl.ds(res + d * BLK * before, BLK, stride=d) if d > 1 else pl.ds(pl.multiple_of(BLK * before, BLK), BLK)
    return cur, prev, jnp.where(n == 0, 1, 0)


def _attn_fwd(z, bias, name):
    s = z.shape[0]
    nblk = s // BLK

    def body(q_ref, k_ref, v_ref, b_ref, y_ref, l_ref):
        lane = lax.broadcasted_iota(jnp.int32, (BLK, BLK), 1)
        head0 = lane < D_HEAD

        for branch, d in enumerate(DILATIONS):
            def step(blk, carry, branch=branch, d=d):
                cur, prev, variant = _block_rows(blk, d, nblk)
                q = (q_ref[cur, :] * (D_HEAD ** -0.5)).astype(bf16)
                kk = jnp.concatenate([k_ref[prev, :], k_ref[cur, :]], axis=0).astype(bf16)
                vv = jnp.concatenate([v_ref[prev, :], v_ref[cur, :]], axis=0).astype(bf16)
                sc = lax.dot_general(_stack_heads(q, head0), kk, (((1,), (1,)), ((), ())), preferred_element_type=f32)
                sc = sc + b_ref[branch, variant, 0]
                m = jnp.max(sc, axis=-1, keepdims=True)
                p = jnp.exp(sc - m)
                den = jnp.sum(p, axis=-1, keepdims=True)
                out = jnp.dot(p.astype(bf16), vv, preferred_element_type=f32) / den
                lse = m + jnp.log(den)
                out = jnp.where(head0, out[:BLK], out[BLK:])
                lse = jnp.where(head0, lse[:BLK], lse[BLK:]) + jnp.zeros((BLK, BLK), f32)
                if branch == 0:
                    y_ref[cur, :] = out
                    l_ref[cur, :] = lse
                else:
                    y_old, l_old = y_ref[cur, :], l_ref[cur, :]
                    top = jnp.maximum(l_old, lse)
                    e_old, e_new = jnp.exp(l_old - top), jnp.exp(lse - top)
                    tot = e_old + e_new
                    y_ref[cur, :] = (e_old * y_old + e_new * out) / tot
                    l_ref[cur, :] = top + jnp.log(tot)
                return carry

            lax.fori_loop(0, nblk, step, 0, unroll=4)

    col = lambda off: pl.BlockSpec((s, BLK), lambda hp: (0, off + hp))
    out = SDS((s, 4 * BLK), f32)
    nbr = len(DILATIONS)
    return _pcall(body, name=name, out_shape=(out, out), grid=(4,),
                  in_specs=[col(Z_Q128), col(Z_K128), col(Z_V128), pl.BlockSpec((nbr, 2, 1, 2 * BLK, 2 * BLK), lambda hp: (0, 0, hp, 0, 0))],
                  out_specs=(col(0), col(0)), compiler_params=_cp(("parallel",)))(z, z, z, bias)


def _attn_bwd(z, dyb, yb, lt, bias, name):
    s = z.shape[0]
    nblk = s // BLK
    scale = D_HEAD ** -0.5
    chunk = 256

    def body(q_ref, k_ref, v_ref, dy_ref, y_ref, lt_ref, b_ref, dq_ref, dk_ref, dv_ref, ds_ref, aq, ak, av):
        lane = lax.broadcasted_iota(jnp.int32, (BLK, BLK), 1)
        head0 = lane < D_HEAD
        ak[...] = jnp.zeros_like(ak)
        av[...] = jnp.zeros_like(av)
        ds_ref[...] = jnp.zeros_like(ds_ref)

        for branch, d in enumerate(DILATIONS):
            def step(blk, carry, branch=branch, d=d):
                cur, prev, variant = _block_rows(blk, d, nblk)
                q = (q_ref[cur, :] * scale).astype(bf16)
                kk = jnp.concatenate([k_ref[prev, :], k_ref[cur, :]], axis=0).astype(bf16)
                vv = jnp.concatenate([v_ref[prev, :], v_ref[cur, :]], axis=0).astype(bf16)
                dy = dy_ref[cur, :]
                lt_b = lt_ref[cur, :]
                qs = _stack_heads(q, head0)
                dys = _stack_heads(dy.astype(bf16), head0)
                dd = jnp.sum(_stack_heads(dy * y_ref[cur, :], head0), axis=-1, keepdims=True)
                lts = jnp.concatenate([lt_b[:, 0:1], lt_b[:, D_HEAD:D_HEAD + 1]], axis=0)
                sc = lax.dot_general(qs, kk, (((1,), (1,)), ((), ())), preferred_element_type=f32)
                p = jnp.exp(sc + b_ref[branch, variant, 0] - lts)
                dp = lax.dot_general(dys, vv, (((1,), (1,)), ((), ())), preferred_element_type=f32)
                ds = p * (dp - dd)
                ds_ref[branch, 0] += ds
                dsb = ds.astype(bf16)
                dq = jnp.dot(dsb, kk, preferred_element_type=f32) * scale
                dkk = lax.dot_general(dsb, qs, (((0,), (0,)), ((), ())), preferred_element_type=f32)
                dvv = lax.dot_general(p.astype(bf16), dys, (((0,), (0,)), ((), ())), preferred_element_type=f32)
                dq = jnp.where(head0, dq[:BLK], dq[BLK:])
                if branch == 0:
                    aq[cur, :] = dq
                else:
                    aq[cur, :] += dq
                ak[prev, :] += dkk[:BLK]
                ak[cur, :] += dkk[BLK:]
                av[prev, :] += dvv[:BLK]
                av[cur, :] += dvv[BLK:]
                return carry

            lax.fori_loop(0, nblk, step, 0, unroll=4)

        def store(c, carry):
            rows = pl.ds(pl.multiple_of(c * chunk, chunk), chunk)
            dq_ref[rows, :] = aq[rows, :].astype(bf16)
            dk_ref[rows, :] = ak[rows, :].astype(bf16)
            dv_ref[rows, :] = av[rows, :].astype(bf16)
            return carry

        lax.fori_loop(0, s // chunk, store, 0)

    col = lambda off: pl.BlockSpec((s, BLK), lambda hp: (0, off + hp))
    nbr = len(DILATIONS)
    tabs = pl.BlockSpec((nbr, 1, 2 * BLK, 2 * BLK), lambda hp: (0, hp, 0, 0))
    bias_spec = pl.BlockSpec((nbr, 2, 1, 2 * BLK, 2 * BLK), lambda hp: (0, 0, hp, 0, 0))
    half = SDS((s, 4 * BLK), bf16)
    return _pcall(body, name=name, out_shape=(half, half, half, SDS((nbr, 4, 2 * BLK, 2 * BLK), f32)), grid=(4,),
                  in_specs=[col(Z_Q128), col(Z_K128), col(Z_V128), col(0), col(0), col(0), bias_spec],
                  out_specs=(col(0), col(0), col(0), tabs), scratch_shapes=[pltpu.VMEM((s, BLK), f32)] * 3,
                  compiler_params=_cp(("parallel",)))(z, z, z, dyb, yb, lt, bias)


def _adamw_math(w, g, m, v):
    m = ADAM_B1 * m + (1.0 - ADAM_B1) * g
    v = ADAM_B2 * v + (1.0 - ADAM_B2) * (g * g)
    m_hat = m / (1.0 - ADAM_B1 ** ADAM_STEP)
    v_hat = v / (1.0 - ADAM_B2 ** ADAM_STEP)
    delta = -ADAM_LR * (m_hat / (jnp.sqrt(v_hat) + ADAM_EPS) + ADAM_WD * w)
    return delta, m, v


def _adamw_layer(land, send, w, m, v, prev, l, tr, me, name):
    _, r, c = w.shape
    assert r % tr == 0

    def body(me_ref, land_ref, own_ref, w_ref, m_ref, v_ref, p0, p1, p2, p3, g_ref, d_ref, nm_ref, nv_ref):
        mine = jnp.zeros((tr, c), jnp.int32) + me_ref[0]
        own = own_ref[...]
        g = jnp.zeros((tr, c), f32)
        for j in range(N_DEV):
            g = g + jnp.where(mine == j, own, land_ref[j]).astype(f32)
        delta, nm, nv = _adamw_math(w_ref[...], g, m_ref[...], v_ref[...])
        g_ref[...] = g
        d_ref[...] = delta
        nm_ref[...] = nm
        nv_ref[...] = nv

    parts = pl.BlockSpec((N_DEV, tr, c), lambda i, me_ref: (0, i, 0))
    own = pl.BlockSpec((None, tr, c), lambda i, me_ref: (me_ref[0], i, 0))
    row = pl.BlockSpec((None, tr, c), lambda i, me_ref: (l, i, 0))
    anywhere = pl.BlockSpec(memory_space=pl.ANY)
    out = SDS(w.shape, f32)
    grid_spec = pltpu.PrefetchScalarGridSpec(num_scalar_prefetch=1, grid=(r // tr,), in_specs=[parts, own, row, row, row] + [anywhere] * 4,
                                             out_specs=(row, row, row, row))
    return _pcall(body, name=name, out_shape=(out, out, out, out), grid_spec=grid_spec,
                  input_output_aliases={6: 0, 7: 1, 8: 2, 9: 3}, compiler_params=_cp(("parallel",)))(me, land, send, w, m, v, *prev)


def _sum_parts(parts, name):
    _, r, c = parts.shape

    def body(p_ref, o_ref):
        g = p_ref[0]
        for j in range(1, N_DEV):
            g = g + p_ref[j]
        o_ref[...] = g

    return _pcall(body, name=name, out_shape=SDS((r, c), f32), compiler_params=_cp())(parts)


def _adamw_small(gs, ws, ms, vs, name):
    n = len(ws)

    def body(*refs):
        for t in range(n):
            delta, nm, nv = _adamw_math(refs[n + t][...], refs[t][...], refs[2 * n + t][...], refs[3 * n + t][...])
            refs[4 * n + t][...] = delta
            refs[5 * n + t][...] = nm
            refs[6 * n + t][...] = nv

    out = [SDS(wt.shape, f32) for wt in ws] * 3
    res = _pcall(body, name=name, out_shape=out, compiler_params=_cp())(*gs, *ws, *ms, *vs)
    return list(res[:n]), list(res[n:2 * n]), list(res[2 * n:])


def _my_place():
    return lax.axis_index("x"), lax.axis_index("y"), lax.axis_index("c")


def _all_gather(xs, name, after=None):
    n = len(xs)

    def body(*refs):
        x_refs, out_refs = refs[:n], refs[-3 - n:-3]
        send_sems, recv_sems, local_sems = refs[-3:]
        mx, my, mc = _my_place()
        me, sibling = (mx, my, mc), (mx, my, 1 - mc)
        chips = [(1 - mx, my), (mx, 1 - my), (1 - mx, 1 - my)]

        def rows(t, px, py, pc):
            return out_refs[t].at[4 * px + 2 * py + pc]

        def copy(t, k, block, to, src=None):
            return pltpu.make_async_remote_copy(src_ref=rows(t, *block) if src is None else src, dst_ref=rows(t, *block),
                                                send_sem=send_sems.at[7 * t + k], recv_sem=recv_sems.at[7 * t + k],
                                                device_id=to, device_id_type=MESH)

        mine = [pltpu.make_async_copy(x_refs[t], rows(t, *me), local_sems.at[t]) for t in range(n)]
        first, passed = [], []
        for t in range(n):
            mine[t].start()
            first += [copy(t, 0, me, sibling, src=x_refs[t])] + [copy(t, 1 + j, me, (*chip, mc), src=x_refs[t]) for j, chip in enumerate(chips)]
        for cp in first:
            cp.start()
        for t in range(n):
            for j, chip in enumerate(chips):
                copy(t, 1 + j, (*chip, mc), me).wait_recv()
                passed.append(copy(t, 4 + j, (*chip, mc), sibling))
                passed[-1].start()
        for t in range(n):
            copy(t, 0, sibling, me).wait_recv()
            for j, chip in enumerate(chips):
                copy(t, 4 + j, (*chip, 1 - mc), me).wait_recv()
        for cp in first + passed:
            cp.wait_send()
        for t in range(n):
            mine[t].wait()

    ins = list(xs) + ([after] if after is not None else [])
    anywhere = pl.BlockSpec(memory_space=pl.ANY)
    return list(_pcall(body, name=name, out_shape=[SDS((N_DEV,) + x.shape, x.dtype) for x in xs],
                       in_specs=[anywhere] * len(ins), out_specs=[anywhere] * n,
                       scratch_shapes=[pltpu.SemaphoreType.DMA((7 * n,)), pltpu.SemaphoreType.DMA((7 * n,)),
                                       pltpu.SemaphoreType.DMA((n,))])(*ins))


HBM_SPEC = pl.BlockSpec(memory_space=pltpu.HBM)
SEM_SPEC = pl.BlockSpec(memory_space=pltpu.SEMAPHORE)
ANY_SPEC = pl.BlockSpec(memory_space=pl.ANY)
DATAFLOW = pltpu.SideEffectType.DATAFLOW_SIDE_EFFECTING


def _peers():
    mx, my, mc = _my_place()
    out = []
    for k in range(1, N_DEV):
        tx, ty, tc = mx ^ ((k >> 2) & 1), my ^ ((k >> 1) & 1), mc ^ (k & 1)
        out.append(((tx, ty, tc), 4 * tx + 2 * ty + tc))
    return out


def _in_hbm(arrays):
    return [pltpu.with_memory_space_constraint(a, pltpu.HBM) for a in arrays]


N_FIRST_LEG, N_SECOND_LEG = 4, 3


def _chip_peers():
    mx, my, mc = _my_place()
    return [((tx, ty, mc), 4 * tx + 2 * ty + mc) for tx, ty in ((1 - mx, my), (mx, 1 - my), (1 - mx, 1 - my))]


def _gather_start(bufs, after, name):
    n = len(bufs)

    def body(*refs):
        ins, s_sem, r_sem, token = refs[:n], refs[n + 1], refs[n + 2], refs[-1]
        mx, my, mc = _my_place()
        me = 4 * mx + 2 * my + mc
        for t in range(n):
            for dev in [(mx, my, 1 - mc)] + [d for d, _ in _chip_peers()]:
                pltpu.make_async_remote_copy(src_ref=ins[t].at[me], dst_ref=ins[t].at[me], send_sem=s_sem.at[t],
                                             recv_sem=r_sem.at[t], device_id=dev, device_id_type=MESH).start()
        token[...] = jnp.zeros_like(token)

    out_shape = ((pltpu.SemaphoreType.DMA((n,)), pltpu.SemaphoreType.DMA((n,))) + tuple(pltpu.HBM(b.shape, b.dtype) for b in bufs)
                 + (SDS((8, 128), f32),))
    out = _pcall(body, name=name, out_shape=out_shape, in_specs=(HBM_SPEC,) * n + (ANY_SPEC,),
                 out_specs=(SEM_SPEC, SEM_SPEC) + (HBM_SPEC,) * n + (pl.BlockSpec(memory_space=pltpu.VMEM),),
                 input_output_aliases={i: 2 + i for i in range(n)},
                 compiler_params=pltpu.CompilerParams(has_side_effects=DATAFLOW))(*_in_hbm(bufs), after)
    return (out[0], out[1]), list(out[2:2 + n]), out[-1]


def _gather_forward(bufs, sems, after, name):
    n = len(bufs)

    def body(*refs):
        ins, s1, r1 = refs[:n], refs[n], refs[n + 1]
        s2, r2, token = refs[n + 3], refs[n + 4], refs[-1]
        mx, my, mc = _my_place()
        sibling = (mx, my, 1 - mc)
        for t in range(n):
            legs = ins[t].at[pl.ds(0, N_FIRST_LEG)]
            first = pltpu.make_async_remote_copy(src_ref=legs, dst_ref=legs, send_sem=s1.at[t], recv_sem=r1.at[t],
                                                 device_id=sibling, device_id_type=MESH)
            first.wait_send()
            first.wait_recv()
        for t in range(n):
            for _, idx in _chip_peers():
                pltpu.make_async_remote_copy(src_ref=ins[t].at[idx], dst_ref=ins[t].at[idx], send_sem=s2.at[t],
                                             recv_sem=r2.at[t], device_id=sibling, device_id_type=MESH).start()
        token[...] = jnp.zeros_like(token)

    out_shape = ((pltpu.SemaphoreType.DMA((n,)), pltpu.SemaphoreType.DMA((n,))) + tuple(pltpu.HBM(b.shape, b.dtype) for b in bufs)
                 + (SDS((8, 128), f32),))
    out = _pcall(body, name=name, out_shape=out_shape, in_specs=(HBM_SPEC,) * n + (SEM_SPEC, SEM_SPEC, ANY_SPEC),
                 out_specs=(SEM_SPEC, SEM_SPEC) + (HBM_SPEC,) * n + (pl.BlockSpec(memory_space=pltpu.VMEM),),
                 input_output_aliases={i: 2 + i for i in range(n)},
                 compiler_params=pltpu.CompilerParams(has_side_effects=DATAFLOW))(*bufs, sems[0], sems[1], after)
    return (out[0], out[1]), list(out[2:2 + n]), out[-1]


def _exchange_start(sends, lands, name):
    nt = len(sends)

    def body(*refs):
        srcs, dsts, s_sem, r_sem, token = refs[:nt], refs[nt:2 * nt], refs[2 * nt], refs[2 * nt + 1], refs[-1]
        mx, my, mc = _my_place()
        me = 4 * mx + 2 * my + mc
        for t in range(nt):
            for dev, idx in _peers():
                pltpu.make_async_remote_copy(src_ref=srcs[t].at[idx], dst_ref=dsts[t].at[me], send_sem=s_sem.at[t],
                                             recv_sem=r_sem.at[t], device_id=dev, device_id_type=MESH).start()
        token[...] = jnp.zeros_like(token)

    both = list(sends) + list(lands)
    out_shape = (pltpu.SemaphoreType.DMA((nt,)), pltpu.SemaphoreType.DMA((nt,))) + tuple(pltpu.HBM(b.shape, b.dtype) for b in both)
    out = _pcall(body, name=name, out_shape=out_shape + (SDS((8, 128), f32),), in_specs=(HBM_SPEC,) * (2 * nt),
                 out_specs=(SEM_SPEC, SEM_SPEC) + (HBM_SPEC,) * (2 * nt) + (pl.BlockSpec(memory_space=pltpu.VMEM),),
                 input_output_aliases={i: 2 + i for i in range(2 * nt)},
                 compiler_params=pltpu.CompilerParams(has_side_effects=DATAFLOW))(*_in_hbm(both))
    return (out[0], out[1]), list(out[2:2 + nt]), list(out[2 + nt:2 + 2 * nt]), out[-1]


def _copies_wait(srcs, dsts, sems, after, name, ncopies=N_DEV - 1):
    nt = len(srcs)
    same = srcs is dsts
    bufs = list(srcs) if same else list(srcs) + list(dsts)
    nb = len(bufs)

    def body(*refs):
        s_sem, r_sem = refs[nb], refs[nb + 1]
        mx, my, mc = _my_place()
        for t in range(nt):
            src = refs[t].at[pl.ds(0, ncopies)]
            dst = src if same else refs[nt + t].at[pl.ds(0, ncopies)]
            cp = pltpu.make_async_remote_copy(src_ref=src, dst_ref=dst, send_sem=s_sem.at[t], recv_sem=r_sem.at[t],
                                              device_id=(mx, my, 1 - mc), device_id_type=MESH)
            cp.wait_send()
            cp.wait_recv()

    out = _pcall(body, name=name, out_shape=tuple(pltpu.HBM(b.shape, b.dtype) for b in bufs),
                 in_specs=(HBM_SPEC,) * nb + (SEM_SPEC, SEM_SPEC, ANY_SPEC), out_specs=(HBM_SPEC,) * nb,
                 input_output_aliases={i: i for i in range(nb)},
                 compiler_params=pltpu.CompilerParams(has_side_effects=DATAFLOW))(*bufs, sems[0], sems[1], after)
    out = list(out)
    return (out, out) if same else (out[:nt], out[nt:])


def _cast_blocks(weights, l, me, after, name):
    nt = len(weights)
    halves = 2
    extra = [] if after is None else [after]

    def body(me_ref, *refs):
        outs = refs[nt + len(extra):]
        for t in range(nt):
            outs[t][...] = refs[t][...].astype(bf16)

    in_specs, out_specs, out_shape = [], [], []
    for wt in weights:
        _, r, c = wt.shape
        tr = r // halves
        in_specs.append(pl.BlockSpec((None, tr, c), lambda i, me_ref: (l, i, 0)))
        out_specs.append(pl.BlockSpec((None, tr, c), lambda i, me_ref: (me_ref[0], i, 0)))
        out_shape.append(SDS((N_DEV, r, c), bf16))
    in_specs += [pl.BlockSpec(memory_space=pl.ANY)] * len(extra)
    grid_spec = pltpu.PrefetchScalarGridSpec(num_scalar_prefetch=1, grid=(halves,), in_specs=in_specs, out_specs=out_specs)
    return list(_pcall(body, name=name, out_shape=out_shape, grid_spec=grid_spec, compiler_params=_cp(("parallel",)))(me, *weights, *extra))

def _layer_fwd_mix(x, p, bias, l, after=None):
    tag = f"l{l}"
    z = _mm(x, p["w_in"], "nt", 1024, 1408, f"{tag}_mm_in", after=after, norm_a=p["norm_mix_g"])
    yb, lt = _attn_fwd(z, bias, f"{tag}_attn_fwd")
    y = _mixer_fwd(z, yb, p["conv_a_w"], p["conv_c_w"], p["conv_c_b"], p["ln_c_g"], p["ln_c_b"], p["out_norm_g"], f"{tag}_mixer_fwd")
    return y, dict(x=x, z=z, yb=yb, lt=lt, y=y)


def _layer_fwd_ffn(x, y, p, l, after=None):
    tag = f"l{l}"
    x1 = _mm(y, p["w_out"], "nn", 512, 1024, f"{tag}_mm_out", res=x, after=after)
    upre = _mm(x1, p["w_up"], "nt", 1024, 1408, f"{tag}_mm_up", norm_a=p["norm_ffn_g"])
    act = _ffn_act_fwd(upre, p["conv_f_w"], f"{tag}_ffn_act")
    x2 = _mm(act, p["w_down"], "nn", 512, 1024, f"{tag}_mm_down", res=x1)
    return x2, dict(x1=x1, upre=upre, act=act)


def _layer_bwd_ffn(dx2, p, sv, l, after=None):
    tag = f"l{l}"
    g = {}
    dact = _mm(dx2, p["w_down"], "nt", 512, 1408, f"{tag}_mm_dact", after=after, b_outer=True)
    dwd = _mm(sv["act"], dx2, "tn", 1408, 1024, f"{tag}_mm_dwdown", out_dtype=bf16)
    g["w_down"] = dwd.reshape(N_DEV, dwd.shape[0] // N_DEV, dwd.shape[1])
    dug, duv, dwg, dwv = _ffn_act_bwd(sv["upre"], dact, p["conv_f_w"], f"{tag}_ffn_act_bwd")
    g["conv_f_w"] = jnp.concatenate([dwg[:K_FFN], dwv[:K_FFN]], axis=1)
    dupre = jnp.concatenate([dug, duv], axis=1)
    dwu = _mm(dupre, sv["x1"], "tn", 1408, 1024, f"{tag}_mm_dwup", out_dtype=bf16, norm_b=p["norm_ffn_g"])
    g["w_up"] = dwu.reshape(N_DEV, dwu.shape[0] // N_DEV, dwu.shape[1])
    dx1, g["norm_ffn_g"] = _mm(dupre, p["w_up"], "nn", 512, 1024, f"{tag}_mm_dh2", rms_bwd=(sv["x1"], p["norm_ffn_g"], dx2))
    return dx1, g


def _layer_bwd_mix(dx1, p, sv, bias, l, after=None):
    tag = f"l{l}"
    g = {}
    dy = _mm(dx1, p["w_out"], "nt", 512, 1024, f"{tag}_mm_dy", after=after)
    dwo = _mm(sv["y"], dx1, "tn", 512, 1024, f"{tag}_mm_dwout", out_dtype=bf16)
    g["w_out"] = dwo.reshape(N_DEV, dwo.shape[0] // N_DEV, dwo.shape[1])
    dza, dzc, dyb, pg, dgb = _mixer_bwd(sv["z"], sv["yb"], dy, p["conv_a_w"], p["conv_c_w"], p["conv_c_b"], p["ln_c_g"],
                                        p["ln_c_b"], p["out_norm_g"], f"{tag}_mixer_bwd")
    g["conv_a_w"] = pg[PG_CONV_A:PG_CONV_A + K_SHORT]
    g["conv_c_w"] = pg[PG_CONV_C:PG_CONV_C + K_CONF]
    g["conv_c_b"], g["ln_c_g"], g["ln_c_b"] = pg[PG_CB:PG_CB + 1], pg[PG_LNG:PG_LNG + 1], pg[PG_LNB:PG_LNB + 1]
    g["out_norm_g"] = jnp.concatenate([pg[PG_OG_A:PG_OG_A + 1], dgb, pg[PG_OG_C:PG_OG_C + 1]], axis=1)
    dq, dk, dv, tabs = _attn_bwd(sv["z"], dyb, sv["yb"], sv["lt"], bias, f"{tag}_attn_bwd")
    dz = jnp.concatenate([dza, dq, dk, dv, dzc], axis=1)
    dwi = _mm(dz, sv["x"], "tn", 1408, 1024, f"{tag}_mm_dwin", out_dtype=bf16, norm_b=p["norm_mix_g"])
    g["w_in"] = dwi.reshape(N_DEV, dwi.shape[0] // N_DEV, dwi.shape[1])
    dx, g["norm_mix_g"] = _mm(dz, p["w_in"], "nn", 512, 1024, f"{tag}_mm_dh", rms_bwd=(sv["x"], p["norm_mix_g"], dx1))
    return dx, g, tabs


SMALL_LAYER_PARAMS = ("norm_mix_g", "conv_a_w", "conv_c_w", "conv_c_b", "ln_c_g", "ln_c_b", "out_norm_g", "norm_ffn_g", "conv_f_w")


BIG = ("w_in", "w_out", "w_up", "w_down")
TRANSPOSED = ("w_in", "w_up")
ADAM_ROWS = {"w_in": 176, "w_out": 128, "w_up": 176, "w_down": 176}
SMALL_SHARDED = ("conv_a_w", "conv_c_w", "conv_f_w")
SMALL_REPLICATED = ("norm_mix_g", "conv_c_b", "ln_c_g", "ln_c_b", "out_norm_g", "norm_ffn_g", "rel_bias", "final_g")
WEIGHTS = ("norm_mix_g", "w_in", "conv_a_w", "conv_c_w", "conv_c_b", "ln_c_g", "ln_c_b", "out_norm_g", "w_out",
           "norm_ffn_g", "w_up", "conv_f_w", "w_down", "rel_bias", "final_g")


def _pack(arrays):
    flat = jnp.concatenate([a.reshape(-1) for a in arrays])
    rows = -(-flat.shape[0] // 1024) * 8
    return jnp.pad(flat, (0, rows * 128 - flat.shape[0])).reshape(rows, 128)


def _unpack(packed, shapes):
    flat, out, at = packed.reshape(-1), [], 0
    for shp in shapes:
        n = int(np.prod(shp))
        out.append(flat[at:at + n].reshape(shp))
        at += n
    return out


def _gather_shards(g, local_shape):
    nd = len(local_shape)
    return jnp.moveaxis(g, 0, nd - 1).reshape(local_shape[:-1] + (N_DEV * local_shape[-1],))


def kernel(x, norm_mix_g, w_in, conv_a_w, conv_c_w, conv_c_b, ln_c_g, ln_c_b, out_norm_g, w_out, norm_ffn_g, w_up, conv_f_w, w_down, rel_bias, final_g, loss_target, m_norm_mix_g, m_w_in, m_conv_a_w, m_conv_c_w, m_conv_c_b, m_ln_c_g, m_ln_c_b, m_out_norm_g, m_w_out, m_norm_ffn_g, m_w_up, m_conv_f_w, m_w_down, m_rel_bias, m_final_g, v_norm_mix_g, v_w_in, v_conv_a_w, v_conv_c_w, v_conv_c_b, v_ln_c_g, v_ln_c_b, v_out_norm_g, v_w_out, v_norm_ffn_g, v_w_up, v_conv_f_w, v_w_down, v_rel_bias, v_final_g):
    w = dict(norm_mix_g=norm_mix_g, w_in=w_in, conv_a_w=conv_a_w, conv_c_w=conv_c_w, conv_c_b=conv_c_b, ln_c_g=ln_c_g,
             ln_c_b=ln_c_b, out_norm_g=out_norm_g, w_out=w_out, norm_ffn_g=norm_ffn_g, w_up=w_up, conv_f_w=conv_f_w,
             w_down=w_down, rel_bias=rel_bias, final_g=final_g)
    mom = dict(norm_mix_g=m_norm_mix_g, w_in=m_w_in, conv_a_w=m_conv_a_w, conv_c_w=m_conv_c_w, conv_c_b=m_conv_c_b,
               ln_c_g=m_ln_c_g, ln_c_b=m_ln_c_b, out_norm_g=m_out_norm_g, w_out=m_w_out, norm_ffn_g=m_norm_ffn_g,
               w_up=m_w_up, conv_f_w=m_conv_f_w, w_down=m_w_down, rel_bias=m_rel_bias, final_g=m_final_g)
    var = dict(norm_mix_g=v_norm_mix_g, w_in=v_w_in, conv_a_w=v_conv_a_w, conv_c_w=v_conv_c_w, conv_c_b=v_conv_c_b,
               ln_c_g=v_ln_c_g, ln_c_b=v_ln_c_b, out_norm_g=v_out_norm_g, w_out=v_w_out, norm_ffn_g=v_norm_ffn_g,
               w_up=v_w_up, conv_f_w=v_conv_f_w, w_down=v_w_down, rel_bias=v_rel_bias, final_g=v_final_g)
    depth = w_in.shape[0]
    me = 4 * lax.axis_index("x") + 2 * lax.axis_index("y") + lax.axis_index("c")
    for k in TRANSPOSED:
        w[k], mom[k], var[k] = (jnp.transpose(a, (0, 2, 1)) for a in (w[k], mom[k], var[k]))

    me_arr = me.astype(jnp.int32).reshape(1)
    bufs = [_cast_blocks([w[k] for k in BIG], 0, me_arr, None, "l0_cast_blocks")]
    in_sems, in_bufs, token = _gather_start(bufs[0][:1], me_arr, "l0_gather_start_in")
    bufs += [_cast_blocks([w[k] for k in BIG], l, me_arr, token, f"l{l}_cast_blocks") for l in range(1, depth)]
    params = {}
    gathered = _all_gather([w[k] for k in SMALL_SHARDED], "gather_small", after=token)
    gs = gathered[0]
    for k, g8 in zip(SMALL_SHARDED, gathered):
        params[k] = _gather_shards(g8, w[k].shape)
    for k in ("norm_mix_g", "conv_c_b", "ln_c_g", "ln_c_b", "out_norm_g", "norm_ffn_g"):
        params[k] = w[k][:, None, :]
    params["rel_bias"] = rel_bias
    params["final_g"] = final_g[None, :]

    def full(k, buf, l):
        return buf.reshape(N_DEV * buf.shape[1], buf.shape[2])

    bias = _bias_tables(rel_bias, "bias_tables").reshape(len(DILATIONS), 2, 4, 2 * BLK, 2 * BLK)
    in_sems, in_bufs, token = _gather_forward(in_bufs, in_sems, gs, "l0_gather_forward_in")
    rest_sems, rest_bufs, token = _gather_start(bufs[0][1:], token, "l0_gather_start_rest")
    xl, after = x[0], token
    layer_params, saved = [], []
    for l in range(depth):
        _, in_bufs = _copies_wait(in_bufs, in_bufs, in_sems, after, f"l{l}_gather_wait_in", N_SECOND_LEG)
        p = {k: params[k][l] for k in SMALL_LAYER_PARAMS}
        p["w_in"] = full("w_in", in_bufs[0], l)
        token = in_bufs[0]
        if l > 0:
            rest_sems, rest_bufs, token = _gather_forward(rest_bufs, rest_sems, token, f"l{l}_gather_forward_rest")
        if l + 1 < depth:
            in_sems, in_bufs, token = _gather_start(bufs[l + 1][:1], token, f"l{l + 1}_gather_start_in")
        y, sv = _layer_fwd_mix(xl, p, bias, l, after=token)
        if l == 0:
            rest_sems, rest_bufs, _ = _gather_forward(rest_bufs, rest_sems, y, "l0_gather_forward_rest")
        _, rest_bufs = _copies_wait(rest_bufs, rest_bufs, rest_sems, y, f"l{l}_gather_wait_rest", N_SECOND_LEG)
        for k, buf in zip(BIG[1:], rest_bufs):
            p[k] = full(k, buf, l)
        token = rest_bufs[0]
        if l + 1 < depth:
            in_sems, in_bufs, token = _gather_forward(in_bufs, in_sems, token, f"l{l + 1}_gather_forward_in")
            rest_sems, rest_bufs, token = _gather_start(bufs[l + 1][1:], token, f"l{l + 1}_gather_start_rest")
        xl, sv2 = _layer_fwd_ffn(xl, y, p, l, after=token)
        layer_params.append(p)
        saved.append({**sv, **sv2})
        after = xl
    loss, dx, d_final = _final_loss(xl, params["final_g"], loss_target[0], "final_loss")
    loss = lax.psum(loss[0, 0], ("x", "y", "c"))

    grads, tabs, pending = [None] * depth, [None] * depth, []
    after = None
    for l in reversed(range(depth)):
        dx1, g_ffn = _layer_bwd_ffn(dx, layer_params[l], saved[l], l, after=after)
        names = ("w_up", "w_down")
        sends = [g_ffn[k] for k in names]
        sems, sends, lands, after = _exchange_start(sends, [lax.empty(s.shape, s.dtype) for s in sends], f"l{l}_exchange_start_ffn")
        pending.append((l, "ffn", names, sems, sends, lands))
        dx, g_mix, tabs[l] = _layer_bwd_mix(dx1, layer_params[l], saved[l], bias, l, after=after)
        names = ("w_out", "w_in")
        sends = [g_mix[k] for k in names]
        sems, sends, lands, after = _exchange_start(sends, [lax.empty(s.shape, s.dtype) for s in sends], f"l{l}_exchange_start_mix")
        pending.append((l, "mix", names, sems, sends, lands))
        grads[l] = {**g_ffn, **g_mix}
    grads_small = {k: jnp.stack([grads[l][k] for l in range(depth)]) for k in SMALL_LAYER_PARAMS}
    grads_small["rel_bias"] = _bias_grad(jnp.stack(tabs).reshape(depth, len(DILATIONS), 8, BLK, 2 * BLK), "bias_grad")[:, :8]
    grads_small["final_g"] = d_final

    out_g, out_d, out_m, out_v = {}, {}, {}, {}
    results = {k: [lax.empty(w[k].shape, f32) for _ in range(4)] for k in BIG}
    after = dx
    for l, part, names, sems, sends, lands in pending:
        sends, lands = _copies_wait(sends, lands, sems, after, f"l{l}_exchange_wait_{part}")
        for k, send, land in zip(names, sends, lands):
            results[k] = _adamw_layer(land, send, w[k], mom[k], var[k], results[k], l, ADAM_ROWS[k], me_arr, f"l{l}_adamw_{k}")
            after = results[k][0]
    for k in BIG:
        res = [jnp.transpose(a, (0, 2, 1)) for a in results[k]] if k in TRANSPOSED else results[k]
        out_g[k], out_d[k], out_m[k], out_v[k] = res
    grads = grads_small

    small = SMALL_SHARDED + SMALL_REPLICATED
    full_shapes = [params[k].shape if k in SMALL_SHARDED else w[k].shape for k in small]
    packed = _pack([grads[k].reshape(shp) for k, shp in zip(small, full_shapes)])
    gsum = _sum_parts(_all_gather([packed], "gather_small_grads")[0], "sum_small_grads")
    gfull = dict(zip(small, _unpack(gsum, full_shapes)))
    for k in SMALL_SHARDED:
        shp = w[k].shape
        gfull[k] = lax.dynamic_index_in_dim(gfull[k].reshape(shp[:-1] + (N_DEV, shp[-1])), me, axis=len(shp) - 1, keepdims=False)
    two_d = lambda a: a.reshape(1, -1) if a.ndim == 1 else a
    res = _adamw_small(*[[two_d(src[k]) for k in small] for src in (gfull, w, mom, var)], "adamw_small")
    for name_map, outs in zip((out_d, out_m, out_v), res):
        name_map.update({k: o.reshape(w[k].shape) for k, o in zip(small, outs)})
    out_g.update(gfull)

    return (loss, dx[None], *[out_g[k] for k in WEIGHTS], *[out_d[k] for k in WEIGHTS], *[out_m[k] for k in WEIGHTS],
            *[out_v[k] for k in WEIGHTS])
```

```python
import math

import numpy as np
import jax
import jax.numpy as jnp
from jax import lax
from jax.experimental import pallas as pl
from jax.experimental.pallas import tpu as pltpu

f32, bf16 = jnp.float32, jnp.bfloat16
SDS = jax.ShapeDtypeStruct
MESH = pl.DeviceIdType.MESH

N_DEV = 8
EPS = 1e-6
NEG = -1e30
D_HEAD = 64
BLK = 128
N_BUCKETS = 32
DILATIONS = (1, 4, 16)
N_KEYS = 128
K_SHORT, K_CONF, K_FFN = 3, 31, 3
ROW_TILE = 256
HALO = 32
HALO_FFN = 8
FFN_FWD_ROWS, FFN_BWD_ROWS = 1024, 1024
FFN_CHUNK = 64
VMEM_LIMIT = 56 << 20
ADAM_LR, ADAM_B1, ADAM_B2, ADAM_EPS, ADAM_WD, ADAM_STEP = 0.001, 0.9, 0.999, 1e-08, 0.01, 10


def _pcall(body, **kw):
    return pl.pallas_call(body, **kw)


def _cp(sem=None):
    kw = dict(vmem_limit_bytes=VMEM_LIMIT)
    if sem is not None:
        kw["dimension_semantics"] = sem
    return pltpu.CompilerParams(**kw)


def _t5_buckets():
    rel = np.arange(BLK)[:, None] - np.arange(2 * BLK)[None, :] + BLK
    out = []
    for d in DILATIONS:
        dist = np.maximum(rel, 0) * d
        max_exact = N_BUCKETS // 2
        d_f = np.maximum(dist, 1).astype(np.float32)
        large = max_exact + (np.log(d_f / np.float32(max_exact)) / np.float32(math.log(2048 / max_exact))
                             * np.float32(N_BUCKETS - max_exact)).astype(np.int32)
        large = np.minimum(large, N_BUCKETS - 1)
        out.append(np.where(dist < max_exact, dist, large).astype(np.int32))
    return np.stack(out)


_BUCKETS = _t5_buckets()


def _mm(a, b, mode, tr, tc, name, res=None, out_dtype=f32, after=None, b_outer=False, norm_a=None, norm_b=None, rms_bwd=None):
    if rms_bwd is not None:
        return _mm_rms_bwd(a, b, tr, name, *rms_bwd)
    ij = (lambda f: (lambda j, i: f(i, j))) if b_outer else (lambda f: f)
    if mode == "nn":
        (m, k), n = a.shape, b.shape[1]
        a_spec = pl.BlockSpec((tr, k), ij(lambda i, j: (i, 0)))
        b_spec = pl.BlockSpec((k, tc), ij(lambda i, j: (0, j)))
        dims = (((1,), (0,)), ((), ()))
    elif mode == "nt":
        (m, k), n = a.shape, b.shape[0]
        a_spec = pl.BlockSpec((tr, k), ij(lambda i, j: (i, 0)))
        b_spec = pl.BlockSpec((tc, k), ij(lambda i, j: (j, 0)))
        dims = (((1,), (1,)), ((), ()))
    else:
        (k, m), n = a.shape, b.shape[1]
        a_spec = pl.BlockSpec((k, tr), ij(lambda i, j: (0, i)))
        b_spec = pl.BlockSpec((k, tc), ij(lambda i, j: (0, j)))
        dims = (((0,), (0,)), ((), ()))
    assert m % tr == 0 and n % tc == 0, (name, m, n, tr, tc)
    o_spec = pl.BlockSpec((tr, tc), ij(lambda i, j: (i, j)))
    grid = (n // tc, m // tr) if b_outer else (m // tr, n // tc)

    gain = norm_a if norm_a is not None else norm_b
    assert norm_b is None or (mode == "tn" and tc == n and not b_outer)
    assert norm_a is None or (mode != "tn" and not b_outer)

    def body(a_ref, b_ref, *rest):
        o_ref = rest[-2] if gain is not None else rest[-1]
        if norm_a is not None:
            ha_sc = rest[-1]

            @pl.when(pl.program_id(1) == 0)
            def _():
                av = a_ref[...]
                ha_sc[...] = (av * _rstd(av) * rest[0][...]).astype(bf16)

            av = ha_sc[...]
        else:
            av = a_ref[...].astype(bf16)
        if norm_b is not None:
            hb_sc = rest[-1]

            @pl.when(pl.program_id(0) == 0)
            def _():
                bv = b_ref[...]
                hb_sc[...] = (bv * _rstd(bv) * rest[0][...]).astype(bf16)

            bv = hb_sc[...]
        else:
            bv = b_ref[...].astype(bf16)
        acc = lax.dot_general(av, bv, dims, preferred_element_type=f32)
        if res is not None:
            acc = acc + rest[1 if gain is not None else 0][...]
        o_ref[...] = acc.astype(out_dtype)

    vec = pl.BlockSpec((1, k if norm_a is not None else n), lambda i, j: (0, 0))
    ins = [a, b] + ([gain] if gain is not None else []) + ([res] if res is not None else []) + ([after] if after is not None else [])
    specs = ([a_spec, b_spec] + ([vec] if gain is not None else []) + ([o_spec] if res is not None else [])
             + ([pl.BlockSpec(memory_space=pl.ANY)] if after is not None else []))
    scratch = [pltpu.VMEM((k, n), bf16)] if norm_b is not None else [pltpu.VMEM((tr, k), bf16)] if norm_a is not None else []
    sem = ("arbitrary", "arbitrary") if gain is not None else ("parallel", "parallel")
    return _pcall(body, name=name, out_shape=SDS((m, n), out_dtype), grid=grid, in_specs=specs, out_specs=o_spec,
                  scratch_shapes=scratch, compiler_params=_cp(sem))(*ins)


def _mm_rms_bwd(a, b, tr, name, x, g, dres):
    (m, k), n = a.shape, b.shape[1]

    def body(a_ref, b_ref, x_ref, g_ref, dres_ref, dx_ref, dg_ref):
        dh = jnp.dot(a_ref[...].astype(bf16), b_ref[...].astype(bf16), preferred_element_type=f32)
        xv = x_ref[...]
        r = _rstd(xv)
        dx_ref[...] = dres_ref[...] + _rms_dx(xv, r, g_ref[...], dh)

        @pl.when(pl.program_id(0) == 0)
        def _():
            dg_ref[...] = jnp.zeros_like(dg_ref)

        dg_ref[...] += _colsum(dh * xv * r)

    row = pl.BlockSpec((tr, n), lambda i: (i, 0))
    vec = pl.BlockSpec((1, n), lambda i: (0, 0))
    return _pcall(body, name=name, out_shape=(SDS((m, n), f32), SDS((1, n), f32)), grid=(m // tr,),
                  in_specs=[pl.BlockSpec((tr, k), lambda i: (i, 0)), pl.BlockSpec((k, n), lambda i: (0, 0), pipeline_mode=pl.Buffered(1)),
                            row, vec, row],
                  out_specs=(row, vec), compiler_params=_cp(("arbitrary",)))(a, b, x, g, dres)


def _rstd(x):
    return lax.rsqrt(jnp.mean(x * x, axis=-1, keepdims=True) + EPS)


def _rms_dx(x, r, g, dy):
    dxh = dy * g
    return r * (dxh - x * (r * r) * jnp.mean(dxh * x, axis=-1, keepdims=True))


def _sigmoid(x):
    return 1.0 / (1.0 + jnp.exp(-x))


def _colsum(x):
    return jnp.sum(x, axis=0, keepdims=True)


def _final_loss(x, g, target, name):
    s, d = x.shape

    def body(x_ref, g_ref, t_ref, loss_ref, dx_ref, dg_ref):
        xv, gv = x_ref[...], g_ref[...]
        r = _rstd(xv)
        diff = xv * r * gv - t_ref[...]
        dy = diff * (1.0 / d)

        @pl.when(pl.program_id(0) == 0)
        def _():
            dg_ref[...] = jnp.zeros_like(dg_ref)
            loss_ref[...] = jnp.zeros_like(loss_ref)

        part = jnp.sum(jnp.sum(diff * diff, axis=1, keepdims=True), axis=0, keepdims=True) * (0.5 / d)
        loss_ref[...] += part
        dx_ref[...] = _rms_dx(xv, r, gv, dy)
        dg_ref[...] += _colsum(dy * xv * r)

    row = pl.BlockSpec((ROW_TILE, d), lambda i: (i, 0))
    vec = pl.BlockSpec((1, d), lambda i: (0, 0))
    one = pl.BlockSpec((8, 128), lambda i: (0, 0))
    return _pcall(body, name=name, out_shape=(SDS((8, 128), f32), SDS((s, d), f32), SDS((1, d), f32)),
                  grid=(s // ROW_TILE,), in_specs=[row, vec, row], out_specs=(one, row, vec),
                  compiler_params=_cp(("arbitrary",)))(x, g, target)


Z_AH, Z_AB, Z_AC, Z_CV, Z_CG = 0, 1, 2, 9, 10
Z_Q128, Z_K128, Z_V128 = 6, 10, 14


def _cur(t, w, col):
    return pl.BlockSpec((t, w), lambda i: (i, col))


def _prev(hb, t, w, col):
    return pl.BlockSpec((hb, w), lambda i: (jnp.maximum(i * (t // hb) - 1, 0), col))


def _next(hb, t, w, col, rows):
    return pl.BlockSpec((hb, w), lambda i: (jnp.minimum((i + 1) * (t // hb), rows // hb - 1), col))


def _layernorm_parts(u, g, b):
    mu = jnp.mean(u, axis=-1, keepdims=True)
    var = jnp.mean(jnp.square(u - mu), axis=-1, keepdims=True)
    rs = lax.rsqrt(var + EPS)
    xhat = (u - mu) * rs
    return xhat, rs, xhat * g + b


def _fill_shifted(rot):
    n = rot.shape[1]
    for r in range(1, 8):
        rot[r, pl.ds(0, n - 8), :] = rot[0, pl.ds(r, n - 8), :]


def _window(rot, offset, rows):
    return rot[offset % 8, pl.ds(offset - offset % 8, rows), :]


def _mixer_fwd(z, yb, conv_a, conv_c, cb, lng, lnb, og, name):
    s = z.shape[0]
    t, hb, w = ROW_TILE, HALO, 256

    def body(ah, ab, ac, cv, cg, ah_p, ac_p, cv_p, cg_p, yb_ref, wa, wc, cb_ref, lng_ref, lnb_ref, og_ref, y_ref, p_sc, u_sc):
        first = pl.program_id(0) == 0

        def prev(ref):
            return jnp.where(first, 0.0, ref[...])

        p_sc[pl.ds(0, hb), :] = prev(ah_p) * prev(ac_p)
        p_sc[pl.ds(hb, t), :] = ah[...] * ac[...]
        conv = jnp.zeros((t, w), f32)
        for k in range(K_SHORT):
            conv = conv + wa[pl.ds(k, 1), :] * p_sc[pl.ds(hb - (K_SHORT - 1) + k, t), :]
        y_a = ab[...] * conv
        u_sc[0, pl.ds(0, hb), :] = prev(cv_p) * _sigmoid(prev(cg_p))
        u_sc[0, pl.ds(hb, t), :] = cv[...] * _sigmoid(cg[...])
        _fill_shifted(u_sc)
        u = jnp.zeros((t, w), f32) + cb_ref[...]
        for k in range(K_CONF):
            u = u + wc[pl.ds(k, 1), :] * _window(u_sc, hb - (K_CONF - 1) + k, t)
        _, _, lnv = _layernorm_parts(u, lng_ref[...], lnb_ref[...])
        y_c = lnv * _sigmoid(lnv)
        ybv = yb_ref[...]
        y_ref[:, 0:256] = (y_a * _rstd(y_a) * og_ref[:, 0:256]).astype(bf16)
        y_ref[:, 256:768] = (ybv * _rstd(ybv) * og_ref[:, 256:768]).astype(bf16)
        y_ref[:, 768:1024] = (y_c * _rstd(y_c) * og_ref[:, 768:1024]).astype(bf16)

    full = lambda shape: pl.BlockSpec(shape, lambda i: tuple(0 for _ in shape))
    in_specs = [_cur(t, w, c) for c in (Z_AH, Z_AB, Z_AC, Z_CV, Z_CG)] + [_prev(hb, t, w, c) for c in (Z_AH, Z_AC, Z_CV, Z_CG)]
    in_specs += [_cur(t, 512, 0), full((K_SHORT, w)), full((K_CONF, w)), full((1, w)), full((1, w)), full((1, w)), full((1, 1024))]
    return _pcall(body, name=name, out_shape=SDS((s, 1024), bf16), grid=(s // t,), in_specs=in_specs,
                  out_specs=_cur(t, 1024, 0), scratch_shapes=[pltpu.VMEM((hb + t, w), f32), pltpu.VMEM((8, hb + t, w), f32)],
                  compiler_params=_cp(("parallel",)))(z, z, z, z, z, z, z, z, z, yb, conv_a, conv_c, cb, lng, lnb, og)


PG_CONV_A, PG_CONV_C, PG_CB, PG_LNG, PG_LNB, PG_OG_A, PG_OG_C, PG_ROWS = 0, 8, 40, 41, 42, 43, 44, 48


def _mixer_bwd(z, yb, dy, conv_a, conv_c, cb, lng, lnb, og, name):
    s = z.shape[0]
    t, hb, w = ROW_TILE, HALO, 256
    e = t + hb

    def body(ah, ab, ac, cv, cg, ah_p, ac_p, cv_p, cg_p, ah_n, ab_n, ac_n, cv_n, cg_n,
             dya, dyb1, dyb2, dyc, dya_n, dyc_n, yb_ref, wa, wc, cb_ref, lng_ref, lnb_ref, og_ref,
             dza_ref, dzc_ref, dyb_ref, pg_ref, dgb_ref, p_sc, u_sc, dc_sc, du_sc):
        i = pl.program_id(0)
        first, last = i == 0, i == pl.num_programs(0) - 1

        def prev(ref):
            return jnp.where(first, 0.0, ref[...])

        def nxt(ref):
            return jnp.where(last, 0.0, ref[...])

        def ext(cur_ref, next_ref):
            return jnp.concatenate([cur_ref[...], nxt(next_ref)], axis=0)

        @pl.when(first)
        def _():
            pg_ref[...] = jnp.zeros_like(pg_ref)
            dgb_ref[...] = jnp.zeros_like(dgb_ref)

        g_a = og_ref[:, 0:256]
        p_sc[pl.ds(0, hb), :] = prev(ah_p) * prev(ac_p)
        p_sc[pl.ds(hb, t), :] = ah[...] * ac[...]
        p_sc[pl.ds(hb + t, hb), :] = nxt(ah_n) * nxt(ac_n)
        conv = jnp.zeros((e, w), f32)
        for k in range(K_SHORT):
            conv = conv + wa[pl.ds(k, 1), :] * p_sc[pl.ds(hb - (K_SHORT - 1) + k, e), :]
        ab_e, dya_e = ext(ab, ab_n), ext(dya, dya_n)
        y_a = ab_e * conv
        r_a = _rstd(y_a)
        dy_a = _rms_dx(y_a, r_a, g_a, dya_e)
        dc_sc[...] = dy_a * ab_e
        dza_ref[:, 256:512] = (dy_a[:t] * conv[:t]).astype(bf16)
        dp = jnp.zeros((t, w), f32)
        for k in range(K_SHORT):
            dp = dp + wa[pl.ds(k, 1), :] * dc_sc[pl.ds(K_SHORT - 1 - k, t), :]
            pg_ref[pl.ds(PG_CONV_A + k, 1), :] += _colsum(dc_sc[pl.ds(0, t), :] * p_sc[pl.ds(hb - (K_SHORT - 1) + k, t), :])
        dza_ref[:, 0:256] = (dp * ac[...]).astype(bf16)
        dza_ref[:, 512:768] = (dp * ah[...]).astype(bf16)
        pg_ref[pl.ds(PG_OG_A, 1), :] += _colsum(dya[...] * y_a[:t] * r_a[:t])

        g_c = og_ref[:, 768:1024]
        u_sc[0, pl.ds(0, hb), :] = prev(cv_p) * _sigmoid(prev(cg_p))
        u_sc[0, pl.ds(hb, t), :] = cv[...] * _sigmoid(cg[...])
        u_sc[0, pl.ds(hb + t, hb), :] = nxt(cv_n) * _sigmoid(nxt(cg_n))
        _fill_shifted(u_sc)
        u = jnp.zeros((e, w), f32) + cb_ref[...]
        for k in range(K_CONF):
            u = u + wc[pl.ds(k, 1), :] * _window(u_sc, hb - (K_CONF - 1) + k, e)
        xhat, rs, lnv = _layernorm_parts(u, lng_ref[...], lnb_ref[...])
        sg = _sigmoid(lnv)
        y_c = lnv * sg
        r_c = _rstd(y_c)
        dyc_e = ext(dyc, dyc_n)
        dlnv = _rms_dx(y_c, r_c, g_c, dyc_e) * (sg * (1.0 + lnv * (1.0 - sg)))
        dxh = dlnv * lng_ref[...]
        du = rs * (dxh - jnp.mean(dxh, axis=-1, keepdims=True) - xhat * jnp.mean(dxh * xhat, axis=-1, keepdims=True))
        du_sc[0] = du
        _fill_shifted(du_sc)
        duin = jnp.zeros((t, w), f32)
        for k in range(K_CONF):
            duin = duin + wc[pl.ds(k, 1), :] * _window(du_sc, K_CONF - 1 - k, t)
            pg_ref[pl.ds(PG_CONV_C + k, 1), :] += _colsum(du[:t] * _window(u_sc, hb - (K_CONF - 1) + k, t))
        sgg = _sigmoid(cg[...])
        dzc_ref[:, 0:256] = (duin * sgg).astype(bf16)
        dzc_ref[:, 256:512] = (duin * cv[...] * sgg * (1.0 - sgg)).astype(bf16)
        pg_ref[pl.ds(PG_CB, 1), :] += _colsum(du[:t])
        pg_ref[pl.ds(PG_LNG, 1), :] += _colsum(dlnv[:t] * xhat[:t])
        pg_ref[pl.ds(PG_LNB, 1), :] += _colsum(dlnv[:t])
        pg_ref[pl.ds(PG_OG_C, 1), :] += _colsum(dyc[...] * y_c[:t] * r_c[:t])

        g_b = og_ref[:, 256:768]
        ybv = yb_ref[...]
        r_b = _rstd(ybv)
        dyb = jnp.concatenate([dyb1[...], dyb2[...]], axis=1)
        dyb_ref[...] = _rms_dx(ybv, r_b, g_b, dyb)
        dgb_ref[...] += _colsum(dyb * ybv * r_b)

    full = lambda shape: pl.BlockSpec(shape, lambda i: tuple(0 for _ in shape))
    zc = (Z_AH, Z_AB, Z_AC, Z_CV, Z_CG)
    in_specs = [_cur(t, w, c) for c in zc] + [_prev(hb, t, w, c) for c in (Z_AH, Z_AC, Z_CV, Z_CG)]
    in_specs += [_next(hb, t, w, c, s) for c in zc]
    in_specs += [_cur(t, w, c) for c in (0, 1, 2, 3)] + [_next(hb, t, w, 0, s), _next(hb, t, w, 3, s)]
    in_specs += [_cur(t, 512, 0), full((K_SHORT, w)), full((K_CONF, w)), full((1, w)), full((1, w)), full((1, w)), full((1, 1024))]
    out_shape = (SDS((s, 768), bf16), SDS((s, 512), bf16), SDS((s, 512), f32), SDS((PG_ROWS, w), f32), SDS((1, 512), f32))
    out_specs = (_cur(t, 768, 0), _cur(t, 512, 0), _cur(t, 512, 0), full((PG_ROWS, w)), full((1, 512)))
    scratch = [pltpu.VMEM((hb + t + hb, w), f32), pltpu.VMEM((8, hb + t + hb, w), f32), pltpu.VMEM((e, w), f32), pltpu.VMEM((8, e, w), f32)]
    args = [z] * 14 + [dy] * 6 + [yb, conv_a, conv_c, cb, lng, lnb, og]
    return _pcall(body, name=name, out_shape=out_shape, grid=(s // t,), in_specs=in_specs, out_specs=out_specs,
                  scratch_shapes=scratch, compiler_params=_cp(("arbitrary",)))(*args)


def _ffn_act_fwd(upre, conv_f, name):
    s, f2 = upre.shape
    t, hb, w = FFN_FWD_ROWS, HALO_FFN, 256
    nj = f2 // 2 // w

    def body(g_c, v_c, g_p, v_p, wg, wv, o_ref):
        first = pl.program_id(1) == 0
        wgs = [wg[pl.ds(k, 1), :] for k in range(K_FFN)]
        wvs = [wv[pl.ds(k, 1), :] for k in range(K_FFN)]
        rows = FFN_CHUNK
        first_tap = hb - (K_FFN - 1)

        def emit(gb, vb, r0):
            ug = sum(wgs[k] * gb[first_tap + k:first_tap + k + rows] for k in range(K_FFN))
            uv = sum(wvs[k] * vb[first_tap + k:first_tap + k + rows] for k in range(K_FFN))
            o_ref[pl.ds(r0, rows), :] = (ug * _sigmoid(ug) * uv).astype(bf16)

        emit(jnp.concatenate([jnp.where(first, 0.0, g_p[...]), g_c[pl.ds(0, rows), :]], axis=0),
             jnp.concatenate([jnp.where(first, 0.0, v_p[...]), v_c[pl.ds(0, rows), :]], axis=0), 0)

        def chunk(c, carry):
            r0 = pl.multiple_of(c * rows, rows)
            back = pl.ds(pl.multiple_of(r0 - hb, hb), rows + hb)
            emit(g_c[back, :], v_c[back, :], r0)
            return carry

        lax.fori_loop(1, t // rows, chunk, 0, unroll=3)

    cur = lambda off: pl.BlockSpec((t, w), lambda j, i: (i, off + j))
    prv = lambda off: pl.BlockSpec((hb, w), lambda j, i: (jnp.maximum(i * (t // hb) - 1, 0), off + j))
    wsp = lambda off: pl.BlockSpec((K_FFN, w), lambda j, i: (0, off + j))
    return _pcall(body, name=name, out_shape=SDS((s, f2 // 2), bf16), grid=(nj, s // t),
                  in_specs=[cur(0), cur(nj), prv(0), prv(nj), wsp(0), wsp(nj)],
                  out_specs=pl.BlockSpec((t, w), lambda j, i: (i, j)),
                  compiler_params=_cp(("parallel", "parallel")))(upre, upre, upre, upre, conv_f, conv_f)


def _ffn_act_bwd(upre, dact, conv_f, name):
    s, f2 = upre.shape
    t, hb, w = FFN_BWD_ROWS, HALO_FFN, 256
    nj = f2 // 2 // w

    def body(g_c, v_c, g_p, v_p, g_n, v_n, da_c, da_n, wg, wv, dg_ref, dv_ref, dwg_ref, dwv_ref):
        i = pl.program_id(1)
        first, last = i == 0, i == pl.num_programs(1) - 1

        @pl.when(first)
        def _():
            dwg_ref[...] = jnp.zeros_like(dwg_ref)
            dwv_ref[...] = jnp.zeros_like(dwv_ref)

        wgs = [wg[pl.ds(k, 1), :] for k in range(K_FFN)]
        wvs = [wv[pl.ds(k, 1), :] for k in range(K_FFN)]
        rows = FFN_CHUNK
        n = rows + hb
        first_tap = hb - (K_FFN - 1)
        nchunks = t // rows

        def groups_of_8(x):
            return sum(x[8 * q:8 * q + 8] for q in range(rows // 8))

        def emit(gb, vb, da, r0, carry):
            g_taps = [gb[first_tap + k:first_tap + k + n] for k in range(K_FFN)]
            v_taps = [vb[first_tap + k:first_tap + k + n] for k in range(K_FFN)]
            ug = sum(wgs[k] * g_taps[k] for k in range(K_FFN))
            uv = sum(wvs[k] * v_taps[k] for k in range(K_FFN))
            sg = _sigmoid(ug)
            dv = da * (ug * sg)
            dg = da * uv * (sg * (1.0 + ug * (1.0 - sg)))
            new = []
            for taps, dval, ws, dref in ((g_taps, dg, wgs, dg_ref), (v_taps, dv, wvs, dv_ref)):
                acc = sum(ws[k] * dval[K_FFN - 1 - k:K_FFN - 1 - k + rows] for k in range(K_FFN))
                dref[pl.ds(r0, rows), :] = acc.astype(bf16)
                new += [groups_of_8(dval[:rows] * taps[k][:rows]) for k in range(K_FFN)]
            return tuple(a + b for a, b in zip(carry, new))

        def chunk(c, carry):
            r0 = pl.multiple_of(c * rows, rows)
            around = pl.ds(pl.multiple_of(r0 - hb, hb), n + hb)
            return emit(g_c[around, :], v_c[around, :], da_c[pl.ds(r0, n), :], r0, carry)

        def edge(prev_ref, cur_ref, lo):
            return jnp.concatenate([jnp.where(first, 0.0, prev_ref[...]), cur_ref[pl.ds(0, lo), :]], axis=0)

        def tail(cur_ref, next_ref, lo):
            return jnp.concatenate([cur_ref[pl.ds(t - lo, lo), :], jnp.where(last, 0.0, next_ref[...])], axis=0)

        zero = jnp.zeros((8, w), f32)
        sums = emit(edge(g_p, g_c, n), edge(v_p, v_c, n), da_c[pl.ds(0, n), :], 0, (zero,) * (2 * K_FFN))
        sums = lax.fori_loop(1, nchunks - 1, chunk, sums, unroll=2)
        sums = emit(tail(g_c, g_n, rows + hb), tail(v_c, v_n, rows + hb), tail(da_c, da_n, rows), t - rows, sums)
        for k in range(K_FFN):
            dwg_ref[pl.ds(k, 1), :] += _colsum(sums[k])
            dwv_ref[pl.ds(k, 1), :] += _colsum(sums[K_FFN + k])

    cur = lambda off: pl.BlockSpec((t, w), lambda j, i: (i, off + j))
    prv = lambda off: pl.BlockSpec((hb, w), lambda j, i: (jnp.maximum(i * (t // hb) - 1, 0), off + j))
    nxt = lambda off: pl.BlockSpec((hb, w), lambda j, i: (jnp.minimum((i + 1) * (t // hb), s // hb - 1), off + j))
    wsp = lambda off: pl.BlockSpec((K_FFN, w), lambda j, i: (0, off + j))
    half = SDS((s, f2 // 2), bf16)
    dws = SDS((8, f2 // 2), f32)
    o_cur = pl.BlockSpec((t, w), lambda j, i: (i, j))
    o_dw = pl.BlockSpec((8, w), lambda j, i: (0, j))
    return _pcall(body, name=name, out_shape=(half, half, dws, dws), grid=(nj, s // t),
                  in_specs=[cur(0), cur(nj), prv(0), prv(nj), nxt(0), nxt(nj), cur(0), nxt(0), wsp(0), wsp(nj)],
                  out_specs=(o_cur, o_cur, o_dw, o_dw),
                  compiler_params=_cp(("parallel", "arbitrary")))(upre, upre, upre, upre, upre, upre, dact, dact, conv_f, conv_f)


def _bias_tables(rel_bias, name):
    nb = len(DILATIONS)

    def body(rb_ref, idx_ref, o_ref):
        h = pl.program_id(1)
        iv = idx_ref[0]
        qi = lax.broadcasted_iota(jnp.int32, (BLK, 2 * BLK), 0)
        kj = lax.broadcasted_iota(jnp.int32, (BLK, 2 * BLK), 1)
        band = jnp.logical_and(kj >= qi, kj <= qi + N_KEYS)
        acc = jnp.zeros((BLK, 2 * BLK), f32)
        for b in range(N_BUCKETS):
            acc = jnp.where(iv == b, rb_ref[b, h], acc)
        o_ref[0, 0, 0] = jnp.where(band, acc, NEG)
        o_ref[0, 1, 0] = jnp.where(jnp.logical_and(band, kj >= BLK), acc, NEG)

    return _pcall(body, name=name, out_shape=SDS((nb, 2, 8, BLK, 2 * BLK), f32), grid=(nb, 8),
                  in_specs=[pl.BlockSpec(memory_space=pltpu.SMEM), pl.BlockSpec((1, BLK, 2 * BLK), lambda br, h: (br, 0, 0))],
                  out_specs=pl.BlockSpec((1, 2, 1, BLK, 2 * BLK), lambda br, h: (br, 0, h, 0, 0)),
                  compiler_params=_cp(("parallel", "parallel")))(rel_bias, jnp.asarray(_BUCKETS))


def _bias_grad(ds_tabs, name):
    nl, nb = ds_tabs.shape[0], ds_tabs.shape[1]

    def body(t_ref, idx_ref, o_ref):
        br, h = pl.program_id(0), pl.program_id(1)

        @pl.when(jnp.logical_and(br == 0, h == 0))
        def _():
            o_ref[...] = jnp.zeros_like(o_ref)

        tab = t_ref[0, 0, 0]
        for l in range(1, nl):
            tab = tab + t_ref[l, 0, 0]
        iv = idx_ref[0]
        row = lax.broadcasted_iota(jnp.int32, (N_BUCKETS, 128), 0)
        col = lax.broadcasted_iota(jnp.int32, (N_BUCKETS, 128), 1)
        acc = jnp.zeros((N_BUCKETS, 128), f32)
        for b in range(N_BUCKETS):
            sel = jnp.where(iv == b, tab, 0.0)
            tot = jnp.sum(jnp.sum(sel, axis=0, keepdims=True), axis=1, keepdims=True)
            acc = acc + jnp.where(jnp.logical_and(row == b, col == h), tot, 0.0)
        o_ref[...] += acc

    return _pcall(body, name=name, out_shape=SDS((N_BUCKETS, 128), f32), grid=(nb, 8),
                  in_specs=[pl.BlockSpec((nl, 1, 1, BLK, 2 * BLK), lambda br, h: (0, br, h, 0, 0)),
                            pl.BlockSpec((1, BLK, 2 * BLK), lambda br, h: (br, 0, 0))],
                  out_specs=pl.BlockSpec((N_BUCKETS, 128), lambda br, h: (0, 0)),
                  compiler_params=_cp(("arbitrary", "arbitrary")))(ds_tabs, jnp.asarray(_BUCKETS))


def _stack_heads(x, head0):
    zero = jnp.zeros_like(x)
    return jnp.concatenate([jnp.where(head0, x, zero), jnp.where(head0, zero, x)], axis=0)


def _block_rows(blk, d, nblk):
    per_res = nblk // d
    res, n = blk // per_res, blk % per_res
    cur = pl.ds(res + d * BLK * n, BLK, stride=d) if d > 1 else pl.ds(pl.multiple_of(BLK * n, BLK), BLK)
    before = jnp.maximum(n - 1, 0)
    prev = pl.ds(res + d * BLK * before, BLK, stride=d) if d > 1 else pl.ds(pl.multiple_of(BLK * before, BLK), BLK)
    return cur, prev, jnp.where(n == 0, 1, 0)


def _attn_fwd(z, bias, name):
    s = z.shape[0]
    nblk = s // BLK

    def body(q_ref, k_ref, v_ref, b_ref, y_ref, l_ref):
        lane = lax.broadcasted_iota(jnp.int32, (BLK, BLK), 1)
        head0 = lane < D_HEAD

        for branch, d in enumerate(DILATIONS):
            def step(blk, carry, branch=branch, d=d):
                cur, prev, variant = _block_rows(blk, d, nblk)
                q = (q_ref[cur, :] * (D_HEAD ** -0.5)).astype(bf16)
                kk = jnp.concatenate([k_ref[prev, :], k_ref[cur, :]], axis=0).astype(bf16)
                vv = jnp.concatenate([v_ref[prev, :], v_ref[cur, :]], axis=0).astype(bf16)
                sc = lax.dot_general(_stack_heads(q, head0), kk, (((1,), (1,)), ((), ())), preferred_element_type=f32)
                sc = sc + b_ref[branch, variant, 0]
                m = jnp.max(sc, axis=-1, keepdims=True)
                p = jnp.exp(sc - m)
                den = jnp.sum(p, axis=-1, keepdims=True)
                out = jnp.dot(p.astype(bf16), vv, preferred_element_type=f32) / den
                lse = m + jnp.log(den)
                out = jnp.where(head0, out[:BLK], out[BLK:])
                lse = jnp.where(head0, lse[:BLK], lse[BLK:]) + jnp.zeros((BLK, BLK), f32)
                if branch == 0:
                    y_ref[cur, :] = out
                    l_ref[cur, :] = lse
                else:
                    y_old, l_old = y_ref[cur, :], l_ref[cur, :]
                    top = jnp.maximum(l_old, lse)
                    e_old, e_new = jnp.exp(l_old - top), jnp.exp(lse - top)
                    tot = e_old + e_new
                    y_ref[cur, :] = (e_old * y_old + e_new * out) / tot
                    l_ref[cur, :] = top + jnp.log(tot)
                return carry

            lax.fori_loop(0, nblk, step, 0, unroll=4)

    col = lambda off: pl.BlockSpec((s, BLK), lambda hp: (0, off + hp))
    out = SDS((s, 4 * BLK), f32)
    nbr = len(DILATIONS)
    return _pcall(body, name=name, out_shape=(out, out), grid=(4,),
                  in_specs=[col(Z_Q128), col(Z_K128), col(Z_V128), pl.BlockSpec((nbr, 2, 1, 2 * BLK, 2 * BLK), lambda hp: (0, 0, hp, 0, 0))],
                  out_specs=(col(0), col(0)), compiler_params=_cp(("parallel",)))(z, z, z, bias)


def _attn_bwd(z, dyb, yb, lt, bias, name):
    s = z.shape[0]
    nblk = s // BLK
    scale = D_HEAD ** -0.5
    chunk = 256

    def body(q_ref, k_ref, v_ref, dy_ref, y_ref, lt_ref, b_ref, dq_ref, dk_ref, dv_ref, ds_ref, aq, ak, av):
        lane = lax.broadcasted_iota(jnp.int32, (BLK, BLK), 1)
        head0 = lane < D_HEAD
        ak[...] = jnp.zeros_like(ak)
        av[...] = jnp.zeros_like(av)
        ds_ref[...] = jnp.zeros_like(ds_ref)

        for branch, d in enumerate(DILATIONS):
            def step(blk, carry, branch=branch, d=d):
                cur, prev, variant = _block_rows(blk, d, nblk)
                q = (q_ref[cur, :] * scale).astype(bf16)
                kk = jnp.concatenate([k_ref[prev, :], k_ref[cur, :]], axis=0).astype(bf16)
                vv = jnp.concatenate([v_ref[prev, :], v_ref[cur, :]], axis=0).astype(bf16)
                dy = dy_ref[cur, :]
                lt_b = lt_ref[cur, :]
                qs = _stack_heads(q, head0)
                dys = _stack_heads(dy.astype(bf16), head0)
                dd = jnp.sum(_stack_heads(dy * y_ref[cur, :], head0), axis=-1, keepdims=True)
                lts = jnp.concatenate([lt_b[:, 0:1], lt_b[:, D_HEAD:D_HEAD + 1]], axis=0)
                sc = lax.dot_general(qs, kk, (((1,), (1,)), ((), ())), preferred_element_type=f32)
                p = jnp.exp(sc + b_ref[branch, variant, 0] - lts)
                dp = lax.dot_general(dys, vv, (((1,), (1,)), ((), ())), preferred_element_type=f32)
                ds = p * (dp - dd)
                ds_ref[branch, 0] += ds
                dsb = ds.astype(bf16)
                dq = jnp.dot(dsb, kk, preferred_element_type=f32) * scale
                dkk = lax.dot_general(dsb, qs, (((0,), (0,)), ((), ())), preferred_element_type=f32)
                dvv = lax.dot_general(p.astype(bf16), dys, (((0,), (0,)), ((), ())), preferred_element_type=f32)
                dq = jnp.where(head0, dq[:BLK], dq[BLK:])
                if branch == 0:
                    aq[cur, :] = dq
                else:
                    aq[cur, :] += dq
                ak[prev, :] += dkk[:BLK]
                ak[cur, :] += dkk[BLK:]
                av[prev, :] += dvv[:BLK]
                av[cur, :] += dvv[BLK:]
                return carry

            lax.fori_loop(0, nblk, step, 0, unroll=4)

        def store(c, carry):
            rows = pl.ds(pl.multiple_of(c * chunk, chunk), chunk)
            dq_ref[rows, :] = aq[rows, :].astype(bf16)
            dk_ref[rows, :] = ak[rows, :].astype(bf16)
            dv_ref[rows, :] = av[rows, :].astype(bf16)
            return carry

        lax.fori_loop(0, s // chunk, store, 0)

    col = lambda off: pl.BlockSpec((s, BLK), lambda hp: (0, off + hp))
    nbr = len(DILATIONS)
    tabs = pl.BlockSpec((nbr, 1, 2 * BLK, 2 * BLK), lambda hp: (0, hp, 0, 0))
    bias_spec = pl.BlockSpec((nbr, 2, 1, 2 * BLK, 2 * BLK), lambda hp: (0, 0, hp, 0, 0))
    half = SDS((s, 4 * BLK), bf16)
    return _pcall(body, name=name, out_shape=(half, half, half, SDS((nbr, 4, 2 * BLK, 2 * BLK), f32)), grid=(4,),
                  in_specs=[col(Z_Q128), col(Z_K128), col(Z_V128), col(0), col(0), col(0), bias_spec],
                  out_specs=(col(0), col(0), col(0), tabs), scratch_shapes=[pltpu.VMEM((s, BLK), f32)] * 3,
                  compiler_params=_cp(("parallel",)))(z, z, z, dyb, yb, lt, bias)


def _adamw_math(w, g, m, v):
    m = ADAM_B1 * m + (1.0 - ADAM_B1) * g
    v = ADAM_B2 * v + (1.0 - ADAM_B2) * (g * g)
    m_hat = m / (1.0 - ADAM_B1 ** ADAM_STEP)
    v_hat = v / (1.0 - ADAM_B2 ** ADAM_STEP)
    delta = -ADAM_LR * (m_hat / (jnp.sqrt(v_hat) + ADAM_EPS) + ADAM_WD * w)
    return delta, m, v


def _adamw_layer(land, send, w, m, v, prev, l, tr, me, name):
    _, r, c = w.shape
    assert r % tr == 0

    def body(me_ref, land_ref, own_ref, w_ref, m_ref, v_ref, p0, p1, p2, p3, g_ref, d_ref, nm_ref, nv_ref):
        mine = jnp.zeros((tr, c), jnp.int32) + me_ref[0]
        own = own_ref[...]
        g = jnp.zeros((tr, c), f32)
        for j in range(N_DEV):
            g = g + jnp.where(mine == j, own, land_ref[j]).astype(f32)
        delta, nm, nv = _adamw_math(w_ref[...], g, m_ref[...], v_ref[...])
        g_ref[...] = g
        d_ref[...] = delta
        nm_ref[...] = nm
        nv_ref[...] = nv

    parts = pl.BlockSpec((N_DEV, tr, c), lambda i, me_ref: (0, i, 0))
    own = pl.BlockSpec((None, tr, c), lambda i, me_ref: (me_ref[0], i, 0))
    row = pl.BlockSpec((None, tr, c), lambda i, me_ref: (l, i, 0))
    anywhere = pl.BlockSpec(memory_space=pl.ANY)
    out = SDS(w.shape, f32)
    grid_spec = pltpu.PrefetchScalarGridSpec(num_scalar_prefetch=1, grid=(r // tr,), in_specs=[parts, own, row, row, row] + [anywhere] * 4,
                                             out_specs=(row, row, row, row))
    return _pcall(body, name=name, out_shape=(out, out, out, out), grid_spec=grid_spec,
                  input_output_aliases={6: 0, 7: 1, 8: 2, 9: 3}, compiler_params=_cp(("parallel",)))(me, land, send, w, m, v, *prev)


def _sum_parts(parts, name):
    _, r, c = parts.shape

    def body(p_ref, o_ref):
        g = p_ref[0]
        for j in range(1, N_DEV):
            g = g + p_ref[j]
        o_ref[...] = g

    return _pcall(body, name=name, out_shape=SDS((r, c), f32), compiler_params=_cp())(parts)


def _adamw_small(gs, ws, ms, vs, name):
    n = len(ws)

    def body(*refs):
        for t in range(n):
            delta, nm, nv = _adamw_math(refs[n + t][...], refs[t][...], refs[2 * n + t][...], refs[3 * n + t][...])
            refs[4 * n + t][...] = delta
            refs[5 * n + t][...] = nm
            refs[6 * n + t][...] = nv

    out = [SDS(wt.shape, f32) for wt in ws] * 3
    res = _pcall(body, name=name, out_shape=out, compiler_params=_cp())(*gs, *ws, *ms, *vs)
    return list(res[:n]), list(res[n:2 * n]), list(res[2 * n:])


def _my_place():
    return lax.axis_index("x"), lax.axis_index("y"), lax.axis_index("c")


def _all_gather(xs, name, after=None):
    n = len(xs)

    def body(*refs):
        x_refs, out_refs = refs[:n], refs[-3 - n:-3]
        send_sems, recv_sems, local_sems = refs[-3:]
        mx, my, mc = _my_place()
        me, sibling = (mx, my, mc), (mx, my, 1 - mc)
        chips = [(1 - mx, my), (mx, 1 - my), (1 - mx, 1 - my)]

        def rows(t, px, py, pc):
            return out_refs[t].at[4 * px + 2 * py + pc]

        def copy(t, k, block, to, src=None):
            return pltpu.make_async_remote_copy(src_ref=rows(t, *block) if src is None else src, dst_ref=rows(t, *block),
                                                send_sem=send_sems.at[7 * t + k], recv_sem=recv_sems.at[7 * t + k],
                                                device_id=to, device_id_type=MESH)

        mine = [pltpu.make_async_copy(x_refs[t], rows(t, *me), local_sems.at[t]) for t in range(n)]
        first, passed = [], []
        for t in range(n):
            mine[t].start()
            first += [copy(t, 0, me, sibling, src=x_refs[t])] + [copy(t, 1 + j, me, (*chip, mc), src=x_refs[t]) for j, chip in enumerate(chips)]
        for cp in first:
            cp.start()
        for t in range(n):
            for j, chip in enumerate(chips):
                copy(t, 1 + j, (*chip, mc), me).wait_recv()
                passed.append(copy(t, 4 + j, (*chip, mc), sibling))
                passed[-1].start()
        for t in range(n):
            copy(t, 0, sibling, me).wait_recv()
            for j, chip in enumerate(chips):
                copy(t, 4 + j, (*chip, 1 - mc), me).wait_recv()
        for cp in first + passed:
            cp.wait_send()
        for t in range(n):
            mine[t].wait()

    ins = list(xs) + ([after] if after is not None else [])
    anywhere = pl.BlockSpec(memory_space=pl.ANY)
    return list(_pcall(body, name=name, out_shape=[SDS((N_DEV,) + x.shape, x.dtype) for x in xs],
                       in_specs=[anywhere] * len(ins), out_specs=[anywhere] * n,
                       scratch_shapes=[pltpu.SemaphoreType.DMA((7 * n,)), pltpu.SemaphoreType.DMA((7 * n,)),
                                       pltpu.SemaphoreType.DMA((n,))])(*ins))


HBM_SPEC = pl.BlockSpec(memory_space=pltpu.HBM)
SEM_SPEC = pl.BlockSpec(memory_space=pltpu.SEMAPHORE)
ANY_SPEC = pl.BlockSpec(memory_space=pl.ANY)
DATAFLOW = pltpu.SideEffectType.DATAFLOW_SIDE_EFFECTING


def _peers():
    mx, my, mc = _my_place()
    out = []
    for k in range(1, N_DEV):
        tx, ty, tc = mx ^ ((k >> 2) & 1), my ^ ((k >> 1) & 1), mc ^ (k & 1)
        out.append(((tx, ty, tc), 4 * tx + 2 * ty + tc))
    return out


def _in_hbm(arrays):
    return [pltpu.with_memory_space_constraint(a, pltpu.HBM) for a in arrays]


N_FIRST_LEG, N_SECOND_LEG = 4, 3


def _chip_peers():
    mx, my, mc = _my_place()
    return [((tx, ty, mc), 4 * tx + 2 * ty + mc) for tx, ty in ((1 - mx, my), (mx, 1 - my), (1 - mx, 1 - my))]


def _gather_start(bufs, after, name):
    n = len(bufs)

    def body(*refs):
        ins, s_sem, r_sem, token = refs[:n], refs[n + 1], refs[n + 2], refs[-1]
        mx, my, mc = _my_place()
        me = 4 * mx + 2 * my + mc
        for t in range(n):
            for dev in [(mx, my, 1 - mc)] + [d for d, _ in _chip_peers()]:
                pltpu.make_async_remote_copy(src_ref=ins[t].at[me], dst_ref=ins[t].at[me], send_sem=s_sem.at[t],
                                             recv_sem=r_sem.at[t], device_id=dev, device_id_type=MESH).start()
        token[...] = jnp.zeros_like(token)

    out_shape = ((pltpu.SemaphoreType.DMA((n,)), pltpu.SemaphoreType.DMA((n,))) + tuple(pltpu.HBM(b.shape, b.dtype) for b in bufs)
                 + (SDS((8, 128), f32),))
    out = _pcall(body, name=name, out_shape=out_shape, in_specs=(HBM_SPEC,) * n + (ANY_SPEC,),
                 out_specs=(SEM_SPEC, SEM_SPEC) + (HBM_SPEC,) * n + (pl.BlockSpec(memory_space=pltpu.VMEM),),
                 input_output_aliases={i: 2 + i for i in range(n)},
                 compiler_params=pltpu.CompilerParams(has_side_effects=DATAFLOW))(*_in_hbm(bufs), after)
    return (out[0], out[1]), list(out[2:2 + n]), out[-1]


def _gather_forward(bufs, sems, after, name, fresh=()):
    n, nf = len(bufs), len(fresh)
    both = list(bufs) + list(fresh)

    nsem = 4 if nf else 2

    def body(*refs):
        ins, news, s1, r1 = refs[:n], refs[n:n + nf], refs[n + nf], refs[n + nf + 1]
        s2, r2 = refs[n + nf + 3:n + nf + 5]
        sf, rf = refs[n + nf + 5:n + nf + 7] if nf else (None, None)
        token = refs[-1]
        mx, my, mc = _my_place()
        me, sibling = 4 * mx + 2 * my + mc, (mx, my, 1 - mc)
        for t in range(n):
            legs = ins[t].at[pl.ds(0, N_FIRST_LEG)]
            first = pltpu.make_async_remote_copy(src_ref=legs, dst_ref=legs, send_sem=s1.at[t], recv_sem=r1.at[t],
                                                 device_id=sibling, device_id_type=MESH)
            first.wait_send()
            first.wait_recv()
        for t in range(n):
            for _, idx in _chip_peers():
                pltpu.make_async_remote_copy(src_ref=ins[t].at[idx], dst_ref=ins[t].at[idx], send_sem=s2.at[t],
                                             recv_sem=r2.at[t], device_id=sibling, device_id_type=MESH).start()
        for t in range(nf):
            for dev in [sibling] + [d for d, _ in _chip_peers()]:
                pltpu.make_async_remote_copy(src_ref=news[t].at[me], dst_ref=news[t].at[me], send_sem=sf.at[t],
                                             recv_sem=rf.at[t], device_id=dev, device_id_type=MESH).start()
        token[...] = jnp.zeros_like(token)

    sem = lambda k: pltpu.SemaphoreType.DMA((k,))
    sem_shapes = (sem(n), sem(n)) + ((sem(nf), sem(nf)) if nf else ())
    out_shape = sem_shapes + tuple(pltpu.HBM(b.shape, b.dtype) for b in both) + (SDS((8, 128), f32),)
    out = _pcall(body, name=name, out_shape=out_shape, in_specs=(HBM_SPEC,) * (n + nf) + (SEM_SPEC, SEM_SPEC, ANY_SPEC),
                 out_specs=(SEM_SPEC,) * nsem + (HBM_SPEC,) * (n + nf) + (pl.BlockSpec(memory_space=pltpu.VMEM),),
                 input_output_aliases={i: nsem + i for i in range(n + nf)},
                 compiler_params=pltpu.CompilerParams(has_side_effects=DATAFLOW))(*bufs, *_in_hbm(fresh), sems[0], sems[1], after)
    fresh_sems = (out[2], out[3]) if nf else None
    return (out[0], out[1]), list(out[nsem:nsem + n]), fresh_sems, list(out[nsem + n:nsem + n + nf]), out[-1]


def _exchange_start(sends, lands, name):
    nt = len(sends)

    def body(*refs):
        srcs, dsts, s_sem, r_sem, token = refs[:nt], refs[nt:2 * nt], refs[2 * nt], refs[2 * nt + 1], refs[-1]
        mx, my, mc = _my_place()
        me = 4 * mx + 2 * my + mc
        for t in range(nt):
            for dev, idx in _peers():
                pltpu.make_async_remote_copy(src_ref=srcs[t].at[idx], dst_ref=dsts[t].at[me], send_sem=s_sem.at[t],
                                             recv_sem=r_sem.at[t], device_id=dev, device_id_type=MESH).start()
        token[...] = jnp.zeros_like(token)

    both = list(sends) + list(lands)
    out_shape = (pltpu.SemaphoreType.DMA((nt,)), pltpu.SemaphoreType.DMA((nt,))) + tuple(pltpu.HBM(b.shape, b.dtype) for b in both)
    out = _pcall(body, name=name, out_shape=out_shape + (SDS((8, 128), f32),), in_specs=(HBM_SPEC,) * (2 * nt),
                 out_specs=(SEM_SPEC, SEM_SPEC) + (HBM_SPEC,) * (2 * nt) + (pl.BlockSpec(memory_space=pltpu.VMEM),),
                 input_output_aliases={i: 2 + i for i in range(2 * nt)},
                 compiler_params=pltpu.CompilerParams(has_side_effects=DATAFLOW))(*_in_hbm(both))
    return (out[0], out[1]), list(out[2:2 + nt]), list(out[2 + nt:2 + 2 * nt]), out[-1]


def _copies_wait(srcs, dsts, sems, after, name, ncopies=N_DEV - 1):
    nt = len(srcs)
    same = srcs is dsts
    bufs = list(srcs) if same else list(srcs) + list(dsts)
    nb = len(bufs)

    def body(*refs):
        s_sem, r_sem = refs[nb], refs[nb + 1]
        mx, my, mc = _my_place()
        for t in range(nt):
            src = refs[t].at[pl.ds(0, ncopies)]
            dst = src if same else refs[nt + t].at[pl.ds(0, ncopies)]
            cp = pltpu.make_async_remote_copy(src_ref=src, dst_ref=dst, send_sem=s_sem.at[t], recv_sem=r_sem.at[t],
                                              device_id=(mx, my, 1 - mc), device_id_type=MESH)
            cp.wait_send()
            cp.wait_recv()

    out = _pcall(body, name=name, out_shape=tuple(pltpu.HBM(b.shape, b.dtype) for b in bufs),
                 in_specs=(HBM_SPEC,) * nb + (SEM_SPEC, SEM_SPEC, ANY_SPEC), out_specs=(HBM_SPEC,) * nb,
                 input_output_aliases={i: i for i in range(nb)},
                 compiler_params=pltpu.CompilerParams(has_side_effects=DATAFLOW))(*bufs, sems[0], sems[1], after)
    out = list(out)
    return (out, out) if same else (out[:nt], out[nt:])


def _cast_blocks(weights, l, me, after, name):
    nt = len(weights)
    halves = 2
    extra = [] if after is None else [after]

    def body(me_ref, *refs):
        outs = refs[nt + len(extra):]
        for t in range(nt):
            outs[t][...] = refs[t][...].astype(bf16)

    in_specs, out_specs, out_shape = [], [], []
    for wt in weights:
        _, r, c = wt.shape
        tr = r // halves
        in_specs.append(pl.BlockSpec((None, tr, c), lambda i, me_ref: (l, i, 0)))
        out_specs.append(pl.BlockSpec((None, tr, c), lambda i, me_ref: (me_ref[0], i, 0)))
        out_shape.append(SDS((N_DEV, r, c), bf16))
    in_specs += [pl.BlockSpec(memory_space=pl.ANY)] * len(extra)
    grid_spec = pltpu.PrefetchScalarGridSpec(num_scalar_prefetch=1, grid=(halves,), in_specs=in_specs, out_specs=out_specs)
    return list(_pcall(body, name=name, out_shape=out_shape, grid_spec=grid_spec, compiler_params=_cp(("parallel",)))(me, *weights, *extra))

def _layer_fwd_mix(x, p, bias, l, after=None):
    tag = f"l{l}"
    z = _mm(x, p["w_in"], "nt", 1024, 1408, f"{tag}_mm_in", after=after, norm_a=p["norm_mix_g"])
    yb, lt = _attn_fwd(z, bias, f"{tag}_attn_fwd")
    y = _mixer_fwd(z, yb, p["conv_a_w"], p["conv_c_w"], p["conv_c_b"], p["ln_c_g"], p["ln_c_b"], p["out_norm_g"], f"{tag}_mixer_fwd")
    return y, dict(x=x, z=z, yb=yb, lt=lt, y=y)


def _layer_fwd_ffn(x, y, p, l, after=None):
    tag = f"l{l}"
    x1 = _mm(y, p["w_out"], "nn", 512, 1024, f"{tag}_mm_out", res=x, after=after)
    upre = _mm(x1, p["w_up"], "nt", 1024, 1408, f"{tag}_mm_up", norm_a=p["norm_ffn_g"])
    act = _ffn_act_fwd(upre, p["conv_f_w"], f"{tag}_ffn_act")
    x2 = _mm(act, p["w_down"], "nn", 512, 1024, f"{tag}_mm_down", res=x1)
    return x2, dict(x1=x1, upre=upre, act=act)


def _layer_bwd_ffn(dx2, p, sv, l, after=None):
    tag = f"l{l}"
    g = {}
    dact = _mm(dx2, p["w_down"], "nt", 512, 1408, f"{tag}_mm_dact", after=after, b_outer=True)
    dwd = _mm(sv["act"], dx2, "tn", 1408, 1024, f"{tag}_mm_dwdown", out_dtype=bf16)
    g["w_down"] = dwd.reshape(N_DEV, dwd.shape[0] // N_DEV, dwd.shape[1])
    dug, duv, dwg, dwv = _ffn_act_bwd(sv["upre"], dact, p["conv_f_w"], f"{tag}_ffn_act_bwd")
    g["conv_f_w"] = jnp.concatenate([dwg[:K_FFN], dwv[:K_FFN]], axis=1)
    dupre = jnp.concatenate([dug, duv], axis=1)
    dwu = _mm(dupre, sv["x1"], "tn", 1408, 1024, f"{tag}_mm_dwup", out_dtype=bf16, norm_b=p["norm_ffn_g"])
    g["w_up"] = dwu.reshape(N_DEV, dwu.shape[0] // N_DEV, dwu.shape[1])
    dx1, g["norm_ffn_g"] = _mm(dupre, p["w_up"], "nn", 512, 1024, f"{tag}_mm_dh2", rms_bwd=(sv["x1"], p["norm_ffn_g"], dx2))
    return dx1, g


def _layer_bwd_mix(dx1, p, sv, bias, l, after=None):
    tag = f"l{l}"
    g = {}
    dy = _mm(dx1, p["w_out"], "nt", 512, 1024, f"{tag}_mm_dy", after=after)
    dwo = _mm(sv["y"], dx1, "tn", 512, 1024, f"{tag}_mm_dwout", out_dtype=bf16)
    g["w_out"] = dwo.reshape(N_DEV, dwo.shape[0] // N_DEV, dwo.shape[1])
    dza, dzc, dyb, pg, dgb = _mixer_bwd(sv["z"], sv["yb"], dy, p["conv_a_w"], p["conv_c_w"], p["conv_c_b"], p["ln_c_g"],
                                        p["ln_c_b"], p["out_norm_g"], f"{tag}_mixer_bwd")
    g["conv_a_w"] = pg[PG_CONV_A:PG_CONV_A + K_SHORT]
    g["conv_c_w"] = pg[PG_CONV_C:PG_CONV_C + K_CONF]
    g["conv_c_b"], g["ln_c_g"], g["ln_c_b"] = pg[PG_CB:PG_CB + 1], pg[PG_LNG:PG_LNG + 1], pg[PG_LNB:PG_LNB + 1]
    g["out_norm_g"] = jnp.concatenate([pg[PG_OG_A:PG_OG_A + 1], dgb, pg[PG_OG_C:PG_OG_C + 1]], axis=1)
    dq, dk, dv, tabs = _attn_bwd(sv["z"], dyb, sv["yb"], sv["lt"], bias, f"{tag}_attn_bwd")
    dz = jnp.concatenate([dza, dq, dk, dv, dzc], axis=1)
    dwi = _mm(dz, sv["x"], "tn", 1408, 1024, f"{tag}_mm_dwin", out_dtype=bf16, norm_b=p["norm_mix_g"])
    g["w_in"] = dwi.reshape(N_DEV, dwi.shape[0] // N_DEV, dwi.shape[1])
    dx, g["norm_mix_g"] = _mm(dz, p["w_in"], "nn", 512, 1024, f"{tag}_mm_dh", rms_bwd=(sv["x"], p["norm_mix_g"], dx1))
    return dx, g, tabs


SMALL_LAYER_PARAMS = ("norm_mix_g", "conv_a_w", "conv_c_w", "conv_c_b", "ln_c_g", "ln_c_b", "out_norm_g", "norm_ffn_g", "conv_f_w")


BIG = ("w_in", "w_out", "w_up", "w_down")
TRANSPOSED = ("w_in", "w_up")
ADAM_ROWS = {"w_in": 176, "w_out": 128, "w_up": 176, "w_down": 176}
SMALL_SHARDED = ("conv_a_w", "conv_c_w", "conv_f_w")
SMALL_REPLICATED = ("norm_mix_g", "conv_c_b", "ln_c_g", "ln_c_b", "out_norm_g", "norm_ffn_g", "rel_bias", "final_g")
WEIGHTS = ("norm_mix_g", "w_in", "conv_a_w", "conv_c_w", "conv_c_b", "ln_c_g", "ln_c_b", "out_norm_g", "w_out",
           "norm_ffn_g", "w_up", "conv_f_w", "w_down", "rel_bias", "final_g")


def _pack(arrays):
    flat = jnp.concatenate([a.reshape(-1) for a in arrays])
    rows = -(-flat.shape[0] // 1024) * 8
    return jnp.pad(flat, (0, rows * 128 - flat.shape[0])).reshape(rows, 128)


def _unpack(packed, shapes):
    flat, out, at = packed.reshape(-1), [], 0
    for shp in shapes:
        n = int(np.prod(shp))
        out.append(flat[at:at + n].reshape(shp))
        at += n
    return out


def _gather_shards(g, local_shape):
    nd = len(local_shape)
    return jnp.moveaxis(g, 0, nd - 1).reshape(local_shape[:-1] + (N_DEV * local_shape[-1],))


def kernel(x, norm_mix_g, w_in, conv_a_w, conv_c_w, conv_c_b, ln_c_g, ln_c_b, out_norm_g, w_out, norm_ffn_g, w_up, conv_f_w, w_down, rel_bias, final_g, loss_target, m_norm_mix_g, m_w_in, m_conv_a_w, m_conv_c_w, m_conv_c_b, m_ln_c_g, m_ln_c_b, m_out_norm_g, m_w_out, m_norm_ffn_g, m_w_up, m_conv_f_w, m_w_down, m_rel_bias, m_final_g, v_norm_mix_g, v_w_in, v_conv_a_w, v_conv_c_w, v_conv_c_b, v_ln_c_g, v_ln_c_b, v_out_norm_g, v_w_out, v_norm_ffn_g, v_w_up, v_conv_f_w, v_w_down, v_rel_bias, v_final_g):
    w = dict(norm_mix_g=norm_mix_g, w_in=w_in, conv_a_w=conv_a_w, conv_c_w=conv_c_w, conv_c_b=conv_c_b, ln_c_g=ln_c_g,
             ln_c_b=ln_c_b, out_norm_g=out_norm_g, w_out=w_out, norm_ffn_g=norm_ffn_g, w_up=w_up, conv_f_w=conv_f_w,
             w_down=w_down, rel_bias=rel_bias, final_g=final_g)
    mom = dict(norm_mix_g=m_norm_mix_g, w_in=m_w_in, conv_a_w=m_conv_a_w, conv_c_w=m_conv_c_w, conv_c_b=m_conv_c_b,
               ln_c_g=m_ln_c_g, ln_c_b=m_ln_c_b, out_norm_g=m_out_norm_g, w_out=m_w_out, norm_ffn_g=m_norm_ffn_g,
               w_up=m_w_up, conv_f_w=m_conv_f_w, w_down=m_w_down, rel_bias=m_rel_bias, final_g=m_final_g)
    var = dict(norm_mix_g=v_norm_mix_g, w_in=v_w_in, conv_a_w=v_conv_a_w, conv_c_w=v_conv_c_w, conv_c_b=v_conv_c_b,
               ln_c_g=v_ln_c_g, ln_c_b=v_ln_c_b, out_norm_g=v_out_norm_g, w_out=v_w_out, norm_ffn_g=v_norm_ffn_g,
               w_up=v_w_up, conv_f_w=v_conv_f_w, w_down=v_w_down, rel_bias=v_rel_bias, final_g=v_final_g)
    depth = w_in.shape[0]
    me = 4 * lax.axis_index("x") + 2 * lax.axis_index("y") + lax.axis_index("c")
    for k in TRANSPOSED:
        w[k], mom[k], var[k] = (jnp.transpose(a, (0, 2, 1)) for a in (w[k], mom[k], var[k]))

    me_arr = me.astype(jnp.int32).reshape(1)
    bufs = [_cast_blocks([w[k] for k in BIG], 0, me_arr, None, "l0_cast_blocks")]
    in_sems, in_bufs, token = _gather_start(bufs[0][:1], me_arr, "l0_gather_start_in")
    bufs += [_cast_blocks([w[k] for k in BIG], l, me_arr, token, f"l{l}_cast_blocks") for l in range(1, depth)]
    params = {}
    gathered = _all_gather([w[k] for k in SMALL_SHARDED], "gather_small", after=token)
    gs = gathered[0]
    for k, g8 in zip(SMALL_SHARDED, gathered):
        params[k] = _gather_shards(g8, w[k].shape)
    for k in ("norm_mix_g", "conv_c_b", "ln_c_g", "ln_c_b", "out_norm_g", "norm_ffn_g"):
        params[k] = w[k][:, None, :]
    params["rel_bias"] = rel_bias
    params["final_g"] = final_g[None, :]

    def full(k, buf, l):
        return buf.reshape(N_DEV * buf.shape[1], buf.shape[2])

    bias = _bias_tables(rel_bias, "bias_tables").reshape(len(DILATIONS), 2, 4, 2 * BLK, 2 * BLK)
    in_sems, in_bufs, rest_sems, rest_bufs, token = _gather_forward(in_bufs, in_sems, gs, "l0_gather_forward_in", fresh=bufs[0][1:])
    xl, after = x[0], token
    layer_params, saved = [], []
    for l in range(depth):
        _, in_bufs = _copies_wait(in_bufs, in_bufs, in_sems, after, f"l{l}_gather_wait_in", N_SECOND_LEG)
        p = {k: params[k][l] for k in SMALL_LAYER_PARAMS}
        p["w_in"] = full("w_in", in_bufs[0], l)
        token = in_bufs[0]
        nxt_in = bufs[l + 1][:1] if l + 1 < depth else []
        if l > 0:
            rest_sems, rest_bufs, in_sems, in_bufs, token = _gather_forward(rest_bufs, rest_sems, token, f"l{l}_gather_forward_rest", fresh=nxt_in)
        else:
            in_sems, in_bufs, token = _gather_start(nxt_in, token, "l1_gather_start_in")
        y, sv = _layer_fwd_mix(xl, p, bias, l, after=token)
        if l == 0:
            rest_sems, rest_bufs, _, _, _ = _gather_forward(rest_bufs, rest_sems, y, "l0_gather_forward_rest")
        _, rest_bufs = _copies_wait(rest_bufs, rest_bufs, rest_sems, y, f"l{l}_gather_wait_rest", N_SECOND_LEG)
        for k, buf in zip(BIG[1:], rest_bufs):
            p[k] = full(k, buf, l)
        token = rest_bufs[0]
        if l + 1 < depth:
            in_sems, in_bufs, rest_sems, rest_bufs, token = _gather_forward(in_bufs, in_sems, token, f"l{l + 1}_gather_forward_in",
                                                                             fresh=bufs[l + 1][1:])
        xl, sv2 = _layer_fwd_ffn(xl, y, p, l, after=token)
        layer_params.append(p)
        saved.append({**sv, **sv2})
        after = xl
    loss, dx, d_final = _final_loss(xl, params["final_g"], loss_target[0], "final_loss")
    loss = lax.psum(loss[0, 0], ("x", "y", "c"))

    grads, tabs, pending = [None] * depth, [None] * depth, []
    after = None
    for l in reversed(range(depth)):
        dx1, g_ffn = _layer_bwd_ffn(dx, layer_params[l], saved[l], l, after=after)
        names = ("w_up", "w_down")
        sends = [g_ffn[k] for k in names]
        sems, sends, lands, after = _exchange_start(sends, [lax.empty(s.shape, s.dtype) for s in sends], f"l{l}_exchange_start_ffn")
        pending.append((l, "ffn", names, sems, sends, lands))
        dx, g_mix, tabs[l] = _layer_bwd_mix(dx1, layer_params[l], saved[l], bias, l, after=after)
        names = ("w_out", "w_in")
        sends = [g_mix[k] for k in names]
        sems, sends, lands, after = _exchange_start(sends, [lax.empty(s.shape, s.dtype) for s in sends], f"l{l}_exchange_start_mix")
        pending.append((l, "mix", names, sems, sends, lands))
        grads[l] = {**g_ffn, **g_mix}
    grads_small = {k: jnp.stack([grads[l][k] for l in range(depth)]) for k in SMALL_LAYER_PARAMS}
    grads_small["rel_bias"] = _bias_grad(jnp.stack(tabs).reshape(depth, len(DILATIONS), 8, BLK, 2 * BLK), "bias_grad")[:, :8]
    grads_small["final_g"] = d_final

    out_g, out_d, out_m, out_v = {}, {}, {}, {}
    grads = grads_small
    small = SMALL_SHARDED + SMALL_REPLICATED
    full_shapes = [params[k].shape if k in SMALL_SHARDED else w[k].shape for k in small]
    packed = _pack([grads[k].reshape(shp) for k, shp in zip(small, full_shapes)])
    gsum = _sum_parts(_all_gather([packed], "gather_small_grads", after=after)[0], "sum_small_grads")
    gfull = dict(zip(small, _unpack(gsum, full_shapes)))
    for k in SMALL_SHARDED:
        shp = w[k].shape
        gfull[k] = lax.dynamic_index_in_dim(gfull[k].reshape(shp[:-1] + (N_DEV, shp[-1])), me, axis=len(shp) - 1, keepdims=False)
    two_d = lambda a: a.reshape(1, -1) if a.ndim == 1 else a
    res = _adamw_small(*[[two_d(src[k]) for k in small] for src in (gfull, w, mom, var)], "adamw_small")
    for name_map, outs in zip((out_d, out_m, out_v), res):
        name_map.update({k: o.reshape(w[k].shape) for k, o in zip(small, outs)})
    out_g.update(gfull)

    results = {k: [lax.empty(w[k].shape, f32) for _ in range(4)] for k in BIG}
    after = res[0][0]
    for l, part, names, sems, sends, lands in pending:
        sends, lands = _copies_wait(sends, lands, sems, after, f"l{l}_exchange_wait_{part}")
        for k, send, land in zip(names, sends, lands):
            results[k] = _adamw_layer(land, send, w[k], mom[k], var[k], results[k], l, ADAM_ROWS[k], me_arr, f"l{l}_adamw_{k}")
            after = results[k][0]
    for k in BIG:
        big = [jnp.transpose(a, (0, 2, 1)) for a in results[k]] if k in TRANSPOSED else results[k]
        out_g[k], out_d[k], out_m[k], out_v[k] = big

    return (loss, dx[None], *[out_g[k] for k in WEIGHTS], *[out_d[k] for k in WEIGHTS], *[out_m[k] for k in WEIGHTS],
            *[out_v[k] for k in WEIGHTS])
```

```python
import math

import numpy as np
import jax
import jax.numpy as jnp
from jax import lax
from jax.experimental import pallas as pl
from jax.experimental.pallas import tpu as pltpu

f32, bf16 = jnp.float32, jnp.bfloat16
SDS = jax.ShapeDtypeStruct
MESH = pl.DeviceIdType.MESH

N_DEV = 8
EPS = 1e-6
NEG = -1e30
D_HEAD = 64
BLK = 128
N_BUCKETS = 32
DILATIONS = (1, 4, 16)
N_KEYS = 128
K_SHORT, K_CONF, K_FFN = 3, 31, 3
ROW_TILE = 256
HALO = 32
HALO_FFN = 8
FFN_FWD_ROWS, FFN_BWD_ROWS = 1024, 1024
FFN_CHUNK = 64
VMEM_LIMIT = 56 << 20
ADAM_LR, ADAM_B1, ADAM_B2, ADAM_EPS, ADAM_WD, ADAM_STEP = 0.001, 0.9, 0.999, 1e-08, 0.01, 10


def _pcall(body, **kw):
    return pl.pallas_call(body, **kw)


def _cp(sem=None):
    kw = dict(vmem_limit_bytes=VMEM_LIMIT)
    if sem is not None:
        kw["dimension_semantics"] = sem
    return pltpu.CompilerParams(**kw)


def _t5_buckets():
    rel = np.arange(BLK)[:, None] - np.arange(2 * BLK)[None, :] + BLK
    out = []
    for d in DILATIONS:
        dist = np.maximum(rel, 0) * d
        max_exact = N_BUCKETS // 2
        d_f = np.maximum(dist, 1).astype(np.float32)
        large = max_exact + (np.log(d_f / np.float32(max_exact)) / np.float32(math.log(2048 / max_exact))
                             * np.float32(N_BUCKETS - max_exact)).astype(np.int32)
        large = np.minimum(large, N_BUCKETS - 1)
        out.append(np.where(dist < max_exact, dist, large).astype(np.int32))
    return np.stack(out)


_BUCKETS = _t5_buckets()


def _mm(a, b, mode, tr, tc, name, res=None, out_dtype=f32, after=None, b_outer=False, norm_a=None, norm_b=None, rms_bwd=None):
    if rms_bwd is not None:
        return _mm_rms_bwd(a, b, tr, name, *rms_bwd)
    ij = (lambda f: (lambda j, i: f(i, j))) if b_outer else (lambda f: f)
    if mode == "nn":
        (m, k), n = a.shape, b.shape[1]
        a_spec = pl.BlockSpec((tr, k), ij(lambda i, j: (i, 0)))
        b_spec = pl.BlockSpec((k, tc), ij(lambda i, j: (0, j)))
        dims = (((1,), (0,)), ((), ()))
    elif mode == "nt":
        (m, k), n = a.shape, b.shape[0]
        a_spec = pl.BlockSpec((tr, k), ij(lambda i, j: (i, 0)))
        b_spec = pl.BlockSpec((tc, k), ij(lambda i, j: (j, 0)))
        dims = (((1,), (1,)), ((), ()))
    else:
        (k, m), n = a.shape, b.shape[1]
        a_spec = pl.BlockSpec((k, tr), ij(lambda i, j: (0, i)))
        b_spec = pl.BlockSpec((k, tc), ij(lambda i, j: (0, j)))
        dims = (((0,), (0,)), ((), ()))
    assert m % tr == 0 and n % tc == 0, (name, m, n, tr, tc)
    o_spec = pl.BlockSpec((tr, tc), ij(lambda i, j: (i, j)))
    grid = (n // tc, m // tr) if b_outer else (m // tr, n // tc)

    gain = norm_a if norm_a is not None else norm_b
    assert norm_b is None or (mode == "tn" and tc == n and not b_outer)
    assert norm_a is None or (mode != "tn" and not b_outer)

    def body(a_ref, b_ref, *rest):
        o_ref = rest[-2] if gain is not None else rest[-1]
        if norm_a is not None:
            ha_sc = rest[-1]

            @pl.when(pl.program_id(1) == 0)
            def _():
                av = a_ref[...]
                ha_sc[...] = (av * _rstd(av) * rest[0][...]).astype(bf16)

            av = ha_sc[...]
        else:
            av = a_ref[...].astype(bf16)
        if norm_b is not None:
            hb_sc = rest[-1]

            @pl.when(pl.program_id(0) == 0)
            def _():
                bv = b_ref[...]
                hb_sc[...] = (bv * _rstd(bv) * rest[0][...]).astype(bf16)

            bv = hb_sc[...]
        else:
            bv = b_ref[...].astype(bf16)
        acc = lax.dot_general(av, bv, dims, preferred_element_type=f32)
        if res is not None:
            acc = acc + rest[1 if gain is not None else 0][...]
        o_ref[...] = acc.astype(out_dtype)

    vec = pl.BlockSpec((1, k if norm_a is not None else n), lambda i, j: (0, 0))
    ins = [a, b] + ([gain] if gain is not None else []) + ([res] if res is not None else []) + ([after] if after is not None else [])
    specs = ([a_spec, b_spec] + ([vec] if gain is not None else []) + ([o_spec] if res is not None else [])
             + ([pl.BlockSpec(memory_space=pl.ANY)] if after is not None else []))
    scratch = [pltpu.VMEM((k, n), bf16)] if norm_b is not None else [pltpu.VMEM((tr, k), bf16)] if norm_a is not None else []
    sem = ("arbitrary", "arbitrary") if gain is not None else ("parallel", "parallel")
    return _pcall(body, name=name, out_shape=SDS((m, n), out_dtype), grid=grid, in_specs=specs, out_specs=o_spec,
                  scratch_shapes=scratch, compiler_params=_cp(sem))(*ins)


def _mm_rms_bwd(a, b, tr, name, x, g, dres, stack, l):
    (m, k), n = a.shape, b.shape[1]

    def body(a_ref, b_ref, x_ref, g_ref, dres_ref, stack_ref, dx_ref, dg_ref):
        dh = jnp.dot(a_ref[...].astype(bf16), b_ref[...].astype(bf16), preferred_element_type=f32)
        xv = x_ref[...]
        r = _rstd(xv)
        dx_ref[...] = dres_ref[...] + _rms_dx(xv, r, g_ref[...], dh)

        @pl.when(pl.program_id(0) == 0)
        def _():
            dg_ref[...] = jnp.zeros_like(dg_ref)

        dg_ref[...] += _colsum(dh * xv * r)

    row = pl.BlockSpec((tr, n), lambda i: (i, 0))
    vec = pl.BlockSpec((1, n), lambda i: (0, 0))
    return _pcall(body, name=name, out_shape=(SDS((m, n), f32), SDS(stack.shape, f32)), grid=(m // tr,),
                  in_specs=[pl.BlockSpec((tr, k), lambda i: (i, 0)), pl.BlockSpec((k, n), lambda i: (0, 0), pipeline_mode=pl.Buffered(1)),
                            row, vec, row, pl.BlockSpec(memory_space=pl.ANY)],
                  out_specs=(row, pl.BlockSpec((None, 1, n), lambda i: (l, 0, 0))), input_output_aliases={5: 1},
                  compiler_params=_cp(("arbitrary",)))(a, b, x, g, dres, stack)


def _rstd(x):
    return lax.rsqrt(jnp.mean(x * x, axis=-1, keepdims=True) + EPS)


def _rms_dx(x, r, g, dy):
    dxh = dy * g
    return r * (dxh - x * (r * r) * jnp.mean(dxh * x, axis=-1, keepdims=True))


def _sigmoid(x):
    return 0.5 * jnp.tanh(0.5 * x) + 0.5


def _colsum(x):
    return jnp.sum(x, axis=0, keepdims=True)


def _final_loss(x, g, target, name):
    s, d = x.shape

    def body(x_ref, g_ref, t_ref, loss_ref, dx_ref, dg_ref):
        xv, gv = x_ref[...], g_ref[...]
        r = _rstd(xv)
        diff = xv * r * gv - t_ref[...]
        dy = diff * (1.0 / d)

        @pl.when(pl.program_id(0) == 0)
        def _():
            dg_ref[...] = jnp.zeros_like(dg_ref)
            loss_ref[...] = jnp.zeros_like(loss_ref)

        part = jnp.sum(jnp.sum(diff * diff, axis=1, keepdims=True), axis=0, keepdims=True) * (0.5 / d)
        loss_ref[...] += part
        dx_ref[...] = _rms_dx(xv, r, gv, dy)
        dg_ref[...] += _colsum(dy * xv * r)

    row = pl.BlockSpec((ROW_TILE, d), lambda i: (i, 0))
    vec = pl.BlockSpec((1, d), lambda i: (0, 0))
    one = pl.BlockSpec((8, 128), lambda i: (0, 0))
    return _pcall(body, name=name, out_shape=(SDS((8, 128), f32), SDS((s, d), f32), SDS((1, d), f32)),
                  grid=(s // ROW_TILE,), in_specs=[row, vec, row], out_specs=(one, row, vec),
                  compiler_params=_cp(("arbitrary",)))(x, g, target)


Z_AH, Z_AB, Z_AC, Z_CV, Z_CG = 0, 1, 2, 9, 10
Z_Q128, Z_K128, Z_V128 = 6, 10, 14


def _cur(t, w, col):
    return pl.BlockSpec((t, w), lambda i: (i, col))


def _prev(hb, t, w, col):
    return pl.BlockSpec((hb, w), lambda i: (jnp.maximum(i * (t // hb) - 1, 0), col))


def _next(hb, t, w, col, rows):
    return pl.BlockSpec((hb, w), lambda i: (jnp.minimum((i + 1) * (t // hb), rows // hb - 1), col))


def _layernorm_parts(u, g, b):
    mu = jnp.mean(u, axis=-1, keepdims=True)
    var = jnp.mean(jnp.square(u - mu), axis=-1, keepdims=True)
    rs = lax.rsqrt(var + EPS)
    xhat = (u - mu) * rs
    return xhat, rs, xhat * g + b


def _fill_shifted(rot):
    n = rot.shape[1]
    for r in range(1, 8):
        rot[r, pl.ds(0, n - 8), :] = rot[0, pl.ds(r, n - 8), :]


def _window(rot, offset, rows):
    return rot[offset % 8, pl.ds(offset - offset % 8, rows), :]


def _mixer_fwd(z, yb, conv_a, conv_c, cb, lng, lnb, og, name):
    s = z.shape[0]
    t, hb, w = ROW_TILE, HALO, 256

    def body(ah, ab, ac, cv, cg, ah_p, ac_p, cv_p, cg_p, yb_ref, wa, wc, cb_ref, lng_ref, lnb_ref, og_ref, y_ref, p_sc, u_sc):
        first = pl.program_id(0) == 0

        def prev(ref):
            return jnp.where(first, 0.0, ref[...])

        p_sc[pl.ds(0, hb), :] = prev(ah_p) * prev(ac_p)
        p_sc[pl.ds(hb, t), :] = ah[...] * ac[...]
        conv = jnp.zeros((t, w), f32)
        for k in range(K_SHORT):
            conv = conv + wa[pl.ds(k, 1), :] * p_sc[pl.ds(hb - (K_SHORT - 1) + k, t), :]
        y_a = ab[...] * conv
        u_sc[0, pl.ds(0, hb), :] = prev(cv_p) * _sigmoid(prev(cg_p))
        u_sc[0, pl.ds(hb, t), :] = cv[...] * _sigmoid(cg[...])
        _fill_shifted(u_sc)
        u = jnp.zeros((t, w), f32) + cb_ref[...]
        for k in range(K_CONF):
            u = u + wc[pl.ds(k, 1), :] * _window(u_sc, hb - (K_CONF - 1) + k, t)
        _, _, lnv = _layernorm_parts(u, lng_ref[...], lnb_ref[...])
        y_c = lnv * _sigmoid(lnv)
        ybv = yb_ref[...]
        y_ref[:, 0:256] = (y_a * _rstd(y_a) * og_ref[:, 0:256]).astype(bf16)
        y_ref[:, 256:768] = (ybv * _rstd(ybv) * og_ref[:, 256:768]).astype(bf16)
        y_ref[:, 768:1024] = (y_c * _rstd(y_c) * og_ref[:, 768:1024]).astype(bf16)

    full = lambda shape: pl.BlockSpec(shape, lambda i: tuple(0 for _ in shape))
    in_specs = [_cur(t, w, c) for c in (Z_AH, Z_AB, Z_AC, Z_CV, Z_CG)] + [_prev(hb, t, w, c) for c in (Z_AH, Z_AC, Z_CV, Z_CG)]
    in_specs += [_cur(t, 512, 0), full((K_SHORT, w)), full((K_CONF, w)), full((1, w)), full((1, w)), full((1, w)), full((1, 1024))]
    return _pcall(body, name=name, out_shape=SDS((s, 1024), bf16), grid=(s // t,), in_specs=in_specs,
                  out_specs=_cur(t, 1024, 0), scratch_shapes=[pltpu.VMEM((hb + t, w), f32), pltpu.VMEM((8, hb + t, w), f32)],
                  compiler_params=_cp(("parallel",)))(z, z, z, z, z, z, z, z, z, yb, conv_a, conv_c, cb, lng, lnb, og)


PG_CONV_A, PG_CONV_C, PG_CB, PG_LNG, PG_LNB, PG_OG_A, PG_OG_C, PG_ROWS = 0, 8, 40, 41, 42, 43, 44, 48


def _mixer_bwd(z, yb, dy, conv_a, conv_c, cb, lng, lnb, og, stacks, l, name):
    s = z.shape[0]
    t, hb, w = ROW_TILE, HALO, 256
    e = t + hb

    def body(ah, ab, ac, cv, cg, ah_p, ac_p, cv_p, cg_p, ah_n, ab_n, ac_n, cv_n, cg_n,
             dya, dyb1, dyb2, dyc, dya_n, dyc_n, yb_ref, wa, wc, cb_ref, lng_ref, lnb_ref, og_ref, pg_stack, dgb_stack,
             dza_ref, dzc_ref, dyb_ref, pg_ref, dgb_ref, p_sc, u_sc, dc_sc, du_sc):
        i = pl.program_id(0)
        first, last = i == 0, i == pl.num_programs(0) - 1

        def prev(ref):
            return jnp.where(first, 0.0, ref[...])

        def nxt(ref):
            return jnp.where(last, 0.0, ref[...])

        def ext(cur_ref, next_ref):
            return jnp.concatenate([cur_ref[...], nxt(next_ref)], axis=0)

        @pl.when(first)
        def _():
            pg_ref[...] = jnp.zeros_like(pg_ref)
            dgb_ref[...] = jnp.zeros_like(dgb_ref)

        g_a = og_ref[:, 0:256]
        p_sc[pl.ds(0, hb), :] = prev(ah_p) * prev(ac_p)
        p_sc[pl.ds(hb, t), :] = ah[...] * ac[...]
        p_sc[pl.ds(hb + t, hb), :] = nxt(ah_n) * nxt(ac_n)
        conv = jnp.zeros((e, w), f32)
        for k in range(K_SHORT):
            conv = conv + wa[pl.ds(k, 1), :] * p_sc[pl.ds(hb - (K_SHORT - 1) + k, e), :]
        ab_e, dya_e = ext(ab, ab_n), ext(dya, dya_n)
        y_a = ab_e * conv
        r_a = _rstd(y_a)
        dy_a = _rms_dx(y_a, r_a, g_a, dya_e)
        dc_sc[...] = dy_a * ab_e
        dza_ref[:, 256:512] = (dy_a[:t] * conv[:t]).astype(bf16)
        dp = jnp.zeros((t, w), f32)
        for k in range(K_SHORT):
            dp = dp + wa[pl.ds(k, 1), :] * dc_sc[pl.ds(K_SHORT - 1 - k, t), :]
            pg_ref[pl.ds(PG_CONV_A + k, 1), :] += _colsum(dc_sc[pl.ds(0, t), :] * p_sc[pl.ds(hb - (K_SHORT - 1) + k, t), :])
        dza_ref[:, 0:256] = (dp * ac[...]).astype(bf16)
        dza_ref[:, 512:768] = (dp * ah[...]).astype(bf16)
        pg_ref[pl.ds(PG_OG_A, 1), :] += _colsum(dya[...] * y_a[:t] * r_a[:t])

        g_c = og_ref[:, 768:1024]
        u_sc[0, pl.ds(0, hb), :] = prev(cv_p) * _sigmoid(prev(cg_p))
        u_sc[0, pl.ds(hb, t), :] = cv[...] * _sigmoid(cg[...])
        u_sc[0, pl.ds(hb + t, hb), :] = nxt(cv_n) * _sigmoid(nxt(cg_n))
        _fill_shifted(u_sc)
        u = jnp.zeros((e, w), f32) + cb_ref[...]
        for k in range(K_CONF):
            u = u + wc[pl.ds(k, 1), :] * _window(u_sc, hb - (K_CONF - 1) + k, e)
        xhat, rs, lnv = _layernorm_parts(u, lng_ref[...], lnb_ref[...])
        sg = _sigmoid(lnv)
        y_c = lnv * sg
        r_c = _rstd(y_c)
        dyc_e = ext(dyc, dyc_n)
        dlnv = _rms_dx(y_c, r_c, g_c, dyc_e) * (sg * (1.0 + lnv * (1.0 - sg)))
        dxh = dlnv * lng_ref[...]
        du = rs * (dxh - jnp.mean(dxh, axis=-1, keepdims=True) - xhat * jnp.mean(dxh * xhat, axis=-1, keepdims=True))
        du_sc[0] = du
        _fill_shifted(du_sc)
        duin = jnp.zeros((t, w), f32)
        for k in range(K_CONF):
            duin = duin + wc[pl.ds(k, 1), :] * _window(du_sc, K_CONF - 1 - k, t)
            pg_ref[pl.ds(PG_CONV_C + k, 1), :] += _colsum(du[:t] * _window(u_sc, hb - (K_CONF - 1) + k, t))
        sgg = _sigmoid(cg[...])
        dzc_ref[:, 0:256] = (duin * sgg).astype(bf16)
        dzc_ref[:, 256:512] = (duin * cv[...] * sgg * (1.0 - sgg)).astype(bf16)
        pg_ref[pl.ds(PG_CB, 1), :] += _colsum(du[:t])
        pg_ref[pl.ds(PG_LNG, 1), :] += _colsum(dlnv[:t] * xhat[:t])
        pg_ref[pl.ds(PG_LNB, 1), :] += _colsum(dlnv[:t])
        pg_ref[pl.ds(PG_OG_C, 1), :] += _colsum(dyc[...] * y_c[:t] * r_c[:t])

        g_b = og_ref[:, 256:768]
        ybv = yb_ref[...]
        r_b = _rstd(ybv)
        dyb = jnp.concatenate([dyb1[...], dyb2[...]], axis=1)
        dyb_ref[...] = _rms_dx(ybv, r_b, g_b, dyb)
        dgb_ref[...] += _colsum(dyb * ybv * r_b)

    full = lambda shape: pl.BlockSpec(shape, lambda i: tuple(0 for _ in shape))
    zc = (Z_AH, Z_AB, Z_AC, Z_CV, Z_CG)
    in_specs = [_cur(t, w, c) for c in zc] + [_prev(hb, t, w, c) for c in (Z_AH, Z_AC, Z_CV, Z_CG)]
    in_specs += [_next(hb, t, w, c, s) for c in zc]
    in_specs += [_cur(t, w, c) for c in (0, 1, 2, 3)] + [_next(hb, t, w, 0, s), _next(hb, t, w, 3, s)]
    in_specs += [_cur(t, 512, 0), full((K_SHORT, w)), full((K_CONF, w)), full((1, w)), full((1, w)), full((1, w)), full((1, 1024))]
    in_specs += [pl.BlockSpec(memory_space=pl.ANY)] * 2
    out_shape = (SDS((s, 768), bf16), SDS((s, 512), bf16), SDS((s, 512), f32), SDS(stacks[0].shape, f32), SDS(stacks[1].shape, f32))
    out_specs = (_cur(t, 768, 0), _cur(t, 512, 0), _cur(t, 512, 0), pl.BlockSpec((None, PG_ROWS, w), lambda i: (l, 0, 0)),
                 pl.BlockSpec((None, 1, 512), lambda i: (l, 0, 0)))
    scratch = [pltpu.VMEM((hb + t + hb, w), f32), pltpu.VMEM((8, hb + t + hb, w), f32), pltpu.VMEM((e, w), f32), pltpu.VMEM((8, e, w), f32)]
    args = [z] * 14 + [dy] * 6 + [yb, conv_a, conv_c, cb, lng, lnb, og] + list(stacks)
    return _pcall(body, name=name, out_shape=out_shape, grid=(s // t,), in_specs=in_specs, out_specs=out_specs,
                  input_output_aliases={27: 3, 28: 4}, scratch_shapes=scratch, compiler_params=_cp(("arbitrary",)))(*args)


def _ffn_act_fwd(upre, conv_f, name):
    s, f2 = upre.shape
    t, hb, w = FFN_FWD_ROWS, HALO_FFN, 256
    nj = f2 // 2 // w

    def body(g_c, v_c, g_p, v_p, wg, wv, o_ref):
        first = pl.program_id(1) == 0
        wgs = [wg[pl.ds(k, 1), :] for k in range(K_FFN)]
        wvs = [wv[pl.ds(k, 1), :] for k in range(K_FFN)]
        rows = FFN_CHUNK
        first_tap = hb - (K_FFN - 1)

        def emit(gb, vb, r0):
            ug = sum(wgs[k] * gb[first_tap + k:first_tap + k + rows] for k in range(K_FFN))
            uv = sum(wvs[k] * vb[first_tap + k:first_tap + k + rows] for k in range(K_FFN))
            o_ref[pl.ds(r0, rows), :] = (ug * _sigmoid(ug) * uv).astype(bf16)

        emit(jnp.concatenate([jnp.where(first, 0.0, g_p[...]), g_c[pl.ds(0, rows), :]], axis=0),
             jnp.concatenate([jnp.where(first, 0.0, v_p[...]), v_c[pl.ds(0, rows), :]], axis=0), 0)

        def chunk(c, carry):
            r0 = pl.multiple_of(c * rows, rows)
            back = pl.ds(pl.multiple_of(r0 - hb, hb), rows + hb)
            emit(g_c[back, :], v_c[back, :], r0)
            return carry

        lax.fori_loop(1, t // rows, chunk, 0, unroll=3)

    cur = lambda off: pl.BlockSpec((t, w), lambda j, i: (i, off + j))
    prv = lambda off: pl.BlockSpec((hb, w), lambda j, i: (jnp.maximum(i * (t // hb) - 1, 0), off + j))
    wsp = lambda off: pl.BlockSpec((K_FFN, w), lambda j, i: (0, off + j))
    return _pcall(body, name=name, out_shape=SDS((s, f2 // 2), bf16), grid=(nj, s // t),
                  in_specs=[cur(0), cur(nj), prv(0), prv(nj), wsp(0), wsp(nj)],
                  out_specs=pl.BlockSpec((t, w), lambda j, i: (i, j)),
                  compiler_params=_cp(("parallel", "parallel")))(upre, upre, upre, upre, conv_f, conv_f)


def _ffn_act_bwd(upre, dact, conv_f, stacks, l, name):
    s, f2 = upre.shape
    t, hb, w = FFN_BWD_ROWS, HALO_FFN, 256
    nj = f2 // 2 // w

    def body(g_c, v_c, g_p, v_p, g_n, v_n, da_c, da_n, wg, wv, sg_ref, sv_ref, dg_ref, dv_ref, dwg_ref, dwv_ref):
        i = pl.program_id(1)
        first, last = i == 0, i == pl.num_programs(1) - 1

        @pl.when(first)
        def _():
            dwg_ref[...] = jnp.zeros_like(dwg_ref)
            dwv_ref[...] = jnp.zeros_like(dwv_ref)

        wgs = [wg[pl.ds(k, 1), :] for k in range(K_FFN)]
        wvs = [wv[pl.ds(k, 1), :] for k in range(K_FFN)]
        rows = FFN_CHUNK
        n = rows + hb
        first_tap = hb - (K_FFN - 1)
        nchunks = t // rows

        def groups_of_8(x):
            return sum(x[8 * q:8 * q + 8] for q in range(rows // 8))

        def emit(gb, vb, da, r0, carry):
            g_taps = [gb[first_tap + k:first_tap + k + n] for k in range(K_FFN)]
            v_taps = [vb[first_tap + k:first_tap + k + n] for k in range(K_FFN)]
            ug = sum(wgs[k] * g_taps[k] for k in range(K_FFN))
            uv = sum(wvs[k] * v_taps[k] for k in range(K_FFN))
            sg = _sigmoid(ug)
            dv = da * (ug * sg)
            dg = da * uv * (sg * (1.0 + ug * (1.0 - sg)))
            new = []
            for taps, dval, ws, dref in ((g_taps, dg, wgs, dg_ref), (v_taps, dv, wvs, dv_ref)):
                acc = sum(ws[k] * dval[K_FFN - 1 - k:K_FFN - 1 - k + rows] for k in range(K_FFN))
                dref[pl.ds(r0, rows), :] = acc.astype(bf16)
                new += [groups_of_8(dval[:rows] * taps[k][:rows]) for k in range(K_FFN)]
            return tuple(a + b for a, b in zip(carry, new))

        def chunk(c, carry):
            r0 = pl.multiple_of(c * rows, rows)
            around = pl.ds(pl.multiple_of(r0 - hb, hb), n + hb)
            return emit(g_c[around, :], v_c[around, :], da_c[pl.ds(r0, n), :], r0, carry)

        def edge(prev_ref, cur_ref, lo):
            return jnp.concatenate([jnp.where(first, 0.0, prev_ref[...]), cur_ref[pl.ds(0, lo), :]], axis=0)

        def tail(cur_ref, next_ref, lo):
            return jnp.concatenate([cur_ref[pl.ds(t - lo, lo), :], jnp.where(last, 0.0, next_ref[...])], axis=0)

        zero = jnp.zeros((8, w), f32)
        sums = emit(edge(g_p, g_c, n), edge(v_p, v_c, n), da_c[pl.ds(0, n), :], 0, (zero,) * (2 * K_FFN))
        sums = lax.fori_loop(1, nchunks - 1, chunk, sums, unroll=2)
        sums = emit(tail(g_c, g_n, rows + hb), tail(v_c, v_n, rows + hb), tail(da_c, da_n, rows), t - rows, sums)
        for k in range(K_FFN):
            dwg_ref[pl.ds(k, 1), :] += _colsum(sums[k])
            dwv_ref[pl.ds(k, 1), :] += _colsum(sums[K_FFN + k])

    cur = lambda off: pl.BlockSpec((t, w), lambda j, i: (i, off + j))
    prv = lambda off: pl.BlockSpec((hb, w), lambda j, i: (jnp.maximum(i * (t // hb) - 1, 0), off + j))
    nxt = lambda off: pl.BlockSpec((hb, w), lambda j, i: (jnp.minimum((i + 1) * (t // hb), s // hb - 1), off + j))
    wsp = lambda off: pl.BlockSpec((K_FFN, w), lambda j, i: (0, off + j))
    half = SDS((s, f2 // 2), bf16)
    dws = SDS(stacks[0].shape, f32)
    o_cur = pl.BlockSpec((t, w), lambda j, i: (i, j))
    o_dw = pl.BlockSpec((None, 8, w), lambda j, i: (l, 0, j))
    anywhere = pl.BlockSpec(memory_space=pl.ANY)
    return _pcall(body, name=name, out_shape=(half, half, dws, dws), grid=(nj, s // t),
                  in_specs=[cur(0), cur(nj), prv(0), prv(nj), nxt(0), nxt(nj), cur(0), nxt(0), wsp(0), wsp(nj), anywhere, anywhere],
                  out_specs=(o_cur, o_cur, o_dw, o_dw), input_output_aliases={10: 2, 11: 3},
                  compiler_params=_cp(("parallel", "arbitrary")))(upre, upre, upre, upre, upre, upre, dact, dact, conv_f, conv_f, *stacks)


def _bias_tables(rel_bias, name):
    nb = len(DILATIONS)

    def body(rb_ref, idx_ref, o_ref):
        h = pl.program_id(1)
        iv = idx_ref[0]
        qi = lax.broadcasted_iota(jnp.int32, (BLK, 2 * BLK), 0)
        kj = lax.broadcasted_iota(jnp.int32, (BLK, 2 * BLK), 1)
        band = jnp.logical_and(kj >= qi, kj <= qi + N_KEYS)
        acc = jnp.zeros((BLK, 2 * BLK), f32)
        for b in range(N_BUCKETS):
            acc = jnp.where(iv == b, rb_ref[b, h], acc)
        o_ref[0, 0, 0] = jnp.where(band, acc, NEG)
        o_ref[0, 1, 0] = jnp.where(jnp.logical_and(band, kj >= BLK), acc, NEG)

    return _pcall(body, name=name, out_shape=SDS((nb, 2, 8, BLK, 2 * BLK), f32), grid=(nb, 8),
                  in_specs=[pl.BlockSpec(memory_space=pltpu.SMEM), pl.BlockSpec((1, BLK, 2 * BLK), lambda br, h: (br, 0, 0))],
                  out_specs=pl.BlockSpec((1, 2, 1, BLK, 2 * BLK), lambda br, h: (br, 0, h, 0, 0)),
                  compiler_params=_cp(("parallel", "parallel")))(rel_bias, jnp.asarray(_BUCKETS))


def _bias_grad(ds_tabs, name):
    nl, nb = ds_tabs.shape[0], ds_tabs.shape[1]

    def body(t_ref, idx_ref, o_ref):
        br, h = pl.program_id(0), pl.program_id(1)

        @pl.when(jnp.logical_and(br == 0, h == 0))
        def _():
            o_ref[...] = jnp.zeros_like(o_ref)

        tab = t_ref[0, 0, 0]
        for l in range(1, nl):
            tab = tab + t_ref[l, 0, 0]
        iv = idx_ref[0]
        row = lax.broadcasted_iota(jnp.int32, (N_BUCKETS, 128), 0)
        col = lax.broadcasted_iota(jnp.int32, (N_BUCKETS, 128), 1)
        acc = jnp.zeros((N_BUCKETS, 128), f32)
        for b in range(N_BUCKETS):
            sel = jnp.where(iv == b, tab, 0.0)
            tot = jnp.sum(jnp.sum(sel, axis=0, keepdims=True), axis=1, keepdims=True)
            acc = acc + jnp.where(jnp.logical_and(row == b, col == h), tot, 0.0)
        o_ref[...] += acc

    return _pcall(body, name=name, out_shape=SDS((N_BUCKETS, 128), f32), grid=(nb, 8),
                  in_specs=[pl.BlockSpec((nl, 1, 1, BLK, 2 * BLK), lambda br, h: (0, br, h, 0, 0)),
                            pl.BlockSpec((1, BLK, 2 * BLK), lambda br, h: (br, 0, 0))],
                  out_specs=pl.BlockSpec((N_BUCKETS, 128), lambda br, h: (0, 0)),
                  compiler_params=_cp(("arbitrary", "arbitrary")))(ds_tabs, jnp.asarray(_BUCKETS))


def _stack_heads(x, head0):
    zero = jnp.zeros_like(x)
    return jnp.concatenate([jnp.where(head0, x, zero), jnp.where(head0, zero, x)], axis=0)


def _block_rows(blk, d, nblk):
    per_res = nblk // d
    res, n = blk // per_res, blk % per_res
    cur = pl.ds(res + d * BLK * n, BLK, stride=d) if d > 1 else pl.ds(pl.multiple_of(BLK * n, BLK), BLK)
    before = jnp.maximum(n - 1, 0)
    prev = pl.ds(res + d * BLK * before, BLK, stride=d) if d > 1 else pl.ds(pl.multiple_of(BLK * before, BLK), BLK)
    return cur, prev, jnp.where(n == 0, 1, 0)


def _attn_fwd(z, bias, name):
    s = z.shape[0]
    nblk = s // BLK

    def body(q_ref, k_ref, v_ref, b_ref, y_ref, l_ref):
        lane = lax.broadcasted_iota(jnp.int32, (BLK, BLK), 1)
        head0 = lane < D_HEAD

        for branch, d in enumerate(DILATIONS):
            def step(blk, carry, branch=branch, d=d):
                cur, prev, variant = _block_rows(blk, d, nblk)
                q = (q_ref[cur, :] * (D_HEAD ** -0.5)).astype(bf16)
                if d == nblk:
                    kk, vv = k_ref[cur, :].astype(bf16), v_ref[cur, :].astype(bf16)
                    bias_blk = b_ref[branch, 1, 0, :, BLK:]
                else:
                    kk = jnp.concatenate([k_ref[prev, :], k_ref[cur, :]], axis=0).astype(bf16)
                    vv = jnp.concatenate([v_ref[prev, :], v_ref[cur, :]], axis=0).astype(bf16)
                    bias_blk = b_ref[branch, variant, 0]
                sc = lax.dot_general(_stack_heads(q, head0), kk, (((1,), (1,)), ((), ())), preferred_element_type=f32)
                sc = sc + bias_blk
                m = jnp.max(sc, axis=-1, keepdims=True)
                p = jnp.exp(sc - m)
                den = jnp.sum(p, axis=-1, keepdims=True)
                out = jnp.dot(p.astype(bf16), vv, preferred_element_type=f32) / den
                lse = m + jnp.log(den)
                out = jnp.where(head0, out[:BLK], out[BLK:])
                lse = jnp.where(head0, lse[:BLK], lse[BLK:]) + jnp.zeros((BLK, BLK), f32)
                if branch == 0:
                    y_ref[cur, :] = out
                    l_ref[cur, :] = lse
                else:
                    y_old, l_old = y_ref[cur, :], l_ref[cur, :]
                    top = jnp.maximum(l_old, lse)
                    e_old, e_new = jnp.exp(l_old - top), jnp.exp(lse - top)
                    tot = e_old + e_new
                    y_ref[cur, :] = (e_old * y_old + e_new * out) / tot
                    l_ref[cur, :] = top + jnp.log(tot)
                return carry

            lax.fori_loop(0, nblk, step, 0, unroll=4)

    col = lambda off: pl.BlockSpec((s, BLK), lambda hp: (0, off + hp))
    out = SDS((s, 4 * BLK), f32)
    nbr = len(DILATIONS)
    return _pcall(body, name=name, out_shape=(out, out), grid=(4,),
                  in_specs=[col(Z_Q128), col(Z_K128), col(Z_V128), pl.BlockSpec((nbr, 2, 1, 2 * BLK, 2 * BLK), lambda hp: (0, 0, hp, 0, 0))],
                  out_specs=(col(0), col(0)), compiler_params=_cp(("parallel",)))(z, z, z, bias)


def _attn_bwd(z, dyb, yb, lt, bias, name):
    s = z.shape[0]
    nblk = s // BLK
    scale = D_HEAD ** -0.5
    chunk = 256

    def body(q_ref, k_ref, v_ref, dy_ref, y_ref, lt_ref, b_ref, dq_ref, dk_ref, dv_ref, ds_ref, aq, ak, av):
        lane = lax.broadcasted_iota(jnp.int32, (BLK, BLK), 1)
        head0 = lane < D_HEAD
        ak[...] = jnp.zeros_like(ak)
        av[...] = jnp.zeros_like(av)
        ds_ref[...] = jnp.zeros_like(ds_ref)

        for branch, d in enumerate(DILATIONS):
            def step(blk, carry, branch=branch, d=d):
                cur, prev, variant = _block_rows(blk, d, nblk)
                own_keys = d == nblk
                q = (q_ref[cur, :] * scale).astype(bf16)
                if own_keys:
                    kk, vv = k_ref[cur, :].astype(bf16), v_ref[cur, :].astype(bf16)
                    bias_blk = b_ref[branch, 1, 0, :, BLK:]
                else:
                    kk = jnp.concatenate([k_ref[prev, :], k_ref[cur, :]], axis=0).astype(bf16)
                    vv = jnp.concatenate([v_ref[prev, :], v_ref[cur, :]], axis=0).astype(bf16)
                    bias_blk = b_ref[branch, variant, 0]
                dy = dy_ref[cur, :]
                lt_b = lt_ref[cur, :]
                qs = _stack_heads(q, head0)
                dys = _stack_heads(dy.astype(bf16), head0)
                dd = jnp.sum(_stack_heads(dy * y_ref[cur, :], head0), axis=-1, keepdims=True)
                lts = jnp.concatenate([lt_b[:, 0:1], lt_b[:, D_HEAD:D_HEAD + 1]], axis=0)
                sc = lax.dot_general(qs, kk, (((1,), (1,)), ((), ())), preferred_element_type=f32)
                p = jnp.exp(sc + bias_blk - lts)
                dp = lax.dot_general(dys, vv, (((1,), (1,)), ((), ())), preferred_element_type=f32)
                ds = p * (dp - dd)
                if own_keys:
                    ds_ref[branch, 0, :, BLK:] += ds
                else:
                    ds_ref[branch, 0] += ds
                dsb = ds.astype(bf16)
                dq = jnp.dot(dsb, kk, preferred_element_type=f32) * scale
                dkk = lax.dot_general(dsb, qs, (((0,), (0,)), ((), ())), preferred_element_type=f32)
                dvv = lax.dot_general(p.astype(bf16), dys, (((0,), (0,)), ((), ())), preferred_element_type=f32)
                dq = jnp.where(head0, dq[:BLK], dq[BLK:])
                if branch == 0:
                    aq[cur, :] = dq
                else:
                    aq[cur, :] += dq
                if own_keys:
                    ak[cur, :] += dkk
                    av[cur, :] += dvv
                else:
                    ak[prev, :] += dkk[:BLK]
                    ak[cur, :] += dkk[BLK:]
                    av[prev, :] += dvv[:BLK]
                    av[cur, :] += dvv[BLK:]
                return carry

            lax.fori_loop(0, nblk, step, 0, unroll=4)

        def store(c, carry):
            rows = pl.ds(pl.multiple_of(c * chunk, chunk), chunk)
            dq_ref[rows, :] = aq[rows, :].astype(bf16)
            dk_ref[rows, :] = ak[rows, :].astype(bf16)
            dv_ref[rows, :] = av[rows, :].astype(bf16)
            return carry

        lax.fori_loop(0, s // chunk, store, 0)

    col = lambda off: pl.BlockSpec((s, BLK), lambda hp: (0, off + hp))
    nbr = len(DILATIONS)
    tabs = pl.BlockSpec((nbr, 1, 2 * BLK, 2 * BLK), lambda hp: (0, hp, 0, 0))
    bias_spec = pl.BlockSpec((nbr, 2, 1, 2 * BLK, 2 * BLK), lambda hp: (0, 0, hp, 0, 0))
    half = SDS((s, 4 * BLK), bf16)
    return _pcall(body, name=name, out_shape=(half, half, half, SDS((nbr, 4, 2 * BLK, 2 * BLK), f32)), grid=(4,),
                  in_specs=[col(Z_Q128), col(Z_K128), col(Z_V128), col(0), col(0), col(0), bias_spec],
                  out_specs=(col(0), col(0), col(0), tabs), scratch_shapes=[pltpu.VMEM((s, BLK), f32)] * 3,
                  compiler_params=_cp(("parallel",)))(z, z, z, dyb, yb, lt, bias)


def _adamw_math(w, g, m, v):
    m = ADAM_B1 * m + (1.0 - ADAM_B1) * g
    v = ADAM_B2 * v + (1.0 - ADAM_B2) * (g * g)
    m_hat = m / (1.0 - ADAM_B1 ** ADAM_STEP)
    v_hat = v / (1.0 - ADAM_B2 ** ADAM_STEP)
    delta = -ADAM_LR * (m_hat / (jnp.sqrt(v_hat) + ADAM_EPS) + ADAM_WD * w)
    return delta, m, v


def _adamw_layer(land, send, w, m, v, prev, l, tr, me, name):
    _, r, c = w.shape
    assert r % tr == 0

    def body(me_ref, land_ref, own_ref, w_ref, m_ref, v_ref, p0, p1, p2, p3, g_ref, d_ref, nm_ref, nv_ref):
        mine = jnp.zeros((tr, c), jnp.int32) + me_ref[0]
        own = own_ref[...]
        g = jnp.zeros((tr, c), f32)
        for j in range(N_DEV):
            g = g + jnp.where(mine == j, own, land_ref[j]).astype(f32)
        delta, nm, nv = _adamw_math(w_ref[...], g, m_ref[...], v_ref[...])
        g_ref[...] = g
        d_ref[...] = delta
        nm_ref[...] = nm
        nv_ref[...] = nv

    parts = pl.BlockSpec((N_DEV, tr, c), lambda i, me_ref: (0, i, 0))
    own = pl.BlockSpec((None, tr, c), lambda i, me_ref: (me_ref[0], i, 0))
    row = pl.BlockSpec((None, tr, c), lambda i, me_ref: (l, i, 0))
    anywhere = pl.BlockSpec(memory_space=pl.ANY)
    out = SDS(w.shape, f32)
    grid_spec = pltpu.PrefetchScalarGridSpec(num_scalar_prefetch=1, grid=(r // tr,), in_specs=[parts, own, row, row, row] + [anywhere] * 4,
                                             out_specs=(row, row, row, row))
    return _pcall(body, name=name, out_shape=(out, out, out, out), grid_spec=grid_spec,
                  input_output_aliases={6: 0, 7: 1, 8: 2, 9: 3}, compiler_params=_cp(("parallel",)))(me, land, send, w, m, v, *prev)


def _sum_parts(parts, name):
    _, r, c = parts.shape

    def body(p_ref, o_ref):
        g = p_ref[0]
        for j in range(1, N_DEV):
            g = g + p_ref[j]
        o_ref[...] = g

    return _pcall(body, name=name, out_shape=SDS((r, c), f32), compiler_params=_cp())(parts)


def _adamw_small(gs, ws, ms, vs, name):
    n = len(ws)

    def body(*refs):
        for t in range(n):
            delta, nm, nv = _adamw_math(refs[n + t][...], refs[t][...], refs[2 * n + t][...], refs[3 * n + t][...])
            refs[4 * n + t][...] = delta
            refs[5 * n + t][...] = nm
            refs[6 * n + t][...] = nv

    out = [SDS(wt.shape, f32) for wt in ws] * 3
    res = _pcall(body, name=name, out_shape=out, compiler_params=_cp())(*gs, *ws, *ms, *vs)
    return list(res[:n]), list(res[n:2 * n]), list(res[2 * n:])


def _my_place():
    return lax.axis_index("x"), lax.axis_index("y"), lax.axis_index("c")


def _all_gather(xs, name, after=None):
    n = len(xs)

    def body(*refs):
        x_refs, out_refs = refs[:n], refs[-3 - n:-3]
        send_sems, recv_sems, local_sems = refs[-3:]
        mx, my, mc = _my_place()
        me, sibling = (mx, my, mc), (mx, my, 1 - mc)
        chips = [(1 - mx, my), (mx, 1 - my), (1 - mx, 1 - my)]

        def rows(t, px, py, pc):
            return out_refs[t].at[4 * px + 2 * py + pc]

        def copy(t, k, block, to, src=None):
            return pltpu.make_async_remote_copy(src_ref=rows(t, *block) if src is None else src, dst_ref=rows(t, *block),
                                                send_sem=send_sems.at[7 * t + k], recv_sem=recv_sems.at[7 * t + k],
                                                device_id=to, device_id_type=MESH)

        mine = [pltpu.make_async_copy(x_refs[t], rows(t, *me), local_sems.at[t]) for t in range(n)]
        first, passed = [], []
        for t in range(n):
            mine[t].start()
            first += [copy(t, 0, me, sibling, src=x_refs[t])] + [copy(t, 1 + j, me, (*chip, mc), src=x_refs[t]) for j, chip in enumerate(chips)]
        for cp in first:
            cp.start()
        for t in range(n):
            for j, chip in enumerate(chips):
                copy(t, 1 + j, (*chip, mc), me).wait_recv()
                passed.append(copy(t, 4 + j, (*chip, mc), sibling))
                passed[-1].start()
        for t in range(n):
            copy(t, 0, sibling, me).wait_recv()
            for j, chip in enumerate(chips):
                copy(t, 4 + j, (*chip, 1 - mc), me).wait_recv()
        for cp in first + passed:
            cp.wait_send()
        for t in range(n):
            mine[t].wait()

    ins = list(xs) + ([after] if after is not None else [])
    anywhere = pl.BlockSpec(memory_space=pl.ANY)
    return list(_pcall(body, name=name, out_shape=[SDS((N_DEV,) + x.shape, x.dtype) for x in xs],
                       in_specs=[anywhere] * len(ins), out_specs=[anywhere] * n,
                       scratch_shapes=[pltpu.SemaphoreType.DMA((7 * n,)), pltpu.SemaphoreType.DMA((7 * n,)),
                                       pltpu.SemaphoreType.DMA((n,))])(*ins))


HBM_SPEC = pl.BlockSpec(memory_space=pltpu.HBM)
SEM_SPEC = pl.BlockSpec(memory_space=pltpu.SEMAPHORE)
ANY_SPEC = pl.BlockSpec(memory_space=pl.ANY)
DATAFLOW = pltpu.SideEffectType.DATAFLOW_SIDE_EFFECTING


def _peers():
    mx, my, mc = _my_place()
    out = []
    for k in range(1, N_DEV):
        tx, ty, tc = mx ^ ((k >> 2) & 1), my ^ ((k >> 1) & 1), mc ^ (k & 1)
        out.append(((tx, ty, tc), 4 * tx + 2 * ty + tc))
    return out


def _in_hbm(arrays):
    return [pltpu.with_memory_space_constraint(a, pltpu.HBM) for a in arrays]


N_FIRST_LEG, N_SECOND_LEG = 4, 3


def _chip_peers():
    mx, my, mc = _my_place()
    return [((tx, ty, mc), 4 * tx + 2 * ty + mc) for tx, ty in ((1 - mx, my), (mx, 1 - my), (1 - mx, 1 - my))]


def _gather_start(bufs, after, name):
    n = len(bufs)

    def body(*refs):
        ins, s_sem, r_sem, token = refs[:n], refs[n + 1], refs[n + 2], refs[-1]
        mx, my, mc = _my_place()
        me = 4 * mx + 2 * my + mc
        for t in range(n):
            for dev in [(mx, my, 1 - mc)] + [d for d, _ in _chip_peers()]:
                pltpu.make_async_remote_copy(src_ref=ins[t].at[me], dst_ref=ins[t].at[me], send_sem=s_sem.at[t],
                                             recv_sem=r_sem.at[t], device_id=dev, device_id_type=MESH).start()
        token[...] = jnp.zeros_like(token)

    out_shape = ((pltpu.SemaphoreType.DMA((n,)), pltpu.SemaphoreType.DMA((n,))) + tuple(pltpu.HBM(b.shape, b.dtype) for b in bufs)
                 + (SDS((8, 128), f32),))
    out = _pcall(body, name=name, out_shape=out_shape, in_specs=(HBM_SPEC,) * n + (ANY_SPEC,),
                 out_specs=(SEM_SPEC, SEM_SPEC) + (HBM_SPEC,) * n + (pl.BlockSpec(memory_space=pltpu.VMEM),),
                 input_output_aliases={i: 2 + i for i in range(n)},
                 compiler_params=pltpu.CompilerParams(has_side_effects=DATAFLOW))(*_in_hbm(bufs), after)
    return (out[0], out[1]), list(out[2:2 + n]), out[-1]


def _gather_forward(bufs, sems, after, name, fresh=()):
    n, nf = len(bufs), len(fresh)
    both = list(bufs) + list(fresh)

    nsem = 4 if nf else 2

    def body(*refs):
        ins, news, s1, r1 = refs[:n], refs[n:n + nf], refs[n + nf], refs[n + nf + 1]
        s2, r2 = refs[n + nf + 3:n + nf + 5]
        sf, rf = refs[n + nf + 5:n + nf + 7] if nf else (None, None)
        token = refs[-1]
        mx, my, mc = _my_place()
        me, sibling = 4 * mx + 2 * my + mc, (mx, my, 1 - mc)
        for t in range(n):
            legs = ins[t].at[pl.ds(0, N_FIRST_LEG)]
            first = pltpu.make_async_remote_copy(src_ref=legs, dst_ref=legs, send_sem=s1.at[t], recv_sem=r1.at[t],
                                                 device_id=sibling, device_id_type=MESH)
            first.wait_send()
            first.wait_recv()
        for t in range(n):
            for _, idx in _chip_peers():
                pltpu.make_async_remote_copy(src_ref=ins[t].at[idx], dst_ref=ins[t].at[idx], send_sem=s2.at[t],
                                             recv_sem=r2.at[t], device_id=sibling, device_id_type=MESH).start()
        for t in range(nf):
            for dev in [sibling] + [d for d, _ in _chip_peers()]:
                pltpu.make_async_remote_copy(src_ref=news[t].at[me], dst_ref=news[t].at[me], send_sem=sf.at[t],
                                             recv_sem=rf.at[t], device_id=dev, device_id_type=MESH).start()
        token[...] = jnp.zeros_like(token)

    sem = lambda k: pltpu.SemaphoreType.DMA((k,))
    sem_shapes = (sem(n), sem(n)) + ((sem(nf), sem(nf)) if nf else ())
    out_shape = sem_shapes + tuple(pltpu.HBM(b.shape, b.dtype) for b in both) + (SDS((8, 128), f32),)
    out = _pcall(body, name=name, out_shape=out_shape, in_specs=(HBM_SPEC,) * (n + nf) + (SEM_SPEC, SEM_SPEC, ANY_SPEC),
                 out_specs=(SEM_SPEC,) * nsem + (HBM_SPEC,) * (n + nf) + (pl.BlockSpec(memory_space=pltpu.VMEM),),
                 input_output_aliases={i: nsem + i for i in range(n + nf)},
                 compiler_params=pltpu.CompilerParams(has_side_effects=DATAFLOW))(*bufs, *_in_hbm(fresh), sems[0], sems[1], after)
    fresh_sems = (out[2], out[3]) if nf else None
    return (out[0], out[1]), list(out[nsem:nsem + n]), fresh_sems, list(out[nsem + n:nsem + n + nf]), out[-1]


def _exchange_start(sends, lands, name):
    nt = len(sends)

    def body(*refs):
        srcs, dsts, s_sem, r_sem, token = refs[:nt], refs[nt:2 * nt], refs[2 * nt], refs[2 * nt + 1], refs[-1]
        mx, my, mc = _my_place()
        me = 4 * mx + 2 * my + mc
        for t in range(nt):
            for dev, idx in _peers():
                pltpu.make_async_remote_copy(src_ref=srcs[t].at[idx], dst_ref=dsts[t].at[me], send_sem=s_sem.at[t],
                                             recv_sem=r_sem.at[t], device_id=dev, device_id_type=MESH).start()
        token[...] = jnp.zeros_like(token)

    both = list(sends) + list(lands)
    out_shape = (pltpu.SemaphoreType.DMA((nt,)), pltpu.SemaphoreType.DMA((nt,))) + tuple(pltpu.HBM(b.shape, b.dtype) for b in both)
    out = _pcall(body, name=name, out_shape=out_shape + (SDS((8, 128), f32),), in_specs=(HBM_SPEC,) * (2 * nt),
                 out_specs=(SEM_SPEC, SEM_SPEC) + (HBM_SPEC,) * (2 * nt) + (pl.BlockSpec(memory_space=pltpu.VMEM),),
                 input_output_aliases={i: 2 + i for i in range(2 * nt)},
                 compiler_params=pltpu.CompilerParams(has_side_effects=DATAFLOW))(*_in_hbm(both))
    return (out[0], out[1]), list(out[2:2 + nt]), list(out[2 + nt:2 + 2 * nt]), out[-1]


def _copies_wait(srcs, dsts, sems, after, name, ncopies=N_DEV - 1):
    nt = len(srcs)
    same = srcs is dsts
    bufs = list(srcs) if same else list(srcs) + list(dsts)
    nb = len(bufs)

    def body(*refs):
        s_sem, r_sem = refs[nb], refs[nb + 1]
        mx, my, mc = _my_place()
        for t in range(nt):
            src = refs[t].at[pl.ds(0, ncopies)]
            dst = src if same else refs[nt + t].at[pl.ds(0, ncopies)]
            cp = pltpu.make_async_remote_copy(src_ref=src, dst_ref=dst, send_sem=s_sem.at[t], recv_sem=r_sem.at[t],
                                              device_id=(mx, my, 1 - mc), device_id_type=MESH)
            cp.wait_send()
            cp.wait_recv()

    out = _pcall(body, name=name, out_shape=tuple(pltpu.HBM(b.shape, b.dtype) for b in bufs),
                 in_specs=(HBM_SPEC,) * nb + (SEM_SPEC, SEM_SPEC, ANY_SPEC), out_specs=(HBM_SPEC,) * nb,
                 input_output_aliases={i: i for i in range(nb)},
                 compiler_params=pltpu.CompilerParams(has_side_effects=DATAFLOW))(*bufs, sems[0], sems[1], after)
    out = list(out)
    return (out, out) if same else (out[:nt], out[nt:])


def _cast_blocks(weights, l, me, after, name):
    nt = len(weights)
    halves = 2
    extra = [] if after is None else [after]

    def body(me_ref, *refs):
        outs = refs[nt + len(extra):]
        for t in range(nt):
            outs[t][...] = refs[t][...].astype(bf16)

    in_specs, out_specs, out_shape = [], [], []
    for wt in weights:
        _, r, c = wt.shape
        tr = r // halves
        in_specs.append(pl.BlockSpec((None, tr, c), lambda i, me_ref: (l, i, 0)))
        out_specs.append(pl.BlockSpec((None, tr, c), lambda i, me_ref: (me_ref[0], i, 0)))
        out_shape.append(SDS((N_DEV, r, c), bf16))
    in_specs += [pl.BlockSpec(memory_space=pl.ANY)] * len(extra)
    grid_spec = pltpu.PrefetchScalarGridSpec(num_scalar_prefetch=1, grid=(halves,), in_specs=in_specs, out_specs=out_specs)
    return list(_pcall(body, name=name, out_shape=out_shape, grid_spec=grid_spec, compiler_params=_cp(("parallel",)))(me, *weights, *extra))

def _layer_fwd_mix(x, p, bias, l, after=None):
    tag = f"l{l}"
    z = _mm(x, p["w_in"], "nt", 1024, 1408, f"{tag}_mm_in", after=after, norm_a=p["norm_mix_g"])
    yb, lt = _attn_fwd(z, bias, f"{tag}_attn_fwd")
    y = _mixer_fwd(z, yb, p["conv_a_w"], p["conv_c_w"], p["conv_c_b"], p["ln_c_g"], p["ln_c_b"], p["out_norm_g"], f"{tag}_mixer_fwd")
    return y, dict(x=x, z=z, yb=yb, lt=lt, y=y)


def _layer_fwd_ffn(x, y, p, l, after=None):
    tag = f"l{l}"
    x1 = _mm(y, p["w_out"], "nn", 512, 1024, f"{tag}_mm_out", res=x, after=after)
    upre = _mm(x1, p["w_up"], "nt", 1024, 1408, f"{tag}_mm_up", norm_a=p["norm_ffn_g"])
    act = _ffn_act_fwd(upre, p["conv_f_w"], f"{tag}_ffn_act")
    x2 = _mm(act, p["w_down"], "nn", 512, 1024, f"{tag}_mm_down", res=x1)
    return x2, dict(x1=x1, upre=upre, act=act)


def _small_stacks(depth, d_model, d_ff):
    shapes = dict(norm_ffn=(depth, 1, d_model), norm_mix=(depth, 1, d_model), conv_f_gate=(depth, 8, d_ff),
                  conv_f_val=(depth, 8, d_ff), mixers=(depth, PG_ROWS, 256), og_b=(depth, 1, 512))
    return {k: lax.empty(shp, f32) for k, shp in shapes.items()}


def _small_grads(st):
    pg = st["mixers"]
    row = lambda r: pg[:, r:r + 1, :]
    return dict(norm_ffn_g=st["norm_ffn"], norm_mix_g=st["norm_mix"],
                conv_f_w=jnp.concatenate([st["conv_f_gate"][:, :K_FFN], st["conv_f_val"][:, :K_FFN]], axis=2),
                conv_a_w=pg[:, PG_CONV_A:PG_CONV_A + K_SHORT], conv_c_w=pg[:, PG_CONV_C:PG_CONV_C + K_CONF],
                conv_c_b=row(PG_CB), ln_c_g=row(PG_LNG), ln_c_b=row(PG_LNB),
                out_norm_g=jnp.concatenate([row(PG_OG_A), st["og_b"], row(PG_OG_C)], axis=2))


def _layer_bwd_ffn(dx2, p, sv, l, st, after=None):
    tag = f"l{l}"
    g, st = {}, dict(st)
    dact = _mm(dx2, p["w_down"], "nt", 512, 1408, f"{tag}_mm_dact", after=after, b_outer=True)
    dwd = _mm(sv["act"], dx2, "tn", 1408, 1024, f"{tag}_mm_dwdown", out_dtype=bf16)
    g["w_down"] = dwd.reshape(N_DEV, dwd.shape[0] // N_DEV, dwd.shape[1])
    dug, duv, st["conv_f_gate"], st["conv_f_val"] = _ffn_act_bwd(sv["upre"], dact, p["conv_f_w"], (st["conv_f_gate"], st["conv_f_val"]),
                                                                 l, f"{tag}_ffn_act_bwd")
    dupre = jnp.concatenate([dug, duv], axis=1)
    dwu = _mm(dupre, sv["x1"], "tn", 1408, 1024, f"{tag}_mm_dwup", out_dtype=bf16, norm_b=p["norm_ffn_g"])
    g["w_up"] = dwu.reshape(N_DEV, dwu.shape[0] // N_DEV, dwu.shape[1])
    dx1, st["norm_ffn"] = _mm(dupre, p["w_up"], "nn", 512, 1024, f"{tag}_mm_dh2",
                              rms_bwd=(sv["x1"], p["norm_ffn_g"], dx2, st["norm_ffn"], l))
    return dx1, g, st


def _layer_bwd_mix(dx1, p, sv, bias, l, st, after=None):
    tag = f"l{l}"
    g, st = {}, dict(st)
    dy = _mm(dx1, p["w_out"], "nt", 512, 1024, f"{tag}_mm_dy", after=after)
    dwo = _mm(sv["y"], dx1, "tn", 512, 1024, f"{tag}_mm_dwout", out_dtype=bf16)
    g["w_out"] = dwo.reshape(N_DEV, dwo.shape[0] // N_DEV, dwo.shape[1])
    dza, dzc, dyb, st["mixers"], st["og_b"] = _mixer_bwd(sv["z"], sv["yb"], dy, p["conv_a_w"], p["conv_c_w"], p["conv_c_b"], p["ln_c_g"],
                                                         p["ln_c_b"], p["out_norm_g"], (st["mixers"], st["og_b"]), l, f"{tag}_mixer_bwd")
    dq, dk, dv, tabs = _attn_bwd(sv["z"], dyb, sv["yb"], sv["lt"], bias, f"{tag}_attn_bwd")
    dz = jnp.concatenate([dza, dq, dk, dv, dzc], axis=1)
    dwi = _mm(dz, sv["x"], "tn", 1408, 1024, f"{tag}_mm_dwin", out_dtype=bf16, norm_b=p["norm_mix_g"])
    g["w_in"] = dwi.reshape(N_DEV, dwi.shape[0] // N_DEV, dwi.shape[1])
    dx, st["norm_mix"] = _mm(dz, p["w_in"], "nn", 512, 1024, f"{tag}_mm_dh", rms_bwd=(sv["x"], p["norm_mix_g"], dx1, st["norm_mix"], l))
    return dx, g, tabs, st


SMALL_LAYER_PARAMS = ("norm_mix_g", "conv_a_w", "conv_c_w", "conv_c_b", "ln_c_g", "ln_c_b", "out_norm_g", "norm_ffn_g", "conv_f_w")


BIG = ("w_in", "w_out", "w_up", "w_down")
TRANSPOSED = ("w_in", "w_up")
ADAM_ROWS = {"w_in": 176, "w_out": 128, "w_up": 176, "w_down": 176}
SMALL_SHARDED = ("conv_a_w", "conv_c_w", "conv_f_w")
SMALL_REPLICATED = ("norm_mix_g", "conv_c_b", "ln_c_g", "ln_c_b", "out_norm_g", "norm_ffn_g", "rel_bias", "final_g")
WEIGHTS = ("norm_mix_g", "w_in", "conv_a_w", "conv_c_w", "conv_c_b", "ln_c_g", "ln_c_b", "out_norm_g", "w_out",
           "norm_ffn_g", "w_up", "conv_f_w", "w_down", "rel_bias", "final_g")


def _pack(arrays):
    flat = jnp.concatenate([a.reshape(-1) for a in arrays])
    rows = -(-flat.shape[0] // 1024) * 8
    return jnp.pad(flat, (0, rows * 128 - flat.shape[0])).reshape(rows, 128)


def _unpack(packed, shapes):
    flat, out, at = packed.reshape(-1), [], 0
    for shp in shapes:
        n = int(np.prod(shp))
        out.append(flat[at:at + n].reshape(shp))
        at += n
    return out


def _gather_shards(g, local_shape):
    nd = len(local_shape)
    return jnp.moveaxis(g, 0, nd - 1).reshape(local_shape[:-1] + (N_DEV * local_shape[-1],))


def kernel(x, norm_mix_g, w_in, conv_a_w, conv_c_w, conv_c_b, ln_c_g, ln_c_b, out_norm_g, w_out, norm_ffn_g, w_up, conv_f_w, w_down, rel_bias, final_g, loss_target, m_norm_mix_g, m_w_in, m_conv_a_w, m_conv_c_w, m_conv_c_b, m_ln_c_g, m_ln_c_b, m_out_norm_g, m_w_out, m_norm_ffn_g, m_w_up, m_conv_f_w, m_w_down, m_rel_bias, m_final_g, v_norm_mix_g, v_w_in, v_conv_a_w, v_conv_c_w, v_conv_c_b, v_ln_c_g, v_ln_c_b, v_out_norm_g, v_w_out, v_norm_ffn_g, v_w_up, v_conv_f_w, v_w_down, v_rel_bias, v_final_g):
    w = dict(norm_mix_g=norm_mix_g, w_in=w_in, conv_a_w=conv_a_w, conv_c_w=conv_c_w, conv_c_b=conv_c_b, ln_c_g=ln_c_g,
             ln_c_b=ln_c_b, out_norm_g=out_norm_g, w_out=w_out, norm_ffn_g=norm_ffn_g, w_up=w_up, conv_f_w=conv_f_w,
             w_down=w_down, rel_bias=rel_bias, final_g=final_g)
    mom = dict(norm_mix_g=m_norm_mix_g, w_in=m_w_in, conv_a_w=m_conv_a_w, conv_c_w=m_conv_c_w, conv_c_b=m_conv_c_b,
               ln_c_g=m_ln_c_g, ln_c_b=m_ln_c_b, out_norm_g=m_out_norm_g, w_out=m_w_out, norm_ffn_g=m_norm_ffn_g,
               w_up=m_w_up, conv_f_w=m_conv_f_w, w_down=m_w_down, rel_bias=m_rel_bias, final_g=m_final_g)
    var = dict(norm_mix_g=v_norm_mix_g, w_in=v_w_in, conv_a_w=v_conv_a_w, conv_c_w=v_conv_c_w, conv_c_b=v_conv_c_b,
               ln_c_g=v_ln_c_g, ln_c_b=v_ln_c_b, out_norm_g=v_out_norm_g, w_out=v_w_out, norm_ffn_g=v_norm_ffn_g,
               w_up=v_w_up, conv_f_w=v_conv_f_w, w_down=v_w_down, rel_bias=v_rel_bias, final_g=v_final_g)
    depth = w_in.shape[0]
    me = 4 * lax.axis_index("x") + 2 * lax.axis_index("y") + lax.axis_index("c")
    for k in TRANSPOSED:
        w[k], mom[k], var[k] = (jnp.transpose(a, (0, 2, 1)) for a in (w[k], mom[k], var[k]))

    me_arr = me.astype(jnp.int32).reshape(1)
    bufs = [_cast_blocks([w[k] for k in BIG], 0, me_arr, None, "l0_cast_blocks")]
    in_sems, in_bufs, token = _gather_start(bufs[0][:1], me_arr, "l0_gather_start_in")
    bufs += [_cast_blocks([w[k] for k in BIG], l, me_arr, token, f"l{l}_cast_blocks") for l in range(1, depth)]
    params = {}
    gathered = _all_gather([w[k] for k in SMALL_SHARDED], "gather_small", after=token)
    gs = gathered[0]
    for k, g8 in zip(SMALL_SHARDED, gathered):
        params[k] = _gather_shards(g8, w[k].shape)
    for k in ("norm_mix_g", "conv_c_b", "ln_c_g", "ln_c_b", "out_norm_g", "norm_ffn_g"):
        params[k] = w[k][:, None, :]
    params["rel_bias"] = rel_bias
    params["final_g"] = final_g[None, :]

    def full(k, buf, l):
        return buf.reshape(N_DEV * buf.shape[1], buf.shape[2])

    bias = _bias_tables(rel_bias, "bias_tables").reshape(len(DILATIONS), 2, 4, 2 * BLK, 2 * BLK)
    in_sems, in_bufs, rest_sems, rest_bufs, token = _gather_forward(in_bufs, in_sems, gs, "l0_gather_forward_in", fresh=bufs[0][1:])
    xl, after = x[0], token
    layer_params, saved = [], []
    for l in range(depth):
        _, in_bufs = _copies_wait(in_bufs, in_bufs, in_sems, after, f"l{l}_gather_wait_in", N_SECOND_LEG)
        p = {k: params[k][l] for k in SMALL_LAYER_PARAMS}
        p["w_in"] = full("w_in", in_bufs[0], l)
        token = in_bufs[0]
        nxt_in = bufs[l + 1][:1] if l + 1 < depth else []
        if l > 0:
            rest_sems, rest_bufs, in_sems, in_bufs, token = _gather_forward(rest_bufs, rest_sems, token, f"l{l}_gather_forward_rest", fresh=nxt_in)
        else:
            in_sems, in_bufs, token = _gather_start(nxt_in, token, "l1_gather_start_in")
        y, sv = _layer_fwd_mix(xl, p, bias, l, after=token)
        if l == 0:
            rest_sems, rest_bufs, _, _, _ = _gather_forward(rest_bufs, rest_sems, y, "l0_gather_forward_rest")
        _, rest_bufs = _copies_wait(rest_bufs, rest_bufs, rest_sems, y, f"l{l}_gather_wait_rest", N_SECOND_LEG)
        for k, buf in zip(BIG[1:], rest_bufs):
            p[k] = full(k, buf, l)
        token = rest_bufs[0]
        if l + 1 < depth:
            in_sems, in_bufs, rest_sems, rest_bufs, token = _gather_forward(in_bufs, in_sems, token, f"l{l + 1}_gather_forward_in",
                                                                             fresh=bufs[l + 1][1:])
        xl, sv2 = _layer_fwd_ffn(xl, y, p, l, after=token)
        layer_params.append(p)
        saved.append({**sv, **sv2})
        after = xl
    loss, dx, d_final = _final_loss(xl, params["final_g"], loss_target[0], "final_loss")
    loss = lax.psum(loss[0, 0], ("x", "y", "c"))

    tabs, pending = [None] * depth, []
    after = None
    stacks = _small_stacks(depth, x.shape[-1], w_down.shape[1] * N_DEV)

    def exchange(l, part, names, g):
        sends = [g[k] for k in names]
        sems, sends, lands, token = _exchange_start(sends, [lax.empty(s.shape, s.dtype) for s in sends], f"l{l}_exchange_start_{part}")
        pending.append((l, part, names, sems, sends, lands))
        return token

    for l in reversed(range(depth)):
        dx1, g_ffn, stacks = _layer_bwd_ffn(dx, layer_params[l], saved[l], l, stacks, after=after)
        after = exchange(l, "ffn", ("w_up", "w_down"), g_ffn)
        dx, g_mix, tabs[l], stacks = _layer_bwd_mix(dx1, layer_params[l], saved[l], bias, l, stacks, after=after)
        after = exchange(l, "mix", ("w_out", "w_in"), g_mix)
    grads_small = _small_grads(stacks)
    grads_small["rel_bias"] = _bias_grad(jnp.stack(tabs).reshape(depth, len(DILATIONS), 8, BLK, 2 * BLK), "bias_grad")[:, :8]
    grads_small["final_g"] = d_final

    out_g, out_d, out_m, out_v = {}, {}, {}, {}
    grads = grads_small
    small = SMALL_SHARDED + SMALL_REPLICATED
    full_shapes = [params[k].shape if k in SMALL_SHARDED else w[k].shape for k in small]
    packed = _pack([grads[k].reshape(shp) for k, shp in zip(small, full_shapes)])
    gsum = _sum_parts(_all_gather([packed], "gather_small_grads", after=after)[0], "sum_small_grads")
    gfull = dict(zip(small, _unpack(gsum, full_shapes)))
    for k in SMALL_SHARDED:
        shp = w[k].shape
        gfull[k] = lax.dynamic_index_in_dim(gfull[k].reshape(shp[:-1] + (N_DEV, shp[-1])), me, axis=len(shp) - 1, keepdims=False)
    two_d = lambda a: a.reshape(1, -1) if a.ndim == 1 else a
    res = _adamw_small(*[[two_d(src[k]) for k in small] for src in (gfull, w, mom, var)], "adamw_small")
    for name_map, outs in zip((out_d, out_m, out_v), res):
        name_map.update({k: o.reshape(w[k].shape) for k, o in zip(small, outs)})
    out_g.update(gfull)

    results = {k: [lax.empty(w[k].shape, f32) for _ in range(4)] for k in BIG}
    after = res[0][0]
    for l, part, names, sems, sends, lands in pending:
        sends, lands = _copies_wait(sends, lands, sems, after, f"l{l}_exchange_wait_{part}")
        for k, send, land in zip(names, sends, lands):
            results[k] = _adamw_layer(land, send, w[k], mom[k], var[k], results[k], l, ADAM_ROWS[k], me_arr, f"l{l}_adamw_{k}")
            after = results[k][0]
    for k in BIG:
        big = [jnp.transpose(a, (0, 2, 1)) for a in results[k]] if k in TRANSPOSED else results[k]
        out_g[k], out_d[k], out_m[k], out_v[k] = big

    return (loss, dx[None], *[out_g[k] for k in WEIGHTS], *[out_d[k] for k in WEIGHTS], *[out_m[k] for k in WEIGHTS],
            *[out_v[k] for k in WEIGHTS])
```

```python
import math

import numpy as np
import jax
import jax.numpy as jnp
from jax import lax
from jax.experimental import pallas as pl
from jax.experimental.pallas import tpu as pltpu

f32, bf16 = jnp.float32, jnp.bfloat16
SDS = jax.ShapeDtypeStruct
MESH = pl.DeviceIdType.MESH

N_DEV = 8
EPS = 1e-6
NEG = -1e30
D_HEAD = 64
BLK = 128
N_BUCKETS = 32
DILATIONS = (1, 4, 16)
N_KEYS = 128
K_SHORT, K_CONF, K_FFN = 3, 31, 3
ROW_TILE = 256
HALO = 32
HALO_FFN = 8
FFN_FWD_ROWS, FFN_BWD_ROWS = 1024, 1024
FFN_CHUNK = 64
VMEM_LIMIT = 56 << 20
ADAM_LR, ADAM_B1, ADAM_B2, ADAM_EPS, ADAM_WD, ADAM_STEP = 0.001, 0.9, 0.999, 1e-08, 0.01, 10


def _pcall(body, **kw):
    return pl.pallas_call(body, **kw)


def _cp(sem=None):
    kw = dict(vmem_limit_bytes=VMEM_LIMIT)
    if sem is not None:
        kw["dimension_semantics"] = sem
    return pltpu.CompilerParams(**kw)


def _t5_buckets():
    rel = np.arange(BLK)[:, None] - np.arange(2 * BLK)[None, :] + BLK
    out = []
    for d in DILATIONS:
        dist = np.maximum(rel, 0) * d
        max_exact = N_BUCKETS // 2
        d_f = np.maximum(dist, 1).astype(np.float32)
        large = max_exact + (np.log(d_f / np.float32(max_exact)) / np.float32(math.log(2048 / max_exact))
                             * np.float32(N_BUCKETS - max_exact)).astype(np.int32)
        large = np.minimum(large, N_BUCKETS - 1)
        out.append(np.where(dist < max_exact, dist, large).astype(np.int32))
    return np.stack(out)


_BUCKETS = _t5_buckets()


def _mm(a, b, mode, tr, tc, name, res=None, out_dtype=f32, after=None, b_outer=False, norm_a=None, norm_b=None, rms_bwd=None):
    if rms_bwd is not None:
        return _mm_rms_bwd(a, b, tr, name, *rms_bwd)
    ij = (lambda f: (lambda j, i: f(i, j))) if b_outer else (lambda f: f)
    if mode == "nn":
        (m, k), n = a.shape, b.shape[1]
        a_spec = pl.BlockSpec((tr, k), ij(lambda i, j: (i, 0)))
        b_spec = pl.BlockSpec((k, tc), ij(lambda i, j: (0, j)))
        dims = (((1,), (0,)), ((), ()))
    elif mode == "nt":
        (m, k), n = a.shape, b.shape[0]
        a_spec = pl.BlockSpec((tr, k), ij(lambda i, j: (i, 0)))
        b_spec = pl.BlockSpec((tc, k), ij(lambda i, j: (j, 0)))
        dims = (((1,), (1,)), ((), ()))
    else:
        (k, m), n = a.shape, b.shape[1]
        a_spec = pl.BlockSpec((k, tr), ij(lambda i, j: (0, i)))
        b_spec = pl.BlockSpec((k, tc), ij(lambda i, j: (0, j)))
        dims = (((0,), (0,)), ((), ()))
    assert m % tr == 0 and n % tc == 0, (name, m, n, tr, tc)
    o_spec = pl.BlockSpec((tr, tc), ij(lambda i, j: (i, j)))
    grid = (n // tc, m // tr) if b_outer else (m // tr, n // tc)

    gain = norm_a if norm_a is not None else norm_b
    assert norm_b is None or (mode == "tn" and tc == n and not b_outer)
    assert norm_a is None or (mode != "tn" and not b_outer)

    def body(a_ref, b_ref, *rest):
        o_ref = rest[-2] if gain is not None else rest[-1]
        if norm_a is not None:
            ha_sc = rest[-1]

            @pl.when(pl.program_id(1) == 0)
            def _():
                av = a_ref[...]
                ha_sc[...] = (av * _rstd(av) * rest[0][...]).astype(bf16)

            av = ha_sc[...]
        else:
            av = a_ref[...].astype(bf16)
        if norm_b is not None:
            hb_sc = rest[-1]

            @pl.when(pl.program_id(0) == 0)
            def _():
                bv = b_ref[...]
                hb_sc[...] = (bv * _rstd(bv) * rest[0][...]).astype(bf16)

            bv = hb_sc[...]
        else:
            bv = b_ref[...].astype(bf16)
        acc = lax.dot_general(av, bv, dims, preferred_element_type=f32)
        if res is not None:
            acc = acc + rest[1 if gain is not None else 0][...]
        o_ref[...] = acc.astype(out_dtype)

    vec = pl.BlockSpec((1, k if norm_a is not None else n), lambda i, j: (0, 0))
    ins = [a, b] + ([gain] if gain is not None else []) + ([res] if res is not None else []) + ([after] if after is not None else [])
    specs = ([a_spec, b_spec] + ([vec] if gain is not None else []) + ([o_spec] if res is not None else [])
             + ([pl.BlockSpec(memory_space=pl.ANY)] if after is not None else []))
    scratch = [pltpu.VMEM((k, n), bf16)] if norm_b is not None else [pltpu.VMEM((tr, k), bf16)] if norm_a is not None else []
    sem = ("arbitrary", "arbitrary") if gain is not None else ("parallel", "parallel")
    return _pcall(body, name=name, out_shape=SDS((m, n), out_dtype), grid=grid, in_specs=specs, out_specs=o_spec,
                  scratch_shapes=scratch, compiler_params=_cp(sem))(*ins)


def _mm_tn_pieces(pieces, b, gain, tr, name):
    k, n = b.shape
    widths = [pc.shape[1] for pc in pieces]
    assert all(wd % tr == 0 for wd in widths)
    starts = [sum(widths[:p]) // tr for p in range(len(pieces))]
    counts = [wd // tr for wd in widths]
    npc = len(pieces)

    def body(*refs):
        a_refs, b_ref, g_ref, o_ref, hb_sc = refs[:npc], refs[npc], refs[npc + 1], refs[npc + 2], refs[npc + 3]
        i = pl.program_id(0)

        @pl.when(i == 0)
        def _():
            bv = b_ref[...]
            hb_sc[...] = (bv * _rstd(bv) * g_ref[...]).astype(bf16)

        for p in range(npc):
            @pl.when(jnp.logical_and(i >= starts[p], i < starts[p] + counts[p]))
            def _(p=p):
                o_ref[...] = lax.dot_general(a_refs[p][...].astype(bf16), hb_sc[...], (((0,), (0,)), ((), ())),
                                             preferred_element_type=f32).astype(bf16)

    a_specs = [pl.BlockSpec((k, tr), lambda i, p=p: (0, jnp.clip(i - starts[p], 0, counts[p] - 1))) for p in range(npc)]
    return _pcall(body, name=name, out_shape=SDS((sum(widths), n), bf16), grid=(sum(counts),),
                  in_specs=a_specs + [pl.BlockSpec((k, n), lambda i: (0, 0)), pl.BlockSpec((1, n), lambda i: (0, 0))],
                  out_specs=pl.BlockSpec((tr, n), lambda i: (i, 0)), scratch_shapes=[pltpu.VMEM((k, n), bf16)],
                  compiler_params=_cp(("arbitrary",)))(*pieces, b, gain)


def _mm_rms_bwd(a, b, tr, name, x, g, dres, stack, l):
    pieces = list(a) if isinstance(a, (list, tuple)) else [a]
    m, n, npc = pieces[0].shape[0], b.shape[1], len(pieces)
    k = sum(pc.shape[1] for pc in pieces)

    def body(*refs):
        a_refs, (b_ref, x_ref, g_ref, dres_ref, stack_ref, dx_ref, dg_ref) = refs[:npc], refs[npc:]
        dh, at = 0.0, 0
        for a_ref in a_refs:
            kp = a_ref.shape[1]
            dh = dh + jnp.dot(a_ref[...].astype(bf16), b_ref[pl.ds(at, kp), :].astype(bf16), preferred_element_type=f32)
            at += kp
        xv = x_ref[...]
        r = _rstd(xv)
        dx_ref[...] = dres_ref[...] + _rms_dx(xv, r, g_ref[...], dh)

        @pl.when(pl.program_id(0) == 0)
        def _():
            dg_ref[...] = jnp.zeros_like(dg_ref)

        dg_ref[...] += _colsum(dh * xv * r)

    row = pl.BlockSpec((tr, n), lambda i: (i, 0))
    vec = pl.BlockSpec((1, n), lambda i: (0, 0))
    a_specs = [pl.BlockSpec((tr, pc.shape[1]), lambda i: (i, 0)) for pc in pieces]
    return _pcall(body, name=name, out_shape=(SDS((m, n), f32), SDS(stack.shape, f32)), grid=(m // tr,),
                  in_specs=a_specs + [pl.BlockSpec((k, n), lambda i: (0, 0), pipeline_mode=pl.Buffered(1)),
                                      row, vec, row, pl.BlockSpec(memory_space=pl.ANY)],
                  out_specs=(row, pl.BlockSpec((None, 1, n), lambda i: (l, 0, 0))), input_output_aliases={npc + 4: 1},
                  compiler_params=_cp(("arbitrary",)))(*pieces, b, x, g, dres, stack)


def _rstd(x):
    return lax.rsqrt(jnp.mean(x * x, axis=-1, keepdims=True) + EPS)


def _rms_dx(x, r, g, dy):
    dxh = dy * g
    return r * (dxh - x * (r * r) * jnp.mean(dxh * x, axis=-1, keepdims=True))


def _sigmoid(x):
    return 0.5 * jnp.tanh(0.5 * x) + 0.5


def _colsum(x):
    return jnp.sum(x, axis=0, keepdims=True)


def _final_loss(x, g, target, name):
    s, d = x.shape

    def body(x_ref, g_ref, t_ref, loss_ref, dx_ref, dg_ref):
        xv, gv = x_ref[...], g_ref[...]
        r = _rstd(xv)
        diff = xv * r * gv - t_ref[...]
        dy = diff * (1.0 / d)

        @pl.when(pl.program_id(0) == 0)
        def _():
            dg_ref[...] = jnp.zeros_like(dg_ref)
            loss_ref[...] = jnp.zeros_like(loss_ref)

        part = jnp.sum(jnp.sum(diff * diff, axis=1, keepdims=True), axis=0, keepdims=True) * (0.5 / d)
        loss_ref[...] += part
        dx_ref[...] = _rms_dx(xv, r, gv, dy)
        dg_ref[...] += _colsum(dy * xv * r)

    row = pl.BlockSpec((ROW_TILE, d), lambda i: (i, 0))
    vec = pl.BlockSpec((1, d), lambda i: (0, 0))
    one = pl.BlockSpec((8, 128), lambda i: (0, 0))
    return _pcall(body, name=name, out_shape=(SDS((8, 128), f32), SDS((s, d), f32), SDS((1, d), f32)),
                  grid=(s // ROW_TILE,), in_specs=[row, vec, row], out_specs=(one, row, vec),
                  compiler_params=_cp(("arbitrary",)))(x, g, target)


Z_AH, Z_AB, Z_AC, Z_CV, Z_CG = 0, 1, 2, 9, 10
Z_Q128, Z_K128, Z_V128 = 6, 10, 14


def _cur(t, w, col):
    return pl.BlockSpec((t, w), lambda i: (i, col))


def _prev(hb, t, w, col):
    return pl.BlockSpec((hb, w), lambda i: (jnp.maximum(i * (t // hb) - 1, 0), col))


def _next(hb, t, w, col, rows):
    return pl.BlockSpec((hb, w), lambda i: (jnp.minimum((i + 1) * (t // hb), rows // hb - 1), col))


def _layernorm_parts(u, g, b):
    mu = jnp.mean(u, axis=-1, keepdims=True)
    var = jnp.mean(jnp.square(u - mu), axis=-1, keepdims=True)
    rs = lax.rsqrt(var + EPS)
    xhat = (u - mu) * rs
    return xhat, rs, xhat * g + b


def _fill_shifted(rot):
    n = rot.shape[1]
    for r in range(1, 8):
        rot[r, pl.ds(0, n - 8), :] = rot[0, pl.ds(r, n - 8), :]


def _window(rot, offset, rows):
    return rot[offset % 8, pl.ds(offset - offset % 8, rows), :]


def _mixer_fwd(z, yb, conv_a, conv_c, cb, lng, lnb, og, name):
    s = z.shape[0]
    t, hb, w = ROW_TILE, HALO, 256

    def body(ah, ab, ac, cv, cg, ah_p, ac_p, cv_p, cg_p, yb_ref, wa, wc, cb_ref, lng_ref, lnb_ref, og_ref, y_ref, p_sc, u_sc):
        first = pl.program_id(0) == 0

        def prev(ref):
            return jnp.where(first, 0.0, ref[...])

        p_sc[pl.ds(0, hb), :] = prev(ah_p) * prev(ac_p)
        p_sc[pl.ds(hb, t), :] = ah[...] * ac[...]
        conv = jnp.zeros((t, w), f32)
        for k in range(K_SHORT):
            conv = conv + wa[pl.ds(k, 1), :] * p_sc[pl.ds(hb - (K_SHORT - 1) + k, t), :]
        y_a = ab[...] * conv
        u_sc[0, pl.ds(0, hb), :] = prev(cv_p) * _sigmoid(prev(cg_p))
        u_sc[0, pl.ds(hb, t), :] = cv[...] * _sigmoid(cg[...])
        _fill_shifted(u_sc)
        u = jnp.zeros((t, w), f32) + cb_ref[...]
        for k in range(K_CONF):
            u = u + wc[pl.ds(k, 1), :] * _window(u_sc, hb - (K_CONF - 1) + k, t)
        _, _, lnv = _layernorm_parts(u, lng_ref[...], lnb_ref[...])
        y_c = lnv * _sigmoid(lnv)
        ybv = yb_ref[...]
        y_ref[:, 0:256] = (y_a * _rstd(y_a) * og_ref[:, 0:256]).astype(bf16)
        y_ref[:, 256:768] = (ybv * _rstd(ybv) * og_ref[:, 256:768]).astype(bf16)
        y_ref[:, 768:1024] = (y_c * _rstd(y_c) * og_ref[:, 768:1024]).astype(bf16)

    full = lambda shape: pl.BlockSpec(shape, lambda i: tuple(0 for _ in shape))
    in_specs = [_cur(t, w, c) for c in (Z_AH, Z_AB, Z_AC, Z_CV, Z_CG)] + [_prev(hb, t, w, c) for c in (Z_AH, Z_AC, Z_CV, Z_CG)]
    in_specs += [_cur(t, 512, 0), full((K_SHORT, w)), full((K_CONF, w)), full((1, w)), full((1, w)), full((1, w)), full((1, 1024))]
    return _pcall(body, name=name, out_shape=SDS((s, 1024), bf16), grid=(s // t,), in_specs=in_specs,
                  out_specs=_cur(t, 1024, 0), scratch_shapes=[pltpu.VMEM((hb + t, w), f32), pltpu.VMEM((8, hb + t, w), f32)],
                  compiler_params=_cp(("parallel",)))(z, z, z, z, z, z, z, z, z, yb, conv_a, conv_c, cb, lng, lnb, og)


PG_CONV_A, PG_CONV_C, PG_CB, PG_LNG, PG_LNB, PG_OG_A, PG_OG_C, PG_ROWS = 0, 8, 40, 41, 42, 43, 44, 48


def _mixer_bwd(z, yb, dy, conv_a, conv_c, cb, lng, lnb, og, stacks, l, name):
    s = z.shape[0]
    t, hb, w = ROW_TILE, HALO, 256
    e = t + hb

    def body(ah, ab, ac, cv, cg, ah_p, ac_p, cv_p, cg_p, ah_n, ab_n, ac_n, cv_n, cg_n,
             dya, dyb1, dyb2, dyc, dya_n, dyc_n, yb_ref, wa, wc, cb_ref, lng_ref, lnb_ref, og_ref, pg_stack, dgb_stack,
             dz_ref, dyb_ref, pg_ref, dgb_ref, p_sc, u_sc, dc_sc, du_sc):
        i = pl.program_id(0)
        first, last = i == 0, i == pl.num_programs(0) - 1

        def prev(ref):
            return jnp.where(first, 0.0, ref[...])

        def nxt(ref):
            return jnp.where(last, 0.0, ref[...])

        def ext(cur_ref, next_ref):
            return jnp.concatenate([cur_ref[...], nxt(next_ref)], axis=0)

        @pl.when(first)
        def _():
            pg_ref[...] = jnp.zeros_like(pg_ref)
            dgb_ref[...] = jnp.zeros_like(dgb_ref)

        g_a = og_ref[:, 0:256]
        p_sc[pl.ds(0, hb), :] = prev(ah_p) * prev(ac_p)
        p_sc[pl.ds(hb, t), :] = ah[...] * ac[...]
        p_sc[pl.ds(hb + t, hb), :] = nxt(ah_n) * nxt(ac_n)
        conv = jnp.zeros((e, w), f32)
        for k in range(K_SHORT):
            conv = conv + wa[pl.ds(k, 1), :] * p_sc[pl.ds(hb - (K_SHORT - 1) + k, e), :]
        ab_e, dya_e = ext(ab, ab_n), ext(dya, dya_n)
        y_a = ab_e * conv
        r_a = _rstd(y_a)
        dy_a = _rms_dx(y_a, r_a, g_a, dya_e)
        dc_sc[...] = dy_a * ab_e
        dz_ref[:, 256:512] = (dy_a[:t] * conv[:t]).astype(bf16)
        dp = jnp.zeros((t, w), f32)
        for k in range(K_SHORT):
            dp = dp + wa[pl.ds(k, 1), :] * dc_sc[pl.ds(K_SHORT - 1 - k, t), :]
            pg_ref[pl.ds(PG_CONV_A + k, 1), :] += _colsum(dc_sc[pl.ds(0, t), :] * p_sc[pl.ds(hb - (K_SHORT - 1) + k, t), :])
        dz_ref[:, 0:256] = (dp * ac[...]).astype(bf16)
        dz_ref[:, 512:768] = (dp * ah[...]).astype(bf16)
        pg_ref[pl.ds(PG_OG_A, 1), :] += _colsum(dya[...] * y_a[:t] * r_a[:t])

        g_c = og_ref[:, 768:1024]
        u_sc[0, pl.ds(0, hb), :] = prev(cv_p) * _sigmoid(prev(cg_p))
        u_sc[0, pl.ds(hb, t), :] = cv[...] * _sigmoid(cg[...])
        u_sc[0, pl.ds(hb + t, hb), :] = nxt(cv_n) * _sigmoid(nxt(cg_n))
        _fill_shifted(u_sc)
        u = jnp.zeros((e, w), f32) + cb_ref[...]
        for k in range(K_CONF):
            u = u + wc[pl.ds(k, 1), :] * _window(u_sc, hb - (K_CONF - 1) + k, e)
        xhat, rs, lnv = _layernorm_parts(u, lng_ref[...], lnb_ref[...])
        sg = _sigmoid(lnv)
        y_c = lnv * sg
        r_c = _rstd(y_c)
        dyc_e = ext(dyc, dyc_n)
        dlnv = _rms_dx(y_c, r_c, g_c, dyc_e) * (sg * (1.0 + lnv * (1.0 - sg)))
        dxh = dlnv * lng_ref[...]
        du = rs * (dxh - jnp.mean(dxh, axis=-1, keepdims=True) - xhat * jnp.mean(dxh * xhat, axis=-1, keepdims=True))
        du_sc[0] = du
        _fill_shifted(du_sc)
        duin = jnp.zeros((t, w), f32)
        for k in range(K_CONF):
            duin = duin + wc[pl.ds(k, 1), :] * _window(du_sc, K_CONF - 1 - k, t)
            pg_ref[pl.ds(PG_CONV_C + k, 1), :] += _colsum(du[:t] * _window(u_sc, hb - (K_CONF - 1) + k, t))
        sgg = _sigmoid(cg[...])
        dz_ref[:, 2304:2560] = (duin * sgg).astype(bf16)
        dz_ref[:, 2560:2816] = (duin * cv[...] * sgg * (1.0 - sgg)).astype(bf16)
        pg_ref[pl.ds(PG_CB, 1), :] += _colsum(du[:t])
        pg_ref[pl.ds(PG_LNG, 1), :] += _colsum(dlnv[:t] * xhat[:t])
        pg_ref[pl.ds(PG_LNB, 1), :] += _colsum(dlnv[:t])
        pg_ref[pl.ds(PG_OG_C, 1), :] += _colsum(dyc[...] * y_c[:t] * r_c[:t])

        g_b = og_ref[:, 256:768]
        ybv = yb_ref[...]
        r_b = _rstd(ybv)
        dyb = jnp.concatenate([dyb1[...], dyb2[...]], axis=1)
        dyb_ref[...] = _rms_dx(ybv, r_b, g_b, dyb)
        dgb_ref[...] += _colsum(dyb * ybv * r_b)

    full = lambda shape: pl.BlockSpec(shape, lambda i: tuple(0 for _ in shape))
    zc = (Z_AH, Z_AB, Z_AC, Z_CV, Z_CG)
    in_specs = [_cur(t, w, c) for c in zc] + [_prev(hb, t, w, c) for c in (Z_AH, Z_AC, Z_CV, Z_CG)]
    in_specs += [_next(hb, t, w, c, s) for c in zc]
    in_specs += [_cur(t, w, c) for c in (0, 1, 2, 3)] + [_next(hb, t, w, 0, s), _next(hb, t, w, 3, s)]
    in_specs += [_cur(t, 512, 0), full((K_SHORT, w)), full((K_CONF, w)), full((1, w)), full((1, w)), full((1, w)), full((1, 1024))]
    in_specs += [pl.BlockSpec(memory_space=pl.ANY)] * 2
    out_shape = (SDS((s, z.shape[1]), bf16), SDS((s, 512), f32), SDS(stacks[0].shape, f32), SDS(stacks[1].shape, f32))
    out_specs = (_cur(t, z.shape[1], 0), _cur(t, 512, 0), pl.BlockSpec((None, PG_ROWS, w), lambda i: (l, 0, 0)),
                 pl.BlockSpec((None, 1, 512), lambda i: (l, 0, 0)))
    scratch = [pltpu.VMEM((hb + t + hb, w), f32), pltpu.VMEM((8, hb + t + hb, w), f32), pltpu.VMEM((e, w), f32), pltpu.VMEM((8, e, w), f32)]
    args = [z] * 14 + [dy] * 6 + [yb, conv_a, conv_c, cb, lng, lnb, og] + list(stacks)
    return _pcall(body, name=name, out_shape=out_shape, grid=(s // t,), in_specs=in_specs, out_specs=out_specs,
                  input_output_aliases={27: 2, 28: 3}, scratch_shapes=scratch, compiler_params=_cp(("arbitrary",)))(*args)


def _ffn_act_fwd(upre, conv_f, name):
    s, f2 = upre.shape
    t, hb, w = FFN_FWD_ROWS, HALO_FFN, 256
    nj = f2 // 2 // w

    def body(g_c, v_c, g_p, v_p, wg, wv, o_ref):
        first = pl.program_id(1) == 0
        wgs = [wg[pl.ds(k, 1), :] for k in range(K_FFN)]
        wvs = [wv[pl.ds(k, 1), :] for k in range(K_FFN)]
        rows = FFN_CHUNK
        first_tap = hb - (K_FFN - 1)

        def emit(gb, vb, r0):
            ug = sum(wgs[k] * gb[first_tap + k:first_tap + k + rows] for k in range(K_FFN))
            uv = sum(wvs[k] * vb[first_tap + k:first_tap + k + rows] for k in range(K_FFN))
            o_ref[pl.ds(r0, rows), :] = (ug * _sigmoid(ug) * uv).astype(bf16)

        emit(jnp.concatenate([jnp.where(first, 0.0, g_p[...]), g_c[pl.ds(0, rows), :]], axis=0),
             jnp.concatenate([jnp.where(first, 0.0, v_p[...]), v_c[pl.ds(0, rows), :]], axis=0), 0)

        def chunk(c, carry):
            r0 = pl.multiple_of(c * rows, rows)
            back = pl.ds(pl.multiple_of(r0 - hb, hb), rows + hb)
            emit(g_c[back, :], v_c[back, :], r0)
            return carry

        lax.fori_loop(1, t // rows, chunk, 0, unroll=3)

    cur = lambda off: pl.BlockSpec((t, w), lambda j, i: (i, off + j))
    prv = lambda off: pl.BlockSpec((hb, w), lambda j, i: (jnp.maximum(i * (t // hb) - 1, 0), off + j))
    wsp = lambda off: pl.BlockSpec((K_FFN, w), lambda j, i: (0, off + j))
    return _pcall(body, name=name, out_shape=SDS((s, f2 // 2), bf16), grid=(nj, s // t),
                  in_specs=[cur(0), cur(nj), prv(0), prv(nj), wsp(0), wsp(nj)],
                  out_specs=pl.BlockSpec((t, w), lambda j, i: (i, j)),
                  compiler_params=_cp(("parallel", "parallel")))(upre, upre, upre, upre, conv_f, conv_f)


def _ffn_act_bwd(upre, dact, conv_f, stacks, l, name):
    s, f2 = upre.shape
    t, hb, w = FFN_BWD_ROWS, HALO_FFN, 256
    nj = f2 // 2 // w

    def body(g_c, v_c, g_p, v_p, g_n, v_n, da_c, da_n, wg, wv, sg_ref, sv_ref, dg_ref, dv_ref, dwg_ref, dwv_ref):
        i = pl.program_id(1)
        first, last = i == 0, i == pl.num_programs(1) - 1

        @pl.when(first)
        def _():
            dwg_ref[...] = jnp.zeros_like(dwg_ref)
            dwv_ref[...] = jnp.zeros_like(dwv_ref)

        wgs = [wg[pl.ds(k, 1), :] for k in range(K_FFN)]
        wvs = [wv[pl.ds(k, 1), :] for k in range(K_FFN)]
        rows = FFN_CHUNK
        n = rows + hb
        first_tap = hb - (K_FFN - 1)
        nchunks = t // rows

        def groups_of_8(x):
            return sum(x[8 * q:8 * q + 8] for q in range(rows // 8))

        def emit(gb, vb, da, r0, carry):
            g_taps = [gb[first_tap + k:first_tap + k + n] for k in range(K_FFN)]
            v_taps = [vb[first_tap + k:first_tap + k + n] for k in range(K_FFN)]
            ug = sum(wgs[k] * g_taps[k] for k in range(K_FFN))
            uv = sum(wvs[k] * v_taps[k] for k in range(K_FFN))
            sg = _sigmoid(ug)
            dv = da * (ug * sg)
            dg = da * uv * (sg * (1.0 + ug * (1.0 - sg)))
            new = []
            for taps, dval, ws, dref in ((g_taps, dg, wgs, dg_ref), (v_taps, dv, wvs, dv_ref)):
                acc = sum(ws[k] * dval[K_FFN - 1 - k:K_FFN - 1 - k + rows] for k in range(K_FFN))
                dref[pl.ds(r0, rows), :] = acc.astype(bf16)
                new += [groups_of_8(dval[:rows] * taps[k][:rows]) for k in range(K_FFN)]
            return tuple(a + b for a, b in zip(carry, new))

        def chunk(c, carry):
            r0 = pl.multiple_of(c * rows, rows)
            around = pl.ds(pl.multiple_of(r0 - hb, hb), n + hb)
            return emit(g_c[around, :], v_c[around, :], da_c[pl.ds(r0, n), :], r0, carry)

        def edge(prev_ref, cur_ref, lo):
            return jnp.concatenate([jnp.where(first, 0.0, prev_ref[...]), cur_ref[pl.ds(0, lo), :]], axis=0)

        def tail(cur_ref, next_ref, lo):
            return jnp.concatenate([cur_ref[pl.ds(t - lo, lo), :], jnp.where(last, 0.0, next_ref[...])], axis=0)

        zero = jnp.zeros((8, w), f32)
        sums = emit(edge(g_p, g_c, n), edge(v_p, v_c, n), da_c[pl.ds(0, n), :], 0, (zero,) * (2 * K_FFN))
        sums = lax.fori_loop(1, nchunks - 1, chunk, sums, unroll=2)
        sums = emit(tail(g_c, g_n, rows + hb), tail(v_c, v_n, rows + hb), tail(da_c, da_n, rows), t - rows, sums)
        for k in range(K_FFN):
            dwg_ref[pl.ds(k, 1), :] += _colsum(sums[k])
            dwv_ref[pl.ds(k, 1), :] += _colsum(sums[K_FFN + k])

    cur = lambda off: pl.BlockSpec((t, w), lambda j, i: (i, off + j))
    prv = lambda off: pl.BlockSpec((hb, w), lambda j, i: (jnp.maximum(i * (t // hb) - 1, 0), off + j))
    nxt = lambda off: pl.BlockSpec((hb, w), lambda j, i: (jnp.minimum((i + 1) * (t // hb), s // hb - 1), off + j))
    wsp = lambda off: pl.BlockSpec((K_FFN, w), lambda j, i: (0, off + j))
    half = SDS((s, f2 // 2), bf16)
    dws = SDS(stacks[0].shape, f32)
    o_cur = pl.BlockSpec((t, w), lambda j, i: (i, j))
    o_dw = pl.BlockSpec((None, 8, w), lambda j, i: (l, 0, j))
    anywhere = pl.BlockSpec(memory_space=pl.ANY)
    return _pcall(body, name=name, out_shape=(half, half, dws, dws), grid=(nj, s // t),
                  in_specs=[cur(0), cur(nj), prv(0), prv(nj), nxt(0), nxt(nj), cur(0), nxt(0), wsp(0), wsp(nj), anywhere, anywhere],
                  out_specs=(o_cur, o_cur, o_dw, o_dw), input_output_aliases={10: 2, 11: 3},
                  compiler_params=_cp(("parallel", "arbitrary")))(upre, upre, upre, upre, upre, upre, dact, dact, conv_f, conv_f, *stacks)


def _bias_tables(rel_bias, name):
    nb = len(DILATIONS)

    def body(rb_ref, idx_ref, o_ref):
        h = pl.program_id(1)
        iv = idx_ref[0]
        qi = lax.broadcasted_iota(jnp.int32, (BLK, 2 * BLK), 0)
        kj = lax.broadcasted_iota(jnp.int32, (BLK, 2 * BLK), 1)
        band = jnp.logical_and(kj >= qi, kj <= qi + N_KEYS)
        acc = jnp.zeros((BLK, 2 * BLK), f32)
        for b in range(N_BUCKETS):
            acc = jnp.where(iv == b, rb_ref[b, h], acc)
        o_ref[0, 0, 0] = jnp.where(band, acc, NEG)
        o_ref[0, 1, 0] = jnp.where(jnp.logical_and(band, kj >= BLK), acc, NEG)

    return _pcall(body, name=name, out_shape=SDS((nb, 2, 8, BLK, 2 * BLK), f32), grid=(nb, 8),
                  in_specs=[pl.BlockSpec(memory_space=pltpu.SMEM), pl.BlockSpec((1, BLK, 2 * BLK), lambda br, h: (br, 0, 0))],
                  out_specs=pl.BlockSpec((1, 2, 1, BLK, 2 * BLK), lambda br, h: (br, 0, h, 0, 0)),
                  compiler_params=_cp(("parallel", "parallel")))(rel_bias, jnp.asarray(_BUCKETS))


def _bias_grad(ds_tabs, name):
    nl, nb = ds_tabs.shape[0], ds_tabs.shape[1]

    def body(t_ref, idx_ref, o_ref):
        br, h = pl.program_id(0), pl.program_id(1)

        @pl.when(jnp.logical_and(br == 0, h == 0))
        def _():
            o_ref[...] = jnp.zeros_like(o_ref)

        tab = t_ref[0, 0, 0]
        for l in range(1, nl):
            tab = tab + t_ref[l, 0, 0]
        iv = idx_ref[0]
        row = lax.broadcasted_iota(jnp.int32, (N_BUCKETS, 128), 0)
        col = lax.broadcasted_iota(jnp.int32, (N_BUCKETS, 128), 1)
        acc = jnp.zeros((N_BUCKETS, 128), f32)
        for b in range(N_BUCKETS):
            sel = jnp.where(iv == b, tab, 0.0)
            tot = jnp.sum(jnp.sum(sel, axis=0, keepdims=True), axis=1, keepdims=True)
            acc = acc + jnp.where(jnp.logical_and(row == b, col == h), tot, 0.0)
        o_ref[...] += acc

    return _pcall(body, name=name, out_shape=SDS((N_BUCKETS, 128), f32), grid=(nb, 8),
                  in_specs=[pl.BlockSpec((nl, 1, 1, BLK, 2 * BLK), lambda br, h: (0, br, h, 0, 0)),
                            pl.BlockSpec((1, BLK, 2 * BLK), lambda br, h: (br, 0, 0))],
                  out_specs=pl.BlockSpec((N_BUCKETS, 128), lambda br, h: (0, 0)),
                  compiler_params=_cp(("arbitrary", "arbitrary")))(ds_tabs, jnp.asarray(_BUCKETS))


def _stack_heads(x, head0):
    zero = jnp.zeros_like(x)
    return jnp.concatenate([jnp.where(head0, x, zero), jnp.where(head0, zero, x)], axis=0)


def _block_rows(blk, d, nblk):
    per_res = nblk // d
    res, n = blk // per_res, blk % per_res
    cur = pl.ds(res + d * BLK * n, BLK, stride=d) if d > 1 else pl.ds(pl.multiple_of(BLK * n, BLK), BLK)
    before = jnp.maximum(n - 1, 0)
    prev = pl.ds(res + d * BLK * before, BLK, stride=d) if d > 1 else pl.ds(pl.multiple_of(BLK * before, BLK), BLK)
    return cur, prev, jnp.where(n == 0, 1, 0)


def _attn_fwd(z, bias, name):
    s = z.shape[0]
    nblk = s // BLK

    def body(q_ref, k_ref, v_ref, b_ref, y_ref, l_ref):
        lane = lax.broadcasted_iota(jnp.int32, (BLK, BLK), 1)
        head0 = lane < D_HEAD

        for branch, d in enumerate(DILATIONS):
            def step(blk, carry, branch=branch, d=d):
                cur, prev, variant = _block_rows(blk, d, nblk)
                q = (q_ref[cur, :] * (D_HEAD ** -0.5)).astype(bf16)
                if d == nblk:
                    kk, vv = k_ref[cur, :].astype(bf16), v_ref[cur, :].astype(bf16)
                    bias_blk = b_ref[branch, 1, 0, :, BLK:]
                else:
                    kk = jnp.concatenate([k_ref[prev, :], k_ref[cur, :]], axis=0).astype(bf16)
                    vv = jnp.concatenate([v_ref[prev, :], v_ref[cur, :]], axis=0).astype(bf16)
                    bias_blk = b_ref[branch, variant, 0]
                sc = lax.dot_general(_stack_heads(q, head0), kk, (((1,), (1,)), ((), ())), preferred_element_type=f32)
                sc = sc + bias_blk
                m = jnp.max(sc, axis=-1, keepdims=True)
                p = jnp.exp(sc - m)
                den = jnp.sum(p, axis=-1, keepdims=True)
                out = jnp.dot(p.astype(bf16), vv, preferred_element_type=f32) / den
                lse = m + jnp.log(den)
                out = jnp.where(head0, out[:BLK], out[BLK:])
                lse = jnp.where(head0, lse[:BLK], lse[BLK:]) + jnp.zeros((BLK, BLK), f32)
                if branch == 0:
                    y_ref[cur, :] = out
                    l_ref[cur, :] = lse
                else:
                    y_old, l_old = y_ref[cur, :], l_ref[cur, :]
                    top = jnp.maximum(l_old, lse)
                    e_old, e_new = jnp.exp(l_old - top), jnp.exp(lse - top)
                    tot = e_old + e_new
                    y_ref[cur, :] = (e_old * y_old + e_new * out) / tot
                    l_ref[cur, :] = top + jnp.log(tot)
                return carry

            lax.fori_loop(0, nblk, step, 0, unroll=4)

    col = lambda off: pl.BlockSpec((s, BLK), lambda hp: (0, off + hp))
    out = SDS((s, 4 * BLK), f32)
    nbr = len(DILATIONS)
    return _pcall(body, name=name, out_shape=(out, out), grid=(4,),
                  in_specs=[col(Z_Q128), col(Z_K128), col(Z_V128), pl.BlockSpec((nbr, 2, 1, 2 * BLK, 2 * BLK), lambda hp: (0, 0, hp, 0, 0))],
                  out_specs=(col(0), col(0)), compiler_params=_cp(("parallel",)))(z, z, z, bias)


def _attn_bwd(z, dyb, yb, lt, bias, dz, name):
    s = z.shape[0]
    nblk = s // BLK
    scale = D_HEAD ** -0.5
    chunk = 256

    def body(q_ref, k_ref, v_ref, dy_ref, y_ref, lt_ref, b_ref, dz_in, dz_ref, ds_ref, aq, ak, av, stage, sems):
        lane = lax.broadcasted_iota(jnp.int32, (BLK, BLK), 1)
        head0 = lane < D_HEAD
        ak[...] = jnp.zeros_like(ak)
        av[...] = jnp.zeros_like(av)
        ds_ref[...] = jnp.zeros_like(ds_ref)

        for branch, d in enumerate(DILATIONS):
            def step(blk, carry, branch=branch, d=d):
                cur, prev, variant = _block_rows(blk, d, nblk)
                own_keys = d == nblk
                q = (q_ref[cur, :] * scale).astype(bf16)
                if own_keys:
                    kk, vv = k_ref[cur, :].astype(bf16), v_ref[cur, :].astype(bf16)
                    bias_blk = b_ref[branch, 1, 0, :, BLK:]
                else:
                    kk = jnp.concatenate([k_ref[prev, :], k_ref[cur, :]], axis=0).astype(bf16)
                    vv = jnp.concatenate([v_ref[prev, :], v_ref[cur, :]], axis=0).astype(bf16)
                    bias_blk = b_ref[branch, variant, 0]
                dy = dy_ref[cur, :]
                lt_b = lt_ref[cur, :]
                qs = _stack_heads(q, head0)
                dys = _stack_heads(dy.astype(bf16), head0)
                dd = jnp.sum(_stack_heads(dy * y_ref[cur, :], head0), axis=-1, keepdims=True)
                lts = jnp.concatenate([lt_b[:, 0:1], lt_b[:, D_HEAD:D_HEAD + 1]], axis=0)
                sc = lax.dot_general(qs, kk, (((1,), (1,)), ((), ())), preferred_element_type=f32)
                p = jnp.exp(sc + bias_blk - lts)
                dp = lax.dot_general(dys, vv, (((1,), (1,)), ((), ())), preferred_element_type=f32)
                ds = p * (dp - dd)
                if own_keys:
                    ds_ref[branch, 0, :, BLK:] += ds
                else:
                    ds_ref[branch, 0] += ds
                dsb = ds.astype(bf16)
                dq = jnp.dot(dsb, kk, preferred_element_type=f32) * scale
                dkk = lax.dot_general(dsb, qs, (((0,), (0,)), ((), ())), preferred_element_type=f32)
                dvv = lax.dot_general(p.astype(bf16), dys, (((0,), (0,)), ((), ())), preferred_element_type=f32)
                dq = jnp.where(head0, dq[:BLK], dq[BLK:])
                if branch == 0:
                    aq[cur, :] = dq
                else:
                    aq[cur, :] += dq
                if own_keys:
                    ak[cur, :] += dkk
                    av[cur, :] += dvv
                else:
                    ak[prev, :] += dkk[:BLK]
                    ak[cur, :] += dkk[BLK:]
                    av[prev, :] += dvv[:BLK]
                    av[cur, :] += dvv[BLK:]
                return carry

            lax.fori_loop(0, nblk, step, 0, unroll=4)

        def store(c, carry):
            rows = pl.ds(pl.multiple_of(c * chunk, chunk), chunk)
            stage[0, rows, :] = aq[rows, :].astype(bf16)
            stage[1, rows, :] = ak[rows, :].astype(bf16)
            stage[2, rows, :] = av[rows, :].astype(bf16)
            return carry

        lax.fori_loop(0, s // chunk, store, 0)
        hp = pl.program_id(0)
        copies = []
        for t, first_col in enumerate((Z_Q128, Z_K128, Z_V128)):
            cols = pl.ds(pl.multiple_of((first_col + hp) * BLK, BLK), BLK)
            copies.append(pltpu.make_async_copy(stage.at[t], dz_ref.at[:, cols], sems.at[t]))
            copies[-1].start()
        for cp in copies:
            cp.wait()

    col = lambda off: pl.BlockSpec((s, BLK), lambda hp: (0, off + hp))
    nbr = len(DILATIONS)
    tabs = pl.BlockSpec((nbr, 1, 2 * BLK, 2 * BLK), lambda hp: (0, hp, 0, 0))
    bias_spec = pl.BlockSpec((nbr, 2, 1, 2 * BLK, 2 * BLK), lambda hp: (0, 0, hp, 0, 0))
    anywhere = pl.BlockSpec(memory_space=pl.ANY)
    scratch = [pltpu.VMEM((s, BLK), f32)] * 3 + [pltpu.VMEM((3, s, BLK), bf16), pltpu.SemaphoreType.DMA((3,))]
    return _pcall(body, name=name, out_shape=(SDS(dz.shape, dz.dtype), SDS((nbr, 4, 2 * BLK, 2 * BLK), f32)), grid=(4,),
                  in_specs=[col(Z_Q128), col(Z_K128), col(Z_V128), col(0), col(0), col(0), bias_spec, anywhere],
                  out_specs=(anywhere, tabs), input_output_aliases={7: 0}, scratch_shapes=scratch,
                  compiler_params=_cp(("arbitrary",)))(z, z, z, dyb, yb, lt, bias, dz)


def _adamw_math(w, g, m, v):
    m = ADAM_B1 * m + (1.0 - ADAM_B1) * g
    v = ADAM_B2 * v + (1.0 - ADAM_B2) * (g * g)
    m_hat = m / (1.0 - ADAM_B1 ** ADAM_STEP)
    v_hat = v / (1.0 - ADAM_B2 ** ADAM_STEP)
    delta = -ADAM_LR * (m_hat / (jnp.sqrt(v_hat) + ADAM_EPS) + ADAM_WD * w)
    return delta, m, v


def _adamw_layer(land, send, w, m, v, prev, l, tr, me, name):
    _, r, c = w.shape
    assert r % tr == 0

    def body(me_ref, land_ref, own_ref, w_ref, m_ref, v_ref, p0, p1, p2, p3, g_ref, d_ref, nm_ref, nv_ref):
        mine = jnp.zeros((tr, c), jnp.int32) + me_ref[0]
        own = own_ref[...]
        g = jnp.zeros((tr, c), f32)
        for j in range(N_DEV):
            g = g + jnp.where(mine == j, own, land_ref[j]).astype(f32)
        delta, nm, nv = _adamw_math(w_ref[...], g, m_ref[...], v_ref[...])
        g_ref[...] = g
        d_ref[...] = delta
        nm_ref[...] = nm
        nv_ref[...] = nv

    parts = pl.BlockSpec((N_DEV, tr, c), lambda i, me_ref: (0, i, 0))
    own = pl.BlockSpec((None, tr, c), lambda i, me_ref: (me_ref[0], i, 0))
    row = pl.BlockSpec((None, tr, c), lambda i, me_ref: (l, i, 0))
    anywhere = pl.BlockSpec(memory_space=pl.ANY)
    out = SDS(w.shape, f32)
    grid_spec = pltpu.PrefetchScalarGridSpec(num_scalar_prefetch=1, grid=(r // tr,), in_specs=[parts, own, row, row, row] + [anywhere] * 4,
                                             out_specs=(row, row, row, row))
    return _pcall(body, name=name, out_shape=(out, out, out, out), grid_spec=grid_spec,
                  input_output_aliases={6: 0, 7: 1, 8: 2, 9: 3}, compiler_params=_cp(("parallel",)))(me, land, send, w, m, v, *prev)


def _sum_parts(parts, name):
    _, r, c = parts.shape

    def body(p_ref, o_ref):
        g = p_ref[0]
        for j in range(1, N_DEV):
            g = g + p_ref[j]
        o_ref[...] = g

    return _pcall(body, name=name, out_shape=SDS((r, c), f32), compiler_params=_cp())(parts)


def _adamw_small(gs, ws, ms, vs, name):
    n = len(ws)

    def body(*refs):
        for t in range(n):
            delta, nm, nv = _adamw_math(refs[n + t][...], refs[t][...], refs[2 * n + t][...], refs[3 * n + t][...])
            refs[4 * n + t][...] = delta
            refs[5 * n + t][...] = nm
            refs[6 * n + t][...] = nv

    out = [SDS(wt.shape, f32) for wt in ws] * 3
    res = _pcall(body, name=name, out_shape=out, compiler_params=_cp())(*gs, *ws, *ms, *vs)
    return list(res[:n]), list(res[n:2 * n]), list(res[2 * n:])


def _my_place():
    return lax.axis_index("x"), lax.axis_index("y"), lax.axis_index("c")


def _all_gather(xs, name, after=None):
    n = len(xs)

    def body(*refs):
        x_refs, out_refs = refs[:n], refs[-3 - n:-3]
        send_sems, recv_sems, local_sems = refs[-3:]
        mx, my, mc = _my_place()
        me, sibling = (mx, my, mc), (mx, my, 1 - mc)
        chips = [(1 - mx, my), (mx, 1 - my), (1 - mx, 1 - my)]

        def rows(t, px, py, pc):
            return out_refs[t].at[4 * px + 2 * py + pc]

        def copy(t, k, block, to, src=None):
            return pltpu.make_async_remote_copy(src_ref=rows(t, *block) if src is None else src, dst_ref=rows(t, *block),
                                                send_sem=send_sems.at[7 * t + k], recv_sem=recv_sems.at[7 * t + k],
                                                device_id=to, device_id_type=MESH)

        mine = [pltpu.make_async_copy(x_refs[t], rows(t, *me), local_sems.at[t]) for t in range(n)]
        first, passed = [], []
        for t in range(n):
            mine[t].start()
            first += [copy(t, 0, me, sibling, src=x_refs[t])] + [copy(t, 1 + j, me, (*chip, mc), src=x_refs[t]) for j, chip in enumerate(chips)]
        for cp in first:
            cp.start()
        for t in range(n):
            for j, chip in enumerate(chips):
                copy(t, 1 + j, (*chip, mc), me).wait_recv()
                passed.append(copy(t, 4 + j, (*chip, mc), sibling))
                passed[-1].start()
        for t in range(n):
            copy(t, 0, sibling, me).wait_recv()
            for j, chip in enumerate(chips):
                copy(t, 4 + j, (*chip, 1 - mc), me).wait_recv()
        for cp in first + passed:
            cp.wait_send()
        for t in range(n):
            mine[t].wait()

    ins = list(xs) + ([after] if after is not None else [])
    anywhere = pl.BlockSpec(memory_space=pl.ANY)
    return list(_pcall(body, name=name, out_shape=[SDS((N_DEV,) + x.shape, x.dtype) for x in xs],
                       in_specs=[anywhere] * len(ins), out_specs=[anywhere] * n,
                       scratch_shapes=[pltpu.SemaphoreType.DMA((7 * n,)), pltpu.SemaphoreType.DMA((7 * n,)),
                                       pltpu.SemaphoreType.DMA((n,))])(*ins))


HBM_SPEC = pl.BlockSpec(memory_space=pltpu.HBM)
SEM_SPEC = pl.BlockSpec(memory_space=pltpu.SEMAPHORE)
ANY_SPEC = pl.BlockSpec(memory_space=pl.ANY)
DATAFLOW = pltpu.SideEffectType.DATAFLOW_SIDE_EFFECTING


def _peers():
    mx, my, mc = _my_place()
    out = []
    for k in range(1, N_DEV):
        tx, ty, tc = mx ^ ((k >> 2) & 1), my ^ ((k >> 1) & 1), mc ^ (k & 1)
        out.append(((tx, ty, tc), 4 * tx + 2 * ty + tc))
    return out


def _in_hbm(arrays):
    return [pltpu.with_memory_space_constraint(a, pltpu.HBM) for a in arrays]


N_FIRST_LEG, N_SECOND_LEG = 4, 3


def _chip_peers():
    mx, my, mc = _my_place()
    return [((tx, ty, mc), 4 * tx + 2 * ty + mc) for tx, ty in ((1 - mx, my), (mx, 1 - my), (1 - mx, 1 - my))]


def _gather_start(bufs, after, name):
    n = len(bufs)

    def body(*refs):
        ins, s_sem, r_sem, token = refs[:n], refs[n + 1], refs[n + 2], refs[-1]
        mx, my, mc = _my_place()
        me = 4 * mx + 2 * my + mc
        for t in range(n):
            for dev in [(mx, my, 1 - mc)] + [d for d, _ in _chip_peers()]:
                pltpu.make_async_remote_copy(src_ref=ins[t].at[me], dst_ref=ins[t].at[me], send_sem=s_sem.at[t],
                                             recv_sem=r_sem.at[t], device_id=dev, device_id_type=MESH).start()
        token[...] = jnp.zeros_like(token)

    out_shape = ((pltpu.SemaphoreType.DMA((n,)), pltpu.SemaphoreType.DMA((n,))) + tuple(pltpu.HBM(b.shape, b.dtype) for b in bufs)
                 + (SDS((8, 128), f32),))
    out = _pcall(body, name=name, out_shape=out_shape, in_specs=(HBM_SPEC,) * n + (ANY_SPEC,),
                 out_specs=(SEM_SPEC, SEM_SPEC) + (HBM_SPEC,) * n + (pl.BlockSpec(memory_space=pltpu.VMEM),),
                 input_output_aliases={i: 2 + i for i in range(n)},
                 compiler_params=pltpu.CompilerParams(has_side_effects=DATAFLOW))(*_in_hbm(bufs), after)
    return (out[0], out[1]), list(out[2:2 + n]), out[-1]


def _gather_forward(bufs, sems, after, name, fresh=()):
    n, nf = len(bufs), len(fresh)
    both = list(bufs) + list(fresh)

    nsem = 4 if nf else 2

    def body(*refs):
        ins, news, s1, r1 = refs[:n], refs[n:n + nf], refs[n + nf], refs[n + nf + 1]
        s2, r2 = refs[n + nf + 3:n + nf + 5]
        sf, rf = refs[n + nf + 5:n + nf + 7] if nf else (None, None)
        token = refs[-1]
        mx, my, mc = _my_place()
        me, sibling = 4 * mx + 2 * my + mc, (mx, my, 1 - mc)
        for t in range(n):
            legs = ins[t].at[pl.ds(0, N_FIRST_LEG)]
            first = pltpu.make_async_remote_copy(src_ref=legs, dst_ref=legs, send_sem=s1.at[t], recv_sem=r1.at[t],
                                                 device_id=sibling, device_id_type=MESH)
            first.wait_send()
            first.wait_recv()
        for t in range(n):
            for _, idx in _chip_peers():
                pltpu.make_async_remote_copy(src_ref=ins[t].at[idx], dst_ref=ins[t].at[idx], send_sem=s2.at[t],
                                             recv_sem=r2.at[t], device_id=sibling, device_id_type=MESH).start()
        for t in range(nf):
            for dev in [sibling] + [d for d, _ in _chip_peers()]:
                pltpu.make_async_remote_copy(src_ref=news[t].at[me], dst_ref=news[t].at[me], send_sem=sf.at[t],
                                             recv_sem=rf.at[t], device_id=dev, device_id_type=MESH).start()
        token[...] = jnp.zeros_like(token)

    sem = lambda k: pltpu.SemaphoreType.DMA((k,))
    sem_shapes = (sem(n), sem(n)) + ((sem(nf), sem(nf)) if nf else ())
    out_shape = sem_shapes + tuple(pltpu.HBM(b.shape, b.dtype) for b in both) + (SDS((8, 128), f32),)
    out = _pcall(body, name=name, out_shape=out_shape, in_specs=(HBM_SPEC,) * (n + nf) + (SEM_SPEC, SEM_SPEC, ANY_SPEC),
                 out_specs=(SEM_SPEC,) * nsem + (HBM_SPEC,) * (n + nf) + (pl.BlockSpec(memory_space=pltpu.VMEM),),
                 input_output_aliases={i: nsem + i for i in range(n + nf)},
                 compiler_params=pltpu.CompilerParams(has_side_effects=DATAFLOW))(*bufs, *_in_hbm(fresh), sems[0], sems[1], after)
    fresh_sems = (out[2], out[3]) if nf else None
    return (out[0], out[1]), list(out[nsem:nsem + n]), fresh_sems, list(out[nsem + n:nsem + n + nf]), out[-1]


def _exchange_start(sends, lands, name):
    nt = len(sends)

    def body(*refs):
        srcs, dsts, s_sem, r_sem, token = refs[:nt], refs[nt:2 * nt], refs[2 * nt], refs[2 * nt + 1], refs[-1]
        mx, my, mc = _my_place()
        me = 4 * mx + 2 * my + mc
        for t in range(nt):
            for dev, idx in _peers():
                pltpu.make_async_remote_copy(src_ref=srcs[t].at[idx], dst_ref=dsts[t].at[me], send_sem=s_sem.at[t],
                                             recv_sem=r_sem.at[t], device_id=dev, device_id_type=MESH).start()
        token[...] = jnp.zeros_like(token)

    both = list(sends) + list(lands)
    out_shape = (pltpu.SemaphoreType.DMA((nt,)), pltpu.SemaphoreType.DMA((nt,))) + tuple(pltpu.HBM(b.shape, b.dtype) for b in both)
    out = _pcall(body, name=name, out_shape=out_shape + (SDS((8, 128), f32),), in_specs=(HBM_SPEC,) * (2 * nt),
                 out_specs=(SEM_SPEC, SEM_SPEC) + (HBM_SPEC,) * (2 * nt) + (pl.BlockSpec(memory_space=pltpu.VMEM),),
                 input_output_aliases={i: 2 + i for i in range(2 * nt)},
                 compiler_params=pltpu.CompilerParams(has_side_effects=DATAFLOW))(*_in_hbm(both))
    return (out[0], out[1]), list(out[2:2 + nt]), list(out[2 + nt:2 + 2 * nt]), out[-1]


def _copies_wait(srcs, dsts, sems, after, name, ncopies=N_DEV - 1):
    nt = len(srcs)
    same = srcs is dsts
    bufs = list(srcs) if same else list(srcs) + list(dsts)
    nb = len(bufs)

    def body(*refs):
        s_sem, r_sem = refs[nb], refs[nb + 1]
        mx, my, mc = _my_place()
        for t in range(nt):
            src = refs[t].at[pl.ds(0, ncopies)]
            dst = src if same else refs[nt + t].at[pl.ds(0, ncopies)]
            cp = pltpu.make_async_remote_copy(src_ref=src, dst_ref=dst, send_sem=s_sem.at[t], recv_sem=r_sem.at[t],
                                              device_id=(mx, my, 1 - mc), device_id_type=MESH)
            cp.wait_send()
            cp.wait_recv()

    out = _pcall(body, name=name, out_shape=tuple(pltpu.HBM(b.shape, b.dtype) for b in bufs),
                 in_specs=(HBM_SPEC,) * nb + (SEM_SPEC, SEM_SPEC, ANY_SPEC), out_specs=(HBM_SPEC,) * nb,
                 input_output_aliases={i: i for i in range(nb)},
                 compiler_params=pltpu.CompilerParams(has_side_effects=DATAFLOW))(*bufs, sems[0], sems[1], after)
    out = list(out)
    return (out, out) if same else (out[:nt], out[nt:])


def _cast_blocks(weights, l, me, after, name):
    nt = len(weights)
    halves = 2
    extra = [] if after is None else [after]

    def body(me_ref, *refs):
        outs = refs[nt + len(extra):]
        for t in range(nt):
            outs[t][...] = refs[t][...].astype(bf16)

    in_specs, out_specs, out_shape = [], [], []
    for wt in weights:
        _, r, c = wt.shape
        tr = r // halves
        in_specs.append(pl.BlockSpec((None, tr, c), lambda i, me_ref: (l, i, 0)))
        out_specs.append(pl.BlockSpec((None, tr, c), lambda i, me_ref: (me_ref[0], i, 0)))
        out_shape.append(SDS((N_DEV, r, c), bf16))
    in_specs += [pl.BlockSpec(memory_space=pl.ANY)] * len(extra)
    grid_spec = pltpu.PrefetchScalarGridSpec(num_scalar_prefetch=1, grid=(halves,), in_specs=in_specs, out_specs=out_specs)
    return list(_pcall(body, name=name, out_shape=out_shape, grid_spec=grid_spec, compiler_params=_cp(("parallel",)))(me, *weights, *extra))

def _layer_fwd_mix(x, p, bias, l, after=None):
    tag = f"l{l}"
    z = _mm(x, p["w_in"], "nt", 1024, 1408, f"{tag}_mm_in", after=after, norm_a=p["norm_mix_g"])
    yb, lt = _attn_fwd(z, bias, f"{tag}_attn_fwd")
    y = _mixer_fwd(z, yb, p["conv_a_w"], p["conv_c_w"], p["conv_c_b"], p["ln_c_g"], p["ln_c_b"], p["out_norm_g"], f"{tag}_mixer_fwd")
    return y, dict(x=x, z=z, yb=yb, lt=lt, y=y)


def _layer_fwd_ffn(x, y, p, l, after=None):
    tag = f"l{l}"
    x1 = _mm(y, p["w_out"], "nn", 512, 1024, f"{tag}_mm_out", res=x, after=after)
    upre = _mm(x1, p["w_up"], "nt", 1024, 1408, f"{tag}_mm_up", norm_a=p["norm_ffn_g"])
    act = _ffn_act_fwd(upre, p["conv_f_w"], f"{tag}_ffn_act")
    x2 = _mm(act, p["w_down"], "nn", 512, 1024, f"{tag}_mm_down", res=x1)
    return x2, dict(x1=x1, upre=upre, act=act)


def _small_stacks(depth, d_model, d_ff):
    shapes = dict(norm_ffn=(depth, 1, d_model), norm_mix=(depth, 1, d_model), conv_f_gate=(depth, 8, d_ff),
                  conv_f_val=(depth, 8, d_ff), mixers=(depth, PG_ROWS, 256), og_b=(depth, 1, 512))
    return {k: lax.empty(shp, f32) for k, shp in shapes.items()}


def _small_grads(st):
    pg = st["mixers"]
    row = lambda r: pg[:, r:r + 1, :]
    return dict(norm_ffn_g=st["norm_ffn"], norm_mix_g=st["norm_mix"],
                conv_f_w=jnp.concatenate([st["conv_f_gate"][:, :K_FFN], st["conv_f_val"][:, :K_FFN]], axis=2),
                conv_a_w=pg[:, PG_CONV_A:PG_CONV_A + K_SHORT], conv_c_w=pg[:, PG_CONV_C:PG_CONV_C + K_CONF],
                conv_c_b=row(PG_CB), ln_c_g=row(PG_LNG), ln_c_b=row(PG_LNB),
                out_norm_g=jnp.concatenate([row(PG_OG_A), st["og_b"], row(PG_OG_C)], axis=2))


def _layer_bwd_ffn(dx2, p, sv, l, st, after=None):
    tag = f"l{l}"
    g, st = {}, dict(st)
    dact = _mm(dx2, p["w_down"], "nt", 512, 1408, f"{tag}_mm_dact", after=after, b_outer=True)
    dwd = _mm(sv["act"], dx2, "tn", 1408, 1024, f"{tag}_mm_dwdown", out_dtype=bf16)
    g["w_down"] = dwd.reshape(N_DEV, dwd.shape[0] // N_DEV, dwd.shape[1])
    dug, duv, st["conv_f_gate"], st["conv_f_val"] = _ffn_act_bwd(sv["upre"], dact, p["conv_f_w"], (st["conv_f_gate"], st["conv_f_val"]),
                                                                 l, f"{tag}_ffn_act_bwd")
    dwu = _mm_tn_pieces([dug, duv], sv["x1"], p["norm_ffn_g"], 1408, f"{tag}_mm_dwup")
    g["w_up"] = dwu.reshape(N_DEV, dwu.shape[0] // N_DEV, dwu.shape[1])
    dx1, st["norm_ffn"] = _mm([dug, duv], p["w_up"], "nn", 512, 1024, f"{tag}_mm_dh2",
                              rms_bwd=(sv["x1"], p["norm_ffn_g"], dx2, st["norm_ffn"], l))
    return dx1, g, st


def _layer_bwd_mix(dx1, p, sv, bias, l, st, after=None):
    tag = f"l{l}"
    g, st = {}, dict(st)
    dy = _mm(dx1, p["w_out"], "nt", 512, 1024, f"{tag}_mm_dy", after=after)
    dwo = _mm(sv["y"], dx1, "tn", 512, 1024, f"{tag}_mm_dwout", out_dtype=bf16)
    g["w_out"] = dwo.reshape(N_DEV, dwo.shape[0] // N_DEV, dwo.shape[1])
    dz, dyb, st["mixers"], st["og_b"] = _mixer_bwd(sv["z"], sv["yb"], dy, p["conv_a_w"], p["conv_c_w"], p["conv_c_b"], p["ln_c_g"],
                                                   p["ln_c_b"], p["out_norm_g"], (st["mixers"], st["og_b"]), l, f"{tag}_mixer_bwd")
    dz, tabs = _attn_bwd(sv["z"], dyb, sv["yb"], sv["lt"], bias, dz, f"{tag}_attn_bwd")
    dwi = _mm(dz, sv["x"], "tn", 1408, 1024, f"{tag}_mm_dwin", out_dtype=bf16, norm_b=p["norm_mix_g"])
    g["w_in"] = dwi.reshape(N_DEV, dwi.shape[0] // N_DEV, dwi.shape[1])
    dx, st["norm_mix"] = _mm(dz, p["w_in"], "nn", 512, 1024, f"{tag}_mm_dh", rms_bwd=(sv["x"], p["norm_mix_g"], dx1, st["norm_mix"], l))
    return dx, g, tabs, st


SMALL_LAYER_PARAMS = ("norm_mix_g", "conv_a_w", "conv_c_w", "conv_c_b", "ln_c_g", "ln_c_b", "out_norm_g", "norm_ffn_g", "conv_f_w")


BIG = ("w_in", "w_out", "w_up", "w_down")
TRANSPOSED = ("w_in", "w_up")
ADAM_ROWS = {"w_in": 176, "w_out": 128, "w_up": 176, "w_down": 176}
SMALL_SHARDED = ("conv_a_w", "conv_c_w", "conv_f_w")
SMALL_REPLICATED = ("norm_mix_g", "conv_c_b", "ln_c_g", "ln_c_b", "out_norm_g", "norm_ffn_g", "rel_bias", "final_g")
WEIGHTS = ("norm_mix_g", "w_in", "conv_a_w", "conv_c_w", "conv_c_b", "ln_c_g", "ln_c_b", "out_norm_g", "w_out",
           "norm_ffn_g", "w_up", "conv_f_w", "w_down", "rel_bias", "final_g")


def _pack(arrays):
    flat = jnp.concatenate([a.reshape(-1) for a in arrays])
    rows = -(-flat.shape[0] // 1024) * 8
    return jnp.pad(flat, (0, rows * 128 - flat.shape[0])).reshape(rows, 128)


def _unpack(packed, shapes):
    flat, out, at = packed.reshape(-1), [], 0
    for shp in shapes:
        n = int(np.prod(shp))
        out.append(flat[at:at + n].reshape(shp))
        at += n
    return out


def _gather_shards(g, local_shape):
    nd = len(local_shape)
    return jnp.moveaxis(g, 0, nd - 1).reshape(local_shape[:-1] + (N_DEV * local_shape[-1],))


def kernel(x, norm_mix_g, w_in, conv_a_w, conv_c_w, conv_c_b, ln_c_g, ln_c_b, out_norm_g, w_out, norm_ffn_g, w_up, conv_f_w, w_down, rel_bias, final_g, loss_target, m_norm_mix_g, m_w_in, m_conv_a_w, m_conv_c_w, m_conv_c_b, m_ln_c_g, m_ln_c_b, m_out_norm_g, m_w_out, m_norm_ffn_g, m_w_up, m_conv_f_w, m_w_down, m_rel_bias, m_final_g, v_norm_mix_g, v_w_in, v_conv_a_w, v_conv_c_w, v_conv_c_b, v_ln_c_g, v_ln_c_b, v_out_norm_g, v_w_out, v_norm_ffn_g, v_w_up, v_conv_f_w, v_w_down, v_rel_bias, v_final_g):
    w = dict(norm_mix_g=norm_mix_g, w_in=w_in, conv_a_w=conv_a_w, conv_c_w=conv_c_w, conv_c_b=conv_c_b, ln_c_g=ln_c_g,
             ln_c_b=ln_c_b, out_norm_g=out_norm_g, w_out=w_out, norm_ffn_g=norm_ffn_g, w_up=w_up, conv_f_w=conv_f_w,
             w_down=w_down, rel_bias=rel_bias, final_g=final_g)
    mom = dict(norm_mix_g=m_norm_mix_g, w_in=m_w_in, conv_a_w=m_conv_a_w, conv_c_w=m_conv_c_w, conv_c_b=m_conv_c_b,
               ln_c_g=m_ln_c_g, ln_c_b=m_ln_c_b, out_norm_g=m_out_norm_g, w_out=m_w_out, norm_ffn_g=m_norm_ffn_g,
               w_up=m_w_up, conv_f_w=m_conv_f_w, w_down=m_w_down, rel_bias=m_rel_bias, final_g=m_final_g)
    var = dict(norm_mix_g=v_norm_mix_g, w_in=v_w_in, conv_a_w=v_conv_a_w, conv_c_w=v_conv_c_w, conv_c_b=v_conv_c_b,
               ln_c_g=v_ln_c_g, ln_c_b=v_ln_c_b, out_norm_g=v_out_norm_g, w_out=v_w_out, norm_ffn_g=v_norm_ffn_g,
               w_up=v_w_up, conv_f_w=v_conv_f_w, w_down=v_w_down, rel_bias=v_rel_bias, final_g=v_final_g)
    depth = w_in.shape[0]
    me = 4 * lax.axis_index("x") + 2 * lax.axis_index("y") + lax.axis_index("c")
    for k in TRANSPOSED:
        w[k], mom[k], var[k] = (jnp.transpose(a, (0, 2, 1)) for a in (w[k], mom[k], var[k]))

    me_arr = me.astype(jnp.int32).reshape(1)
    bufs = [_cast_blocks([w[k] for k in BIG], 0, me_arr, None, "l0_cast_blocks")]
    in_sems, in_bufs, token = _gather_start(bufs[0][:1], me_arr, "l0_gather_start_in")
    bufs += [_cast_blocks([w[k] for k in BIG], l, me_arr, token, f"l{l}_cast_blocks") for l in range(1, depth)]
    params = {}
    gathered = _all_gather([w[k] for k in SMALL_SHARDED], "gather_small", after=token)
    gs = gathered[0]
    for k, g8 in zip(SMALL_SHARDED, gathered):
        params[k] = _gather_shards(g8, w[k].shape)
    for k in ("norm_mix_g", "conv_c_b", "ln_c_g", "ln_c_b", "out_norm_g", "norm_ffn_g"):
        params[k] = w[k][:, None, :]
    params["rel_bias"] = rel_bias
    params["final_g"] = final_g[None, :]

    def full(k, buf, l):
        return buf.reshape(N_DEV * buf.shape[1], buf.shape[2])

    bias = _bias_tables(rel_bias, "bias_tables").reshape(len(DILATIONS), 2, 4, 2 * BLK, 2 * BLK)
    in_sems, in_bufs, rest_sems, rest_bufs, token = _gather_forward(in_bufs, in_sems, gs, "l0_gather_forward_in", fresh=bufs[0][1:])
    xl, after = x[0], token
    layer_params, saved = [], []
    for l in range(depth):
        _, in_bufs = _copies_wait(in_bufs, in_bufs, in_sems, after, f"l{l}_gather_wait_in", N_SECOND_LEG)
        p = {k: params[k][l] for k in SMALL_LAYER_PARAMS}
        p["w_in"] = full("w_in", in_bufs[0], l)
        token = in_bufs[0]
        nxt_in = bufs[l + 1][:1] if l + 1 < depth else []
        if l > 0:
            rest_sems, rest_bufs, in_sems, in_bufs, token = _gather_forward(rest_bufs, rest_sems, token, f"l{l}_gather_forward_rest", fresh=nxt_in)
        else:
            in_sems, in_bufs, token = _gather_start(nxt_in, token, "l1_gather_start_in")
        y, sv = _layer_fwd_mix(xl, p, bias, l, after=token)
        if l == 0:
            rest_sems, rest_bufs, _, _, _ = _gather_forward(rest_bufs, rest_sems, y, "l0_gather_forward_rest")
        _, rest_bufs = _copies_wait(rest_bufs, rest_bufs, rest_sems, y, f"l{l}_gather_wait_rest", N_SECOND_LEG)
        for k, buf in zip(BIG[1:], rest_bufs):
            p[k] = full(k, buf, l)
        token = rest_bufs[0]
        if l + 1 < depth:
            in_sems, in_bufs, rest_sems, rest_bufs, token = _gather_forward(in_bufs, in_sems, token, f"l{l + 1}_gather_forward_in",
                                                                             fresh=bufs[l + 1][1:])
        xl, sv2 = _layer_fwd_ffn(xl, y, p, l, after=token)
        layer_params.append(p)
        saved.append({**sv, **sv2})
        after = xl
    loss, dx, d_final = _final_loss(xl, params["final_g"], loss_target[0], "final_loss")
    loss = lax.psum(loss[0, 0], ("x", "y", "c"))

    tabs, pending = [None] * depth, []
    after = None
    stacks = _small_stacks(depth, x.shape[-1], w_down.shape[1] * N_DEV)

    def exchange(l, part, names, g):
        sends = [g[k] for k in names]
        sems, sends, lands, token = _exchange_start(sends, [lax.empty(s.shape, s.dtype) for s in sends], f"l{l}_exchange_start_{part}")
        pending.append((l, part, names, sems, sends, lands))
        return token

    for l in reversed(range(depth)):
        dx1, g_ffn, stacks = _layer_bwd_ffn(dx, layer_params[l], saved[l], l, stacks, after=after)
        after = exchange(l, "ffn", ("w_up", "w_down"), g_ffn)
        dx, g_mix, tabs[l], stacks = _layer_bwd_mix(dx1, layer_params[l], saved[l], bias, l, stacks, after=after)
        after = exchange(l, "mix", ("w_out", "w_in"), g_mix)
    grads_small = _small_grads(stacks)
    grads_small["rel_bias"] = _bias_grad(jnp.stack(tabs).reshape(depth, len(DILATIONS), 8, BLK, 2 * BLK), "bias_grad")[:, :8]
    grads_small["final_g"] = d_final

    out_g, out_d, out_m, out_v = {}, {}, {}, {}
    grads = grads_small
    small = SMALL_SHARDED + SMALL_REPLICATED
    full_shapes = [params[k].shape if k in SMALL_SHARDED else w[k].shape for k in small]
    packed = _pack([grads[k].reshape(shp) for k, shp in zip(small, full_shapes)])
    gsum = _sum_parts(_all_gather([packed], "gather_small_grads", after=after)[0], "sum_small_grads")
    gfull = dict(zip(small, _unpack(gsum, full_shapes)))
    for k in SMALL_SHARDED:
        shp = w[k].shape
        gfull[k] = lax.dynamic_index_in_dim(gfull[k].reshape(shp[:-1] + (N_DEV, shp[-1])), me, axis=len(shp) - 1, keepdims=False)
    two_d = lambda a: a.reshape(1, -1) if a.ndim == 1 else a
    res = _adamw_small(*[[two_d(src[k]) for k in small] for src in (gfull, w, mom, var)], "adamw_small")
    for name_map, outs in zip((out_d, out_m, out_v), res):
        name_map.update({k: o.reshape(w[k].shape) for k, o in zip(small, outs)})
    out_g.update(gfull)

    results = {k: [lax.empty(w[k].shape, f32) for _ in range(4)] for k in BIG}
    after = res[0][0]
    for l, part, names, sems, sends, lands in pending:
        sends, lands = _copies_wait(sends, lands, sems, after, f"l{l}_exchange_wait_{part}")
        for k, send, land in zip(names, sends, lands):
            results[k] = _adamw_layer(land, send, w[k], mom[k], var[k], results[k], l, ADAM_ROWS[k], me_arr, f"l{l}_adamw_{k}")
            after = results[k][0]
    for k in BIG:
        big = [jnp.transpose(a, (0, 2, 1)) for a in results[k]] if k in TRANSPOSED else results[k]
        out_g[k], out_d[k], out_m[k], out_v[k] = big

    return (loss, dx[None], *[out_g[k] for k in WEIGHTS], *[out_d[k] for k in WEIGHTS], *[out_m[k] for k in WEIGHTS],
            *[out_v[k] for k in WEIGHTS])
```

```python
import math

import numpy as np
import jax
import jax.numpy as jnp
from jax import lax
from jax.experimental import pallas as pl
from jax.experimental.pallas import tpu as pltpu

f32, bf16 = jnp.float32, jnp.bfloat16
SDS = jax.ShapeDtypeStruct
MESH = pl.DeviceIdType.MESH

N_DEV = 8
EPS = 1e-6
NEG = -1e30
D_HEAD = 64
BLK = 128
N_BUCKETS = 32
DILATIONS = (1, 4, 16)
N_KEYS = 128
K_SHORT, K_CONF, K_FFN = 3, 31, 3
ROW_TILE = 512
MIXER_ROWS = 512
HALO = 32
HALO_FFN = 8
FFN_FWD_ROWS, FFN_BWD_ROWS = 1024, 1024
FFN_CHUNK = 64
VMEM_LIMIT = 56 << 20
ADAM_LR, ADAM_B1, ADAM_B2, ADAM_EPS, ADAM_WD, ADAM_STEP = 0.001, 0.9, 0.999, 1e-08, 0.01, 10


def _pcall(body, **kw):
    return pl.pallas_call(body, **kw)


def _cp(sem=None):
    kw = dict(vmem_limit_bytes=VMEM_LIMIT)
    if sem is not None:
        kw["dimension_semantics"] = sem
    return pltpu.CompilerParams(**kw)


def _t5_buckets():
    rel = np.arange(BLK)[:, None] - np.arange(2 * BLK)[None, :] + BLK
    out = []
    for d in DILATIONS:
        dist = np.maximum(rel, 0) * d
        max_exact = N_BUCKETS // 2
        d_f = np.maximum(dist, 1).astype(np.float32)
        large = max_exact + (np.log(d_f / np.float32(max_exact)) / np.float32(math.log(2048 / max_exact))
                             * np.float32(N_BUCKETS - max_exact)).astype(np.int32)
        large = np.minimum(large, N_BUCKETS - 1)
        out.append(np.where(dist < max_exact, dist, large).astype(np.int32))
    return np.stack(out)


_BUCKETS = _t5_buckets()


def _mm(a, b, mode, tr, tc, name, res=None, out_dtype=f32, after=None, b_outer=False, norm_a=None, norm_b=None, rms_bwd=None):
    if rms_bwd is not None:
        return _mm_rms_bwd(a, b, tr, name, *rms_bwd)
    ij = (lambda f: (lambda j, i: f(i, j))) if b_outer else (lambda f: f)
    if mode == "nn":
        (m, k), n = a.shape, b.shape[1]
        a_spec = pl.BlockSpec((tr, k), ij(lambda i, j: (i, 0)))
        b_spec = pl.BlockSpec((k, tc), ij(lambda i, j: (0, j)))
        dims = (((1,), (0,)), ((), ()))
    elif mode == "nt":
        (m, k), n = a.shape, b.shape[0]
        a_spec = pl.BlockSpec((tr, k), ij(lambda i, j: (i, 0)))
        b_spec = pl.BlockSpec((tc, k), ij(lambda i, j: (j, 0)))
        dims = (((1,), (1,)), ((), ()))
    else:
        (k, m), n = a.shape, b.shape[1]
        a_spec = pl.BlockSpec((k, tr), ij(lambda i, j: (0, i)))
        b_spec = pl.BlockSpec((k, tc), ij(lambda i, j: (0, j)))
        dims = (((0,), (0,)), ((), ()))
    assert m % tr == 0 and n % tc == 0, (name, m, n, tr, tc)
    o_spec = pl.BlockSpec((tr, tc), ij(lambda i, j: (i, j)))
    grid = (n // tc, m // tr) if b_outer else (m // tr, n // tc)

    gain = norm_a if norm_a is not None else norm_b
    assert norm_b is None or (mode == "tn" and tc == n and not b_outer)
    assert norm_a is None or (mode != "tn" and not b_outer)

    def body(a_ref, b_ref, *rest):
        o_ref = rest[-2] if gain is not None else rest[-1]
        if norm_a is not None:
            ha_sc = rest[-1]

            @pl.when(pl.program_id(1) == 0)
            def _():
                av = a_ref[...]
                ha_sc[...] = (av * _rstd(av) * rest[0][...]).astype(bf16)

            av = ha_sc[...]
        else:
            av = a_ref[...].astype(bf16)
        if norm_b is not None:
            hb_sc = rest[-1]

            @pl.when(pl.program_id(0) == 0)
            def _():
                bv = b_ref[...]
                hb_sc[...] = (bv * _rstd(bv) * rest[0][...]).astype(bf16)

            bv = hb_sc[...]
        else:
            bv = b_ref[...].astype(bf16)
        acc = lax.dot_general(av, bv, dims, preferred_element_type=f32)
        if res is not None:
            acc = acc + rest[1 if gain is not None else 0][...]
        o_ref[...] = acc.astype(out_dtype)

    vec = pl.BlockSpec((1, k if norm_a is not None else n), lambda i, j: (0, 0))
    ins = [a, b] + ([gain] if gain is not None else []) + ([res] if res is not None else []) + ([after] if after is not None else [])
    specs = ([a_spec, b_spec] + ([vec] if gain is not None else []) + ([o_spec] if res is not None else [])
             + ([pl.BlockSpec(memory_space=pl.ANY)] if after is not None else []))
    scratch = [pltpu.VMEM((k, n), bf16)] if norm_b is not None else [pltpu.VMEM((tr, k), bf16)] if norm_a is not None else []
    sem = ("arbitrary", "arbitrary") if gain is not None else ("parallel", "parallel")
    return _pcall(body, name=name, out_shape=SDS((m, n), out_dtype), grid=grid, in_specs=specs, out_specs=o_spec,
                  scratch_shapes=scratch, compiler_params=_cp(sem))(*ins)


def _mm_tn_pieces(pieces, b, gain, tr, name):
    k, n = b.shape
    widths = [pc.shape[1] for pc in pieces]
    assert all(wd % tr == 0 for wd in widths)
    starts = [sum(widths[:p]) // tr for p in range(len(pieces))]
    counts = [wd // tr for wd in widths]
    npc = len(pieces)

    def body(*refs):
        a_refs, b_ref, g_ref, o_ref, hb_sc = refs[:npc], refs[npc], refs[npc + 1], refs[npc + 2], refs[npc + 3]
        i = pl.program_id(0)

        @pl.when(i == 0)
        def _():
            bv = b_ref[...]
            hb_sc[...] = (bv * _rstd(bv) * g_ref[...]).astype(bf16)

        for p in range(npc):
            @pl.when(jnp.logical_and(i >= starts[p], i < starts[p] + counts[p]))
            def _(p=p):
                o_ref[...] = lax.dot_general(a_refs[p][...].astype(bf16), hb_sc[...], (((0,), (0,)), ((), ())),
                                             preferred_element_type=f32).astype(bf16)

    a_specs = [pl.BlockSpec((k, tr), lambda i, p=p: (0, jnp.clip(i - starts[p], 0, counts[p] - 1))) for p in range(npc)]
    return _pcall(body, name=name, out_shape=SDS((sum(widths), n), bf16), grid=(sum(counts),),
                  in_specs=a_specs + [pl.BlockSpec((k, n), lambda i: (0, 0)), pl.BlockSpec((1, n), lambda i: (0, 0))],
                  out_specs=pl.BlockSpec((tr, n), lambda i: (i, 0)), scratch_shapes=[pltpu.VMEM((k, n), bf16)],
                  compiler_params=_cp(("arbitrary",)))(*pieces, b, gain)


def _mm_rms_bwd(a, b, tr, name, x, g, dres, stack, l):
    pieces = list(a) if isinstance(a, (list, tuple)) else [a]
    m, n, npc = pieces[0].shape[0], b.shape[1], len(pieces)
    k = sum(pc.shape[1] for pc in pieces)

    def body(*refs):
        a_refs, (b_ref, x_ref, g_ref, dres_ref, stack_ref, dx_ref, dg_ref) = refs[:npc], refs[npc:]
        dh, at = 0.0, 0
        for a_ref in a_refs:
            kp = a_ref.shape[1]
            dh = dh + jnp.dot(a_ref[...].astype(bf16), b_ref[pl.ds(at, kp), :].astype(bf16), preferred_element_type=f32)
            at += kp
        xv = x_ref[...]
        r = _rstd(xv)
        dx_ref[...] = dres_ref[...] + _rms_dx(xv, r, g_ref[...], dh)

        @pl.when(pl.program_id(0) == 0)
        def _():
            dg_ref[...] = jnp.zeros_like(dg_ref)

        dg_ref[...] += _colsum(dh * xv * r)

    row = pl.BlockSpec((tr, n), lambda i: (i, 0))
    vec = pl.BlockSpec((1, n), lambda i: (0, 0))
    a_specs = [pl.BlockSpec((tr, pc.shape[1]), lambda i: (i, 0)) for pc in pieces]
    return _pcall(body, name=name, out_shape=(SDS((m, n), f32), SDS(stack.shape, f32)), grid=(m // tr,),
                  in_specs=a_specs + [pl.BlockSpec((k, n), lambda i: (0, 0), pipeline_mode=pl.Buffered(1)),
                                      row, vec, row, pl.BlockSpec(memory_space=pl.ANY)],
                  out_specs=(row, pl.BlockSpec((None, 1, n), lambda i: (l, 0, 0))), input_output_aliases={npc + 4: 1},
                  compiler_params=_cp(("arbitrary",)))(*pieces, b, x, g, dres, stack)


def _rstd(x):
    return lax.rsqrt(jnp.mean(x * x, axis=-1, keepdims=True) + EPS)


def _rms_dx(x, r, g, dy):
    dxh = dy * g
    return r * (dxh - x * (r * r) * jnp.mean(dxh * x, axis=-1, keepdims=True))


def _sigmoid(x):
    return 0.5 * jnp.tanh(0.5 * x) + 0.5


def _colsum(x):
    return jnp.sum(x, axis=0, keepdims=True)


def _final_loss(x, g, target, name):
    s, d = x.shape

    def body(x_ref, g_ref, t_ref, loss_ref, dx_ref, dg_ref):
        xv, gv = x_ref[...], g_ref[...]
        r = _rstd(xv)
        diff = xv * r * gv - t_ref[...]
        dy = diff * (1.0 / d)

        @pl.when(pl.program_id(0) == 0)
        def _():
            dg_ref[...] = jnp.zeros_like(dg_ref)
            loss_ref[...] = jnp.zeros_like(loss_ref)

        part = jnp.sum(jnp.sum(diff * diff, axis=1, keepdims=True), axis=0, keepdims=True) * (0.5 / d)
        loss_ref[...] += part
        dx_ref[...] = _rms_dx(xv, r, gv, dy)
        dg_ref[...] += _colsum(dy * xv * r)

    row = pl.BlockSpec((ROW_TILE, d), lambda i: (i, 0))
    vec = pl.BlockSpec((1, d), lambda i: (0, 0))
    one = pl.BlockSpec((8, 128), lambda i: (0, 0))
    return _pcall(body, name=name, out_shape=(SDS((8, 128), f32), SDS((s, d), f32), SDS((1, d), f32)),
                  grid=(s // ROW_TILE,), in_specs=[row, vec, row], out_specs=(one, row, vec),
                  compiler_params=_cp(("arbitrary",)))(x, g, target)


Z_AH, Z_AB, Z_AC, Z_CV, Z_CG = 0, 1, 2, 9, 10
Z_Q128, Z_K128, Z_V128 = 6, 10, 14


def _cur(t, w, col):
    return pl.BlockSpec((t, w), lambda i: (i, col))


def _prev(hb, t, w, col):
    return pl.BlockSpec((hb, w), lambda i: (jnp.maximum(i * (t // hb) - 1, 0), col))


def _next(hb, t, w, col, rows):
    return pl.BlockSpec((hb, w), lambda i: (jnp.minimum((i + 1) * (t // hb), rows // hb - 1), col))


def _layernorm_parts(u, g, b):
    mu = jnp.mean(u, axis=-1, keepdims=True)
    var = jnp.mean(jnp.square(u - mu), axis=-1, keepdims=True)
    rs = lax.rsqrt(var + EPS)
    xhat = (u - mu) * rs
    return xhat, rs, xhat * g + b


def _fill_shifted(rot):
    n = rot.shape[1]
    for r in range(1, 8):
        rot[r, pl.ds(0, n - 8), :] = rot[0, pl.ds(r, n - 8), :]


def _window(rot, offset, rows):
    return rot[offset % 8, pl.ds(offset - offset % 8, rows), :]


def _mixer_fwd(z, yb, conv_a, conv_c, cb, lng, lnb, og, name):
    s = z.shape[0]
    t, hb, w = MIXER_ROWS, HALO, 256

    def body(ah, ab, ac, cv, cg, ah_p, ac_p, cv_p, cg_p, yb_ref, wa, wc, cb_ref, lng_ref, lnb_ref, og_ref, y_ref, p_sc, u_sc):
        first = pl.program_id(0) == 0

        def prev(ref):
            return jnp.where(first, 0.0, ref[...])

        p_sc[pl.ds(0, hb), :] = prev(ah_p) * prev(ac_p)
        p_sc[pl.ds(hb, t), :] = ah[...] * ac[...]
        conv = jnp.zeros((t, w), f32)
        for k in range(K_SHORT):
            conv = conv + wa[pl.ds(k, 1), :] * p_sc[pl.ds(hb - (K_SHORT - 1) + k, t), :]
        y_a = ab[...] * conv
        u_sc[0, pl.ds(0, hb), :] = prev(cv_p) * _sigmoid(prev(cg_p))
        u_sc[0, pl.ds(hb, t), :] = cv[...] * _sigmoid(cg[...])
        _fill_shifted(u_sc)
        u = jnp.zeros((t, w), f32) + cb_ref[...]
        for k in range(K_CONF):
            u = u + wc[pl.ds(k, 1), :] * _window(u_sc, hb - (K_CONF - 1) + k, t)
        _, _, lnv = _layernorm_parts(u, lng_ref[...], lnb_ref[...])
        y_c = lnv * _sigmoid(lnv)
        ybv = yb_ref[...]
        y_ref[:, 0:256] = (y_a * _rstd(y_a) * og_ref[:, 0:256]).astype(bf16)
        y_ref[:, 256:768] = (ybv * _rstd(ybv) * og_ref[:, 256:768]).astype(bf16)
        y_ref[:, 768:1024] = (y_c * _rstd(y_c) * og_ref[:, 768:1024]).astype(bf16)

    full = lambda shape: pl.BlockSpec(shape, lambda i: tuple(0 for _ in shape))
    in_specs = [_cur(t, w, c) for c in (Z_AH, Z_AB, Z_AC, Z_CV, Z_CG)] + [_prev(hb, t, w, c) for c in (Z_AH, Z_AC, Z_CV, Z_CG)]
    in_specs += [_cur(t, 512, 0), full((K_SHORT, w)), full((K_CONF, w)), full((1, w)), full((1, w)), full((1, w)), full((1, 1024))]
    return _pcall(body, name=name, out_shape=SDS((s, 1024), bf16), grid=(s // t,), in_specs=in_specs,
                  out_specs=_cur(t, 1024, 0), scratch_shapes=[pltpu.VMEM((hb + t, w), f32), pltpu.VMEM((8, hb + t, w), f32)],
                  compiler_params=_cp(("parallel",)))(z, z, z, z, z, z, z, z, z, yb, conv_a, conv_c, cb, lng, lnb, og)


PG_CONV_A, PG_CONV_C, PG_CB, PG_LNG, PG_LNB, PG_OG_A, PG_OG_C, PG_ROWS = 0, 8, 40, 41, 42, 43, 44, 48


def _mixer_bwd(z, yb, dy, conv_a, conv_c, cb, lng, lnb, og, stacks, l, name):
    s = z.shape[0]
    t, hb, w = MIXER_ROWS, HALO, 256
    e = t + hb

    def body(ah, ab, ac, cv, cg, ah_p, ac_p, cv_p, cg_p, ah_n, ab_n, ac_n, cv_n, cg_n,
             dya, dyb1, dyb2, dyc, dya_n, dyc_n, yb_ref, wa, wc, cb_ref, lng_ref, lnb_ref, og_ref, pg_stack, dgb_stack,
             dz_ref, dyb_ref, pg_ref, dgb_ref, p_sc, u_sc, dc_sc, du_sc):
        i = pl.program_id(0)
        first, last = i == 0, i == pl.num_programs(0) - 1

        def prev(ref):
            return jnp.where(first, 0.0, ref[...])

        def nxt(ref):
            return jnp.where(last, 0.0, ref[...])

        def ext(cur_ref, next_ref):
            return jnp.concatenate([cur_ref[...], nxt(next_ref)], axis=0)

        @pl.when(first)
        def _():
            pg_ref[...] = jnp.zeros_like(pg_ref)
            dgb_ref[...] = jnp.zeros_like(dgb_ref)

        g_a = og_ref[:, 0:256]
        p_sc[pl.ds(0, hb), :] = prev(ah_p) * prev(ac_p)
        p_sc[pl.ds(hb, t), :] = ah[...] * ac[...]
        p_sc[pl.ds(hb + t, hb), :] = nxt(ah_n) * nxt(ac_n)
        conv = jnp.zeros((e, w), f32)
        for k in range(K_SHORT):
            conv = conv + wa[pl.ds(k, 1), :] * p_sc[pl.ds(hb - (K_SHORT - 1) + k, e), :]
        ab_e, dya_e = ext(ab, ab_n), ext(dya, dya_n)
        y_a = ab_e * conv
        r_a = _rstd(y_a)
        dy_a = _rms_dx(y_a, r_a, g_a, dya_e)
        dc_sc[...] = dy_a * ab_e
        dz_ref[:, 256:512] = (dy_a[:t] * conv[:t]).astype(bf16)
        dp = jnp.zeros((t, w), f32)
        for k in range(K_SHORT):
            dp = dp + wa[pl.ds(k, 1), :] * dc_sc[pl.ds(K_SHORT - 1 - k, t), :]
            pg_ref[pl.ds(PG_CONV_A + k, 1), :] += _colsum(dc_sc[pl.ds(0, t), :] * p_sc[pl.ds(hb - (K_SHORT - 1) + k, t), :])
        dz_ref[:, 0:256] = (dp * ac[...]).astype(bf16)
        dz_ref[:, 512:768] = (dp * ah[...]).astype(bf16)
        pg_ref[pl.ds(PG_OG_A, 1), :] += _colsum(dya[...] * y_a[:t] * r_a[:t])

        g_c = og_ref[:, 768:1024]
        u_sc[0, pl.ds(0, hb), :] = prev(cv_p) * _sigmoid(prev(cg_p))
        u_sc[0, pl.ds(hb, t), :] = cv[...] * _sigmoid(cg[...])
        u_sc[0, pl.ds(hb + t, hb), :] = nxt(cv_n) * _sigmoid(nxt(cg_n))
        _fill_shifted(u_sc)
        u = jnp.zeros((e, w), f32) + cb_ref[...]
        for k in range(K_CONF):
            u = u + wc[pl.ds(k, 1), :] * _window(u_sc, hb - (K_CONF - 1) + k, e)
        xhat, rs, lnv = _layernorm_parts(u, lng_ref[...], lnb_ref[...])
        sg = _sigmoid(lnv)
        y_c = lnv * sg
        r_c = _rstd(y_c)
        dyc_e = ext(dyc, dyc_n)
        dlnv = _rms_dx(y_c, r_c, g_c, dyc_e) * (sg * (1.0 + lnv * (1.0 - sg)))
        dxh = dlnv * lng_ref[...]
        du = rs * (dxh - jnp.mean(dxh, axis=-1, keepdims=True) - xhat * jnp.mean(dxh * xhat, axis=-1, keepdims=True))
        du_sc[0] = du
        _fill_shifted(du_sc)
        duin = jnp.zeros((t, w), f32)
        for k in range(K_CONF):
            duin = duin + wc[pl.ds(k, 1), :] * _window(du_sc, K_CONF - 1 - k, t)
            pg_ref[pl.ds(PG_CONV_C + k, 1), :] += _colsum(du[:t] * _window(u_sc, hb - (K_CONF - 1) + k, t))
        sgg = _sigmoid(cg[...])
        dz_ref[:, 2304:2560] = (duin * sgg).astype(bf16)
        dz_ref[:, 2560:2816] = (duin * cv[...] * sgg * (1.0 - sgg)).astype(bf16)
        pg_ref[pl.ds(PG_CB, 1), :] += _colsum(du[:t])
        pg_ref[pl.ds(PG_LNG, 1), :] += _colsum(dlnv[:t] * xhat[:t])
        pg_ref[pl.ds(PG_LNB, 1), :] += _colsum(dlnv[:t])
        pg_ref[pl.ds(PG_OG_C, 1), :] += _colsum(dyc[...] * y_c[:t] * r_c[:t])

        g_b = og_ref[:, 256:768]
        ybv = yb_ref[...]
        r_b = _rstd(ybv)
        dyb = jnp.concatenate([dyb1[...], dyb2[...]], axis=1)
        dyb_ref[...] = _rms_dx(ybv, r_b, g_b, dyb)
        dgb_ref[...] += _colsum(dyb * ybv * r_b)

    full = lambda shape: pl.BlockSpec(shape, lambda i: tuple(0 for _ in shape))
    zc = (Z_AH, Z_AB, Z_AC, Z_CV, Z_CG)
    in_specs = [_cur(t, w, c) for c in zc] + [_prev(hb, t, w, c) for c in (Z_AH, Z_AC, Z_CV, Z_CG)]
    in_specs += [_next(hb, t, w, c, s) for c in zc]
    in_specs += [_cur(t, w, c) for c in (0, 1, 2, 3)] + [_next(hb, t, w, 0, s), _next(hb, t, w, 3, s)]
    in_specs += [_cur(t, 512, 0), full((K_SHORT, w)), full((K_CONF, w)), full((1, w)), full((1, w)), full((1, w)), full((1, 1024))]
    in_specs += [pl.BlockSpec(memory_space=pl.ANY)] * 2
    out_shape = (SDS((s, z.shape[1]), bf16), SDS((s, 512), f32), SDS(stacks[0].shape, f32), SDS(stacks[1].shape, f32))
    out_specs = (_cur(t, z.shape[1], 0), _cur(t, 512, 0), pl.BlockSpec((None, PG_ROWS, w), lambda i: (l, 0, 0)),
                 pl.BlockSpec((None, 1, 512), lambda i: (l, 0, 0)))
    scratch = [pltpu.VMEM((hb + t + hb, w), f32), pltpu.VMEM((8, hb + t + hb, w), f32), pltpu.VMEM((e, w), f32), pltpu.VMEM((8, e, w), f32)]
    args = [z] * 14 + [dy] * 6 + [yb, conv_a, conv_c, cb, lng, lnb, og] + list(stacks)
    return _pcall(body, name=name, out_shape=out_shape, grid=(s // t,), in_specs=in_specs, out_specs=out_specs,
                  input_output_aliases={27: 2, 28: 3}, scratch_shapes=scratch, compiler_params=_cp(("arbitrary",)))(*args)


def _ffn_act_fwd(upre, conv_f, name):
    s, f2 = upre.shape
    t, hb, w = FFN_FWD_ROWS, HALO_FFN, 256
    nj = f2 // 2 // w

    def body(g_c, v_c, g_p, v_p, wg, wv, o_ref):
        first = pl.program_id(1) == 0
        wgs = [wg[pl.ds(k, 1), :] for k in range(K_FFN)]
        wvs = [wv[pl.ds(k, 1), :] for k in range(K_FFN)]
        rows = FFN_CHUNK
        first_tap = hb - (K_FFN - 1)

        def emit(gb, vb, r0):
            ug = sum(wgs[k] * gb[first_tap + k:first_tap + k + rows] for k in range(K_FFN))
            uv = sum(wvs[k] * vb[first_tap + k:first_tap + k + rows] for k in range(K_FFN))
            o_ref[pl.ds(r0, rows), :] = (ug * _sigmoid(ug) * uv).astype(bf16)

        emit(jnp.concatenate([jnp.where(first, 0.0, g_p[...]), g_c[pl.ds(0, rows), :]], axis=0),
             jnp.concatenate([jnp.where(first, 0.0, v_p[...]), v_c[pl.ds(0, rows), :]], axis=0), 0)

        def chunk(c, carry):
            r0 = pl.multiple_of(c * rows, rows)
            back = pl.ds(pl.multiple_of(r0 - hb, hb), rows + hb)
            emit(g_c[back, :], v_c[back, :], r0)
            return carry

        lax.fori_loop(1, t // rows, chunk, 0, unroll=3)

    cur = lambda off: pl.BlockSpec((t, w), lambda j, i: (i, off + j))
    prv = lambda off: pl.BlockSpec((hb, w), lambda j, i: (jnp.maximum(i * (t // hb) - 1, 0), off + j))
    wsp = lambda off: pl.BlockSpec((K_FFN, w), lambda j, i: (0, off + j))
    return _pcall(body, name=name, out_shape=SDS((s, f2 // 2), bf16), grid=(nj, s // t),
                  in_specs=[cur(0), cur(nj), prv(0), prv(nj), wsp(0), wsp(nj)],
                  out_specs=pl.BlockSpec((t, w), lambda j, i: (i, j)),
                  compiler_params=_cp(("parallel", "parallel")))(upre, upre, upre, upre, conv_f, conv_f)


def _ffn_act_bwd(upre, dact, conv_f, stacks, l, name):
    s, f2 = upre.shape
    t, hb, w = FFN_BWD_ROWS, HALO_FFN, 256
    nj = f2 // 2 // w

    def body(g_c, v_c, g_p, v_p, g_n, v_n, da_c, da_n, wg, wv, sg_ref, sv_ref, dg_ref, dv_ref, dwg_ref, dwv_ref):
        i = pl.program_id(1)
        first, last = i == 0, i == pl.num_programs(1) - 1

        @pl.when(first)
        def _():
            dwg_ref[...] = jnp.zeros_like(dwg_ref)
            dwv_ref[...] = jnp.zeros_like(dwv_ref)

        wgs = [wg[pl.ds(k, 1), :] for k in range(K_FFN)]
        wvs = [wv[pl.ds(k, 1), :] for k in range(K_FFN)]
        rows = FFN_CHUNK
        n = rows + hb
        first_tap = hb - (K_FFN - 1)
        nchunks = t // rows

        def groups_of_8(x):
            return sum(x[8 * q:8 * q + 8] for q in range(rows // 8))

        def emit(gb, vb, da, r0, carry):
            g_taps = [gb[first_tap + k:first_tap + k + n] for k in range(K_FFN)]
            v_taps = [vb[first_tap + k:first_tap + k + n] for k in range(K_FFN)]
            ug = sum(wgs[k] * g_taps[k] for k in range(K_FFN))
            uv = sum(wvs[k] * v_taps[k] for k in range(K_FFN))
            sg = _sigmoid(ug)
            dv = da * (ug * sg)
            dg = da * uv * (sg * (1.0 + ug * (1.0 - sg)))
            new = []
            for taps, dval, ws, dref in ((g_taps, dg, wgs, dg_ref), (v_taps, dv, wvs, dv_ref)):
                acc = sum(ws[k] * dval[K_FFN - 1 - k:K_FFN - 1 - k + rows] for k in range(K_FFN))
                dref[pl.ds(r0, rows), :] = acc.astype(bf16)
                new += [groups_of_8(dval[:rows] * taps[k][:rows]) for k in range(K_FFN)]
            return tuple(a + b for a, b in zip(carry, new))

        def chunk(c, carry):
            r0 = pl.multiple_of(c * rows, rows)
            around = pl.ds(pl.multiple_of(r0 - hb, hb), n + hb)
            return emit(g_c[around, :], v_c[around, :], da_c[pl.ds(r0, n), :], r0, carry)

        def edge(prev_ref, cur_ref, lo):
            return jnp.concatenate([jnp.where(first, 0.0, prev_ref[...]), cur_ref[pl.ds(0, lo), :]], axis=0)

        def tail(cur_ref, next_ref, lo):
            return jnp.concatenate([cur_ref[pl.ds(t - lo, lo), :], jnp.where(last, 0.0, next_ref[...])], axis=0)

        zero = jnp.zeros((8, w), f32)
        sums = emit(edge(g_p, g_c, n), edge(v_p, v_c, n), da_c[pl.ds(0, n), :], 0, (zero,) * (2 * K_FFN))
        sums = lax.fori_loop(1, nchunks - 1, chunk, sums, unroll=2)
        sums = emit(tail(g_c, g_n, rows + hb), tail(v_c, v_n, rows + hb), tail(da_c, da_n, rows), t - rows, sums)
        for k in range(K_FFN):
            dwg_ref[pl.ds(k, 1), :] += _colsum(sums[k])
            dwv_ref[pl.ds(k, 1), :] += _colsum(sums[K_FFN + k])

    cur = lambda off: pl.BlockSpec((t, w), lambda j, i: (i, off + j))
    prv = lambda off: pl.BlockSpec((hb, w), lambda j, i: (jnp.maximum(i * (t // hb) - 1, 0), off + j))
    nxt = lambda off: pl.BlockSpec((hb, w), lambda j, i: (jnp.minimum((i + 1) * (t // hb), s // hb - 1), off + j))
    wsp = lambda off: pl.BlockSpec((K_FFN, w), lambda j, i: (0, off + j))
    half = SDS((s, f2 // 2), bf16)
    dws = SDS(stacks[0].shape, f32)
    o_cur = pl.BlockSpec((t, w), lambda j, i: (i, j))
    o_dw = pl.BlockSpec((None, 8, w), lambda j, i: (l, 0, j))
    anywhere = pl.BlockSpec(memory_space=pl.ANY)
    return _pcall(body, name=name, out_shape=(half, half, dws, dws), grid=(nj, s // t),
                  in_specs=[cur(0), cur(nj), prv(0), prv(nj), nxt(0), nxt(nj), cur(0), nxt(0), wsp(0), wsp(nj), anywhere, anywhere],
                  out_specs=(o_cur, o_cur, o_dw, o_dw), input_output_aliases={10: 2, 11: 3},
                  compiler_params=_cp(("parallel", "arbitrary")))(upre, upre, upre, upre, upre, upre, dact, dact, conv_f, conv_f, *stacks)


def _bias_tables(rel_bias, name):
    nb = len(DILATIONS)

    def body(rb_ref, idx_ref, o_ref):
        h = pl.program_id(1)
        iv = idx_ref[0]
        qi = lax.broadcasted_iota(jnp.int32, (BLK, 2 * BLK), 0)
        kj = lax.broadcasted_iota(jnp.int32, (BLK, 2 * BLK), 1)
        band = jnp.logical_and(kj >= qi, kj <= qi + N_KEYS)
        acc = jnp.zeros((BLK, 2 * BLK), f32)
        for b in range(N_BUCKETS):
            acc = jnp.where(iv == b, rb_ref[b, h], acc)
        o_ref[0, 0, 0] = jnp.where(band, acc, NEG)
        o_ref[0, 1, 0] = jnp.where(jnp.logical_and(band, kj >= BLK), acc, NEG)

    return _pcall(body, name=name, out_shape=SDS((nb, 2, 8, BLK, 2 * BLK), f32), grid=(nb, 8),
                  in_specs=[pl.BlockSpec(memory_space=pltpu.SMEM), pl.BlockSpec((1, BLK, 2 * BLK), lambda br, h: (br, 0, 0))],
                  out_specs=pl.BlockSpec((1, 2, 1, BLK, 2 * BLK), lambda br, h: (br, 0, h, 0, 0)),
                  compiler_params=_cp(("parallel", "parallel")))(rel_bias, jnp.asarray(_BUCKETS))


def _bias_grad(ds_tabs, name):
    nl, nb = ds_tabs.shape[0], ds_tabs.shape[1]

    def body(t_ref, idx_ref, o_ref):
        br, h = pl.program_id(0), pl.program_id(1)

        @pl.when(jnp.logical_and(br == 0, h == 0))
        def _():
            o_ref[...] = jnp.zeros_like(o_ref)

        tab = t_ref[0, 0, 0]
        for l in range(1, nl):
            tab = tab + t_ref[l, 0, 0]
        iv = idx_ref[0]
        row = lax.broadcasted_iota(jnp.int32, (N_BUCKETS, 128), 0)
        col = lax.broadcasted_iota(jnp.int32, (N_BUCKETS, 128), 1)
        acc = jnp.zeros((N_BUCKETS, 128), f32)
        for b in range(N_BUCKETS):
            sel = jnp.where(iv == b, tab, 0.0)
            tot = jnp.sum(jnp.sum(sel, axis=0, keepdims=True), axis=1, keepdims=True)
            acc = acc + jnp.where(jnp.logical_and(row == b, col == h), tot, 0.0)
        o_ref[...] += acc

    return _pcall(body, name=name, out_shape=SDS((N_BUCKETS, 128), f32), grid=(nb, 8),
                  in_specs=[pl.BlockSpec((nl, 1, 1, BLK, 2 * BLK), lambda br, h: (0, br, h, 0, 0)),
                            pl.BlockSpec((1, BLK, 2 * BLK), lambda br, h: (br, 0, 0))],
                  out_specs=pl.BlockSpec((N_BUCKETS, 128), lambda br, h: (0, 0)),
                  compiler_params=_cp(("arbitrary", "arbitrary")))(ds_tabs, jnp.asarray(_BUCKETS))


def _stack_heads(x, head0):
    zero = jnp.zeros_like(x)
    return jnp.concatenate([jnp.where(head0, x, zero), jnp.where(head0, zero, x)], axis=0)


def _block_rows(blk, d, nblk):
    per_res = nblk // d
    res, n = blk // per_res, blk % per_res
    cur = pl.ds(res + d * BLK * n, BLK, stride=d) if d > 1 else pl.ds(pl.multiple_of(BLK * n, BLK), BLK)
    before = jnp.maximum(n - 1, 0)
    prev = pl.ds(res + d * BLK * before, BLK, stride=d) if d > 1 else pl.ds(pl.multiple_of(BLK * before, BLK), BLK)
    return cur, prev, jnp.where(n == 0, 1, 0)


def _attn_fwd(z, bias, name):
    s = z.shape[0]
    nblk = s // BLK

    def body(q_ref, k_ref, v_ref, b_ref, y_ref, l_ref):
        lane = lax.broadcasted_iota(jnp.int32, (BLK, BLK), 1)
        head0 = lane < D_HEAD

        for branch, d in enumerate(DILATIONS):
            def step(blk, carry, branch=branch, d=d):
                cur, prev, variant = _block_rows(blk, d, nblk)
                q = (q_ref[cur, :] * (D_HEAD ** -0.5)).astype(bf16)
                if d == nblk:
                    kk, vv = k_ref[cur, :].astype(bf16), v_ref[cur, :].astype(bf16)
                    bias_blk = b_ref[branch, 1, 0, :, BLK:]
                else:
                    kk = jnp.concatenate([k_ref[prev, :], k_ref[cur, :]], axis=0).astype(bf16)
                    vv = jnp.concatenate([v_ref[prev, :], v_ref[cur, :]], axis=0).astype(bf16)
                    bias_blk = b_ref[branch, variant, 0]
                sc = lax.dot_general(_stack_heads(q, head0), kk, (((1,), (1,)), ((), ())), preferred_element_type=f32)
                sc = sc + bias_blk
                m = jnp.max(sc, axis=-1, keepdims=True)
                p = jnp.exp(sc - m)
                den = jnp.sum(p, axis=-1, keepdims=True)
                out = jnp.dot(p.astype(bf16), vv, preferred_element_type=f32) / den
                lse = m + jnp.log(den)
                out = jnp.where(head0, out[:BLK], out[BLK:])
                lse = jnp.where(head0, lse[:BLK], lse[BLK:]) + jnp.zeros((BLK, BLK), f32)
                if branch == 0:
                    y_ref[cur, :] = out
                    l_ref[cur, :] = lse
                else:
                    y_old, l_old = y_ref[cur, :], l_ref[cur, :]
                    top = jnp.maximum(l_old, lse)
                    e_old, e_new = jnp.exp(l_old - top), jnp.exp(lse - top)
                    tot = e_old + e_new
                    y_ref[cur, :] = (e_old * y_old + e_new * out) / tot
                    l_ref[cur, :] = top + jnp.log(tot)
                return carry

            lax.fori_loop(0, nblk, step, 0, unroll=4)

    col = lambda off: pl.BlockSpec((s, BLK), lambda hp: (0, off + hp))
    out = SDS((s, 4 * BLK), f32)
    nbr = len(DILATIONS)
    return _pcall(body, name=name, out_shape=(out, out), grid=(4,),
                  in_specs=[col(Z_Q128), col(Z_K128), col(Z_V128), pl.BlockSpec((nbr, 2, 1, 2 * BLK, 2 * BLK), lambda hp: (0, 0, hp, 0, 0))],
                  out_specs=(col(0), col(0)), compiler_params=_cp(("parallel",)))(z, z, z, bias)


def _attn_bwd(z, dyb, yb, lt, bias, dz, name):
    s = z.shape[0]
    nblk = s // BLK
    scale = D_HEAD ** -0.5
    chunk = 256

    def body(q_ref, k_ref, v_ref, dy_ref, y_ref, lt_ref, b_ref, dz_in, dz_ref, ds_ref, aq, ak, av, stage, sems):
        lane = lax.broadcasted_iota(jnp.int32, (BLK, BLK), 1)
        head0 = lane < D_HEAD
        ak[...] = jnp.zeros_like(ak)
        av[...] = jnp.zeros_like(av)
        ds_ref[...] = jnp.zeros_like(ds_ref)

        for branch, d in enumerate(DILATIONS):
            def step(blk, carry, branch=branch, d=d):
                cur, prev, variant = _block_rows(blk, d, nblk)
                own_keys = d == nblk
                q = (q_ref[cur, :] * scale).astype(bf16)
                if own_keys:
                    kk, vv = k_ref[cur, :].astype(bf16), v_ref[cur, :].astype(bf16)
                    bias_blk = b_ref[branch, 1, 0, :, BLK:]
                else:
                    kk = jnp.concatenate([k_ref[prev, :], k_ref[cur, :]], axis=0).astype(bf16)
                    vv = jnp.concatenate([v_ref[prev, :], v_ref[cur, :]], axis=0).astype(bf16)
                    bias_blk = b_ref[branch, variant, 0]
                dy = dy_ref[cur, :]
                lt_b = lt_ref[cur, :]
                qs = _stack_heads(q, head0)
                dys = _stack_heads(dy.astype(bf16), head0)
                dd = jnp.sum(_stack_heads(dy * y_ref[cur, :], head0), axis=-1, keepdims=True)
                lts = jnp.concatenate([lt_b[:, 0:1], lt_b[:, D_HEAD:D_HEAD + 1]], axis=0)
                sc = lax.dot_general(qs, kk, (((1,), (1,)), ((), ())), preferred_element_type=f32)
                p = jnp.exp(sc + bias_blk - lts)
                dp = lax.dot_general(dys, vv, (((1,), (1,)), ((), ())), preferred_element_type=f32)
                ds = p * (dp - dd)
                if own_keys:
                    ds_ref[branch, 0, :, BLK:] += ds
                else:
                    ds_ref[branch, 0] += ds
                dsb = ds.astype(bf16)
                dq = jnp.dot(dsb, kk, preferred_element_type=f32) * scale
                dkk = lax.dot_general(dsb, qs, (((0,), (0,)), ((), ())), preferred_element_type=f32)
                dvv = lax.dot_general(p.astype(bf16), dys, (((0,), (0,)), ((), ())), preferred_element_type=f32)
                dq = jnp.where(head0, dq[:BLK], dq[BLK:])
                if branch == 0:
                    aq[cur, :] = dq
                else:
                    aq[cur, :] += dq
                if own_keys:
                    ak[cur, :] += dkk
                    av[cur, :] += dvv
                else:
                    ak[prev, :] += dkk[:BLK]
                    ak[cur, :] += dkk[BLK:]
                    av[prev, :] += dvv[:BLK]
                    av[cur, :] += dvv[BLK:]
                return carry

            lax.fori_loop(0, nblk, step, 0, unroll=4)

        def store(c, carry):
            rows = pl.ds(pl.multiple_of(c * chunk, chunk), chunk)
            stage[0, rows, :] = aq[rows, :].astype(bf16)
            stage[1, rows, :] = ak[rows, :].astype(bf16)
            stage[2, rows, :] = av[rows, :].astype(bf16)
            return carry

        lax.fori_loop(0, s // chunk, store, 0)
        hp = pl.program_id(0)
        copies = []
        for t, first_col in enumerate((Z_Q128, Z_K128, Z_V128)):
            cols = pl.ds(pl.multiple_of((first_col + hp) * BLK, BLK), BLK)
            copies.append(pltpu.make_async_copy(stage.at[t], dz_ref.at[:, cols], sems.at[t]))
            copies[-1].start()
        for cp in copies:
            cp.wait()

    col = lambda off: pl.BlockSpec((s, BLK), lambda hp: (0, off + hp))
    nbr = len(DILATIONS)
    tabs = pl.BlockSpec((nbr, 1, 2 * BLK, 2 * BLK), lambda hp: (0, hp, 0, 0))
    bias_spec = pl.BlockSpec((nbr, 2, 1, 2 * BLK, 2 * BLK), lambda hp: (0, 0, hp, 0, 0))
    anywhere = pl.BlockSpec(memory_space=pl.ANY)
    scratch = [pltpu.VMEM((s, BLK), f32)] * 3 + [pltpu.VMEM((3, s, BLK), bf16), pltpu.SemaphoreType.DMA((3,))]
    return _pcall(body, name=name, out_shape=(SDS(dz.shape, dz.dtype), SDS((nbr, 4, 2 * BLK, 2 * BLK), f32)), grid=(4,),
                  in_specs=[col(Z_Q128), col(Z_K128), col(Z_V128), col(0), col(0), col(0), bias_spec, anywhere],
                  out_specs=(anywhere, tabs), input_output_aliases={7: 0}, scratch_shapes=scratch,
                  compiler_params=_cp(("arbitrary",)))(z, z, z, dyb, yb, lt, bias, dz)


def _adamw_math(w, g, m, v):
    m = ADAM_B1 * m + (1.0 - ADAM_B1) * g
    v = ADAM_B2 * v + (1.0 - ADAM_B2) * (g * g)
    m_hat = m / (1.0 - ADAM_B1 ** ADAM_STEP)
    v_hat = v / (1.0 - ADAM_B2 ** ADAM_STEP)
    delta = -ADAM_LR * (m_hat / (jnp.sqrt(v_hat) + ADAM_EPS) + ADAM_WD * w)
    return delta, m, v


def _adamw_layer(land, send, w, m, v, prev, l, tr, me, name):
    _, r, c = w.shape
    assert r % tr == 0

    def body(me_ref, land_ref, own_ref, w_ref, m_ref, v_ref, p0, p1, p2, p3, g_ref, d_ref, nm_ref, nv_ref):
        mine = jnp.zeros((tr, c), jnp.int32) + me_ref[0]
        own = own_ref[...]
        g = jnp.zeros((tr, c), f32)
        for j in range(N_DEV):
            g = g + jnp.where(mine == j, own, land_ref[j]).astype(f32)
        delta, nm, nv = _adamw_math(w_ref[...], g, m_ref[...], v_ref[...])
        g_ref[...] = g
        d_ref[...] = delta
        nm_ref[...] = nm
        nv_ref[...] = nv

    parts = pl.BlockSpec((N_DEV, tr, c), lambda i, me_ref: (0, i, 0))
    own = pl.BlockSpec((None, tr, c), lambda i, me_ref: (me_ref[0], i, 0))
    row = pl.BlockSpec((None, tr, c), lambda i, me_ref: (l, i, 0))
    anywhere = pl.BlockSpec(memory_space=pl.ANY)
    out = SDS(w.shape, f32)
    grid_spec = pltpu.PrefetchScalarGridSpec(num_scalar_prefetch=1, grid=(r // tr,), in_specs=[parts, own, row, row, row] + [anywhere] * 4,
                                             out_specs=(row, row, row, row))
    return _pcall(body, name=name, out_shape=(out, out, out, out), grid_spec=grid_spec,
                  input_output_aliases={6: 0, 7: 1, 8: 2, 9: 3}, compiler_params=_cp(("parallel",)))(me, land, send, w, m, v, *prev)


def _sum_parts(parts, after, name):
    _, r, c = parts.shape

    def body(p_ref, after_ref, o_ref):
        g = p_ref[0]
        for j in range(1, N_DEV):
            g = g + p_ref[j]
        o_ref[...] = g

    return _pcall(body, name=name, out_shape=SDS((r, c), f32),
                  in_specs=[pl.BlockSpec(memory_space=pltpu.VMEM), pl.BlockSpec(memory_space=pl.ANY)],
                  out_specs=pl.BlockSpec(memory_space=pltpu.VMEM), compiler_params=_cp())(parts, after)


def _adamw_small(gs, ws, ms, vs, name):
    n = len(ws)

    def body(*refs):
        for t in range(n):
            delta, nm, nv = _adamw_math(refs[n + t][...], refs[t][...], refs[2 * n + t][...], refs[3 * n + t][...])
            refs[4 * n + t][...] = delta
            refs[5 * n + t][...] = nm
            refs[6 * n + t][...] = nv

    out = [SDS(wt.shape, f32) for wt in ws] * 3
    res = _pcall(body, name=name, out_shape=out, compiler_params=_cp())(*gs, *ws, *ms, *vs)
    return list(res[:n]), list(res[n:2 * n]), list(res[2 * n:])


def _my_place():
    return lax.axis_index("x"), lax.axis_index("y"), lax.axis_index("c")


def _all_gather(xs, name, after=None):
    n = len(xs)

    def body(*refs):
        x_refs, out_refs = refs[:n], refs[-3 - n:-3]
        send_sems, recv_sems, local_sems = refs[-3:]
        mx, my, mc = _my_place()
        me, sibling = (mx, my, mc), (mx, my, 1 - mc)
        chips = [(1 - mx, my), (mx, 1 - my), (1 - mx, 1 - my)]

        def rows(t, px, py, pc):
            return out_refs[t].at[4 * px + 2 * py + pc]

        def copy(t, k, block, to, src=None):
            return pltpu.make_async_remote_copy(src_ref=rows(t, *block) if src is None else src, dst_ref=rows(t, *block),
                                                send_sem=send_sems.at[7 * t + k], recv_sem=recv_sems.at[7 * t + k],
                                                device_id=to, device_id_type=MESH)

        mine = [pltpu.make_async_copy(x_refs[t], rows(t, *me), local_sems.at[t]) for t in range(n)]
        first, passed = [], []
        for t in range(n):
            mine[t].start()
            first += [copy(t, 0, me, sibling, src=x_refs[t])] + [copy(t, 1 + j, me, (*chip, mc), src=x_refs[t]) for j, chip in enumerate(chips)]
        for cp in first:
            cp.start()
        for t in range(n):
            for j, chip in enumerate(chips):
                copy(t, 1 + j, (*chip, mc), me).wait_recv()
                passed.append(copy(t, 4 + j, (*chip, mc), sibling))
                passed[-1].start()
        for t in range(n):
            copy(t, 0, sibling, me).wait_recv()
            for j, chip in enumerate(chips):
                copy(t, 4 + j, (*chip, 1 - mc), me).wait_recv()
        for cp in first + passed:
            cp.wait_send()
        for t in range(n):
            mine[t].wait()

    ins = list(xs) + ([after] if after is not None else [])
    anywhere = pl.BlockSpec(memory_space=pl.ANY)
    return list(_pcall(body, name=name, out_shape=[SDS((N_DEV,) + x.shape, x.dtype) for x in xs],
                       in_specs=[anywhere] * len(ins), out_specs=[anywhere] * n,
                       scratch_shapes=[pltpu.SemaphoreType.DMA((7 * n,)), pltpu.SemaphoreType.DMA((7 * n,)),
                                       pltpu.SemaphoreType.DMA((n,))])(*ins))


HBM_SPEC = pl.BlockSpec(memory_space=pltpu.HBM)
SEM_SPEC = pl.BlockSpec(memory_space=pltpu.SEMAPHORE)
ANY_SPEC = pl.BlockSpec(memory_space=pl.ANY)
DATAFLOW = pltpu.SideEffectType.DATAFLOW_SIDE_EFFECTING


def _peers():
    mx, my, mc = _my_place()
    out = []
    for k in range(1, N_DEV):
        tx, ty, tc = mx ^ ((k >> 2) & 1), my ^ ((k >> 1) & 1), mc ^ (k & 1)
        out.append(((tx, ty, tc), 4 * tx + 2 * ty + tc))
    return out


def _in_hbm(arrays):
    return [pltpu.with_memory_space_constraint(a, pltpu.HBM) for a in arrays]


N_FIRST_LEG, N_SECOND_LEG = 4, 3


def _chip_peers():
    mx, my, mc = _my_place()
    return [((tx, ty, mc), 4 * tx + 2 * ty + mc) for tx, ty in ((1 - mx, my), (mx, 1 - my), (1 - mx, 1 - my))]


def _gather_start(bufs, after, name):
    n = len(bufs)

    def body(*refs):
        ins, s_sem, r_sem, token = refs[:n], refs[n + 1], refs[n + 2], refs[-1]
        mx, my, mc = _my_place()
        me = 4 * mx + 2 * my + mc
        for t in range(n):
            for dev in [(mx, my, 1 - mc)] + [d for d, _ in _chip_peers()]:
                pltpu.make_async_remote_copy(src_ref=ins[t].at[me], dst_ref=ins[t].at[me], send_sem=s_sem.at[t],
                                             recv_sem=r_sem.at[t], device_id=dev, device_id_type=MESH).start()
        token[...] = jnp.zeros_like(token)

    out_shape = ((pltpu.SemaphoreType.DMA((n,)), pltpu.SemaphoreType.DMA((n,))) + tuple(pltpu.HBM(b.shape, b.dtype) for b in bufs)
                 + (SDS((8, 128), f32),))
    out = _pcall(body, name=name, out_shape=out_shape, in_specs=(HBM_SPEC,) * n + (ANY_SPEC,),
                 out_specs=(SEM_SPEC, SEM_SPEC) + (HBM_SPEC,) * n + (pl.BlockSpec(memory_space=pltpu.VMEM),),
                 input_output_aliases={i: 2 + i for i in range(n)},
                 compiler_params=pltpu.CompilerParams(has_side_effects=DATAFLOW))(*_in_hbm(bufs), after)
    return (out[0], out[1]), list(out[2:2 + n]), out[-1]


def _gather_forward(bufs, sems, after, name, fresh=()):
    n, nf = len(bufs), len(fresh)
    both = list(bufs) + list(fresh)

    nsem = 4 if nf else 2

    def body(*refs):
        ins, news, s1, r1 = refs[:n], refs[n:n + nf], refs[n + nf], refs[n + nf + 1]
        s2, r2 = refs[n + nf + 3:n + nf + 5]
        sf, rf = refs[n + nf + 5:n + nf + 7] if nf else (None, None)
        token = refs[-1]
        mx, my, mc = _my_place()
        me, sibling = 4 * mx + 2 * my + mc, (mx, my, 1 - mc)
        for t in range(n):
            legs = ins[t].at[pl.ds(0, N_FIRST_LEG)]
            first = pltpu.make_async_remote_copy(src_ref=legs, dst_ref=legs, send_sem=s1.at[t], recv_sem=r1.at[t],
                                                 device_id=sibling, device_id_type=MESH)
            first.wait_send()
            first.wait_recv()
        for t in range(n):
            for _, idx in _chip_peers():
                pltpu.make_async_remote_copy(src_ref=ins[t].at[idx], dst_ref=ins[t].at[idx], send_sem=s2.at[t],
                                             recv_sem=r2.at[t], device_id=sibling, device_id_type=MESH).start()
        for t in range(nf):
            for dev in [sibling] + [d for d, _ in _chip_peers()]:
                pltpu.make_async_remote_copy(src_ref=news[t].at[me], dst_ref=news[t].at[me], send_sem=sf.at[t],
                                             recv_sem=rf.at[t], device_id=dev, device_id_type=MESH).start()
        token[...] = jnp.zeros_like(token)

    sem = lambda k: pltpu.SemaphoreType.DMA((k,))
    sem_shapes = (sem(n), sem(n)) + ((sem(nf), sem(nf)) if nf else ())
    out_shape = sem_shapes + tuple(pltpu.HBM(b.shape, b.dtype) for b in both) + (SDS((8, 128), f32),)
    out = _pcall(body, name=name, out_shape=out_shape, in_specs=(HBM_SPEC,) * (n + nf) + (SEM_SPEC, SEM_SPEC, ANY_SPEC),
                 out_specs=(SEM_SPEC,) * nsem + (HBM_SPEC,) * (n + nf) + (pl.BlockSpec(memory_space=pltpu.VMEM),),
                 input_output_aliases={i: nsem + i for i in range(n + nf)},
                 compiler_params=pltpu.CompilerParams(has_side_effects=DATAFLOW))(*bufs, *_in_hbm(fresh), sems[0], sems[1], after)
    fresh_sems = (out[2], out[3]) if nf else None
    return (out[0], out[1]), list(out[nsem:nsem + n]), fresh_sems, list(out[nsem + n:nsem + n + nf]), out[-1]


def _exchange_start(sends, lands, name, after=None):
    nt = len(sends)
    extra = [] if after is None else [after]

    def body(*refs):
        srcs, dsts, token = refs[:nt], refs[nt:2 * nt], refs[-1]
        s_sem, r_sem = refs[2 * nt + len(extra)], refs[2 * nt + len(extra) + 1]
        mx, my, mc = _my_place()
        me = 4 * mx + 2 * my + mc
        for t in range(nt):
            for dev, idx in _peers():
                pltpu.make_async_remote_copy(src_ref=srcs[t].at[idx], dst_ref=dsts[t].at[me], send_sem=s_sem.at[t],
                                             recv_sem=r_sem.at[t], device_id=dev, device_id_type=MESH).start()
        token[...] = jnp.zeros_like(token)

    both = list(sends) + list(lands)
    out_shape = (pltpu.SemaphoreType.DMA((nt,)), pltpu.SemaphoreType.DMA((nt,))) + tuple(pltpu.HBM(b.shape, b.dtype) for b in both)
    out = _pcall(body, name=name, out_shape=out_shape + (SDS((8, 128), f32),), in_specs=(HBM_SPEC,) * (2 * nt) + (ANY_SPEC,) * len(extra),
                 out_specs=(SEM_SPEC, SEM_SPEC) + (HBM_SPEC,) * (2 * nt) + (pl.BlockSpec(memory_space=pltpu.VMEM),),
                 input_output_aliases={i: 2 + i for i in range(2 * nt)},
                 compiler_params=pltpu.CompilerParams(has_side_effects=DATAFLOW))(*_in_hbm(both), *extra)
    return (out[0], out[1]), list(out[2:2 + nt]), list(out[2 + nt:2 + 2 * nt]), out[-1]


def _copies_wait(srcs, dsts, sems, after, name, ncopies=N_DEV - 1):
    nt = len(srcs)
    same = srcs is dsts
    bufs = list(srcs) if same else list(srcs) + list(dsts)
    nb = len(bufs)

    def body(*refs):
        s_sem, r_sem = refs[nb], refs[nb + 1]
        mx, my, mc = _my_place()
        for t in range(nt):
            src = refs[t].at[pl.ds(0, ncopies)]
            dst = src if same else refs[nt + t].at[pl.ds(0, ncopies)]
            cp = pltpu.make_async_remote_copy(src_ref=src, dst_ref=dst, send_sem=s_sem.at[t], recv_sem=r_sem.at[t],
                                              device_id=(mx, my, 1 - mc), device_id_type=MESH)
            cp.wait_send()
            cp.wait_recv()

    out = _pcall(body, name=name, out_shape=tuple(pltpu.HBM(b.shape, b.dtype) for b in bufs),
                 in_specs=(HBM_SPEC,) * nb + (SEM_SPEC, SEM_SPEC, ANY_SPEC), out_specs=(HBM_SPEC,) * nb,
                 input_output_aliases={i: i for i in range(nb)},
                 compiler_params=pltpu.CompilerParams(has_side_effects=DATAFLOW))(*bufs, sems[0], sems[1], after)
    out = list(out)
    return (out, out) if same else (out[:nt], out[nt:])


def _cast_blocks(weights, l, me, after, name):
    nt = len(weights)
    halves = 2
    extra = [] if after is None else [after]

    def body(me_ref, *refs):
        outs = refs[nt + len(extra):]
        for t in range(nt):
            outs[t][...] = refs[t][...].astype(bf16)

    in_specs, out_specs, out_shape = [], [], []
    for wt in weights:
        _, r, c = wt.shape
        tr = r // halves
        in_specs.append(pl.BlockSpec((None, tr, c), lambda i, me_ref: (l, i, 0)))
        out_specs.append(pl.BlockSpec((None, tr, c), lambda i, me_ref: (me_ref[0], i, 0)))
        out_shape.append(SDS((N_DEV, r, c), bf16))
    in_specs += [pl.BlockSpec(memory_space=pl.ANY)] * len(extra)
    grid_spec = pltpu.PrefetchScalarGridSpec(num_scalar_prefetch=1, grid=(halves,), in_specs=in_specs, out_specs=out_specs)
    return list(_pcall(body, name=name, out_shape=out_shape, grid_spec=grid_spec, compiler_params=_cp(("parallel",)))(me, *weights, *extra))

def _layer_fwd_mix(x, p, bias, l, after=None):
    tag = f"l{l}"
    z = _mm(x, p["w_in"], "nt", 1024, 1408, f"{tag}_mm_in", after=after, norm_a=p["norm_mix_g"])
    yb, lt = _attn_fwd(z, bias, f"{tag}_attn_fwd")
    y = _mixer_fwd(z, yb, p["conv_a_w"], p["conv_c_w"], p["conv_c_b"], p["ln_c_g"], p["ln_c_b"], p["out_norm_g"], f"{tag}_mixer_fwd")
    return y, dict(x=x, z=z, yb=yb, lt=lt, y=y)


def _layer_fwd_ffn(x, y, p, l, after=None):
    tag = f"l{l}"
    x1 = _mm(y, p["w_out"], "nn", 512, 1024, f"{tag}_mm_out", res=x, after=after)
    upre = _mm(x1, p["w_up"], "nt", 1024, 1408, f"{tag}_mm_up", norm_a=p["norm_ffn_g"])
    act = _ffn_act_fwd(upre, p["conv_f_w"], f"{tag}_ffn_act")
    x2 = _mm(act, p["w_down"], "nn", 512, 1024, f"{tag}_mm_down", res=x1)
    return x2, dict(x1=x1, upre=upre, act=act)


def _small_stacks(depth, d_model, d_ff):
    shapes = dict(norm_ffn=(depth, 1, d_model), norm_mix=(depth, 1, d_model), conv_f_gate=(depth, 8, d_ff),
                  conv_f_val=(depth, 8, d_ff), mixers=(depth, PG_ROWS, 256), og_b=(depth, 1, 512))
    return {k: lax.empty(shp, f32) for k, shp in shapes.items()}


def _small_grads(st):
    pg = st["mixers"]
    row = lambda r: pg[:, r:r + 1, :]
    return dict(norm_ffn_g=st["norm_ffn"], norm_mix_g=st["norm_mix"],
                conv_f_w=jnp.concatenate([st["conv_f_gate"][:, :K_FFN], st["conv_f_val"][:, :K_FFN]], axis=2),
                conv_a_w=pg[:, PG_CONV_A:PG_CONV_A + K_SHORT], conv_c_w=pg[:, PG_CONV_C:PG_CONV_C + K_CONF],
                conv_c_b=row(PG_CB), ln_c_g=row(PG_LNG), ln_c_b=row(PG_LNB),
                out_norm_g=jnp.concatenate([row(PG_OG_A), st["og_b"], row(PG_OG_C)], axis=2))


def _layer_bwd_ffn(dx2, p, sv, l, st, after=None):
    tag = f"l{l}"
    g, st = {}, dict(st)
    dact = _mm(dx2, p["w_down"], "nt", 512, 1408, f"{tag}_mm_dact", after=after, b_outer=True)
    dwd = _mm(sv["act"], dx2, "tn", 1408, 1024, f"{tag}_mm_dwdown", out_dtype=bf16)
    g["w_down"] = dwd.reshape(N_DEV, dwd.shape[0] // N_DEV, dwd.shape[1])
    dug, duv, st["conv_f_gate"], st["conv_f_val"] = _ffn_act_bwd(sv["upre"], dact, p["conv_f_w"], (st["conv_f_gate"], st["conv_f_val"]),
                                                                 l, f"{tag}_ffn_act_bwd")
    dwu = _mm_tn_pieces([dug, duv], sv["x1"], p["norm_ffn_g"], 1408, f"{tag}_mm_dwup")
    g["w_up"] = dwu.reshape(N_DEV, dwu.shape[0] // N_DEV, dwu.shape[1])
    dx1, st["norm_ffn"] = _mm([dug, duv], p["w_up"], "nn", 512, 1024, f"{tag}_mm_dh2",
                              rms_bwd=(sv["x1"], p["norm_ffn_g"], dx2, st["norm_ffn"], l))
    return dx1, g, st


def _layer_bwd_mix(dx1, p, sv, bias, l, st, after=None):
    tag = f"l{l}"
    g, st = {}, dict(st)
    dy = _mm(dx1, p["w_out"], "nt", 512, 1024, f"{tag}_mm_dy", after=after)
    dwo = _mm(sv["y"], dx1, "tn", 512, 1024, f"{tag}_mm_dwout", out_dtype=bf16)
    g["w_out"] = dwo.reshape(N_DEV, dwo.shape[0] // N_DEV, dwo.shape[1])
    dz, dyb, st["mixers"], st["og_b"] = _mixer_bwd(sv["z"], sv["yb"], dy, p["conv_a_w"], p["conv_c_w"], p["conv_c_b"], p["ln_c_g"],
                                                   p["ln_c_b"], p["out_norm_g"], (st["mixers"], st["og_b"]), l, f"{tag}_mixer_bwd")
    dz, tabs = _attn_bwd(sv["z"], dyb, sv["yb"], sv["lt"], bias, dz, f"{tag}_attn_bwd")
    dwi = _mm(dz, sv["x"], "tn", 1408, 1024, f"{tag}_mm_dwin", out_dtype=bf16, norm_b=p["norm_mix_g"])
    g["w_in"] = dwi.reshape(N_DEV, dwi.shape[0] // N_DEV, dwi.shape[1])
    dx, st["norm_mix"] = _mm(dz, p["w_in"], "nn", 512, 1024, f"{tag}_mm_dh", rms_bwd=(sv["x"], p["norm_mix_g"], dx1, st["norm_mix"], l))
    return dx, g, tabs, st


SMALL_LAYER_PARAMS = ("norm_mix_g", "conv_a_w", "conv_c_w", "conv_c_b", "ln_c_g", "ln_c_b", "out_norm_g", "norm_ffn_g", "conv_f_w")


BIG = ("w_in", "w_out", "w_up", "w_down")
TRANSPOSED = ("w_in", "w_up")
ADAM_ROWS = {"w_in": 176, "w_out": 128, "w_up": 176, "w_down": 176}
SMALL_SHARDED = ("conv_a_w", "conv_c_w", "conv_f_w")
SMALL_REPLICATED = ("norm_mix_g", "conv_c_b", "ln_c_g", "ln_c_b", "out_norm_g", "norm_ffn_g", "rel_bias", "final_g")
WEIGHTS = ("norm_mix_g", "w_in", "conv_a_w", "conv_c_w", "conv_c_b", "ln_c_g", "ln_c_b", "out_norm_g", "w_out",
           "norm_ffn_g", "w_up", "conv_f_w", "w_down", "rel_bias", "final_g")


def _pack(arrays):
    flat = jnp.concatenate([a.reshape(-1) for a in arrays])
    rows = -(-flat.shape[0] // 1024) * 8
    return jnp.pad(flat, (0, rows * 128 - flat.shape[0])).reshape(rows, 128)


def _unpack(packed, shapes):
    flat, out, at = packed.reshape(-1), [], 0
    for shp in shapes:
        n = int(np.prod(shp))
        out.append(flat[at:at + n].reshape(shp))
        at += n
    return out


def _gather_shards(g, local_shape):
    nd = len(local_shape)
    return jnp.moveaxis(g, 0, nd - 1).reshape(local_shape[:-1] + (N_DEV * local_shape[-1],))


def kernel(x, norm_mix_g, w_in, conv_a_w, conv_c_w, conv_c_b, ln_c_g, ln_c_b, out_norm_g, w_out, norm_ffn_g, w_up, conv_f_w, w_down, rel_bias, final_g, loss_target, m_norm_mix_g, m_w_in, m_conv_a_w, m_conv_c_w, m_conv_c_b, m_ln_c_g, m_ln_c_b, m_out_norm_g, m_w_out, m_norm_ffn_g, m_w_up, m_conv_f_w, m_w_down, m_rel_bias, m_final_g, v_norm_mix_g, v_w_in, v_conv_a_w, v_conv_c_w, v_conv_c_b, v_ln_c_g, v_ln_c_b, v_out_norm_g, v_w_out, v_norm_ffn_g, v_w_up, v_conv_f_w, v_w_down, v_rel_bias, v_final_g):
    w = dict(norm_mix_g=norm_mix_g, w_in=w_in, conv_a_w=conv_a_w, conv_c_w=conv_c_w, conv_c_b=conv_c_b, ln_c_g=ln_c_g,
             ln_c_b=ln_c_b, out_norm_g=out_norm_g, w_out=w_out, norm_ffn_g=norm_ffn_g, w_up=w_up, conv_f_w=conv_f_w,
             w_down=w_down, rel_bias=rel_bias, final_g=final_g)
    mom = dict(norm_mix_g=m_norm_mix_g, w_in=m_w_in, conv_a_w=m_conv_a_w, conv_c_w=m_conv_c_w, conv_c_b=m_conv_c_b,
               ln_c_g=m_ln_c_g, ln_c_b=m_ln_c_b, out_norm_g=m_out_norm_g, w_out=m_w_out, norm_ffn_g=m_norm_ffn_g,
               w_up=m_w_up, conv_f_w=m_conv_f_w, w_down=m_w_down, rel_bias=m_rel_bias, final_g=m_final_g)
    var = dict(norm_mix_g=v_norm_mix_g, w_in=v_w_in, conv_a_w=v_conv_a_w, conv_c_w=v_conv_c_w, conv_c_b=v_conv_c_b,
               ln_c_g=v_ln_c_g, ln_c_b=v_ln_c_b, out_norm_g=v_out_norm_g, w_out=v_w_out, norm_ffn_g=v_norm_ffn_g,
               w_up=v_w_up, conv_f_w=v_conv_f_w, w_down=v_w_down, rel_bias=v_rel_bias, final_g=v_final_g)
    depth = w_in.shape[0]
    me = 4 * lax.axis_index("x") + 2 * lax.axis_index("y") + lax.axis_index("c")
    for k in TRANSPOSED:
        w[k], mom[k], var[k] = (jnp.transpose(a, (0, 2, 1)) for a in (w[k], mom[k], var[k]))

    me_arr = me.astype(jnp.int32).reshape(1)
    bufs = [_cast_blocks([w[k] for k in BIG], 0, me_arr, None, "l0_cast_blocks")]
    in_sems, in_bufs, token = _gather_start(bufs[0][:1], me_arr, "l0_gather_start_in")
    bufs += [_cast_blocks([w[k] for k in BIG], l, me_arr, token, f"l{l}_cast_blocks") for l in range(1, depth)]
    params = {}
    gathered = _all_gather([w[k] for k in SMALL_SHARDED], "gather_small", after=token)
    gs = gathered[0]
    for k, g8 in zip(SMALL_SHARDED, gathered):
        params[k] = _gather_shards(g8, w[k].shape)
    for k in ("norm_mix_g", "conv_c_b", "ln_c_g", "ln_c_b", "out_norm_g", "norm_ffn_g"):
        params[k] = w[k][:, None, :]
    params["rel_bias"] = rel_bias
    params["final_g"] = final_g[None, :]

    def full(k, buf, l):
        return buf.reshape(N_DEV * buf.shape[1], buf.shape[2])

    bias = _bias_tables(rel_bias, "bias_tables").reshape(len(DILATIONS), 2, 4, 2 * BLK, 2 * BLK)
    in_sems, in_bufs, rest_sems, rest_bufs, token = _gather_forward(in_bufs, in_sems, gs, "l0_gather_forward_in", fresh=bufs[0][1:])
    xl, after = x[0], token
    layer_params, saved = [], []
    for l in range(depth):
        _, in_bufs = _copies_wait(in_bufs, in_bufs, in_sems, after, f"l{l}_gather_wait_in", N_SECOND_LEG)
        p = {k: params[k][l] for k in SMALL_LAYER_PARAMS}
        p["w_in"] = full("w_in", in_bufs[0], l)
        token = in_bufs[0]
        nxt_in = bufs[l + 1][:1] if l + 1 < depth else []
        if l > 0:
            rest_sems, rest_bufs, in_sems, in_bufs, token = _gather_forward(rest_bufs, rest_sems, token, f"l{l}_gather_forward_rest", fresh=nxt_in)
        else:
            in_sems, in_bufs, token = _gather_start(nxt_in, token, "l1_gather_start_in")
        y, sv = _layer_fwd_mix(xl, p, bias, l, after=token)
        if l == 0:
            rest_sems, rest_bufs, _, _, _ = _gather_forward(rest_bufs, rest_sems, y, "l0_gather_forward_rest")
        _, rest_bufs = _copies_wait(rest_bufs, rest_bufs, rest_sems, y, f"l{l}_gather_wait_rest", N_SECOND_LEG)
        for k, buf in zip(BIG[1:], rest_bufs):
            p[k] = full(k, buf, l)
        token = rest_bufs[0]
        if l + 1 < depth:
            in_sems, in_bufs, rest_sems, rest_bufs, token = _gather_forward(in_bufs, in_sems, token, f"l{l + 1}_gather_forward_in",
                                                                             fresh=bufs[l + 1][1:])
        xl, sv2 = _layer_fwd_ffn(xl, y, p, l, after=token)
        layer_params.append(p)
        saved.append({**sv, **sv2})
        after = xl
    loss_tile, dx, d_final = _final_loss(xl, params["final_g"], loss_target[0], "final_loss")

    tabs, pending = [None] * depth, []
    after = None
    stacks = _small_stacks(depth, x.shape[-1], w_down.shape[1] * N_DEV)

    def exchange(l, part, names, g, after=None):
        sends = [g[k] for k in names]
        sems, sends, lands, token = _exchange_start(sends, [lax.empty(s.shape, s.dtype) for s in sends],
                                                    f"l{l}_exchange_start_{part}", after=after)
        pending.append((l, part, names, sems, sends, lands))
        return token

    for l in reversed(range(depth)):
        dx1, g_ffn, stacks = _layer_bwd_ffn(dx, layer_params[l], saved[l], l, stacks, after=after)
        after = exchange(l, "ffn", ("w_up", "w_down"), g_ffn)
        dx, g_mix, tabs[l], stacks = _layer_bwd_mix(dx1, layer_params[l], saved[l], bias, l, stacks, after=after)
        if l > 0:
            after = exchange(l, "mix", ("w_out", "w_in"), g_mix)
    grads_small = _small_grads(stacks)
    grads_small["rel_bias"] = _bias_grad(jnp.stack(tabs).reshape(depth, len(DILATIONS), 8, BLK, 2 * BLK), "bias_grad")[:, :8]
    grads_small["final_g"] = d_final

    out_g, out_d, out_m, out_v = {}, {}, {}, {}
    grads = grads_small
    small = SMALL_SHARDED + SMALL_REPLICATED
    full_shapes = [params[k].shape if k in SMALL_SHARDED else w[k].shape for k in small]
    packed = _pack([grads[k].reshape(shp) for k, shp in zip(small, full_shapes)] + [loss_tile[0, :1]])
    gathered_small = _all_gather([packed], "gather_small_grads", after=after)[0]
    token = exchange(0, "mix", ("w_out", "w_in"), g_mix, after=gathered_small)
    gsum = _sum_parts(gathered_small, token, "sum_small_grads")
    *summed, loss = _unpack(gsum, full_shapes + [(1,)])
    loss = loss[0]
    gfull = dict(zip(small, summed))
    for k in SMALL_SHARDED:
        shp = w[k].shape
        gfull[k] = lax.dynamic_index_in_dim(gfull[k].reshape(shp[:-1] + (N_DEV, shp[-1])), me, axis=len(shp) - 1, keepdims=False)
    two_d = lambda a: a.reshape(1, -1) if a.ndim == 1 else a
    res = _adamw_small(*[[two_d(src[k]) for k in small] for src in (gfull, w, mom, var)], "adamw_small")
    for name_map, outs in zip((out_d, out_m, out_v), res):
        name_map.update({k: o.reshape(w[k].shape) for k, o in zip(small, outs)})
    out_g.update(gfull)

    results = {k: [lax.empty(w[k].shape, f32) for _ in range(4)] for k in BIG}
    after = res[0][0]
    for l, part, names, sems, sends, lands in pending:
        sends, lands = _copies_wait(sends, lands, sems, after, f"l{l}_exchange_wait_{part}")
        for k, send, land in zip(names, sends, lands):
            results[k] = _adamw_layer(land, send, w[k], mom[k], var[k], results[k], l, ADAM_ROWS[k], me_arr, f"l{l}_adamw_{k}")
            after = results[k][0]
    for k in BIG:
        big = [jnp.transpose(a, (0, 2, 1)) for a in results[k]] if k in TRANSPOSED else results[k]
        out_g[k], out_d[k], out_m[k], out_v[k] = big

    return (loss, dx[None], *[out_g[k] for k in WEIGHTS], *[out_d[k] for k in WEIGHTS], *[out_m[k] for k in WEIGHTS],
            *[out_v[k] for k in WEIGHTS])
```

```python
import math

import numpy as np
import jax
import jax.numpy as jnp
from jax import lax
from jax.experimental import pallas as pl
from jax.experimental.pallas import tpu as pltpu

f32, bf16 = jnp.float32, jnp.bfloat16
SDS = jax.ShapeDtypeStruct
MESH = pl.DeviceIdType.MESH

N_DEV = 8
EPS = 1e-6
NEG = -1e30
D_HEAD = 64
BLK = 128
N_BUCKETS = 32
DILATIONS = (1, 4, 16)
N_KEYS = 128
K_SHORT, K_CONF, K_FFN = 3, 31, 3
ROW_TILE = 512
MIXER_ROWS = 512
HALO = 32
HALO_FFN = 8
FFN_FWD_ROWS, FFN_BWD_ROWS = 1024, 1024
FFN_CHUNK = 64
VMEM_LIMIT = 56 << 20
ADAM_LR, ADAM_B1, ADAM_B2, ADAM_EPS, ADAM_WD, ADAM_STEP = 0.001, 0.9, 0.999, 1e-08, 0.01, 10


def _pcall(body, **kw):
    return pl.pallas_call(body, **kw)


def _cp(sem=None):
    kw = dict(vmem_limit_bytes=VMEM_LIMIT)
    if sem is not None:
        kw["dimension_semantics"] = sem
    return pltpu.CompilerParams(**kw)


def _t5_buckets():
    rel = np.arange(BLK)[:, None] - np.arange(2 * BLK)[None, :] + BLK
    out = []
    for d in DILATIONS:
        dist = np.maximum(rel, 0) * d
        max_exact = N_BUCKETS // 2
        d_f = np.maximum(dist, 1).astype(np.float32)
        large = max_exact + (np.log(d_f / np.float32(max_exact)) / np.float32(math.log(2048 / max_exact))
                             * np.float32(N_BUCKETS - max_exact)).astype(np.int32)
        large = np.minimum(large, N_BUCKETS - 1)
        out.append(np.where(dist < max_exact, dist, large).astype(np.int32))
    return np.stack(out)


_BUCKETS = _t5_buckets()


def _mm(a, b, mode, tr, tc, name, res=None, out_dtype=f32, after=None, b_outer=False, norm_a=None, norm_b=None, rms_bwd=None):
    if rms_bwd is not None:
        return _mm_rms_bwd(a, b, tr, name, *rms_bwd)
    ij = (lambda f: (lambda j, i: f(i, j))) if b_outer else (lambda f: f)
    if mode == "nn":
        (m, k), n = a.shape, b.shape[1]
        a_spec = pl.BlockSpec((tr, k), ij(lambda i, j: (i, 0)))
        b_spec = pl.BlockSpec((k, tc), ij(lambda i, j: (0, j)))
        dims = (((1,), (0,)), ((), ()))
    elif mode == "nt":
        (m, k), n = a.shape, b.shape[0]
        a_spec = pl.BlockSpec((tr, k), ij(lambda i, j: (i, 0)))
        b_spec = pl.BlockSpec((tc, k), ij(lambda i, j: (j, 0)))
        dims = (((1,), (1,)), ((), ()))
    else:
        (k, m), n = a.shape, b.shape[1]
        a_spec = pl.BlockSpec((k, tr), ij(lambda i, j: (0, i)))
        b_spec = pl.BlockSpec((k, tc), ij(lambda i, j: (0, j)))
        dims = (((0,), (0,)), ((), ()))
    assert m % tr == 0 and n % tc == 0, (name, m, n, tr, tc)
    o_spec = pl.BlockSpec((tr, tc), ij(lambda i, j: (i, j)))
    grid = (n // tc, m // tr) if b_outer else (m // tr, n // tc)

    gain = norm_a if norm_a is not None else norm_b
    assert norm_b is None or (mode == "tn" and tc == n and not b_outer)
    assert norm_a is None or (mode != "tn" and not b_outer)

    def body(a_ref, b_ref, *rest):
        o_ref = rest[-2] if gain is not None else rest[-1]
        if norm_a is not None:
            ha_sc = rest[-1]

            @pl.when(pl.program_id(1) == 0)
            def _():
                av = a_ref[...]
                ha_sc[...] = (av * _rstd(av) * rest[0][...]).astype(bf16)

            av = ha_sc[...]
        else:
            av = a_ref[...].astype(bf16)
        if norm_b is not None:
            hb_sc = rest[-1]

            @pl.when(pl.program_id(0) == 0)
            def _():
                bv = b_ref[...]
                hb_sc[...] = (bv * _rstd(bv) * rest[0][...]).astype(bf16)

            bv = hb_sc[...]
        else:
            bv = b_ref[...].astype(bf16)
        acc = lax.dot_general(av, bv, dims, preferred_element_type=f32)
        if res is not None:
            acc = acc + rest[1 if gain is not None else 0][...]
        o_ref[...] = acc.astype(out_dtype)

    vec = pl.BlockSpec((1, k if norm_a is not None else n), lambda i, j: (0, 0))
    ins = [a, b] + ([gain] if gain is not None else []) + ([res] if res is not None else []) + ([after] if after is not None else [])
    specs = ([a_spec, b_spec] + ([vec] if gain is not None else []) + ([o_spec] if res is not None else [])
             + ([pl.BlockSpec(memory_space=pl.ANY)] if after is not None else []))
    scratch = [pltpu.VMEM((k, n), bf16)] if norm_b is not None else [pltpu.VMEM((tr, k), bf16)] if norm_a is not None else []
    sem = ("arbitrary", "arbitrary") if gain is not None else ("parallel", "parallel")
    return _pcall(body, name=name, out_shape=SDS((m, n), out_dtype), grid=grid, in_specs=specs, out_specs=o_spec,
                  scratch_shapes=scratch, compiler_params=_cp(sem))(*ins)


def _mm_tn_pieces(pieces, b, gain, tr, name):
    k, n = b.shape
    widths = [pc.shape[1] for pc in pieces]
    assert all(wd % tr == 0 for wd in widths)
    starts = [sum(widths[:p]) // tr for p in range(len(pieces))]
    counts = [wd // tr for wd in widths]
    npc = len(pieces)

    def body(*refs):
        a_refs, b_ref, g_ref, o_ref, hb_sc = refs[:npc], refs[npc], refs[npc + 1], refs[npc + 2], refs[npc + 3]
        i = pl.program_id(0)

        @pl.when(i == 0)
        def _():
            bv = b_ref[...]
            hb_sc[...] = (bv * _rstd(bv) * g_ref[...]).astype(bf16)

        for p in range(npc):
            @pl.when(jnp.logical_and(i >= starts[p], i < starts[p] + counts[p]))
            def _(p=p):
                o_ref[...] = lax.dot_general(a_refs[p][...].astype(bf16), hb_sc[...], (((0,), (0,)), ((), ())),
                                             preferred_element_type=f32).astype(bf16)

    a_specs = [pl.BlockSpec((k, tr), lambda i, p=p: (0, jnp.clip(i - starts[p], 0, counts[p] - 1))) for p in range(npc)]
    return _pcall(body, name=name, out_shape=SDS((sum(widths), n), bf16), grid=(sum(counts),),
                  in_specs=a_specs + [pl.BlockSpec((k, n), lambda i: (0, 0)), pl.BlockSpec((1, n), lambda i: (0, 0))],
                  out_specs=pl.BlockSpec((tr, n), lambda i: (i, 0)), scratch_shapes=[pltpu.VMEM((k, n), bf16)],
                  compiler_params=_cp(("arbitrary",)))(*pieces, b, gain)


def _mm_rms_bwd(a, b, tr, name, x, g, dres, stack, l):
    pieces = list(a) if isinstance(a, (list, tuple)) else [a]
    m, n, npc = pieces[0].shape[0], b.shape[1], len(pieces)
    k = sum(pc.shape[1] for pc in pieces)

    def body(*refs):
        a_refs, (b_ref, x_ref, g_ref, dres_ref, stack_ref, dx_ref, dg_ref) = refs[:npc], refs[npc:]
        dh, at = 0.0, 0
        for a_ref in a_refs:
            kp = a_ref.shape[1]
            dh = dh + jnp.dot(a_ref[...].astype(bf16), b_ref[pl.ds(at, kp), :].astype(bf16), preferred_element_type=f32)
            at += kp
        xv = x_ref[...]
        r = _rstd(xv)
        dx_ref[...] = dres_ref[...] + _rms_dx(xv, r, g_ref[...], dh)

        @pl.when(pl.program_id(0) == 0)
        def _():
            dg_ref[...] = jnp.zeros_like(dg_ref)

        dg_ref[...] += _colsum(dh * xv * r)

    row = pl.BlockSpec((tr, n), lambda i: (i, 0))
    vec = pl.BlockSpec((1, n), lambda i: (0, 0))
    a_specs = [pl.BlockSpec((tr, pc.shape[1]), lambda i: (i, 0)) for pc in pieces]
    return _pcall(body, name=name, out_shape=(SDS((m, n), f32), SDS(stack.shape, f32)), grid=(m // tr,),
                  in_specs=a_specs + [pl.BlockSpec((k, n), lambda i: (0, 0), pipeline_mode=pl.Buffered(1)),
                                      row, vec, row, pl.BlockSpec(memory_space=pl.ANY)],
                  out_specs=(row, pl.BlockSpec((None, 1, n), lambda i: (l, 0, 0))), input_output_aliases={npc + 4: 1},
                  compiler_params=_cp(("arbitrary",)))(*pieces, b, x, g, dres, stack)


def _rstd(x):
    return lax.rsqrt(jnp.mean(x * x, axis=-1, keepdims=True) + EPS)


def _rms_dx(x, r, g, dy):
    dxh = dy * g
    return r * (dxh - x * (r * r) * jnp.mean(dxh * x, axis=-1, keepdims=True))


def _sigmoid(x):
    return 0.5 * jnp.tanh(0.5 * x) + 0.5


def _colsum(x):
    return jnp.sum(x, axis=0, keepdims=True)


def _final_loss(x, g, target, name):
    s, d = x.shape

    def body(x_ref, g_ref, t_ref, loss_ref, dx_ref, dg_ref):
        xv, gv = x_ref[...], g_ref[...]
        r = _rstd(xv)
        diff = xv * r * gv - t_ref[...]
        dy = diff * (1.0 / d)

        @pl.when(pl.program_id(0) == 0)
        def _():
            dg_ref[...] = jnp.zeros_like(dg_ref)
            loss_ref[...] = jnp.zeros_like(loss_ref)

        part = jnp.sum(jnp.sum(diff * diff, axis=1, keepdims=True), axis=0, keepdims=True) * (0.5 / d)
        loss_ref[...] += part
        dx_ref[...] = _rms_dx(xv, r, gv, dy)
        dg_ref[...] += _colsum(dy * xv * r)

    row = pl.BlockSpec((ROW_TILE, d), lambda i: (i, 0))
    vec = pl.BlockSpec((1, d), lambda i: (0, 0))
    one = pl.BlockSpec((8, 128), lambda i: (0, 0))
    return _pcall(body, name=name, out_shape=(SDS((8, 128), f32), SDS((s, d), f32), SDS((1, d), f32)),
                  grid=(s // ROW_TILE,), in_specs=[row, vec, row], out_specs=(one, row, vec),
                  compiler_params=_cp(("arbitrary",)))(x, g, target)


Z_AH, Z_AB, Z_AC, Z_CV, Z_CG = 0, 1, 2, 9, 10
Z_Q128, Z_K128, Z_V128 = 6, 10, 14


def _cur(t, w, col):
    return pl.BlockSpec((t, w), lambda i: (i, col))


def _prev(hb, t, w, col):
    return pl.BlockSpec((hb, w), lambda i: (jnp.maximum(i * (t // hb) - 1, 0), col))


def _next(hb, t, w, col, rows):
    return pl.BlockSpec((hb, w), lambda i: (jnp.minimum((i + 1) * (t // hb), rows // hb - 1), col))


def _layernorm_parts(u, g, b):
    mu = jnp.mean(u, axis=-1, keepdims=True)
    var = jnp.mean(jnp.square(u - mu), axis=-1, keepdims=True)
    rs = lax.rsqrt(var + EPS)
    xhat = (u - mu) * rs
    return xhat, rs, xhat * g + b


def _fill_shifted(rot):
    n = rot.shape[1]
    for r in range(1, 8):
        rot[r, pl.ds(0, n - 8), :] = rot[0, pl.ds(r, n - 8), :]


def _window(rot, offset, rows):
    return rot[offset % 8, pl.ds(offset - offset % 8, rows), :]


def _mixer_fwd(z, yb, conv_a, conv_c, cb, lng, lnb, og, name):
    s = z.shape[0]
    t, hb, w = MIXER_ROWS, HALO, 256

    def body(ah, ab, ac, cv, cg, ah_p, ac_p, cv_p, cg_p, yb_ref, wa, wc, cb_ref, lng_ref, lnb_ref, og_ref, y_ref, p_sc, u_sc):
        first = pl.program_id(0) == 0

        def prev(ref):
            return jnp.where(first, 0.0, ref[...])

        p_sc[pl.ds(0, hb), :] = prev(ah_p) * prev(ac_p)
        p_sc[pl.ds(hb, t), :] = ah[...] * ac[...]
        conv = jnp.zeros((t, w), f32)
        for k in range(K_SHORT):
            conv = conv + wa[pl.ds(k, 1), :] * p_sc[pl.ds(hb - (K_SHORT - 1) + k, t), :]
        y_a = ab[...] * conv
        u_sc[0, pl.ds(0, hb), :] = prev(cv_p) * _sigmoid(prev(cg_p))
        u_sc[0, pl.ds(hb, t), :] = cv[...] * _sigmoid(cg[...])
        _fill_shifted(u_sc)
        u = jnp.zeros((t, w), f32) + cb_ref[...]
        for k in range(K_CONF):
            u = u + wc[pl.ds(k, 1), :] * _window(u_sc, hb - (K_CONF - 1) + k, t)
        _, _, lnv = _layernorm_parts(u, lng_ref[...], lnb_ref[...])
        y_c = lnv * _sigmoid(lnv)
        ybv = yb_ref[...]
        y_ref[:, 0:256] = (y_a * _rstd(y_a) * og_ref[:, 0:256]).astype(bf16)
        y_ref[:, 256:768] = (ybv * _rstd(ybv) * og_ref[:, 256:768]).astype(bf16)
        y_ref[:, 768:1024] = (y_c * _rstd(y_c) * og_ref[:, 768:1024]).astype(bf16)

    full = lambda shape: pl.BlockSpec(shape, lambda i: tuple(0 for _ in shape))
    in_specs = [_cur(t, w, c) for c in (Z_AH, Z_AB, Z_AC, Z_CV, Z_CG)] + [_prev(hb, t, w, c) for c in (Z_AH, Z_AC, Z_CV, Z_CG)]
    in_specs += [_cur(t, 512, 0), full((K_SHORT, w)), full((K_CONF, w)), full((1, w)), full((1, w)), full((1, w)), full((1, 1024))]
    return _pcall(body, name=name, out_shape=SDS((s, 1024), bf16), grid=(s // t,), in_specs=in_specs,
                  out_specs=_cur(t, 1024, 0), scratch_shapes=[pltpu.VMEM((hb + t, w), f32), pltpu.VMEM((8, hb + t, w), f32)],
                  compiler_params=_cp(("parallel",)))(z, z, z, z, z, z, z, z, z, yb, conv_a, conv_c, cb, lng, lnb, og)


PG_CONV_A, PG_CONV_C, PG_CB, PG_LNG, PG_LNB, PG_OG_A, PG_OG_C, PG_ROWS = 0, 8, 40, 41, 42, 43, 44, 48


def _mixer_bwd(z, yb, dy, conv_a, conv_c, cb, lng, lnb, og, stacks, l, name):
    s = z.shape[0]
    t, hb, w = MIXER_ROWS, HALO, 256
    e = t + hb

    def body(ah, ab, ac, cv, cg, ah_p, ac_p, cv_p, cg_p, ah_n, ab_n, ac_n, cv_n, cg_n,
             dya, dyb1, dyb2, dyc, dya_n, dyc_n, yb_ref, wa, wc, cb_ref, lng_ref, lnb_ref, og_ref, pg_stack, dgb_stack,
             dz_ref, dyb_ref, pg_ref, dgb_ref, p_sc, u_sc, dc_sc, du_sc):
        i = pl.program_id(0)
        first, last = i == 0, i == pl.num_programs(0) - 1

        def prev(ref):
            return jnp.where(first, 0.0, ref[...])

        def nxt(ref):
            return jnp.where(last, 0.0, ref[...])

        def ext(cur_ref, next_ref):
            return jnp.concatenate([cur_ref[...], nxt(next_ref)], axis=0)

        @pl.when(first)
        def _():
            pg_ref[...] = jnp.zeros_like(pg_ref)
            dgb_ref[...] = jnp.zeros_like(dgb_ref)

        g_a = og_ref[:, 0:256]
        p_sc[pl.ds(0, hb), :] = prev(ah_p) * prev(ac_p)
        p_sc[pl.ds(hb, t), :] = ah[...] * ac[...]
        p_sc[pl.ds(hb + t, hb), :] = nxt(ah_n) * nxt(ac_n)
        conv = jnp.zeros((e, w), f32)
        for k in range(K_SHORT):
            conv = conv + wa[pl.ds(k, 1), :] * p_sc[pl.ds(hb - (K_SHORT - 1) + k, e), :]
        ab_e, dya_e = ext(ab, ab_n), ext(dya, dya_n)
        y_a = ab_e * conv
        r_a = _rstd(y_a)
        dy_a = _rms_dx(y_a, r_a, g_a, dya_e)
        dc_sc[...] = dy_a * ab_e
        dz_ref[:, 256:512] = (dy_a[:t] * conv[:t]).astype(bf16)
        dp = jnp.zeros((t, w), f32)
        for k in range(K_SHORT):
            dp = dp + wa[pl.ds(k, 1), :] * dc_sc[pl.ds(K_SHORT - 1 - k, t), :]
            pg_ref[pl.ds(PG_CONV_A + k, 1), :] += _colsum(dc_sc[pl.ds(0, t), :] * p_sc[pl.ds(hb - (K_SHORT - 1) + k, t), :])
        dz_ref[:, 0:256] = (dp * ac[...]).astype(bf16)
        dz_ref[:, 512:768] = (dp * ah[...]).astype(bf16)
        pg_ref[pl.ds(PG_OG_A, 1), :] += _colsum(dya[...] * y_a[:t] * r_a[:t])

        g_c = og_ref[:, 768:1024]
        u_sc[0, pl.ds(0, hb), :] = prev(cv_p) * _sigmoid(prev(cg_p))
        u_sc[0, pl.ds(hb, t), :] = cv[...] * _sigmoid(cg[...])
        u_sc[0, pl.ds(hb + t, hb), :] = nxt(cv_n) * _sigmoid(nxt(cg_n))
        _fill_shifted(u_sc)
        u = jnp.zeros((e, w), f32) + cb_ref[...]
        for k in range(K_CONF):
            u = u + wc[pl.ds(k, 1), :] * _window(u_sc, hb - (K_CONF - 1) + k, e)
        xhat, rs, lnv = _layernorm_parts(u, lng_ref[...], lnb_ref[...])
        sg = _sigmoid(lnv)
        y_c = lnv * sg
        r_c = _rstd(y_c)
        dyc_e = ext(dyc, dyc_n)
        dlnv = _rms_dx(y_c, r_c, g_c, dyc_e) * (sg * (1.0 + lnv * (1.0 - sg)))
        dxh = dlnv * lng_ref[...]
        du = rs * (dxh - jnp.mean(dxh, axis=-1, keepdims=True) - xhat * jnp.mean(dxh * xhat, axis=-1, keepdims=True))
        du_sc[0] = du
        _fill_shifted(du_sc)
        duin = jnp.zeros((t, w), f32)
        for k in range(K_CONF):
            duin = duin + wc[pl.ds(k, 1), :] * _window(du_sc, K_CONF - 1 - k, t)
            pg_ref[pl.ds(PG_CONV_C + k, 1), :] += _colsum(du[:t] * _window(u_sc, hb - (K_CONF - 1) + k, t))
        sgg = _sigmoid(cg[...])
        dz_ref[:, 2304:2560] = (duin * sgg).astype(bf16)
        dz_ref[:, 2560:2816] = (duin * cv[...] * sgg * (1.0 - sgg)).astype(bf16)
        pg_ref[pl.ds(PG_CB, 1), :] += _colsum(du[:t])
        pg_ref[pl.ds(PG_LNG, 1), :] += _colsum(dlnv[:t] * xhat[:t])
        pg_ref[pl.ds(PG_LNB, 1), :] += _colsum(dlnv[:t])
        pg_ref[pl.ds(PG_OG_C, 1), :] += _colsum(dyc[...] * y_c[:t] * r_c[:t])

        g_b = og_ref[:, 256:768]
        ybv = yb_ref[...]
        r_b = _rstd(ybv)
        dyb = jnp.concatenate([dyb1[...], dyb2[...]], axis=1)
        dyb_ref[...] = _rms_dx(ybv, r_b, g_b, dyb)
        dgb_ref[...] += _colsum(dyb * ybv * r_b)

    full = lambda shape: pl.BlockSpec(shape, lambda i: tuple(0 for _ in shape))
    zc = (Z_AH, Z_AB, Z_AC, Z_CV, Z_CG)
    in_specs = [_cur(t, w, c) for c in zc] + [_prev(hb, t, w, c) for c in (Z_AH, Z_AC, Z_CV, Z_CG)]
    in_specs += [_next(hb, t, w, c, s) for c in zc]
    in_specs += [_cur(t, w, c) for c in (0, 1, 2, 3)] + [_next(hb, t, w, 0, s), _next(hb, t, w, 3, s)]
    in_specs += [_cur(t, 512, 0), full((K_SHORT, w)), full((K_CONF, w)), full((1, w)), full((1, w)), full((1, w)), full((1, 1024))]
    in_specs += [pl.BlockSpec(memory_space=pl.ANY)] * 2
    out_shape = (SDS((s, z.shape[1]), bf16), SDS((s, 512), f32), SDS(stacks[0].shape, f32), SDS(stacks[1].shape, f32))
    out_specs = (_cur(t, z.shape[1], 0), _cur(t, 512, 0), pl.BlockSpec((None, PG_ROWS, w), lambda i: (l, 0, 0)),
                 pl.BlockSpec((None, 1, 512), lambda i: (l, 0, 0)))
    scratch = [pltpu.VMEM((hb + t + hb, w), f32), pltpu.VMEM((8, hb + t + hb, w), f32), pltpu.VMEM((e, w), f32), pltpu.VMEM((8, e, w), f32)]
    args = [z] * 14 + [dy] * 6 + [yb, conv_a, conv_c, cb, lng, lnb, og] + list(stacks)
    return _pcall(body, name=name, out_shape=out_shape, grid=(s // t,), in_specs=in_specs, out_specs=out_specs,
                  input_output_aliases={27: 2, 28: 3}, scratch_shapes=scratch, compiler_params=_cp(("arbitrary",)))(*args)


def _ffn_act_fwd(upre, conv_f, name):
    s, f2 = upre.shape
    t, hb, w = FFN_FWD_ROWS, HALO_FFN, 256
    nj = f2 // 2 // w

    def body(g_c, v_c, g_p, v_p, wg, wv, o_ref):
        first = pl.program_id(1) == 0
        wgs = [wg[pl.ds(k, 1), :] for k in range(K_FFN)]
        wvs = [wv[pl.ds(k, 1), :] for k in range(K_FFN)]
        rows = FFN_CHUNK
        first_tap = hb - (K_FFN - 1)

        def emit(gb, vb, r0):
            ug = sum(wgs[k] * gb[first_tap + k:first_tap + k + rows] for k in range(K_FFN))
            uv = sum(wvs[k] * vb[first_tap + k:first_tap + k + rows] for k in range(K_FFN))
            o_ref[pl.ds(r0, rows), :] = (ug * _sigmoid(ug) * uv).astype(bf16)

        emit(jnp.concatenate([jnp.where(first, 0.0, g_p[...]), g_c[pl.ds(0, rows), :]], axis=0),
             jnp.concatenate([jnp.where(first, 0.0, v_p[...]), v_c[pl.ds(0, rows), :]], axis=0), 0)

        def chunk(c, carry):
            r0 = pl.multiple_of(c * rows, rows)
            back = pl.ds(pl.multiple_of(r0 - hb, hb), rows + hb)
            emit(g_c[back, :], v_c[back, :], r0)
            return carry

        lax.fori_loop(1, t // rows, chunk, 0, unroll=3)

    cur = lambda off: pl.BlockSpec((t, w), lambda j, i: (i, off + j))
    prv = lambda off: pl.BlockSpec((hb, w), lambda j, i: (jnp.maximum(i * (t // hb) - 1, 0), off + j))
    wsp = lambda off: pl.BlockSpec((K_FFN, w), lambda j, i: (0, off + j))
    return _pcall(body, name=name, out_shape=SDS((s, f2 // 2), bf16), grid=(nj, s // t),
                  in_specs=[cur(0), cur(nj), prv(0), prv(nj), wsp(0), wsp(nj)],
                  out_specs=pl.BlockSpec((t, w), lambda j, i: (i, j)),
                  compiler_params=_cp(("parallel", "parallel")))(upre, upre, upre, upre, conv_f, conv_f)


def _ffn_act_bwd(upre, dact, conv_f, stacks, l, name):
    s, f2 = upre.shape
    t, hb, w = FFN_BWD_ROWS, HALO_FFN, 256
    nj = f2 // 2 // w

    def body(g_c, v_c, g_p, v_p, g_n, v_n, da_c, da_n, wg, wv, sg_ref, sv_ref, dg_ref, dv_ref, dwg_ref, dwv_ref):
        i = pl.program_id(1)
        first, last = i == 0, i == pl.num_programs(1) - 1

        @pl.when(first)
        def _():
            dwg_ref[...] = jnp.zeros_like(dwg_ref)
            dwv_ref[...] = jnp.zeros_like(dwv_ref)

        wgs = [wg[pl.ds(k, 1), :] for k in range(K_FFN)]
        wvs = [wv[pl.ds(k, 1), :] for k in range(K_FFN)]
        rows = FFN_CHUNK
        n = rows + hb
        first_tap = hb - (K_FFN - 1)
        nchunks = t // rows

        def groups_of_8(x):
            return sum(x[8 * q:8 * q + 8] for q in range(rows // 8))

        def emit(gb, vb, da, r0, carry):
            g_taps = [gb[first_tap + k:first_tap + k + n] for k in range(K_FFN)]
            v_taps = [vb[first_tap + k:first_tap + k + n] for k in range(K_FFN)]
            ug = sum(wgs[k] * g_taps[k] for k in range(K_FFN))
            uv = sum(wvs[k] * v_taps[k] for k in range(K_FFN))
            sg = _sigmoid(ug)
            dv = da * (ug * sg)
            dg = da * uv * (sg * (1.0 + ug * (1.0 - sg)))
            new = []
            for taps, dval, ws, dref in ((g_taps, dg, wgs, dg_ref), (v_taps, dv, wvs, dv_ref)):
                acc = sum(ws[k] * dval[K_FFN - 1 - k:K_FFN - 1 - k + rows] for k in range(K_FFN))
                dref[pl.ds(r0, rows), :] = acc.astype(bf16)
                new += [groups_of_8(dval[:rows] * taps[k][:rows]) for k in range(K_FFN)]
            return tuple(a + b for a, b in zip(carry, new))

        def chunk(c, carry):
            r0 = pl.multiple_of(c * rows, rows)
            around = pl.ds(pl.multiple_of(r0 - hb, hb), n + hb)
            return emit(g_c[around, :], v_c[around, :], da_c[pl.ds(r0, n), :], r0, carry)

        def edge(prev_ref, cur_ref, lo):
            return jnp.concatenate([jnp.where(first, 0.0, prev_ref[...]), cur_ref[pl.ds(0, lo), :]], axis=0)

        def tail(cur_ref, next_ref, lo):
            return jnp.concatenate([cur_ref[pl.ds(t - lo, lo), :], jnp.where(last, 0.0, next_ref[...])], axis=0)

        zero = jnp.zeros((8, w), f32)
        sums = emit(edge(g_p, g_c, n), edge(v_p, v_c, n), da_c[pl.ds(0, n), :], 0, (zero,) * (2 * K_FFN))
        sums = lax.fori_loop(1, nchunks - 1, chunk, sums, unroll=2)
        sums = emit(tail(g_c, g_n, rows + hb), tail(v_c, v_n, rows + hb), tail(da_c, da_n, rows), t - rows, sums)
        for k in range(K_FFN):
            dwg_ref[pl.ds(k, 1), :] += _colsum(sums[k])
            dwv_ref[pl.ds(k, 1), :] += _colsum(sums[K_FFN + k])

    cur = lambda off: pl.BlockSpec((t, w), lambda j, i: (i, off + j))
    prv = lambda off: pl.BlockSpec((hb, w), lambda j, i: (jnp.maximum(i * (t // hb) - 1, 0), off + j))
    nxt = lambda off: pl.BlockSpec((hb, w), lambda j, i: (jnp.minimum((i + 1) * (t // hb), s // hb - 1), off + j))
    wsp = lambda off: pl.BlockSpec((K_FFN, w), lambda j, i: (0, off + j))
    half = SDS((s, f2 // 2), bf16)
    dws = SDS(stacks[0].shape, f32)
    o_cur = pl.BlockSpec((t, w), lambda j, i: (i, j))
    o_dw = pl.BlockSpec((None, 8, w), lambda j, i: (l, 0, j))
    anywhere = pl.BlockSpec(memory_space=pl.ANY)
    return _pcall(body, name=name, out_shape=(half, half, dws, dws), grid=(nj, s // t),
                  in_specs=[cur(0), cur(nj), prv(0), prv(nj), nxt(0), nxt(nj), cur(0), nxt(0), wsp(0), wsp(nj), anywhere, anywhere],
                  out_specs=(o_cur, o_cur, o_dw, o_dw), input_output_aliases={10: 2, 11: 3},
                  compiler_params=_cp(("parallel", "arbitrary")))(upre, upre, upre, upre, upre, upre, dact, dact, conv_f, conv_f, *stacks)


def _bias_tables(rel_bias, name):
    nb = len(DILATIONS)

    def body(rb_ref, idx_ref, o_ref):
        h = pl.program_id(1)
        iv = idx_ref[0]
        qi = lax.broadcasted_iota(jnp.int32, (BLK, 2 * BLK), 0)
        kj = lax.broadcasted_iota(jnp.int32, (BLK, 2 * BLK), 1)
        band = jnp.logical_and(kj >= qi, kj <= qi + N_KEYS)
        acc = jnp.zeros((BLK, 2 * BLK), f32)
        for b in range(N_BUCKETS):
            acc = jnp.where(iv == b, rb_ref[b, h], acc)
        o_ref[0, 0, 0] = jnp.where(band, acc, NEG)
        o_ref[0, 1, 0] = jnp.where(jnp.logical_and(band, kj >= BLK), acc, NEG)

    return _pcall(body, name=name, out_shape=SDS((nb, 2, 8, BLK, 2 * BLK), f32), grid=(nb, 8),
                  in_specs=[pl.BlockSpec(memory_space=pltpu.SMEM), pl.BlockSpec((1, BLK, 2 * BLK), lambda br, h: (br, 0, 0))],
                  out_specs=pl.BlockSpec((1, 2, 1, BLK, 2 * BLK), lambda br, h: (br, 0, h, 0, 0)),
                  compiler_params=_cp(("parallel", "parallel")))(rel_bias, jnp.asarray(_BUCKETS))


def _bias_grad(ds_tabs, name):
    nl, nb = ds_tabs.shape[0], ds_tabs.shape[1]

    def body(t_ref, idx_ref, o_ref):
        br, h = pl.program_id(0), pl.program_id(1)

        @pl.when(jnp.logical_and(br == 0, h == 0))
        def _():
            o_ref[...] = jnp.zeros_like(o_ref)

        tab = t_ref[0, 0, 0]
        for l in range(1, nl):
            tab = tab + t_ref[l, 0, 0]
        iv = idx_ref[0]
        row = lax.broadcasted_iota(jnp.int32, (N_BUCKETS, 128), 0)
        col = lax.broadcasted_iota(jnp.int32, (N_BUCKETS, 128), 1)
        acc = jnp.zeros((N_BUCKETS, 128), f32)
        for b in range(N_BUCKETS):
            sel = jnp.where(iv == b, tab, 0.0)
            tot = jnp.sum(jnp.sum(sel, axis=0, keepdims=True), axis=1, keepdims=True)
            acc = acc + jnp.where(jnp.logical_and(row == b, col == h), tot, 0.0)
        o_ref[...] += acc

    return _pcall(body, name=name, out_shape=SDS((N_BUCKETS, 128), f32), grid=(nb, 8),
                  in_specs=[pl.BlockSpec((nl, 1, 1, BLK, 2 * BLK), lambda br, h: (0, br, h, 0, 0)),
                            pl.BlockSpec((1, BLK, 2 * BLK), lambda br, h: (br, 0, 0))],
                  out_specs=pl.BlockSpec((N_BUCKETS, 128), lambda br, h: (0, 0)),
                  compiler_params=_cp(("arbitrary", "arbitrary")))(ds_tabs, jnp.asarray(_BUCKETS))


def _stack_heads(x, head0):
    zero = jnp.zeros_like(x)
    return jnp.concatenate([jnp.where(head0, x, zero), jnp.where(head0, zero, x)], axis=0)


def _block_rows(blk, d, nblk):
    per_res = nblk // d
    res, n = blk // per_res, blk % per_res
    cur = pl.ds(res + d * BLK * n, BLK, stride=d) if d > 1 else pl.ds(pl.multiple_of(BLK * n, BLK), BLK)
    before = jnp.maximum(n - 1, 0)
    prev = pl.ds(res + d * BLK * before, BLK, stride=d) if d > 1 else pl.ds(pl.multiple_of(BLK * before, BLK), BLK)
    return cur, prev, jnp.where(n == 0, 1, 0)


def _attn_fwd(z, bias, name):
    s = z.shape[0]
    nblk = s // BLK

    def body(q_ref, k_ref, v_ref, b_ref, y_ref, l_ref):
        lane = lax.broadcasted_iota(jnp.int32, (BLK, BLK), 1)
        head0 = lane < D_HEAD

        for branch, d in enumerate(DILATIONS):
            def step(blk, carry, branch=branch, d=d):
                cur, prev, variant = _block_rows(blk, d, nblk)
                q = (q_ref[cur, :] * (D_HEAD ** -0.5)).astype(bf16)
                if d == nblk:
                    kk, vv = k_ref[cur, :].astype(bf16), v_ref[cur, :].astype(bf16)
                    bias_blk = b_ref[branch, 1, 0, :, BLK:]
                else:
                    kk = jnp.concatenate([k_ref[prev, :], k_ref[cur, :]], axis=0).astype(bf16)
                    vv = jnp.concatenate([v_ref[prev, :], v_ref[cur, :]], axis=0).astype(bf16)
                    bias_blk = b_ref[branch, variant, 0]
                sc = lax.dot_general(_stack_heads(q, head0), kk, (((1,), (1,)), ((), ())), preferred_element_type=f32)
                sc = sc + bias_blk
                m = jnp.max(sc, axis=-1, keepdims=True)
                p = jnp.exp(sc - m)
                den = jnp.sum(p, axis=-1, keepdims=True)
                out = jnp.dot(p.astype(bf16), vv, preferred_element_type=f32) / den
                lse = m + jnp.log(den)
                out = jnp.where(head0, out[:BLK], out[BLK:])
                lse = jnp.where(head0, lse[:BLK], lse[BLK:]) + jnp.zeros((BLK, BLK), f32)
                if branch == 0:
                    y_ref[cur, :] = out
                    l_ref[cur, :] = lse
                else:
                    y_old, l_old = y_ref[cur, :], l_ref[cur, :]
                    top = jnp.maximum(l_old, lse)
                    e_old, e_new = jnp.exp(l_old - top), jnp.exp(lse - top)
                    tot = e_old + e_new
                    y_ref[cur, :] = (e_old * y_old + e_new * out) / tot
                    l_ref[cur, :] = top + jnp.log(tot)
                return carry

            lax.fori_loop(0, nblk, step, 0, unroll=16)

    col = lambda off: pl.BlockSpec((s, BLK), lambda hp: (0, off + hp))
    out = SDS((s, 4 * BLK), f32)
    nbr = len(DILATIONS)
    return _pcall(body, name=name, out_shape=(out, out), grid=(4,),
                  in_specs=[col(Z_Q128), col(Z_K128), col(Z_V128), pl.BlockSpec((nbr, 2, 1, 2 * BLK, 2 * BLK), lambda hp: (0, 0, hp, 0, 0))],
                  out_specs=(col(0), col(0)), compiler_params=_cp(("parallel",)))(z, z, z, bias)


def _attn_bwd(z, dyb, yb, lt, bias, dz, name):
    s = z.shape[0]
    nblk = s // BLK
    scale = D_HEAD ** -0.5
    chunk = 256

    def body(q_ref, k_ref, v_ref, dy_ref, y_ref, lt_ref, b_ref, dz_in, dz_ref, ds_ref, aq, ak, av, stage, sems):
        lane = lax.broadcasted_iota(jnp.int32, (BLK, BLK), 1)
        head0 = lane < D_HEAD
        ak[...] = jnp.zeros_like(ak)
        av[...] = jnp.zeros_like(av)
        ds_ref[...] = jnp.zeros_like(ds_ref)

        for branch, d in enumerate(DILATIONS):
            def step(blk, carry, branch=branch, d=d):
                cur, prev, variant = _block_rows(blk, d, nblk)
                own_keys = d == nblk
                q = (q_ref[cur, :] * scale).astype(bf16)
                if own_keys:
                    kk, vv = k_ref[cur, :].astype(bf16), v_ref[cur, :].astype(bf16)
                    bias_blk = b_ref[branch, 1, 0, :, BLK:]
                else:
                    kk = jnp.concatenate([k_ref[prev, :], k_ref[cur, :]], axis=0).astype(bf16)
                    vv = jnp.concatenate([v_ref[prev, :], v_ref[cur, :]], axis=0).astype(bf16)
                    bias_blk = b_ref[branch, variant, 0]
                dy = dy_ref[cur, :]
                lt_b = lt_ref[cur, :]
                qs = _stack_heads(q, head0)
                dys = _stack_heads(dy.astype(bf16), head0)
                dd = jnp.sum(_stack_heads(dy * y_ref[cur, :], head0), axis=-1, keepdims=True)
                lts = jnp.concatenate([lt_b[:, 0:1], lt_b[:, D_HEAD:D_HEAD + 1]], axis=0)
                sc = lax.dot_general(qs, kk, (((1,), (1,)), ((), ())), preferred_element_type=f32)
                p = jnp.exp(sc + bias_blk - lts)
                dp = lax.dot_general(dys, vv, (((1,), (1,)), ((), ())), preferred_element_type=f32)
                ds = p * (dp - dd)
                if own_keys:
                    ds_ref[branch, 0, :, BLK:] += ds
                else:
                    ds_ref[branch, 0] += ds
                dsb = ds.astype(bf16)
                dq = jnp.dot(dsb, kk, preferred_element_type=f32) * scale
                dkk = lax.dot_general(dsb, qs, (((0,), (0,)), ((), ())), preferred_element_type=f32)
                dvv = lax.dot_general(p.astype(bf16), dys, (((0,), (0,)), ((), ())), preferred_element_type=f32)
                dq = jnp.where(head0, dq[:BLK], dq[BLK:])
                if branch == 0:
                    aq[cur, :] = dq
                else:
                    aq[cur, :] += dq
                if own_keys:
                    ak[cur, :] += dkk
                    av[cur, :] += dvv
                else:
                    ak[prev, :] += dkk[:BLK]
                    ak[cur, :] += dkk[BLK:]
                    av[prev, :] += dvv[:BLK]
                    av[cur, :] += dvv[BLK:]
                return carry

            lax.fori_loop(0, nblk, step, 0, unroll=16)

        def store(c, carry):
            rows = pl.ds(pl.multiple_of(c * chunk, chunk), chunk)
            stage[0, rows, :] = aq[rows, :].astype(bf16)
            stage[1, rows, :] = ak[rows, :].astype(bf16)
            stage[2, rows, :] = av[rows, :].astype(bf16)
            return carry

        lax.fori_loop(0, s // chunk, store, 0)
        hp = pl.program_id(0)
        copies = []
        for t, first_col in enumerate((Z_Q128, Z_K128, Z_V128)):
            cols = pl.ds(pl.multiple_of((first_col + hp) * BLK, BLK), BLK)
            copies.append(pltpu.make_async_copy(stage.at[t], dz_ref.at[:, cols], sems.at[t]))
            copies[-1].start()
        for cp in copies:
            cp.wait()

    col = lambda off: pl.BlockSpec((s, BLK), lambda hp: (0, off + hp))
    nbr = len(DILATIONS)
    tabs = pl.BlockSpec((nbr, 1, 2 * BLK, 2 * BLK), lambda hp: (0, hp, 0, 0))
    bias_spec = pl.BlockSpec((nbr, 2, 1, 2 * BLK, 2 * BLK), lambda hp: (0, 0, hp, 0, 0))
    anywhere = pl.BlockSpec(memory_space=pl.ANY)
    scratch = [pltpu.VMEM((s, BLK), f32)] * 3 + [pltpu.VMEM((3, s, BLK), bf16), pltpu.SemaphoreType.DMA((3,))]
    return _pcall(body, name=name, out_shape=(SDS(dz.shape, dz.dtype), SDS((nbr, 4, 2 * BLK, 2 * BLK), f32)), grid=(4,),
                  in_specs=[col(Z_Q128), col(Z_K128), col(Z_V128), col(0), col(0), col(0), bias_spec, anywhere],
                  out_specs=(anywhere, tabs), input_output_aliases={7: 0}, scratch_shapes=scratch,
                  compiler_params=_cp(("arbitrary",)))(z, z, z, dyb, yb, lt, bias, dz)


def _adamw_math(w, g, m, v):
    m = ADAM_B1 * m + (1.0 - ADAM_B1) * g
    v = ADAM_B2 * v + (1.0 - ADAM_B2) * (g * g)
    m_hat = m / (1.0 - ADAM_B1 ** ADAM_STEP)
    v_hat = v / (1.0 - ADAM_B2 ** ADAM_STEP)
    delta = -ADAM_LR * (m_hat / (jnp.sqrt(v_hat) + ADAM_EPS) + ADAM_WD * w)
    return delta, m, v


def _adamw_layer(land, send, w, m, v, prev, l, tr, me, name):
    _, r, c = w.shape
    assert r % tr == 0

    def body(me_ref, land_ref, own_ref, w_ref, m_ref, v_ref, p0, p1, p2, p3, g_ref, d_ref, nm_ref, nv_ref):
        mine = jnp.zeros((tr, c), jnp.int32) + me_ref[0]
        own = own_ref[...]
        g = jnp.zeros((tr, c), f32)
        for j in range(N_DEV):
            g = g + jnp.where(mine == j, own, land_ref[j]).astype(f32)
        delta, nm, nv = _adamw_math(w_ref[...], g, m_ref[...], v_ref[...])
        g_ref[...] = g
        d_ref[...] = delta
        nm_ref[...] = nm
        nv_ref[...] = nv

    parts = pl.BlockSpec((N_DEV, tr, c), lambda i, me_ref: (0, i, 0))
    own = pl.BlockSpec((None, tr, c), lambda i, me_ref: (me_ref[0], i, 0))
    row = pl.BlockSpec((None, tr, c), lambda i, me_ref: (l, i, 0))
    anywhere = pl.BlockSpec(memory_space=pl.ANY)
    out = SDS(w.shape, f32)
    grid_spec = pltpu.PrefetchScalarGridSpec(num_scalar_prefetch=1, grid=(r // tr,), in_specs=[parts, own, row, row, row] + [anywhere] * 4,
                                             out_specs=(row, row, row, row))
    return _pcall(body, name=name, out_shape=(out, out, out, out), grid_spec=grid_spec,
                  input_output_aliases={6: 0, 7: 1, 8: 2, 9: 3}, compiler_params=_cp(("parallel",)))(me, land, send, w, m, v, *prev)


def _sum_parts(parts, after, name):
    _, r, c = parts.shape

    def body(p_ref, after_ref, o_ref):
        g = p_ref[0]
        for j in range(1, N_DEV):
            g = g + p_ref[j]
        o_ref[...] = g

    return _pcall(body, name=name, out_shape=SDS((r, c), f32),
                  in_specs=[pl.BlockSpec(memory_space=pltpu.VMEM), pl.BlockSpec(memory_space=pl.ANY)],
                  out_specs=pl.BlockSpec(memory_space=pltpu.VMEM), compiler_params=_cp())(parts, after)


def _adamw_small(gs, ws, ms, vs, name):
    n = len(ws)

    def body(*refs):
        for t in range(n):
            delta, nm, nv = _adamw_math(refs[n + t][...], refs[t][...], refs[2 * n + t][...], refs[3 * n + t][...])
            refs[4 * n + t][...] = delta
            refs[5 * n + t][...] = nm
            refs[6 * n + t][...] = nv

    out = [SDS(wt.shape, f32) for wt in ws] * 3
    res = _pcall(body, name=name, out_shape=out, compiler_params=_cp())(*gs, *ws, *ms, *vs)
    return list(res[:n]), list(res[n:2 * n]), list(res[2 * n:])


def _my_place():
    return lax.axis_index("x"), lax.axis_index("y"), lax.axis_index("c")


def _all_gather(xs, name, after=None):
    n = len(xs)

    def body(*refs):
        x_refs, out_refs = refs[:n], refs[-3 - n:-3]
        send_sems, recv_sems, local_sems = refs[-3:]
        mx, my, mc = _my_place()
        me, sibling = (mx, my, mc), (mx, my, 1 - mc)
        chips = [(1 - mx, my), (mx, 1 - my), (1 - mx, 1 - my)]

        def rows(t, px, py, pc):
            return out_refs[t].at[4 * px + 2 * py + pc]

        def copy(t, k, block, to, src=None):
            return pltpu.make_async_remote_copy(src_ref=rows(t, *block) if src is None else src, dst_ref=rows(t, *block),
                                                send_sem=send_sems.at[7 * t + k], recv_sem=recv_sems.at[7 * t + k],
                                                device_id=to, device_id_type=MESH)

        mine = [pltpu.make_async_copy(x_refs[t], rows(t, *me), local_sems.at[t]) for t in range(n)]
        first, passed = [], []
        for t in range(n):
            mine[t].start()
            first += [copy(t, 0, me, sibling, src=x_refs[t])] + [copy(t, 1 + j, me, (*chip, mc), src=x_refs[t]) for j, chip in enumerate(chips)]
        for cp in first:
            cp.start()
        for t in range(n):
            for j, chip in enumerate(chips):
                copy(t, 1 + j, (*chip, mc), me).wait_recv()
                passed.append(copy(t, 4 + j, (*chip, mc), sibling))
                passed[-1].start()
        for t in range(n):
            copy(t, 0, sibling, me).wait_recv()
            for j, chip in enumerate(chips):
                copy(t, 4 + j, (*chip, 1 - mc), me).wait_recv()
        for cp in first + passed:
            cp.wait_send()
        for t in range(n):
            mine[t].wait()

    ins = list(xs) + ([after] if after is not None else [])
    anywhere = pl.BlockSpec(memory_space=pl.ANY)
    return list(_pcall(body, name=name, out_shape=[SDS((N_DEV,) + x.shape, x.dtype) for x in xs],
                       in_specs=[anywhere] * len(ins), out_specs=[anywhere] * n,
                       scratch_shapes=[pltpu.SemaphoreType.DMA((7 * n,)), pltpu.SemaphoreType.DMA((7 * n,)),
                                       pltpu.SemaphoreType.DMA((n,))])(*ins))


HBM_SPEC = pl.BlockSpec(memory_space=pltpu.HBM)
SEM_SPEC = pl.BlockSpec(memory_space=pltpu.SEMAPHORE)
ANY_SPEC = pl.BlockSpec(memory_space=pl.ANY)
DATAFLOW = pltpu.SideEffectType.DATAFLOW_SIDE_EFFECTING


def _peers():
    mx, my, mc = _my_place()
    out = []
    for k in range(1, N_DEV):
        tx, ty, tc = mx ^ ((k >> 2) & 1), my ^ ((k >> 1) & 1), mc ^ (k & 1)
        out.append(((tx, ty, tc), 4 * tx + 2 * ty + tc))
    return out


def _in_hbm(arrays):
    return [pltpu.with_memory_space_constraint(a, pltpu.HBM) for a in arrays]


N_FIRST_LEG, N_SECOND_LEG = 4, 3


def _chip_peers():
    mx, my, mc = _my_place()
    return [((tx, ty, mc), 4 * tx + 2 * ty + mc) for tx, ty in ((1 - mx, my), (mx, 1 - my), (1 - mx, 1 - my))]


def _gather_start(bufs, after, name):
    n = len(bufs)

    def body(*refs):
        ins, s_sem, r_sem, token = refs[:n], refs[n + 1], refs[n + 2], refs[-1]
        mx, my, mc = _my_place()
        me = 4 * mx + 2 * my + mc
        for t in range(n):
            for dev in [(mx, my, 1 - mc)] + [d for d, _ in _chip_peers()]:
                pltpu.make_async_remote_copy(src_ref=ins[t].at[me], dst_ref=ins[t].at[me], send_sem=s_sem.at[t],
                                             recv_sem=r_sem.at[t], device_id=dev, device_id_type=MESH).start()
        token[...] = jnp.zeros_like(token)

    out_shape = ((pltpu.SemaphoreType.DMA((n,)), pltpu.SemaphoreType.DMA((n,))) + tuple(pltpu.HBM(b.shape, b.dtype) for b in bufs)
                 + (SDS((8, 128), f32),))
    out = _pcall(body, name=name, out_shape=out_shape, in_specs=(HBM_SPEC,) * n + (ANY_SPEC,),
                 out_specs=(SEM_SPEC, SEM_SPEC) + (HBM_SPEC,) * n + (pl.BlockSpec(memory_space=pltpu.VMEM),),
                 input_output_aliases={i: 2 + i for i in range(n)},
                 compiler_params=pltpu.CompilerParams(has_side_effects=DATAFLOW))(*_in_hbm(bufs), after)
    return (out[0], out[1]), list(out[2:2 + n]), out[-1]


def _gather_forward(bufs, sems, after, name, fresh=()):
    n, nf = len(bufs), len(fresh)
    both = list(bufs) + list(fresh)

    nsem = 4 if nf else 2

    def body(*refs):
        ins, news, s1, r1 = refs[:n], refs[n:n + nf], refs[n + nf], refs[n + nf + 1]
        s2, r2 = refs[n + nf + 3:n + nf + 5]
        sf, rf = refs[n + nf + 5:n + nf + 7] if nf else (None, None)
        token = refs[-1]
        mx, my, mc = _my_place()
        me, sibling = 4 * mx + 2 * my + mc, (mx, my, 1 - mc)
        for t in range(n):
            legs = ins[t].at[pl.ds(0, N_FIRST_LEG)]
            first = pltpu.make_async_remote_copy(src_ref=legs, dst_ref=legs, send_sem=s1.at[t], recv_sem=r1.at[t],
                                                 device_id=sibling, device_id_type=MESH)
            first.wait_send()
            first.wait_recv()
        for t in range(n):
            for _, idx in _chip_peers():
                pltpu.make_async_remote_copy(src_ref=ins[t].at[idx], dst_ref=ins[t].at[idx], send_sem=s2.at[t],
                                             recv_sem=r2.at[t], device_id=sibling, device_id_type=MESH).start()
        for t in range(nf):
            for dev in [sibling] + [d for d, _ in _chip_peers()]:
                pltpu.make_async_remote_copy(src_ref=news[t].at[me], dst_ref=news[t].at[me], send_sem=sf.at[t],
                                             recv_sem=rf.at[t], device_id=dev, device_id_type=MESH).start()
        token[...] = jnp.zeros_like(token)

    sem = lambda k: pltpu.SemaphoreType.DMA((k,))
    sem_shapes = (sem(n), sem(n)) + ((sem(nf), sem(nf)) if nf else ())
    out_shape = sem_shapes + tuple(pltpu.HBM(b.shape, b.dtype) for b in both) + (SDS((8, 128), f32),)
    out = _pcall(body, name=name, out_shape=out_shape, in_specs=(HBM_SPEC,) * (n + nf) + (SEM_SPEC, SEM_SPEC, ANY_SPEC),
                 out_specs=(SEM_SPEC,) * nsem + (HBM_SPEC,) * (n + nf) + (pl.BlockSpec(memory_space=pltpu.VMEM),),
                 input_output_aliases={i: nsem + i for i in range(n + nf)},
                 compiler_params=pltpu.CompilerParams(has_side_effects=DATAFLOW))(*bufs, *_in_hbm(fresh), sems[0], sems[1], after)
    fresh_sems = (out[2], out[3]) if nf else None
    return (out[0], out[1]), list(out[nsem:nsem + n]), fresh_sems, list(out[nsem + n:nsem + n + nf]), out[-1]


def _exchange_start(sends, lands, name, after=None):
    nt = len(sends)
    extra = [] if after is None else [after]

    def body(*refs):
        srcs, dsts, token = refs[:nt], refs[nt:2 * nt], refs[-1]
        s_sem, r_sem = refs[2 * nt + len(extra)], refs[2 * nt + len(extra) + 1]
        mx, my, mc = _my_place()
        me = 4 * mx + 2 * my + mc
        for t in range(nt):
            for dev, idx in _peers():
                pltpu.make_async_remote_copy(src_ref=srcs[t].at[idx], dst_ref=dsts[t].at[me], send_sem=s_sem.at[t],
                                             recv_sem=r_sem.at[t], device_id=dev, device_id_type=MESH).start()
        token[...] = jnp.zeros_like(token)

    both = list(sends) + list(lands)
    out_shape = (pltpu.SemaphoreType.DMA((nt,)), pltpu.SemaphoreType.DMA((nt,))) + tuple(pltpu.HBM(b.shape, b.dtype) for b in both)
    out = _pcall(body, name=name, out_shape=out_shape + (SDS((8, 128), f32),), in_specs=(HBM_SPEC,) * (2 * nt) + (ANY_SPEC,) * len(extra),
                 out_specs=(SEM_SPEC, SEM_SPEC) + (HBM_SPEC,) * (2 * nt) + (pl.BlockSpec(memory_space=pltpu.VMEM),),
                 input_output_aliases={i: 2 + i for i in range(2 * nt)},
                 compiler_params=pltpu.CompilerParams(has_side_effects=DATAFLOW))(*_in_hbm(both), *extra)
    return (out[0], out[1]), list(out[2:2 + nt]), list(out[2 + nt:2 + 2 * nt]), out[-1]


def _copies_wait(srcs, dsts, sems, after, name, ncopies=N_DEV - 1):
    nt = len(srcs)
    same = srcs is dsts
    bufs = list(srcs) if same else list(srcs) + list(dsts)
    nb = len(bufs)

    def body(*refs):
        s_sem, r_sem = refs[nb], refs[nb + 1]
        mx, my, mc = _my_place()
        for t in range(nt):
            src = refs[t].at[pl.ds(0, ncopies)]
            dst = src if same else refs[nt + t].at[pl.ds(0, ncopies)]
            cp = pltpu.make_async_remote_copy(src_ref=src, dst_ref=dst, send_sem=s_sem.at[t], recv_sem=r_sem.at[t],
                                              device_id=(mx, my, 1 - mc), device_id_type=MESH)
            cp.wait_send()
            cp.wait_recv()

    out = _pcall(body, name=name, out_shape=tuple(pltpu.HBM(b.shape, b.dtype) for b in bufs),
                 in_specs=(HBM_SPEC,) * nb + (SEM_SPEC, SEM_SPEC, ANY_SPEC), out_specs=(HBM_SPEC,) * nb,
                 input_output_aliases={i: i for i in range(nb)},
                 compiler_params=pltpu.CompilerParams(has_side_effects=DATAFLOW))(*bufs, sems[0], sems[1], after)
    out = list(out)
    return (out, out) if same else (out[:nt], out[nt:])


def _cast_blocks(weights, l, me, after, name):
    nt = len(weights)
    halves = 2
    extra = [] if after is None else [after]

    def body(me_ref, *refs):
        outs = refs[nt + len(extra):]
        for t in range(nt):
            outs[t][...] = refs[t][...].astype(bf16)

    in_specs, out_specs, out_shape = [], [], []
    for wt in weights:
        _, r, c = wt.shape
        tr = r // halves
        in_specs.append(pl.BlockSpec((None, tr, c), lambda i, me_ref: (l, i, 0)))
        out_specs.append(pl.BlockSpec((None, tr, c), lambda i, me_ref: (me_ref[0], i, 0)))
        out_shape.append(SDS((N_DEV, r, c), bf16))
    in_specs += [pl.BlockSpec(memory_space=pl.ANY)] * len(extra)
    grid_spec = pltpu.PrefetchScalarGridSpec(num_scalar_prefetch=1, grid=(halves,), in_specs=in_specs, out_specs=out_specs)
    return list(_pcall(body, name=name, out_shape=out_shape, grid_spec=grid_spec, compiler_params=_cp(("parallel",)))(me, *weights, *extra))

def _layer_fwd_mix(x, p, bias, l, after=None):
    tag = f"l{l}"
    z = _mm(x, p["w_in"], "nt", 1024, 1408, f"{tag}_mm_in", after=after, norm_a=p["norm_mix_g"])
    yb, lt = _attn_fwd(z, bias, f"{tag}_attn_fwd")
    y = _mixer_fwd(z, yb, p["conv_a_w"], p["conv_c_w"], p["conv_c_b"], p["ln_c_g"], p["ln_c_b"], p["out_norm_g"], f"{tag}_mixer_fwd")
    return y, dict(x=x, z=z, yb=yb, lt=lt, y=y)


def _layer_fwd_ffn(x, y, p, l, after=None):
    tag = f"l{l}"
    x1 = _mm(y, p["w_out"], "nn", 512, 1024, f"{tag}_mm_out", res=x, after=after)
    upre = _mm(x1, p["w_up"], "nt", 1024, 1408, f"{tag}_mm_up", norm_a=p["norm_ffn_g"])
    act = _ffn_act_fwd(upre, p["conv_f_w"], f"{tag}_ffn_act")
    x2 = _mm(act, p["w_down"], "nn", 512, 1024, f"{tag}_mm_down", res=x1)
    return x2, dict(x1=x1, upre=upre, act=act)


def _small_stacks(depth, d_model, d_ff):
    shapes = dict(norm_ffn=(depth, 1, d_model), norm_mix=(depth, 1, d_model), conv_f_gate=(depth, 8, d_ff),
                  conv_f_val=(depth, 8, d_ff), mixers=(depth, PG_ROWS, 256), og_b=(depth, 1, 512))
    return {k: lax.empty(shp, f32) for k, shp in shapes.items()}


def _small_grads(st):
    pg = st["mixers"]
    row = lambda r: pg[:, r:r + 1, :]
    return dict(norm_ffn_g=st["norm_ffn"], norm_mix_g=st["norm_mix"],
                conv_f_w=jnp.concatenate([st["conv_f_gate"][:, :K_FFN], st["conv_f_val"][:, :K_FFN]], axis=2),
                conv_a_w=pg[:, PG_CONV_A:PG_CONV_A + K_SHORT], conv_c_w=pg[:, PG_CONV_C:PG_CONV_C + K_CONF],
                conv_c_b=row(PG_CB), ln_c_g=row(PG_LNG), ln_c_b=row(PG_LNB),
                out_norm_g=jnp.concatenate([row(PG_OG_A), st["og_b"], row(PG_OG_C)], axis=2))


def _layer_bwd_ffn(dx2, p, sv, l, st, after=None):
    tag = f"l{l}"
    g, st = {}, dict(st)
    dact = _mm(dx2, p["w_down"], "nt", 512, 1408, f"{tag}_mm_dact", after=after, b_outer=True)
    dwd = _mm(sv["act"], dx2, "tn", 1408, 1024, f"{tag}_mm_dwdown", out_dtype=bf16)
    g["w_down"] = dwd.reshape(N_DEV, dwd.shape[0] // N_DEV, dwd.shape[1])
    dug, duv, st["conv_f_gate"], st["conv_f_val"] = _ffn_act_bwd(sv["upre"], dact, p["conv_f_w"], (st["conv_f_gate"], st["conv_f_val"]),
                                                                 l, f"{tag}_ffn_act_bwd")
    dwu = _mm_tn_pieces([dug, duv], sv["x1"], p["norm_ffn_g"], 1408, f"{tag}_mm_dwup")
    g["w_up"] = dwu.reshape(N_DEV, dwu.shape[0] // N_DEV, dwu.shape[1])
    dx1, st["norm_ffn"] = _mm([dug, duv], p["w_up"], "nn", 512, 1024, f"{tag}_mm_dh2",
                              rms_bwd=(sv["x1"], p["norm_ffn_g"], dx2, st["norm_ffn"], l))
    return dx1, g, st


def _layer_bwd_mix(dx1, p, sv, bias, l, st, after=None):
    tag = f"l{l}"
    g, st = {}, dict(st)
    dy = _mm(dx1, p["w_out"], "nt", 512, 1024, f"{tag}_mm_dy", after=after)
    dwo = _mm(sv["y"], dx1, "tn", 512, 1024, f"{tag}_mm_dwout", out_dtype=bf16)
    g["w_out"] = dwo.reshape(N_DEV, dwo.shape[0] // N_DEV, dwo.shape[1])
    dz, dyb, st["mixers"], st["og_b"] = _mixer_bwd(sv["z"], sv["yb"], dy, p["conv_a_w"], p["conv_c_w"], p["conv_c_b"], p["ln_c_g"],
                                                   p["ln_c_b"], p["out_norm_g"], (st["mixers"], st["og_b"]), l, f"{tag}_mixer_bwd")
    dz, tabs = _attn_bwd(sv["z"], dyb, sv["yb"], sv["lt"], bias, dz, f"{tag}_attn_bwd")
    dwi = _mm(dz, sv["x"], "tn", 1408, 1024, f"{tag}_mm_dwin", out_dtype=bf16, norm_b=p["norm_mix_g"])
    g["w_in"] = dwi.reshape(N_DEV, dwi.shape[0] // N_DEV, dwi.shape[1])
    dx, st["norm_mix"] = _mm(dz, p["w_in"], "nn", 512, 1024, f"{tag}_mm_dh", rms_bwd=(sv["x"], p["norm_mix_g"], dx1, st["norm_mix"], l))
    return dx, g, tabs, st


SMALL_LAYER_PARAMS = ("norm_mix_g", "conv_a_w", "conv_c_w", "conv_c_b", "ln_c_g", "ln_c_b", "out_norm_g", "norm_ffn_g", "conv_f_w")


BIG = ("w_in", "w_out", "w_up", "w_down")
TRANSPOSED = ("w_in", "w_up")
ADAM_ROWS = {"w_in": 176, "w_out": 128, "w_up": 176, "w_down": 176}
SMALL_SHARDED = ("conv_a_w", "conv_c_w", "conv_f_w")
SMALL_REPLICATED = ("norm_mix_g", "conv_c_b", "ln_c_g", "ln_c_b", "out_norm_g", "norm_ffn_g", "rel_bias", "final_g")
WEIGHTS = ("norm_mix_g", "w_in", "conv_a_w", "conv_c_w", "conv_c_b", "ln_c_g", "ln_c_b", "out_norm_g", "w_out",
           "norm_ffn_g", "w_up", "conv_f_w", "w_down", "rel_bias", "final_g")


def _pack(arrays):
    flat = jnp.concatenate([a.reshape(-1) for a in arrays])
    rows = -(-flat.shape[0] // 1024) * 8
    return jnp.pad(flat, (0, rows * 128 - flat.shape[0])).reshape(rows, 128)


def _unpack(packed, shapes):
    flat, out, at = packed.reshape(-1), [], 0
    for shp in shapes:
        n = int(np.prod(shp))
        out.append(flat[at:at + n].reshape(shp))
        at += n
    return out


def _gather_shards(g, local_shape):
    nd = len(local_shape)
    return jnp.moveaxis(g, 0, nd - 1).reshape(local_shape[:-1] + (N_DEV * local_shape[-1],))


def kernel(x, norm_mix_g, w_in, conv_a_w, conv_c_w, conv_c_b, ln_c_g, ln_c_b, out_norm_g, w_out, norm_ffn_g, w_up, conv_f_w, w_down, rel_bias, final_g, loss_target, m_norm_mix_g, m_w_in, m_conv_a_w, m_conv_c_w, m_conv_c_b, m_ln_c_g, m_ln_c_b, m_out_norm_g, m_w_out, m_norm_ffn_g, m_w_up, m_conv_f_w, m_w_down, m_rel_bias, m_final_g, v_norm_mix_g, v_w_in, v_conv_a_w, v_conv_c_w, v_conv_c_b, v_ln_c_g, v_ln_c_b, v_out_norm_g, v_w_out, v_norm_ffn_g, v_w_up, v_conv_f_w, v_w_down, v_rel_bias, v_final_g):
    w = dict(norm_mix_g=norm_mix_g, w_in=w_in, conv_a_w=conv_a_w, conv_c_w=conv_c_w, conv_c_b=conv_c_b, ln_c_g=ln_c_g,
             ln_c_b=ln_c_b, out_norm_g=out_norm_g, w_out=w_out, norm_ffn_g=norm_ffn_g, w_up=w_up, conv_f_w=conv_f_w,
             w_down=w_down, rel_bias=rel_bias, final_g=final_g)
    mom = dict(norm_mix_g=m_norm_mix_g, w_in=m_w_in, conv_a_w=m_conv_a_w, conv_c_w=m_conv_c_w, conv_c_b=m_conv_c_b,
               ln_c_g=m_ln_c_g, ln_c_b=m_ln_c_b, out_norm_g=m_out_norm_g, w_out=m_w_out, norm_ffn_g=m_norm_ffn_g,
               w_up=m_w_up, conv_f_w=m_conv_f_w, w_down=m_w_down, rel_bias=m_rel_bias, final_g=m_final_g)
    var = dict(norm_mix_g=v_norm_mix_g, w_in=v_w_in, conv_a_w=v_conv_a_w, conv_c_w=v_conv_c_w, conv_c_b=v_conv_c_b,
               ln_c_g=v_ln_c_g, ln_c_b=v_ln_c_b, out_norm_g=v_out_norm_g, w_out=v_w_out, norm_ffn_g=v_norm_ffn_g,
               w_up=v_w_up, conv_f_w=v_conv_f_w, w_down=v_w_down, rel_bias=v_rel_bias, final_g=v_final_g)
    depth = w_in.shape[0]
    me = 4 * lax.axis_index("x") + 2 * lax.axis_index("y") + lax.axis_index("c")
    for k in TRANSPOSED:
        w[k], mom[k], var[k] = (jnp.transpose(a, (0, 2, 1)) for a in (w[k], mom[k], var[k]))

    me_arr = me.astype(jnp.int32).reshape(1)
    bufs = [_cast_blocks([w[k] for k in BIG], 0, me_arr, None, "l0_cast_blocks")]
    in_sems, in_bufs, token = _gather_start(bufs[0][:1], me_arr, "l0_gather_start_in")
    bufs += [_cast_blocks([w[k] for k in BIG], l, me_arr, token, f"l{l}_cast_blocks") for l in range(1, depth)]
    params = {}
    gathered = _all_gather([w[k] for k in SMALL_SHARDED], "gather_small", after=token)
    gs = gathered[0]
    for k, g8 in zip(SMALL_SHARDED, gathered):
        params[k] = _gather_shards(g8, w[k].shape)
    for k in ("norm_mix_g", "conv_c_b", "ln_c_g", "ln_c_b", "out_norm_g", "norm_ffn_g"):
        params[k] = w[k][:, None, :]
    params["rel_bias"] = rel_bias
    params["final_g"] = final_g[None, :]

    def full(k, buf, l):
        return buf.reshape(N_DEV * buf.shape[1], buf.shape[2])

    bias = _bias_tables(rel_bias, "bias_tables").reshape(len(DILATIONS), 2, 4, 2 * BLK, 2 * BLK)
    in_sems, in_bufs, rest_sems, rest_bufs, token = _gather_forward(in_bufs, in_sems, gs, "l0_gather_forward_in", fresh=bufs[0][1:])
    xl, after = x[0], token
    layer_params, saved = [], []
    for l in range(depth):
        _, in_bufs = _copies_wait(in_bufs, in_bufs, in_sems, after, f"l{l}_gather_wait_in", N_SECOND_LEG)
        p = {k: params[k][l] for k in SMALL_LAYER_PARAMS}
        p["w_in"] = full("w_in", in_bufs[0], l)
        token = in_bufs[0]
        nxt_in = bufs[l + 1][:1] if l + 1 < depth else []
        if l > 0:
            rest_sems, rest_bufs, in_sems, in_bufs, token = _gather_forward(rest_bufs, rest_sems, token, f"l{l}_gather_forward_rest", fresh=nxt_in)
        else:
            in_sems, in_bufs, token = _gather_start(nxt_in, token, "l1_gather_start_in")
        y, sv = _layer_fwd_mix(xl, p, bias, l, after=token)
        if l == 0:
            rest_sems, rest_bufs, _, _, _ = _gather_forward(rest_bufs, rest_sems, y, "l0_gather_forward_rest")
        _, rest_bufs = _copies_wait(rest_bufs, rest_bufs, rest_sems, y, f"l{l}_gather_wait_rest", N_SECOND_LEG)
        for k, buf in zip(BIG[1:], rest_bufs):
            p[k] = full(k, buf, l)
        token = rest_bufs[0]
        if l + 1 < depth:
            in_sems, in_bufs, rest_sems, rest_bufs, token = _gather_forward(in_bufs, in_sems, token, f"l{l + 1}_gather_forward_in",
                                                                             fresh=bufs[l + 1][1:])
        xl, sv2 = _layer_fwd_ffn(xl, y, p, l, after=token)
        layer_params.append(p)
        saved.append({**sv, **sv2})
        after = xl
    loss_tile, dx, d_final = _final_loss(xl, params["final_g"], loss_target[0], "final_loss")

    tabs, pending = [None] * depth, []
    after = None
    stacks = _small_stacks(depth, x.shape[-1], w_down.shape[1] * N_DEV)

    def exchange(l, part, names, g, after=None):
        sends = [g[k] for k in names]
        sems, sends, lands, token = _exchange_start(sends, [lax.empty(s.shape, s.dtype) for s in sends],
                                                    f"l{l}_exchange_start_{part}", after=after)
        pending.append((l, part, names, sems, sends, lands))
        return token

    for l in reversed(range(depth)):
        dx1, g_ffn, stacks = _layer_bwd_ffn(dx, layer_params[l], saved[l], l, stacks, after=after)
        after = exchange(l, "ffn", ("w_up", "w_down"), g_ffn)
        dx, g_mix, tabs[l], stacks = _layer_bwd_mix(dx1, layer_params[l], saved[l], bias, l, stacks, after=after)
        if l > 0:
            after = exchange(l, "mix", ("w_out", "w_in"), g_mix)
    grads_small = _small_grads(stacks)
    grads_small["rel_bias"] = _bias_grad(jnp.stack(tabs).reshape(depth, len(DILATIONS), 8, BLK, 2 * BLK), "bias_grad")[:, :8]
    grads_small["final_g"] = d_final

    out_g, out_d, out_m, out_v = {}, {}, {}, {}
    grads = grads_small
    small = SMALL_SHARDED + SMALL_REPLICATED
    full_shapes = [params[k].shape if k in SMALL_SHARDED else w[k].shape for k in small]
    packed = _pack([grads[k].reshape(shp) for k, shp in zip(small, full_shapes)] + [loss_tile[0, :1]])
    gathered_small = _all_gather([packed], "gather_small_grads", after=after)[0]
    token = exchange(0, "mix", ("w_out", "w_in"), g_mix, after=gathered_small)
    gsum = _sum_parts(gathered_small, token, "sum_small_grads")
    *summed, loss = _unpack(gsum, full_shapes + [(1,)])
    loss = loss[0]
    gfull = dict(zip(small, summed))
    for k in SMALL_SHARDED:
        shp = w[k].shape
        gfull[k] = lax.dynamic_index_in_dim(gfull[k].reshape(shp[:-1] + (N_DEV, shp[-1])), me, axis=len(shp) - 1, keepdims=False)
    two_d = lambda a: a.reshape(1, -1) if a.ndim == 1 else a
    res = _adamw_small(*[[two_d(src[k]) for k in small] for src in (gfull, w, mom, var)], "adamw_small")
    for name_map, outs in zip((out_d, out_m, out_v), res):
        name_map.update({k: o.reshape(w[k].shape) for k, o in zip(small, outs)})
    out_g.update(gfull)

    results = {k: [lax.empty(w[k].shape, f32) for _ in range(4)] for k in BIG}
    after = res[0][0]
    for l, part, names, sems, sends, lands in pending:
        sends, lands = _copies_wait(sends, lands, sems, after, f"l{l}_exchange_wait_{part}")
        for k, send, land in zip(names, sends, lands):
            results[k] = _adamw_layer(land, send, w[k], mom[k], var[k], results[k], l, ADAM_ROWS[k], me_arr, f"l{l}_adamw_{k}")
            after = results[k][0]
    for k in BIG:
        big = [jnp.transpose(a, (0, 2, 1)) for a in results[k]] if k in TRANSPOSED else results[k]
        out_g[k], out_d[k], out_m[k], out_v[k] = big

    return (loss, dx[None], *[out_g[k] for k in WEIGHTS], *[out_d[k] for k in WEIGHTS], *[out_m[k] for k in WEIGHTS],
            *[out_v[k] for k in WEIGHTS])
```

```python
import math

import numpy as np
import jax
import jax.numpy as jnp
from jax import lax
from jax.experimental import pallas as pl
from jax.experimental.pallas import tpu as pltpu

f32, bf16 = jnp.float32, jnp.bfloat16
SDS = jax.ShapeDtypeStruct
MESH = pl.DeviceIdType.MESH

N_DEV = 8
EPS = 1e-6
NEG = -1e30
D_HEAD = 64
BLK = 128
N_BUCKETS = 32
DILATIONS = (1, 4, 16)
N_KEYS = 128
K_SHORT, K_CONF, K_FFN = 3, 31, 3
ROW_TILE = 512
MIXER_ROWS = 512
HALO = 32
HALO_FFN = 8
FFN_FWD_ROWS, FFN_BWD_ROWS = 1024, 1024
FFN_CHUNK = 128
VMEM_LIMIT = 56 << 20
ADAM_LR, ADAM_B1, ADAM_B2, ADAM_EPS, ADAM_WD, ADAM_STEP = 0.001, 0.9, 0.999, 1e-08, 0.01, 10


def _pcall(body, **kw):
    return pl.pallas_call(body, **kw)


def _cp(sem=None):
    kw = dict(vmem_limit_bytes=VMEM_LIMIT)
    if sem is not None:
        kw["dimension_semantics"] = sem
    return pltpu.CompilerParams(**kw)


def _t5_buckets():
    rel = np.arange(BLK)[:, None] - np.arange(2 * BLK)[None, :] + BLK
    out = []
    for d in DILATIONS:
        dist = np.maximum(rel, 0) * d
        max_exact = N_BUCKETS // 2
        d_f = np.maximum(dist, 1).astype(np.float32)
        large = max_exact + (np.log(d_f / np.float32(max_exact)) / np.float32(math.log(2048 / max_exact))
                             * np.float32(N_BUCKETS - max_exact)).astype(np.int32)
        large = np.minimum(large, N_BUCKETS - 1)
        out.append(np.where(dist < max_exact, dist, large).astype(np.int32))
    return np.stack(out)


_BUCKETS = _t5_buckets()


def _mm(a, b, mode, tr, tc, name, res=None, out_dtype=f32, after=None, b_outer=False, norm_a=None, norm_b=None, rms_bwd=None):
    if rms_bwd is not None:
        return _mm_rms_bwd(a, b, tr, name, *rms_bwd)
    ij = (lambda f: (lambda j, i: f(i, j))) if b_outer else (lambda f: f)
    if mode == "nn":
        (m, k), n = a.shape, b.shape[1]
        a_spec = pl.BlockSpec((tr, k), ij(lambda i, j: (i, 0)))
        b_spec = pl.BlockSpec((k, tc), ij(lambda i, j: (0, j)))
        dims = (((1,), (0,)), ((), ()))
    elif mode == "nt":
        (m, k), n = a.shape, b.shape[0]
        a_spec = pl.BlockSpec((tr, k), ij(lambda i, j: (i, 0)))
        b_spec = pl.BlockSpec((tc, k), ij(lambda i, j: (j, 0)))
        dims = (((1,), (1,)), ((), ()))
    else:
        (k, m), n = a.shape, b.shape[1]
        a_spec = pl.BlockSpec((k, tr), ij(lambda i, j: (0, i)))
        b_spec = pl.BlockSpec((k, tc), ij(lambda i, j: (0, j)))
        dims = (((0,), (0,)), ((), ()))
    assert m % tr == 0 and n % tc == 0, (name, m, n, tr, tc)
    o_spec = pl.BlockSpec((tr, tc), ij(lambda i, j: (i, j)))
    grid = (n // tc, m // tr) if b_outer else (m // tr, n // tc)

    gain = norm_a if norm_a is not None else norm_b
    assert norm_b is None or (mode == "tn" and tc == n and not b_outer)
    assert norm_a is None or (mode != "tn" and not b_outer)

    def body(a_ref, b_ref, *rest):
        o_ref = rest[-2] if gain is not None else rest[-1]
        if norm_a is not None:
            ha_sc = rest[-1]

            @pl.when(pl.program_id(1) == 0)
            def _():
                av = a_ref[...]
                ha_sc[...] = (av * _rstd(av) * rest[0][...]).astype(bf16)

            av = ha_sc[...]
        else:
            av = a_ref[...].astype(bf16)
        if norm_b is not None:
            hb_sc = rest[-1]

            @pl.when(pl.program_id(0) == 0)
            def _():
                bv = b_ref[...]
                hb_sc[...] = (bv * _rstd(bv) * rest[0][...]).astype(bf16)

            bv = hb_sc[...]
        else:
            bv = b_ref[...].astype(bf16)
        acc = lax.dot_general(av, bv, dims, preferred_element_type=f32)
        if res is not None:
            acc = acc + rest[1 if gain is not None else 0][...]
        o_ref[...] = acc.astype(out_dtype)

    vec = pl.BlockSpec((1, k if norm_a is not None else n), lambda i, j: (0, 0))
    ins = [a, b] + ([gain] if gain is not None else []) + ([res] if res is not None else []) + ([after] if after is not None else [])
    specs = ([a_spec, b_spec] + ([vec] if gain is not None else []) + ([o_spec] if res is not None else [])
             + ([pl.BlockSpec(memory_space=pl.ANY)] if after is not None else []))
    scratch = [pltpu.VMEM((k, n), bf16)] if norm_b is not None else [pltpu.VMEM((tr, k), bf16)] if norm_a is not None else []
    sem = ("arbitrary", "arbitrary") if gain is not None else ("parallel", "parallel")
    return _pcall(body, name=name, out_shape=SDS((m, n), out_dtype), grid=grid, in_specs=specs, out_specs=o_spec,
                  scratch_shapes=scratch, compiler_params=_cp(sem))(*ins)


def _mm_tn_pieces(pieces, b, gain, tr, name):
    k, n = b.shape
    widths = [pc.shape[1] for pc in pieces]
    assert all(wd % tr == 0 for wd in widths)
    starts = [sum(widths[:p]) // tr for p in range(len(pieces))]
    counts = [wd // tr for wd in widths]
    npc = len(pieces)

    def body(*refs):
        a_refs, b_ref, g_ref, o_ref, hb_sc = refs[:npc], refs[npc], refs[npc + 1], refs[npc + 2], refs[npc + 3]
        i = pl.program_id(0)

        @pl.when(i == 0)
        def _():
            bv = b_ref[...]
            hb_sc[...] = (bv * _rstd(bv) * g_ref[...]).astype(bf16)

        for p in range(npc):
            @pl.when(jnp.logical_and(i >= starts[p], i < starts[p] + counts[p]))
            def _(p=p):
                o_ref[...] = lax.dot_general(a_refs[p][...].astype(bf16), hb_sc[...], (((0,), (0,)), ((), ())),
                                             preferred_element_type=f32).astype(bf16)

    a_specs = [pl.BlockSpec((k, tr), lambda i, p=p: (0, jnp.clip(i - starts[p], 0, counts[p] - 1))) for p in range(npc)]
    return _pcall(body, name=name, out_shape=SDS((sum(widths), n), bf16), grid=(sum(counts),),
                  in_specs=a_specs + [pl.BlockSpec((k, n), lambda i: (0, 0)), pl.BlockSpec((1, n), lambda i: (0, 0))],
                  out_specs=pl.BlockSpec((tr, n), lambda i: (i, 0)), scratch_shapes=[pltpu.VMEM((k, n), bf16)],
                  compiler_params=_cp(("arbitrary",)))(*pieces, b, gain)


def _mm_rms_bwd(a, b, tr, name, x, g, dres, stack, l):
    pieces = list(a) if isinstance(a, (list, tuple)) else [a]
    m, n, npc = pieces[0].shape[0], b.shape[1], len(pieces)
    k = sum(pc.shape[1] for pc in pieces)

    def body(*refs):
        a_refs, (b_ref, x_ref, g_ref, dres_ref, stack_ref, dx_ref, dg_ref) = refs[:npc], refs[npc:]
        dh, at = 0.0, 0
        for a_ref in a_refs:
            kp = a_ref.shape[1]
            dh = dh + jnp.dot(a_ref[...].astype(bf16), b_ref[pl.ds(at, kp), :].astype(bf16), preferred_element_type=f32)
            at += kp
        xv = x_ref[...]
        r = _rstd(xv)
        dx_ref[...] = dres_ref[...] + _rms_dx(xv, r, g_ref[...], dh)

        @pl.when(pl.program_id(0) == 0)
        def _():
            dg_ref[...] = jnp.zeros_like(dg_ref)

        dg_ref[...] += _colsum(dh * xv * r)

    row = pl.BlockSpec((tr, n), lambda i: (i, 0))
    vec = pl.BlockSpec((1, n), lambda i: (0, 0))
    a_specs = [pl.BlockSpec((tr, pc.shape[1]), lambda i: (i, 0)) for pc in pieces]
    return _pcall(body, name=name, out_shape=(SDS((m, n), f32), SDS(stack.shape, f32)), grid=(m // tr,),
                  in_specs=a_specs + [pl.BlockSpec((k, n), lambda i: (0, 0), pipeline_mode=pl.Buffered(1)),
                                      row, vec, row, pl.BlockSpec(memory_space=pl.ANY)],
                  out_specs=(row, pl.BlockSpec((None, 1, n), lambda i: (l, 0, 0))), input_output_aliases={npc + 4: 1},
                  compiler_params=_cp(("arbitrary",)))(*pieces, b, x, g, dres, stack)


def _rstd(x):
    return lax.rsqrt(jnp.mean(x * x, axis=-1, keepdims=True) + EPS)


def _rms_dx(x, r, g, dy):
    dxh = dy * g
    return r * (dxh - x * (r * r) * jnp.mean(dxh * x, axis=-1, keepdims=True))


def _sigmoid(x):
    return 0.5 * jnp.tanh(0.5 * x) + 0.5


def _colsum(x):
    return jnp.sum(x, axis=0, keepdims=True)


def _final_loss(x, g, target, name):
    s, d = x.shape

    def body(x_ref, g_ref, t_ref, loss_ref, dx_ref, dg_ref):
        xv, gv = x_ref[...], g_ref[...]
        r = _rstd(xv)
        diff = xv * r * gv - t_ref[...]
        dy = diff * (1.0 / d)

        @pl.when(pl.program_id(0) == 0)
        def _():
            dg_ref[...] = jnp.zeros_like(dg_ref)
            loss_ref[...] = jnp.zeros_like(loss_ref)

        part = jnp.sum(jnp.sum(diff * diff, axis=1, keepdims=True), axis=0, keepdims=True) * (0.5 / d)
        loss_ref[...] += part
        dx_ref[...] = _rms_dx(xv, r, gv, dy)
        dg_ref[...] += _colsum(dy * xv * r)

    row = pl.BlockSpec((ROW_TILE, d), lambda i: (i, 0))
    vec = pl.BlockSpec((1, d), lambda i: (0, 0))
    one = pl.BlockSpec((8, 128), lambda i: (0, 0))
    return _pcall(body, name=name, out_shape=(SDS((8, 128), f32), SDS((s, d), f32), SDS((1, d), f32)),
                  grid=(s // ROW_TILE,), in_specs=[row, vec, row], out_specs=(one, row, vec),
                  compiler_params=_cp(("arbitrary",)))(x, g, target)


Z_AH, Z_AB, Z_AC, Z_CV, Z_CG = 0, 1, 2, 9, 10
Z_Q128, Z_K128, Z_V128 = 6, 10, 14


def _cur(t, w, col):
    return pl.BlockSpec((t, w), lambda i: (i, col))


def _prev(hb, t, w, col):
    return pl.BlockSpec((hb, w), lambda i: (jnp.maximum(i * (t // hb) - 1, 0), col))


def _next(hb, t, w, col, rows):
    return pl.BlockSpec((hb, w), lambda i: (jnp.minimum((i + 1) * (t // hb), rows // hb - 1), col))


def _layernorm_parts(u, g, b):
    mu = jnp.mean(u, axis=-1, keepdims=True)
    var = jnp.mean(jnp.square(u - mu), axis=-1, keepdims=True)
    rs = lax.rsqrt(var + EPS)
    xhat = (u - mu) * rs
    return xhat, rs, xhat * g + b


def _fill_shifted(rot):
    n = rot.shape[1]
    for r in range(1, 8):
        rot[r, pl.ds(0, n - 8), :] = rot[0, pl.ds(r, n - 8), :]


def _window(rot, offset, rows):
    return rot[offset % 8, pl.ds(offset - offset % 8, rows), :]


def _mixer_fwd(z, yb, conv_a, conv_c, cb, lng, lnb, og, name):
    s = z.shape[0]
    t, hb, w = MIXER_ROWS, HALO, 256

    def body(ah, ab, ac, cv, cg, ah_p, ac_p, cv_p, cg_p, yb_ref, wa, wc, cb_ref, lng_ref, lnb_ref, og_ref, y_ref, p_sc, u_sc):
        first = pl.program_id(0) == 0

        def prev(ref):
            return jnp.where(first, 0.0, ref[...])

        p_sc[pl.ds(0, hb), :] = prev(ah_p) * prev(ac_p)
        p_sc[pl.ds(hb, t), :] = ah[...] * ac[...]
        conv = jnp.zeros((t, w), f32)
        for k in range(K_SHORT):
            conv = conv + wa[pl.ds(k, 1), :] * p_sc[pl.ds(hb - (K_SHORT - 1) + k, t), :]
        y_a = ab[...] * conv
        u_sc[0, pl.ds(0, hb), :] = prev(cv_p) * _sigmoid(prev(cg_p))
        u_sc[0, pl.ds(hb, t), :] = cv[...] * _sigmoid(cg[...])
        _fill_shifted(u_sc)
        u = jnp.zeros((t, w), f32) + cb_ref[...]
        for k in range(K_CONF):
            u = u + wc[pl.ds(k, 1), :] * _window(u_sc, hb - (K_CONF - 1) + k, t)
        _, _, lnv = _layernorm_parts(u, lng_ref[...], lnb_ref[...])
        y_c = lnv * _sigmoid(lnv)
        ybv = yb_ref[...]
        y_ref[:, 0:256] = (y_a * _rstd(y_a) * og_ref[:, 0:256]).astype(bf16)
        y_ref[:, 256:768] = (ybv * _rstd(ybv) * og_ref[:, 256:768]).astype(bf16)
        y_ref[:, 768:1024] = (y_c * _rstd(y_c) * og_ref[:, 768:1024]).astype(bf16)

    full = lambda shape: pl.BlockSpec(shape, lambda i: tuple(0 for _ in shape))
    in_specs = [_cur(t, w, c) for c in (Z_AH, Z_AB, Z_AC, Z_CV, Z_CG)] + [_prev(hb, t, w, c) for c in (Z_AH, Z_AC, Z_CV, Z_CG)]
    in_specs += [_cur(t, 512, 0), full((K_SHORT, w)), full((K_CONF, w)), full((1, w)), full((1, w)), full((1, w)), full((1, 1024))]
    return _pcall(body, name=name, out_shape=SDS((s, 1024), bf16), grid=(s // t,), in_specs=in_specs,
                  out_specs=_cur(t, 1024, 0), scratch_shapes=[pltpu.VMEM((hb + t, w), f32), pltpu.VMEM((8, hb + t, w), f32)],
                  compiler_params=_cp(("parallel",)))(z, z, z, z, z, z, z, z, z, yb, conv_a, conv_c, cb, lng, lnb, og)


PG_CONV_A, PG_CONV_C, PG_CB, PG_LNG, PG_LNB, PG_OG_A, PG_OG_C, PG_ROWS = 0, 8, 40, 41, 42, 43, 44, 48


def _mixer_bwd(z, yb, dy, conv_a, conv_c, cb, lng, lnb, og, stacks, l, name):
    s = z.shape[0]
    t, hb, w = MIXER_ROWS, HALO, 256
    e = t + hb

    def body(ah, ab, ac, cv, cg, ah_p, ac_p, cv_p, cg_p, ah_n, ab_n, ac_n, cv_n, cg_n,
             dya, dyb1, dyb2, dyc, dya_n, dyc_n, yb_ref, wa, wc, cb_ref, lng_ref, lnb_ref, og_ref, pg_stack, dgb_stack,
             dz_ref, dyb_ref, pg_ref, dgb_ref, p_sc, u_sc, dc_sc, du_sc):
        i = pl.program_id(0)
        first, last = i == 0, i == pl.num_programs(0) - 1

        def prev(ref):
            return jnp.where(first, 0.0, ref[...])

        def nxt(ref):
            return jnp.where(last, 0.0, ref[...])

        def ext(cur_ref, next_ref):
            return jnp.concatenate([cur_ref[...], nxt(next_ref)], axis=0)

        @pl.when(first)
        def _():
            pg_ref[...] = jnp.zeros_like(pg_ref)
            dgb_ref[...] = jnp.zeros_like(dgb_ref)

        g_a = og_ref[:, 0:256]
        p_sc[pl.ds(0, hb), :] = prev(ah_p) * prev(ac_p)
        p_sc[pl.ds(hb, t), :] = ah[...] * ac[...]
        p_sc[pl.ds(hb + t, hb), :] = nxt(ah_n) * nxt(ac_n)
        conv = jnp.zeros((e, w), f32)
        for k in range(K_SHORT):
            conv = conv + wa[pl.ds(k, 1), :] * p_sc[pl.ds(hb - (K_SHORT - 1) + k, e), :]
        ab_e, dya_e = ext(ab, ab_n), ext(dya, dya_n)
        y_a = ab_e * conv
        r_a = _rstd(y_a)
        dy_a = _rms_dx(y_a, r_a, g_a, dya_e)
        dc_sc[...] = dy_a * ab_e
        dz_ref[:, 256:512] = (dy_a[:t] * conv[:t]).astype(bf16)
        dp = jnp.zeros((t, w), f32)
        for k in range(K_SHORT):
            dp = dp + wa[pl.ds(k, 1), :] * dc_sc[pl.ds(K_SHORT - 1 - k, t), :]
            pg_ref[pl.ds(PG_CONV_A + k, 1), :] += _colsum(dc_sc[pl.ds(0, t), :] * p_sc[pl.ds(hb - (K_SHORT - 1) + k, t), :])
        dz_ref[:, 0:256] = (dp * ac[...]).astype(bf16)
        dz_ref[:, 512:768] = (dp * ah[...]).astype(bf16)
        pg_ref[pl.ds(PG_OG_A, 1), :] += _colsum(dya[...] * y_a[:t] * r_a[:t])

        g_c = og_ref[:, 768:1024]
        u_sc[0, pl.ds(0, hb), :] = prev(cv_p) * _sigmoid(prev(cg_p))
        u_sc[0, pl.ds(hb, t), :] = cv[...] * _sigmoid(cg[...])
        u_sc[0, pl.ds(hb + t, hb), :] = nxt(cv_n) * _sigmoid(nxt(cg_n))
        _fill_shifted(u_sc)
        u = jnp.zeros((e, w), f32) + cb_ref[...]
        for k in range(K_CONF):
            u = u + wc[pl.ds(k, 1), :] * _window(u_sc, hb - (K_CONF - 1) + k, e)
        xhat, rs, lnv = _layernorm_parts(u, lng_ref[...], lnb_ref[...])
        sg = _sigmoid(lnv)
        y_c = lnv * sg
        r_c = _rstd(y_c)
        dyc_e = ext(dyc, dyc_n)
        dlnv = _rms_dx(y_c, r_c, g_c, dyc_e) * (sg * (1.0 + lnv * (1.0 - sg)))
        dxh = dlnv * lng_ref[...]
        du = rs * (dxh - jnp.mean(dxh, axis=-1, keepdims=True) - xhat * jnp.mean(dxh * xhat, axis=-1, keepdims=True))
        du_sc[0] = du
        _fill_shifted(du_sc)
        duin = jnp.zeros((t, w), f32)
        for k in range(K_CONF):
            duin = duin + wc[pl.ds(k, 1), :] * _window(du_sc, K_CONF - 1 - k, t)
            pg_ref[pl.ds(PG_CONV_C + k, 1), :] += _colsum(du[:t] * _window(u_sc, hb - (K_CONF - 1) + k, t))
        sgg = _sigmoid(cg[...])
        dz_ref[:, 2304:2560] = (duin * sgg).astype(bf16)
        dz_ref[:, 2560:2816] = (duin * cv[...] * sgg * (1.0 - sgg)).astype(bf16)
        pg_ref[pl.ds(PG_CB, 1), :] += _colsum(du[:t])
        pg_ref[pl.ds(PG_LNG, 1), :] += _colsum(dlnv[:t] * xhat[:t])
        pg_ref[pl.ds(PG_LNB, 1), :] += _colsum(dlnv[:t])
        pg_ref[pl.ds(PG_OG_C, 1), :] += _colsum(dyc[...] * y_c[:t] * r_c[:t])

        g_b = og_ref[:, 256:768]
        ybv = yb_ref[...]
        r_b = _rstd(ybv)
        dyb = jnp.concatenate([dyb1[...], dyb2[...]], axis=1)
        dyb_ref[...] = _rms_dx(ybv, r_b, g_b, dyb)
        dgb_ref[...] += _colsum(dyb * ybv * r_b)

    full = lambda shape: pl.BlockSpec(shape, lambda i: tuple(0 for _ in shape))
    zc = (Z_AH, Z_AB, Z_AC, Z_CV, Z_CG)
    in_specs = [_cur(t, w, c) for c in zc] + [_prev(hb, t, w, c) for c in (Z_AH, Z_AC, Z_CV, Z_CG)]
    in_specs += [_next(hb, t, w, c, s) for c in zc]
    in_specs += [_cur(t, w, c) for c in (0, 1, 2, 3)] + [_next(hb, t, w, 0, s), _next(hb, t, w, 3, s)]
    in_specs += [_cur(t, 512, 0), full((K_SHORT, w)), full((K_CONF, w)), full((1, w)), full((1, w)), full((1, w)), full((1, 1024))]
    in_specs += [pl.BlockSpec(memory_space=pl.ANY)] * 2
    out_shape = (SDS((s, z.shape[1]), bf16), SDS((s, 512), f32), SDS(stacks[0].shape, f32), SDS(stacks[1].shape, f32))
    out_specs = (_cur(t, z.shape[1], 0), _cur(t, 512, 0), pl.BlockSpec((None, PG_ROWS, w), lambda i: (l, 0, 0)),
                 pl.BlockSpec((None, 1, 512), lambda i: (l, 0, 0)))
    scratch = [pltpu.VMEM((hb + t + hb, w), f32), pltpu.VMEM((8, hb + t + hb, w), f32), pltpu.VMEM((e, w), f32), pltpu.VMEM((8, e, w), f32)]
    args = [z] * 14 + [dy] * 6 + [yb, conv_a, conv_c, cb, lng, lnb, og] + list(stacks)
    return _pcall(body, name=name, out_shape=out_shape, grid=(s // t,), in_specs=in_specs, out_specs=out_specs,
                  input_output_aliases={27: 2, 28: 3}, scratch_shapes=scratch, compiler_params=_cp(("arbitrary",)))(*args)


def _ffn_act_fwd(upre, conv_f, name):
    s, f2 = upre.shape
    t, hb, w = FFN_FWD_ROWS, HALO_FFN, 256
    nj = f2 // 2 // w

    def body(g_c, v_c, g_p, v_p, wg, wv, o_ref):
        first = pl.program_id(1) == 0
        wgs = [wg[pl.ds(k, 1), :] for k in range(K_FFN)]
        wvs = [wv[pl.ds(k, 1), :] for k in range(K_FFN)]
        rows = FFN_CHUNK
        first_tap = hb - (K_FFN - 1)

        def emit(gb, vb, r0):
            ug = sum(wgs[k] * gb[first_tap + k:first_tap + k + rows] for k in range(K_FFN))
            uv = sum(wvs[k] * vb[first_tap + k:first_tap + k + rows] for k in range(K_FFN))
            o_ref[pl.ds(r0, rows), :] = (ug * _sigmoid(ug) * uv).astype(bf16)

        emit(jnp.concatenate([jnp.where(first, 0.0, g_p[...]), g_c[pl.ds(0, rows), :]], axis=0),
             jnp.concatenate([jnp.where(first, 0.0, v_p[...]), v_c[pl.ds(0, rows), :]], axis=0), 0)

        def chunk(c, carry):
            r0 = pl.multiple_of(c * rows, rows)
            back = pl.ds(pl.multiple_of(r0 - hb, hb), rows + hb)
            emit(g_c[back, :], v_c[back, :], r0)
            return carry

        lax.fori_loop(1, t // rows, chunk, 0, unroll=t // rows - 1)

    cur = lambda off: pl.BlockSpec((t, w), lambda j, i: (i, off + j))
    prv = lambda off: pl.BlockSpec((hb, w), lambda j, i: (jnp.maximum(i * (t // hb) - 1, 0), off + j))
    wsp = lambda off: pl.BlockSpec((K_FFN, w), lambda j, i: (0, off + j))
    return _pcall(body, name=name, out_shape=SDS((s, f2 // 2), bf16), grid=(nj, s // t),
                  in_specs=[cur(0), cur(nj), prv(0), prv(nj), wsp(0), wsp(nj)],
                  out_specs=pl.BlockSpec((t, w), lambda j, i: (i, j)),
                  compiler_params=_cp(("parallel", "parallel")))(upre, upre, upre, upre, conv_f, conv_f)


def _ffn_act_bwd(upre, dact, conv_f, stacks, l, name):
    s, f2 = upre.shape
    t, hb, w = FFN_BWD_ROWS, HALO_FFN, 256
    nj = f2 // 2 // w

    def body(g_c, v_c, g_p, v_p, g_n, v_n, da_c, da_n, wg, wv, sg_ref, sv_ref, dg_ref, dv_ref, dwg_ref, dwv_ref):
        i = pl.program_id(1)
        first, last = i == 0, i == pl.num_programs(1) - 1

        @pl.when(first)
        def _():
            dwg_ref[...] = jnp.zeros_like(dwg_ref)
            dwv_ref[...] = jnp.zeros_like(dwv_ref)

        wgs = [wg[pl.ds(k, 1), :] for k in range(K_FFN)]
        wvs = [wv[pl.ds(k, 1), :] for k in range(K_FFN)]
        rows = FFN_CHUNK
        n = rows + hb
        first_tap = hb - (K_FFN - 1)
        nchunks = t // rows

        def groups_of_8(x):
            return sum(x[8 * q:8 * q + 8] for q in range(rows // 8))

        def emit(gb, vb, da, r0, carry):
            g_taps = [gb[first_tap + k:first_tap + k + n] for k in range(K_FFN)]
            v_taps = [vb[first_tap + k:first_tap + k + n] for k in range(K_FFN)]
            ug = sum(wgs[k] * g_taps[k] for k in range(K_FFN))
            uv = sum(wvs[k] * v_taps[k] for k in range(K_FFN))
            sg = _sigmoid(ug)
            dv = da * (ug * sg)
            dg = da * uv * (sg * (1.0 + ug * (1.0 - sg)))
            new = []
            for taps, dval, ws, dref in ((g_taps, dg, wgs, dg_ref), (v_taps, dv, wvs, dv_ref)):
                acc = sum(ws[k] * dval[K_FFN - 1 - k:K_FFN - 1 - k + rows] for k in range(K_FFN))
                dref[pl.ds(r0, rows), :] = acc.astype(bf16)
                new += [groups_of_8(dval[:rows] * taps[k][:rows]) for k in range(K_FFN)]
            return tuple(a + b for a, b in zip(carry, new))

        def chunk(c, carry):
            r0 = pl.multiple_of(c * rows, rows)
            around = pl.ds(pl.multiple_of(r0 - hb, hb), n + hb)
            return emit(g_c[around, :], v_c[around, :], da_c[pl.ds(r0, n), :], r0, carry)

        def edge(prev_ref, cur_ref, lo):
            return jnp.concatenate([jnp.where(first, 0.0, prev_ref[...]), cur_ref[pl.ds(0, lo), :]], axis=0)

        def tail(cur_ref, next_ref, lo):
            return jnp.concatenate([cur_ref[pl.ds(t - lo, lo), :], jnp.where(last, 0.0, next_ref[...])], axis=0)

        zero = jnp.zeros((8, w), f32)
        sums = emit(edge(g_p, g_c, n), edge(v_p, v_c, n), da_c[pl.ds(0, n), :], 0, (zero,) * (2 * K_FFN))
        sums = lax.fori_loop(1, nchunks - 1, chunk, sums, unroll=nchunks - 2)
        sums = emit(tail(g_c, g_n, rows + hb), tail(v_c, v_n, rows + hb), tail(da_c, da_n, rows), t - rows, sums)
        for k in range(K_FFN):
            dwg_ref[pl.ds(k, 1), :] += _colsum(sums[k])
            dwv_ref[pl.ds(k, 1), :] += _colsum(sums[K_FFN + k])

    cur = lambda off: pl.BlockSpec((t, w), lambda j, i: (i, off + j))
    prv = lambda off: pl.BlockSpec((hb, w), lambda j, i: (jnp.maximum(i * (t // hb) - 1, 0), off + j))
    nxt = lambda off: pl.BlockSpec((hb, w), lambda j, i: (jnp.minimum((i + 1) * (t // hb), s // hb - 1), off + j))
    wsp = lambda off: pl.BlockSpec((K_FFN, w), lambda j, i: (0, off + j))
    half = SDS((s, f2 // 2), bf16)
    dws = SDS(stacks[0].shape, f32)
    o_cur = pl.BlockSpec((t, w), lambda j, i: (i, j))
    o_dw = pl.BlockSpec((None, 8, w), lambda j, i: (l, 0, j))
    anywhere = pl.BlockSpec(memory_space=pl.ANY)
    return _pcall(body, name=name, out_shape=(half, half, dws, dws), grid=(nj, s // t),
                  in_specs=[cur(0), cur(nj), prv(0), prv(nj), nxt(0), nxt(nj), cur(0), nxt(0), wsp(0), wsp(nj), anywhere, anywhere],
                  out_specs=(o_cur, o_cur, o_dw, o_dw), input_output_aliases={10: 2, 11: 3},
                  compiler_params=_cp(("parallel", "arbitrary")))(upre, upre, upre, upre, upre, upre, dact, dact, conv_f, conv_f, *stacks)


def _bias_tables(rel_bias, name):
    nb = len(DILATIONS)

    def body(rb_ref, idx_ref, o_ref):
        h = pl.program_id(1)
        iv = idx_ref[0]
        qi = lax.broadcasted_iota(jnp.int32, (BLK, 2 * BLK), 0)
        kj = lax.broadcasted_iota(jnp.int32, (BLK, 2 * BLK), 1)
        band = jnp.logical_and(kj >= qi, kj <= qi + N_KEYS)
        acc = jnp.zeros((BLK, 2 * BLK), f32)
        for b in range(N_BUCKETS):
            acc = jnp.where(iv == b, rb_ref[b, h], acc)
        o_ref[0, 0, 0] = jnp.where(band, acc, NEG)
        o_ref[0, 1, 0] = jnp.where(jnp.logical_and(band, kj >= BLK), acc, NEG)

    return _pcall(body, name=name, out_shape=SDS((nb, 2, 8, BLK, 2 * BLK), f32), grid=(nb, 8),
                  in_specs=[pl.BlockSpec(memory_space=pltpu.SMEM), pl.BlockSpec((1, BLK, 2 * BLK), lambda br, h: (br, 0, 0))],
                  out_specs=pl.BlockSpec((1, 2, 1, BLK, 2 * BLK), lambda br, h: (br, 0, h, 0, 0)),
                  compiler_params=_cp(("parallel", "parallel")))(rel_bias, jnp.asarray(_BUCKETS))


def _bias_grad(ds_tabs, name):
    nl, nb = ds_tabs.shape[0], ds_tabs.shape[1]

    def body(t_ref, idx_ref, o_ref):
        br, h = pl.program_id(0), pl.program_id(1)

        @pl.when(jnp.logical_and(br == 0, h == 0))
        def _():
            o_ref[...] = jnp.zeros_like(o_ref)

        tab = t_ref[0, 0, 0]
        for l in range(1, nl):
            tab = tab + t_ref[l, 0, 0]
        iv = idx_ref[0]
        row = lax.broadcasted_iota(jnp.int32, (N_BUCKETS, 128), 0)
        col = lax.broadcasted_iota(jnp.int32, (N_BUCKETS, 128), 1)
        acc = jnp.zeros((N_BUCKETS, 128), f32)
        for b in range(N_BUCKETS):
            sel = jnp.where(iv == b, tab, 0.0)
            tot = jnp.sum(jnp.sum(sel, axis=0, keepdims=True), axis=1, keepdims=True)
            acc = acc + jnp.where(jnp.logical_and(row == b, col == h), tot, 0.0)
        o_ref[...] += acc

    return _pcall(body, name=name, out_shape=SDS((N_BUCKETS, 128), f32), grid=(nb, 8),
                  in_specs=[pl.BlockSpec((nl, 1, 1, BLK, 2 * BLK), lambda br, h: (0, br, h, 0, 0)),
                            pl.BlockSpec((1, BLK, 2 * BLK), lambda br, h: (br, 0, 0))],
                  out_specs=pl.BlockSpec((N_BUCKETS, 128), lambda br, h: (0, 0)),
                  compiler_params=_cp(("arbitrary", "arbitrary")))(ds_tabs, jnp.asarray(_BUCKETS))


def _stack_heads(x, head0):
    zero = jnp.zeros_like(x)
    return jnp.concatenate([jnp.where(head0, x, zero), jnp.where(head0, zero, x)], axis=0)


def _block_rows(blk, d, nblk):
    per_res = nblk // d
    res, n = blk // per_res, blk % per_res
    cur = pl.ds(res + d * BLK * n, BLK, stride=d) if d > 1 else pl.ds(pl.multiple_of(BLK * n, BLK), BLK)
    before = jnp.maximum(n - 1, 0)
    prev = pl.ds(res + d * BLK * before, BLK, stride=d) if d > 1 else pl.ds(pl.multiple_of(BLK * before, BLK), BLK)
    return cur, prev, jnp.where(n == 0, 1, 0)


def _attn_fwd(z, bias, name):
    s = z.shape[0]
    nblk = s // BLK

    def body(q_ref, k_ref, v_ref, b_ref, y_ref, l_ref):
        lane = lax.broadcasted_iota(jnp.int32, (BLK, BLK), 1)
        head0 = lane < D_HEAD

        for branch, d in enumerate(DILATIONS):
            def step(blk, carry, branch=branch, d=d):
                cur, prev, variant = _block_rows(blk, d, nblk)
                q = (q_ref[cur, :] * (D_HEAD ** -0.5)).astype(bf16)
                if d == nblk:
                    kk, vv = k_ref[cur, :].astype(bf16), v_ref[cur, :].astype(bf16)
                    bias_blk = b_ref[branch, 1, 0, :, BLK:]
                else:
                    kk = jnp.concatenate([k_ref[prev, :], k_ref[cur, :]], axis=0).astype(bf16)
                    vv = jnp.concatenate([v_ref[prev, :], v_ref[cur, :]], axis=0).astype(bf16)
                    bias_blk = b_ref[branch, variant, 0]
                sc = lax.dot_general(_stack_heads(q, head0), kk, (((1,), (1,)), ((), ())), preferred_element_type=f32)
                sc = sc + bias_blk
                m = jnp.max(sc, axis=-1, keepdims=True)
                p = jnp.exp(sc - m)
                den = jnp.sum(p, axis=-1, keepdims=True)
                out = jnp.dot(p.astype(bf16), vv, preferred_element_type=f32) / den
                lse = m + jnp.log(den)
                out = jnp.where(head0, out[:BLK], out[BLK:])
                lse = jnp.where(head0, lse[:BLK], lse[BLK:]) + jnp.zeros((BLK, BLK), f32)
                if branch == 0:
                    y_ref[cur, :] = out
                    l_ref[cur, :] = lse
                else:
                    y_old, l_old = y_ref[cur, :], l_ref[cur, :]
                    top = jnp.maximum(l_old, lse)
                    e_old, e_new = jnp.exp(l_old - top), jnp.exp(lse - top)
                    tot = e_old + e_new
                    y_ref[cur, :] = (e_old * y_old + e_new * out) / tot
                    l_ref[cur, :] = top + jnp.log(tot)
                return carry

            lax.fori_loop(0, nblk, step, 0, unroll=16)

    col = lambda off: pl.BlockSpec((s, BLK), lambda hp: (0, off + hp))
    out = SDS((s, 4 * BLK), f32)
    nbr = len(DILATIONS)
    return _pcall(body, name=name, out_shape=(out, out), grid=(4,),
                  in_specs=[col(Z_Q128), col(Z_K128), col(Z_V128), pl.BlockSpec((nbr, 2, 1, 2 * BLK, 2 * BLK), lambda hp: (0, 0, hp, 0, 0))],
                  out_specs=(col(0), col(0)), compiler_params=_cp(("parallel",)))(z, z, z, bias)


def _attn_bwd(z, dyb, yb, lt, bias, dz, name):
    s = z.shape[0]
    nblk = s // BLK
    scale = D_HEAD ** -0.5
    chunk = 256

    def body(q_ref, k_ref, v_ref, dy_ref, y_ref, lt_ref, b_ref, dz_in, dz_ref, ds_ref, aq, ak, av, stage, sems):
        lane = lax.broadcasted_iota(jnp.int32, (BLK, BLK), 1)
        head0 = lane < D_HEAD
        ak[...] = jnp.zeros_like(ak)
        av[...] = jnp.zeros_like(av)
        ds_ref[...] = jnp.zeros_like(ds_ref)

        for branch, d in enumerate(DILATIONS):
            def step(blk, carry, branch=branch, d=d):
                cur, prev, variant = _block_rows(blk, d, nblk)
                own_keys = d == nblk
                q = (q_ref[cur, :] * scale).astype(bf16)
                if own_keys:
                    kk, vv = k_ref[cur, :].astype(bf16), v_ref[cur, :].astype(bf16)
                    bias_blk = b_ref[branch, 1, 0, :, BLK:]
                else:
                    kk = jnp.concatenate([k_ref[prev, :], k_ref[cur, :]], axis=0).astype(bf16)
                    vv = jnp.concatenate([v_ref[prev, :], v_ref[cur, :]], axis=0).astype(bf16)
                    bias_blk = b_ref[branch, variant, 0]
                dy = dy_ref[cur, :]
                lt_b = lt_ref[cur, :]
                qs = _stack_heads(q, head0)
                dys = _stack_heads(dy.astype(bf16), head0)
                dd = jnp.sum(_stack_heads(dy * y_ref[cur, :], head0), axis=-1, keepdims=True)
                lts = jnp.concatenate([lt_b[:, 0:1], lt_b[:, D_HEAD:D_HEAD + 1]], axis=0)
                sc = lax.dot_general(qs, kk, (((1,), (1,)), ((), ())), preferred_element_type=f32)
                p = jnp.exp(sc + bias_blk - lts)
                dp = lax.dot_general(dys, vv, (((1,), (1,)), ((), ())), preferred_element_type=f32)
                ds = p * (dp - dd)
                if own_keys:
                    ds_ref[branch, 0, :, BLK:] += ds
                else:
                    ds_ref[branch, 0] += ds
                dsb = ds.astype(bf16)
                dq = jnp.dot(dsb, kk, preferred_element_type=f32) * scale
                dkk = lax.dot_general(dsb, qs, (((0,), (0,)), ((), ())), preferred_element_type=f32)
                dvv = lax.dot_general(p.astype(bf16), dys, (((0,), (0,)), ((), ())), preferred_element_type=f32)
                dq = jnp.where(head0, dq[:BLK], dq[BLK:])
                if branch == 0:
                    aq[cur, :] = dq
                else:
                    aq[cur, :] += dq
                if own_keys:
                    ak[cur, :] += dkk
                    av[cur, :] += dvv
                else:
                    ak[prev, :] += dkk[:BLK]
                    ak[cur, :] += dkk[BLK:]
                    av[prev, :] += dvv[:BLK]
                    av[cur, :] += dvv[BLK:]
                return carry

            lax.fori_loop(0, nblk, step, 0, unroll=16)

        def store(c, carry):
            rows = pl.ds(pl.multiple_of(c * chunk, chunk), chunk)
            stage[0, rows, :] = aq[rows, :].astype(bf16)
            stage[1, rows, :] = ak[rows, :].astype(bf16)
            stage[2, rows, :] = av[rows, :].astype(bf16)
            return carry

        lax.fori_loop(0, s // chunk, store, 0)
        hp = pl.program_id(0)
        copies = []
        for t, first_col in enumerate((Z_Q128, Z_K128, Z_V128)):
            cols = pl.ds(pl.multiple_of((first_col + hp) * BLK, BLK), BLK)
            copies.append(pltpu.make_async_copy(stage.at[t], dz_ref.at[:, cols], sems.at[t]))
            copies[-1].start()
        for cp in copies:
            cp.wait()

    col = lambda off: pl.BlockSpec((s, BLK), lambda hp: (0, off + hp))
    nbr = len(DILATIONS)
    tabs = pl.BlockSpec((nbr, 1, 2 * BLK, 2 * BLK), lambda hp: (0, hp, 0, 0))
    bias_spec = pl.BlockSpec((nbr, 2, 1, 2 * BLK, 2 * BLK), lambda hp: (0, 0, hp, 0, 0))
    anywhere = pl.BlockSpec(memory_space=pl.ANY)
    scratch = [pltpu.VMEM((s, BLK), f32)] * 3 + [pltpu.VMEM((3, s, BLK), bf16), pltpu.SemaphoreType.DMA((3,))]
    return _pcall(body, name=name, out_shape=(SDS(dz.shape, dz.dtype), SDS((nbr, 4, 2 * BLK, 2 * BLK), f32)), grid=(4,),
                  in_specs=[col(Z_Q128), col(Z_K128), col(Z_V128), col(0), col(0), col(0), bias_spec, anywhere],
                  out_specs=(anywhere, tabs), input_output_aliases={7: 0}, scratch_shapes=scratch,
                  compiler_params=_cp(("arbitrary",)))(z, z, z, dyb, yb, lt, bias, dz)


def _adamw_math(w, g, m, v):
    m = ADAM_B1 * m + (1.0 - ADAM_B1) * g
    v = ADAM_B2 * v + (1.0 - ADAM_B2) * (g * g)
    m_hat = m / (1.0 - ADAM_B1 ** ADAM_STEP)
    v_hat = v / (1.0 - ADAM_B2 ** ADAM_STEP)
    delta = -ADAM_LR * (m_hat / (jnp.sqrt(v_hat) + ADAM_EPS) + ADAM_WD * w)
    return delta, m, v


def _adamw_layer(land, send, w, m, v, prev, l, tr, me, name):
    _, r, c = w.shape
    assert r % tr == 0

    def body(me_ref, land_ref, own_ref, w_ref, m_ref, v_ref, p0, p1, p2, p3, g_ref, d_ref, nm_ref, nv_ref):
        mine = jnp.zeros((tr, c), jnp.int32) + me_ref[0]
        own = own_ref[...]
        g = jnp.zeros((tr, c), f32)
        for j in range(N_DEV):
            g = g + jnp.where(mine == j, own, land_ref[j]).astype(f32)
        delta, nm, nv = _adamw_math(w_ref[...], g, m_ref[...], v_ref[...])
        g_ref[...] = g
        d_ref[...] = delta
        nm_ref[...] = nm
        nv_ref[...] = nv

    parts = pl.BlockSpec((N_DEV, tr, c), lambda i, me_ref: (0, i, 0))
    own = pl.BlockSpec((None, tr, c), lambda i, me_ref: (me_ref[0], i, 0))
    row = pl.BlockSpec((None, tr, c), lambda i, me_ref: (l, i, 0))
    anywhere = pl.BlockSpec(memory_space=pl.ANY)
    out = SDS(w.shape, f32)
    grid_spec = pltpu.PrefetchScalarGridSpec(num_scalar_prefetch=1, grid=(r // tr,), in_specs=[parts, own, row, row, row] + [anywhere] * 4,
                                             out_specs=(row, row, row, row))
    return _pcall(body, name=name, out_shape=(out, out, out, out), grid_spec=grid_spec,
                  input_output_aliases={6: 0, 7: 1, 8: 2, 9: 3}, compiler_params=_cp(("parallel",)))(me, land, send, w, m, v, *prev)


def _sum_parts(parts, after, name):
    _, r, c = parts.shape

    def body(p_ref, after_ref, o_ref):
        g = p_ref[0]
        for j in range(1, N_DEV):
            g = g + p_ref[j]
        o_ref[...] = g

    return _pcall(body, name=name, out_shape=SDS((r, c), f32),
                  in_specs=[pl.BlockSpec(memory_space=pltpu.VMEM), pl.BlockSpec(memory_space=pl.ANY)],
                  out_specs=pl.BlockSpec(memory_space=pltpu.VMEM), compiler_params=_cp())(parts, after)


def _adamw_small(gs, ws, ms, vs, name):
    n = len(ws)

    def body(*refs):
        for t in range(n):
            delta, nm, nv = _adamw_math(refs[n + t][...], refs[t][...], refs[2 * n + t][...], refs[3 * n + t][...])
            refs[4 * n + t][...] = delta
            refs[5 * n + t][...] = nm
            refs[6 * n + t][...] = nv

    out = [SDS(wt.shape, f32) for wt in ws] * 3
    res = _pcall(body, name=name, out_shape=out, compiler_params=_cp())(*gs, *ws, *ms, *vs)
    return list(res[:n]), list(res[n:2 * n]), list(res[2 * n:])


def _my_place():
    return lax.axis_index("x"), lax.axis_index("y"), lax.axis_index("c")


def _all_gather(xs, name, after=None):
    n = len(xs)

    def body(*refs):
        x_refs, out_refs = refs[:n], refs[-3 - n:-3]
        send_sems, recv_sems, local_sems = refs[-3:]
        mx, my, mc = _my_place()
        me, sibling = (mx, my, mc), (mx, my, 1 - mc)
        chips = [(1 - mx, my), (mx, 1 - my), (1 - mx, 1 - my)]

        def rows(t, px, py, pc):
            return out_refs[t].at[4 * px + 2 * py + pc]

        def copy(t, k, block, to, src=None):
            return pltpu.make_async_remote_copy(src_ref=rows(t, *block) if src is None else src, dst_ref=rows(t, *block),
                                                send_sem=send_sems.at[7 * t + k], recv_sem=recv_sems.at[7 * t + k],
                                                device_id=to, device_id_type=MESH)

        mine = [pltpu.make_async_copy(x_refs[t], rows(t, *me), local_sems.at[t]) for t in range(n)]
        first, passed = [], []
        for t in range(n):
            mine[t].start()
            first += [copy(t, 0, me, sibling, src=x_refs[t])] + [copy(t, 1 + j, me, (*chip, mc), src=x_refs[t]) for j, chip in enumerate(chips)]
        for cp in first:
            cp.start()
        for t in range(n):
            for j, chip in enumerate(chips):
                copy(t, 1 + j, (*chip, mc), me).wait_recv()
                passed.append(copy(t, 4 + j, (*chip, mc), sibling))
                passed[-1].start()
        for t in range(n):
            copy(t, 0, sibling, me).wait_recv()
            for j, chip in enumerate(chips):
                copy(t, 4 + j, (*chip, 1 - mc), me).wait_recv()
        for cp in first + passed:
            cp.wait_send()
        for t in range(n):
            mine[t].wait()

    ins = list(xs) + ([after] if after is not None else [])
    anywhere = pl.BlockSpec(memory_space=pl.ANY)
    return list(_pcall(body, name=name, out_shape=[SDS((N_DEV,) + x.shape, x.dtype) for x in xs],
                       in_specs=[anywhere] * len(ins), out_specs=[anywhere] * n,
                       scratch_shapes=[pltpu.SemaphoreType.DMA((7 * n,)), pltpu.SemaphoreType.DMA((7 * n,)),
                                       pltpu.SemaphoreType.DMA((n,))])(*ins))


HBM_SPEC = pl.BlockSpec(memory_space=pltpu.HBM)
SEM_SPEC = pl.BlockSpec(memory_space=pltpu.SEMAPHORE)
ANY_SPEC = pl.BlockSpec(memory_space=pl.ANY)
DATAFLOW = pltpu.SideEffectType.DATAFLOW_SIDE_EFFECTING


def _peers():
    mx, my, mc = _my_place()
    out = []
    for k in range(1, N_DEV):
        tx, ty, tc = mx ^ ((k >> 2) & 1), my ^ ((k >> 1) & 1), mc ^ (k & 1)
        out.append(((tx, ty, tc), 4 * tx + 2 * ty + tc))
    return out


def _in_hbm(arrays):
    return [pltpu.with_memory_space_constraint(a, pltpu.HBM) for a in arrays]


N_FIRST_LEG, N_SECOND_LEG = 4, 3


def _chip_peers():
    mx, my, mc = _my_place()
    return [((tx, ty, mc), 4 * tx + 2 * ty + mc) for tx, ty in ((1 - mx, my), (mx, 1 - my), (1 - mx, 1 - my))]


def _gather_start(bufs, after, name):
    n = len(bufs)

    def body(*refs):
        ins, s_sem, r_sem, token = refs[:n], refs[n + 1], refs[n + 2], refs[-1]
        mx, my, mc = _my_place()
        me = 4 * mx + 2 * my + mc
        for t in range(n):
            for dev in [(mx, my, 1 - mc)] + [d for d, _ in _chip_peers()]:
                pltpu.make_async_remote_copy(src_ref=ins[t].at[me], dst_ref=ins[t].at[me], send_sem=s_sem.at[t],
                                             recv_sem=r_sem.at[t], device_id=dev, device_id_type=MESH).start()
        token[...] = jnp.zeros_like(token)

    out_shape = ((pltpu.SemaphoreType.DMA((n,)), pltpu.SemaphoreType.DMA((n,))) + tuple(pltpu.HBM(b.shape, b.dtype) for b in bufs)
                 + (SDS((8, 128), f32),))
    out = _pcall(body, name=name, out_shape=out_shape, in_specs=(HBM_SPEC,) * n + (ANY_SPEC,),
                 out_specs=(SEM_SPEC, SEM_SPEC) + (HBM_SPEC,) * n + (pl.BlockSpec(memory_space=pltpu.VMEM),),
                 input_output_aliases={i: 2 + i for i in range(n)},
                 compiler_params=pltpu.CompilerParams(has_side_effects=DATAFLOW))(*_in_hbm(bufs), after)
    return (out[0], out[1]), list(out[2:2 + n]), out[-1]


def _gather_forward(bufs, sems, after, name, fresh=()):
    n, nf = len(bufs), len(fresh)
    both = list(bufs) + list(fresh)

    nsem = 4 if nf else 2

    def body(*refs):
        ins, news, s1, r1 = refs[:n], refs[n:n + nf], refs[n + nf], refs[n + nf + 1]
        s2, r2 = refs[n + nf + 3:n + nf + 5]
        sf, rf = refs[n + nf + 5:n + nf + 7] if nf else (None, None)
        token = refs[-1]
        mx, my, mc = _my_place()
        me, sibling = 4 * mx + 2 * my + mc, (mx, my, 1 - mc)
        for t in range(n):
            legs = ins[t].at[pl.ds(0, N_FIRST_LEG)]
            first = pltpu.make_async_remote_copy(src_ref=legs, dst_ref=legs, send_sem=s1.at[t], recv_sem=r1.at[t],
                                                 device_id=sibling, device_id_type=MESH)
            first.wait_send()
            first.wait_recv()
        for t in range(n):
            for _, idx in _chip_peers():
                pltpu.make_async_remote_copy(src_ref=ins[t].at[idx], dst_ref=ins[t].at[idx], send_sem=s2.at[t],
                                             recv_sem=r2.at[t], device_id=sibling, device_id_type=MESH).start()
        for t in range(nf):
            for dev in [sibling] + [d for d, _ in _chip_peers()]:
                pltpu.make_async_remote_copy(src_ref=news[t].at[me], dst_ref=news[t].at[me], send_sem=sf.at[t],
                                             recv_sem=rf.at[t], device_id=dev, device_id_type=MESH).start()
        token[...] = jnp.zeros_like(token)

    sem = lambda k: pltpu.SemaphoreType.DMA((k,))
    sem_shapes = (sem(n), sem(n)) + ((sem(nf), sem(nf)) if nf else ())
    out_shape = sem_shapes + tuple(pltpu.HBM(b.shape, b.dtype) for b in both) + (SDS((8, 128), f32),)
    out = _pcall(body, name=name, out_shape=out_shape, in_specs=(HBM_SPEC,) * (n + nf) + (SEM_SPEC, SEM_SPEC, ANY_SPEC),
                 out_specs=(SEM_SPEC,) * nsem + (HBM_SPEC,) * (n + nf) + (pl.BlockSpec(memory_space=pltpu.VMEM),),
                 input_output_aliases={i: nsem + i for i in range(n + nf)},
                 compiler_params=pltpu.CompilerParams(has_side_effects=DATAFLOW))(*bufs, *_in_hbm(fresh), sems[0], sems[1], after)
    fresh_sems = (out[2], out[3]) if nf else None
    return (out[0], out[1]), list(out[nsem:nsem + n]), fresh_sems, list(out[nsem + n:nsem + n + nf]), out[-1]


def _exchange_start(sends, lands, name, after=None):
    nt = len(sends)
    extra = [] if after is None else [after]

    def body(*refs):
        srcs, dsts, token = refs[:nt], refs[nt:2 * nt], refs[-1]
        s_sem, r_sem = refs[2 * nt + len(extra)], refs[2 * nt + len(extra) + 1]
        mx, my, mc = _my_place()
        me = 4 * mx + 2 * my + mc
        for t in range(nt):
            for dev, idx in _peers():
                pltpu.make_async_remote_copy(src_ref=srcs[t].at[idx], dst_ref=dsts[t].at[me], send_sem=s_sem.at[t],
                                             recv_sem=r_sem.at[t], device_id=dev, device_id_type=MESH).start()
        token[...] = jnp.zeros_like(token)

    both = list(sends) + list(lands)
    out_shape = (pltpu.SemaphoreType.DMA((nt,)), pltpu.SemaphoreType.DMA((nt,))) + tuple(pltpu.HBM(b.shape, b.dtype) for b in both)
    out = _pcall(body, name=name, out_shape=out_shape + (SDS((8, 128), f32),), in_specs=(HBM_SPEC,) * (2 * nt) + (ANY_SPEC,) * len(extra),
                 out_specs=(SEM_SPEC, SEM_SPEC) + (HBM_SPEC,) * (2 * nt) + (pl.BlockSpec(memory_space=pltpu.VMEM),),
                 input_output_aliases={i: 2 + i for i in range(2 * nt)},
                 compiler_params=pltpu.CompilerParams(has_side_effects=DATAFLOW))(*_in_hbm(both), *extra)
    return (out[0], out[1]), list(out[2:2 + nt]), list(out[2 + nt:2 + 2 * nt]), out[-1]


def _copies_wait(srcs, dsts, sems, after, name, ncopies=N_DEV - 1):
    nt = len(srcs)
    same = srcs is dsts
    bufs = list(srcs) if same else list(srcs) + list(dsts)
    nb = len(bufs)

    def body(*refs):
        s_sem, r_sem = refs[nb], refs[nb + 1]
        mx, my, mc = _my_place()
        for t in range(nt):
            src = refs[t].at[pl.ds(0, ncopies)]
            dst = src if same else refs[nt + t].at[pl.ds(0, ncopies)]
            cp = pltpu.make_async_remote_copy(src_ref=src, dst_ref=dst, send_sem=s_sem.at[t], recv_sem=r_sem.at[t],
                                              device_id=(mx, my, 1 - mc), device_id_type=MESH)
            cp.wait_send()
            cp.wait_recv()

    out = _pcall(body, name=name, out_shape=tuple(pltpu.HBM(b.shape, b.dtype) for b in bufs),
                 in_specs=(HBM_SPEC,) * nb + (SEM_SPEC, SEM_SPEC, ANY_SPEC), out_specs=(HBM_SPEC,) * nb,
                 input_output_aliases={i: i for i in range(nb)},
                 compiler_params=pltpu.CompilerParams(has_side_effects=DATAFLOW))(*bufs, sems[0], sems[1], after)
    out = list(out)
    return (out, out) if same else (out[:nt], out[nt:])


def _cast_blocks(weights, l, me, after, name):
    nt = len(weights)
    halves = 2
    extra = [] if after is None else [after]

    def body(me_ref, *refs):
        outs = refs[nt + len(extra):]
        for t in range(nt):
            outs[t][...] = refs[t][...].astype(bf16)

    in_specs, out_specs, out_shape = [], [], []
    for wt in weights:
        _, r, c = wt.shape
        tr = r // halves
        in_specs.append(pl.BlockSpec((None, tr, c), lambda i, me_ref: (l, i, 0)))
        out_specs.append(pl.BlockSpec((None, tr, c), lambda i, me_ref: (me_ref[0], i, 0)))
        out_shape.append(SDS((N_DEV, r, c), bf16))
    in_specs += [pl.BlockSpec(memory_space=pl.ANY)] * len(extra)
    grid_spec = pltpu.PrefetchScalarGridSpec(num_scalar_prefetch=1, grid=(halves,), in_specs=in_specs, out_specs=out_specs)
    return list(_pcall(body, name=name, out_shape=out_shape, grid_spec=grid_spec, compiler_params=_cp(("parallel",)))(me, *weights, *extra))

def _layer_fwd_mix(x, p, bias, l, after=None):
    tag = f"l{l}"
    z = _mm(x, p["w_in"], "nt", 1024, 1408, f"{tag}_mm_in", after=after, norm_a=p["norm_mix_g"])
    yb, lt = _attn_fwd(z, bias, f"{tag}_attn_fwd")
    y = _mixer_fwd(z, yb, p["conv_a_w"], p["conv_c_w"], p["conv_c_b"], p["ln_c_g"], p["ln_c_b"], p["out_norm_g"], f"{tag}_mixer_fwd")
    return y, dict(x=x, z=z, yb=yb, lt=lt, y=y)


def _layer_fwd_ffn(x, y, p, l, after=None):
    tag = f"l{l}"
    x1 = _mm(y, p["w_out"], "nn", 512, 1024, f"{tag}_mm_out", res=x, after=after)
    upre = _mm(x1, p["w_up"], "nt", 1024, 1408, f"{tag}_mm_up", norm_a=p["norm_ffn_g"])
    act = _ffn_act_fwd(upre, p["conv_f_w"], f"{tag}_ffn_act")
    x2 = _mm(act, p["w_down"], "nn", 512, 1024, f"{tag}_mm_down", res=x1)
    return x2, dict(x1=x1, upre=upre, act=act)


def _small_stacks(depth, d_model, d_ff):
    shapes = dict(norm_ffn=(depth, 1, d_model), norm_mix=(depth, 1, d_model), conv_f_gate=(depth, 8, d_ff),
                  conv_f_val=(depth, 8, d_ff), mixers=(depth, PG_ROWS, 256), og_b=(depth, 1, 512))
    return {k: lax.empty(shp, f32) for k, shp in shapes.items()}


def _small_grads(st):
    pg = st["mixers"]
    row = lambda r: pg[:, r:r + 1, :]
    return dict(norm_ffn_g=st["norm_ffn"], norm_mix_g=st["norm_mix"],
                conv_f_w=jnp.concatenate([st["conv_f_gate"][:, :K_FFN], st["conv_f_val"][:, :K_FFN]], axis=2),
                conv_a_w=pg[:, PG_CONV_A:PG_CONV_A + K_SHORT], conv_c_w=pg[:, PG_CONV_C:PG_CONV_C + K_CONF],
                conv_c_b=row(PG_CB), ln_c_g=row(PG_LNG), ln_c_b=row(PG_LNB),
                out_norm_g=jnp.concatenate([row(PG_OG_A), st["og_b"], row(PG_OG_C)], axis=2))


def _layer_bwd_ffn(dx2, p, sv, l, st, after=None):
    tag = f"l{l}"
    g, st = {}, dict(st)
    dact = _mm(dx2, p["w_down"], "nt", 512, 1408, f"{tag}_mm_dact", after=after, b_outer=True)
    dwd = _mm(sv["act"], dx2, "tn", 1408, 1024, f"{tag}_mm_dwdown", out_dtype=bf16)
    g["w_down"] = dwd.reshape(N_DEV, dwd.shape[0] // N_DEV, dwd.shape[1])
    dug, duv, st["conv_f_gate"], st["conv_f_val"] = _ffn_act_bwd(sv["upre"], dact, p["conv_f_w"], (st["conv_f_gate"], st["conv_f_val"]),
                                                                 l, f"{tag}_ffn_act_bwd")
    dwu = _mm_tn_pieces([dug, duv], sv["x1"], p["norm_ffn_g"], 1408, f"{tag}_mm_dwup")
    g["w_up"] = dwu.reshape(N_DEV, dwu.shape[0] // N_DEV, dwu.shape[1])
    dx1, st["norm_ffn"] = _mm([dug, duv], p["w_up"], "nn", 512, 1024, f"{tag}_mm_dh2",
                              rms_bwd=(sv["x1"], p["norm_ffn_g"], dx2, st["norm_ffn"], l))
    return dx1, g, st


def _layer_bwd_mix(dx1, p, sv, bias, l, st, after=None):
    tag = f"l{l}"
    g, st = {}, dict(st)
    dy = _mm(dx1, p["w_out"], "nt", 512, 1024, f"{tag}_mm_dy", after=after)
    dwo = _mm(sv["y"], dx1, "tn", 512, 1024, f"{tag}_mm_dwout", out_dtype=bf16)
    g["w_out"] = dwo.reshape(N_DEV, dwo.shape[0] // N_DEV, dwo.shape[1])
    dz, dyb, st["mixers"], st["og_b"] = _mixer_bwd(sv["z"], sv["yb"], dy, p["conv_a_w"], p["conv_c_w"], p["conv_c_b"], p["ln_c_g"],
                                                   p["ln_c_b"], p["out_norm_g"], (st["mixers"], st["og_b"]), l, f"{tag}_mixer_bwd")
    dz, tabs = _attn_bwd(sv["z"], dyb, sv["yb"], sv["lt"], bias, dz, f"{tag}_attn_bwd")
    dwi = _mm(dz, sv["x"], "tn", 1408, 1024, f"{tag}_mm_dwin", out_dtype=bf16, norm_b=p["norm_mix_g"])
    g["w_in"] = dwi.reshape(N_DEV, dwi.shape[0] // N_DEV, dwi.shape[1])
    dx, st["norm_mix"] = _mm(dz, p["w_in"], "nn", 512, 1024, f"{tag}_mm_dh", rms_bwd=(sv["x"], p["norm_mix_g"], dx1, st["norm_mix"], l))
    return dx, g, tabs, st


SMALL_LAYER_PARAMS = ("norm_mix_g", "conv_a_w", "conv_c_w", "conv_c_b", "ln_c_g", "ln_c_b", "out_norm_g", "norm_ffn_g", "conv_f_w")


BIG = ("w_in", "w_out", "w_up", "w_down")
TRANSPOSED = ("w_in", "w_up")
ADAM_ROWS = {"w_in": 176, "w_out": 128, "w_up": 176, "w_down": 176}
SMALL_SHARDED = ("conv_a_w", "conv_c_w", "conv_f_w")
SMALL_REPLICATED = ("norm_mix_g", "conv_c_b", "ln_c_g", "ln_c_b", "out_norm_g", "norm_ffn_g", "rel_bias", "final_g")
WEIGHTS = ("norm_mix_g", "w_in", "conv_a_w", "conv_c_w", "conv_c_b", "ln_c_g", "ln_c_b", "out_norm_g", "w_out",
           "norm_ffn_g", "w_up", "conv_f_w", "w_down", "rel_bias", "final_g")


def _pack(arrays):
    flat = jnp.concatenate([a.reshape(-1) for a in arrays])
    rows = -(-flat.shape[0] // 1024) * 8
    return jnp.pad(flat, (0, rows * 128 - flat.shape[0])).reshape(rows, 128)


def _unpack(packed, shapes):
    flat, out, at = packed.reshape(-1), [], 0
    for shp in shapes:
        n = int(np.prod(shp))
        out.append(flat[at:at + n].reshape(shp))
        at += n
    return out


def _gather_shards(g, local_shape):
    nd = len(local_shape)
    return jnp.moveaxis(g, 0, nd - 1).reshape(local_shape[:-1] + (N_DEV * local_shape[-1],))


def kernel(x, norm_mix_g, w_in, conv_a_w, conv_c_w, conv_c_b, ln_c_g, ln_c_b, out_norm_g, w_out, norm_ffn_g, w_up, conv_f_w, w_down, rel_bias, final_g, loss_target, m_norm_mix_g, m_w_in, m_conv_a_w, m_conv_c_w, m_conv_c_b, m_ln_c_g, m_ln_c_b, m_out_norm_g, m_w_out, m_norm_ffn_g, m_w_up, m_conv_f_w, m_w_down, m_rel_bias, m_final_g, v_norm_mix_g, v_w_in, v_conv_a_w, v_conv_c_w, v_conv_c_b, v_ln_c_g, v_ln_c_b, v_out_norm_g, v_w_out, v_norm_ffn_g, v_w_up, v_conv_f_w, v_w_down, v_rel_bias, v_final_g):
    w = dict(norm_mix_g=norm_mix_g, w_in=w_in, conv_a_w=conv_a_w, conv_c_w=conv_c_w, conv_c_b=conv_c_b, ln_c_g=ln_c_g,
             ln_c_b=ln_c_b, out_norm_g=out_norm_g, w_out=w_out, norm_ffn_g=norm_ffn_g, w_up=w_up, conv_f_w=conv_f_w,
             w_down=w_down, rel_bias=rel_bias, final_g=final_g)
    mom = dict(norm_mix_g=m_norm_mix_g, w_in=m_w_in, conv_a_w=m_conv_a_w, conv_c_w=m_conv_c_w, conv_c_b=m_conv_c_b,
               ln_c_g=m_ln_c_g, ln_c_b=m_ln_c_b, out_norm_g=m_out_norm_g, w_out=m_w_out, norm_ffn_g=m_norm_ffn_g,
               w_up=m_w_up, conv_f_w=m_conv_f_w, w_down=m_w_down, rel_bias=m_rel_bias, final_g=m_final_g)
    var = dict(norm_mix_g=v_norm_mix_g, w_in=v_w_in, conv_a_w=v_conv_a_w, conv_c_w=v_conv_c_w, conv_c_b=v_conv_c_b,
               ln_c_g=v_ln_c_g, ln_c_b=v_ln_c_b, out_norm_g=v_out_norm_g, w_out=v_w_out, norm_ffn_g=v_norm_ffn_g,
               w_up=v_w_up, conv_f_w=v_conv_f_w, w_down=v_w_down, rel_bias=v_rel_bias, final_g=v_final_g)
    depth = w_in.shape[0]
    me = 4 * lax.axis_index("x") + 2 * lax.axis_index("y") + lax.axis_index("c")
    for k in TRANSPOSED:
        w[k], mom[k], var[k] = (jnp.transpose(a, (0, 2, 1)) for a in (w[k], mom[k], var[k]))

    me_arr = me.astype(jnp.int32).reshape(1)
    bufs = [_cast_blocks([w[k] for k in BIG], 0, me_arr, None, "l0_cast_blocks")]
    in_sems, in_bufs, token = _gather_start(bufs[0][:1], me_arr, "l0_gather_start_in")
    bufs += [_cast_blocks([w[k] for k in BIG], l, me_arr, token, f"l{l}_cast_blocks") for l in range(1, depth)]
    params = {}
    gathered = _all_gather([w[k] for k in SMALL_SHARDED], "gather_small", after=token)
    gs = gathered[0]
    for k, g8 in zip(SMALL_SHARDED, gathered):
        params[k] = _gather_shards(g8, w[k].shape)
    for k in ("norm_mix_g", "conv_c_b", "ln_c_g", "ln_c_b", "out_norm_g", "norm_ffn_g"):
        params[k] = w[k][:, None, :]
    params["rel_bias"] = rel_bias
    params["final_g"] = final_g[None, :]

    def full(k, buf, l):
        return buf.reshape(N_DEV * buf.shape[1], buf.shape[2])

    bias = _bias_tables(rel_bias, "bias_tables").reshape(len(DILATIONS), 2, 4, 2 * BLK, 2 * BLK)
    in_sems, in_bufs, rest_sems, rest_bufs, token = _gather_forward(in_bufs, in_sems, gs, "l0_gather_forward_in", fresh=bufs[0][1:])
    xl, after = x[0], token
    layer_params, saved = [], []
    for l in range(depth):
        _, in_bufs = _copies_wait(in_bufs, in_bufs, in_sems, after, f"l{l}_gather_wait_in", N_SECOND_LEG)
        p = {k: params[k][l] for k in SMALL_LAYER_PARAMS}
        p["w_in"] = full("w_in", in_bufs[0], l)
        token = in_bufs[0]
        nxt_in = bufs[l + 1][:1] if l + 1 < depth else []
        if l > 0:
            rest_sems, rest_bufs, in_sems, in_bufs, token = _gather_forward(rest_bufs, rest_sems, token, f"l{l}_gather_forward_rest", fresh=nxt_in)
        else:
            in_sems, in_bufs, token = _gather_start(nxt_in, token, "l1_gather_start_in")
        y, sv = _layer_fwd_mix(xl, p, bias, l, after=token)
        if l == 0:
            rest_sems, rest_bufs, _, _, _ = _gather_forward(rest_bufs, rest_sems, y, "l0_gather_forward_rest")
        _, rest_bufs = _copies_wait(rest_bufs, rest_bufs, rest_sems, y, f"l{l}_gather_wait_rest", N_SECOND_LEG)
        for k, buf in zip(BIG[1:], rest_bufs):
            p[k] = full(k, buf, l)
        token = rest_bufs[0]
        if l + 1 < depth:
            in_sems, in_bufs, rest_sems, rest_bufs, token = _gather_forward(in_bufs, in_sems, token, f"l{l + 1}_gather_forward_in",
                                                                             fresh=bufs[l + 1][1:])
        xl, sv2 = _layer_fwd_ffn(xl, y, p, l, after=token)
        layer_params.append(p)
        saved.append({**sv, **sv2})
        after = xl
    loss_tile, dx, d_final = _final_loss(xl, params["final_g"], loss_target[0], "final_loss")

    tabs, pending = [None] * depth, []
    after = None
    stacks = _small_stacks(depth, x.shape[-1], w_down.shape[1] * N_DEV)

    def exchange(l, part, names, g, after=None):
        sends = [g[k] for k in names]
        sems, sends, lands, token = _exchange_start(sends, [lax.empty(s.shape, s.dtype) for s in sends],
                                                    f"l{l}_exchange_start_{part}", after=after)
        pending.append((l, part, names, sems, sends, lands))
        return token

    for l in reversed(range(depth)):
        dx1, g_ffn, stacks = _layer_bwd_ffn(dx, layer_params[l], saved[l], l, stacks, after=after)
        after = exchange(l, "ffn", ("w_up", "w_down"), g_ffn)
        dx, g_mix, tabs[l], stacks = _layer_bwd_mix(dx1, layer_params[l], saved[l], bias, l, stacks, after=after)
        if l > 0:
            after = exchange(l, "mix", ("w_out", "w_in"), g_mix)
    grads_small = _small_grads(stacks)
    grads_small["rel_bias"] = _bias_grad(jnp.stack(tabs).reshape(depth, len(DILATIONS), 8, BLK, 2 * BLK), "bias_grad")[:, :8]
    grads_small["final_g"] = d_final

    out_g, out_d, out_m, out_v = {}, {}, {}, {}
    grads = grads_small
    small = SMALL_SHARDED + SMALL_REPLICATED
    full_shapes = [params[k].shape if k in SMALL_SHARDED else w[k].shape for k in small]
    packed = _pack([grads[k].reshape(shp) for k, shp in zip(small, full_shapes)] + [loss_tile[0, :1]])
    gathered_small = _all_gather([packed], "gather_small_grads", after=after)[0]
    token = exchange(0, "mix", ("w_out", "w_in"), g_mix, after=gathered_small)
    gsum = _sum_parts(gathered_small, token, "sum_small_grads")
    *summed, loss = _unpack(gsum, full_shapes + [(1,)])
    loss = loss[0]
    gfull = dict(zip(small, summed))
    for k in SMALL_SHARDED:
        shp = w[k].shape
        gfull[k] = lax.dynamic_index_in_dim(gfull[k].reshape(shp[:-1] + (N_DEV, shp[-1])), me, axis=len(shp) - 1, keepdims=False)
    two_d = lambda a: a.reshape(1, -1) if a.ndim == 1 else a
    res = _adamw_small(*[[two_d(src[k]) for k in small] for src in (gfull, w, mom, var)], "adamw_small")
    for name_map, outs in zip((out_d, out_m, out_v), res):
        name_map.update({k: o.reshape(w[k].shape) for k, o in zip(small, outs)})
    out_g.update(gfull)

    results = {k: [lax.empty(w[k].shape, f32) for _ in range(4)] for k in BIG}
    after = res[0][0]
    for l, part, names, sems, sends, lands in pending:
        sends, lands = _copies_wait(sends, lands, sems, after, f"l{l}_exchange_wait_{part}")
        for k, send, land in zip(names, sends, lands):
            results[k] = _adamw_layer(land, send, w[k], mom[k], var[k], results[k], l, ADAM_ROWS[k], me_arr, f"l{l}_adamw_{k}")
            after = results[k][0]
    for k in BIG:
        big = [jnp.transpose(a, (0, 2, 1)) for a in results[k]] if k in TRANSPOSED else results[k]
        out_g[k], out_d[k], out_m[k], out_v[k] = big

    return (loss, dx[None], *[out_g[k] for k in WEIGHTS], *[out_d[k] for k in WEIGHTS], *[out_m[k] for k in WEIGHTS],
            *[out_v[k] for k in WEIGHTS])
```

```python
import math

import numpy as np
import jax
import jax.numpy as jnp
from jax import lax
from jax.experimental import pallas as pl
from jax.experimental.pallas import tpu as pltpu

f32, bf16 = jnp.float32, jnp.bfloat16
SDS = jax.ShapeDtypeStruct
MESH = pl.DeviceIdType.MESH

N_DEV = 8
EPS = 1e-6
NEG = -1e30
D_HEAD = 64
BLK = 128
N_BUCKETS = 32
DILATIONS = (1, 4, 16)
N_KEYS = 128
K_SHORT, K_CONF, K_FFN = 3, 31, 3
ROW_TILE = 512
MIXER_ROWS = 512
HALO = 32
HALO_FFN = 8
FFN_FWD_ROWS, FFN_BWD_ROWS = 2048, 2048
FFN_CHUNK = 128
VMEM_LIMIT = 56 << 20
ADAM_LR, ADAM_B1, ADAM_B2, ADAM_EPS, ADAM_WD, ADAM_STEP = 0.001, 0.9, 0.999, 1e-08, 0.01, 10


def _pcall(body, **kw):
    return pl.pallas_call(body, **kw)


def _cp(sem=None):
    kw = dict(vmem_limit_bytes=VMEM_LIMIT)
    if sem is not None:
        kw["dimension_semantics"] = sem
    return pltpu.CompilerParams(**kw)


def _t5_buckets():
    rel = np.arange(BLK)[:, None] - np.arange(2 * BLK)[None, :] + BLK
    out = []
    for d in DILATIONS:
        dist = np.maximum(rel, 0) * d
        max_exact = N_BUCKETS // 2
        d_f = np.maximum(dist, 1).astype(np.float32)
        large = max_exact + (np.log(d_f / np.float32(max_exact)) / np.float32(math.log(2048 / max_exact))
                             * np.float32(N_BUCKETS - max_exact)).astype(np.int32)
        large = np.minimum(large, N_BUCKETS - 1)
        out.append(np.where(dist < max_exact, dist, large).astype(np.int32))
    return np.stack(out)


_BUCKETS = _t5_buckets()


def _mm(a, b, mode, tr, tc, name, res=None, out_dtype=f32, after=None, b_outer=False, norm_a=None, norm_b=None, rms_bwd=None):
    if rms_bwd is not None:
        return _mm_rms_bwd(a, b, tr, name, *rms_bwd)
    ij = (lambda f: (lambda j, i: f(i, j))) if b_outer else (lambda f: f)
    if mode == "nn":
        (m, k), n = a.shape, b.shape[1]
        a_spec = pl.BlockSpec((tr, k), ij(lambda i, j: (i, 0)))
        b_spec = pl.BlockSpec((k, tc), ij(lambda i, j: (0, j)))
        dims = (((1,), (0,)), ((), ()))
    elif mode == "nt":
        (m, k), n = a.shape, b.shape[0]
        a_spec = pl.BlockSpec((tr, k), ij(lambda i, j: (i, 0)))
        b_spec = pl.BlockSpec((tc, k), ij(lambda i, j: (j, 0)))
        dims = (((1,), (1,)), ((), ()))
    else:
        (k, m), n = a.shape, b.shape[1]
        a_spec = pl.BlockSpec((k, tr), ij(lambda i, j: (0, i)))
        b_spec = pl.BlockSpec((k, tc), ij(lambda i, j: (0, j)))
        dims = (((0,), (0,)), ((), ()))
    assert m % tr == 0 and n % tc == 0, (name, m, n, tr, tc)
    o_spec = pl.BlockSpec((tr, tc), ij(lambda i, j: (i, j)))
    grid = (n // tc, m // tr) if b_outer else (m // tr, n // tc)

    gain = norm_a if norm_a is not None else norm_b
    assert norm_b is None or (mode == "tn" and tc == n and not b_outer)
    assert norm_a is None or (mode != "tn" and not b_outer)

    def body(a_ref, b_ref, *rest):
        o_ref = rest[-2] if gain is not None else rest[-1]
        if norm_a is not None:
            ha_sc = rest[-1]

            @pl.when(pl.program_id(1) == 0)
            def _():
                av = a_ref[...]
                ha_sc[...] = (av * _rstd(av) * rest[0][...]).astype(bf16)

            av = ha_sc[...]
        else:
            av = a_ref[...].astype(bf16)
        if norm_b is not None:
            hb_sc = rest[-1]

            @pl.when(pl.program_id(0) == 0)
            def _():
                bv = b_ref[...]
                hb_sc[...] = (bv * _rstd(bv) * rest[0][...]).astype(bf16)

            bv = hb_sc[...]
        else:
            bv = b_ref[...].astype(bf16)
        acc = lax.dot_general(av, bv, dims, preferred_element_type=f32)
        if res is not None:
            acc = acc + rest[1 if gain is not None else 0][...]
        o_ref[...] = acc.astype(out_dtype)

    vec = pl.BlockSpec((1, k if norm_a is not None else n), lambda i, j: (0, 0))
    ins = [a, b] + ([gain] if gain is not None else []) + ([res] if res is not None else []) + ([after] if after is not None else [])
    specs = ([a_spec, b_spec] + ([vec] if gain is not None else []) + ([o_spec] if res is not None else [])
             + ([pl.BlockSpec(memory_space=pl.ANY)] if after is not None else []))
    scratch = [pltpu.VMEM((k, n), bf16)] if norm_b is not None else [pltpu.VMEM((tr, k), bf16)] if norm_a is not None else []
    sem = ("arbitrary", "arbitrary") if gain is not None else ("parallel", "parallel")
    return _pcall(body, name=name, out_shape=SDS((m, n), out_dtype), grid=grid, in_specs=specs, out_specs=o_spec,
                  scratch_shapes=scratch, compiler_params=_cp(sem))(*ins)


def _mm_tn_pieces(pieces, b, gain, tr, name):
    k, n = b.shape
    widths = [pc.shape[1] for pc in pieces]
    assert all(wd % tr == 0 for wd in widths)
    starts = [sum(widths[:p]) // tr for p in range(len(pieces))]
    counts = [wd // tr for wd in widths]
    npc = len(pieces)

    def body(*refs):
        a_refs, b_ref, g_ref, o_ref, hb_sc = refs[:npc], refs[npc], refs[npc + 1], refs[npc + 2], refs[npc + 3]
        i = pl.program_id(0)

        @pl.when(i == 0)
        def _():
            bv = b_ref[...]
            hb_sc[...] = (bv * _rstd(bv) * g_ref[...]).astype(bf16)

        for p in range(npc):
            @pl.when(jnp.logical_and(i >= starts[p], i < starts[p] + counts[p]))
            def _(p=p):
                o_ref[...] = lax.dot_general(a_refs[p][...].astype(bf16), hb_sc[...], (((0,), (0,)), ((), ())),
                                             preferred_element_type=f32).astype(bf16)

    a_specs = [pl.BlockSpec((k, tr), lambda i, p=p: (0, jnp.clip(i - starts[p], 0, counts[p] - 1))) for p in range(npc)]
    return _pcall(body, name=name, out_shape=SDS((sum(widths), n), bf16), grid=(sum(counts),),
                  in_specs=a_specs + [pl.BlockSpec((k, n), lambda i: (0, 0)), pl.BlockSpec((1, n), lambda i: (0, 0))],
                  out_specs=pl.BlockSpec((tr, n), lambda i: (i, 0)), scratch_shapes=[pltpu.VMEM((k, n), bf16)],
                  compiler_params=_cp(("arbitrary",)))(*pieces, b, gain)


def _mm_rms_bwd(a, b, tr, name, x, g, dres, stack, l):
    pieces = list(a) if isinstance(a, (list, tuple)) else [a]
    m, n, npc = pieces[0].shape[0], b.shape[1], len(pieces)
    k = sum(pc.shape[1] for pc in pieces)

    def body(*refs):
        a_refs, (b_ref, x_ref, g_ref, dres_ref, stack_ref, dx_ref, dg_ref) = refs[:npc], refs[npc:]
        dh, at = 0.0, 0
        for a_ref in a_refs:
            kp = a_ref.shape[1]
            dh = dh + jnp.dot(a_ref[...].astype(bf16), b_ref[pl.ds(at, kp), :].astype(bf16), preferred_element_type=f32)
            at += kp
        xv = x_ref[...]
        r = _rstd(xv)
        dx_ref[...] = dres_ref[...] + _rms_dx(xv, r, g_ref[...], dh)

        @pl.when(pl.program_id(0) == 0)
        def _():
            dg_ref[...] = jnp.zeros_like(dg_ref)

        dg_ref[...] += _colsum(dh * xv * r)

    row = pl.BlockSpec((tr, n), lambda i: (i, 0))
    vec = pl.BlockSpec((1, n), lambda i: (0, 0))
    a_specs = [pl.BlockSpec((tr, pc.shape[1]), lambda i: (i, 0)) for pc in pieces]
    return _pcall(body, name=name, out_shape=(SDS((m, n), f32), SDS(stack.shape, f32)), grid=(m // tr,),
                  in_specs=a_specs + [pl.BlockSpec((k, n), lambda i: (0, 0), pipeline_mode=pl.Buffered(1)),
                                      row, vec, row, pl.BlockSpec(memory_space=pl.ANY)],
                  out_specs=(row, pl.BlockSpec((None, 1, n), lambda i: (l, 0, 0))), input_output_aliases={npc + 4: 1},
                  compiler_params=_cp(("arbitrary",)))(*pieces, b, x, g, dres, stack)


def _rstd(x):
    return lax.rsqrt(jnp.mean(x * x, axis=-1, keepdims=True) + EPS)


def _rms_dx(x, r, g, dy):
    dxh = dy * g
    return r * (dxh - x * (r * r) * jnp.mean(dxh * x, axis=-1, keepdims=True))


def _sigmoid(x):
    return 0.5 * jnp.tanh(0.5 * x) + 0.5


def _colsum(x):
    return jnp.sum(x, axis=0, keepdims=True)


def _final_loss(x, g, target, name):
    s, d = x.shape

    def body(x_ref, g_ref, t_ref, loss_ref, dx_ref, dg_ref):
        xv, gv = x_ref[...], g_ref[...]
        r = _rstd(xv)
        diff = xv * r * gv - t_ref[...]
        dy = diff * (1.0 / d)

        @pl.when(pl.program_id(0) == 0)
        def _():
            dg_ref[...] = jnp.zeros_like(dg_ref)
            loss_ref[...] = jnp.zeros_like(loss_ref)

        part = jnp.sum(jnp.sum(diff * diff, axis=1, keepdims=True), axis=0, keepdims=True) * (0.5 / d)
        loss_ref[...] += part
        dx_ref[...] = _rms_dx(xv, r, gv, dy)
        dg_ref[...] += _colsum(dy * xv * r)

    row = pl.BlockSpec((ROW_TILE, d), lambda i: (i, 0))
    vec = pl.BlockSpec((1, d), lambda i: (0, 0))
    one = pl.BlockSpec((8, 128), lambda i: (0, 0))
    return _pcall(body, name=name, out_shape=(SDS((8, 128), f32), SDS((s, d), f32), SDS((1, d), f32)),
                  grid=(s // ROW_TILE,), in_specs=[row, vec, row], out_specs=(one, row, vec),
                  compiler_params=_cp(("arbitrary",)))(x, g, target)


Z_AH, Z_AB, Z_AC, Z_CV, Z_CG = 0, 1, 2, 9, 10
Z_Q128, Z_K128, Z_V128 = 6, 10, 14


def _cur(t, w, col):
    return pl.BlockSpec((t, w), lambda i: (i, col))


def _prev(hb, t, w, col):
    return pl.BlockSpec((hb, w), lambda i: (jnp.maximum(i * (t // hb) - 1, 0), col))


def _next(hb, t, w, col, rows):
    return pl.BlockSpec((hb, w), lambda i: (jnp.minimum((i + 1) * (t // hb), rows // hb - 1), col))


def _layernorm_parts(u, g, b):
    mu = jnp.mean(u, axis=-1, keepdims=True)
    var = jnp.mean(jnp.square(u - mu), axis=-1, keepdims=True)
    rs = lax.rsqrt(var + EPS)
    xhat = (u - mu) * rs
    return xhat, rs, xhat * g + b


def _fill_shifted(rot):
    n = rot.shape[1]
    for r in range(1, 8):
        rot[r, pl.ds(0, n - 8), :] = rot[0, pl.ds(r, n - 8), :]


def _window(rot, offset, rows):
    return rot[offset % 8, pl.ds(offset - offset % 8, rows), :]


def _mixer_fwd(z, yb, conv_a, conv_c, cb, lng, lnb, og, name):
    s = z.shape[0]
    t, hb, w = MIXER_ROWS, HALO, 256

    def body(ah, ab, ac, cv, cg, ah_p, ac_p, cv_p, cg_p, yb_ref, wa, wc, cb_ref, lng_ref, lnb_ref, og_ref, y_ref, p_sc, u_sc):
        first = pl.program_id(0) == 0

        def prev(ref):
            return jnp.where(first, 0.0, ref[...])

        p_sc[pl.ds(0, hb), :] = prev(ah_p) * prev(ac_p)
        p_sc[pl.ds(hb, t), :] = ah[...] * ac[...]
        conv = jnp.zeros((t, w), f32)
        for k in range(K_SHORT):
            conv = conv + wa[pl.ds(k, 1), :] * p_sc[pl.ds(hb - (K_SHORT - 1) + k, t), :]
        y_a = ab[...] * conv
        u_sc[0, pl.ds(0, hb), :] = prev(cv_p) * _sigmoid(prev(cg_p))
        u_sc[0, pl.ds(hb, t), :] = cv[...] * _sigmoid(cg[...])
        _fill_shifted(u_sc)
        u = jnp.zeros((t, w), f32) + cb_ref[...]
        for k in range(K_CONF):
            u = u + wc[pl.ds(k, 1), :] * _window(u_sc, hb - (K_CONF - 1) + k, t)
        _, _, lnv = _layernorm_parts(u, lng_ref[...], lnb_ref[...])
        y_c = lnv * _sigmoid(lnv)
        ybv = yb_ref[...]
        y_ref[:, 0:256] = (y_a * _rstd(y_a) * og_ref[:, 0:256]).astype(bf16)
        y_ref[:, 256:768] = (ybv * _rstd(ybv) * og_ref[:, 256:768]).astype(bf16)
        y_ref[:, 768:1024] = (y_c * _rstd(y_c) * og_ref[:, 768:1024]).astype(bf16)

    full = lambda shape: pl.BlockSpec(shape, lambda i: tuple(0 for _ in shape))
    in_specs = [_cur(t, w, c) for c in (Z_AH, Z_AB, Z_AC, Z_CV, Z_CG)] + [_prev(hb, t, w, c) for c in (Z_AH, Z_AC, Z_CV, Z_CG)]
    in_specs += [_cur(t, 512, 0), full((K_SHORT, w)), full((K_CONF, w)), full((1, w)), full((1, w)), full((1, w)), full((1, 1024))]
    return _pcall(body, name=name, out_shape=SDS((s, 1024), bf16), grid=(s // t,), in_specs=in_specs,
                  out_specs=_cur(t, 1024, 0), scratch_shapes=[pltpu.VMEM((hb + t, w), f32), pltpu.VMEM((8, hb + t, w), f32)],
                  compiler_params=_cp(("parallel",)))(z, z, z, z, z, z, z, z, z, yb, conv_a, conv_c, cb, lng, lnb, og)


PG_CONV_A, PG_CONV_C, PG_CB, PG_LNG, PG_LNB, PG_OG_A, PG_OG_C, PG_ROWS = 0, 8, 40, 41, 42, 43, 44, 48


def _mixer_bwd(z, yb, dy, conv_a, conv_c, cb, lng, lnb, og, stacks, l, name):
    s = z.shape[0]
    t, hb, w = MIXER_ROWS, HALO, 256
    e = t + hb

    def body(ah, ab, ac, cv, cg, ah_p, ac_p, cv_p, cg_p, ah_n, ab_n, ac_n, cv_n, cg_n,
             dya, dyb1, dyb2, dyc, dya_n, dyc_n, yb_ref, wa, wc, cb_ref, lng_ref, lnb_ref, og_ref, pg_stack, dgb_stack,
             dz_ref, dyb_ref, pg_ref, dgb_ref, p_sc, u_sc, dc_sc, du_sc):
        i = pl.program_id(0)
        first, last = i == 0, i == pl.num_programs(0) - 1

        def prev(ref):
            return jnp.where(first, 0.0, ref[...])

        def nxt(ref):
            return jnp.where(last, 0.0, ref[...])

        def ext(cur_ref, next_ref):
            return jnp.concatenate([cur_ref[...], nxt(next_ref)], axis=0)

        @pl.when(first)
        def _():
            pg_ref[...] = jnp.zeros_like(pg_ref)
            dgb_ref[...] = jnp.zeros_like(dgb_ref)

        g_a = og_ref[:, 0:256]
        p_sc[pl.ds(0, hb), :] = prev(ah_p) * prev(ac_p)
        p_sc[pl.ds(hb, t), :] = ah[...] * ac[...]
        p_sc[pl.ds(hb + t, hb), :] = nxt(ah_n) * nxt(ac_n)
        conv = jnp.zeros((e, w), f32)
        for k in range(K_SHORT):
            conv = conv + wa[pl.ds(k, 1), :] * p_sc[pl.ds(hb - (K_SHORT - 1) + k, e), :]
        ab_e, dya_e = ext(ab, ab_n), ext(dya, dya_n)
        y_a = ab_e * conv
        r_a = _rstd(y_a)
        dy_a = _rms_dx(y_a, r_a, g_a, dya_e)
        dc_sc[...] = dy_a * ab_e
        dz_ref[:, 256:512] = (dy_a[:t] * conv[:t]).astype(bf16)
        dp = jnp.zeros((t, w), f32)
        for k in range(K_SHORT):
            dp = dp + wa[pl.ds(k, 1), :] * dc_sc[pl.ds(K_SHORT - 1 - k, t), :]
            pg_ref[pl.ds(PG_CONV_A + k, 1), :] += _colsum(dc_sc[pl.ds(0, t), :] * p_sc[pl.ds(hb - (K_SHORT - 1) + k, t), :])
        dz_ref[:, 0:256] = (dp * ac[...]).astype(bf16)
        dz_ref[:, 512:768] = (dp * ah[...]).astype(bf16)
        pg_ref[pl.ds(PG_OG_A, 1), :] += _colsum(dya[...] * y_a[:t] * r_a[:t])

        g_c = og_ref[:, 768:1024]
        u_sc[0, pl.ds(0, hb), :] = prev(cv_p) * _sigmoid(prev(cg_p))
        u_sc[0, pl.ds(hb, t), :] = cv[...] * _sigmoid(cg[...])
        u_sc[0, pl.ds(hb + t, hb), :] = nxt(cv_n) * _sigmoid(nxt(cg_n))
        _fill_shifted(u_sc)
        u = jnp.zeros((e, w), f32) + cb_ref[...]
        for k in range(K_CONF):
            u = u + wc[pl.ds(k, 1), :] * _window(u_sc, hb - (K_CONF - 1) + k, e)
        xhat, rs, lnv = _layernorm_parts(u, lng_ref[...], lnb_ref[...])
        sg = _sigmoid(lnv)
        y_c = lnv * sg
        r_c = _rstd(y_c)
        dyc_e = ext(dyc, dyc_n)
        dlnv = _rms_dx(y_c, r_c, g_c, dyc_e) * (sg * (1.0 + lnv * (1.0 - sg)))
        dxh = dlnv * lng_ref[...]
        du = rs * (dxh - jnp.mean(dxh, axis=-1, keepdims=True) - xhat * jnp.mean(dxh * xhat, axis=-1, keepdims=True))
        du_sc[0] = du
        _fill_shifted(du_sc)
        duin = jnp.zeros((t, w), f32)
        for k in range(K_CONF):
            duin = duin + wc[pl.ds(k, 1), :] * _window(du_sc, K_CONF - 1 - k, t)
            pg_ref[pl.ds(PG_CONV_C + k, 1), :] += _colsum(du[:t] * _window(u_sc, hb - (K_CONF - 1) + k, t))
        sgg = _sigmoid(cg[...])
        dz_ref[:, 2304:2560] = (duin * sgg).astype(bf16)
        dz_ref[:, 2560:2816] = (duin * cv[...] * sgg * (1.0 - sgg)).astype(bf16)
        pg_ref[pl.ds(PG_CB, 1), :] += _colsum(du[:t])
        pg_ref[pl.ds(PG_LNG, 1), :] += _colsum(dlnv[:t] * xhat[:t])
        pg_ref[pl.ds(PG_LNB, 1), :] += _colsum(dlnv[:t])
        pg_ref[pl.ds(PG_OG_C, 1), :] += _colsum(dyc[...] * y_c[:t] * r_c[:t])

        g_b = og_ref[:, 256:768]
        ybv = yb_ref[...]
        r_b = _rstd(ybv)
        dyb = jnp.concatenate([dyb1[...], dyb2[...]], axis=1)
        dyb_ref[...] = _rms_dx(ybv, r_b, g_b, dyb)
        dgb_ref[...] += _colsum(dyb * ybv * r_b)

    full = lambda shape: pl.BlockSpec(shape, lambda i: tuple(0 for _ in shape))
    zc = (Z_AH, Z_AB, Z_AC, Z_CV, Z_CG)
    in_specs = [_cur(t, w, c) for c in zc] + [_prev(hb, t, w, c) for c in (Z_AH, Z_AC, Z_CV, Z_CG)]
    in_specs += [_next(hb, t, w, c, s) for c in zc]
    in_specs += [_cur(t, w, c) for c in (0, 1, 2, 3)] + [_next(hb, t, w, 0, s), _next(hb, t, w, 3, s)]
    in_specs += [_cur(t, 512, 0), full((K_SHORT, w)), full((K_CONF, w)), full((1, w)), full((1, w)), full((1, w)), full((1, 1024))]
    in_specs += [pl.BlockSpec(memory_space=pl.ANY)] * 2
    out_shape = (SDS((s, z.shape[1]), bf16), SDS((s, 512), f32), SDS(stacks[0].shape, f32), SDS(stacks[1].shape, f32))
    out_specs = (_cur(t, z.shape[1], 0), _cur(t, 512, 0), pl.BlockSpec((None, PG_ROWS, w), lambda i: (l, 0, 0)),
                 pl.BlockSpec((None, 1, 512), lambda i: (l, 0, 0)))
    scratch = [pltpu.VMEM((hb + t + hb, w), f32), pltpu.VMEM((8, hb + t + hb, w), f32), pltpu.VMEM((e, w), f32), pltpu.VMEM((8, e, w), f32)]
    args = [z] * 14 + [dy] * 6 + [yb, conv_a, conv_c, cb, lng, lnb, og] + list(stacks)
    return _pcall(body, name=name, out_shape=out_shape, grid=(s // t,), in_specs=in_specs, out_specs=out_specs,
                  input_output_aliases={27: 2, 28: 3}, scratch_shapes=scratch, compiler_params=_cp(("arbitrary",)))(*args)


def _ffn_act_fwd(upre, conv_f, name):
    s, f2 = upre.shape
    t, hb, w = FFN_FWD_ROWS, HALO_FFN, 256
    nj = f2 // 2 // w

    def body(g_c, v_c, g_p, v_p, wg, wv, o_ref):
        first = pl.program_id(1) == 0
        wgs = [wg[pl.ds(k, 1), :] for k in range(K_FFN)]
        wvs = [wv[pl.ds(k, 1), :] for k in range(K_FFN)]
        rows = FFN_CHUNK
        first_tap = hb - (K_FFN - 1)

        def emit(gb, vb, r0):
            ug = sum(wgs[k] * gb[first_tap + k:first_tap + k + rows] for k in range(K_FFN))
            uv = sum(wvs[k] * vb[first_tap + k:first_tap + k + rows] for k in range(K_FFN))
            o_ref[pl.ds(r0, rows), :] = (ug * _sigmoid(ug) * uv).astype(bf16)

        emit(jnp.concatenate([jnp.where(first, 0.0, g_p[...]), g_c[pl.ds(0, rows), :]], axis=0),
             jnp.concatenate([jnp.where(first, 0.0, v_p[...]), v_c[pl.ds(0, rows), :]], axis=0), 0)

        def chunk(c, carry):
            r0 = pl.multiple_of(c * rows, rows)
            back = pl.ds(pl.multiple_of(r0 - hb, hb), rows + hb)
            emit(g_c[back, :], v_c[back, :], r0)
            return carry

        lax.fori_loop(1, t // rows, chunk, 0, unroll=t // rows - 1)

    cur = lambda off: pl.BlockSpec((t, w), lambda j, i: (i, off + j))
    prv = lambda off: pl.BlockSpec((hb, w), lambda j, i: (jnp.maximum(i * (t // hb) - 1, 0), off + j))
    wsp = lambda off: pl.BlockSpec((K_FFN, w), lambda j, i: (0, off + j))
    return _pcall(body, name=name, out_shape=SDS((s, f2 // 2), bf16), grid=(nj, s // t),
                  in_specs=[cur(0), cur(nj), prv(0), prv(nj), wsp(0), wsp(nj)],
                  out_specs=pl.BlockSpec((t, w), lambda j, i: (i, j)),
                  compiler_params=_cp(("parallel", "parallel")))(upre, upre, upre, upre, conv_f, conv_f)


def _ffn_act_bwd(upre, dact, conv_f, stacks, l, name):
    s, f2 = upre.shape
    t, hb, w = FFN_BWD_ROWS, HALO_FFN, 256
    nj = f2 // 2 // w

    def body(g_c, v_c, g_p, v_p, g_n, v_n, da_c, da_n, wg, wv, sg_ref, sv_ref, dg_ref, dv_ref, dwg_ref, dwv_ref):
        i = pl.program_id(1)
        first, last = i == 0, i == pl.num_programs(1) - 1

        @pl.when(first)
        def _():
            dwg_ref[...] = jnp.zeros_like(dwg_ref)
            dwv_ref[...] = jnp.zeros_like(dwv_ref)

        wgs = [wg[pl.ds(k, 1), :] for k in range(K_FFN)]
        wvs = [wv[pl.ds(k, 1), :] for k in range(K_FFN)]
        rows = FFN_CHUNK
        n = rows + hb
        first_tap = hb - (K_FFN - 1)
        nchunks = t // rows

        def groups_of_8(x):
            return sum(x[8 * q:8 * q + 8] for q in range(rows // 8))

        def emit(gb, vb, da, r0, carry):
            g_taps = [gb[first_tap + k:first_tap + k + n] for k in range(K_FFN)]
            v_taps = [vb[first_tap + k:first_tap + k + n] for k in range(K_FFN)]
            ug = sum(wgs[k] * g_taps[k] for k in range(K_FFN))
            uv = sum(wvs[k] * v_taps[k] for k in range(K_FFN))
            sg = _sigmoid(ug)
            dv = da * (ug * sg)
            dg = da * uv * (sg * (1.0 + ug * (1.0 - sg)))
            new = []
            for taps, dval, ws, dref in ((g_taps, dg, wgs, dg_ref), (v_taps, dv, wvs, dv_ref)):
                acc = sum(ws[k] * dval[K_FFN - 1 - k:K_FFN - 1 - k + rows] for k in range(K_FFN))
                dref[pl.ds(r0, rows), :] = acc.astype(bf16)
                new += [groups_of_8(dval[:rows] * taps[k][:rows]) for k in range(K_FFN)]
            return tuple(a + b for a, b in zip(carry, new))

        def chunk(c, carry):
            r0 = pl.multiple_of(c * rows, rows)
            around = pl.ds(pl.multiple_of(r0 - hb, hb), n + hb)
            return emit(g_c[around, :], v_c[around, :], da_c[pl.ds(r0, n), :], r0, carry)

        def edge(prev_ref, cur_ref, lo):
            return jnp.concatenate([jnp.where(first, 0.0, prev_ref[...]), cur_ref[pl.ds(0, lo), :]], axis=0)

        def tail(cur_ref, next_ref, lo):
            return jnp.concatenate([cur_ref[pl.ds(t - lo, lo), :], jnp.where(last, 0.0, next_ref[...])], axis=0)

        zero = jnp.zeros((8, w), f32)
        sums = emit(edge(g_p, g_c, n), edge(v_p, v_c, n), da_c[pl.ds(0, n), :], 0, (zero,) * (2 * K_FFN))
        sums = lax.fori_loop(1, nchunks - 1, chunk, sums, unroll=nchunks - 2)
        sums = emit(tail(g_c, g_n, rows + hb), tail(v_c, v_n, rows + hb), tail(da_c, da_n, rows), t - rows, sums)
        for k in range(K_FFN):
            dwg_ref[pl.ds(k, 1), :] += _colsum(sums[k])
            dwv_ref[pl.ds(k, 1), :] += _colsum(sums[K_FFN + k])

    cur = lambda off: pl.BlockSpec((t, w), lambda j, i: (i, off + j))
    prv = lambda off: pl.BlockSpec((hb, w), lambda j, i: (jnp.maximum(i * (t // hb) - 1, 0), off + j))
    nxt = lambda off: pl.BlockSpec((hb, w), lambda j, i: (jnp.minimum((i + 1) * (t // hb), s // hb - 1), off + j))
    wsp = lambda off: pl.BlockSpec((K_FFN, w), lambda j, i: (0, off + j))
    half = SDS((s, f2 // 2), bf16)
    dws = SDS(stacks[0].shape, f32)
    o_cur = pl.BlockSpec((t, w), lambda j, i: (i, j))
    o_dw = pl.BlockSpec((None, 8, w), lambda j, i: (l, 0, j))
    anywhere = pl.BlockSpec(memory_space=pl.ANY)
    return _pcall(body, name=name, out_shape=(half, half, dws, dws), grid=(nj, s // t),
                  in_specs=[cur(0), cur(nj), prv(0), prv(nj), nxt(0), nxt(nj), cur(0), nxt(0), wsp(0), wsp(nj), anywhere, anywhere],
                  out_specs=(o_cur, o_cur, o_dw, o_dw), input_output_aliases={10: 2, 11: 3},
                  compiler_params=_cp(("parallel", "arbitrary")))(upre, upre, upre, upre, upre, upre, dact, dact, conv_f, conv_f, *stacks)


def _bias_tables(rel_bias, name):
    nb = len(DILATIONS)

    def body(rb_ref, idx_ref, o_ref):
        h = pl.program_id(1)
        iv = idx_ref[0]
        qi = lax.broadcasted_iota(jnp.int32, (BLK, 2 * BLK), 0)
        kj = lax.broadcasted_iota(jnp.int32, (BLK, 2 * BLK), 1)
        band = jnp.logical_and(kj >= qi, kj <= qi + N_KEYS)
        acc = jnp.zeros((BLK, 2 * BLK), f32)
        for b in range(N_BUCKETS):
            acc = jnp.where(iv == b, rb_ref[b, h], acc)
        o_ref[0, 0, 0] = jnp.where(band, acc, NEG)
        o_ref[0, 1, 0] = jnp.where(jnp.logical_and(band, kj >= BLK), acc, NEG)

    return _pcall(body, name=name, out_shape=SDS((nb, 2, 8, BLK, 2 * BLK), f32), grid=(nb, 8),
                  in_specs=[pl.BlockSpec(memory_space=pltpu.SMEM), pl.BlockSpec((1, BLK, 2 * BLK), lambda br, h: (br, 0, 0))],
                  out_specs=pl.BlockSpec((1, 2, 1, BLK, 2 * BLK), lambda br, h: (br, 0, h, 0, 0)),
                  compiler_params=_cp(("parallel", "parallel")))(rel_bias, jnp.asarray(_BUCKETS))


def _bias_grad(ds_tabs, name):
    nl, nb = ds_tabs.shape[0], ds_tabs.shape[1]

    def body(t_ref, idx_ref, o_ref):
        br, h = pl.program_id(0), pl.program_id(1)

        @pl.when(jnp.logical_and(br == 0, h == 0))
        def _():
            o_ref[...] = jnp.zeros_like(o_ref)

        tab = t_ref[0, 0, 0]
        for l in range(1, nl):
            tab = tab + t_ref[l, 0, 0]
        iv = idx_ref[0]
        row = lax.broadcasted_iota(jnp.int32, (N_BUCKETS, 128), 0)
        col = lax.broadcasted_iota(jnp.int32, (N_BUCKETS, 128), 1)
        acc = jnp.zeros((N_BUCKETS, 128), f32)
        for b in range(N_BUCKETS):
            sel = jnp.where(iv == b, tab, 0.0)
            tot = jnp.sum(jnp.sum(sel, axis=0, keepdims=True), axis=1, keepdims=True)
            acc = acc + jnp.where(jnp.logical_and(row == b, col == h), tot, 0.0)
        o_ref[...] += acc

    return _pcall(body, name=name, out_shape=SDS((N_BUCKETS, 128), f32), grid=(nb, 8),
                  in_specs=[pl.BlockSpec((nl, 1, 1, BLK, 2 * BLK), lambda br, h: (0, br, h, 0, 0)),
                            pl.BlockSpec((1, BLK, 2 * BLK), lambda br, h: (br, 0, 0))],
                  out_specs=pl.BlockSpec((N_BUCKETS, 128), lambda br, h: (0, 0)),
                  compiler_params=_cp(("arbitrary", "arbitrary")))(ds_tabs, jnp.asarray(_BUCKETS))


def _stack_heads(x, head0):
    zero = jnp.zeros_like(x)
    return jnp.concatenate([jnp.where(head0, x, zero), jnp.where(head0, zero, x)], axis=0)


def _block_rows(blk, d, nblk):
    per_res = nblk // d
    res, n = blk // per_res, blk % per_res
    cur = pl.ds(res + d * BLK * n, BLK, stride=d) if d > 1 else pl.ds(pl.multiple_of(BLK * n, BLK), BLK)
    before = jnp.maximum(n - 1, 0)
    prev = pl.ds(res + d * BLK * before, BLK, stride=d) if d > 1 else pl.ds(pl.multiple_of(BLK * before, BLK), BLK)
    return cur, prev, jnp.where(n == 0, 1, 0)


def _attn_fwd(z, bias, name):
    s = z.shape[0]
    nblk = s // BLK

    def body(q_ref, k_ref, v_ref, b_ref, y_ref, l_ref):
        lane = lax.broadcasted_iota(jnp.int32, (BLK, BLK), 1)
        head0 = lane < D_HEAD

        for branch, d in enumerate(DILATIONS):
            def step(blk, carry, branch=branch, d=d):
                cur, prev, variant = _block_rows(blk, d, nblk)
                q = (q_ref[cur, :] * (D_HEAD ** -0.5)).astype(bf16)
                if d == nblk:
                    kk, vv = k_ref[cur, :].astype(bf16), v_ref[cur, :].astype(bf16)
                    bias_blk = b_ref[branch, 1, 0, :, BLK:]
                else:
                    kk = jnp.concatenate([k_ref[prev, :], k_ref[cur, :]], axis=0).astype(bf16)
                    vv = jnp.concatenate([v_ref[prev, :], v_ref[cur, :]], axis=0).astype(bf16)
                    bias_blk = b_ref[branch, variant, 0]
                sc = lax.dot_general(_stack_heads(q, head0), kk, (((1,), (1,)), ((), ())), preferred_element_type=f32)
                sc = sc + bias_blk
                m = jnp.max(sc, axis=-1, keepdims=True)
                p = jnp.exp(sc - m)
                den = jnp.sum(p, axis=-1, keepdims=True)
                out = jnp.dot(p.astype(bf16), vv, preferred_element_type=f32) / den
                lse = m + jnp.log(den)
                out = jnp.where(head0, out[:BLK], out[BLK:])
                lse = jnp.where(head0, lse[:BLK], lse[BLK:]) + jnp.zeros((BLK, BLK), f32)
                if branch == 0:
                    y_ref[cur, :] = out
                    l_ref[cur, :] = lse
                else:
                    y_old, l_old = y_ref[cur, :], l_ref[cur, :]
                    top = jnp.maximum(l_old, lse)
                    e_old, e_new = jnp.exp(l_old - top), jnp.exp(lse - top)
                    tot = e_old + e_new
                    y_ref[cur, :] = (e_old * y_old + e_new * out) / tot
                    l_ref[cur, :] = top + jnp.log(tot)
                return carry

            lax.fori_loop(0, nblk, step, 0, unroll=16)

    col = lambda off: pl.BlockSpec((s, BLK), lambda hp: (0, off + hp))
    out = SDS((s, 4 * BLK), f32)
    nbr = len(DILATIONS)
    return _pcall(body, name=name, out_shape=(out, out), grid=(4,),
                  in_specs=[col(Z_Q128), col(Z_K128), col(Z_V128), pl.BlockSpec((nbr, 2, 1, 2 * BLK, 2 * BLK), lambda hp: (0, 0, hp, 0, 0))],
                  out_specs=(col(0), col(0)), compiler_params=_cp(("parallel",)))(z, z, z, bias)


def _attn_bwd(z, dyb, yb, lt, bias, dz, name):
    s = z.shape[0]
    nblk = s // BLK
    scale = D_HEAD ** -0.5
    chunk = 256

    def body(q_ref, k_ref, v_ref, dy_ref, y_ref, lt_ref, b_ref, dz_in, dz_ref, ds_ref, aq, ak, av, stage, sems):
        lane = lax.broadcasted_iota(jnp.int32, (BLK, BLK), 1)
        head0 = lane < D_HEAD
        ak[...] = jnp.zeros_like(ak)
        av[...] = jnp.zeros_like(av)
        ds_ref[...] = jnp.zeros_like(ds_ref)

        for branch, d in enumerate(DILATIONS):
            def step(blk, carry, branch=branch, d=d):
                cur, prev, variant = _block_rows(blk, d, nblk)
                own_keys = d == nblk
                q = (q_ref[cur, :] * scale).astype(bf16)
                if own_keys:
                    kk, vv = k_ref[cur, :].astype(bf16), v_ref[cur, :].astype(bf16)
                    bias_blk = b_ref[branch, 1, 0, :, BLK:]
                else:
                    kk = jnp.concatenate([k_ref[prev, :], k_ref[cur, :]], axis=0).astype(bf16)
                    vv = jnp.concatenate([v_ref[prev, :], v_ref[cur, :]], axis=0).astype(bf16)
                    bias_blk = b_ref[branch, variant, 0]
                dy = dy_ref[cur, :]
                lt_b = lt_ref[cur, :]
                qs = _stack_heads(q, head0)
                dys = _stack_heads(dy.astype(bf16), head0)
                dd = jnp.sum(_stack_heads(dy * y_ref[cur, :], head0), axis=-1, keepdims=True)
                lts = jnp.concatenate([lt_b[:, 0:1], lt_b[:, D_HEAD:D_HEAD + 1]], axis=0)
                sc = lax.dot_general(qs, kk, (((1,), (1,)), ((), ())), preferred_element_type=f32)
                p = jnp.exp(sc + bias_blk - lts)
                dp = lax.dot_general(dys, vv, (((1,), (1,)), ((), ())), preferred_element_type=f32)
                ds = p * (dp - dd)
                if own_keys:
                    ds_ref[branch, 0, :, BLK:] += ds
                else:
                    ds_ref[branch, 0] += ds
                dsb = ds.astype(bf16)
                dq = jnp.dot(dsb, kk, preferred_element_type=f32) * scale
                dkk = lax.dot_general(dsb, qs, (((0,), (0,)), ((), ())), preferred_element_type=f32)
                dvv = lax.dot_general(p.astype(bf16), dys, (((0,), (0,)), ((), ())), preferred_element_type=f32)
                dq = jnp.where(head0, dq[:BLK], dq[BLK:])
                if branch == 0:
                    aq[cur, :] = dq
                else:
                    aq[cur, :] += dq
                if own_keys:
                    ak[cur, :] += dkk
                    av[cur, :] += dvv
                else:
                    ak[prev, :] += dkk[:BLK]
                    ak[cur, :] += dkk[BLK:]
                    av[prev, :] += dvv[:BLK]
                    av[cur, :] += dvv[BLK:]
                return carry

            lax.fori_loop(0, nblk, step, 0, unroll=16)

        def store(c, carry):
            rows = pl.ds(pl.multiple_of(c * chunk, chunk), chunk)
            stage[0, rows, :] = aq[rows, :].astype(bf16)
            stage[1, rows, :] = ak[rows, :].astype(bf16)
            stage[2, rows, :] = av[rows, :].astype(bf16)
            return carry

        lax.fori_loop(0, s // chunk, store, 0)
        hp = pl.program_id(0)
        copies = []
        for t, first_col in enumerate((Z_Q128, Z_K128, Z_V128)):
            cols = pl.ds(pl.multiple_of((first_col + hp) * BLK, BLK), BLK)
            copies.append(pltpu.make_async_copy(stage.at[t], dz_ref.at[:, cols], sems.at[t]))
            copies[-1].start()
        for cp in copies:
            cp.wait()

    col = lambda off: pl.BlockSpec((s, BLK), lambda hp: (0, off + hp))
    nbr = len(DILATIONS)
    tabs = pl.BlockSpec((nbr, 1, 2 * BLK, 2 * BLK), lambda hp: (0, hp, 0, 0))
    bias_spec = pl.BlockSpec((nbr, 2, 1, 2 * BLK, 2 * BLK), lambda hp: (0, 0, hp, 0, 0))
    anywhere = pl.BlockSpec(memory_space=pl.ANY)
    scratch = [pltpu.VMEM((s, BLK), f32)] * 3 + [pltpu.VMEM((3, s, BLK), bf16), pltpu.SemaphoreType.DMA((3,))]
    return _pcall(body, name=name, out_shape=(SDS(dz.shape, dz.dtype), SDS((nbr, 4, 2 * BLK, 2 * BLK), f32)), grid=(4,),
                  in_specs=[col(Z_Q128), col(Z_K128), col(Z_V128), col(0), col(0), col(0), bias_spec, anywhere],
                  out_specs=(anywhere, tabs), input_output_aliases={7: 0}, scratch_shapes=scratch,
                  compiler_params=_cp(("arbitrary",)))(z, z, z, dyb, yb, lt, bias, dz)


def _adamw_math(w, g, m, v):
    m = ADAM_B1 * m + (1.0 - ADAM_B1) * g
    v = ADAM_B2 * v + (1.0 - ADAM_B2) * (g * g)
    m_hat = m / (1.0 - ADAM_B1 ** ADAM_STEP)
    v_hat = v / (1.0 - ADAM_B2 ** ADAM_STEP)
    delta = -ADAM_LR * (m_hat / (jnp.sqrt(v_hat) + ADAM_EPS) + ADAM_WD * w)
    return delta, m, v


def _adamw_layer(land, send, w, m, v, prev, l, tr, me, name):
    _, r, c = w.shape
    assert r % tr == 0

    def body(me_ref, land_ref, own_ref, w_ref, m_ref, v_ref, p0, p1, p2, p3, g_ref, d_ref, nm_ref, nv_ref):
        mine = jnp.zeros((tr, c), jnp.int32) + me_ref[0]
        own = own_ref[...]
        g = jnp.zeros((tr, c), f32)
        for j in range(N_DEV):
            g = g + jnp.where(mine == j, own, land_ref[j]).astype(f32)
        delta, nm, nv = _adamw_math(w_ref[...], g, m_ref[...], v_ref[...])
        g_ref[...] = g
        d_ref[...] = delta
        nm_ref[...] = nm
        nv_ref[...] = nv

    parts = pl.BlockSpec((N_DEV, tr, c), lambda i, me_ref: (0, i, 0))
    own = pl.BlockSpec((None, tr, c), lambda i, me_ref: (me_ref[0], i, 0))
    row = pl.BlockSpec((None, tr, c), lambda i, me_ref: (l, i, 0))
    anywhere = pl.BlockSpec(memory_space=pl.ANY)
    out = SDS(w.shape, f32)
    grid_spec = pltpu.PrefetchScalarGridSpec(num_scalar_prefetch=1, grid=(r // tr,), in_specs=[parts, own, row, row, row] + [anywhere] * 4,
                                             out_specs=(row, row, row, row))
    return _pcall(body, name=name, out_shape=(out, out, out, out), grid_spec=grid_spec,
                  input_output_aliases={6: 0, 7: 1, 8: 2, 9: 3}, compiler_params=_cp(("parallel",)))(me, land, send, w, m, v, *prev)


def _sum_parts(parts, after, name):
    _, r, c = parts.shape

    def body(p_ref, after_ref, o_ref):
        g = p_ref[0]
        for j in range(1, N_DEV):
            g = g + p_ref[j]
        o_ref[...] = g

    return _pcall(body, name=name, out_shape=SDS((r, c), f32),
                  in_specs=[pl.BlockSpec(memory_space=pltpu.VMEM), pl.BlockSpec(memory_space=pl.ANY)],
                  out_specs=pl.BlockSpec(memory_space=pltpu.VMEM), compiler_params=_cp())(parts, after)


def _adamw_small(gs, ws, ms, vs, name):
    n = len(ws)

    def body(*refs):
        for t in range(n):
            delta, nm, nv = _adamw_math(refs[n + t][...], refs[t][...], refs[2 * n + t][...], refs[3 * n + t][...])
            refs[4 * n + t][...] = delta
            refs[5 * n + t][...] = nm
            refs[6 * n + t][...] = nv

    out = [SDS(wt.shape, f32) for wt in ws] * 3
    res = _pcall(body, name=name, out_shape=out, compiler_params=_cp())(*gs, *ws, *ms, *vs)
    return list(res[:n]), list(res[n:2 * n]), list(res[2 * n:])


def _my_place():
    return lax.axis_index("x"), lax.axis_index("y"), lax.axis_index("c")


def _all_gather(xs, name, after=None):
    n = len(xs)

    def body(*refs):
        x_refs, out_refs = refs[:n], refs[-3 - n:-3]
        send_sems, recv_sems, local_sems = refs[-3:]
        mx, my, mc = _my_place()
        me, sibling = (mx, my, mc), (mx, my, 1 - mc)
        chips = [(1 - mx, my), (mx, 1 - my), (1 - mx, 1 - my)]

        def rows(t, px, py, pc):
            return out_refs[t].at[4 * px + 2 * py + pc]

        def copy(t, k, block, to, src=None):
            return pltpu.make_async_remote_copy(src_ref=rows(t, *block) if src is None else src, dst_ref=rows(t, *block),
                                                send_sem=send_sems.at[7 * t + k], recv_sem=recv_sems.at[7 * t + k],
                                                device_id=to, device_id_type=MESH)

        mine = [pltpu.make_async_copy(x_refs[t], rows(t, *me), local_sems.at[t]) for t in range(n)]
        first, passed = [], []
        for t in range(n):
            mine[t].start()
            first += [copy(t, 0, me, sibling, src=x_refs[t])] + [copy(t, 1 + j, me, (*chip, mc), src=x_refs[t]) for j, chip in enumerate(chips)]
        for cp in first:
            cp.start()
        for t in range(n):
            for j, chip in enumerate(chips):
                copy(t, 1 + j, (*chip, mc), me).wait_recv()
                passed.append(copy(t, 4 + j, (*chip, mc), sibling))
                passed[-1].start()
        for t in range(n):
            copy(t, 0, sibling, me).wait_recv()
            for j, chip in enumerate(chips):
                copy(t, 4 + j, (*chip, 1 - mc), me).wait_recv()
        for cp in first + passed:
            cp.wait_send()
        for t in range(n):
            mine[t].wait()

    ins = list(xs) + ([after] if after is not None else [])
    anywhere = pl.BlockSpec(memory_space=pl.ANY)
    return list(_pcall(body, name=name, out_shape=[SDS((N_DEV,) + x.shape, x.dtype) for x in xs],
                       in_specs=[anywhere] * len(ins), out_specs=[anywhere] * n,
                       scratch_shapes=[pltpu.SemaphoreType.DMA((7 * n,)), pltpu.SemaphoreType.DMA((7 * n,)),
                                       pltpu.SemaphoreType.DMA((n,))])(*ins))


HBM_SPEC = pl.BlockSpec(memory_space=pltpu.HBM)
SEM_SPEC = pl.BlockSpec(memory_space=pltpu.SEMAPHORE)
ANY_SPEC = pl.BlockSpec(memory_space=pl.ANY)
DATAFLOW = pltpu.SideEffectType.DATAFLOW_SIDE_EFFECTING


def _peers():
    mx, my, mc = _my_place()
    out = []
    for k in range(1, N_DEV):
        tx, ty, tc = mx ^ ((k >> 2) & 1), my ^ ((k >> 1) & 1), mc ^ (k & 1)
        out.append(((tx, ty, tc), 4 * tx + 2 * ty + tc))
    return out


def _in_hbm(arrays):
    return [pltpu.with_memory_space_constraint(a, pltpu.HBM) for a in arrays]


N_FIRST_LEG, N_SECOND_LEG = 4, 3


def _chip_peers():
    mx, my, mc = _my_place()
    return [((tx, ty, mc), 4 * tx + 2 * ty + mc) for tx, ty in ((1 - mx, my), (mx, 1 - my), (1 - mx, 1 - my))]


def _gather_start(bufs, after, name):
    n = len(bufs)

    def body(*refs):
        ins, s_sem, r_sem, token = refs[:n], refs[n + 1], refs[n + 2], refs[-1]
        mx, my, mc = _my_place()
        me = 4 * mx + 2 * my + mc
        for t in range(n):
            for dev in [(mx, my, 1 - mc)] + [d for d, _ in _chip_peers()]:
                pltpu.make_async_remote_copy(src_ref=ins[t].at[me], dst_ref=ins[t].at[me], send_sem=s_sem.at[t],
                                             recv_sem=r_sem.at[t], device_id=dev, device_id_type=MESH).start()
        token[...] = jnp.zeros_like(token)

    out_shape = ((pltpu.SemaphoreType.DMA((n,)), pltpu.SemaphoreType.DMA((n,))) + tuple(pltpu.HBM(b.shape, b.dtype) for b in bufs)
                 + (SDS((8, 128), f32),))
    out = _pcall(body, name=name, out_shape=out_shape, in_specs=(HBM_SPEC,) * n + (ANY_SPEC,),
                 out_specs=(SEM_SPEC, SEM_SPEC) + (HBM_SPEC,) * n + (pl.BlockSpec(memory_space=pltpu.VMEM),),
                 input_output_aliases={i: 2 + i for i in range(n)},
                 compiler_params=pltpu.CompilerParams(has_side_effects=DATAFLOW))(*_in_hbm(bufs), after)
    return (out[0], out[1]), list(out[2:2 + n]), out[-1]


def _gather_forward(bufs, sems, after, name, fresh=()):
    n, nf = len(bufs), len(fresh)
    both = list(bufs) + list(fresh)

    nsem = 4 if nf else 2

    def body(*refs):
        ins, news, s1, r1 = refs[:n], refs[n:n + nf], refs[n + nf], refs[n + nf + 1]
        s2, r2 = refs[n + nf + 3:n + nf + 5]
        sf, rf = refs[n + nf + 5:n + nf + 7] if nf else (None, None)
        token = refs[-1]
        mx, my, mc = _my_place()
        me, sibling = 4 * mx + 2 * my + mc, (mx, my, 1 - mc)
        for t in range(n):
            legs = ins[t].at[pl.ds(0, N_FIRST_LEG)]
            first = pltpu.make_async_remote_copy(src_ref=legs, dst_ref=legs, send_sem=s1.at[t], recv_sem=r1.at[t],
                                                 device_id=sibling, device_id_type=MESH)
            first.wait_send()
            first.wait_recv()
        for t in range(n):
            for _, idx in _chip_peers():
                pltpu.make_async_remote_copy(src_ref=ins[t].at[idx], dst_ref=ins[t].at[idx], send_sem=s2.at[t],
                                             recv_sem=r2.at[t], device_id=sibling, device_id_type=MESH).start()
        for t in range(nf):
            for dev in [sibling] + [d for d, _ in _chip_peers()]:
                pltpu.make_async_remote_copy(src_ref=news[t].at[me], dst_ref=news[t].at[me], send_sem=sf.at[t],
                                             recv_sem=rf.at[t], device_id=dev, device_id_type=MESH).start()
        token[...] = jnp.zeros_like(token)

    sem = lambda k: pltpu.SemaphoreType.DMA((k,))
    sem_shapes = (sem(n), sem(n)) + ((sem(nf), sem(nf)) if nf else ())
    out_shape = sem_shapes + tuple(pltpu.HBM(b.shape, b.dtype) for b in both) + (SDS((8, 128), f32),)
    out = _pcall(body, name=name, out_shape=out_shape, in_specs=(HBM_SPEC,) * (n + nf) + (SEM_SPEC, SEM_SPEC, ANY_SPEC),
                 out_specs=(SEM_SPEC,) * nsem + (HBM_SPEC,) * (n + nf) + (pl.BlockSpec(memory_space=pltpu.VMEM),),
                 input_output_aliases={i: nsem + i for i in range(n + nf)},
                 compiler_params=pltpu.CompilerParams(has_side_effects=DATAFLOW))(*bufs, *_in_hbm(fresh), sems[0], sems[1], after)
    fresh_sems = (out[2], out[3]) if nf else None
    return (out[0], out[1]), list(out[nsem:nsem + n]), fresh_sems, list(out[nsem + n:nsem + n + nf]), out[-1]


def _exchange_start(sends, lands, name, after=None):
    nt = len(sends)
    extra = [] if after is None else [after]

    def body(*refs):
        srcs, dsts, token = refs[:nt], refs[nt:2 * nt], refs[-1]
        s_sem, r_sem = refs[2 * nt + len(extra)], refs[2 * nt + len(extra) + 1]
        mx, my, mc = _my_place()
        me = 4 * mx + 2 * my + mc
        for t in range(nt):
            for dev, idx in _peers():
                pltpu.make_async_remote_copy(src_ref=srcs[t].at[idx], dst_ref=dsts[t].at[me], send_sem=s_sem.at[t],
                                             recv_sem=r_sem.at[t], device_id=dev, device_id_type=MESH).start()
        token[...] = jnp.zeros_like(token)

    both = list(sends) + list(lands)
    out_shape = (pltpu.SemaphoreType.DMA((nt,)), pltpu.SemaphoreType.DMA((nt,))) + tuple(pltpu.HBM(b.shape, b.dtype) for b in both)
    out = _pcall(body, name=name, out_shape=out_shape + (SDS((8, 128), f32),), in_specs=(HBM_SPEC,) * (2 * nt) + (ANY_SPEC,) * len(extra),
                 out_specs=(SEM_SPEC, SEM_SPEC) + (HBM_SPEC,) * (2 * nt) + (pl.BlockSpec(memory_space=pltpu.VMEM),),
                 input_output_aliases={i: 2 + i for i in range(2 * nt)},
                 compiler_params=pltpu.CompilerParams(has_side_effects=DATAFLOW))(*_in_hbm(both), *extra)
    return (out[0], out[1]), list(out[2:2 + nt]), list(out[2 + nt:2 + 2 * nt]), out[-1]


def _copies_wait(srcs, dsts, sems, after, name, ncopies=N_DEV - 1):
    nt = len(srcs)
    same = srcs is dsts
    bufs = list(srcs) if same else list(srcs) + list(dsts)
    nb = len(bufs)

    def body(*refs):
        s_sem, r_sem = refs[nb], refs[nb + 1]
        mx, my, mc = _my_place()
        for t in range(nt):
            src = refs[t].at[pl.ds(0, ncopies)]
            dst = src if same else refs[nt + t].at[pl.ds(0, ncopies)]
            cp = pltpu.make_async_remote_copy(src_ref=src, dst_ref=dst, send_sem=s_sem.at[t], recv_sem=r_sem.at[t],
                                              device_id=(mx, my, 1 - mc), device_id_type=MESH)
            cp.wait_send()
            cp.wait_recv()

    out = _pcall(body, name=name, out_shape=tuple(pltpu.HBM(b.shape, b.dtype) for b in bufs),
                 in_specs=(HBM_SPEC,) * nb + (SEM_SPEC, SEM_SPEC, ANY_SPEC), out_specs=(HBM_SPEC,) * nb,
                 input_output_aliases={i: i for i in range(nb)},
                 compiler_params=pltpu.CompilerParams(has_side_effects=DATAFLOW))(*bufs, sems[0], sems[1], after)
    out = list(out)
    return (out, out) if same else (out[:nt], out[nt:])


def _cast_blocks(weights, l, me, after, name):
    nt = len(weights)
    halves = 2
    extra = [] if after is None else [after]

    def body(me_ref, *refs):
        outs = refs[nt + len(extra):]
        for t in range(nt):
            outs[t][...] = refs[t][...].astype(bf16)

    in_specs, out_specs, out_shape = [], [], []
    for wt in weights:
        _, r, c = wt.shape
        tr = r // halves
        in_specs.append(pl.BlockSpec((None, tr, c), lambda i, me_ref: (l, i, 0)))
        out_specs.append(pl.BlockSpec((None, tr, c), lambda i, me_ref: (me_ref[0], i, 0)))
        out_shape.append(SDS((N_DEV, r, c), bf16))
    in_specs += [pl.BlockSpec(memory_space=pl.ANY)] * len(extra)
    grid_spec = pltpu.PrefetchScalarGridSpec(num_scalar_prefetch=1, grid=(halves,), in_specs=in_specs, out_specs=out_specs)
    return list(_pcall(body, name=name, out_shape=out_shape, grid_spec=grid_spec, compiler_params=_cp(("parallel",)))(me, *weights, *extra))

def _layer_fwd_mix(x, p, bias, l, after=None):
    tag = f"l{l}"
    z = _mm(x, p["w_in"], "nt", 1024, 1408, f"{tag}_mm_in", after=after, norm_a=p["norm_mix_g"])
    yb, lt = _attn_fwd(z, bias, f"{tag}_attn_fwd")
    y = _mixer_fwd(z, yb, p["conv_a_w"], p["conv_c_w"], p["conv_c_b"], p["ln_c_g"], p["ln_c_b"], p["out_norm_g"], f"{tag}_mixer_fwd")
    return y, dict(x=x, z=z, yb=yb, lt=lt, y=y)


def _layer_fwd_ffn(x, y, p, l, after=None):
    tag = f"l{l}"
    x1 = _mm(y, p["w_out"], "nn", 512, 1024, f"{tag}_mm_out", res=x, after=after)
    upre = _mm(x1, p["w_up"], "nt", 1024, 1408, f"{tag}_mm_up", norm_a=p["norm_ffn_g"])
    act = _ffn_act_fwd(upre, p["conv_f_w"], f"{tag}_ffn_act")
    x2 = _mm(act, p["w_down"], "nn", 512, 1024, f"{tag}_mm_down", res=x1)
    return x2, dict(x1=x1, upre=upre, act=act)


def _small_stacks(depth, d_model, d_ff):
    shapes = dict(norm_ffn=(depth, 1, d_model), norm_mix=(depth, 1, d_model), conv_f_gate=(depth, 8, d_ff),
                  conv_f_val=(depth, 8, d_ff), mixers=(depth, PG_ROWS, 256), og_b=(depth, 1, 512))
    return {k: lax.empty(shp, f32) for k, shp in shapes.items()}


def _small_grads(st):
    pg = st["mixers"]
    row = lambda r: pg[:, r:r + 1, :]
    return dict(norm_ffn_g=st["norm_ffn"], norm_mix_g=st["norm_mix"],
                conv_f_w=jnp.concatenate([st["conv_f_gate"][:, :K_FFN], st["conv_f_val"][:, :K_FFN]], axis=2),
                conv_a_w=pg[:, PG_CONV_A:PG_CONV_A + K_SHORT], conv_c_w=pg[:, PG_CONV_C:PG_CONV_C + K_CONF],
                conv_c_b=row(PG_CB), ln_c_g=row(PG_LNG), ln_c_b=row(PG_LNB),
                out_norm_g=jnp.concatenate([row(PG_OG_A), st["og_b"], row(PG_OG_C)], axis=2))


def _layer_bwd_ffn(dx2, p, sv, l, st, after=None):
    tag = f"l{l}"
    g, st = {}, dict(st)
    dact = _mm(dx2, p["w_down"], "nt", 512, 1408, f"{tag}_mm_dact", after=after, b_outer=True)
    dwd = _mm(sv["act"], dx2, "tn", 1408, 1024, f"{tag}_mm_dwdown", out_dtype=bf16)
    g["w_down"] = dwd.reshape(N_DEV, dwd.shape[0] // N_DEV, dwd.shape[1])
    dug, duv, st["conv_f_gate"], st["conv_f_val"] = _ffn_act_bwd(sv["upre"], dact, p["conv_f_w"], (st["conv_f_gate"], st["conv_f_val"]),
                                                                 l, f"{tag}_ffn_act_bwd")
    dwu = _mm_tn_pieces([dug, duv], sv["x1"], p["norm_ffn_g"], 1408, f"{tag}_mm_dwup")
    g["w_up"] = dwu.reshape(N_DEV, dwu.shape[0] // N_DEV, dwu.shape[1])
    dx1, st["norm_ffn"] = _mm([dug, duv], p["w_up"], "nn", 512, 1024, f"{tag}_mm_dh2",
                              rms_bwd=(sv["x1"], p["norm_ffn_g"], dx2, st["norm_ffn"], l))
    return dx1, g, st


def _layer_bwd_mix(dx1, p, sv, bias, l, st, after=None):
    tag = f"l{l}"
    g, st = {}, dict(st)
    dy = _mm(dx1, p["w_out"], "nt", 512, 1024, f"{tag}_mm_dy", after=after)
    dwo = _mm(sv["y"], dx1, "tn", 512, 1024, f"{tag}_mm_dwout", out_dtype=bf16)
    g["w_out"] = dwo.reshape(N_DEV, dwo.shape[0] // N_DEV, dwo.shape[1])
    dz, dyb, st["mixers"], st["og_b"] = _mixer_bwd(sv["z"], sv["yb"], dy, p["conv_a_w"], p["conv_c_w"], p["conv_c_b"], p["ln_c_g"],
                                                   p["ln_c_b"], p["out_norm_g"], (st["mixers"], st["og_b"]), l, f"{tag}_mixer_bwd")
    dz, tabs = _attn_bwd(sv["z"], dyb, sv["yb"], sv["lt"], bias, dz, f"{tag}_attn_bwd")
    dwi = _mm(dz, sv["x"], "tn", 1408, 1024, f"{tag}_mm_dwin", out_dtype=bf16, norm_b=p["norm_mix_g"])
    g["w_in"] = dwi.reshape(N_DEV, dwi.shape[0] // N_DEV, dwi.shape[1])
    dx, st["norm_mix"] = _mm(dz, p["w_in"], "nn", 512, 1024, f"{tag}_mm_dh", rms_bwd=(sv["x"], p["norm_mix_g"], dx1, st["norm_mix"], l))
    return dx, g, tabs, st


SMALL_LAYER_PARAMS = ("norm_mix_g", "conv_a_w", "conv_c_w", "conv_c_b", "ln_c_g", "ln_c_b", "out_norm_g", "norm_ffn_g", "conv_f_w")


BIG = ("w_in", "w_out", "w_up", "w_down")
TRANSPOSED = ("w_in", "w_up")
ADAM_ROWS = {"w_in": 176, "w_out": 128, "w_up": 176, "w_down": 176}
SMALL_SHARDED = ("conv_a_w", "conv_c_w", "conv_f_w")
SMALL_REPLICATED = ("norm_mix_g", "conv_c_b", "ln_c_g", "ln_c_b", "out_norm_g", "norm_ffn_g", "rel_bias", "final_g")
WEIGHTS = ("norm_mix_g", "w_in", "conv_a_w", "conv_c_w", "conv_c_b", "ln_c_g", "ln_c_b", "out_norm_g", "w_out",
           "norm_ffn_g", "w_up", "conv_f_w", "w_down", "rel_bias", "final_g")


def _pack(arrays):
    flat = jnp.concatenate([a.reshape(-1) for a in arrays])
    rows = -(-flat.shape[0] // 1024) * 8
    return jnp.pad(flat, (0, rows * 128 - flat.shape[0])).reshape(rows, 128)


def _unpack(packed, shapes):
    flat, out, at = packed.reshape(-1), [], 0
    for shp in shapes:
        n = int(np.prod(shp))
        out.append(flat[at:at + n].reshape(shp))
        at += n
    return out


def _gather_shards(g, local_shape):
    nd = len(local_shape)
    return jnp.moveaxis(g, 0, nd - 1).reshape(local_shape[:-1] + (N_DEV * local_shape[-1],))


def kernel(x, norm_mix_g, w_in, conv_a_w, conv_c_w, conv_c_b, ln_c_g, ln_c_b, out_norm_g, w_out, norm_ffn_g, w_up, conv_f_w, w_down, rel_bias, final_g, loss_target, m_norm_mix_g, m_w_in, m_conv_a_w, m_conv_c_w, m_conv_c_b, m_ln_c_g, m_ln_c_b, m_out_norm_g, m_w_out, m_norm_ffn_g, m_w_up, m_conv_f_w, m_w_down, m_rel_bias, m_final_g, v_norm_mix_g, v_w_in, v_conv_a_w, v_conv_c_w, v_conv_c_b, v_ln_c_g, v_ln_c_b, v_out_norm_g, v_w_out, v_norm_ffn_g, v_w_up, v_conv_f_w, v_w_down, v_rel_bias, v_final_g):
    w = dict(norm_mix_g=norm_mix_g, w_in=w_in, conv_a_w=conv_a_w, conv_c_w=conv_c_w, conv_c_b=conv_c_b, ln_c_g=ln_c_g,
             ln_c_b=ln_c_b, out_norm_g=out_norm_g, w_out=w_out, norm_ffn_g=norm_ffn_g, w_up=w_up, conv_f_w=conv_f_w,
             w_down=w_down, rel_bias=rel_bias, final_g=final_g)
    mom = dict(norm_mix_g=m_norm_mix_g, w_in=m_w_in, conv_a_w=m_conv_a_w, conv_c_w=m_conv_c_w, conv_c_b=m_conv_c_b,
               ln_c_g=m_ln_c_g, ln_c_b=m_ln_c_b, out_norm_g=m_out_norm_g, w_out=m_w_out, norm_ffn_g=m_norm_ffn_g,
               w_up=m_w_up, conv_f_w=m_conv_f_w, w_down=m_w_down, rel_bias=m_rel_bias, final_g=m_final_g)
    var = dict(norm_mix_g=v_norm_mix_g, w_in=v_w_in, conv_a_w=v_conv_a_w, conv_c_w=v_conv_c_w, conv_c_b=v_conv_c_b,
               ln_c_g=v_ln_c_g, ln_c_b=v_ln_c_b, out_norm_g=v_out_norm_g, w_out=v_w_out, norm_ffn_g=v_norm_ffn_g,
               w_up=v_w_up, conv_f_w=v_conv_f_w, w_down=v_w_down, rel_bias=v_rel_bias, final_g=v_final_g)
    depth = w_in.shape[0]
    me = 4 * lax.axis_index("x") + 2 * lax.axis_index("y") + lax.axis_index("c")
    for k in TRANSPOSED:
        w[k], mom[k], var[k] = (jnp.transpose(a, (0, 2, 1)) for a in (w[k], mom[k], var[k]))

    me_arr = me.astype(jnp.int32).reshape(1)
    bufs = [_cast_blocks([w[k] for k in BIG], 0, me_arr, None, "l0_cast_blocks")]
    in_sems, in_bufs, token = _gather_start(bufs[0][:1], me_arr, "l0_gather_start_in")
    bufs += [_cast_blocks([w[k] for k in BIG], l, me_arr, token, f"l{l}_cast_blocks") for l in range(1, depth)]
    params = {}
    gathered = _all_gather([w[k] for k in SMALL_SHARDED], "gather_small", after=token)
    gs = gathered[0]
    for k, g8 in zip(SMALL_SHARDED, gathered):
        params[k] = _gather_shards(g8, w[k].shape)
    for k in ("norm_mix_g", "conv_c_b", "ln_c_g", "ln_c_b", "out_norm_g", "norm_ffn_g"):
        params[k] = w[k][:, None, :]
    params["rel_bias"] = rel_bias
    params["final_g"] = final_g[None, :]

    def full(k, buf, l):
        return buf.reshape(N_DEV * buf.shape[1], buf.shape[2])

    bias = _bias_tables(rel_bias, "bias_tables").reshape(len(DILATIONS), 2, 4, 2 * BLK, 2 * BLK)
    in_sems, in_bufs, rest_sems, rest_bufs, token = _gather_forward(in_bufs, in_sems, gs, "l0_gather_forward_in", fresh=bufs[0][1:])
    xl, after = x[0], token
    layer_params, saved = [], []
    for l in range(depth):
        _, in_bufs = _copies_wait(in_bufs, in_bufs, in_sems, after, f"l{l}_gather_wait_in", N_SECOND_LEG)
        p = {k: params[k][l] for k in SMALL_LAYER_PARAMS}
        p["w_in"] = full("w_in", in_bufs[0], l)
        token = in_bufs[0]
        nxt_in = bufs[l + 1][:1] if l + 1 < depth else []
        if l > 0:
            rest_sems, rest_bufs, in_sems, in_bufs, token = _gather_forward(rest_bufs, rest_sems, token, f"l{l}_gather_forward_rest", fresh=nxt_in)
        else:
            in_sems, in_bufs, token = _gather_start(nxt_in, token, "l1_gather_start_in")
        y, sv = _layer_fwd_mix(xl, p, bias, l, after=token)
        if l == 0:
            rest_sems, rest_bufs, _, _, _ = _gather_forward(rest_bufs, rest_sems, y, "l0_gather_forward_rest")
        _, rest_bufs = _copies_wait(rest_bufs, rest_bufs, rest_sems, y, f"l{l}_gather_wait_rest", N_SECOND_LEG)
        for k, buf in zip(BIG[1:], rest_bufs):
            p[k] = full(k, buf, l)
        token = rest_bufs[0]
        if l + 1 < depth:
            in_sems, in_bufs, rest_sems, rest_bufs, token = _gather_forward(in_bufs, in_sems, token, f"l{l + 1}_gather_forward_in",
                                                                             fresh=bufs[l + 1][1:])
        xl, sv2 = _layer_fwd_ffn(xl, y, p, l, after=token)
        layer_params.append(p)
        saved.append({**sv, **sv2})
        after = xl
    loss_tile, dx, d_final = _final_loss(xl, params["final_g"], loss_target[0], "final_loss")

    tabs, pending = [None] * depth, []
    after = None
    stacks = _small_stacks(depth, x.shape[-1], w_down.shape[1] * N_DEV)

    def exchange(l, part, names, g, after=None):
        sends = [g[k] for k in names]
        sems, sends, lands, token = _exchange_start(sends, [lax.empty(s.shape, s.dtype) for s in sends],
                                                    f"l{l}_exchange_start_{part}", after=after)
        pending.append((l, part, names, sems, sends, lands))
        return token

    for l in reversed(range(depth)):
        dx1, g_ffn, stacks = _layer_bwd_ffn(dx, layer_params[l], saved[l], l, stacks, after=after)
        after = exchange(l, "ffn", ("w_up", "w_down"), g_ffn)
        dx, g_mix, tabs[l], stacks = _layer_bwd_mix(dx1, layer_params[l], saved[l], bias, l, stacks, after=after)
        if l > 0:
            after = exchange(l, "mix", ("w_out", "w_in"), g_mix)
    grads_small = _small_grads(stacks)
    grads_small["rel_bias"] = _bias_grad(jnp.stack(tabs).reshape(depth, len(DILATIONS), 8, BLK, 2 * BLK), "bias_grad")[:, :8]
    grads_small["final_g"] = d_final

    out_g, out_d, out_m, out_v = {}, {}, {}, {}
    grads = grads_small
    small = SMALL_SHARDED + SMALL_REPLICATED
    full_shapes = [params[k].shape if k in SMALL_SHARDED else w[k].shape for k in small]
    packed = _pack([grads[k].reshape(shp) for k, shp in zip(small, full_shapes)] + [loss_tile[0, :1]])
    gathered_small = _all_gather([packed], "gather_small_grads", after=after)[0]
    token = exchange(0, "mix", ("w_out", "w_in"), g_mix, after=gathered_small)
    gsum = _sum_parts(gathered_small, token, "sum_small_grads")
    *summed, loss = _unpack(gsum, full_shapes + [(1,)])
    loss = loss[0]
    gfull = dict(zip(small, summed))
    for k in SMALL_SHARDED:
        shp = w[k].shape
        gfull[k] = lax.dynamic_index_in_dim(gfull[k].reshape(shp[:-1] + (N_DEV, shp[-1])), me, axis=len(shp) - 1, keepdims=False)
    two_d = lambda a: a.reshape(1, -1) if a.ndim == 1 else a
    res = _adamw_small(*[[two_d(src[k]) for k in small] for src in (gfull, w, mom, var)], "adamw_small")
    for name_map, outs in zip((out_d, out_m, out_v), res):
        name_map.update({k: o.reshape(w[k].shape) for k, o in zip(small, outs)})
    out_g.update(gfull)

    results = {k: [lax.empty(w[k].shape, f32) for _ in range(4)] for k in BIG}
    after = res[0][0]
    for l, part, names, sems, sends, lands in pending:
        sends, lands = _copies_wait(sends, lands, sems, after, f"l{l}_exchange_wait_{part}")
        for k, send, land in zip(names, sends, lands):
            results[k] = _adamw_layer(land, send, w[k], mom[k], var[k], results[k], l, ADAM_ROWS[k], me_arr, f"l{l}_adamw_{k}")
            after = results[k][0]
    for k in BIG:
        big = [jnp.transpose(a, (0, 2, 1)) for a in results[k]] if k in TRANSPOSED else results[k]
        out_g[k], out_d[k], out_m[k], out_v[k] = big

    return (loss, dx[None], *[out_g[k] for k in WEIGHTS], *[out_d[k] for k in WEIGHTS], *[out_m[k] for k in WEIGHTS],
            *[out_v[k] for k in WEIGHTS])
```

```python
import math

import numpy as np
import jax
import jax.numpy as jnp
from jax import lax
from jax.experimental import pallas as pl
from jax.experimental.pallas import tpu as pltpu

f32, bf16 = jnp.float32, jnp.bfloat16
SDS = jax.ShapeDtypeStruct
MESH = pl.DeviceIdType.MESH

N_DEV = 8
EPS = 1e-6
NEG = -1e30
D_HEAD = 64
BLK = 128
N_BUCKETS = 32
DILATIONS = (1, 4, 16)
N_KEYS = 128
K_SHORT, K_CONF, K_FFN = 3, 31, 3
ROW_TILE = 512
MIXER_ROWS = 512
HALO = 32
HALO_FFN = 8
FFN_FWD_ROWS, FFN_BWD_ROWS = 2048, 2048
FFN_CHUNK = 128
VMEM_LIMIT = 56 << 20
ADAM_LR, ADAM_B1, ADAM_B2, ADAM_EPS, ADAM_WD, ADAM_STEP = 0.001, 0.9, 0.999, 1e-08, 0.01, 10


def _pcall(body, **kw):
    return pl.pallas_call(body, **kw)


def _cp(sem=None):
    kw = dict(vmem_limit_bytes=VMEM_LIMIT)
    if sem is not None:
        kw["dimension_semantics"] = sem
    return pltpu.CompilerParams(**kw)


def _t5_buckets():
    rel = np.arange(BLK)[:, None] - np.arange(2 * BLK)[None, :] + BLK
    out = []
    for d in DILATIONS:
        dist = np.maximum(rel, 0) * d
        max_exact = N_BUCKETS // 2
        d_f = np.maximum(dist, 1).astype(np.float32)
        large = max_exact + (np.log(d_f / np.float32(max_exact)) / np.float32(math.log(2048 / max_exact))
                             * np.float32(N_BUCKETS - max_exact)).astype(np.int32)
        large = np.minimum(large, N_BUCKETS - 1)
        out.append(np.where(dist < max_exact, dist, large).astype(np.int32))
    return np.stack(out)


_BUCKETS = _t5_buckets()


def _mm(a, b, mode, tr, tc, name, res=None, out_dtype=f32, after=None, b_outer=False, norm_a=None, norm_b=None, rms_bwd=None):
    if rms_bwd is not None:
        return _mm_rms_bwd(a, b, tr, name, *rms_bwd)
    ij = (lambda f: (lambda j, i: f(i, j))) if b_outer else (lambda f: f)
    if mode == "nn":
        (m, k), n = a.shape, b.shape[1]
        a_spec = pl.BlockSpec((tr, k), ij(lambda i, j: (i, 0)))
        b_spec = pl.BlockSpec((k, tc), ij(lambda i, j: (0, j)))
        dims = (((1,), (0,)), ((), ()))
    elif mode == "nt":
        (m, k), n = a.shape, b.shape[0]
        a_spec = pl.BlockSpec((tr, k), ij(lambda i, j: (i, 0)))
        b_spec = pl.BlockSpec((tc, k), ij(lambda i, j: (j, 0)))
        dims = (((1,), (1,)), ((), ()))
    else:
        (k, m), n = a.shape, b.shape[1]
        a_spec = pl.BlockSpec((k, tr), ij(lambda i, j: (0, i)))
        b_spec = pl.BlockSpec((k, tc), ij(lambda i, j: (0, j)))
        dims = (((0,), (0,)), ((), ()))
    assert m % tr == 0 and n % tc == 0, (name, m, n, tr, tc)
    o_spec = pl.BlockSpec((tr, tc), ij(lambda i, j: (i, j)))
    grid = (n // tc, m // tr) if b_outer else (m // tr, n // tc)

    gain = norm_a if norm_a is not None else norm_b
    assert norm_b is None or (mode == "tn" and tc == n and not b_outer)
    assert norm_a is None or (mode != "tn" and not b_outer)

    def body(a_ref, b_ref, *rest):
        o_ref = rest[-2] if gain is not None else rest[-1]
        if norm_a is not None:
            ha_sc = rest[-1]

            @pl.when(pl.program_id(1) == 0)
            def _():
                av = a_ref[...]
                ha_sc[...] = (av * _rstd(av) * rest[0][...]).astype(bf16)

            av = ha_sc[...]
        else:
            av = a_ref[...].astype(bf16)
        if norm_b is not None:
            hb_sc = rest[-1]

            @pl.when(pl.program_id(0) == 0)
            def _():
                bv = b_ref[...]
                hb_sc[...] = (bv * _rstd(bv) * rest[0][...]).astype(bf16)

            bv = hb_sc[...]
        else:
            bv = b_ref[...].astype(bf16)
        acc = lax.dot_general(av, bv, dims, preferred_element_type=f32)
        if res is not None:
            acc = acc + rest[1 if gain is not None else 0][...]
        o_ref[...] = acc.astype(out_dtype)

    vec = pl.BlockSpec((1, k if norm_a is not None else n), lambda i, j: (0, 0))
    ins = [a, b] + ([gain] if gain is not None else []) + ([res] if res is not None else []) + ([after] if after is not None else [])
    specs = ([a_spec, b_spec] + ([vec] if gain is not None else []) + ([o_spec] if res is not None else [])
             + ([pl.BlockSpec(memory_space=pl.ANY)] if after is not None else []))
    scratch = [pltpu.VMEM((k, n), bf16)] if norm_b is not None else [pltpu.VMEM((tr, k), bf16)] if norm_a is not None else []
    sem = ("arbitrary", "arbitrary") if gain is not None else ("parallel", "parallel")
    return _pcall(body, name=name, out_shape=SDS((m, n), out_dtype), grid=grid, in_specs=specs, out_specs=o_spec,
                  scratch_shapes=scratch, compiler_params=_cp(sem))(*ins)


def _mm_tn_pieces(pieces, b, gain, tr, name):
    k, n = b.shape
    widths = [pc.shape[1] for pc in pieces]
    assert all(wd % tr == 0 for wd in widths)
    starts = [sum(widths[:p]) // tr for p in range(len(pieces))]
    counts = [wd // tr for wd in widths]
    npc = len(pieces)

    def body(*refs):
        a_refs, b_ref, g_ref, o_ref, hb_sc = refs[:npc], refs[npc], refs[npc + 1], refs[npc + 2], refs[npc + 3]
        i = pl.program_id(0)

        @pl.when(i == 0)
        def _():
            bv = b_ref[...]
            hb_sc[...] = (bv * _rstd(bv) * g_ref[...]).astype(bf16)

        for p in range(npc):
            @pl.when(jnp.logical_and(i >= starts[p], i < starts[p] + counts[p]))
            def _(p=p):
                o_ref[...] = lax.dot_general(a_refs[p][...].astype(bf16), hb_sc[...], (((0,), (0,)), ((), ())),
                                             preferred_element_type=f32).astype(bf16)

    a_specs = [pl.BlockSpec((k, tr), lambda i, p=p: (0, jnp.clip(i - starts[p], 0, counts[p] - 1))) for p in range(npc)]
    return _pcall(body, name=name, out_shape=SDS((sum(widths), n), bf16), grid=(sum(counts),),
                  in_specs=a_specs + [pl.BlockSpec((k, n), lambda i: (0, 0)), pl.BlockSpec((1, n), lambda i: (0, 0))],
                  out_specs=pl.BlockSpec((tr, n), lambda i: (i, 0)), scratch_shapes=[pltpu.VMEM((k, n), bf16)],
                  compiler_params=_cp(("arbitrary",)))(*pieces, b, gain)


def _mm_rms_bwd(a, b, tr, name, x, g, dres, stack, l):
    pieces = list(a) if isinstance(a, (list, tuple)) else [a]
    m, n, npc = pieces[0].shape[0], b.shape[1], len(pieces)
    k = sum(pc.shape[1] for pc in pieces)

    def body(*refs):
        a_refs, (b_ref, x_ref, g_ref, dres_ref, stack_ref, dx_ref, dg_ref) = refs[:npc], refs[npc:]
        dh, at = 0.0, 0
        for a_ref in a_refs:
            kp = a_ref.shape[1]
            dh = dh + jnp.dot(a_ref[...].astype(bf16), b_ref[pl.ds(at, kp), :].astype(bf16), preferred_element_type=f32)
            at += kp
        xv = x_ref[...]
        r = _rstd(xv)
        dx_ref[...] = dres_ref[...] + _rms_dx(xv, r, g_ref[...], dh)

        @pl.when(pl.program_id(0) == 0)
        def _():
            dg_ref[...] = jnp.zeros_like(dg_ref)

        dg_ref[...] += _colsum(dh * xv * r)

    row = pl.BlockSpec((tr, n), lambda i: (i, 0))
    vec = pl.BlockSpec((1, n), lambda i: (0, 0))
    a_specs = [pl.BlockSpec((tr, pc.shape[1]), lambda i: (i, 0)) for pc in pieces]
    return _pcall(body, name=name, out_shape=(SDS((m, n), f32), SDS(stack.shape, f32)), grid=(m // tr,),
                  in_specs=a_specs + [pl.BlockSpec((k, n), lambda i: (0, 0), pipeline_mode=pl.Buffered(1)),
                                      row, vec, row, pl.BlockSpec(memory_space=pl.ANY)],
                  out_specs=(row, pl.BlockSpec((None, 1, n), lambda i: (l, 0, 0))), input_output_aliases={npc + 4: 1},
                  compiler_params=_cp(("arbitrary",)))(*pieces, b, x, g, dres, stack)


def _rstd(x):
    return lax.rsqrt(jnp.mean(x * x, axis=-1, keepdims=True) + EPS)


def _rms_dx(x, r, g, dy):
    dxh = dy * g
    return r * (dxh - x * (r * r) * jnp.mean(dxh * x, axis=-1, keepdims=True))


def _sigmoid(x):
    return 0.5 * jnp.tanh(0.5 * x) + 0.5


def _colsum(x):
    return jnp.sum(x, axis=0, keepdims=True)


def _final_loss(x, g, target, name):
    s, d = x.shape

    def body(x_ref, g_ref, t_ref, loss_ref, dx_ref, dg_ref):
        xv, gv = x_ref[...], g_ref[...]
        r = _rstd(xv)
        diff = xv * r * gv - t_ref[...]
        dy = diff * (1.0 / d)

        @pl.when(pl.program_id(0) == 0)
        def _():
            dg_ref[...] = jnp.zeros_like(dg_ref)
            loss_ref[...] = jnp.zeros_like(loss_ref)

        part = jnp.sum(jnp.sum(diff * diff, axis=1, keepdims=True), axis=0, keepdims=True) * (0.5 / d)
        loss_ref[...] += part
        dx_ref[...] = _rms_dx(xv, r, gv, dy)
        dg_ref[...] += _colsum(dy * xv * r)

    row = pl.BlockSpec((ROW_TILE, d), lambda i: (i, 0))
    vec = pl.BlockSpec((1, d), lambda i: (0, 0))
    one = pl.BlockSpec((8, 128), lambda i: (0, 0))
    return _pcall(body, name=name, out_shape=(SDS((8, 128), f32), SDS((s, d), f32), SDS((1, d), f32)),
                  grid=(s // ROW_TILE,), in_specs=[row, vec, row], out_specs=(one, row, vec),
                  compiler_params=_cp(("arbitrary",)))(x, g, target)


Z_AH, Z_AB, Z_AC, Z_CV, Z_CG = 0, 1, 2, 9, 10
Z_Q128, Z_K128, Z_V128 = 6, 10, 14


def _cur(t, w, col):
    return pl.BlockSpec((t, w), lambda i: (i, col))


def _prev(hb, t, w, col):
    return pl.BlockSpec((hb, w), lambda i: (jnp.maximum(i * (t // hb) - 1, 0), col))


def _next(hb, t, w, col, rows):
    return pl.BlockSpec((hb, w), lambda i: (jnp.minimum((i + 1) * (t // hb), rows // hb - 1), col))


def _layernorm_parts(u, g, b):
    mu = jnp.mean(u, axis=-1, keepdims=True)
    var = jnp.mean(jnp.square(u - mu), axis=-1, keepdims=True)
    rs = lax.rsqrt(var + EPS)
    xhat = (u - mu) * rs
    return xhat, rs, xhat * g + b


def _fill_shifted(rot):
    n = rot.shape[1]
    for r in range(1, 8):
        rot[r, pl.ds(0, n - 8), :] = rot[0, pl.ds(r, n - 8), :]


def _window(rot, offset, rows):
    return rot[offset % 8, pl.ds(offset - offset % 8, rows), :]


def _mixer_fwd(z, yb, conv_a, conv_c, cb, lng, lnb, og, name):
    s = z.shape[0]
    t, hb, w = MIXER_ROWS, HALO, 256

    def body(ah, ab, ac, cv, cg, ah_p, ac_p, cv_p, cg_p, yb_ref, wa, wc, cb_ref, lng_ref, lnb_ref, og_ref, y_ref, u_ref, p_sc, u_sc):
        first = pl.program_id(0) == 0

        def prev(ref):
            return jnp.where(first, 0.0, ref[...])

        p_sc[pl.ds(0, hb), :] = prev(ah_p) * prev(ac_p)
        p_sc[pl.ds(hb, t), :] = ah[...] * ac[...]
        conv = jnp.zeros((t, w), f32)
        for k in range(K_SHORT):
            conv = conv + wa[pl.ds(k, 1), :] * p_sc[pl.ds(hb - (K_SHORT - 1) + k, t), :]
        y_a = ab[...] * conv
        u_sc[0, pl.ds(0, hb), :] = prev(cv_p) * _sigmoid(prev(cg_p))
        u_sc[0, pl.ds(hb, t), :] = cv[...] * _sigmoid(cg[...])
        _fill_shifted(u_sc)
        u = jnp.zeros((t, w), f32) + cb_ref[...]
        for k in range(K_CONF):
            u = u + wc[pl.ds(k, 1), :] * _window(u_sc, hb - (K_CONF - 1) + k, t)
        u_ref[...] = u
        _, _, lnv = _layernorm_parts(u, lng_ref[...], lnb_ref[...])
        y_c = lnv * _sigmoid(lnv)
        ybv = yb_ref[...]
        y_ref[:, 0:256] = (y_a * _rstd(y_a) * og_ref[:, 0:256]).astype(bf16)
        y_ref[:, 256:768] = (ybv * _rstd(ybv) * og_ref[:, 256:768]).astype(bf16)
        y_ref[:, 768:1024] = (y_c * _rstd(y_c) * og_ref[:, 768:1024]).astype(bf16)

    full = lambda shape: pl.BlockSpec(shape, lambda i: tuple(0 for _ in shape))
    in_specs = [_cur(t, w, c) for c in (Z_AH, Z_AB, Z_AC, Z_CV, Z_CG)] + [_prev(hb, t, w, c) for c in (Z_AH, Z_AC, Z_CV, Z_CG)]
    in_specs += [_cur(t, 512, 0), full((K_SHORT, w)), full((K_CONF, w)), full((1, w)), full((1, w)), full((1, w)), full((1, 1024))]
    return _pcall(body, name=name, out_shape=(SDS((s, 1024), bf16), SDS((s, w), f32)), grid=(s // t,), in_specs=in_specs,
                  out_specs=(_cur(t, 1024, 0), _cur(t, w, 0)),
                  scratch_shapes=[pltpu.VMEM((hb + t, w), f32), pltpu.VMEM((8, hb + t, w), f32)],
                  compiler_params=_cp(("parallel",)))(z, z, z, z, z, z, z, z, z, yb, conv_a, conv_c, cb, lng, lnb, og)


PG_CONV_A, PG_CONV_C, PG_CB, PG_LNG, PG_LNB, PG_OG_A, PG_OG_C, PG_ROWS = 0, 8, 40, 41, 42, 43, 44, 48


def _mixer_bwd(z, yb, dy, u_saved, conv_a, conv_c, cb, lng, lnb, og, stacks, l, name):
    s = z.shape[0]
    t, hb, w = MIXER_ROWS, HALO, 256
    e = t + hb

    def body(ah, ab, ac, cv, cg, ah_p, ac_p, cv_p, cg_p, ah_n, ab_n, ac_n, cv_n, cg_n,
             dya, dyb1, dyb2, dyc, dya_n, dyc_n, yb_ref, wa, wc, cb_ref, lng_ref, lnb_ref, og_ref, pg_stack, dgb_stack, u_c, u_n,
             dz_ref, dyb_ref, pg_ref, dgb_ref, p_sc, u_sc, dc_sc, du_sc):
        i = pl.program_id(0)
        first, last = i == 0, i == pl.num_programs(0) - 1

        def prev(ref):
            return jnp.where(first, 0.0, ref[...])

        def nxt(ref):
            return jnp.where(last, 0.0, ref[...])

        def ext(cur_ref, next_ref):
            return jnp.concatenate([cur_ref[...], nxt(next_ref)], axis=0)

        @pl.when(first)
        def _():
            pg_ref[...] = jnp.zeros_like(pg_ref)
            dgb_ref[...] = jnp.zeros_like(dgb_ref)

        g_a = og_ref[:, 0:256]
        p_sc[pl.ds(0, hb), :] = prev(ah_p) * prev(ac_p)
        p_sc[pl.ds(hb, t), :] = ah[...] * ac[...]
        p_sc[pl.ds(hb + t, hb), :] = nxt(ah_n) * nxt(ac_n)
        conv = jnp.zeros((e, w), f32)
        for k in range(K_SHORT):
            conv = conv + wa[pl.ds(k, 1), :] * p_sc[pl.ds(hb - (K_SHORT - 1) + k, e), :]
        ab_e, dya_e = ext(ab, ab_n), ext(dya, dya_n)
        y_a = ab_e * conv
        r_a = _rstd(y_a)
        dy_a = _rms_dx(y_a, r_a, g_a, dya_e)
        dc_sc[...] = dy_a * ab_e
        dz_ref[:, 256:512] = (dy_a[:t] * conv[:t]).astype(bf16)
        dp = jnp.zeros((t, w), f32)
        for k in range(K_SHORT):
            dp = dp + wa[pl.ds(k, 1), :] * dc_sc[pl.ds(K_SHORT - 1 - k, t), :]
            pg_ref[pl.ds(PG_CONV_A + k, 1), :] += _colsum(dc_sc[pl.ds(0, t), :] * p_sc[pl.ds(hb - (K_SHORT - 1) + k, t), :])
        dz_ref[:, 0:256] = (dp * ac[...]).astype(bf16)
        dz_ref[:, 512:768] = (dp * ah[...]).astype(bf16)
        pg_ref[pl.ds(PG_OG_A, 1), :] += _colsum(dya[...] * y_a[:t] * r_a[:t])

        g_c = og_ref[:, 768:1024]
        u_sc[0, pl.ds(0, hb), :] = prev(cv_p) * _sigmoid(prev(cg_p))
        u_sc[0, pl.ds(hb, t), :] = cv[...] * _sigmoid(cg[...])
        u_sc[0, pl.ds(hb + t, hb), :] = nxt(cv_n) * _sigmoid(nxt(cg_n))
        _fill_shifted(u_sc)
        u = ext(u_c, u_n)
        xhat, rs, lnv = _layernorm_parts(u, lng_ref[...], lnb_ref[...])
        sg = _sigmoid(lnv)
        y_c = lnv * sg
        r_c = _rstd(y_c)
        dyc_e = ext(dyc, dyc_n)
        dlnv = _rms_dx(y_c, r_c, g_c, dyc_e) * (sg * (1.0 + lnv * (1.0 - sg)))
        dxh = dlnv * lng_ref[...]
        du = rs * (dxh - jnp.mean(dxh, axis=-1, keepdims=True) - xhat * jnp.mean(dxh * xhat, axis=-1, keepdims=True))
        du_sc[0] = du
        _fill_shifted(du_sc)
        duin = jnp.zeros((t, w), f32)
        for k in range(K_CONF):
            duin = duin + wc[pl.ds(k, 1), :] * _window(du_sc, K_CONF - 1 - k, t)
            pg_ref[pl.ds(PG_CONV_C + k, 1), :] += _colsum(du[:t] * _window(u_sc, hb - (K_CONF - 1) + k, t))
        sgg = _sigmoid(cg[...])
        dz_ref[:, 2304:2560] = (duin * sgg).astype(bf16)
        dz_ref[:, 2560:2816] = (duin * cv[...] * sgg * (1.0 - sgg)).astype(bf16)
        pg_ref[pl.ds(PG_CB, 1), :] += _colsum(du[:t])
        pg_ref[pl.ds(PG_LNG, 1), :] += _colsum(dlnv[:t] * xhat[:t])
        pg_ref[pl.ds(PG_LNB, 1), :] += _colsum(dlnv[:t])
        pg_ref[pl.ds(PG_OG_C, 1), :] += _colsum(dyc[...] * y_c[:t] * r_c[:t])

        g_b = og_ref[:, 256:768]
        ybv = yb_ref[...]
        r_b = _rstd(ybv)
        dyb = jnp.concatenate([dyb1[...], dyb2[...]], axis=1)
        dyb_ref[...] = _rms_dx(ybv, r_b, g_b, dyb)
        dgb_ref[...] += _colsum(dyb * ybv * r_b)

    full = lambda shape: pl.BlockSpec(shape, lambda i: tuple(0 for _ in shape))
    zc = (Z_AH, Z_AB, Z_AC, Z_CV, Z_CG)
    in_specs = [_cur(t, w, c) for c in zc] + [_prev(hb, t, w, c) for c in (Z_AH, Z_AC, Z_CV, Z_CG)]
    in_specs += [_next(hb, t, w, c, s) for c in zc]
    in_specs += [_cur(t, w, c) for c in (0, 1, 2, 3)] + [_next(hb, t, w, 0, s), _next(hb, t, w, 3, s)]
    in_specs += [_cur(t, 512, 0), full((K_SHORT, w)), full((K_CONF, w)), full((1, w)), full((1, w)), full((1, w)), full((1, 1024))]
    in_specs += [pl.BlockSpec(memory_space=pl.ANY)] * 2 + [_cur(t, w, 0), _next(hb, t, w, 0, s)]
    out_shape = (SDS((s, z.shape[1]), bf16), SDS((s, 512), f32), SDS(stacks[0].shape, f32), SDS(stacks[1].shape, f32))
    out_specs = (_cur(t, z.shape[1], 0), _cur(t, 512, 0), pl.BlockSpec((None, PG_ROWS, w), lambda i: (l, 0, 0)),
                 pl.BlockSpec((None, 1, 512), lambda i: (l, 0, 0)))
    scratch = [pltpu.VMEM((hb + t + hb, w), f32), pltpu.VMEM((8, hb + t + hb, w), f32), pltpu.VMEM((e, w), f32), pltpu.VMEM((8, e, w), f32)]
    args = [z] * 14 + [dy] * 6 + [yb, conv_a, conv_c, cb, lng, lnb, og] + list(stacks) + [u_saved, u_saved]
    return _pcall(body, name=name, out_shape=out_shape, grid=(s // t,), in_specs=in_specs, out_specs=out_specs,
                  input_output_aliases={27: 2, 28: 3}, scratch_shapes=scratch, compiler_params=_cp(("arbitrary",)))(*args)


def _ffn_act_fwd(upre, conv_f, name):
    s, f2 = upre.shape
    t, hb, w = FFN_FWD_ROWS, HALO_FFN, 256
    nj = f2 // 2 // w

    def body(g_c, v_c, g_p, v_p, wg, wv, o_ref):
        first = pl.program_id(1) == 0
        wgs = [wg[pl.ds(k, 1), :] for k in range(K_FFN)]
        wvs = [wv[pl.ds(k, 1), :] for k in range(K_FFN)]
        rows = FFN_CHUNK
        first_tap = hb - (K_FFN - 1)

        def emit(gb, vb, r0):
            ug = sum(wgs[k] * gb[first_tap + k:first_tap + k + rows] for k in range(K_FFN))
            uv = sum(wvs[k] * vb[first_tap + k:first_tap + k + rows] for k in range(K_FFN))
            o_ref[pl.ds(r0, rows), :] = (ug * _sigmoid(ug) * uv).astype(bf16)

        emit(jnp.concatenate([jnp.where(first, 0.0, g_p[...]), g_c[pl.ds(0, rows), :]], axis=0),
             jnp.concatenate([jnp.where(first, 0.0, v_p[...]), v_c[pl.ds(0, rows), :]], axis=0), 0)

        def chunk(c, carry):
            r0 = pl.multiple_of(c * rows, rows)
            back = pl.ds(pl.multiple_of(r0 - hb, hb), rows + hb)
            emit(g_c[back, :], v_c[back, :], r0)
            return carry

        lax.fori_loop(1, t // rows, chunk, 0, unroll=t // rows - 1)

    cur = lambda off: pl.BlockSpec((t, w), lambda j, i: (i, off + j))
    prv = lambda off: pl.BlockSpec((hb, w), lambda j, i: (jnp.maximum(i * (t // hb) - 1, 0), off + j))
    wsp = lambda off: pl.BlockSpec((K_FFN, w), lambda j, i: (0, off + j))
    return _pcall(body, name=name, out_shape=SDS((s, f2 // 2), bf16), grid=(nj, s // t),
                  in_specs=[cur(0), cur(nj), prv(0), prv(nj), wsp(0), wsp(nj)],
                  out_specs=pl.BlockSpec((t, w), lambda j, i: (i, j)),
                  compiler_params=_cp(("parallel", "parallel")))(upre, upre, upre, upre, conv_f, conv_f)


def _ffn_act_bwd(upre, dact, conv_f, stacks, l, name):
    s, f2 = upre.shape
    t, hb, w = FFN_BWD_ROWS, HALO_FFN, 256
    nj = f2 // 2 // w

    def body(g_c, v_c, g_p, v_p, g_n, v_n, da_c, da_n, wg, wv, sg_ref, sv_ref, dg_ref, dv_ref, dwg_ref, dwv_ref):
        i = pl.program_id(1)
        first, last = i == 0, i == pl.num_programs(1) - 1

        @pl.when(first)
        def _():
            dwg_ref[...] = jnp.zeros_like(dwg_ref)
            dwv_ref[...] = jnp.zeros_like(dwv_ref)

        wgs = [wg[pl.ds(k, 1), :] for k in range(K_FFN)]
        wvs = [wv[pl.ds(k, 1), :] for k in range(K_FFN)]
        rows = FFN_CHUNK
        n = rows + hb
        first_tap = hb - (K_FFN - 1)
        nchunks = t // rows

        def groups_of_8(x):
            return sum(x[8 * q:8 * q + 8] for q in range(rows // 8))

        def emit(gb, vb, da, r0, carry):
            g_taps = [gb[first_tap + k:first_tap + k + n] for k in range(K_FFN)]
            v_taps = [vb[first_tap + k:first_tap + k + n] for k in range(K_FFN)]
            ug = sum(wgs[k] * g_taps[k] for k in range(K_FFN))
            uv = sum(wvs[k] * v_taps[k] for k in range(K_FFN))
            sg = _sigmoid(ug)
            dv = da * (ug * sg)
            dg = da * uv * (sg * (1.0 + ug * (1.0 - sg)))
            new = []
            for taps, dval, ws, dref in ((g_taps, dg, wgs, dg_ref), (v_taps, dv, wvs, dv_ref)):
                acc = sum(ws[k] * dval[K_FFN - 1 - k:K_FFN - 1 - k + rows] for k in range(K_FFN))
                dref[pl.ds(r0, rows), :] = acc.astype(bf16)
                new += [groups_of_8(dval[:rows] * taps[k][:rows]) for k in range(K_FFN)]
            return tuple(a + b for a, b in zip(carry, new))

        def chunk(c, carry):
            r0 = pl.multiple_of(c * rows, rows)
            around = pl.ds(pl.multiple_of(r0 - hb, hb), n + hb)
            return emit(g_c[around, :], v_c[around, :], da_c[pl.ds(r0, n), :], r0, carry)

        def edge(prev_ref, cur_ref, lo):
            return jnp.concatenate([jnp.where(first, 0.0, prev_ref[...]), cur_ref[pl.ds(0, lo), :]], axis=0)

        def tail(cur_ref, next_ref, lo):
            return jnp.concatenate([cur_ref[pl.ds(t - lo, lo), :], jnp.where(last, 0.0, next_ref[...])], axis=0)

        zero = jnp.zeros((8, w), f32)
        sums = emit(edge(g_p, g_c, n), edge(v_p, v_c, n), da_c[pl.ds(0, n), :], 0, (zero,) * (2 * K_FFN))
        sums = lax.fori_loop(1, nchunks - 1, chunk, sums, unroll=nchunks - 2)
        sums = emit(tail(g_c, g_n, rows + hb), tail(v_c, v_n, rows + hb), tail(da_c, da_n, rows), t - rows, sums)
        for k in range(K_FFN):
            dwg_ref[pl.ds(k, 1), :] += _colsum(sums[k])
            dwv_ref[pl.ds(k, 1), :] += _colsum(sums[K_FFN + k])

    cur = lambda off: pl.BlockSpec((t, w), lambda j, i: (i, off + j))
    prv = lambda off: pl.BlockSpec((hb, w), lambda j, i: (jnp.maximum(i * (t // hb) - 1, 0), off + j))
    nxt = lambda off: pl.BlockSpec((hb, w), lambda j, i: (jnp.minimum((i + 1) * (t // hb), s // hb - 1), off + j))
    wsp = lambda off: pl.BlockSpec((K_FFN, w), lambda j, i: (0, off + j))
    half = SDS((s, f2 // 2), bf16)
    dws = SDS(stacks[0].shape, f32)
    o_cur = pl.BlockSpec((t, w), lambda j, i: (i, j))
    o_dw = pl.BlockSpec((None, 8, w), lambda j, i: (l, 0, j))
    anywhere = pl.BlockSpec(memory_space=pl.ANY)
    return _pcall(body, name=name, out_shape=(half, half, dws, dws), grid=(nj, s // t),
                  in_specs=[cur(0), cur(nj), prv(0), prv(nj), nxt(0), nxt(nj), cur(0), nxt(0), wsp(0), wsp(nj), anywhere, anywhere],
                  out_specs=(o_cur, o_cur, o_dw, o_dw), input_output_aliases={10: 2, 11: 3},
                  compiler_params=_cp(("parallel", "arbitrary")))(upre, upre, upre, upre, upre, upre, dact, dact, conv_f, conv_f, *stacks)


def _bias_tables(rel_bias, name):
    nb = len(DILATIONS)

    def body(rb_ref, idx_ref, o_ref):
        h = pl.program_id(1)
        iv = idx_ref[0]
        qi = lax.broadcasted_iota(jnp.int32, (BLK, 2 * BLK), 0)
        kj = lax.broadcasted_iota(jnp.int32, (BLK, 2 * BLK), 1)
        band = jnp.logical_and(kj >= qi, kj <= qi + N_KEYS)
        acc = jnp.zeros((BLK, 2 * BLK), f32)
        for b in range(N_BUCKETS):
            acc = jnp.where(iv == b, rb_ref[b, h], acc)
        o_ref[0, 0, 0] = jnp.where(band, acc, NEG)
        o_ref[0, 1, 0] = jnp.where(jnp.logical_and(band, kj >= BLK), acc, NEG)

    return _pcall(body, name=name, out_shape=SDS((nb, 2, 8, BLK, 2 * BLK), f32), grid=(nb, 8),
                  in_specs=[pl.BlockSpec(memory_space=pltpu.SMEM), pl.BlockSpec((1, BLK, 2 * BLK), lambda br, h: (br, 0, 0))],
                  out_specs=pl.BlockSpec((1, 2, 1, BLK, 2 * BLK), lambda br, h: (br, 0, h, 0, 0)),
                  compiler_params=_cp(("parallel", "parallel")))(rel_bias, jnp.asarray(_BUCKETS))


def _bias_grad(ds_tabs, name):
    nl, nb = ds_tabs.shape[0], ds_tabs.shape[1]

    def body(t_ref, idx_ref, o_ref):
        br, h = pl.program_id(0), pl.program_id(1)

        @pl.when(jnp.logical_and(br == 0, h == 0))
        def _():
            o_ref[...] = jnp.zeros_like(o_ref)

        tab = t_ref[0, 0, 0]
        for l in range(1, nl):
            tab = tab + t_ref[l, 0, 0]
        iv = idx_ref[0]
        row = lax.broadcasted_iota(jnp.int32, (N_BUCKETS, 128), 0)
        col = lax.broadcasted_iota(jnp.int32, (N_BUCKETS, 128), 1)
        acc = jnp.zeros((N_BUCKETS, 128), f32)
        for b in range(N_BUCKETS):
            sel = jnp.where(iv == b, tab, 0.0)
            tot = jnp.sum(jnp.sum(sel, axis=0, keepdims=True), axis=1, keepdims=True)
            acc = acc + jnp.where(jnp.logical_and(row == b, col == h), tot, 0.0)
        o_ref[...] += acc

    return _pcall(body, name=name, out_shape=SDS((N_BUCKETS, 128), f32), grid=(nb, 8),
                  in_specs=[pl.BlockSpec((nl, 1, 1, BLK, 2 * BLK), lambda br, h: (0, br, h, 0, 0)),
                            pl.BlockSpec((1, BLK, 2 * BLK), lambda br, h: (br, 0, 0))],
                  out_specs=pl.BlockSpec((N_BUCKETS, 128), lambda br, h: (0, 0)),
                  compiler_params=_cp(("arbitrary", "arbitrary")))(ds_tabs, jnp.asarray(_BUCKETS))


def _stack_heads(x, head0):
    zero = jnp.zeros_like(x)
    return jnp.concatenate([jnp.where(head0, x, zero), jnp.where(head0, zero, x)], axis=0)


def _block_rows(blk, d, nblk):
    per_res = nblk // d
    res, n = blk // per_res, blk % per_res
    cur = pl.ds(res + d * BLK * n, BLK, stride=d) if d > 1 else pl.ds(pl.multiple_of(BLK * n, BLK), BLK)
    before = jnp.maximum(n - 1, 0)
    prev = pl.ds(res + d * BLK * before, BLK, stride=d) if d > 1 else pl.ds(pl.multiple_of(BLK * before, BLK), BLK)
    return cur, prev, jnp.where(n == 0, 1, 0)


def _attn_fwd(z, bias, name):
    s = z.shape[0]
    nblk = s // BLK

    def body(q_ref, k_ref, v_ref, b_ref, y_ref, l_ref):
        lane = lax.broadcasted_iota(jnp.int32, (BLK, BLK), 1)
        head0 = lane < D_HEAD

        for branch, d in enumerate(DILATIONS):
            def step(blk, carry, branch=branch, d=d):
                cur, prev, variant = _block_rows(blk, d, nblk)
                q = (q_ref[cur, :] * (D_HEAD ** -0.5)).astype(bf16)
                if d == nblk:
                    kk, vv = k_ref[cur, :].astype(bf16), v_ref[cur, :].astype(bf16)
                    bias_blk = b_ref[branch, 1, 0, :, BLK:]
                else:
                    kk = jnp.concatenate([k_ref[prev, :], k_ref[cur, :]], axis=0).astype(bf16)
                    vv = jnp.concatenate([v_ref[prev, :], v_ref[cur, :]], axis=0).astype(bf16)
                    bias_blk = b_ref[branch, variant, 0]
                sc = lax.dot_general(_stack_heads(q, head0), kk, (((1,), (1,)), ((), ())), preferred_element_type=f32)
                sc = sc + bias_blk
                m = jnp.max(sc, axis=-1, keepdims=True)
                p = jnp.exp(sc - m)
                den = jnp.sum(p, axis=-1, keepdims=True)
                out = jnp.dot(p.astype(bf16), vv, preferred_element_type=f32) / den
                lse = m + jnp.log(den)
                out = jnp.where(head0, out[:BLK], out[BLK:])
                lse = jnp.where(head0, lse[:BLK], lse[BLK:]) + jnp.zeros((BLK, BLK), f32)
                if branch == 0:
                    y_ref[cur, :] = out
                    l_ref[cur, :] = lse
                else:
                    y_old, l_old = y_ref[cur, :], l_ref[cur, :]
                    top = jnp.maximum(l_old, lse)
                    e_old, e_new = jnp.exp(l_old - top), jnp.exp(lse - top)
                    tot = e_old + e_new
                    y_ref[cur, :] = (e_old * y_old + e_new * out) / tot
                    l_ref[cur, :] = top + jnp.log(tot)
                return carry

            lax.fori_loop(0, nblk, step, 0, unroll=16)

    col = lambda off: pl.BlockSpec((s, BLK), lambda hp: (0, off + hp))
    out = SDS((s, 4 * BLK), f32)
    nbr = len(DILATIONS)
    return _pcall(body, name=name, out_shape=(out, out), grid=(4,),
                  in_specs=[col(Z_Q128), col(Z_K128), col(Z_V128), pl.BlockSpec((nbr, 2, 1, 2 * BLK, 2 * BLK), lambda hp: (0, 0, hp, 0, 0))],
                  out_specs=(col(0), col(0)), compiler_params=_cp(("parallel",)))(z, z, z, bias)


def _attn_bwd(z, dyb, yb, lt, bias, dz, name):
    s = z.shape[0]
    nblk = s // BLK
    scale = D_HEAD ** -0.5
    chunk = 256

    def body(q_ref, k_ref, v_ref, dy_ref, y_ref, lt_ref, b_ref, dz_in, dz_ref, ds_ref, aq, ak, av, stage, sems):
        lane = lax.broadcasted_iota(jnp.int32, (BLK, BLK), 1)
        head0 = lane < D_HEAD
        ak[...] = jnp.zeros_like(ak)
        av[...] = jnp.zeros_like(av)
        ds_ref[...] = jnp.zeros_like(ds_ref)

        for branch, d in enumerate(DILATIONS):
            def step(blk, carry, branch=branch, d=d):
                cur, prev, variant = _block_rows(blk, d, nblk)
                own_keys = d == nblk
                q = (q_ref[cur, :] * scale).astype(bf16)
                if own_keys:
                    kk, vv = k_ref[cur, :].astype(bf16), v_ref[cur, :].astype(bf16)
                    bias_blk = b_ref[branch, 1, 0, :, BLK:]
                else:
                    kk = jnp.concatenate([k_ref[prev, :], k_ref[cur, :]], axis=0).astype(bf16)
                    vv = jnp.concatenate([v_ref[prev, :], v_ref[cur, :]], axis=0).astype(bf16)
                    bias_blk = b_ref[branch, variant, 0]
                dy = dy_ref[cur, :]
                lt_b = lt_ref[cur, :]
                qs = _stack_heads(q, head0)
                dys = _stack_heads(dy.astype(bf16), head0)
                dd = jnp.sum(_stack_heads(dy * y_ref[cur, :], head0), axis=-1, keepdims=True)
                lts = jnp.concatenate([lt_b[:, 0:1], lt_b[:, D_HEAD:D_HEAD + 1]], axis=0)
                sc = lax.dot_general(qs, kk, (((1,), (1,)), ((), ())), preferred_element_type=f32)
                p = jnp.exp(sc + bias_blk - lts)
                dp = lax.dot_general(dys, vv, (((1,), (1,)), ((), ())), preferred_element_type=f32)
                ds = p * (dp - dd)
                if own_keys:
                    ds_ref[branch, 0, :, BLK:] += ds
                else:
                    ds_ref[branch, 0] += ds
                dsb = ds.astype(bf16)
                dq = jnp.dot(dsb, kk, preferred_element_type=f32) * scale
                dkk = lax.dot_general(dsb, qs, (((0,), (0,)), ((), ())), preferred_element_type=f32)
                dvv = lax.dot_general(p.astype(bf16), dys, (((0,), (0,)), ((), ())), preferred_element_type=f32)
                dq = jnp.where(head0, dq[:BLK], dq[BLK:])
                if branch == 0:
                    aq[cur, :] = dq
                else:
                    aq[cur, :] += dq
                if own_keys:
                    ak[cur, :] += dkk
                    av[cur, :] += dvv
                else:
                    ak[prev, :] += dkk[:BLK]
                    ak[cur, :] += dkk[BLK:]
                    av[prev, :] += dvv[:BLK]
                    av[cur, :] += dvv[BLK:]
                return carry

            lax.fori_loop(0, nblk, step, 0, unroll=16)

        def store(c, carry):
            rows = pl.ds(pl.multiple_of(c * chunk, chunk), chunk)
            stage[0, rows, :] = aq[rows, :].astype(bf16)
            stage[1, rows, :] = ak[rows, :].astype(bf16)
            stage[2, rows, :] = av[rows, :].astype(bf16)
            return carry

        lax.fori_loop(0, s // chunk, store, 0)
        hp = pl.program_id(0)
        copies = []
        for t, first_col in enumerate((Z_Q128, Z_K128, Z_V128)):
            cols = pl.ds(pl.multiple_of((first_col + hp) * BLK, BLK), BLK)
            copies.append(pltpu.make_async_copy(stage.at[t], dz_ref.at[:, cols], sems.at[t]))
            copies[-1].start()
        for cp in copies:
            cp.wait()

    col = lambda off: pl.BlockSpec((s, BLK), lambda hp: (0, off + hp))
    nbr = len(DILATIONS)
    tabs = pl.BlockSpec((nbr, 1, 2 * BLK, 2 * BLK), lambda hp: (0, hp, 0, 0))
    bias_spec = pl.BlockSpec((nbr, 2, 1, 2 * BLK, 2 * BLK), lambda hp: (0, 0, hp, 0, 0))
    anywhere = pl.BlockSpec(memory_space=pl.ANY)
    scratch = [pltpu.VMEM((s, BLK), f32)] * 3 + [pltpu.VMEM((3, s, BLK), bf16), pltpu.SemaphoreType.DMA((3,))]
    return _pcall(body, name=name, out_shape=(SDS(dz.shape, dz.dtype), SDS((nbr, 4, 2 * BLK, 2 * BLK), f32)), grid=(4,),
                  in_specs=[col(Z_Q128), col(Z_K128), col(Z_V128), col(0), col(0), col(0), bias_spec, anywhere],
                  out_specs=(anywhere, tabs), input_output_aliases={7: 0}, scratch_shapes=scratch,
                  compiler_params=_cp(("arbitrary",)))(z, z, z, dyb, yb, lt, bias, dz)


def _adamw_math(w, g, m, v):
    m = ADAM_B1 * m + (1.0 - ADAM_B1) * g
    v = ADAM_B2 * v + (1.0 - ADAM_B2) * (g * g)
    m_hat = m / (1.0 - ADAM_B1 ** ADAM_STEP)
    v_hat = v / (1.0 - ADAM_B2 ** ADAM_STEP)
    delta = -ADAM_LR * (m_hat / (jnp.sqrt(v_hat) + ADAM_EPS) + ADAM_WD * w)
    return delta, m, v


def _adamw_layer(land, send, w, m, v, prev, l, tr, me, name):
    _, r, c = w.shape
    assert r % tr == 0

    def body(me_ref, land_ref, own_ref, w_ref, m_ref, v_ref, p0, p1, p2, p3, g_ref, d_ref, nm_ref, nv_ref):
        mine = jnp.zeros((tr, c), jnp.int32) + me_ref[0]
        own = own_ref[...]
        g = jnp.zeros((tr, c), f32)
        for j in range(N_DEV):
            g = g + jnp.where(mine == j, own, land_ref[j]).astype(f32)
        delta, nm, nv = _adamw_math(w_ref[...], g, m_ref[...], v_ref[...])
        g_ref[...] = g
        d_ref[...] = delta
        nm_ref[...] = nm
        nv_ref[...] = nv

    parts = pl.BlockSpec((N_DEV, tr, c), lambda i, me_ref: (0, i, 0))
    own = pl.BlockSpec((None, tr, c), lambda i, me_ref: (me_ref[0], i, 0))
    row = pl.BlockSpec((None, tr, c), lambda i, me_ref: (l, i, 0))
    anywhere = pl.BlockSpec(memory_space=pl.ANY)
    out = SDS(w.shape, f32)
    grid_spec = pltpu.PrefetchScalarGridSpec(num_scalar_prefetch=1, grid=(r // tr,), in_specs=[parts, own, row, row, row] + [anywhere] * 4,
                                             out_specs=(row, row, row, row))
    return _pcall(body, name=name, out_shape=(out, out, out, out), grid_spec=grid_spec,
                  input_output_aliases={6: 0, 7: 1, 8: 2, 9: 3}, compiler_params=_cp(("parallel",)))(me, land, send, w, m, v, *prev)


def _sum_parts(parts, after, name):
    _, r, c = parts.shape

    def body(p_ref, after_ref, o_ref):
        g = p_ref[0]
        for j in range(1, N_DEV):
            g = g + p_ref[j]
        o_ref[...] = g

    return _pcall(body, name=name, out_shape=SDS((r, c), f32),
                  in_specs=[pl.BlockSpec(memory_space=pltpu.VMEM), pl.BlockSpec(memory_space=pl.ANY)],
                  out_specs=pl.BlockSpec(memory_space=pltpu.VMEM), compiler_params=_cp())(parts, after)


def _adamw_small(gs, ws, ms, vs, name):
    n = len(ws)

    def body(*refs):
        for t in range(n):
            delta, nm, nv = _adamw_math(refs[n + t][...], refs[t][...], refs[2 * n + t][...], refs[3 * n + t][...])
            refs[4 * n + t][...] = delta
            refs[5 * n + t][...] = nm
            refs[6 * n + t][...] = nv

    out = [SDS(wt.shape, f32) for wt in ws] * 3
    res = _pcall(body, name=name, out_shape=out, compiler_params=_cp())(*gs, *ws, *ms, *vs)
    return list(res[:n]), list(res[n:2 * n]), list(res[2 * n:])


def _my_place():
    return lax.axis_index("x"), lax.axis_index("y"), lax.axis_index("c")


def _all_gather(xs, name, after=None):
    n = len(xs)

    def body(*refs):
        x_refs, out_refs = refs[:n], refs[-3 - n:-3]
        send_sems, recv_sems, local_sems = refs[-3:]
        mx, my, mc = _my_place()
        me, sibling = (mx, my, mc), (mx, my, 1 - mc)
        chips = [(1 - mx, my), (mx, 1 - my), (1 - mx, 1 - my)]

        def rows(t, px, py, pc):
            return out_refs[t].at[4 * px + 2 * py + pc]

        def copy(t, k, block, to, src=None):
            return pltpu.make_async_remote_copy(src_ref=rows(t, *block) if src is None else src, dst_ref=rows(t, *block),
                                                send_sem=send_sems.at[7 * t + k], recv_sem=recv_sems.at[7 * t + k],
                                                device_id=to, device_id_type=MESH)

        mine = [pltpu.make_async_copy(x_refs[t], rows(t, *me), local_sems.at[t]) for t in range(n)]
        first, passed = [], []
        for t in range(n):
            mine[t].start()
            first += [copy(t, 0, me, sibling, src=x_refs[t])] + [copy(t, 1 + j, me, (*chip, mc), src=x_refs[t]) for j, chip in enumerate(chips)]
        for cp in first:
            cp.start()
        for t in range(n):
            for j, chip in enumerate(chips):
                copy(t, 1 + j, (*chip, mc), me).wait_recv()
                passed.append(copy(t, 4 + j, (*chip, mc), sibling))
                passed[-1].start()
        for t in range(n):
            copy(t, 0, sibling, me).wait_recv()
            for j, chip in enumerate(chips):
                copy(t, 4 + j, (*chip, 1 - mc), me).wait_recv()
        for cp in first + passed:
            cp.wait_send()
        for t in range(n):
            mine[t].wait()

    ins = list(xs) + ([after] if after is not None else [])
    anywhere = pl.BlockSpec(memory_space=pl.ANY)
    return list(_pcall(body, name=name, out_shape=[SDS((N_DEV,) + x.shape, x.dtype) for x in xs],
                       in_specs=[anywhere] * len(ins), out_specs=[anywhere] * n,
                       scratch_shapes=[pltpu.SemaphoreType.DMA((7 * n,)), pltpu.SemaphoreType.DMA((7 * n,)),
                                       pltpu.SemaphoreType.DMA((n,))])(*ins))


HBM_SPEC = pl.BlockSpec(memory_space=pltpu.HBM)
SEM_SPEC = pl.BlockSpec(memory_space=pltpu.SEMAPHORE)
ANY_SPEC = pl.BlockSpec(memory_space=pl.ANY)
DATAFLOW = pltpu.SideEffectType.DATAFLOW_SIDE_EFFECTING


def _peers():
    mx, my, mc = _my_place()
    out = []
    for k in range(1, N_DEV):
        tx, ty, tc = mx ^ ((k >> 2) & 1), my ^ ((k >> 1) & 1), mc ^ (k & 1)
        out.append(((tx, ty, tc), 4 * tx + 2 * ty + tc))
    return out


def _in_hbm(arrays):
    return [pltpu.with_memory_space_constraint(a, pltpu.HBM) for a in arrays]


N_FIRST_LEG, N_SECOND_LEG = 4, 3


def _chip_peers():
    mx, my, mc = _my_place()
    return [((tx, ty, mc), 4 * tx + 2 * ty + mc) for tx, ty in ((1 - mx, my), (mx, 1 - my), (1 - mx, 1 - my))]


def _gather_start(bufs, after, name):
    n = len(bufs)

    def body(*refs):
        ins, s_sem, r_sem, token = refs[:n], refs[n + 1], refs[n + 2], refs[-1]
        mx, my, mc = _my_place()
        me = 4 * mx + 2 * my + mc
        for t in range(n):
            for dev in [(mx, my, 1 - mc)] + [d for d, _ in _chip_peers()]:
                pltpu.make_async_remote_copy(src_ref=ins[t].at[me], dst_ref=ins[t].at[me], send_sem=s_sem.at[t],
                                             recv_sem=r_sem.at[t], device_id=dev, device_id_type=MESH).start()
        token[...] = jnp.zeros_like(token)

    out_shape = ((pltpu.SemaphoreType.DMA((n,)), pltpu.SemaphoreType.DMA((n,))) + tuple(pltpu.HBM(b.shape, b.dtype) for b in bufs)
                 + (SDS((8, 128), f32),))
    out = _pcall(body, name=name, out_shape=out_shape, in_specs=(HBM_SPEC,) * n + (ANY_SPEC,),
                 out_specs=(SEM_SPEC, SEM_SPEC) + (HBM_SPEC,) * n + (pl.BlockSpec(memory_space=pltpu.VMEM),),
                 input_output_aliases={i: 2 + i for i in range(n)},
                 compiler_params=pltpu.CompilerParams(has_side_effects=DATAFLOW))(*_in_hbm(bufs), after)
    return (out[0], out[1]), list(out[2:2 + n]), out[-1]


def _gather_forward(bufs, sems, after, name, fresh=()):
    n, nf = len(bufs), len(fresh)
    both = list(bufs) + list(fresh)

    nsem = 4 if nf else 2

    def body(*refs):
        ins, news, s1, r1 = refs[:n], refs[n:n + nf], refs[n + nf], refs[n + nf + 1]
        s2, r2 = refs[n + nf + 3:n + nf + 5]
        sf, rf = refs[n + nf + 5:n + nf + 7] if nf else (None, None)
        token = refs[-1]
        mx, my, mc = _my_place()
        me, sibling = 4 * mx + 2 * my + mc, (mx, my, 1 - mc)
        for t in range(n):
            legs = ins[t].at[pl.ds(0, N_FIRST_LEG)]
            first = pltpu.make_async_remote_copy(src_ref=legs, dst_ref=legs, send_sem=s1.at[t], recv_sem=r1.at[t],
                                                 device_id=sibling, device_id_type=MESH)
            first.wait_send()
            first.wait_recv()
        for t in range(n):
            for _, idx in _chip_peers():
                pltpu.make_async_remote_copy(src_ref=ins[t].at[idx], dst_ref=ins[t].at[idx], send_sem=s2.at[t],
                                             recv_sem=r2.at[t], device_id=sibling, device_id_type=MESH).start()
        for t in range(nf):
            for dev in [sibling] + [d for d, _ in _chip_peers()]:
                pltpu.make_async_remote_copy(src_ref=news[t].at[me], dst_ref=news[t].at[me], send_sem=sf.at[t],
                                             recv_sem=rf.at[t], device_id=dev, device_id_type=MESH).start()
        token[...] = jnp.zeros_like(token)

    sem = lambda k: pltpu.SemaphoreType.DMA((k,))
    sem_shapes = (sem(n), sem(n)) + ((sem(nf), sem(nf)) if nf else ())
    out_shape = sem_shapes + tuple(pltpu.HBM(b.shape, b.dtype) for b in both) + (SDS((8, 128), f32),)
    out = _pcall(body, name=name, out_shape=out_shape, in_specs=(HBM_SPEC,) * (n + nf) + (SEM_SPEC, SEM_SPEC, ANY_SPEC),
                 out_specs=(SEM_SPEC,) * nsem + (HBM_SPEC,) * (n + nf) + (pl.BlockSpec(memory_space=pltpu.VMEM),),
                 input_output_aliases={i: nsem + i for i in range(n + nf)},
                 compiler_params=pltpu.CompilerParams(has_side_effects=DATAFLOW))(*bufs, *_in_hbm(fresh), sems[0], sems[1], after)
    fresh_sems = (out[2], out[3]) if nf else None
    return (out[0], out[1]), list(out[nsem:nsem + n]), fresh_sems, list(out[nsem + n:nsem + n + nf]), out[-1]


def _exchange_start(sends, lands, name, after=None):
    nt = len(sends)
    extra = [] if after is None else [after]

    def body(*refs):
        srcs, dsts, token = refs[:nt], refs[nt:2 * nt], refs[-1]
        s_sem, r_sem = refs[2 * nt + len(extra)], refs[2 * nt + len(extra) + 1]
        mx, my, mc = _my_place()
        me = 4 * mx + 2 * my + mc
        for t in range(nt):
            for dev, idx in _peers():
                pltpu.make_async_remote_copy(src_ref=srcs[t].at[idx], dst_ref=dsts[t].at[me], send_sem=s_sem.at[t],
                                             recv_sem=r_sem.at[t], device_id=dev, device_id_type=MESH).start()
        token[...] = jnp.zeros_like(token)

    both = list(sends) + list(lands)
    out_shape = (pltpu.SemaphoreType.DMA((nt,)), pltpu.SemaphoreType.DMA((nt,))) + tuple(pltpu.HBM(b.shape, b.dtype) for b in both)
    out = _pcall(body, name=name, out_shape=out_shape + (SDS((8, 128), f32),), in_specs=(HBM_SPEC,) * (2 * nt) + (ANY_SPEC,) * len(extra),
                 out_specs=(SEM_SPEC, SEM_SPEC) + (HBM_SPEC,) * (2 * nt) + (pl.BlockSpec(memory_space=pltpu.VMEM),),
                 input_output_aliases={i: 2 + i for i in range(2 * nt)},
                 compiler_params=pltpu.CompilerParams(has_side_effects=DATAFLOW))(*_in_hbm(both), *extra)
    return (out[0], out[1]), list(out[2:2 + nt]), list(out[2 + nt:2 + 2 * nt]), out[-1]


def _copies_wait(srcs, dsts, sems, after, name, ncopies=N_DEV - 1):
    nt = len(srcs)
    same = srcs is dsts
    bufs = list(srcs) if same else list(srcs) + list(dsts)
    nb = len(bufs)

    def body(*refs):
        s_sem, r_sem = refs[nb], refs[nb + 1]
        mx, my, mc = _my_place()
        for t in range(nt):
            src = refs[t].at[pl.ds(0, ncopies)]
            dst = src if same else refs[nt + t].at[pl.ds(0, ncopies)]
            cp = pltpu.make_async_remote_copy(src_ref=src, dst_ref=dst, send_sem=s_sem.at[t], recv_sem=r_sem.at[t],
                                              device_id=(mx, my, 1 - mc), device_id_type=MESH)
            cp.wait_send()
            cp.wait_recv()

    out = _pcall(body, name=name, out_shape=tuple(pltpu.HBM(b.shape, b.dtype) for b in bufs),
                 in_specs=(HBM_SPEC,) * nb + (SEM_SPEC, SEM_SPEC, ANY_SPEC), out_specs=(HBM_SPEC,) * nb,
                 input_output_aliases={i: i for i in range(nb)},
                 compiler_params=pltpu.CompilerParams(has_side_effects=DATAFLOW))(*bufs, sems[0], sems[1], after)
    out = list(out)
    return (out, out) if same else (out[:nt], out[nt:])


def _cast_blocks(weights, l, me, after, name):
    nt = len(weights)
    halves = 2
    extra = [] if after is None else [after]

    def body(me_ref, *refs):
        outs = refs[nt + len(extra):]
        for t in range(nt):
            outs[t][...] = refs[t][...].astype(bf16)

    in_specs, out_specs, out_shape = [], [], []
    for wt in weights:
        _, r, c = wt.shape
        tr = r // halves
        in_specs.append(pl.BlockSpec((None, tr, c), lambda i, me_ref: (l, i, 0)))
        out_specs.append(pl.BlockSpec((None, tr, c), lambda i, me_ref: (me_ref[0], i, 0)))
        out_shape.append(SDS((N_DEV, r, c), bf16))
    in_specs += [pl.BlockSpec(memory_space=pl.ANY)] * len(extra)
    grid_spec = pltpu.PrefetchScalarGridSpec(num_scalar_prefetch=1, grid=(halves,), in_specs=in_specs, out_specs=out_specs)
    return list(_pcall(body, name=name, out_shape=out_shape, grid_spec=grid_spec, compiler_params=_cp(("parallel",)))(me, *weights, *extra))

def _layer_fwd_mix(x, p, bias, l, after=None):
    tag = f"l{l}"
    z = _mm(x, p["w_in"], "nt", 1024, 1408, f"{tag}_mm_in", after=after, norm_a=p["norm_mix_g"])
    yb, lt = _attn_fwd(z, bias, f"{tag}_attn_fwd")
    y, u = _mixer_fwd(z, yb, p["conv_a_w"], p["conv_c_w"], p["conv_c_b"], p["ln_c_g"], p["ln_c_b"], p["out_norm_g"], f"{tag}_mixer_fwd")
    return y, dict(x=x, z=z, yb=yb, lt=lt, y=y, u=u)


def _layer_fwd_ffn(x, y, p, l, after=None):
    tag = f"l{l}"
    x1 = _mm(y, p["w_out"], "nn", 512, 1024, f"{tag}_mm_out", res=x, after=after)
    upre = _mm(x1, p["w_up"], "nt", 1024, 1408, f"{tag}_mm_up", norm_a=p["norm_ffn_g"])
    act = _ffn_act_fwd(upre, p["conv_f_w"], f"{tag}_ffn_act")
    x2 = _mm(act, p["w_down"], "nn", 512, 1024, f"{tag}_mm_down", res=x1)
    return x2, dict(x1=x1, upre=upre, act=act)


def _small_stacks(depth, d_model, d_ff):
    shapes = dict(norm_ffn=(depth, 1, d_model), norm_mix=(depth, 1, d_model), conv_f_gate=(depth, 8, d_ff),
                  conv_f_val=(depth, 8, d_ff), mixers=(depth, PG_ROWS, 256), og_b=(depth, 1, 512))
    return {k: lax.empty(shp, f32) for k, shp in shapes.items()}


def _small_grads(st):
    pg = st["mixers"]
    row = lambda r: pg[:, r:r + 1, :]
    return dict(norm_ffn_g=st["norm_ffn"], norm_mix_g=st["norm_mix"],
                conv_f_w=jnp.concatenate([st["conv_f_gate"][:, :K_FFN], st["conv_f_val"][:, :K_FFN]], axis=2),
                conv_a_w=pg[:, PG_CONV_A:PG_CONV_A + K_SHORT], conv_c_w=pg[:, PG_CONV_C:PG_CONV_C + K_CONF],
                conv_c_b=row(PG_CB), ln_c_g=row(PG_LNG), ln_c_b=row(PG_LNB),
                out_norm_g=jnp.concatenate([row(PG_OG_A), st["og_b"], row(PG_OG_C)], axis=2))


def _layer_bwd_ffn(dx2, p, sv, l, st, after=None):
    tag = f"l{l}"
    g, st = {}, dict(st)
    dact = _mm(dx2, p["w_down"], "nt", 512, 1408, f"{tag}_mm_dact", after=after, b_outer=True)
    dwd = _mm(sv["act"], dx2, "tn", 1408, 1024, f"{tag}_mm_dwdown", out_dtype=bf16)
    g["w_down"] = dwd.reshape(N_DEV, dwd.shape[0] // N_DEV, dwd.shape[1])
    dug, duv, st["conv_f_gate"], st["conv_f_val"] = _ffn_act_bwd(sv["upre"], dact, p["conv_f_w"], (st["conv_f_gate"], st["conv_f_val"]),
                                                                 l, f"{tag}_ffn_act_bwd")
    dwu = _mm_tn_pieces([dug, duv], sv["x1"], p["norm_ffn_g"], 1408, f"{tag}_mm_dwup")
    g["w_up"] = dwu.reshape(N_DEV, dwu.shape[0] // N_DEV, dwu.shape[1])
    dx1, st["norm_ffn"] = _mm([dug, duv], p["w_up"], "nn", 512, 1024, f"{tag}_mm_dh2",
                              rms_bwd=(sv["x1"], p["norm_ffn_g"], dx2, st["norm_ffn"], l))
    return dx1, g, st


def _layer_bwd_mix(dx1, p, sv, bias, l, st, after=None):
    tag = f"l{l}"
    g, st = {}, dict(st)
    dy = _mm(dx1, p["w_out"], "nt", 512, 1024, f"{tag}_mm_dy", after=after)
    dwo = _mm(sv["y"], dx1, "tn", 512, 1024, f"{tag}_mm_dwout", out_dtype=bf16)
    g["w_out"] = dwo.reshape(N_DEV, dwo.shape[0] // N_DEV, dwo.shape[1])
    dz, dyb, st["mixers"], st["og_b"] = _mixer_bwd(sv["z"], sv["yb"], dy, sv["u"], p["conv_a_w"], p["conv_c_w"], p["conv_c_b"], p["ln_c_g"],
                                                   p["ln_c_b"], p["out_norm_g"], (st["mixers"], st["og_b"]), l, f"{tag}_mixer_bwd")
    dz, tabs = _attn_bwd(sv["z"], dyb, sv["yb"], sv["lt"], bias, dz, f"{tag}_attn_bwd")
    dwi = _mm(dz, sv["x"], "tn", 1408, 1024, f"{tag}_mm_dwin", out_dtype=bf16, norm_b=p["norm_mix_g"])
    g["w_in"] = dwi.reshape(N_DEV, dwi.shape[0] // N_DEV, dwi.shape[1])
    dx, st["norm_mix"] = _mm(dz, p["w_in"], "nn", 512, 1024, f"{tag}_mm_dh", rms_bwd=(sv["x"], p["norm_mix_g"], dx1, st["norm_mix"], l))
    return dx, g, tabs, st


SMALL_LAYER_PARAMS = ("norm_mix_g", "conv_a_w", "conv_c_w", "conv_c_b", "ln_c_g", "ln_c_b", "out_norm_g", "norm_ffn_g", "conv_f_w")


BIG = ("w_in", "w_out", "w_up", "w_down")
TRANSPOSED = ("w_in", "w_up")
ADAM_ROWS = {"w_in": 176, "w_out": 128, "w_up": 176, "w_down": 176}
SMALL_SHARDED = ("conv_a_w", "conv_c_w", "conv_f_w")
SMALL_REPLICATED = ("norm_mix_g", "conv_c_b", "ln_c_g", "ln_c_b", "out_norm_g", "norm_ffn_g", "rel_bias", "final_g")
WEIGHTS = ("norm_mix_g", "w_in", "conv_a_w", "conv_c_w", "conv_c_b", "ln_c_g", "ln_c_b", "out_norm_g", "w_out",
           "norm_ffn_g", "w_up", "conv_f_w", "w_down", "rel_bias", "final_g")


def _pack(arrays):
    flat = jnp.concatenate([a.reshape(-1) for a in arrays])
    rows = -(-flat.shape[0] // 1024) * 8
    return jnp.pad(flat, (0, rows * 128 - flat.shape[0])).reshape(rows, 128)


def _unpack(packed, shapes):
    flat, out, at = packed.reshape(-1), [], 0
    for shp in shapes:
        n = int(np.prod(shp))
        out.append(flat[at:at + n].reshape(shp))
        at += n
    return out


def _gather_shards(g, local_shape):
    nd = len(local_shape)
    return jnp.moveaxis(g, 0, nd - 1).reshape(local_shape[:-1] + (N_DEV * local_shape[-1],))


def kernel(x, norm_mix_g, w_in, conv_a_w, conv_c_w, conv_c_b, ln_c_g, ln_c_b, out_norm_g, w_out, norm_ffn_g, w_up, conv_f_w, w_down, rel_bias, final_g, loss_target, m_norm_mix_g, m_w_in, m_conv_a_w, m_conv_c_w, m_conv_c_b, m_ln_c_g, m_ln_c_b, m_out_norm_g, m_w_out, m_norm_ffn_g, m_w_up, m_conv_f_w, m_w_down, m_rel_bias, m_final_g, v_norm_mix_g, v_w_in, v_conv_a_w, v_conv_c_w, v_conv_c_b, v_ln_c_g, v_ln_c_b, v_out_norm_g, v_w_out, v_norm_ffn_g, v_w_up, v_conv_f_w, v_w_down, v_rel_bias, v_final_g):
    w = dict(norm_mix_g=norm_mix_g, w_in=w_in, conv_a_w=conv_a_w, conv_c_w=conv_c_w, conv_c_b=conv_c_b, ln_c_g=ln_c_g,
             ln_c_b=ln_c_b, out_norm_g=out_norm_g, w_out=w_out, norm_ffn_g=norm_ffn_g, w_up=w_up, conv_f_w=conv_f_w,
             w_down=w_down, rel_bias=rel_bias, final_g=final_g)
    mom = dict(norm_mix_g=m_norm_mix_g, w_in=m_w_in, conv_a_w=m_conv_a_w, conv_c_w=m_conv_c_w, conv_c_b=m_conv_c_b,
               ln_c_g=m_ln_c_g, ln_c_b=m_ln_c_b, out_norm_g=m_out_norm_g, w_out=m_w_out, norm_ffn_g=m_norm_ffn_g,
               w_up=m_w_up, conv_f_w=m_conv_f_w, w_down=m_w_down, rel_bias=m_rel_bias, final_g=m_final_g)
    var = dict(norm_mix_g=v_norm_mix_g, w_in=v_w_in, conv_a_w=v_conv_a_w, conv_c_w=v_conv_c_w, conv_c_b=v_conv_c_b,
               ln_c_g=v_ln_c_g, ln_c_b=v_ln_c_b, out_norm_g=v_out_norm_g, w_out=v_w_out, norm_ffn_g=v_norm_ffn_g,
               w_up=v_w_up, conv_f_w=v_conv_f_w, w_down=v_w_down, rel_bias=v_rel_bias, final_g=v_final_g)
    depth = w_in.shape[0]
    me = 4 * lax.axis_index("x") + 2 * lax.axis_index("y") + lax.axis_index("c")
    for k in TRANSPOSED:
        w[k], mom[k], var[k] = (jnp.transpose(a, (0, 2, 1)) for a in (w[k], mom[k], var[k]))

    me_arr = me.astype(jnp.int32).reshape(1)
    bufs = [_cast_blocks([w[k] for k in BIG], 0, me_arr, None, "l0_cast_blocks")]
    in_sems, in_bufs, token = _gather_start(bufs[0][:1], me_arr, "l0_gather_start_in")
    bufs += [_cast_blocks([w[k] for k in BIG], l, me_arr, token, f"l{l}_cast_blocks") for l in range(1, depth)]
    params = {}
    gathered = _all_gather([w[k] for k in SMALL_SHARDED], "gather_small", after=token)
    gs = gathered[0]
    for k, g8 in zip(SMALL_SHARDED, gathered):
        params[k] = _gather_shards(g8, w[k].shape)
    for k in ("norm_mix_g", "conv_c_b", "ln_c_g", "ln_c_b", "out_norm_g", "norm_ffn_g"):
        params[k] = w[k][:, None, :]
    params["rel_bias"] = rel_bias
    params["final_g"] = final_g[None, :]

    def full(k, buf, l):
        return buf.reshape(N_DEV * buf.shape[1], buf.shape[2])

    bias = _bias_tables(rel_bias, "bias_tables").reshape(len(DILATIONS), 2, 4, 2 * BLK, 2 * BLK)
    in_sems, in_bufs, rest_sems, rest_bufs, token = _gather_forward(in_bufs, in_sems, gs, "l0_gather_forward_in", fresh=bufs[0][1:])
    xl, after = x[0], token
    layer_params, saved = [], []
    for l in range(depth):
        _, in_bufs = _copies_wait(in_bufs, in_bufs, in_sems, after, f"l{l}_gather_wait_in", N_SECOND_LEG)
        p = {k: params[k][l] for k in SMALL_LAYER_PARAMS}
        p["w_in"] = full("w_in", in_bufs[0], l)
        token = in_bufs[0]
        nxt_in = bufs[l + 1][:1] if l + 1 < depth else []
        if l > 0:
            rest_sems, rest_bufs, in_sems, in_bufs, token = _gather_forward(rest_bufs, rest_sems, token, f"l{l}_gather_forward_rest", fresh=nxt_in)
        else:
            in_sems, in_bufs, token = _gather_start(nxt_in, token, "l1_gather_start_in")
        y, sv = _layer_fwd_mix(xl, p, bias, l, after=token)
        if l == 0:
            rest_sems, rest_bufs, _, _, _ = _gather_forward(rest_bufs, rest_sems, y, "l0_gather_forward_rest")
        _, rest_bufs = _copies_wait(rest_bufs, rest_bufs, rest_sems, y, f"l{l}_gather_wait_rest", N_SECOND_LEG)
        for k, buf in zip(BIG[1:], rest_bufs):
            p[k] = full(k, buf, l)
        token = rest_bufs[0]
        if l + 1 < depth:
            in_sems, in_bufs, rest_sems, rest_bufs, token = _gather_forward(in_bufs, in_sems, token, f"l{l + 1}_gather_forward_in",
                                                                             fresh=bufs[l + 1][1:])
        xl, sv2 = _layer_fwd_ffn(xl, y, p, l, after=token)
        layer_params.append(p)
        saved.append({**sv, **sv2})
        after = xl
    loss_tile, dx, d_final = _final_loss(xl, params["final_g"], loss_target[0], "final_loss")

    tabs, pending = [None] * depth, []
    after = None
    stacks = _small_stacks(depth, x.shape[-1], w_down.shape[1] * N_DEV)

    def exchange(l, part, names, g, after=None):
        sends = [g[k] for k in names]
        sems, sends, lands, token = _exchange_start(sends, [lax.empty(s.shape, s.dtype) for s in sends],
                                                    f"l{l}_exchange_start_{part}", after=after)
        pending.append((l, part, names, sems, sends, lands))
        return token

    for l in reversed(range(depth)):
        dx1, g_ffn, stacks = _layer_bwd_ffn(dx, layer_params[l], saved[l], l, stacks, after=after)
        after = exchange(l, "ffn", ("w_up", "w_down"), g_ffn)
        dx, g_mix, tabs[l], stacks = _layer_bwd_mix(dx1, layer_params[l], saved[l], bias, l, stacks, after=after)
        if l > 0:
            after = exchange(l, "mix", ("w_out", "w_in"), g_mix)
    grads_small = _small_grads(stacks)
    grads_small["rel_bias"] = _bias_grad(jnp.stack(tabs).reshape(depth, len(DILATIONS), 8, BLK, 2 * BLK), "bias_grad")[:, :8]
    grads_small["final_g"] = d_final

    out_g, out_d, out_m, out_v = {}, {}, {}, {}
    grads = grads_small
    small = SMALL_SHARDED + SMALL_REPLICATED
    full_shapes = [params[k].shape if k in SMALL_SHARDED else w[k].shape for k in small]
    packed = _pack([grads[k].reshape(shp) for k, shp in zip(small, full_shapes)] + [loss_tile[0, :1]])
    gathered_small = _all_gather([packed], "gather_small_grads", after=after)[0]
    token = exchange(0, "mix", ("w_out", "w_in"), g_mix, after=gathered_small)
    gsum = _sum_parts(gathered_small, token, "sum_small_grads")
    *summed, loss = _unpack(gsum, full_shapes + [(1,)])
    loss = loss[0]
    gfull = dict(zip(small, summed))
    for k in SMALL_SHARDED:
        shp = w[k].shape
        gfull[k] = lax.dynamic_index_in_dim(gfull[k].reshape(shp[:-1] + (N_DEV, shp[-1])), me, axis=len(shp) - 1, keepdims=False)
    two_d = lambda a: a.reshape(1, -1) if a.ndim == 1 else a
    res = _adamw_small(*[[two_d(src[k]) for k in small] for src in (gfull, w, mom, var)], "adamw_small")
    for name_map, outs in zip((out_d, out_m, out_v), res):
        name_map.update({k: o.reshape(w[k].shape) for k, o in zip(small, outs)})
    out_g.update(gfull)

    results = {k: [lax.empty(w[k].shape, f32) for _ in range(4)] for k in BIG}
    after = res[0][0]
    for l, part, names, sems, sends, lands in pending:
        sends, lands = _copies_wait(sends, lands, sems, after, f"l{l}_exchange_wait_{part}")
        for k, send, land in zip(names, sends, lands):
            results[k] = _adamw_layer(land, send, w[k], mom[k], var[k], results[k], l, ADAM_ROWS[k], me_arr, f"l{l}_adamw_{k}")
            after = results[k][0]
    for k in BIG:
        big = [jnp.transpose(a, (0, 2, 1)) for a in results[k]] if k in TRANSPOSED else results[k]
        out_g[k], out_d[k], out_m[k], out_v[k] = big

    return (loss, dx[None], *[out_g[k] for k in WEIGHTS], *[out_d[k] for k in WEIGHTS], *[out_m[k] for k in WEIGHTS],
            *[out_v[k] for k in WEIGHTS])
```

```python
import math

import numpy as np
import jax
import jax.numpy as jnp
from jax import lax
from jax.experimental import pallas as pl
from jax.experimental.pallas import tpu as pltpu

f32, bf16 = jnp.float32, jnp.bfloat16
SDS = jax.ShapeDtypeStruct
MESH = pl.DeviceIdType.MESH

N_DEV = 8
EPS = 1e-6
NEG = -1e30
D_HEAD = 64
BLK = 128
N_BUCKETS = 32
DILATIONS = (1, 4, 16)
N_KEYS = 128
K_SHORT, K_CONF, K_FFN = 3, 31, 3
ROW_TILE = 512
MIXER_ROWS = 512
HALO = 32
HALO_FFN = 8
FFN_FWD_ROWS, FFN_BWD_ROWS = 2048, 2048
FFN_CHUNK = 128
VMEM_LIMIT = 56 << 20
ADAM_LR, ADAM_B1, ADAM_B2, ADAM_EPS, ADAM_WD, ADAM_STEP = 0.001, 0.9, 0.999, 1e-08, 0.01, 10


def _pcall(body, **kw):
    return pl.pallas_call(body, **kw)


def _cp(sem=None):
    kw = dict(vmem_limit_bytes=VMEM_LIMIT)
    if sem is not None:
        kw["dimension_semantics"] = sem
    return pltpu.CompilerParams(**kw)


def _t5_buckets():
    rel = np.arange(BLK)[:, None] - np.arange(2 * BLK)[None, :] + BLK
    out = []
    for d in DILATIONS:
        dist = np.maximum(rel, 0) * d
        max_exact = N_BUCKETS // 2
        d_f = np.maximum(dist, 1).astype(np.float32)
        large = max_exact + (np.log(d_f / np.float32(max_exact)) / np.float32(math.log(2048 / max_exact))
                             * np.float32(N_BUCKETS - max_exact)).astype(np.int32)
        large = np.minimum(large, N_BUCKETS - 1)
        out.append(np.where(dist < max_exact, dist, large).astype(np.int32))
    return np.stack(out)


_BUCKETS = _t5_buckets()


def _mm(a, b, mode, tr, tc, name, res=None, out_dtype=f32, after=None, b_outer=False, norm_a=None, norm_b=None, rms_bwd=None):
    if rms_bwd is not None:
        return _mm_rms_bwd(a, b, tr, name, *rms_bwd)
    ij = (lambda f: (lambda j, i: f(i, j))) if b_outer else (lambda f: f)
    if mode == "nn":
        (m, k), n = a.shape, b.shape[1]
        a_spec = pl.BlockSpec((tr, k), ij(lambda i, j: (i, 0)))
        b_spec = pl.BlockSpec((k, tc), ij(lambda i, j: (0, j)))
        dims = (((1,), (0,)), ((), ()))
    elif mode == "nt":
        (m, k), n = a.shape, b.shape[0]
        a_spec = pl.BlockSpec((tr, k), ij(lambda i, j: (i, 0)))
        b_spec = pl.BlockSpec((tc, k), ij(lambda i, j: (j, 0)))
        dims = (((1,), (1,)), ((), ()))
    else:
        (k, m), n = a.shape, b.shape[1]
        a_spec = pl.BlockSpec((k, tr), ij(lambda i, j: (0, i)))
        b_spec = pl.BlockSpec((k, tc), ij(lambda i, j: (0, j)))
        dims = (((0,), (0,)), ((), ()))
    assert m % tr == 0 and n % tc == 0, (name, m, n, tr, tc)
    o_spec = pl.BlockSpec((tr, tc), ij(lambda i, j: (i, j)))
    grid = (n // tc, m // tr) if b_outer else (m // tr, n // tc)

    gain = norm_a if norm_a is not None else norm_b
    assert norm_b is None or (mode == "tn" and tc == n and not b_outer)
    assert norm_a is None or (mode != "tn" and not b_outer)

    def body(a_ref, b_ref, *rest):
        o_ref = rest[-2] if gain is not None else rest[-1]
        if norm_a is not None:
            ha_sc = rest[-1]

            @pl.when(pl.program_id(1) == 0)
            def _():
                av = a_ref[...]
                ha_sc[...] = (av * _rstd(av) * rest[0][...]).astype(bf16)

            av = ha_sc[...]
        else:
            av = a_ref[...].astype(bf16)
        if norm_b is not None:
            hb_sc = rest[-1]

            @pl.when(pl.program_id(0) == 0)
            def _():
                bv = b_ref[...]
                hb_sc[...] = (bv * _rstd(bv) * rest[0][...]).astype(bf16)

            bv = hb_sc[...]
        else:
            bv = b_ref[...].astype(bf16)
        acc = lax.dot_general(av, bv, dims, preferred_element_type=f32)
        if res is not None:
            acc = acc + rest[1 if gain is not None else 0][...]
        o_ref[...] = acc.astype(out_dtype)

    vec = pl.BlockSpec((1, k if norm_a is not None else n), lambda i, j: (0, 0))
    ins = [a, b] + ([gain] if gain is not None else []) + ([res] if res is not None else []) + ([after] if after is not None else [])
    specs = ([a_spec, b_spec] + ([vec] if gain is not None else []) + ([o_spec] if res is not None else [])
             + ([pl.BlockSpec(memory_space=pl.ANY)] if after is not None else []))
    scratch = [pltpu.VMEM((k, n), bf16)] if norm_b is not None else [pltpu.VMEM((tr, k), bf16)] if norm_a is not None else []
    sem = ("arbitrary", "arbitrary") if gain is not None else ("parallel", "parallel")
    return _pcall(body, name=name, out_shape=SDS((m, n), out_dtype), grid=grid, in_specs=specs, out_specs=o_spec,
                  scratch_shapes=scratch, compiler_params=_cp(sem))(*ins)


def _mm_tn_pieces(pieces, b, gain, tr, name):
    k, n = b.shape
    widths = [pc.shape[1] for pc in pieces]
    assert all(wd % tr == 0 for wd in widths)
    starts = [sum(widths[:p]) // tr for p in range(len(pieces))]
    counts = [wd // tr for wd in widths]
    npc = len(pieces)

    def body(*refs):
        a_refs, b_ref, g_ref, o_ref, hb_sc = refs[:npc], refs[npc], refs[npc + 1], refs[npc + 2], refs[npc + 3]
        i = pl.program_id(0)

        @pl.when(i == 0)
        def _():
            bv = b_ref[...]
            hb_sc[...] = (bv * _rstd(bv) * g_ref[...]).astype(bf16)

        for p in range(npc):
            @pl.when(jnp.logical_and(i >= starts[p], i < starts[p] + counts[p]))
            def _(p=p):
                o_ref[...] = lax.dot_general(a_refs[p][...].astype(bf16), hb_sc[...], (((0,), (0,)), ((), ())),
                                             preferred_element_type=f32).astype(bf16)

    a_specs = [pl.BlockSpec((k, tr), lambda i, p=p: (0, jnp.clip(i - starts[p], 0, counts[p] - 1))) for p in range(npc)]
    return _pcall(body, name=name, out_shape=SDS((sum(widths), n), bf16), grid=(sum(counts),),
                  in_specs=a_specs + [pl.BlockSpec((k, n), lambda i: (0, 0)), pl.BlockSpec((1, n), lambda i: (0, 0))],
                  out_specs=pl.BlockSpec((tr, n), lambda i: (i, 0)), scratch_shapes=[pltpu.VMEM((k, n), bf16)],
                  compiler_params=_cp(("arbitrary",)))(*pieces, b, gain)


def _mm_rms_bwd(a, b, tr, name, x, g, dres, stack, l):
    pieces = list(a) if isinstance(a, (list, tuple)) else [a]
    m, n, npc = pieces[0].shape[0], b.shape[1], len(pieces)
    k = sum(pc.shape[1] for pc in pieces)

    def body(*refs):
        a_refs, (b_ref, x_ref, g_ref, dres_ref, stack_ref, dx_ref, dg_ref) = refs[:npc], refs[npc:]
        dh, at = 0.0, 0
        for a_ref in a_refs:
            kp = a_ref.shape[1]
            dh = dh + jnp.dot(a_ref[...].astype(bf16), b_ref[pl.ds(at, kp), :].astype(bf16), preferred_element_type=f32)
            at += kp
        xv = x_ref[...]
        r = _rstd(xv)
        dx_ref[...] = dres_ref[...] + _rms_dx(xv, r, g_ref[...], dh)

        @pl.when(pl.program_id(0) == 0)
        def _():
            dg_ref[...] = jnp.zeros_like(dg_ref)

        dg_ref[...] += _colsum(dh * xv * r)

    row = pl.BlockSpec((tr, n), lambda i: (i, 0))
    vec = pl.BlockSpec((1, n), lambda i: (0, 0))
    a_specs = [pl.BlockSpec((tr, pc.shape[1]), lambda i: (i, 0)) for pc in pieces]
    return _pcall(body, name=name, out_shape=(SDS((m, n), f32), SDS(stack.shape, f32)), grid=(m // tr,),
                  in_specs=a_specs + [pl.BlockSpec((k, n), lambda i: (0, 0), pipeline_mode=pl.Buffered(1)),
                                      row, vec, row, pl.BlockSpec(memory_space=pl.ANY)],
                  out_specs=(row, pl.BlockSpec((None, 1, n), lambda i: (l, 0, 0))), input_output_aliases={npc + 4: 1},
                  compiler_params=_cp(("arbitrary",)))(*pieces, b, x, g, dres, stack)


def _rstd(x):
    return lax.rsqrt(jnp.mean(x * x, axis=-1, keepdims=True) + EPS)


def _rms_dx(x, r, g, dy):
    dxh = dy * g
    return r * (dxh - x * (r * r) * jnp.mean(dxh * x, axis=-1, keepdims=True))


def _sigmoid(x):
    return 0.5 * jnp.tanh(0.5 * x) + 0.5


def _colsum(x):
    return jnp.sum(x, axis=0, keepdims=True)


def _final_loss(x, g, target, name):
    s, d = x.shape

    def body(x_ref, g_ref, t_ref, loss_ref, dx_ref, dg_ref):
        xv, gv = x_ref[...], g_ref[...]
        r = _rstd(xv)
        diff = xv * r * gv - t_ref[...]
        dy = diff * (1.0 / d)

        @pl.when(pl.program_id(0) == 0)
        def _():
            dg_ref[...] = jnp.zeros_like(dg_ref)
            loss_ref[...] = jnp.zeros_like(loss_ref)

        part = jnp.sum(jnp.sum(diff * diff, axis=1, keepdims=True), axis=0, keepdims=True) * (0.5 / d)
        loss_ref[...] += part
        dx_ref[...] = _rms_dx(xv, r, gv, dy)
        dg_ref[...] += _colsum(dy * xv * r)

    row = pl.BlockSpec((ROW_TILE, d), lambda i: (i, 0))
    vec = pl.BlockSpec((1, d), lambda i: (0, 0))
    one = pl.BlockSpec((8, 128), lambda i: (0, 0))
    return _pcall(body, name=name, out_shape=(SDS((8, 128), f32), SDS((s, d), f32), SDS((1, d), f32)),
                  grid=(s // ROW_TILE,), in_specs=[row, vec, row], out_specs=(one, row, vec),
                  compiler_params=_cp(("arbitrary",)))(x, g, target)


Z_AH, Z_AB, Z_AC, Z_CV, Z_CG = 0, 1, 2, 9, 10
Z_Q128, Z_K128, Z_V128 = 6, 10, 14


def _cur(t, w, col):
    return pl.BlockSpec((t, w), lambda i: (i, col))


def _prev(hb, t, w, col):
    return pl.BlockSpec((hb, w), lambda i: (jnp.maximum(i * (t // hb) - 1, 0), col))


def _next(hb, t, w, col, rows):
    return pl.BlockSpec((hb, w), lambda i: (jnp.minimum((i + 1) * (t // hb), rows // hb - 1), col))


def _layernorm_parts(u, g, b):
    mu = jnp.mean(u, axis=-1, keepdims=True)
    var = jnp.mean(jnp.square(u - mu), axis=-1, keepdims=True)
    rs = lax.rsqrt(var + EPS)
    xhat = (u - mu) * rs
    return xhat, rs, xhat * g + b


def _fill_shifted(rot):
    n = rot.shape[1]
    for r in range(1, 8):
        rot[r, pl.ds(0, n - 8), :] = rot[0, pl.ds(r, n - 8), :]


def _window(rot, offset, rows):
    return rot[offset % 8, pl.ds(offset - offset % 8, rows), :]


def _mixer_fwd(z, yb, conv_a, conv_c, cb, lng, lnb, og, name):
    s = z.shape[0]
    t, hb, w = MIXER_ROWS, HALO, 256

    def body(ah, ab, ac, cv, cg, ah_p, ac_p, cv_p, cg_p, yb_ref, wa, wc, cb_ref, lng_ref, lnb_ref, og_ref, y_ref, u_ref, p_sc, u_sc):
        first = pl.program_id(0) == 0

        def prev(ref):
            return jnp.where(first, 0.0, ref[...])

        p_sc[pl.ds(0, hb), :] = prev(ah_p) * prev(ac_p)
        p_sc[pl.ds(hb, t), :] = ah[...] * ac[...]
        conv = jnp.zeros((t, w), f32)
        for k in range(K_SHORT):
            conv = conv + wa[pl.ds(k, 1), :] * p_sc[pl.ds(hb - (K_SHORT - 1) + k, t), :]
        y_a = ab[...] * conv
        u_sc[0, pl.ds(0, hb), :] = prev(cv_p) * _sigmoid(prev(cg_p))
        u_sc[0, pl.ds(hb, t), :] = cv[...] * _sigmoid(cg[...])
        _fill_shifted(u_sc)
        u = jnp.zeros((t, w), f32) + cb_ref[...]
        for k in range(K_CONF):
            u = u + wc[pl.ds(k, 1), :] * _window(u_sc, hb - (K_CONF - 1) + k, t)
        u_ref[...] = u
        _, _, lnv = _layernorm_parts(u, lng_ref[...], lnb_ref[...])
        y_c = lnv * _sigmoid(lnv)
        ybv = yb_ref[...]
        y_ref[:, 0:256] = (y_a * _rstd(y_a) * og_ref[:, 0:256]).astype(bf16)
        y_ref[:, 256:768] = (ybv * _rstd(ybv) * og_ref[:, 256:768]).astype(bf16)
        y_ref[:, 768:1024] = (y_c * _rstd(y_c) * og_ref[:, 768:1024]).astype(bf16)

    full = lambda shape: pl.BlockSpec(shape, lambda i: tuple(0 for _ in shape))
    in_specs = [_cur(t, w, c) for c in (Z_AH, Z_AB, Z_AC, Z_CV, Z_CG)] + [_prev(hb, t, w, c) for c in (Z_AH, Z_AC, Z_CV, Z_CG)]
    in_specs += [_cur(t, 512, 0), full((K_SHORT, w)), full((K_CONF, w)), full((1, w)), full((1, w)), full((1, w)), full((1, 1024))]
    return _pcall(body, name=name, out_shape=(SDS((s, 1024), bf16), SDS((s, w), f32)), grid=(s // t,), in_specs=in_specs,
                  out_specs=(_cur(t, 1024, 0), _cur(t, w, 0)),
                  scratch_shapes=[pltpu.VMEM((hb + t, w), f32), pltpu.VMEM((8, hb + t, w), f32)],
                  compiler_params=_cp(("parallel",)))(z, z, z, z, z, z, z, z, z, yb, conv_a, conv_c, cb, lng, lnb, og)


PG_CONV_A, PG_CONV_C, PG_CB, PG_LNG, PG_LNB, PG_OG_A, PG_OG_C, PG_ROWS = 0, 8, 40, 41, 42, 43, 44, 48


def _mixer_bwd(z, yb, dy, u_saved, conv_a, conv_c, cb, lng, lnb, og, stacks, l, name):
    s = z.shape[0]
    t, hb, w = MIXER_ROWS, HALO, 256
    e = t + hb

    def body(ah, ab, ac, cv, cg, ah_p, ac_p, cv_p, cg_p, ah_n, ab_n, ac_n, cv_n, cg_n,
             dya, dyb1, dyb2, dyc, dya_n, dyc_n, yb_ref, wa, wc, cb_ref, lng_ref, lnb_ref, og_ref, pg_stack, dgb_stack, u_c, u_n,
             dz_ref, dyb_ref, pg_ref, dgb_ref, p_sc, u_sc, dc_sc, du_sc):
        i = pl.program_id(0)
        first, last = i == 0, i == pl.num_programs(0) - 1

        def prev(ref):
            return jnp.where(first, 0.0, ref[...])

        def nxt(ref):
            return jnp.where(last, 0.0, ref[...])

        def ext(cur_ref, next_ref):
            return jnp.concatenate([cur_ref[...], nxt(next_ref)], axis=0)

        @pl.when(first)
        def _():
            pg_ref[...] = jnp.zeros_like(pg_ref)
            dgb_ref[...] = jnp.zeros_like(dgb_ref)

        g_a = og_ref[:, 0:256]
        p_sc[pl.ds(0, hb), :] = prev(ah_p) * prev(ac_p)
        p_sc[pl.ds(hb, t), :] = ah[...] * ac[...]
        p_sc[pl.ds(hb + t, hb), :] = nxt(ah_n) * nxt(ac_n)
        conv = jnp.zeros((e, w), f32)
        for k in range(K_SHORT):
            conv = conv + wa[pl.ds(k, 1), :] * p_sc[pl.ds(hb - (K_SHORT - 1) + k, e), :]
        ab_e, dya_e = ext(ab, ab_n), ext(dya, dya_n)
        y_a = ab_e * conv
        r_a = _rstd(y_a)
        dy_a = _rms_dx(y_a, r_a, g_a, dya_e)
        dc_sc[...] = dy_a * ab_e
        dz_ref[:, 256:512] = (dy_a[:t] * conv[:t]).astype(bf16)
        dp = jnp.zeros((t, w), f32)
        for k in range(K_SHORT):
            dp = dp + wa[pl.ds(k, 1), :] * dc_sc[pl.ds(K_SHORT - 1 - k, t), :]
            pg_ref[pl.ds(PG_CONV_A + k, 1), :] += _colsum(dc_sc[pl.ds(0, t), :] * p_sc[pl.ds(hb - (K_SHORT - 1) + k, t), :])
        dz_ref[:, 0:256] = (dp * ac[...]).astype(bf16)
        dz_ref[:, 512:768] = (dp * ah[...]).astype(bf16)
        pg_ref[pl.ds(PG_OG_A, 1), :] += _colsum(dya[...] * y_a[:t] * r_a[:t])

        g_c = og_ref[:, 768:1024]
        u_sc[0, pl.ds(0, hb), :] = prev(cv_p) * _sigmoid(prev(cg_p))
        u_sc[0, pl.ds(hb, t), :] = cv[...] * _sigmoid(cg[...])
        u_sc[0, pl.ds(hb + t, hb), :] = nxt(cv_n) * _sigmoid(nxt(cg_n))
        _fill_shifted(u_sc)
        u = ext(u_c, u_n)
        xhat, rs, lnv = _layernorm_parts(u, lng_ref[...], lnb_ref[...])
        sg = _sigmoid(lnv)
        y_c = lnv * sg
        r_c = _rstd(y_c)
        dyc_e = ext(dyc, dyc_n)
        dlnv = _rms_dx(y_c, r_c, g_c, dyc_e) * (sg * (1.0 + lnv * (1.0 - sg)))
        dxh = dlnv * lng_ref[...]
        du = rs * (dxh - jnp.mean(dxh, axis=-1, keepdims=True) - xhat * jnp.mean(dxh * xhat, axis=-1, keepdims=True))
        du_sc[0] = du
        _fill_shifted(du_sc)
        duin = jnp.zeros((t, w), f32)
        for k in range(K_CONF):
            duin = duin + wc[pl.ds(k, 1), :] * _window(du_sc, K_CONF - 1 - k, t)
            pg_ref[pl.ds(PG_CONV_C + k, 1), :] += _colsum(du[:t] * _window(u_sc, hb - (K_CONF - 1) + k, t))
        sgg = _sigmoid(cg[...])
        dz_ref[:, 2304:2560] = (duin * sgg).astype(bf16)
        dz_ref[:, 2560:2816] = (duin * cv[...] * sgg * (1.0 - sgg)).astype(bf16)
        pg_ref[pl.ds(PG_CB, 1), :] += _colsum(du[:t])
        pg_ref[pl.ds(PG_LNG, 1), :] += _colsum(dlnv[:t] * xhat[:t])
        pg_ref[pl.ds(PG_LNB, 1), :] += _colsum(dlnv[:t])
        pg_ref[pl.ds(PG_OG_C, 1), :] += _colsum(dyc[...] * y_c[:t] * r_c[:t])

        g_b = og_ref[:, 256:768]
        ybv = yb_ref[...]
        r_b = _rstd(ybv)
        dyb = jnp.concatenate([dyb1[...], dyb2[...]], axis=1)
        dyb_ref[...] = _rms_dx(ybv, r_b, g_b, dyb)
        dgb_ref[...] += _colsum(dyb * ybv * r_b)

    full = lambda shape: pl.BlockSpec(shape, lambda i: tuple(0 for _ in shape))
    zc = (Z_AH, Z_AB, Z_AC, Z_CV, Z_CG)
    in_specs = [_cur(t, w, c) for c in zc] + [_prev(hb, t, w, c) for c in (Z_AH, Z_AC, Z_CV, Z_CG)]
    in_specs += [_next(hb, t, w, c, s) for c in zc]
    in_specs += [_cur(t, w, c) for c in (0, 1, 2, 3)] + [_next(hb, t, w, 0, s), _next(hb, t, w, 3, s)]
    in_specs += [_cur(t, 512, 0), full((K_SHORT, w)), full((K_CONF, w)), full((1, w)), full((1, w)), full((1, w)), full((1, 1024))]
    in_specs += [pl.BlockSpec(memory_space=pl.ANY)] * 2 + [_cur(t, w, 0), _next(hb, t, w, 0, s)]
    out_shape = (SDS((s, z.shape[1]), bf16), SDS((s, 512), f32), SDS(stacks[0].shape, f32), SDS(stacks[1].shape, f32))
    out_specs = (_cur(t, z.shape[1], 0), _cur(t, 512, 0), pl.BlockSpec((None, PG_ROWS, w), lambda i: (l, 0, 0)),
                 pl.BlockSpec((None, 1, 512), lambda i: (l, 0, 0)))
    scratch = [pltpu.VMEM((hb + t + hb, w), f32), pltpu.VMEM((8, hb + t + hb, w), f32), pltpu.VMEM((e, w), f32), pltpu.VMEM((8, e, w), f32)]
    args = [z] * 14 + [dy] * 6 + [yb, conv_a, conv_c, cb, lng, lnb, og] + list(stacks) + [u_saved, u_saved]
    return _pcall(body, name=name, out_shape=out_shape, grid=(s // t,), in_specs=in_specs, out_specs=out_specs,
                  input_output_aliases={27: 2, 28: 3}, scratch_shapes=scratch, compiler_params=_cp(("arbitrary",)))(*args)


def _ffn_act_fwd(upre, conv_f, name):
    s, f2 = upre.shape
    t, hb, w = FFN_FWD_ROWS, HALO_FFN, 256
    nj = f2 // 2 // w

    def body(g_c, v_c, g_p, v_p, wg, wv, o_ref):
        first = pl.program_id(1) == 0
        wgs = [wg[pl.ds(k, 1), :] for k in range(K_FFN)]
        wvs = [wv[pl.ds(k, 1), :] for k in range(K_FFN)]
        rows = FFN_CHUNK
        first_tap = hb - (K_FFN - 1)

        def emit(gb, vb, r0):
            ug = sum(wgs[k] * gb[first_tap + k:first_tap + k + rows] for k in range(K_FFN))
            uv = sum(wvs[k] * vb[first_tap + k:first_tap + k + rows] for k in range(K_FFN))
            o_ref[pl.ds(r0, rows), :] = (ug * _sigmoid(ug) * uv).astype(bf16)

        emit(jnp.concatenate([jnp.where(first, 0.0, g_p[...]), g_c[pl.ds(0, rows), :]], axis=0),
             jnp.concatenate([jnp.where(first, 0.0, v_p[...]), v_c[pl.ds(0, rows), :]], axis=0), 0)

        def chunk(c, carry):
            r0 = pl.multiple_of(c * rows, rows)
            back = pl.ds(pl.multiple_of(r0 - hb, hb), rows + hb)
            emit(g_c[back, :], v_c[back, :], r0)
            return carry

        lax.fori_loop(1, t // rows, chunk, 0, unroll=t // rows - 1)

    cur = lambda off: pl.BlockSpec((t, w), lambda j, i: (i, off + j))
    prv = lambda off: pl.BlockSpec((hb, w), lambda j, i: (jnp.maximum(i * (t // hb) - 1, 0), off + j))
    wsp = lambda off: pl.BlockSpec((K_FFN, w), lambda j, i: (0, off + j))
    return _pcall(body, name=name, out_shape=SDS((s, f2 // 2), bf16), grid=(nj, s // t),
                  in_specs=[cur(0), cur(nj), prv(0), prv(nj), wsp(0), wsp(nj)],
                  out_specs=pl.BlockSpec((t, w), lambda j, i: (i, j)),
                  compiler_params=_cp(("parallel", "parallel")))(upre, upre, upre, upre, conv_f, conv_f)


def _ffn_act_bwd(upre, dact, conv_f, stacks, l, name):
    s, f2 = upre.shape
    t, hb, w = FFN_BWD_ROWS, HALO_FFN, 256
    nj = f2 // 2 // w

    def body(g_c, v_c, g_p, v_p, g_n, v_n, da_c, da_n, wg, wv, sg_ref, sv_ref, dg_ref, dv_ref, dwg_ref, dwv_ref):
        i = pl.program_id(1)
        first, last = i == 0, i == pl.num_programs(1) - 1

        @pl.when(first)
        def _():
            dwg_ref[...] = jnp.zeros_like(dwg_ref)
            dwv_ref[...] = jnp.zeros_like(dwv_ref)

        wgs = [wg[pl.ds(k, 1), :] for k in range(K_FFN)]
        wvs = [wv[pl.ds(k, 1), :] for k in range(K_FFN)]
        rows = FFN_CHUNK
        n = rows + hb
        first_tap = hb - (K_FFN - 1)
        nchunks = t // rows

        def groups_of_8(x):
            return sum(x[8 * q:8 * q + 8] for q in range(rows // 8))

        def emit(gb, vb, da, r0, carry):
            g_taps = [gb[first_tap + k:first_tap + k + n] for k in range(K_FFN)]
            v_taps = [vb[first_tap + k:first_tap + k + n] for k in range(K_FFN)]
            ug = sum(wgs[k] * g_taps[k] for k in range(K_FFN))
            uv = sum(wvs[k] * v_taps[k] for k in range(K_FFN))
            sg = _sigmoid(ug)
            dv = da * (ug * sg)
            dg = da * uv * (sg * (1.0 + ug * (1.0 - sg)))
            new = []
            for taps, dval, ws, dref in ((g_taps, dg, wgs, dg_ref), (v_taps, dv, wvs, dv_ref)):
                acc = sum(ws[k] * dval[K_FFN - 1 - k:K_FFN - 1 - k + rows] for k in range(K_FFN))
                dref[pl.ds(r0, rows), :] = acc.astype(bf16)
                new += [groups_of_8(dval[:rows] * taps[k][:rows]) for k in range(K_FFN)]
            return tuple(a + b for a, b in zip(carry, new))

        def chunk(c, carry):
            r0 = pl.multiple_of(c * rows, rows)
            around = pl.ds(pl.multiple_of(r0 - hb, hb), n + hb)
            return emit(g_c[around, :], v_c[around, :], da_rows(r0), r0, carry)

        def da_rows(r0):
            return da_c[pl.ds(r0, n + 8), :].astype(f32)[:n]

        def edge(prev_ref, cur_ref, lo):
            return jnp.concatenate([jnp.where(first, 0.0, prev_ref[...]), cur_ref[pl.ds(0, lo), :]], axis=0)

        def tail(cur_ref, next_ref, lo):
            return jnp.concatenate([cur_ref[pl.ds(t - lo, lo), :], jnp.where(last, 0.0, next_ref[...])], axis=0)

        zero = jnp.zeros((8, w), f32)
        sums = emit(edge(g_p, g_c, n), edge(v_p, v_c, n), da_rows(0), 0, (zero,) * (2 * K_FFN))
        sums = lax.fori_loop(1, nchunks - 1, chunk, sums, unroll=nchunks - 2)
        da_tail = jnp.concatenate([da_c[pl.ds(t - rows, rows), :].astype(f32),
                                   jnp.where(last, 0.0, da_n[...].astype(f32)[:hb])], axis=0)
        sums = emit(tail(g_c, g_n, rows + hb), tail(v_c, v_n, rows + hb), da_tail, t - rows, sums)
        for k in range(K_FFN):
            dwg_ref[pl.ds(k, 1), :] += _colsum(sums[k])
            dwv_ref[pl.ds(k, 1), :] += _colsum(sums[K_FFN + k])

    cur = lambda off: pl.BlockSpec((t, w), lambda j, i: (i, off + j))
    prv = lambda off: pl.BlockSpec((hb, w), lambda j, i: (jnp.maximum(i * (t // hb) - 1, 0), off + j))
    nxt = lambda off: pl.BlockSpec((hb, w), lambda j, i: (jnp.minimum((i + 1) * (t // hb), s // hb - 1), off + j))
    wsp = lambda off: pl.BlockSpec((K_FFN, w), lambda j, i: (0, off + j))
    da_nxt = pl.BlockSpec((16, w), lambda j, i: (jnp.minimum((i + 1) * (t // 16), s // 16 - 1), j))
    half = SDS((s, f2 // 2), bf16)
    dws = SDS(stacks[0].shape, f32)
    o_cur = pl.BlockSpec((t, w), lambda j, i: (i, j))
    o_dw = pl.BlockSpec((None, 8, w), lambda j, i: (l, 0, j))
    anywhere = pl.BlockSpec(memory_space=pl.ANY)
    return _pcall(body, name=name, out_shape=(half, half, dws, dws), grid=(nj, s // t),
                  in_specs=[cur(0), cur(nj), prv(0), prv(nj), nxt(0), nxt(nj), cur(0), da_nxt, wsp(0), wsp(nj), anywhere, anywhere],
                  out_specs=(o_cur, o_cur, o_dw, o_dw), input_output_aliases={10: 2, 11: 3},
                  compiler_params=_cp(("parallel", "arbitrary")))(upre, upre, upre, upre, upre, upre, dact, dact, conv_f, conv_f, *stacks)


def _bias_tables(rel_bias, name):
    nb = len(DILATIONS)

    def body(rb_ref, idx_ref, o_ref):
        h = pl.program_id(1)
        iv = idx_ref[0]
        qi = lax.broadcasted_iota(jnp.int32, (BLK, 2 * BLK), 0)
        kj = lax.broadcasted_iota(jnp.int32, (BLK, 2 * BLK), 1)
        band = jnp.logical_and(kj >= qi, kj <= qi + N_KEYS)
        acc = jnp.zeros((BLK, 2 * BLK), f32)
        for b in range(N_BUCKETS):
            acc = jnp.where(iv == b, rb_ref[b, h], acc)
        o_ref[0, 0, 0] = jnp.where(band, acc, NEG)
        o_ref[0, 1, 0] = jnp.where(jnp.logical_and(band, kj >= BLK), acc, NEG)

    return _pcall(body, name=name, out_shape=SDS((nb, 2, 8, BLK, 2 * BLK), f32), grid=(nb, 8),
                  in_specs=[pl.BlockSpec(memory_space=pltpu.SMEM), pl.BlockSpec((1, BLK, 2 * BLK), lambda br, h: (br, 0, 0))],
                  out_specs=pl.BlockSpec((1, 2, 1, BLK, 2 * BLK), lambda br, h: (br, 0, h, 0, 0)),
                  compiler_params=_cp(("parallel", "parallel")))(rel_bias, jnp.asarray(_BUCKETS))


def _bias_grad(ds_tabs, name):
    nl, nb = ds_tabs.shape[0], ds_tabs.shape[1]

    def body(t_ref, idx_ref, o_ref):
        br, h = pl.program_id(0), pl.program_id(1)

        @pl.when(jnp.logical_and(br == 0, h == 0))
        def _():
            o_ref[...] = jnp.zeros_like(o_ref)

        tab = t_ref[0, 0, 0]
        for l in range(1, nl):
            tab = tab + t_ref[l, 0, 0]
        iv = idx_ref[0]
        row = lax.broadcasted_iota(jnp.int32, (N_BUCKETS, 128), 0)
        col = lax.broadcasted_iota(jnp.int32, (N_BUCKETS, 128), 1)
        acc = jnp.zeros((N_BUCKETS, 128), f32)
        for b in range(N_BUCKETS):
            sel = jnp.where(iv == b, tab, 0.0)
            tot = jnp.sum(jnp.sum(sel, axis=0, keepdims=True), axis=1, keepdims=True)
            acc = acc + jnp.where(jnp.logical_and(row == b, col == h), tot, 0.0)
        o_ref[...] += acc

    return _pcall(body, name=name, out_shape=SDS((N_BUCKETS, 128), f32), grid=(nb, 8),
                  in_specs=[pl.BlockSpec((nl, 1, 1, BLK, 2 * BLK), lambda br, h: (0, br, h, 0, 0)),
                            pl.BlockSpec((1, BLK, 2 * BLK), lambda br, h: (br, 0, 0))],
                  out_specs=pl.BlockSpec((N_BUCKETS, 128), lambda br, h: (0, 0)),
                  compiler_params=_cp(("arbitrary", "arbitrary")))(ds_tabs, jnp.asarray(_BUCKETS))


def _stack_heads(x, head0):
    zero = jnp.zeros_like(x)
    return jnp.concatenate([jnp.where(head0, x, zero), jnp.where(head0, zero, x)], axis=0)


def _block_rows(blk, d, nblk):
    per_res = nblk // d
    res, n = blk // per_res, blk % per_res
    cur = pl.ds(res + d * BLK * n, BLK, stride=d) if d > 1 else pl.ds(pl.multiple_of(BLK * n, BLK), BLK)
    before = jnp.maximum(n - 1, 0)
    prev = pl.ds(res + d * BLK * before, BLK, stride=d) if d > 1 else pl.ds(pl.multiple_of(BLK * before, BLK), BLK)
    return cur, prev, jnp.where(n == 0, 1, 0)


def _attn_fwd(z, bias, name):
    s = z.shape[0]
    nblk = s // BLK

    def body(q_ref, k_ref, v_ref, b_ref, y_ref, l_ref):
        lane = lax.broadcasted_iota(jnp.int32, (BLK, BLK), 1)
        head0 = lane < D_HEAD

        for branch, d in enumerate(DILATIONS):
            def step(blk, carry, branch=branch, d=d):
                cur, prev, variant = _block_rows(blk, d, nblk)
                q = (q_ref[cur, :] * (D_HEAD ** -0.5)).astype(bf16)
                if d == nblk:
                    kk, vv = k_ref[cur, :].astype(bf16), v_ref[cur, :].astype(bf16)
                    bias_blk = b_ref[branch, 1, 0, :, BLK:]
                else:
                    kk = jnp.concatenate([k_ref[prev, :], k_ref[cur, :]], axis=0).astype(bf16)
                    vv = jnp.concatenate([v_ref[prev, :], v_ref[cur, :]], axis=0).astype(bf16)
                    bias_blk = b_ref[branch, variant, 0]
                sc = lax.dot_general(_stack_heads(q, head0), kk, (((1,), (1,)), ((), ())), preferred_element_type=f32)
                sc = sc + bias_blk
                m = jnp.max(sc, axis=-1, keepdims=True)
                p = jnp.exp(sc - m)
                den = jnp.sum(p, axis=-1, keepdims=True)
                out = jnp.dot(p.astype(bf16), vv, preferred_element_type=f32) / den
                lse = m + jnp.log(den)
                out = jnp.where(head0, out[:BLK], out[BLK:])
                lse = jnp.where(head0, lse[:BLK], lse[BLK:]) + jnp.zeros((BLK, BLK), f32)
                if branch == 0:
                    y_ref[cur, :] = out
                    l_ref[cur, :] = lse
                else:
                    y_old, l_old = y_ref[cur, :], l_ref[cur, :]
                    top = jnp.maximum(l_old, lse)
                    e_old, e_new = jnp.exp(l_old - top), jnp.exp(lse - top)
                    tot = e_old + e_new
                    y_ref[cur, :] = (e_old * y_old + e_new * out) / tot
                    l_ref[cur, :] = top + jnp.log(tot)
                return carry

            lax.fori_loop(0, nblk, step, 0, unroll=16)

    col = lambda off: pl.BlockSpec((s, BLK), lambda hp: (0, off + hp))
    out = SDS((s, 4 * BLK), f32)
    nbr = len(DILATIONS)
    return _pcall(body, name=name, out_shape=(out, out), grid=(4,),
                  in_specs=[col(Z_Q128), col(Z_K128), col(Z_V128), pl.BlockSpec((nbr, 2, 1, 2 * BLK, 2 * BLK), lambda hp: (0, 0, hp, 0, 0))],
                  out_specs=(col(0), col(0)), compiler_params=_cp(("parallel",)))(z, z, z, bias)


def _attn_bwd(z, dyb, yb, lt, bias, dz, name):
    s = z.shape[0]
    nblk = s // BLK
    scale = D_HEAD ** -0.5
    chunk = 256

    def body(q_ref, k_ref, v_ref, dy_ref, y_ref, lt_ref, b_ref, dz_in, dz_ref, ds_ref, aq, ak, av, stage, sems):
        lane = lax.broadcasted_iota(jnp.int32, (BLK, BLK), 1)
        head0 = lane < D_HEAD
        ak[...] = jnp.zeros_like(ak)
        av[...] = jnp.zeros_like(av)
        ds_ref[...] = jnp.zeros_like(ds_ref)

        for branch, d in enumerate(DILATIONS):
            def step(blk, carry, branch=branch, d=d):
                cur, prev, variant = _block_rows(blk, d, nblk)
                own_keys = d == nblk
                q = (q_ref[cur, :] * scale).astype(bf16)
                if own_keys:
                    kk, vv = k_ref[cur, :].astype(bf16), v_ref[cur, :].astype(bf16)
                    bias_blk = b_ref[branch, 1, 0, :, BLK:]
                else:
                    kk = jnp.concatenate([k_ref[prev, :], k_ref[cur, :]], axis=0).astype(bf16)
                    vv = jnp.concatenate([v_ref[prev, :], v_ref[cur, :]], axis=0).astype(bf16)
                    bias_blk = b_ref[branch, variant, 0]
                dy = dy_ref[cur, :]
                lt_b = lt_ref[cur, :]
                qs = _stack_heads(q, head0)
                dys = _stack_heads(dy.astype(bf16), head0)
                dd = jnp.sum(_stack_heads(dy * y_ref[cur, :], head0), axis=-1, keepdims=True)
                lts = jnp.concatenate([lt_b[:, 0:1], lt_b[:, D_HEAD:D_HEAD + 1]], axis=0)
                sc = lax.dot_general(qs, kk, (((1,), (1,)), ((), ())), preferred_element_type=f32)
                p = jnp.exp(sc + bias_blk - lts)
                dp = lax.dot_general(dys, vv, (((1,), (1,)), ((), ())), preferred_element_type=f32)
                ds = p * (dp - dd)
                if own_keys:
                    ds_ref[branch, 0, :, BLK:] += ds
                else:
                    ds_ref[branch, 0] += ds
                dsb = ds.astype(bf16)
                dq = jnp.dot(dsb, kk, preferred_element_type=f32) * scale
                dkk = lax.dot_general(dsb, qs, (((0,), (0,)), ((), ())), preferred_element_type=f32)
                dvv = lax.dot_general(p.astype(bf16), dys, (((0,), (0,)), ((), ())), preferred_element_type=f32)
                dq = jnp.where(head0, dq[:BLK], dq[BLK:])
                if branch == 0:
                    aq[cur, :] = dq
                else:
                    aq[cur, :] += dq
                if own_keys:
                    ak[cur, :] += dkk
                    av[cur, :] += dvv
                else:
                    ak[prev, :] += dkk[:BLK]
                    ak[cur, :] += dkk[BLK:]
                    av[prev, :] += dvv[:BLK]
                    av[cur, :] += dvv[BLK:]
                return carry

            lax.fori_loop(0, nblk, step, 0, unroll=16)

        def store(c, carry):
            rows = pl.ds(pl.multiple_of(c * chunk, chunk), chunk)
            stage[0, rows, :] = aq[rows, :].astype(bf16)
            stage[1, rows, :] = ak[rows, :].astype(bf16)
            stage[2, rows, :] = av[rows, :].astype(bf16)
            return carry

        lax.fori_loop(0, s // chunk, store, 0)
        hp = pl.program_id(0)
        copies = []
        for t, first_col in enumerate((Z_Q128, Z_K128, Z_V128)):
            cols = pl.ds(pl.multiple_of((first_col + hp) * BLK, BLK), BLK)
            copies.append(pltpu.make_async_copy(stage.at[t], dz_ref.at[:, cols], sems.at[t]))
            copies[-1].start()
        for cp in copies:
            cp.wait()

    col = lambda off: pl.BlockSpec((s, BLK), lambda hp: (0, off + hp))
    nbr = len(DILATIONS)
    tabs = pl.BlockSpec((nbr, 1, 2 * BLK, 2 * BLK), lambda hp: (0, hp, 0, 0))
    bias_spec = pl.BlockSpec((nbr, 2, 1, 2 * BLK, 2 * BLK), lambda hp: (0, 0, hp, 0, 0))
    anywhere = pl.BlockSpec(memory_space=pl.ANY)
    scratch = [pltpu.VMEM((s, BLK), f32)] * 3 + [pltpu.VMEM((3, s, BLK), bf16), pltpu.SemaphoreType.DMA((3,))]
    return _pcall(body, name=name, out_shape=(SDS(dz.shape, dz.dtype), SDS((nbr, 4, 2 * BLK, 2 * BLK), f32)), grid=(4,),
                  in_specs=[col(Z_Q128), col(Z_K128), col(Z_V128), col(0), col(0), col(0), bias_spec, anywhere],
                  out_specs=(anywhere, tabs), input_output_aliases={7: 0}, scratch_shapes=scratch,
                  compiler_params=_cp(("arbitrary",)))(z, z, z, dyb, yb, lt, bias, dz)


def _adamw_math(w, g, m, v):
    m = ADAM_B1 * m + (1.0 - ADAM_B1) * g
    v = ADAM_B2 * v + (1.0 - ADAM_B2) * (g * g)
    m_hat = m / (1.0 - ADAM_B1 ** ADAM_STEP)
    v_hat = v / (1.0 - ADAM_B2 ** ADAM_STEP)
    delta = -ADAM_LR * (m_hat / (jnp.sqrt(v_hat) + ADAM_EPS) + ADAM_WD * w)
    return delta, m, v


def _adamw_layer(land, send, w, m, v, prev, l, tr, me, name):
    _, r, c = w.shape
    assert r % tr == 0

    def body(me_ref, land_ref, own_ref, w_ref, m_ref, v_ref, p0, p1, p2, p3, g_ref, d_ref, nm_ref, nv_ref):
        mine = jnp.zeros((tr, c), jnp.int32) + me_ref[0]
        own = own_ref[...]
        g = jnp.zeros((tr, c), f32)
        for j in range(N_DEV):
            g = g + jnp.where(mine == j, own, land_ref[j]).astype(f32)
        delta, nm, nv = _adamw_math(w_ref[...], g, m_ref[...], v_ref[...])
        g_ref[...] = g
        d_ref[...] = delta
        nm_ref[...] = nm
        nv_ref[...] = nv

    parts = pl.BlockSpec((N_DEV, tr, c), lambda i, me_ref: (0, i, 0))
    own = pl.BlockSpec((None, tr, c), lambda i, me_ref: (me_ref[0], i, 0))
    row = pl.BlockSpec((None, tr, c), lambda i, me_ref: (l, i, 0))
    anywhere = pl.BlockSpec(memory_space=pl.ANY)
    out = SDS(w.shape, f32)
    grid_spec = pltpu.PrefetchScalarGridSpec(num_scalar_prefetch=1, grid=(r // tr,), in_specs=[parts, own, row, row, row] + [anywhere] * 4,
                                             out_specs=(row, row, row, row))
    return _pcall(body, name=name, out_shape=(out, out, out, out), grid_spec=grid_spec,
                  input_output_aliases={6: 0, 7: 1, 8: 2, 9: 3}, compiler_params=_cp(("parallel",)))(me, land, send, w, m, v, *prev)


def _sum_parts(parts, after, name):
    _, r, c = parts.shape

    def body(p_ref, after_ref, o_ref):
        g = p_ref[0]
        for j in range(1, N_DEV):
            g = g + p_ref[j]
        o_ref[...] = g

    return _pcall(body, name=name, out_shape=SDS((r, c), f32),
                  in_specs=[pl.BlockSpec(memory_space=pltpu.VMEM), pl.BlockSpec(memory_space=pl.ANY)],
                  out_specs=pl.BlockSpec(memory_space=pltpu.VMEM), compiler_params=_cp())(parts, after)


def _adamw_small(gs, ws, ms, vs, name):
    n = len(ws)

    def body(*refs):
        for t in range(n):
            delta, nm, nv = _adamw_math(refs[n + t][...], refs[t][...], refs[2 * n + t][...], refs[3 * n + t][...])
            refs[4 * n + t][...] = delta
            refs[5 * n + t][...] = nm
            refs[6 * n + t][...] = nv

    out = [SDS(wt.shape, f32) for wt in ws] * 3
    res = _pcall(body, name=name, out_shape=out, compiler_params=_cp())(*gs, *ws, *ms, *vs)
    return list(res[:n]), list(res[n:2 * n]), list(res[2 * n:])


def _my_place():
    return lax.axis_index("x"), lax.axis_index("y"), lax.axis_index("c")


def _all_gather(xs, name, after=None):
    n = len(xs)

    def body(*refs):
        x_refs, out_refs = refs[:n], refs[-3 - n:-3]
        send_sems, recv_sems, local_sems = refs[-3:]
        mx, my, mc = _my_place()
        me, sibling = (mx, my, mc), (mx, my, 1 - mc)
        chips = [(1 - mx, my), (mx, 1 - my), (1 - mx, 1 - my)]

        def rows(t, px, py, pc):
            return out_refs[t].at[4 * px + 2 * py + pc]

        def copy(t, k, block, to, src=None):
            return pltpu.make_async_remote_copy(src_ref=rows(t, *block) if src is None else src, dst_ref=rows(t, *block),
                                                send_sem=send_sems.at[7 * t + k], recv_sem=recv_sems.at[7 * t + k],
                                                device_id=to, device_id_type=MESH)

        mine = [pltpu.make_async_copy(x_refs[t], rows(t, *me), local_sems.at[t]) for t in range(n)]
        first, passed = [], []
        for t in range(n):
            mine[t].start()
            first += [copy(t, 0, me, sibling, src=x_refs[t])] + [copy(t, 1 + j, me, (*chip, mc), src=x_refs[t]) for j, chip in enumerate(chips)]
        for cp in first:
            cp.start()
        for t in range(n):
            for j, chip in enumerate(chips):
                copy(t, 1 + j, (*chip, mc), me).wait_recv()
                passed.append(copy(t, 4 + j, (*chip, mc), sibling))
                passed[-1].start()
        for t in range(n):
            copy(t, 0, sibling, me).wait_recv()
            for j, chip in enumerate(chips):
                copy(t, 4 + j, (*chip, 1 - mc), me).wait_recv()
        for cp in first + passed:
            cp.wait_send()
        for t in range(n):
            mine[t].wait()

    ins = list(xs) + ([after] if after is not None else [])
    anywhere = pl.BlockSpec(memory_space=pl.ANY)
    return list(_pcall(body, name=name, out_shape=[SDS((N_DEV,) + x.shape, x.dtype) for x in xs],
                       in_specs=[anywhere] * len(ins), out_specs=[anywhere] * n,
                       scratch_shapes=[pltpu.SemaphoreType.DMA((7 * n,)), pltpu.SemaphoreType.DMA((7 * n,)),
                                       pltpu.SemaphoreType.DMA((n,))])(*ins))


HBM_SPEC = pl.BlockSpec(memory_space=pltpu.HBM)
SEM_SPEC = pl.BlockSpec(memory_space=pltpu.SEMAPHORE)
ANY_SPEC = pl.BlockSpec(memory_space=pl.ANY)
DATAFLOW = pltpu.SideEffectType.DATAFLOW_SIDE_EFFECTING


def _peers():
    mx, my, mc = _my_place()
    out = []
    for k in range(1, N_DEV):
        tx, ty, tc = mx ^ ((k >> 2) & 1), my ^ ((k >> 1) & 1), mc ^ (k & 1)
        out.append(((tx, ty, tc), 4 * tx + 2 * ty + tc))
    return out


def _in_hbm(arrays):
    return [pltpu.with_memory_space_constraint(a, pltpu.HBM) for a in arrays]


N_FIRST_LEG, N_SECOND_LEG = 4, 3


def _chip_peers():
    mx, my, mc = _my_place()
    return [((tx, ty, mc), 4 * tx + 2 * ty + mc) for tx, ty in ((1 - mx, my), (mx, 1 - my), (1 - mx, 1 - my))]


def _gather_start(bufs, after, name):
    n = len(bufs)

    def body(*refs):
        ins, s_sem, r_sem, token = refs[:n], refs[n + 1], refs[n + 2], refs[-1]
        mx, my, mc = _my_place()
        me = 4 * mx + 2 * my + mc
        for t in range(n):
            for dev in [(mx, my, 1 - mc)] + [d for d, _ in _chip_peers()]:
                pltpu.make_async_remote_copy(src_ref=ins[t].at[me], dst_ref=ins[t].at[me], send_sem=s_sem.at[t],
                                             recv_sem=r_sem.at[t], device_id=dev, device_id_type=MESH).start()
        token[...] = jnp.zeros_like(token)

    out_shape = ((pltpu.SemaphoreType.DMA((n,)), pltpu.SemaphoreType.DMA((n,))) + tuple(pltpu.HBM(b.shape, b.dtype) for b in bufs)
                 + (SDS((8, 128), f32),))
    out = _pcall(body, name=name, out_shape=out_shape, in_specs=(HBM_SPEC,) * n + (ANY_SPEC,),
                 out_specs=(SEM_SPEC, SEM_SPEC) + (HBM_SPEC,) * n + (pl.BlockSpec(memory_space=pltpu.VMEM),),
                 input_output_aliases={i: 2 + i for i in range(n)},
                 compiler_params=pltpu.CompilerParams(has_side_effects=DATAFLOW))(*_in_hbm(bufs), after)
    return (out[0], out[1]), list(out[2:2 + n]), out[-1]


def _gather_forward(bufs, sems, after, name, fresh=()):
    n, nf = len(bufs), len(fresh)
    both = list(bufs) + list(fresh)

    nsem = 4 if nf else 2

    def body(*refs):
        ins, news, s1, r1 = refs[:n], refs[n:n + nf], refs[n + nf], refs[n + nf + 1]
        s2, r2 = refs[n + nf + 3:n + nf + 5]
        sf, rf = refs[n + nf + 5:n + nf + 7] if nf else (None, None)
        token = refs[-1]
        mx, my, mc = _my_place()
        me, sibling = 4 * mx + 2 * my + mc, (mx, my, 1 - mc)
        for t in range(n):
            legs = ins[t].at[pl.ds(0, N_FIRST_LEG)]
            first = pltpu.make_async_remote_copy(src_ref=legs, dst_ref=legs, send_sem=s1.at[t], recv_sem=r1.at[t],
                                                 device_id=sibling, device_id_type=MESH)
            first.wait_send()
            first.wait_recv()
        for t in range(n):
            for _, idx in _chip_peers():
                pltpu.make_async_remote_copy(src_ref=ins[t].at[idx], dst_ref=ins[t].at[idx], send_sem=s2.at[t],
                                             recv_sem=r2.at[t], device_id=sibling, device_id_type=MESH).start()
        for t in range(nf):
            for dev in [sibling] + [d for d, _ in _chip_peers()]:
                pltpu.make_async_remote_copy(src_ref=news[t].at[me], dst_ref=news[t].at[me], send_sem=sf.at[t],
                                             recv_sem=rf.at[t], device_id=dev, device_id_type=MESH).start()
        token[...] = jnp.zeros_like(token)

    sem = lambda k: pltpu.SemaphoreType.DMA((k,))
    sem_shapes = (sem(n), sem(n)) + ((sem(nf), sem(nf)) if nf else ())
    out_shape = sem_shapes + tuple(pltpu.HBM(b.shape, b.dtype) for b in both) + (SDS((8, 128), f32),)
    out = _pcall(body, name=name, out_shape=out_shape, in_specs=(HBM_SPEC,) * (n + nf) + (SEM_SPEC, SEM_SPEC, ANY_SPEC),
                 out_specs=(SEM_SPEC,) * nsem + (HBM_SPEC,) * (n + nf) + (pl.BlockSpec(memory_space=pltpu.VMEM),),
                 input_output_aliases={i: nsem + i for i in range(n + nf)},
                 compiler_params=pltpu.CompilerParams(has_side_effects=DATAFLOW))(*bufs, *_in_hbm(fresh), sems[0], sems[1], after)
    fresh_sems = (out[2], out[3]) if nf else None
    return (out[0], out[1]), list(out[nsem:nsem + n]), fresh_sems, list(out[nsem + n:nsem + n + nf]), out[-1]


def _exchange_start(sends, lands, name, after=None):
    nt = len(sends)
    extra = [] if after is None else [after]

    def body(*refs):
        srcs, dsts, token = refs[:nt], refs[nt:2 * nt], refs[-1]
        s_sem, r_sem = refs[2 * nt + len(extra)], refs[2 * nt + len(extra) + 1]
        mx, my, mc = _my_place()
        me = 4 * mx + 2 * my + mc
        for t in range(nt):
            for dev, idx in _peers():
                pltpu.make_async_remote_copy(src_ref=srcs[t].at[idx], dst_ref=dsts[t].at[me], send_sem=s_sem.at[t],
                                             recv_sem=r_sem.at[t], device_id=dev, device_id_type=MESH).start()
        token[...] = jnp.zeros_like(token)

    both = list(sends) + list(lands)
    out_shape = (pltpu.SemaphoreType.DMA((nt,)), pltpu.SemaphoreType.DMA((nt,))) + tuple(pltpu.HBM(b.shape, b.dtype) for b in both)
    out = _pcall(body, name=name, out_shape=out_shape + (SDS((8, 128), f32),), in_specs=(HBM_SPEC,) * (2 * nt) + (ANY_SPEC,) * len(extra),
                 out_specs=(SEM_SPEC, SEM_SPEC) + (HBM_SPEC,) * (2 * nt) + (pl.BlockSpec(memory_space=pltpu.VMEM),),
                 input_output_aliases={i: 2 + i for i in range(2 * nt)},
                 compiler_params=pltpu.CompilerParams(has_side_effects=DATAFLOW))(*_in_hbm(both), *extra)
    return (out[0], out[1]), list(out[2:2 + nt]), list(out[2 + nt:2 + 2 * nt]), out[-1]


def _copies_wait(srcs, dsts, sems, after, name, ncopies=N_DEV - 1):
    nt = len(srcs)
    same = srcs is dsts
    bufs = list(srcs) if same else list(srcs) + list(dsts)
    nb = len(bufs)

    def body(*refs):
        s_sem, r_sem = refs[nb], refs[nb + 1]
        mx, my, mc = _my_place()
        for t in range(nt):
            src = refs[t].at[pl.ds(0, ncopies)]
            dst = src if same else refs[nt + t].at[pl.ds(0, ncopies)]
            cp = pltpu.make_async_remote_copy(src_ref=src, dst_ref=dst, send_sem=s_sem.at[t], recv_sem=r_sem.at[t],
                                              device_id=(mx, my, 1 - mc), device_id_type=MESH)
            cp.wait_send()
            cp.wait_recv()

    out = _pcall(body, name=name, out_shape=tuple(pltpu.HBM(b.shape, b.dtype) for b in bufs),
                 in_specs=(HBM_SPEC,) * nb + (SEM_SPEC, SEM_SPEC, ANY_SPEC), out_specs=(HBM_SPEC,) * nb,
                 input_output_aliases={i: i for i in range(nb)},
                 compiler_params=pltpu.CompilerParams(has_side_effects=DATAFLOW))(*bufs, sems[0], sems[1], after)
    out = list(out)
    return (out, out) if same else (out[:nt], out[nt:])


def _cast_blocks(weights, l, me, after, name):
    nt = len(weights)
    halves = 2
    extra = [] if after is None else [after]

    def body(me_ref, *refs):
        outs = refs[nt + len(extra):]
        for t in range(nt):
            outs[t][...] = refs[t][...].astype(bf16)

    in_specs, out_specs, out_shape = [], [], []
    for wt in weights:
        _, r, c = wt.shape
        tr = r // halves
        in_specs.append(pl.BlockSpec((None, tr, c), lambda i, me_ref: (l, i, 0)))
        out_specs.append(pl.BlockSpec((None, tr, c), lambda i, me_ref: (me_ref[0], i, 0)))
        out_shape.append(SDS((N_DEV, r, c), bf16))
    in_specs += [pl.BlockSpec(memory_space=pl.ANY)] * len(extra)
    grid_spec = pltpu.PrefetchScalarGridSpec(num_scalar_prefetch=1, grid=(halves,), in_specs=in_specs, out_specs=out_specs)
    return list(_pcall(body, name=name, out_shape=out_shape, grid_spec=grid_spec, compiler_params=_cp(("parallel",)))(me, *weights, *extra))

def _layer_fwd_mix(x, p, bias, l, after=None):
    tag = f"l{l}"
    z = _mm(x, p["w_in"], "nt", 1024, 1408, f"{tag}_mm_in", after=after, norm_a=p["norm_mix_g"])
    yb, lt = _attn_fwd(z, bias, f"{tag}_attn_fwd")
    y, u = _mixer_fwd(z, yb, p["conv_a_w"], p["conv_c_w"], p["conv_c_b"], p["ln_c_g"], p["ln_c_b"], p["out_norm_g"], f"{tag}_mixer_fwd")
    return y, dict(x=x, z=z, yb=yb, lt=lt, y=y, u=u)


def _layer_fwd_ffn(x, y, p, l, after=None):
    tag = f"l{l}"
    x1 = _mm(y, p["w_out"], "nn", 512, 1024, f"{tag}_mm_out", res=x, after=after)
    upre = _mm(x1, p["w_up"], "nt", 1024, 1408, f"{tag}_mm_up", norm_a=p["norm_ffn_g"])
    act = _ffn_act_fwd(upre, p["conv_f_w"], f"{tag}_ffn_act")
    x2 = _mm(act, p["w_down"], "nn", 512, 1024, f"{tag}_mm_down", res=x1)
    return x2, dict(x1=x1, upre=upre, act=act)


def _small_stacks(depth, d_model, d_ff):
    shapes = dict(norm_ffn=(depth, 1, d_model), norm_mix=(depth, 1, d_model), conv_f_gate=(depth, 8, d_ff),
                  conv_f_val=(depth, 8, d_ff), mixers=(depth, PG_ROWS, 256), og_b=(depth, 1, 512))
    return {k: lax.empty(shp, f32) for k, shp in shapes.items()}


def _small_grads(st):
    pg = st["mixers"]
    row = lambda r: pg[:, r:r + 1, :]
    return dict(norm_ffn_g=st["norm_ffn"], norm_mix_g=st["norm_mix"],
                conv_f_w=jnp.concatenate([st["conv_f_gate"][:, :K_FFN], st["conv_f_val"][:, :K_FFN]], axis=2),
                conv_a_w=pg[:, PG_CONV_A:PG_CONV_A + K_SHORT], conv_c_w=pg[:, PG_CONV_C:PG_CONV_C + K_CONF],
                conv_c_b=row(PG_CB), ln_c_g=row(PG_LNG), ln_c_b=row(PG_LNB),
                out_norm_g=jnp.concatenate([row(PG_OG_A), st["og_b"], row(PG_OG_C)], axis=2))


def _layer_bwd_ffn(dx2, p, sv, l, st, after=None):
    tag = f"l{l}"
    g, st = {}, dict(st)
    dact = _mm(dx2, p["w_down"], "nt", 512, 1408, f"{tag}_mm_dact", out_dtype=bf16, after=after, b_outer=True)
    dwd = _mm(sv["act"], dx2, "tn", 1408, 1024, f"{tag}_mm_dwdown", out_dtype=bf16)
    g["w_down"] = dwd.reshape(N_DEV, dwd.shape[0] // N_DEV, dwd.shape[1])
    dug, duv, st["conv_f_gate"], st["conv_f_val"] = _ffn_act_bwd(sv["upre"], dact, p["conv_f_w"], (st["conv_f_gate"], st["conv_f_val"]),
                                                                 l, f"{tag}_ffn_act_bwd")
    dwu = _mm_tn_pieces([dug, duv], sv["x1"], p["norm_ffn_g"], 1408, f"{tag}_mm_dwup")
    g["w_up"] = dwu.reshape(N_DEV, dwu.shape[0] // N_DEV, dwu.shape[1])
    dx1, st["norm_ffn"] = _mm([dug, duv], p["w_up"], "nn", 512, 1024, f"{tag}_mm_dh2",
                              rms_bwd=(sv["x1"], p["norm_ffn_g"], dx2, st["norm_ffn"], l))
    return dx1, g, st


def _layer_bwd_mix(dx1, p, sv, bias, l, st, after=None):
    tag = f"l{l}"
    g, st = {}, dict(st)
    dy = _mm(dx1, p["w_out"], "nt", 512, 1024, f"{tag}_mm_dy", after=after)
    dwo = _mm(sv["y"], dx1, "tn", 512, 1024, f"{tag}_mm_dwout", out_dtype=bf16)
    g["w_out"] = dwo.reshape(N_DEV, dwo.shape[0] // N_DEV, dwo.shape[1])
    dz, dyb, st["mixers"], st["og_b"] = _mixer_bwd(sv["z"], sv["yb"], dy, sv["u"], p["conv_a_w"], p["conv_c_w"], p["conv_c_b"], p["ln_c_g"],
                                                   p["ln_c_b"], p["out_norm_g"], (st["mixers"], st["og_b"]), l, f"{tag}_mixer_bwd")
    dz, tabs = _attn_bwd(sv["z"], dyb, sv["yb"], sv["lt"], bias, dz, f"{tag}_attn_bwd")
    dwi = _mm(dz, sv["x"], "tn", 1408, 1024, f"{tag}_mm_dwin", out_dtype=bf16, norm_b=p["norm_mix_g"])
    g["w_in"] = dwi.reshape(N_DEV, dwi.shape[0] // N_DEV, dwi.shape[1])
    dx, st["norm_mix"] = _mm(dz, p["w_in"], "nn", 512, 1024, f"{tag}_mm_dh", rms_bwd=(sv["x"], p["norm_mix_g"], dx1, st["norm_mix"], l))
    return dx, g, tabs, st


SMALL_LAYER_PARAMS = ("norm_mix_g", "conv_a_w", "conv_c_w", "conv_c_b", "ln_c_g", "ln_c_b", "out_norm_g", "norm_ffn_g", "conv_f_w")


BIG = ("w_in", "w_out", "w_up", "w_down")
TRANSPOSED = ("w_in", "w_up")
ADAM_ROWS = {"w_in": 176, "w_out": 128, "w_up": 176, "w_down": 176}
SMALL_SHARDED = ("conv_a_w", "conv_c_w", "conv_f_w")
SMALL_REPLICATED = ("norm_mix_g", "conv_c_b", "ln_c_g", "ln_c_b", "out_norm_g", "norm_ffn_g", "rel_bias", "final_g")
WEIGHTS = ("norm_mix_g", "w_in", "conv_a_w", "conv_c_w", "conv_c_b", "ln_c_g", "ln_c_b", "out_norm_g", "w_out",
           "norm_ffn_g", "w_up", "conv_f_w", "w_down", "rel_bias", "final_g")


def _pack(arrays):
    flat = jnp.concatenate([a.reshape(-1) for a in arrays])
    rows = -(-flat.shape[0] // 1024) * 8
    return jnp.pad(flat, (0, rows * 128 - flat.shape[0])).reshape(rows, 128)


def _unpack(packed, shapes):
    flat, out, at = packed.reshape(-1), [], 0
    for shp in shapes:
        n = int(np.prod(shp))
        out.append(flat[at:at + n].reshape(shp))
        at += n
    return out


def _gather_shards(g, local_shape):
    nd = len(local_shape)
    return jnp.moveaxis(g, 0, nd - 1).reshape(local_shape[:-1] + (N_DEV * local_shape[-1],))


def kernel(x, norm_mix_g, w_in, conv_a_w, conv_c_w, conv_c_b, ln_c_g, ln_c_b, out_norm_g, w_out, norm_ffn_g, w_up, conv_f_w, w_down, rel_bias, final_g, loss_target, m_norm_mix_g, m_w_in, m_conv_a_w, m_conv_c_w, m_conv_c_b, m_ln_c_g, m_ln_c_b, m_out_norm_g, m_w_out, m_norm_ffn_g, m_w_up, m_conv_f_w, m_w_down, m_rel_bias, m_final_g, v_norm_mix_g, v_w_in, v_conv_a_w, v_conv_c_w, v_conv_c_b, v_ln_c_g, v_ln_c_b, v_out_norm_g, v_w_out, v_norm_ffn_g, v_w_up, v_conv_f_w, v_w_down, v_rel_bias, v_final_g):
    w = dict(norm_mix_g=norm_mix_g, w_in=w_in, conv_a_w=conv_a_w, conv_c_w=conv_c_w, conv_c_b=conv_c_b, ln_c_g=ln_c_g,
             ln_c_b=ln_c_b, out_norm_g=out_norm_g, w_out=w_out, norm_ffn_g=norm_ffn_g, w_up=w_up, conv_f_w=conv_f_w,
             w_down=w_down, rel_bias=rel_bias, final_g=final_g)
    mom = dict(norm_mix_g=m_norm_mix_g, w_in=m_w_in, conv_a_w=m_conv_a_w, conv_c_w=m_conv_c_w, conv_c_b=m_conv_c_b,
               ln_c_g=m_ln_c_g, ln_c_b=m_ln_c_b, out_norm_g=m_out_norm_g, w_out=m_w_out, norm_ffn_g=m_norm_ffn_g,
               w_up=m_w_up, conv_f_w=m_conv_f_w, w_down=m_w_down, rel_bias=m_rel_bias, final_g=m_final_g)
    var = dict(norm_mix_g=v_norm_mix_g, w_in=v_w_in, conv_a_w=v_conv_a_w, conv_c_w=v_conv_c_w, conv_c_b=v_conv_c_b,
               ln_c_g=v_ln_c_g, ln_c_b=v_ln_c_b, out_norm_g=v_out_norm_g, w_out=v_w_out, norm_ffn_g=v_norm_ffn_g,
               w_up=v_w_up, conv_f_w=v_conv_f_w, w_down=v_w_down, rel_bias=v_rel_bias, final_g=v_final_g)
    depth = w_in.shape[0]
    me = 4 * lax.axis_index("x") + 2 * lax.axis_index("y") + lax.axis_index("c")
    for k in TRANSPOSED:
        w[k], mom[k], var[k] = (jnp.transpose(a, (0, 2, 1)) for a in (w[k], mom[k], var[k]))

    me_arr = me.astype(jnp.int32).reshape(1)
    bufs = [_cast_blocks([w[k] for k in BIG], 0, me_arr, None, "l0_cast_blocks")]
    in_sems, in_bufs, token = _gather_start(bufs[0][:1], me_arr, "l0_gather_start_in")
    bufs += [_cast_blocks([w[k] for k in BIG], l, me_arr, token, f"l{l}_cast_blocks") for l in range(1, depth)]
    params = {}
    gathered = _all_gather([w[k] for k in SMALL_SHARDED], "gather_small", after=token)
    gs = gathered[0]
    for k, g8 in zip(SMALL_SHARDED, gathered):
        params[k] = _gather_shards(g8, w[k].shape)
    for k in ("norm_mix_g", "conv_c_b", "ln_c_g", "ln_c_b", "out_norm_g", "norm_ffn_g"):
        params[k] = w[k][:, None, :]
    params["rel_bias"] = rel_bias
    params["final_g"] = final_g[None, :]

    def full(k, buf, l):
        return buf.reshape(N_DEV * buf.shape[1], buf.shape[2])

    bias = _bias_tables(rel_bias, "bias_tables").reshape(len(DILATIONS), 2, 4, 2 * BLK, 2 * BLK)
    in_sems, in_bufs, rest_sems, rest_bufs, token = _gather_forward(in_bufs, in_sems, gs, "l0_gather_forward_in", fresh=bufs[0][1:])
    xl, after = x[0], token
    layer_params, saved = [], []
    for l in range(depth):
        _, in_bufs = _copies_wait(in_bufs, in_bufs, in_sems, after, f"l{l}_gather_wait_in", N_SECOND_LEG)
        p = {k: params[k][l] for k in SMALL_LAYER_PARAMS}
        p["w_in"] = full("w_in", in_bufs[0], l)
        token = in_bufs[0]
        nxt_in = bufs[l + 1][:1] if l + 1 < depth else []
        if l > 0:
            rest_sems, rest_bufs, in_sems, in_bufs, token = _gather_forward(rest_bufs, rest_sems, token, f"l{l}_gather_forward_rest", fresh=nxt_in)
        else:
            in_sems, in_bufs, token = _gather_start(nxt_in, token, "l1_gather_start_in")
        y, sv = _layer_fwd_mix(xl, p, bias, l, after=token)
        if l == 0:
            rest_sems, rest_bufs, _, _, _ = _gather_forward(rest_bufs, rest_sems, y, "l0_gather_forward_rest")
        _, rest_bufs = _copies_wait(rest_bufs, rest_bufs, rest_sems, y, f"l{l}_gather_wait_rest", N_SECOND_LEG)
        for k, buf in zip(BIG[1:], rest_bufs):
            p[k] = full(k, buf, l)
        token = rest_bufs[0]
        if l + 1 < depth:
            in_sems, in_bufs, rest_sems, rest_bufs, token = _gather_forward(in_bufs, in_sems, token, f"l{l + 1}_gather_forward_in",
                                                                             fresh=bufs[l + 1][1:])
        xl, sv2 = _layer_fwd_ffn(xl, y, p, l, after=token)
        layer_params.append(p)
        saved.append({**sv, **sv2})
        after = xl
    loss_tile, dx, d_final = _final_loss(xl, params["final_g"], loss_target[0], "final_loss")

    tabs, pending = [None] * depth, []
    after = None
    stacks = _small_stacks(depth, x.shape[-1], w_down.shape[1] * N_DEV)

    def exchange(l, part, names, g, after=None):
        sends = [g[k] for k in names]
        sems, sends, lands, token = _exchange_start(sends, [lax.empty(s.shape, s.dtype) for s in sends],
                                                    f"l{l}_exchange_start_{part}", after=after)
        pending.append((l, part, names, sems, sends, lands))
        return token

    for l in reversed(range(depth)):
        dx1, g_ffn, stacks = _layer_bwd_ffn(dx, layer_params[l], saved[l], l, stacks, after=after)
        after = exchange(l, "ffn", ("w_up", "w_down"), g_ffn)
        dx, g_mix, tabs[l], stacks = _layer_bwd_mix(dx1, layer_params[l], saved[l], bias, l, stacks, after=after)
        if l > 0:
            after = exchange(l, "mix", ("w_out", "w_in"), g_mix)
    grads_small = _small_grads(stacks)
    grads_small["rel_bias"] = _bias_grad(jnp.stack(tabs).reshape(depth, len(DILATIONS), 8, BLK, 2 * BLK), "bias_grad")[:, :8]
    grads_small["final_g"] = d_final

    out_g, out_d, out_m, out_v = {}, {}, {}, {}
    grads = grads_small
    small = SMALL_SHARDED + SMALL_REPLICATED
    full_shapes = [params[k].shape if k in SMALL_SHARDED else w[k].shape for k in small]
    packed = _pack([grads[k].reshape(shp) for k, shp in zip(small, full_shapes)] + [loss_tile[0, :1]])
    gathered_small = _all_gather([packed], "gather_small_grads", after=after)[0]
    token = exchange(0, "mix", ("w_out", "w_in"), g_mix, after=gathered_small)
    gsum = _sum_parts(gathered_small, token, "sum_small_grads")
    *summed, loss = _unpack(gsum, full_shapes + [(1,)])
    loss = loss[0]
    gfull = dict(zip(small, summed))
    for k in SMALL_SHARDED:
        shp = w[k].shape
        gfull[k] = lax.dynamic_index_in_dim(gfull[k].reshape(shp[:-1] + (N_DEV, shp[-1])), me, axis=len(shp) - 1, keepdims=False)
    two_d = lambda a: a.reshape(1, -1) if a.ndim == 1 else a
    res = _adamw_small(*[[two_d(src[k]) for k in small] for src in (gfull, w, mom, var)], "adamw_small")
    for name_map, outs in zip((out_d, out_m, out_v), res):
        name_map.update({k: o.reshape(w[k].shape) for k, o in zip(small, outs)})
    out_g.update(gfull)

    results = {k: [lax.empty(w[k].shape, f32) for _ in range(4)] for k in BIG}
    after = res[0][0]
    for l, part, names, sems, sends, lands in pending:
        sends, lands = _copies_wait(sends, lands, sems, after, f"l{l}_exchange_wait_{part}")
        for k, send, land in zip(names, sends, lands):
            results[k] = _adamw_layer(land, send, w[k], mom[k], var[k], results[k], l, ADAM_ROWS[k], me_arr, f"l{l}_adamw_{k}")
            after = results[k][0]
    for k in BIG:
        big = [jnp.transpose(a, (0, 2, 1)) for a in results[k]] if k in TRANSPOSED else results[k]
        out_g[k], out_d[k], out_m[k], out_v[k] = big

    return (loss, dx[None], *[out_g[k] for k in WEIGHTS], *[out_d[k] for k in WEIGHTS], *[out_m[k] for k in WEIGHTS],
            *[out_v[k] for k in WEIGHTS])
```
